```python
import math
import jax, jax.numpy as jnp
from jax import lax
import numpy as np

D_MODEL = 1024
BATCH = 8
SEQ = 8192
DEPTH = 1

N_POOL_GROUPS = 4
POOL_WINDOWS = (2, 4, 8, 16)
D_POOL = D_MODEL // 2
POOL_GROUP_DIM = D_POOL // N_POOL_GROUPS
HEAD_DIM = 64
D_ATTN = D_MODEL - D_POOL
N_HEADS = D_ATTN // HEAD_DIM
D_MIX = D_POOL + D_ATTN
D_IN = D_POOL + 3 * D_ATTN
D_FF = ((8 * D_MODEL // 3 + 255) // 256) * 256
CONV_WIDTH = 3
Q_BLOCK = 128
N_MOD = 6
EPS = 1e-6

kernel_name = "hybrid_pool_stickbreaking_convffn_adaln"


def rmsnorm(x, g):
    xf = x.astype(jnp.float32)
    y = xf * lax.rsqrt(jnp.mean(xf * xf, axis=-1, keepdims=True) + EPS)
    return (y * g.astype(jnp.float32)).astype(x.dtype)


def modulate(h, shift, scale):
    return h * (1.0 + scale[:, None, :]) + shift[:, None, :]


def pool_mixer(u, pool_w, pool_b, pool_scale):
    b, s, _ = u.shape
    uf = u.astype(jnp.float32).reshape(b, s, N_POOL_GROUPS, POOL_GROUP_DIM)
    cs = jnp.concatenate([jnp.zeros_like(uf[:, :1]), jnp.cumsum(uf, axis=1)], axis=1)
    t1 = jnp.arange(1, s + 1)
    outs = []
    for g, w in enumerate(POOL_WINDOWS):
        cg = cs[:, :, g]
        lo = jnp.maximum(t1 - w, 0)
        win_sum = cg[:, 1:] - cg[:, lo]
        count = jnp.minimum(t1, w).astype(jnp.float32)
        outs.append(win_sum / count[None, :, None] - uf[:, :, g])
    pooled = jnp.stack(outs, axis=2)
    y = jnp.einsum('bsgc,gcd->bsgd', pooled, pool_w.astype(jnp.float32)) + pool_b.astype(jnp.float32)
    y = y.reshape(b, s, D_POOL) * pool_scale.astype(jnp.float32)
    return y.astype(u.dtype)


def stick_breaking_attention(q, k, v):
    b, h, s, d = q.shape
    nb = s // Q_BLOCK
    qb = q.reshape(b, h, nb, Q_BLOCK, d).transpose(2, 0, 1, 3, 4)
    k_pos = jnp.arange(s)
    inv_sqrt_d = 1.0 / math.sqrt(d)

    def one_block(args):
        q_blk, blk = args
        q_pos = blk * Q_BLOCK + jnp.arange(Q_BLOCK)
        causal = k_pos[None, :] < q_pos[:, None]
        z = jnp.einsum('bhqd,bhkd->bhqk', q_blk, k) * inv_sqrt_d
        log_beta = jax.nn.log_sigmoid(z)
        log_1m = jnp.where(causal, jax.nn.log_sigmoid(-z), 0.0)
        later = lax.cumsum(log_1m, axis=3, reverse=True) - log_1m
        a = jnp.exp(jnp.where(causal, log_beta + later, -jnp.inf))
        return jnp.einsum('bhqk,bhkd->bhqd', a, v)

    o = lax.map(one_block, (qb, jnp.arange(nb)))
    return o.transpose(1, 2, 0, 3, 4).reshape(b, h, s, d)


def causal_depthwise_conv(u, w, bias):
    ch = u.shape[-1]
    y = lax.conv_general_dilated(
        u, w[:, None, :].astype(u.dtype), window_strides=(1,),
        padding=[(CONV_WIDTH - 1, 0)], dimension_numbers=('NWC', 'WIO', 'NWC'),
        feature_group_count=ch)
    return y + bias.astype(u.dtype)


def _fwd_setup_inputs(seed: int = 0) -> dict:
    key = jax.random.key(seed)
    ks = jax.random.split(key, 20)
    f32 = jnp.float32
    n = lambda k, shape, s: jax.random.normal(k, shape, f32) * s
    return {
        "x": n(ks[0], (BATCH, SEQ, D_MODEL), 1.0),
        "c": n(ks[1], (BATCH, D_MODEL), 1.0),
        "ada_w": n(ks[2], (DEPTH, D_MODEL, N_MOD * D_MODEL), 0.5 * D_MODEL ** -0.5),
        "ada_b": n(ks[3], (DEPTH, N_MOD * D_MODEL), 0.02),
        "norm1_g": 1.0 + n(ks[4], (DEPTH, D_MODEL), 0.05),
        "w_in": n(ks[5], (DEPTH, D_MODEL, D_IN), D_MODEL ** -0.5),
        "pool_w": n(ks[6], (DEPTH, N_POOL_GROUPS, POOL_GROUP_DIM, POOL_GROUP_DIM), POOL_GROUP_DIM ** -0.5),
        "pool_b": n(ks[7], (DEPTH, N_POOL_GROUPS, POOL_GROUP_DIM), 0.02),
        "pool_scale": 1.0 + n(ks[8], (DEPTH, D_POOL), 0.1),
        "q_norm_g": 1.0 + n(ks[9], (DEPTH, HEAD_DIM), 0.05),
        "k_norm_g": 1.0 + n(ks[10], (DEPTH, HEAD_DIM), 0.05),
        "attn_out_g": 1.0 + n(ks[11], (DEPTH, N_HEADS, HEAD_DIM), 0.05),
        "w_out": n(ks[12], (DEPTH, D_MIX, D_MODEL), D_MIX ** -0.5),
        "norm2_g": 1.0 + n(ks[13], (DEPTH, D_MODEL), 0.05),
        "w_up": n(ks[14], (DEPTH, D_MODEL, 2 * D_FF), D_MODEL ** -0.5),
        "conv_w": n(ks[15], (DEPTH, CONV_WIDTH, 2 * D_FF), CONV_WIDTH ** -0.5),
        "conv_b": n(ks[16], (DEPTH, 2 * D_FF), 0.02),
        "w_down": n(ks[17], (DEPTH, D_FF, D_MODEL), D_FF ** -0.5),
    }


def _fwd_reference(x, c, ada_w, ada_b, norm1_g, w_in, pool_w, pool_b, pool_scale, q_norm_g, k_norm_g,
              attn_out_g, w_out, norm2_g, w_up, conv_w, conv_b, w_down):
    b, s, _ = x.shape
    c_act = jax.nn.silu(c)
    for l in range(DEPTH):
        mod = jnp.einsum('bd,de->be', c_act, ada_w[l]) + ada_b[l]
        shift1, scale1, gate1, shift2, scale2, gate2 = jnp.split(mod, N_MOD, axis=-1)

        h = modulate(rmsnorm(x, norm1_g[l]), shift1, scale1)
        proj = jnp.einsum('bsd,de->bse', h, w_in[l])
        u, q, k, v = jnp.split(proj, [D_POOL, D_POOL + D_ATTN, D_POOL + 2 * D_ATTN], axis=-1)

        y_pool = pool_mixer(u, pool_w[l], pool_b[l], pool_scale[l])

        to_heads = lambda t: t.reshape(b, s, N_HEADS, HEAD_DIM).transpose(0, 2, 1, 3)
        qh = rmsnorm(to_heads(q), q_norm_g[l]).astype(jnp.float32)
        kh = rmsnorm(to_heads(k), k_norm_g[l]).astype(jnp.float32)
        vh = to_heads(v).astype(jnp.float32)
        o = stick_breaking_attention(qh, kh, vh)
        o = o.transpose(0, 2, 1, 3).astype(x.dtype)
        o = rmsnorm(o, attn_out_g[l]).reshape(b, s, D_ATTN)

        mix = jnp.concatenate([y_pool, o], axis=-1)
        x = x + gate1[:, None, :] * jnp.einsum('bse,ed->bsd', mix, w_out[l])

        h2 = modulate(rmsnorm(x, norm2_g[l]), shift2, scale2)
        up = jnp.einsum('bsd,df->bsf', h2, w_up[l])
        up = causal_depthwise_conv(up, conv_w[l], conv_b[l])
        gate, val = jnp.split(up, 2, axis=-1)
        ffn = jnp.einsum('bsf,fd->bsd', jax.nn.silu(gate) * val, w_down[l])
        x = x + gate2[:, None, :] * ffn
    return x


import jax as _jax
import jax.numpy as _jnp

TWIN_FORMAT = 'train_step'
FWD_PARAMS = ['x', 'c', 'ada_w', 'ada_b', 'norm1_g', 'w_in', 'pool_w', 'pool_b', 'pool_scale', 'q_norm_g', 'k_norm_g', 'attn_out_g', 'w_out', 'norm2_g', 'w_up', 'conv_w', 'conv_b', 'w_down']
TWIN_WEIGHTS = ['ada_w', 'ada_b', 'norm1_g', 'w_in', 'pool_w', 'pool_b', 'pool_scale', 'q_norm_g', 'k_norm_g', 'attn_out_g', 'w_out', 'norm2_g', 'w_up', 'conv_w', 'conv_b', 'w_down']
TWIN_DIFF_INPUT = 'x'
TWIN_INPUTS = ['x', 'c', 'ada_w', 'ada_b', 'norm1_g', 'w_in', 'pool_w', 'pool_b', 'pool_scale', 'q_norm_g', 'k_norm_g', 'attn_out_g', 'w_out', 'norm2_g', 'w_up', 'conv_w', 'conv_b', 'w_down', 'loss_target', 'm_ada_w', 'm_ada_b', 'm_norm1_g', 'm_w_in', 'm_pool_w', 'm_pool_b', 'm_pool_scale', 'm_q_norm_g', 'm_k_norm_g', 'm_attn_out_g', 'm_w_out', 'm_norm2_g', 'm_w_up', 'm_conv_w', 'm_conv_b', 'm_w_down', 'v_ada_w', 'v_ada_b', 'v_norm1_g', 'v_w_in', 'v_pool_w', 'v_pool_b', 'v_pool_scale', 'v_q_norm_g', 'v_k_norm_g', 'v_attn_out_g', 'v_w_out', 'v_norm2_g', 'v_w_up', 'v_conv_w', 'v_conv_b', 'v_w_down']
TWIN_OUTPUTS = ['loss', 'grad_x', 'grad_ada_w', 'grad_ada_b', 'grad_norm1_g', 'grad_w_in', 'grad_pool_w', 'grad_pool_b', 'grad_pool_scale', 'grad_q_norm_g', 'grad_k_norm_g', 'grad_attn_out_g', 'grad_w_out', 'grad_norm2_g', 'grad_w_up', 'grad_conv_w', 'grad_conv_b', 'grad_w_down', 'delta_ada_w', 'delta_ada_b', 'delta_norm1_g', 'delta_w_in', 'delta_pool_w', 'delta_pool_b', 'delta_pool_scale', 'delta_q_norm_g', 'delta_k_norm_g', 'delta_attn_out_g', 'delta_w_out', 'delta_norm2_g', 'delta_w_up', 'delta_conv_w', 'delta_conv_b', 'delta_w_down', 'new_m_ada_w', 'new_m_ada_b', 'new_m_norm1_g', 'new_m_w_in', 'new_m_pool_w', 'new_m_pool_b', 'new_m_pool_scale', 'new_m_q_norm_g', 'new_m_k_norm_g', 'new_m_attn_out_g', 'new_m_w_out', 'new_m_norm2_g', 'new_m_w_up', 'new_m_conv_w', 'new_m_conv_b', 'new_m_w_down', 'new_v_ada_w', 'new_v_ada_b', 'new_v_norm1_g', 'new_v_w_in', 'new_v_pool_w', 'new_v_pool_b', 'new_v_pool_scale', 'new_v_q_norm_g', 'new_v_k_norm_g', 'new_v_attn_out_g', 'new_v_w_out', 'new_v_norm2_g', 'new_v_w_up', 'new_v_conv_w', 'new_v_conv_b', 'new_v_w_down']
TWIN_LEAF_KINDS = {'loss': 'loss', 'grad_x': 'grad_x', 'grad_ada_w': 'grad_w', 'grad_ada_b': 'grad_w', 'grad_norm1_g': 'grad_w', 'grad_w_in': 'grad_w', 'grad_pool_w': 'grad_w', 'grad_pool_b': 'grad_w', 'grad_pool_scale': 'grad_w', 'grad_q_norm_g': 'grad_w', 'grad_k_norm_g': 'grad_w', 'grad_attn_out_g': 'grad_w', 'grad_w_out': 'grad_w', 'grad_norm2_g': 'grad_w', 'grad_w_up': 'grad_w', 'grad_conv_w': 'grad_w', 'grad_conv_b': 'grad_w', 'grad_w_down': 'grad_w', 'delta_ada_w': 'delta_w', 'delta_ada_b': 'delta_w', 'delta_norm1_g': 'delta_w', 'delta_w_in': 'delta_w', 'delta_pool_w': 'delta_w', 'delta_pool_b': 'delta_w', 'delta_pool_scale': 'delta_w', 'delta_q_norm_g': 'delta_w', 'delta_k_norm_g': 'delta_w', 'delta_attn_out_g': 'delta_w', 'delta_w_out': 'delta_w', 'delta_norm2_g': 'delta_w', 'delta_w_up': 'delta_w', 'delta_conv_w': 'delta_w', 'delta_conv_b': 'delta_w', 'delta_w_down': 'delta_w', 'new_m_ada_w': 'new_m', 'new_m_ada_b': 'new_m', 'new_m_norm1_g': 'new_m', 'new_m_w_in': 'new_m', 'new_m_pool_w': 'new_m', 'new_m_pool_b': 'new_m', 'new_m_pool_scale': 'new_m', 'new_m_q_norm_g': 'new_m', 'new_m_k_norm_g': 'new_m', 'new_m_attn_out_g': 'new_m', 'new_m_w_out': 'new_m', 'new_m_norm2_g': 'new_m', 'new_m_w_up': 'new_m', 'new_m_conv_w': 'new_m', 'new_m_conv_b': 'new_m', 'new_m_w_down': 'new_m', 'new_v_ada_w': 'new_v', 'new_v_ada_b': 'new_v', 'new_v_norm1_g': 'new_v', 'new_v_w_in': 'new_v', 'new_v_pool_w': 'new_v', 'new_v_pool_b': 'new_v', 'new_v_pool_scale': 'new_v', 'new_v_q_norm_g': 'new_v', 'new_v_k_norm_g': 'new_v', 'new_v_attn_out_g': 'new_v', 'new_v_w_out': 'new_v', 'new_v_norm2_g': 'new_v', 'new_v_w_up': 'new_v', 'new_v_conv_w': 'new_v', 'new_v_conv_b': 'new_v', 'new_v_w_down': 'new_v'}


def _forward(args):
    return _fwd_reference(*[args[k] for k in FWD_PARAMS])


def _output_shape():
    def fwd():
        inp = _fwd_setup_inputs(0)
        return _fwd_reference(*[inp[k] for k in FWD_PARAMS])
    out = _jax.eval_shape(fwd)
    return out.shape, out.dtype

N_MICROBATCH = 1
ADAM_LR = 0.001
ADAM_B1 = 0.9
ADAM_B2 = 0.999
ADAM_EPS = 1e-08
ADAM_WD = 0.01
ADAM_STEP = 10
PER_EXAMPLE_BATCH_AXIS = {'x': 0, 'c': 0, 'loss_target': 0}
SHARED_INPUTS = []
_WEIGHT_DTYPES = {'ada_w': _jnp.float32, 'ada_b': _jnp.float32, 'norm1_g': _jnp.float32, 'w_in': _jnp.float32, 'pool_w': _jnp.float32, 'pool_b': _jnp.float32, 'pool_scale': _jnp.float32, 'q_norm_g': _jnp.float32, 'k_norm_g': _jnp.float32, 'attn_out_g': _jnp.float32, 'w_out': _jnp.float32, 'norm2_g': _jnp.float32, 'w_up': _jnp.float32, 'conv_w': _jnp.float32, 'conv_b': _jnp.float32, 'w_down': _jnp.float32}
MOMENT_SCALE = {'ada_w': 2.108127e+00, 'ada_b': 5.177012e+00, 'norm1_g': 2.652234e+00, 'w_in': 2.237366e-01, 'pool_w': 4.681170e-01, 'pool_b': 1.213800e+00, 'pool_scale': 5.202009e+00, 'q_norm_g': 1.467898e-01, 'k_norm_g': 1.461517e-01, 'attn_out_g': 5.898417e+00, 'w_out': 3.494464e-01, 'norm2_g': 6.906115e+00, 'w_up': 1.776101e-01, 'conv_w': 1.022665e+00, 'conv_b': 8.380560e-01, 'w_down': 1.289578e-01}


def _to_microbatches(a, axis):
    t = _jnp.moveaxis(a, axis, 0)
    t = t.reshape((N_MICROBATCH, t.shape[0] // N_MICROBATCH) + t.shape[1:])
    return _jnp.moveaxis(t, 1, axis + 1)


def setup_inputs(seed: int = 0) -> dict:
    inp = _fwd_setup_inputs(seed)
    key = _jax.random.fold_in(_jax.random.key(seed), 7919)
    shape, _ = _output_shape()
    out = dict(inp)
    out["loss_target"] = _jax.random.normal(_jax.random.fold_in(key, 0), shape, _jnp.float32)
    for i, name in enumerate(TWIN_WEIGHTS):
        w = inp[name].astype(_jnp.float32)
        if MOMENT_SCALE is None:
            s = _jnp.sqrt(_jnp.mean(_jnp.square(w)) + 1e-30)
        else:
            s = MOMENT_SCALE[name]
        km, kv = _jax.random.split(_jax.random.fold_in(key, i + 1))
        out[name] = w
        out["m_" + name] = s * _jax.random.normal(km, w.shape, _jnp.float32)
        out["v_" + name] = (s * s) * _jax.random.uniform(kv, w.shape, _jnp.float32, 0.5, 1.5)
    if N_MICROBATCH > 1:
        for name, axis in PER_EXAMPLE_BATCH_AXIS.items():
            out[name] = _to_microbatches(out[name], axis)
    return {'x': out['x'], 'c': out['c'], 'ada_w': out['ada_w'], 'ada_b': out['ada_b'], 'norm1_g': out['norm1_g'], 'w_in': out['w_in'], 'pool_w': out['pool_w'], 'pool_b': out['pool_b'], 'pool_scale': out['pool_scale'], 'q_norm_g': out['q_norm_g'], 'k_norm_g': out['k_norm_g'], 'attn_out_g': out['attn_out_g'], 'w_out': out['w_out'], 'norm2_g': out['norm2_g'], 'w_up': out['w_up'], 'conv_w': out['conv_w'], 'conv_b': out['conv_b'], 'w_down': out['w_down'], 'loss_target': out['loss_target'], 'm_ada_w': out['m_ada_w'], 'm_ada_b': out['m_ada_b'], 'm_norm1_g': out['m_norm1_g'], 'm_w_in': out['m_w_in'], 'm_pool_w': out['m_pool_w'], 'm_pool_b': out['m_pool_b'], 'm_pool_scale': out['m_pool_scale'], 'm_q_norm_g': out['m_q_norm_g'], 'm_k_norm_g': out['m_k_norm_g'], 'm_attn_out_g': out['m_attn_out_g'], 'm_w_out': out['m_w_out'], 'm_norm2_g': out['m_norm2_g'], 'm_w_up': out['m_w_up'], 'm_conv_w': out['m_conv_w'], 'm_conv_b': out['m_conv_b'], 'm_w_down': out['m_w_down'], 'v_ada_w': out['v_ada_w'], 'v_ada_b': out['v_ada_b'], 'v_norm1_g': out['v_norm1_g'], 'v_w_in': out['v_w_in'], 'v_pool_w': out['v_pool_w'], 'v_pool_b': out['v_pool_b'], 'v_pool_scale': out['v_pool_scale'], 'v_q_norm_g': out['v_q_norm_g'], 'v_k_norm_g': out['v_k_norm_g'], 'v_attn_out_g': out['v_attn_out_g'], 'v_w_out': out['v_w_out'], 'v_norm2_g': out['v_norm2_g'], 'v_w_up': out['v_w_up'], 'v_conv_w': out['v_conv_w'], 'v_conv_b': out['v_conv_b'], 'v_w_down': out['v_w_down']}


def _loss(weights, diff, rest, loss_target):
    with _jax.named_scope("forward"):
        args = {**rest, TWIN_DIFF_INPUT: diff, **{k: w.astype(_WEIGHT_DTYPES[k]) for k, w in weights.items()}}
        y = _forward(args)
    with _jax.named_scope("loss_head"):
        err = _jnp.square(y.astype(_jnp.float32) - loss_target)
        return 0.5 * _jnp.sum(_jnp.mean(err, axis=-1)) if err.ndim else 0.5 * err


def _adamw(w, g, m, v):
    m = ADAM_B1 * m + (1.0 - ADAM_B1) * g
    v = ADAM_B2 * v + (1.0 - ADAM_B2) * _jnp.square(g)
    m_hat = m / (1.0 - ADAM_B1 ** ADAM_STEP)
    v_hat = v / (1.0 - ADAM_B2 ** ADAM_STEP)
    delta = -ADAM_LR * (m_hat / (_jnp.sqrt(v_hat) + ADAM_EPS) + ADAM_WD * w)
    return delta, m, v


def reference(x, c, ada_w, ada_b, norm1_g, w_in, pool_w, pool_b, pool_scale, q_norm_g, k_norm_g, attn_out_g, w_out, norm2_g, w_up, conv_w, conv_b, w_down, loss_target, m_ada_w, m_ada_b, m_norm1_g, m_w_in, m_pool_w, m_pool_b, m_pool_scale, m_q_norm_g, m_k_norm_g, m_attn_out_g, m_w_out, m_norm2_g, m_w_up, m_conv_w, m_conv_b, m_w_down, v_ada_w, v_ada_b, v_norm1_g, v_w_in, v_pool_w, v_pool_b, v_pool_scale, v_q_norm_g, v_k_norm_g, v_attn_out_g, v_w_out, v_norm2_g, v_w_up, v_conv_w, v_conv_b, v_w_down):
    given = dict(x=x, c=c, ada_w=ada_w, ada_b=ada_b, norm1_g=norm1_g, w_in=w_in, pool_w=pool_w, pool_b=pool_b, pool_scale=pool_scale, q_norm_g=q_norm_g, k_norm_g=k_norm_g, attn_out_g=attn_out_g, w_out=w_out, norm2_g=norm2_g, w_up=w_up, conv_w=conv_w, conv_b=conv_b, w_down=w_down, loss_target=loss_target, m_ada_w=m_ada_w, m_ada_b=m_ada_b, m_norm1_g=m_norm1_g, m_w_in=m_w_in, m_pool_w=m_pool_w, m_pool_b=m_pool_b, m_pool_scale=m_pool_scale, m_q_norm_g=m_q_norm_g, m_k_norm_g=m_k_norm_g, m_attn_out_g=m_attn_out_g, m_w_out=m_w_out, m_norm2_g=m_norm2_g, m_w_up=m_w_up, m_conv_w=m_conv_w, m_conv_b=m_conv_b, m_w_down=m_w_down, v_ada_w=v_ada_w, v_ada_b=v_ada_b, v_norm1_g=v_norm1_g, v_w_in=v_w_in, v_pool_w=v_pool_w, v_pool_b=v_pool_b, v_pool_scale=v_pool_scale, v_q_norm_g=v_q_norm_g, v_k_norm_g=v_k_norm_g, v_attn_out_g=v_attn_out_g, v_w_out=v_w_out, v_norm2_g=v_norm2_g, v_w_up=v_w_up, v_conv_w=v_conv_w, v_conv_b=v_conv_b, v_w_down=v_w_down)
    weights = {n: given[n] for n in TWIN_WEIGHTS}
    shared = {n: given[n] for n in SHARED_INPUTS}
    per_example = {n: given[n] for n in ['x', 'c']}
    grad_fn = _jax.value_and_grad(_loss, argnums=(0, 1))

    def one_microbatch(ex, loss_target):
        ex = dict(ex)
        diff = ex.pop(TWIN_DIFF_INPUT)
        return grad_fn(weights, diff, {**shared, **ex}, loss_target)

    if N_MICROBATCH == 1:
        loss, (grad_w, grad_x) = one_microbatch(per_example, given["loss_target"])
    else:
        def body(carry, xs):
            loss_sum, grad_sum = carry
            l_k, (gw_k, gx_k) = one_microbatch(xs[0], xs[1])
            with _jax.named_scope("update"):
                return (loss_sum + l_k, _jax.tree.map(_jnp.add, grad_sum, gw_k)), gx_k

        init = (_jnp.zeros((), _jnp.float32), _jax.tree.map(_jnp.zeros_like, weights))
        (loss, grad_w), grad_x = _jax.lax.scan(body, init, (per_example, given["loss_target"]))
    with _jax.named_scope("update"):
        delta_w, new_m, new_v = {}, {}, {}
        for n in TWIN_WEIGHTS:
            delta_w[n], new_m[n], new_v[n] = _adamw(weights[n], grad_w[n], given["m_" + n], given["v_" + n])
    return (loss, grad_x, *[grad_w[n] for n in TWIN_WEIGHTS], *[delta_w[n] for n in TWIN_WEIGHTS],
            *[new_m[n] for n in TWIN_WEIGHTS], *[new_v[n] for n in TWIN_WEIGHTS])
```

```python
import functools
import math

import numpy as np
import jax
import jax.numpy as jnp
from jax import lax
from jax.experimental import pallas as pl
from jax.experimental.pallas import tpu as pltpu

F32, BF16 = jnp.float32, jnp.bfloat16
D = 1024
DP = 512
DA = 512
HD = 64
DIN = DP + 3 * DA
DFF = 2816
POOL_WINDOWS = (2, 4, 8, 16)
HALO = 16
EPS = 1e-6
LANES = 128
SUBLANES = 8
NDEV = 8
VMEM_LIMIT = 56 * 1024 * 1024
MESH = pl.DeviceIdType.MESH

ADAM_LR, ADAM_B1, ADAM_B2, ADAM_EPS, ADAM_WD, ADAM_STEP = 0.001, 0.9, 0.999, 1e-08, 0.01, 10

NN = (((1,), (0,)), ((), ()))
NT = (((1,), (1,)), ((), ()))
TN = (((0,), (0,)), ((), ()))


def _params(*sem):
    return pltpu.CompilerParams(dimension_semantics=sem, vmem_limit_bytes=VMEM_LIMIT)


def _full(shape):
    nd = len(shape)
    return pl.BlockSpec(shape, lambda *_: (0,) * nd)


def _dot(a, b, dn=NN):
    return lax.dot_general(a, b, dn, preferred_element_type=F32)


def _split_dot(a, b, dn=NN):
    hi = a.astype(BF16)
    lo = (a - hi.astype(F32)).astype(BF16)
    return _dot(hi, b, dn) + _dot(lo, b, dn)


def _colsum8(v):
    r, n = v.shape
    return v.reshape(r // SUBLANES, SUBLANES, n).sum(axis=0)


def _block_diag_ones(n, blk):
    i = np.arange(n) // blk
    return jnp.asarray((i[:, None] == i[None, :]).astype(np.float32), BF16)


def _matmul(a, b, *, mode, out_dtype, tm, tn, tk, name, n_outer=False):
    if mode == "tn":
        K, M = a.shape
        N = b.shape[1]
    elif mode == "nt":
        M, K = a.shape
        N = b.shape[0]
    else:
        M, K = a.shape
        N = b.shape[1]
    tm, tn, tk = min(tm, M), min(tn, N), min(tk, K)
    assert M % tm == 0 and N % tn == 0 and K % tk == 0, (name, M, N, K, tm, tn, tk)
    nk = K // tk
    dn = {"nn": NN, "nt": NT, "tn": TN}[mode]

    def body(a_ref, b_ref, o_ref, *acc):
        if nk == 1:
            o_ref[...] = _dot(a_ref[...], b_ref[...], dn).astype(o_ref.dtype)
            return
        acc_ref, = acc
        k = pl.program_id(2)

        @pl.when(k == 0)
        def _():
            acc_ref[...] = jnp.zeros_like(acc_ref)

        acc_ref[...] += _dot(a_ref[...], b_ref[...], dn)

        @pl.when(k == nk - 1)
        def _():
            o_ref[...] = acc_ref[...].astype(o_ref.dtype)

    if n_outer:
        gi = lambda g: (g[1], g[0], g[2])
        grid = (N // tn, M // tm, nk)
    else:
        gi = lambda g: g
        grid = (M // tm, N // tn, nk)

    def amap(*g):
        i, j, k = gi(g)
        return (k, i) if mode == "tn" else (i, k)

    def bmap(*g):
        i, j, k = gi(g)
        return (j, k) if mode == "nt" else (k, j)

    def omap(*g):
        i, j, k = gi(g)
        return (i, j)

    a_blk = (tk, tm) if mode == "tn" else (tm, tk)
    b_blk = (tn, tk) if mode == "nt" else (tk, tn)
    return pl.pallas_call(
        body, name=name, grid=grid,
        in_specs=[pl.BlockSpec(a_blk, amap), pl.BlockSpec(b_blk, bmap)],
        out_specs=pl.BlockSpec((tm, tn), omap),
        out_shape=jax.ShapeDtypeStruct((M, N), out_dtype),
        scratch_shapes=[] if nk == 1 else [pltpu.VMEM((tm, tn), F32)],
        compiler_params=_params("parallel", "parallel", "arbitrary"),
    )(a, b)


def _slot(swap, px, py, pc):
    return 4 * py + 2 * px + pc if swap else 4 * px + 2 * py + pc


def _all_gather(arrs, swaps, name):
    n = len(arrs)

    def body(*refs):
        ins, outs = refs[:n], refs[n:2 * n]
        send, recv, loc = refs[2 * n:]
        x, y, c = lax.axis_index("x"), lax.axis_index("y"), lax.axis_index("c")
        sib = (x, y, 1 - c)
        chips = [(1 - x, y), (x, 1 - y), (1 - x, 1 - y)]

        def copy(a, k, blk, to, src=None):
            rows = outs[a].at[_slot(swaps[a], *blk)]
            return pltpu.make_async_remote_copy(
                src_ref=rows if src is None else src, dst_ref=rows,
                send_sem=send.at[7 * a + k], recv_sem=recv.at[7 * a + k], device_id=to, device_id_type=MESH)

        started = []
        for a in range(n):
            mine = pltpu.make_async_copy(ins[a], outs[a].at[_slot(swaps[a], x, y, c)], loc.at[a])
            mine.start()
            started.append(mine)
        sends = []
        for a in range(n):
            first = [copy(a, 0, (x, y, c), sib, src=ins[a])]
            first += [copy(a, 1 + j, (x, y, c), (*chip, c), src=ins[a]) for j, chip in enumerate(chips)]
            for cp in first:
                cp.start()
            sends += first
        for a in range(n):
            for j, chip in enumerate(chips):
                copy(a, 1 + j, (*chip, c), (x, y, c)).wait_recv()
                fwd = copy(a, 4 + j, (*chip, c), sib)
                fwd.start()
                sends.append(fwd)
        for a in range(n):
            copy(a, 0, sib, (x, y, c)).wait_recv()
            for j, chip in enumerate(chips):
                copy(a, 4 + j, (*chip, 1 - c), (x, y, c)).wait_recv()
        for cp in sends:
            cp.wait_send()
        for mine in started:
            mine.wait()

    any_spec = pl.BlockSpec(memory_space=pl.ANY)
    return pl.pallas_call(
        body, name=name,
        in_specs=[any_spec] * n, out_specs=[any_spec] * n,
        out_shape=[jax.ShapeDtypeStruct((NDEV,) + a.shape, a.dtype) for a in arrs],
        scratch_shapes=[pltpu.SemaphoreType.DMA((7 * n,)), pltpu.SemaphoreType.DMA((7 * n,)),
                        pltpu.SemaphoreType.DMA((n,))],
    )(*arrs)


def _pair_exchange(arrs, swaps, name):
    n = len(arrs)

    def body(*refs):
        ins = refs[:n]
        owns, gots = refs[n:2 * n], refs[2 * n:3 * n]
        send, recv, loc = refs[3 * n:]
        x, y, c = lax.axis_index("x"), lax.axis_index("y"), lax.axis_index("c")
        sib = (x, y, 1 - c)
        locs, rems = [], []
        for a in range(n):
            for k in range(4):
                kx, ky = k // 2, k % 2
                lc = pltpu.make_async_copy(ins[a].at[_slot(swaps[a], kx, ky, c)], owns[a].at[k], loc.at[4 * a + k])
                lc.start()
                locs.append(lc)
                rc = pltpu.make_async_remote_copy(
                    src_ref=ins[a].at[_slot(swaps[a], kx, ky, 1 - c)], dst_ref=gots[a].at[k],
                    send_sem=send.at[4 * a + k], recv_sem=recv.at[4 * a + k], device_id=sib, device_id_type=MESH)
                rc.start()
                rems.append(rc)
        for rc in rems:
            rc.wait_recv()
        for rc in rems:
            rc.wait_send()
        for lc in locs:
            lc.wait()

    any_spec = pl.BlockSpec(memory_space=pl.ANY)
    shapes = [jax.ShapeDtypeStruct((4,) + a.shape[1:], a.dtype) for a in arrs]
    res = pl.pallas_call(
        body, name=name,
        in_specs=[any_spec] * n, out_specs=[any_spec] * (2 * n), out_shape=shapes + shapes,
        scratch_shapes=[pltpu.SemaphoreType.DMA((4 * n,)), pltpu.SemaphoreType.DMA((4 * n,)),
                        pltpu.SemaphoreType.DMA((4 * n,))],
    )(*arrs)
    return res[:n], res[n:]


def _chip_exchange(arrs, name):
    n = len(arrs)

    def body(*refs):
        ins, outs = refs[:n], refs[n:2 * n]
        send, recv, loc = refs[2 * n:]
        x, y, c = lax.axis_index("x"), lax.axis_index("y"), lax.axis_index("c")
        chips = [(1 - x, y), (x, 1 - y), (1 - x, 1 - y)]
        locs, rems = [], []
        for a in range(n):
            lc = pltpu.make_async_copy(ins[a].at[2 * x + y], outs[a].at[3], loc.at[a])
            lc.start()
            locs.append(lc)
            for j, (px, py) in enumerate(chips):
                rc = pltpu.make_async_remote_copy(
                    src_ref=ins[a].at[2 * px + py], dst_ref=outs[a].at[j],
                    send_sem=send.at[3 * a + j], recv_sem=recv.at[3 * a + j], device_id=(px, py, c), device_id_type=MESH)
                rc.start()
                rems.append(rc)
        for rc in rems:
            rc.wait_recv()
        for rc in rems:
            rc.wait_send()
        for lc in locs:
            lc.wait()

    any_spec = pl.BlockSpec(memory_space=pl.ANY)
    return pl.pallas_call(
        body, name=name,
        in_specs=[any_spec] * n, out_specs=[any_spec] * n,
        out_shape=[jax.ShapeDtypeStruct(a.shape, a.dtype) for a in arrs],
        scratch_shapes=[pltpu.SemaphoreType.DMA((3 * n,)), pltpu.SemaphoreType.DMA((3 * n,)),
                        pltpu.SemaphoreType.DMA((n,))],
    )(*arrs)


def _pair_sum(own, got, name):
    _, r, c = own.shape
    tr = r if r <= 352 else r // 2
    spec = pl.BlockSpec((None, tr, c), lambda k, i: (k, i, 0))

    def body(a_ref, b_ref, o_ref):
        o_ref[...] = a_ref[...] + b_ref[...]

    return pl.pallas_call(
        body, name=name, grid=(4, r // tr), in_specs=[spec, spec], out_specs=spec,
        out_shape=jax.ShapeDtypeStruct(own.shape, own.dtype), compiler_params=_params("parallel", "parallel"),
    )(own, got)


def _adamw_math(w, g, m, v):
    m = ADAM_B1 * m + (1.0 - ADAM_B1) * g
    v = ADAM_B2 * v + (1.0 - ADAM_B2) * (g * g)
    m_hat = m / (1.0 - ADAM_B1 ** ADAM_STEP)
    v_hat = v / (1.0 - ADAM_B2 ** ADAM_STEP)
    delta = -ADAM_LR * (m_hat / (jnp.sqrt(v_hat) + ADAM_EPS) + ADAM_WD * w)
    return delta, m, v


def _adamw(w, m, v, parts, order, name):
    r, c = w.shape
    tr = r
    for cand in (256, 352, 128):
        if r % cand == 0:
            tr = cand
            break
    p = parts.shape[0]
    spec = pl.BlockSpec((tr, c), lambda i: (i, 0))

    def body(w_ref, m_ref, v_ref, p_ref, g_ref, d_ref, nm_ref, nv_ref):
        g = p_ref[order[0]]
        for k in order[1:]:
            g = g + p_ref[k]
        d, nm, nv = _adamw_math(w_ref[...], g, m_ref[...], v_ref[...])
        g_ref[...] = g
        d_ref[...] = d
        nm_ref[...] = nm
        nv_ref[...] = nv

    out = jax.ShapeDtypeStruct((r, c), F32)
    return pl.pallas_call(
        body, name=name, grid=(r // tr,),
        in_specs=[spec, spec, spec, pl.BlockSpec((p, tr, c), lambda i: (0, i, 0))],
        out_specs=[spec] * 4, out_shape=[out] * 4, compiler_params=_params("parallel"),
    )(w, m, v, parts)


def _vec(n):
    return pl.BlockSpec((1, n), lambda *_: (0, 0))


def _ln_mod(x, g, scale, shift, *, ts, name):
    s = x.shape[0]
    row = pl.BlockSpec((ts, D), lambda i: (i, 0))

    def body(x_ref, g_ref, sc_ref, sh_ref, h_ref):
        xv = x_ref[...]
        r = lax.rsqrt(jnp.mean(xv * xv, axis=-1, keepdims=True) + EPS)
        h = (xv * r) * g_ref[...]
        h_ref[...] = (h * (1.0 + sc_ref[...]) + sh_ref[...]).astype(BF16)

    return pl.pallas_call(
        body, name=name, grid=(s // ts,), in_specs=[row, _vec(D), _vec(D), _vec(D)], out_specs=row,
        out_shape=jax.ShapeDtypeStruct((s, D), BF16), compiler_params=_params("parallel"),
    )(x, g, scale, shift)


def _group_rsqrt(t, bd):
    return lax.rsqrt(_split_dot(t * t, bd) * (1.0 / HD) + EPS)


def _qk_norm(proj, qg, kg, bd, *, ts, name):
    s = proj.shape[0]

    def body(q_ref, k_ref, v_ref, qg_ref, kg_ref, bd_ref, o_ref):
        bdv = bd_ref[...]
        q, k = q_ref[...], k_ref[...]
        o_ref[:, 0:DA] = (q * _group_rsqrt(q, bdv) * qg_ref[...]).astype(BF16)
        o_ref[:, DA:2 * DA] = (k * _group_rsqrt(k, bdv) * kg_ref[...]).astype(BF16)
        o_ref[:, 2 * DA:] = v_ref[...].astype(BF16)

    col = lambda j: pl.BlockSpec((ts, DA), lambda i: (i, j))
    return pl.pallas_call(
        body, name=name, grid=(s // ts,),
        in_specs=[col(1), col(2), col(3), _vec(DA), _vec(DA), _full((DA, DA))],
        out_specs=pl.BlockSpec((ts, 3 * DA), lambda i: (i, 0)),
        out_shape=jax.ShapeDtypeStruct((s, 3 * DA), BF16), compiler_params=_params("parallel"),
    )(proj, proj, proj, qg, kg, bd)


def _tri(t, strict):
    r = lax.broadcasted_iota(jnp.int32, (t, t), 0)
    c = lax.broadcasted_iota(jnp.int32, (t, t), 1)
    return jnp.where(r > c if strict else r >= c, 1.0, 0.0).astype(BF16)


def _log_terms(z):
    l = jnp.log(1.0 + jnp.exp(-jnp.abs(z)))
    return jnp.minimum(z, 0.0) - l, jnp.minimum(-z, 0.0) - l


def _attn_fwd(qkv, *, t, name):
    s = qkv.shape[0]
    nrep = t // LANES

    def body(q_ref, k_ref, v_ref, o_ref, tot_ref, oacc, rc):
        i = pl.program_id(1)
        lane = lax.broadcasted_iota(jnp.int32, (t, LANES), 1)
        q = q_ref[...] * 0.125
        qs = [jnp.where(lane < HD, q, 0.0).astype(BF16), jnp.where(lane >= HD, q, 0.0).astype(BF16)]
        rr = lax.broadcasted_iota(jnp.int32, (t, t), 0)
        cc = lax.broadcasted_iota(jnp.int32, (t, t), 1)
        valid = cc < rr
        mstrict = _tri(t, True)
        oacc[...] = jnp.zeros_like(oacc)
        rc[...] = jnp.zeros_like(rc)

        def tile(kb, masked):
            rows = pl.ds(pl.multiple_of(kb * t, t), t)
            k = k_ref[rows, :]
            v = v_ref[rows, :]
            for a in range(2):
                z = _dot(qs[a], k, NT)
                b, m = _log_terms(z)
                if masked:
                    m = jnp.where(valid, m, 0.0)
                mb = m.astype(BF16)
                rloc = _dot(mb, mstrict)
                rcv = rc[a]
                p = jnp.exp(b + rloc + jnp.tile(rcv, (1, nrep)))
                if masked:
                    p = jnp.where(valid, p, 0.0)
                oacc[a] += _dot(p.astype(BF16), v)
                rc[a] = rcv + (rloc[:, 0:1] + mb[:, 0:1].astype(F32))

        tile(i, True)

        def step(j, carry):
            tile(i - 1 - j, False)
            return carry

        lax.fori_loop(0, i, step, 0)
        o_ref[...] = jnp.where(lane < HD, oacc[0], oacc[1])
        tot_ref[...] = jnp.where(lane < HD, rc[0], rc[1])

    qspec = pl.BlockSpec((t, LANES), lambda p, i: (i, p))
    return pl.pallas_call(
        body, name=name, grid=(DA // LANES, s // t),
        in_specs=[qspec,
                  pl.BlockSpec((s, LANES), lambda p, i: (0, DA // LANES + p)),
                  pl.BlockSpec((s, LANES), lambda p, i: (0, 2 * DA // LANES + p))],
        out_specs=[qspec, qspec],
        out_shape=[jax.ShapeDtypeStruct((s, DA), F32), jax.ShapeDtypeStruct((s, DA), F32)],
        scratch_shapes=[pltpu.VMEM((2, t, LANES), F32), pltpu.VMEM((2, t, LANES), F32)],
        compiler_params=_params("parallel", "arbitrary"),
    )(qkv, qkv, qkv)


def _attn_bwd(qkv, do, tot, *, t, name):
    s = qkv.shape[0]
    nrep = t // LANES

    def body(q_ref, k_ref, v_ref, do_ref, tot_ref, dq_ref, dk_ref, dv_ref, dqacc, rem, gc):
        i = pl.program_id(1)

        @pl.when(i == 0)
        def _():
            dk_ref[...] = jnp.zeros_like(dk_ref)
            dv_ref[...] = jnp.zeros_like(dv_ref)

        lane = lax.broadcasted_iota(jnp.int32, (t, LANES), 1)
        heads = [lane < HD, lane >= HD]
        q = q_ref[...] * 0.125
        qs = [jnp.where(h, q, 0.0).astype(BF16) for h in heads]
        dov = do_ref[...]
        dob = [jnp.where(h, dov, 0.0).astype(BF16) for h in heads]
        rr = lax.broadcasted_iota(jnp.int32, (t, t), 0)
        cc = lax.broadcasted_iota(jnp.int32, (t, t), 1)
        valid = cc < rr
        up_incl = jnp.where(rr <= cc, 1.0, 0.0).astype(BF16)
        up_strict = jnp.where(rr < cc, 1.0, 0.0).astype(BF16)
        dqacc[...] = jnp.zeros_like(dqacc)
        gc[...] = jnp.zeros_like(gc)
        totv = tot_ref[...]
        swapped = pltpu.roll(totv, HD, axis=1)
        rem[0] = jnp.where(heads[0], totv, swapped)
        rem[1] = jnp.where(heads[1], totv, swapped)

        def tile(kb, masked):
            rows = pl.ds(pl.multiple_of(kb * t, t), t)
            k = k_ref[rows, :]
            v = v_ref[rows, :]
            dk_add = jnp.zeros((t, LANES), F32)
            dv_add = jnp.zeros((t, LANES), F32)
            for a in range(2):
                z = _dot(qs[a], k, NT)
                b, m = _log_terms(z)
                if masked:
                    m = jnp.where(valid, m, 0.0)
                mb = m.astype(BF16)
                ploc = _dot(mb, up_incl)
                remv = rem[a]
                p = jnp.exp(b + (jnp.tile(remv, (1, nrep)) - ploc))
                if masked:
                    p = jnp.where(valid, p, 0.0)
                g = p * _dot(dob[a], v, NT)
                gb = g.astype(BF16)
                cloc = _dot(gb, up_strict)
                gcv = gc[a]
                dz = g - jnp.exp(b) * (g + (jnp.tile(gcv, (1, nrep)) + cloc))
                if masked:
                    dz = jnp.where(valid, dz, 0.0)
                dzb = dz.astype(BF16)
                dqacc[a] += _dot(dzb, k)
                dk_add += _dot(dzb, qs[a], TN)
                dv_add += _dot(p.astype(BF16), dob[a], TN)
                rem[a] = remv - ploc[:, t - 1:t]
                gc[a] = gcv + (cloc[:, t - 1:t] + gb[:, t - 1:t].astype(F32))
            dk_ref[rows, :] += dk_add
            dv_ref[rows, :] += dv_add

        def step(kb, carry):
            tile(kb, False)
            return carry

        lax.fori_loop(0, i, step, 0)
        tile(i, True)
        dq_ref[...] = jnp.where(heads[0], dqacc[0], dqacc[1]) * 0.125

    qspec = pl.BlockSpec((t, LANES), lambda p, i: (i, p))
    full = pl.BlockSpec((s, LANES), lambda p, i: (0, p))
    out = jax.ShapeDtypeStruct((s, DA), F32)
    return pl.pallas_call(
        body, name=name, grid=(DA // LANES, s // t),
        in_specs=[qspec, pl.BlockSpec((s, LANES), lambda p, i: (0, DA // LANES + p)),
                  pl.BlockSpec((s, LANES), lambda p, i: (0, 2 * DA // LANES + p)), qspec, qspec],
        out_specs=[qspec, full, full], out_shape=[out, out, out],
        scratch_shapes=[pltpu.VMEM((2, t, LANES), F32)] * 3,
        compiler_params=_params("parallel", "arbitrary"),
    )(qkv, qkv, qkv, do, tot)


def _shift_rows(v, k):
    return pltpu.roll(v, k % v.shape[0], axis=0)


def _pooled(u, uh, i, g, w, ts):
    halo = jnp.where(i > 0, uh, 0.0)
    ue = jnp.concatenate([halo, u], axis=0)
    acc, span = ue, 1
    while span < w:
        acc = acc + _shift_rows(acc, span)
        span *= 2
    tpos = i * ts + lax.broadcasted_iota(jnp.int32, (ts, 1), 0)
    cnt = jnp.minimum(tpos + 1, w).astype(F32)
    return acc[HALO:] / cnt - u


def _pool_mix(proj, o, pw, pb, ps, ag, bd, *, ts, name):
    s = proj.shape[0]
    hb = ts // HALO

    def body(u_ref, uh_ref, o_ref, pw_ref, pb_ref, ps_ref, ag_ref, bd_ref, mix_ref):
        i = pl.program_id(0)
        for g, w in enumerate(POOL_WINDOWS):
            cols = slice(g * LANES, (g + 1) * LANES)
            pooled = _pooled(u_ref[:, cols], uh_ref[:, cols], i, g, w, ts)
            yv = (_dot(pooled.astype(BF16), pw_ref[g]) + pb_ref[:, cols]) * ps_ref[:, cols]
            mix_ref[:, cols] = yv.astype(BF16)
        ov = o_ref[...]
        mix_ref[:, DP:] = (ov * _group_rsqrt(ov, bd_ref[...]) * ag_ref[...]).astype(BF16)

    return pl.pallas_call(
        body, name=name, grid=(s // ts,),
        in_specs=[pl.BlockSpec((ts, DP), lambda i: (i, 0)),
                  pl.BlockSpec((HALO, DP), lambda i: (jnp.maximum(i * hb - 1, 0), 0)),
                  pl.BlockSpec((ts, DA), lambda i: (i, 0)),
                  _full((4, LANES, LANES)), _vec(DP), _vec(DP), _vec(DA), _full((DA, DA))],
        out_specs=pl.BlockSpec((ts, D), lambda i: (i, 0)),
        out_shape=jax.ShapeDtypeStruct((s, D), BF16), compiler_params=_params("parallel"),
    )(proj, proj, o, pw, pb, ps, ag, bd)


def _res_ln_mod(x, att, gate, g, scale, shift, *, ts, name):
    s = x.shape[0]
    row = pl.BlockSpec((ts, D), lambda i: (i, 0))

    def body(x_ref, a_ref, gt_ref, g_ref, sc_ref, sh_ref, x1_ref, h_ref):
        x1 = x_ref[...] + gt_ref[...] * a_ref[...]
        x1_ref[...] = x1
        r = lax.rsqrt(jnp.mean(x1 * x1, axis=-1, keepdims=True) + EPS)
        h = (x1 * r) * g_ref[...]
        h_ref[...] = (h * (1.0 + sc_ref[...]) + sh_ref[...]).astype(BF16)

    return pl.pallas_call(
        body, name=name, grid=(s // ts,), in_specs=[row, row] + [_vec(D)] * 4, out_specs=[row, row],
        out_shape=[jax.ShapeDtypeStruct((s, D), F32), jax.ShapeDtypeStruct((s, D), BF16)],
        compiler_params=_params("parallel"),
    )(x, att, gate, g, scale, shift)


CF = DFF // 2


def _conv(u, uh, w_ref, b_ref, i):
    halo = jnp.where(i > 0, uh, 0.0)
    ue = jnp.concatenate([halo, u], axis=0)
    y = w_ref[2:3, :] * ue + w_ref[1:2, :] * _shift_rows(ue, 1) + w_ref[0:1, :] * _shift_rows(ue, 2)
    return y[HALO:] + b_ref[...]


def _conv_gate(up, cw, cb, *, ts, name):
    s = up.shape[0]
    hb = ts // HALO

    def body(u_ref, uh_ref, w_ref, b_ref, a_ref):
        i = pl.program_id(0)
        c = _conv(u_ref[...], uh_ref[...], w_ref, b_ref, i)
        gt, vl = c[:, :CF], c[:, CF:]
        a_ref[...] = (gt / (1.0 + jnp.exp(-gt)) * vl).astype(BF16)

    return pl.pallas_call(
        body, name=name, grid=(s // ts, 2),
        in_specs=[pl.BlockSpec((ts, 2 * CF), lambda i, j: (i, j)),
                  pl.BlockSpec((HALO, 2 * CF), lambda i, j: (jnp.maximum(i * hb - 1, 0), j)),
                  pl.BlockSpec((3, 2 * CF), lambda i, j: (0, j)), pl.BlockSpec((1, 2 * CF), lambda i, j: (0, j))],
        out_specs=pl.BlockSpec((ts, CF), lambda i, j: (i, j)),
        out_shape=jax.ShapeDtypeStruct((s, DFF), BF16), compiler_params=_params("parallel", "parallel"),
    )(up, up, cw, cb)


def _loss_head(x1, ffn, tgt, gate2, *, ts, name):
    s = x1.shape[0]
    n = s // ts
    row = pl.BlockSpec((ts, D), lambda i: (i, 0))
    acc8 = pl.BlockSpec((SUBLANES, D), lambda i: (0, 0))

    def body(x_ref, f_ref, t_ref, g_ref, dy_ref, df_ref, dg_ref, loss_ref, lacc):
        i = pl.program_id(0)

        @pl.when(i == 0)
        def _():
            lacc[...] = jnp.zeros_like(lacc)
            dg_ref[...] = jnp.zeros_like(dg_ref)

        f = f_ref[...]
        diff = x_ref[...] + g_ref[...] * f - t_ref[...]
        lacc[...] += _colsum8(diff * diff)
        dy = diff * (1.0 / D)
        dy_ref[...] = dy
        df_ref[...] = (dy * g_ref[...]).astype(BF16)
        dg_ref[...] += _colsum8(dy * f)

        @pl.when(i == n - 1)
        def _():
            loss_ref[...] = jnp.full((SUBLANES, LANES), (0.5 / D) * jnp.sum(lacc[...]), F32)

    return pl.pallas_call(
        body, name=name, grid=(n,), in_specs=[row, row, row, _vec(D)],
        out_specs=[row, row, acc8, _full((SUBLANES, LANES))],
        out_shape=[jax.ShapeDtypeStruct((s, D), F32), jax.ShapeDtypeStruct((s, D), BF16),
                   jax.ShapeDtypeStruct((SUBLANES, D), F32), jax.ShapeDtypeStruct((SUBLANES, LANES), F32)],
        scratch_shapes=[pltpu.VMEM((SUBLANES, D), F32)], compiler_params=_params("arbitrary"),
    )(x1, ffn, tgt, gate2)


def _gate_bwd(da, up, cw, cb, *, ts, name):
    s = up.shape[0]
    hb = ts // HALO

    def body(da_ref, u_ref, uh_ref, w_ref, b_ref, d_ref, db_ref):
        i = pl.program_id(1)

        @pl.when(i == 0)
        def _():
            db_ref[...] = jnp.zeros_like(db_ref)

        c = _conv(u_ref[...], uh_ref[...], w_ref, b_ref, i)
        gt, vl = c[:, :CF], c[:, CF:]
        sg = 1.0 / (1.0 + jnp.exp(-gt))
        dav = da_ref[...]
        dgt = dav * vl * (sg * (1.0 + gt * (1.0 - sg)))
        dvl = dav * (gt * sg)
        d_ref[:, :CF] = dgt.astype(BF16)
        d_ref[:, CF:] = dvl.astype(BF16)
        db_ref[:, :CF] += _colsum8(dgt)
        db_ref[:, CF:] += _colsum8(dvl)

    return pl.pallas_call(
        body, name=name, grid=(2, s // ts),
        in_specs=[pl.BlockSpec((ts, CF), lambda j, i: (i, j)),
                  pl.BlockSpec((ts, 2 * CF), lambda j, i: (i, j)),
                  pl.BlockSpec((HALO, 2 * CF), lambda j, i: (jnp.maximum(i * hb - 1, 0), j)),
                  pl.BlockSpec((3, 2 * CF), lambda j, i: (0, j)), pl.BlockSpec((1, 2 * CF), lambda j, i: (0, j))],
        out_specs=[pl.BlockSpec((ts, 2 * CF), lambda j, i: (i, j)),
                   pl.BlockSpec((SUBLANES, 2 * CF), lambda j, i: (0, j))],
        out_shape=[jax.ShapeDtypeStruct((s, 2 * DFF), BF16), jax.ShapeDtypeStruct((SUBLANES, 2 * DFF), F32)],
        compiler_params=_params("parallel", "arbitrary"),
    )(da, up, up, cw, cb)


def _conv_bwd(dc, up, cw, *, ts, tc, name):
    s = up.shape[0]
    hb = ts // HALO
    nb = s // HALO

    def body(d_ref, dn_ref, u_ref, uh_ref, w_ref, du_ref, dw_ref):
        i = pl.program_id(1)
        n = s // ts

        @pl.when(i == 0)
        def _():
            dw_ref[...] = jnp.zeros_like(dw_ref)

        dcur = d_ref[...].astype(F32)
        nxt = jnp.where(i < n - 1, dn_ref[...].astype(F32), 0.0)
        de = jnp.concatenate([dcur, nxt], axis=0)
        du = w_ref[2:3, :] * de + w_ref[1:2, :] * _shift_rows(de, -1) + w_ref[0:1, :] * _shift_rows(de, -2)
        du_ref[...] = du[:ts].astype(BF16)
        u = u_ref[...]
        ue = jnp.concatenate([jnp.where(i > 0, uh_ref[...], 0.0), u], axis=0)
        dw_ref[16:24, :] += _colsum8(dcur * u)
        dw_ref[8:16, :] += _colsum8(dcur * _shift_rows(ue, 1)[HALO:])
        dw_ref[0:8, :] += _colsum8(dcur * _shift_rows(ue, 2)[HALO:])

    return pl.pallas_call(
        body, name=name, grid=(2 * DFF // tc, s // ts),
        in_specs=[pl.BlockSpec((ts, tc), lambda j, i: (i, j)),
                  pl.BlockSpec((HALO, tc), lambda j, i: (jnp.minimum((i + 1) * hb, nb - 1), j)),
                  pl.BlockSpec((ts, tc), lambda j, i: (i, j)),
                  pl.BlockSpec((HALO, tc), lambda j, i: (jnp.maximum(i * hb - 1, 0), j)),
                  pl.BlockSpec((3, tc), lambda j, i: (0, j))],
        out_specs=[pl.BlockSpec((ts, tc), lambda j, i: (i, j)), pl.BlockSpec((24, tc), lambda j, i: (0, j))],
        out_shape=[jax.ShapeDtypeStruct((s, 2 * DFF), BF16), jax.ShapeDtypeStruct((24, 2 * DFF), F32)],
        compiler_params=_params("parallel", "arbitrary"),
    )(dc, dc, up, up, cw)


def _ln_mod_bwd(dh, xin, g, scale, resid, extra, gate, *, ts, name):
    s = xin.shape[0]
    row = pl.BlockSpec((ts, D), lambda i: (i, 0))
    acc8 = pl.BlockSpec((SUBLANES, D), lambda i: (0, 0))
    with_gate = extra is not None

    def body(*refs):
        if with_gate:
            dh_ref, x_ref, g_ref, sc_ref, r_ref, e_ref, gt_ref, dx_ref, da_ref, dsh, dsc, dg, dgt = refs
        else:
            dh_ref, x_ref, g_ref, sc_ref, r_ref, dx_ref, dsh, dsc, dg = refs
        i = pl.program_id(0)

        @pl.when(i == 0)
        def _():
            for acc in (dsh, dsc, dg) + ((dgt,) if with_gate else ()):
                acc[...] = jnp.zeros_like(acc)

        xv, dhv = x_ref[...], dh_ref[...]
        r = lax.rsqrt(jnp.mean(xv * xv, axis=-1, keepdims=True) + EPS)
        xn = xv * r
        dsh[...] += _colsum8(dhv)
        dsc[...] += _colsum8(dhv * (xn * g_ref[...]))
        dhp = dhv * (1.0 + sc_ref[...])
        dg[...] += _colsum8(dhp * xn)
        dxn = dhp * g_ref[...]
        dx = r_ref[...] + r * (dxn - xn * jnp.mean(dxn * xn, axis=-1, keepdims=True))
        dx_ref[...] = dx
        if with_gate:
            da_ref[...] = (dx * gt_ref[...]).astype(BF16)
            dgt[...] += _colsum8(dx * e_ref[...])

    f32o, p8 = jax.ShapeDtypeStruct((s, D), F32), jax.ShapeDtypeStruct((SUBLANES, D), F32)
    if with_gate:
        ins, in_specs = (dh, xin, g, scale, resid, extra, gate), [row, row, _vec(D), _vec(D), row, row, _vec(D)]
        out_specs, out_shape = [row, row, acc8, acc8, acc8, acc8], [f32o, jax.ShapeDtypeStruct((s, D), BF16), p8, p8, p8, p8]
    else:
        ins, in_specs = (dh, xin, g, scale, resid), [row, row, _vec(D), _vec(D), row]
        out_specs, out_shape = [row, acc8, acc8, acc8], [f32o, p8, p8, p8]
    return pl.pallas_call(
        body, name=name, grid=(s // ts,), in_specs=in_specs, out_specs=out_specs, out_shape=out_shape,
        compiler_params=_params("arbitrary"),
    )(*ins)


def _group_norm_bwd(t, dn_out, gvec, bd):
    r = _group_rsqrt(t, bd)
    dg_terms = dn_out * t * r
    dn = dn_out * gvec
    dt = r * (dn - t * (r * r) * (_split_dot(dn * t, bd) * (1.0 / HD)))
    return dt, dg_terms


def _mix_bwd(dmix, proj, o, pw, pb, ps, ag, bd, *, ts, name):
    s = proj.shape[0]
    hb = ts // HALO
    nb = s // HALO

    def body(dm_ref, dmn_ref, u_ref, uh_ref, o_ref, pw_ref, pb_ref, ps_ref, ag_ref, bd_ref,
             du_ref, do_ref, dpw_ref, dpb_ref, dps_ref, dag_ref):
        i = pl.program_id(0)
        n = s // ts

        @pl.when(i == 0)
        def _():
            for acc in (dpw_ref, dpb_ref, dps_ref, dag_ref):
                acc[...] = jnp.zeros_like(acc)

        for g, w in enumerate(POOL_WINDOWS):
            cols = slice(g * LANES, (g + 1) * LANES)
            wg = pw_ref[g]
            psg = ps_ref[:, cols]
            pooled = _pooled(u_ref[:, cols], uh_ref[:, cols], i, g, w, ts).astype(BF16)
            dy = dm_ref[:, cols]
            dps_ref[:, cols] += _colsum8(dy * (_dot(pooled, wg) + pb_ref[:, cols]))
            dpre = dy * psg
            dpb_ref[:, cols] += _colsum8(dpre)
            dpreb = dpre.astype(BF16)
            dpw_ref[g * LANES:(g + 1) * LANES, :] += _dot(pooled, dpreb, TN)
            dpool = _dot(dpreb, wg, NT)
            dnext = _dot((dmn_ref[:, cols] * psg).astype(BF16), wg, NT)
            dpe = jnp.concatenate([dpool, jnp.where(i < n - 1, dnext, 0.0)], axis=0)
            tpos = i * ts + lax.broadcasted_iota(jnp.int32, (ts + HALO, 1), 0)
            acc = dpe / jnp.minimum(tpos + 1, w).astype(F32)
            span = 1
            while span < w:
                acc = acc + _shift_rows(acc, -span)
                span *= 2
            du_ref[:, cols] = acc[:ts] - dpool
        ov = o_ref[...]
        dov, dg_terms = _group_norm_bwd(ov, dm_ref[:, DP:], ag_ref[...], bd_ref[...])
        do_ref[...] = dov
        dag_ref[...] += _colsum8(dg_terms)

    p8 = jax.ShapeDtypeStruct((SUBLANES, DP), F32)
    acc8 = pl.BlockSpec((SUBLANES, DP), lambda i: (0, 0))
    half = pl.BlockSpec((ts, DP), lambda i: (i, 0))
    return pl.pallas_call(
        body, name=name, grid=(s // ts,),
        in_specs=[pl.BlockSpec((ts, D), lambda i: (i, 0)),
                  pl.BlockSpec((HALO, DP), lambda i: (jnp.minimum((i + 1) * hb, nb - 1), 0)),
                  half, pl.BlockSpec((HALO, DP), lambda i: (jnp.maximum(i * hb - 1, 0), 0)),
                  half, _full((4, LANES, LANES)), _vec(DP), _vec(DP), _vec(DA), _full((DA, DA))],
        out_specs=[half, half, _full((DP, LANES)), acc8, acc8, acc8],
        out_shape=[jax.ShapeDtypeStruct((s, DP), F32), jax.ShapeDtypeStruct((s, DA), F32),
                   jax.ShapeDtypeStruct((DP, LANES), F32), p8, p8, p8],
        compiler_params=_params("arbitrary"),
    )(dmix, dmix, proj, proj, o, pw, pb, ps, ag, bd)


def _qk_norm_bwd(du, dq, dk, dv, proj, qg, kg, bd, *, ts, name):
    s = proj.shape[0]

    def body(du_ref, dq_ref, dk_ref, dv_ref, q_ref, k_ref, qg_ref, kg_ref, bd_ref, dp_ref, dqg_ref, dkg_ref):
        i = pl.program_id(0)

        @pl.when(i == 0)
        def _():
            dqg_ref[...] = jnp.zeros_like(dqg_ref)
            dkg_ref[...] = jnp.zeros_like(dkg_ref)

        bdv = bd_ref[...]
        dqr, tq = _group_norm_bwd(q_ref[...], dq_ref[...], qg_ref[...], bdv)
        dkr, tk = _group_norm_bwd(k_ref[...], dk_ref[...], kg_ref[...], bdv)
        dqg_ref[...] += _colsum8(tq)
        dkg_ref[...] += _colsum8(tk)
        dp_ref[:, 0:DP] = du_ref[...].astype(BF16)
        dp_ref[:, DP:DP + DA] = dqr.astype(BF16)
        dp_ref[:, DP + DA:DP + 2 * DA] = dkr.astype(BF16)
        dp_ref[:, DP + 2 * DA:] = dv_ref[...].astype(BF16)

    half = pl.BlockSpec((ts, DA), lambda i: (i, 0))
    col = lambda j: pl.BlockSpec((ts, DA), lambda i: (i, j))
    acc8 = pl.BlockSpec((SUBLANES, DA), lambda i: (0, 0))
    p8 = jax.ShapeDtypeStruct((SUBLANES, DA), F32)
    return pl.pallas_call(
        body, name=name, grid=(s // ts,),
        in_specs=[half, half, half, half, col(1), col(2), _vec(DA), _vec(DA), _full((DA, DA))],
        out_specs=[pl.BlockSpec((ts, DIN), lambda i: (i, 0)), acc8, acc8],
        out_shape=[jax.ShapeDtypeStruct((s, DIN), BF16), p8, p8],
        compiler_params=_params("arbitrary"),
    )(du, dq, dk, dv, proj, proj, qg, kg, bd)


def _split3(a):
    hi = a.astype(BF16)
    return hi, (a - hi.astype(F32)).astype(BF16)


def _dot3(a, b, dn):
    ah, al = _split3(a)
    bh, bl = _split3(b)
    return _dot(ah, bh, dn) + (_dot(ah, bl, dn) + _dot(al, bh, dn))


def _ada_fwd(c_all, w, b, name):
    nw = w.shape[1]

    def body(c_ref, w_ref, b_ref, o_ref):
        cv = c_ref[...]
        act = cv / (1.0 + jnp.exp(-cv))
        o_ref[...] = _dot3(act, w_ref[...], NN) + b_ref[...]

    return pl.pallas_call(
        body, name=name, in_specs=[_full((NDEV, D)), _full(w.shape), _full((1, nw))], out_specs=_full((NDEV, nw)),
        out_shape=jax.ShapeDtypeStruct((NDEV, nw), F32), grid=(1,), compiler_params=_params("arbitrary"),
    )(c_all, w, b)


def _ada_bwd(c_all, dmod, name):
    nw = dmod.shape[1]

    def body(c_ref, d_ref, o_ref):
        cv = c_ref[...]
        act = cv / (1.0 + jnp.exp(-cv))
        o_ref[...] = _dot3(act, d_ref[...], TN)[None]

    return pl.pallas_call(
        body, name=name, in_specs=[_full((NDEV, D)), _full((NDEV, nw))], out_specs=_full((1, D, nw)),
        out_shape=jax.ShapeDtypeStruct((1, D, nw), F32), grid=(1,), compiler_params=_params("arbitrary"),
    )(c_all, dmod)


def _fold_heads(v):
    acc = v[:, 0:HD]
    for h in range(1, DA // HD):
        acc = acc + v[:, h * HD:(h + 1) * HD]
    return acc


def _small_update(gathered, gathered_pw, gathered_cw, specs, params, name):
    names = [sp[0] for sp in specs]
    flat = []
    for nme in names + ["pool_w", "conv_w"]:
        flat += list(params[nme])
    n_in = len(flat)

    def body(*refs):
        ga_ref, gp_ref, gc_ref = refs[0], refs[1], refs[2]
        prm = refs[3:3 + n_in]
        outs = refs[3 + n_in:]
        per_dev = [jnp.sum(ga_ref[dv], axis=0, keepdims=True) for dv in range(NDEV)]
        total = per_dev[0]
        for dv in range(1, NDEV):
            total = total + per_dev[dv]
        k = 0
        for idx, (nme, off, width, fold) in enumerate(specs):
            g = total[:, off:off + width]
            if fold:
                g = _fold_heads(g)
            w_ref, m_ref, v_ref = prm[3 * idx:3 * idx + 3]
            d, nm, nv = _adamw_math(w_ref[...], g, m_ref[...], v_ref[...])
            for val in (g, d, nm, nv):
                outs[k][...] = val
                k += 1
        gpw = gp_ref[0]
        for dv in range(1, NDEV):
            gpw = gpw + gp_ref[dv]
        w_ref, m_ref, v_ref = prm[3 * len(specs):3 * len(specs) + 3]
        d, nm, nv = _adamw_math(w_ref[...], gpw, m_ref[...], v_ref[...])
        for val in (gpw, d, nm, nv):
            outs[k][...] = val
            k += 1
        gcw = gc_ref[0]
        for dv in range(1, NDEV):
            gcw = gcw + gc_ref[dv]
        w_ref, m_ref, v_ref = prm[3 * len(specs) + 3:3 * len(specs) + 6]
        for tap in range(3):
            row = slice(tap, tap + 1)
            g = jnp.sum(gcw[SUBLANES * tap:SUBLANES * (tap + 1)], axis=0, keepdims=True)
            d, nm, nv = _adamw_math(w_ref[row, :], g, m_ref[row, :], v_ref[row, :])
            for q, val in enumerate((g, d, nm, nv)):
                outs[k + q][row, :] = val
        k += 4
        for dv in range(NDEV):
            outs[k][dv:dv + 1, :] = per_dev[dv][:, 0:6 * D]

    out_shape, out_specs = [], []
    for nme in names + ["pool_w", "conv_w"]:
        shp = params[nme][0].shape
        out_shape += [jax.ShapeDtypeStruct(shp, F32)] * 4
        out_specs += [_full(shp)] * 4
    out_shape.append(jax.ShapeDtypeStruct((NDEV, 6 * D), F32))
    out_specs.append(_full((NDEV, 6 * D)))
    res = pl.pallas_call(
        body, name=name, grid=(1,),
        in_specs=[_full(gathered.shape), _full(gathered_pw.shape), _full(gathered_cw.shape)] + [_full(a.shape) for a in flat],
        out_specs=out_specs, out_shape=out_shape, compiler_params=_params("arbitrary"),
    )(gathered, gathered_pw, gathered_cw, *flat)
    out = {nme: tuple(res[4 * i:4 * i + 4]) for i, nme in enumerate(names + ["pool_w", "conv_w"])}
    return out, res[-1]


def _row_tile(s):
    return 512 if s % 512 == 0 else s


def kernel(x, c, ada_w, ada_b, norm1_g, w_in, pool_w, pool_b, pool_scale, q_norm_g, k_norm_g, attn_out_g, w_out, norm2_g, w_up, conv_w, conv_b, w_down, loss_target, m_ada_w, m_ada_b, m_norm1_g, m_w_in, m_pool_w, m_pool_b, m_pool_scale, m_q_norm_g, m_k_norm_g, m_attn_out_g, m_w_out, m_norm2_g, m_w_up, m_conv_w, m_conv_b, m_w_down, v_ada_w, v_ada_b, v_norm1_g, v_w_in, v_pool_w, v_pool_b, v_pool_scale, v_q_norm_g, v_k_norm_g, v_attn_out_g, v_w_out, v_norm2_g, v_w_up, v_conv_w, v_conv_b, v_w_down):
    ax, ay, ac = lax.axis_index("x"), lax.axis_index("y"), lax.axis_index("c")
    me = 4 * ax + 2 * ay + ac
    me_swapped = 4 * ay + 2 * ax + ac
    xs, tgt = x[0], loss_target[0]
    s = xs.shape[0]
    ts = _row_tile(s)
    t_attn = 256 if s % 256 == 0 else s
    bd = _block_diag_ones(DA, HD)

    c_all = _all_gather([jnp.broadcast_to(c, (SUBLANES, D))], [False], "gather_c")[0][:, 0, :]
    n_ada = ada_w.shape[2]
    ada_b_mine = lax.dynamic_slice_in_dim(ada_b, me * n_ada, n_ada, axis=1)
    mod_part = _ada_fwd(c_all, ada_w[0], ada_b_mine, "ada_fwd")
    mod_all = _all_gather([mod_part], [False], "gather_mod")[0]
    mod = lax.dynamic_index_in_dim(mod_all, me, axis=1, keepdims=False).reshape(1, 6 * D)
    shift1, scale1, gate1, shift2, scale2, gate2 = [mod[:, k * D:(k + 1) * D] for k in range(6)]

    w_in_t = w_in[0].T.astype(BF16)
    w_up_t = w_up[0].T.astype(BF16)
    gw_in, gw_out, gw_up, gw_down = _all_gather(
        [w_in_t, w_out[0].astype(BF16), w_up_t, w_down[0].astype(BF16)], [False, False, True, False], "gather_w")
    w_in_full = gw_in.reshape(DIN, D)
    w_out_full = gw_out.reshape(D, D)
    w_up_full = gw_up.reshape(2 * DFF, D)
    w_down_full = gw_down.reshape(DFF, D)
    gcw = _all_gather([jnp.pad(conv_w[0], ((0, 5), (0, 64)))], [True], "gather_conv")[0]
    cw_full = jnp.transpose(gcw[:, :3, :704], (1, 0, 2)).reshape(3, 2 * DFF)
    cb_full = jnp.transpose(conv_b.reshape(1, 2, 2, 2, 704), (0, 2, 1, 3, 4)).reshape(1, 2 * DFF)

    qg = jnp.tile(q_norm_g, (1, DA // HD))
    kg = jnp.tile(k_norm_g, (1, DA // HD))
    ag = attn_out_g.reshape(1, DA)
    pw = pool_w[0].astype(BF16)
    pb = pool_b.reshape(1, DP)
    h1 = _ln_mod(xs, norm1_g, scale1, shift1, ts=ts, name="ln1")
    proj = _matmul(h1, w_in_full, mode="nt", out_dtype=F32, tm=ts, tn=DIN, tk=D, name="in_proj")
    qkv = _qk_norm(proj, qg, kg, bd, ts=ts, name="qk_norm")
    o_raw, m_tot = _attn_fwd(qkv, t=t_attn, name="attn_fwd")
    mix = _pool_mix(proj, o_raw, pw, pb, pool_scale, ag, bd, ts=ts, name="pool_mix")
    att = _matmul(mix, w_out_full, mode="nn", out_dtype=F32, tm=ts, tn=D, tk=D, name="out_proj")
    x1, h2 = _res_ln_mod(xs, att, gate1, norm2_g, scale2, shift2, ts=ts, name="res_ln2")
    up = _matmul(h2, w_up_full, mode="nt", out_dtype=F32, tm=ts, tn=CF, tk=D, name="up_proj", n_outer=True)
    act = _conv_gate(up, cw_full, cb_full, ts=ts // 2, name="conv_gate")
    ffn = _matmul(act, w_down_full, mode="nn", out_dtype=F32, tm=ts, tn=D, tk=DFF, name="down_proj")
    dy, dffn, dgate2_p, loss_p = _loss_head(x1, ffn, tgt, gate2, ts=ts, name="loss_head")
    loss = lax.psum(loss_p[0, 0], ("x", "y", "c"))

    da = _matmul(dffn, w_down_full, mode="nt", out_dtype=F32, tm=ts, tn=CF, tk=D, name="down_bwd")
    g_w_down = _matmul(act, dffn, mode="tn", out_dtype=F32, tm=CF, tn=D, tk=ts, name="down_wgrad")
    dconv, dcb_p = _gate_bwd(da, up, cw_full, cb_full, ts=ts // 2, name="gate_bwd")
    dup, dcw_p = _conv_bwd(dconv, up, cw_full, ts=ts, tc=CF, name="conv_bwd")
    dh2 = _matmul(dup, w_up_full, mode="nn", out_dtype=F32, tm=ts, tn=D, tk=CF, name="up_bwd")
    g_w_up_t = _matmul(dup, h2, mode="tn", out_dtype=F32, tm=CF, tn=D, tk=ts, name="up_wgrad")
    dx1, datt, dshift2_p, dscale2_p, dnorm2_p, dgate1_p = _ln_mod_bwd(
        dh2, x1, norm2_g, scale2, dy, att, gate1, ts=ts, name="ln2_bwd")

    dmix = _matmul(datt, w_out_full, mode="nt", out_dtype=F32, tm=ts, tn=D, tk=D, name="out_bwd")
    g_w_out = _matmul(mix, datt, mode="tn", out_dtype=F32, tm=D, tn=D, tk=ts, name="out_wgrad")
    du, do_raw, g_pw_p, dpb_p, dps_p, dag_p = _mix_bwd(dmix, proj, o_raw, pw, pb, pool_scale, ag, bd, ts=ts, name="mix_bwd")
    dqn, dkn, dvv = _attn_bwd(qkv, do_raw, m_tot, t=t_attn, name="attn_bwd")
    dproj, dqg_p, dkg_p = _qk_norm_bwd(du, dqn, dkn, dvv, proj, qg, kg, bd, ts=ts, name="qk_norm_bwd")
    dh1 = _matmul(dproj, w_in_full, mode="nn", out_dtype=F32, tm=ts, tn=D, tk=DIN, name="in_bwd")
    g_w_in_t = _matmul(dproj, h1, mode="tn", out_dtype=F32, tm=DIN // 2, tn=D, tk=ts, name="in_wgrad")
    grad_x, dshift1_p, dscale1_p, dnorm1_p = _ln_mod_bwd(dh1, xs, norm1_g, scale1, dx1, None, None, ts=ts, name="ln1_bwd")

    big = [g_w_in_t.reshape(NDEV, DIN // NDEV, D), g_w_out.reshape(NDEV, D // NDEV, D),
           g_w_up_t.reshape(NDEV, 2 * DFF // NDEV, D), g_w_down.reshape(NDEV, DFF // NDEV, D)]
    owns, gots = _pair_exchange(big, [False, False, True, False], "rs_pair")
    sums = [_pair_sum(o_, g_, "rs_pair_sum%d" % k) for k, (o_, g_) in enumerate(zip(owns, gots))]
    parts = _chip_exchange(sums, "rs_chip")
    order = (3, 0, 1, 2)
    tr = lambda a: a[0].T
    r_in = _adamw(tr(w_in), tr(m_w_in), tr(v_w_in), parts[0], order, "adamw_w_in")
    r_out = _adamw(w_out[0], m_w_out[0], v_w_out[0], parts[1], order, "adamw_w_out")
    r_up = _adamw(tr(w_up), tr(m_w_up), tr(v_w_up), parts[2], order, "adamw_w_up")
    r_down = _adamw(w_down[0], m_w_down[0], v_w_down[0], parts[3], order, "adamw_w_down")
    r_in = [a.T[None] for a in r_in]
    r_up = [a.T[None] for a in r_up]
    r_out = [a[None] for a in r_out]
    r_down = [a[None] for a in r_down]

    dcb_nat = jnp.transpose(dcb_p.reshape(SUBLANES, 2, 2, 2, 704), (0, 2, 1, 3, 4)).reshape(SUBLANES, 2 * DFF)
    pieces = [dshift1_p, dscale1_p, dgate1_p, dshift2_p, dscale2_p, dgate2_p,
              dnorm1_p, dnorm2_p, dcb_nat, dpb_p, dps_p, dag_p, dqg_p, dkg_p]
    packed = jnp.concatenate(pieces, axis=1)
    gathered, gathered_pw, gathered_cw = _all_gather([packed, g_pw_p, dcw_p], [False, False, False], "gather_small")
    gathered_cw = lax.dynamic_index_in_dim(gathered_cw.reshape(NDEV, 3 * SUBLANES, NDEV, 704), me_swapped, axis=2, keepdims=False)
    specs = [("ada_b", 0, 6 * D, False)]
    off = 6 * D
    for nme, width, fold in (("norm1_g", D, False), ("norm2_g", D, False), ("conv_b", 2 * DFF, False),
                             ("pool_b", DP, False), ("pool_scale", DP, False), ("attn_out_g", DA, False),
                             ("q_norm_g", DA, True), ("k_norm_g", DA, True)):
        specs.append((nme, off, width, fold))
        off += width
    small = {
        "ada_b": (ada_b, m_ada_b, v_ada_b),
        "norm1_g": (norm1_g, m_norm1_g, v_norm1_g), "norm2_g": (norm2_g, m_norm2_g, v_norm2_g),
        "conv_b": (conv_b, m_conv_b, v_conv_b),
        "pool_b": (pb, m_pool_b.reshape(1, DP), v_pool_b.reshape(1, DP)),
        "pool_scale": (pool_scale, m_pool_scale, v_pool_scale),
        "attn_out_g": (ag, m_attn_out_g.reshape(1, DA), v_attn_out_g.reshape(1, DA)),
        "q_norm_g": (q_norm_g, m_q_norm_g, v_q_norm_g), "k_norm_g": (k_norm_g, m_k_norm_g, v_k_norm_g),
        "pool_w": (pool_w.reshape(DP, LANES), m_pool_w.reshape(DP, LANES), v_pool_w.reshape(DP, LANES)),
        "conv_w": (conv_w[0], m_conv_w[0], v_conv_w[0]),
    }
    upd, dmod_all = _small_update(gathered, gathered_pw, gathered_cw, specs, small, "small_update")
    g_ada_w = _ada_bwd(c_all, lax.dynamic_slice_in_dim(dmod_all, me * n_ada, n_ada, axis=1), "ada_bwd")
    r_ada = [a[None] for a in _adamw(ada_w[0], m_ada_w[0], v_ada_w[0], g_ada_w, (0,), "adamw_ada_w")]

    shapes = {"ada_b": ada_b.shape, "norm1_g": norm1_g.shape, "pool_w": pool_w.shape, "pool_b": pool_b.shape,
              "pool_scale": pool_scale.shape, "q_norm_g": q_norm_g.shape, "k_norm_g": k_norm_g.shape,
              "attn_out_g": attn_out_g.shape, "norm2_g": norm2_g.shape, "conv_w": conv_w.shape, "conv_b": conv_b.shape}
    res = {nme: [a.reshape(shapes[nme]) for a in upd[nme]] for nme in shapes}
    res.update(ada_w=r_ada, w_in=r_in, w_out=r_out, w_up=r_up, w_down=r_down)
    names = ["ada_w", "ada_b", "norm1_g", "w_in", "pool_w", "pool_b", "pool_scale", "q_norm_g", "k_norm_g",
             "attn_out_g", "w_out", "norm2_g", "w_up", "conv_w", "conv_b", "w_down"]
    outs = [loss, grad_x[None]]
    for q in range(4):
        outs += [res[nme][q] for nme in names]
    return tuple(outs)
```

```python
import functools
import math

import numpy as np
import jax
import jax.numpy as jnp
from jax import lax
from jax.experimental import pallas as pl
from jax.experimental.pallas import tpu as pltpu

F32, BF16 = jnp.float32, jnp.bfloat16
D = 1024
DP = 512
DA = 512
HD = 64
DIN = DP + 3 * DA
DFF = 2816
POOL_WINDOWS = (2, 4, 8, 16)
HALO = 16
EPS = 1e-6
LANES = 128
SUBLANES = 8
NDEV = 8
VMEM_LIMIT = 56 * 1024 * 1024
MESH = pl.DeviceIdType.MESH

ADAM_LR, ADAM_B1, ADAM_B2, ADAM_EPS, ADAM_WD, ADAM_STEP = 0.001, 0.9, 0.999, 1e-08, 0.01, 10

NN = (((1,), (0,)), ((), ()))
NT = (((1,), (1,)), ((), ()))
TN = (((0,), (0,)), ((), ()))


def _params(*sem):
    return pltpu.CompilerParams(dimension_semantics=sem, vmem_limit_bytes=VMEM_LIMIT)


def _full(shape):
    nd = len(shape)
    return pl.BlockSpec(shape, lambda *_: (0,) * nd)


def _dot(a, b, dn=NN):
    return lax.dot_general(a, b, dn, preferred_element_type=F32)


def _split_dot(a, b, dn=NN):
    hi = a.astype(BF16)
    lo = (a - hi.astype(F32)).astype(BF16)
    return _dot(hi, b, dn) + _dot(lo, b, dn)


def _colsum8(v):
    r, n = v.shape
    return v.reshape(r // SUBLANES, SUBLANES, n).sum(axis=0)


def _block_diag_ones(n, blk):
    i = np.arange(n) // blk
    return jnp.asarray((i[:, None] == i[None, :]).astype(np.float32), BF16)


def _matmul(a, b, *, mode, out_dtype, tm, tn, tk, name, n_outer=False):
    if mode == "tn":
        K, M = a.shape
        N = b.shape[1]
    elif mode == "nt":
        M, K = a.shape
        N = b.shape[0]
    else:
        M, K = a.shape
        N = b.shape[1]
    tm, tn, tk = min(tm, M), min(tn, N), min(tk, K)
    assert M % tm == 0 and N % tn == 0 and K % tk == 0, (name, M, N, K, tm, tn, tk)
    nk = K // tk
    dn = {"nn": NN, "nt": NT, "tn": TN}[mode]

    def body(a_ref, b_ref, o_ref, *acc):
        if nk == 1:
            o_ref[...] = _dot(a_ref[...], b_ref[...], dn).astype(o_ref.dtype)
            return
        acc_ref, = acc
        k = pl.program_id(2)

        @pl.when(k == 0)
        def _():
            acc_ref[...] = jnp.zeros_like(acc_ref)

        acc_ref[...] += _dot(a_ref[...], b_ref[...], dn)

        @pl.when(k == nk - 1)
        def _():
            o_ref[...] = acc_ref[...].astype(o_ref.dtype)

    if n_outer:
        gi = lambda g: (g[1], g[0], g[2])
        grid = (N // tn, M // tm, nk)
    else:
        gi = lambda g: g
        grid = (M // tm, N // tn, nk)

    def amap(*g):
        i, j, k = gi(g)
        return (k, i) if mode == "tn" else (i, k)

    def bmap(*g):
        i, j, k = gi(g)
        return (j, k) if mode == "nt" else (k, j)

    def omap(*g):
        i, j, k = gi(g)
        return (i, j)

    a_blk = (tk, tm) if mode == "tn" else (tm, tk)
    b_blk = (tn, tk) if mode == "nt" else (tk, tn)
    return pl.pallas_call(
        body, name=name, grid=grid,
        in_specs=[pl.BlockSpec(a_blk, amap), pl.BlockSpec(b_blk, bmap)],
        out_specs=pl.BlockSpec((tm, tn), omap),
        out_shape=jax.ShapeDtypeStruct((M, N), out_dtype),
        scratch_shapes=[] if nk == 1 else [pltpu.VMEM((tm, tn), F32)],
        compiler_params=_params("parallel", "parallel", "arbitrary"),
    )(a, b)


def _slot(swap, px, py, pc):
    return 4 * py + 2 * px + pc if swap else 4 * px + 2 * py + pc


def _all_gather(arrs, swaps, name):
    n = len(arrs)

    def body(*refs):
        ins, outs = refs[:n], refs[n:2 * n]
        send, recv, loc = refs[2 * n:]
        x, y, c = lax.axis_index("x"), lax.axis_index("y"), lax.axis_index("c")
        sib = (x, y, 1 - c)
        chips = [(1 - x, y), (x, 1 - y), (1 - x, 1 - y)]

        def copy(a, k, blk, to, src=None):
            rows = outs[a].at[_slot(swaps[a], *blk)]
            return pltpu.make_async_remote_copy(
                src_ref=rows if src is None else src, dst_ref=rows,
                send_sem=send.at[7 * a + k], recv_sem=recv.at[7 * a + k], device_id=to, device_id_type=MESH)

        started = []
        for a in range(n):
            mine = pltpu.make_async_copy(ins[a], outs[a].at[_slot(swaps[a], x, y, c)], loc.at[a])
            mine.start()
            started.append(mine)
        sends = []
        for a in range(n):
            first = [copy(a, 0, (x, y, c), sib, src=ins[a])]
            first += [copy(a, 1 + j, (x, y, c), (*chip, c), src=ins[a]) for j, chip in enumerate(chips)]
            for cp in first:
                cp.start()
            sends += first
        for a in range(n):
            for j, chip in enumerate(chips):
                copy(a, 1 + j, (*chip, c), (x, y, c)).wait_recv()
                fwd = copy(a, 4 + j, (*chip, c), sib)
                fwd.start()
                sends.append(fwd)
        for a in range(n):
            copy(a, 0, sib, (x, y, c)).wait_recv()
            for j, chip in enumerate(chips):
                copy(a, 4 + j, (*chip, 1 - c), (x, y, c)).wait_recv()
        for cp in sends:
            cp.wait_send()
        for mine in started:
            mine.wait()

    any_spec = pl.BlockSpec(memory_space=pl.ANY)
    return pl.pallas_call(
        body, name=name,
        in_specs=[any_spec] * n, out_specs=[any_spec] * n,
        out_shape=[jax.ShapeDtypeStruct((NDEV,) + a.shape, a.dtype) for a in arrs],
        scratch_shapes=[pltpu.SemaphoreType.DMA((7 * n,)), pltpu.SemaphoreType.DMA((7 * n,)),
                        pltpu.SemaphoreType.DMA((n,))],
    )(*arrs)


def _pair_exchange(arrs, swaps, name):
    n = len(arrs)

    def body(*refs):
        ins, gots = refs[:n], refs[n:2 * n]
        send, recv = refs[2 * n:]
        x, y, c = lax.axis_index("x"), lax.axis_index("y"), lax.axis_index("c")
        sib = (x, y, 1 - c)
        rems = []
        for a in range(n):
            for k in range(4):
                kx, ky = k // 2, k % 2
                rc = pltpu.make_async_remote_copy(
                    src_ref=ins[a].at[_slot(swaps[a], kx, ky, 1 - c)], dst_ref=gots[a].at[k],
                    send_sem=send.at[4 * a + k], recv_sem=recv.at[4 * a + k], device_id=sib, device_id_type=MESH)
                rc.start()
                rems.append(rc)
        for rc in rems:
            rc.wait_recv()
        for rc in rems:
            rc.wait_send()

    any_spec = pl.BlockSpec(memory_space=pl.ANY)
    return pl.pallas_call(
        body, name=name,
        in_specs=[any_spec] * n, out_specs=[any_spec] * n,
        out_shape=[jax.ShapeDtypeStruct((4,) + a.shape[1:], a.dtype) for a in arrs],
        scratch_shapes=[pltpu.SemaphoreType.DMA((4 * n,)), pltpu.SemaphoreType.DMA((4 * n,))],
    )(*arrs)


def _chip_exchange(arrs, name):
    n = len(arrs)

    def body(*refs):
        ins, outs = refs[:n], refs[n:2 * n]
        send, recv = refs[2 * n:]
        x, y, c = lax.axis_index("x"), lax.axis_index("y"), lax.axis_index("c")
        chips = [(1 - x, y), (x, 1 - y), (1 - x, 1 - y)]
        rems = []
        for a in range(n):
            for j, (px, py) in enumerate(chips):
                rc = pltpu.make_async_remote_copy(
                    src_ref=ins[a].at[2 * px + py], dst_ref=outs[a].at[j],
                    send_sem=send.at[3 * a + j], recv_sem=recv.at[3 * a + j], device_id=(px, py, c), device_id_type=MESH)
                rc.start()
                rems.append(rc)
        for rc in rems:
            rc.wait_recv()
        for rc in rems:
            rc.wait_send()

    any_spec = pl.BlockSpec(memory_space=pl.ANY)
    return pl.pallas_call(
        body, name=name,
        in_specs=[any_spec] * n, out_specs=[any_spec] * n,
        out_shape=[jax.ShapeDtypeStruct((3,) + a.shape[1:], a.dtype) for a in arrs],
        scratch_shapes=[pltpu.SemaphoreType.DMA((3 * n,)), pltpu.SemaphoreType.DMA((3 * n,))],
    )(*arrs)


def _pair_sum(grads, got, swap, core, name):
    _, r, c = got.shape
    tr = r if r <= 352 else r // 2

    def own_map(k, i, core_ref):
        return (_slot(swap, k // 2, k % 2, core_ref[0]), i, 0)

    def body(core_ref, a_ref, b_ref, o_ref):
        o_ref[...] = a_ref[...] + b_ref[...]

    spec = pl.BlockSpec((None, tr, c), lambda k, i, core_ref: (k, i, 0))
    return pl.pallas_call(
        body, name=name,
        grid_spec=pltpu.PrefetchScalarGridSpec(
            num_scalar_prefetch=1, grid=(4, r // tr),
            in_specs=[pl.BlockSpec((None, tr, c), own_map), spec], out_specs=spec),
        out_shape=jax.ShapeDtypeStruct(got.shape, got.dtype), compiler_params=_params("parallel", "parallel"),
    )(core, grads, got)


def _adamw_math(w, g, m, v):
    m = ADAM_B1 * m + (1.0 - ADAM_B1) * g
    v = ADAM_B2 * v + (1.0 - ADAM_B2) * (g * g)
    m_hat = m / (1.0 - ADAM_B1 ** ADAM_STEP)
    v_hat = v / (1.0 - ADAM_B2 ** ADAM_STEP)
    delta = -ADAM_LR * (m_hat / (jnp.sqrt(v_hat) + ADAM_EPS) + ADAM_WD * w)
    return delta, m, v


def _adamw_tile(r):
    for cand in (256, 352, 128):
        if r % cand == 0:
            return cand
    return r


def _adamw(w, m, v, g, name):
    r, c = w.shape
    tr = _adamw_tile(r)
    spec = pl.BlockSpec((tr, c), lambda i: (i, 0))

    def body(w_ref, m_ref, v_ref, g_ref, d_ref, nm_ref, nv_ref):
        d_ref[...], nm_ref[...], nv_ref[...] = _adamw_math(w_ref[...], g_ref[...], m_ref[...], v_ref[...])

    out = jax.ShapeDtypeStruct((r, c), F32)
    return pl.pallas_call(
        body, name=name, grid=(r // tr,), in_specs=[spec] * 4, out_specs=[spec] * 3, out_shape=[out] * 3,
        compiler_params=_params("parallel"),
    )(w, m, v, g)


def _adamw_reduce(w, m, v, sums, recv, chip, name):
    r, c = w.shape
    tr = _adamw_tile(r)
    spec = pl.BlockSpec((tr, c), lambda i, chip_ref: (i, 0))

    def body(chip_ref, w_ref, m_ref, v_ref, s_ref, p_ref, g_ref, d_ref, nm_ref, nv_ref):
        g = ((s_ref[...] + p_ref[0]) + p_ref[1]) + p_ref[2]
        g_ref[...] = g
        d_ref[...], nm_ref[...], nv_ref[...] = _adamw_math(w_ref[...], g, m_ref[...], v_ref[...])

    out = jax.ShapeDtypeStruct((r, c), F32)
    return pl.pallas_call(
        body, name=name,
        grid_spec=pltpu.PrefetchScalarGridSpec(
            num_scalar_prefetch=1, grid=(r // tr,),
            in_specs=[spec, spec, spec, pl.BlockSpec((None, tr, c), lambda i, chip_ref: (chip_ref[0], i, 0)),
                      pl.BlockSpec((3, tr, c), lambda i, chip_ref: (0, i, 0))],
            out_specs=[spec] * 4),
        out_shape=[out] * 4, compiler_params=_params("parallel"),
    )(chip, w, m, v, sums, recv)


def _vec(n):
    return pl.BlockSpec((1, n), lambda *_: (0, 0))


def _ln_mod(x, g, scale, shift, *, ts, name):
    s = x.shape[0]
    row = pl.BlockSpec((ts, D), lambda i: (i, 0))

    def body(x_ref, g_ref, sc_ref, sh_ref, h_ref):
        xv = x_ref[...]
        r = lax.rsqrt(jnp.mean(xv * xv, axis=-1, keepdims=True) + EPS)
        h = (xv * r) * g_ref[...]
        h_ref[...] = (h * (1.0 + sc_ref[...]) + sh_ref[...]).astype(BF16)

    return pl.pallas_call(
        body, name=name, grid=(s // ts,), in_specs=[row, _vec(D), _vec(D), _vec(D)], out_specs=row,
        out_shape=jax.ShapeDtypeStruct((s, D), BF16), compiler_params=_params("parallel"),
    )(x, g, scale, shift)


def _group_rsqrt(t, bd):
    return lax.rsqrt(_split_dot(t * t, bd) * (1.0 / HD) + EPS)


def _qk_norm(proj, qg, kg, bd, *, ts, name):
    s = proj.shape[0]

    def body(q_ref, k_ref, v_ref, qg_ref, kg_ref, bd_ref, o_ref):
        bdv = bd_ref[...]
        q, k = q_ref[...], k_ref[...]
        o_ref[:, 0:DA] = (q * _group_rsqrt(q, bdv) * qg_ref[...]).astype(BF16)
        o_ref[:, DA:2 * DA] = (k * _group_rsqrt(k, bdv) * kg_ref[...]).astype(BF16)
        o_ref[:, 2 * DA:] = v_ref[...].astype(BF16)

    col = lambda j: pl.BlockSpec((ts, DA), lambda i: (i, j))
    return pl.pallas_call(
        body, name=name, grid=(s // ts,),
        in_specs=[col(1), col(2), col(3), _vec(DA), _vec(DA), _full((DA, DA))],
        out_specs=pl.BlockSpec((ts, 3 * DA), lambda i: (i, 0)),
        out_shape=jax.ShapeDtypeStruct((s, 3 * DA), BF16), compiler_params=_params("parallel"),
    )(proj, proj, proj, qg, kg, bd)


def _log_terms(z):
    neg_abs = lax.bitcast_convert_type(lax.bitcast_convert_type(z, jnp.uint32) | jnp.uint32(0x80000000), F32)
    b = jnp.minimum(z, 0.0) - jnp.log(1.0 + jnp.exp(neg_abs))
    return b, b - z


def _head_masks(rows):
    lane = lax.broadcasted_iota(jnp.int32, (rows, LANES), 1)
    return [lane < HD, lane >= HD]


def _attn_fwd(qkv, *, tq, tk, name):
    s = qkv.shape[0]
    nrep = tk // LANES
    ndiag = tq // tk

    def body(q_ref, k_ref, v_ref, o_ref, tot_ref, oacc, rc):
        i = pl.program_id(1)
        heads = _head_masks(tq)
        q = q_ref[...] * 0.125
        qs = [jnp.where(h, q, 0.0).astype(BF16) for h in heads]
        dif = lax.broadcasted_iota(jnp.int32, (tq, tk), 0) - lax.broadcasted_iota(jnp.int32, (tq, tk), 1)
        kr = lax.broadcasted_iota(jnp.int32, (tk, tk), 0)
        kc = lax.broadcasted_iota(jnp.int32, (tk, tk), 1)
        later = jnp.where(kr > kc, 1.0, 0.0).astype(BF16)
        oacc[...] = jnp.zeros_like(oacc)
        rc[...] = jnp.zeros_like(rc)

        def tile(kb, thr):
            rows = pl.ds(pl.multiple_of(kb * tk, tk), tk)
            k = k_ref[rows, :]
            v = v_ref[rows, :]
            rcv = [rc[0], rc[1]]
            zs = [_dot(qs[a], k, NT) for a in range(2)]
            bs, mbs = [], []
            for a in range(2):
                b, m = _log_terms(zs[a])
                if thr is not None:
                    m = jnp.where(dif > thr, m, 0.0)
                bs.append(b)
                mbs.append(m.astype(BF16))
            rl = [_dot(mbs[a], later) for a in range(2)]
            for a in range(2):
                p = jnp.exp(bs[a] + (rl[a] + jnp.tile(rcv[a], (1, nrep))))
                if thr is not None:
                    p = jnp.where(dif > thr, p, 0.0)
                oacc[a] += _dot(p.astype(BF16), v)
                rc[a] = rcv[a] + (rl[a][:, 0:1] + mbs[a][:, 0:1].astype(F32))

        for d in reversed(range(ndiag)):
            tile(i * ndiag + d, d * tk)

        def step(j, carry):
            tile(i * ndiag - 1 - j, None)
            return carry

        lax.fori_loop(0, i * ndiag, step, 0)
        o_ref[...] = jnp.where(heads[0], oacc[0], oacc[1])
        tot_ref[...] = jnp.where(heads[0], rc[0], rc[1])

    qspec = pl.BlockSpec((tq, LANES), lambda p, i: (i, p))
    return pl.pallas_call(
        body, name=name, grid=(DA // LANES, s // tq),
        in_specs=[qspec,
                  pl.BlockSpec((s, LANES), lambda p, i: (0, DA // LANES + p)),
                  pl.BlockSpec((s, LANES), lambda p, i: (0, 2 * DA // LANES + p))],
        out_specs=[qspec, qspec],
        out_shape=[jax.ShapeDtypeStruct((s, DA), F32), jax.ShapeDtypeStruct((s, DA), F32)],
        scratch_shapes=[pltpu.VMEM((2, tq, LANES), F32), pltpu.VMEM((2, tq, LANES), F32)],
        compiler_params=_params("parallel", "arbitrary"),
    )(qkv, qkv, qkv)


def _attn_bwd(qkv, do, tot, *, tq, tk, name):
    s = qkv.shape[0]
    nrep = tk // LANES
    ndiag = tq // tk

    def body(q_ref, k_ref, v_ref, do_ref, tot_ref, dq_ref, dk_ref, dv_ref, dqacc, rem, gc):
        i = pl.program_id(1)

        @pl.when(i == 0)
        def _():
            dk_ref[...] = jnp.zeros_like(dk_ref)
            dv_ref[...] = jnp.zeros_like(dv_ref)

        heads = _head_masks(tq)
        q = q_ref[...] * 0.125
        qs = [jnp.where(h, q, 0.0).astype(BF16) for h in heads]
        dov = do_ref[...]
        dob = [jnp.where(h, dov, 0.0).astype(BF16) for h in heads]
        dif = lax.broadcasted_iota(jnp.int32, (tq, tk), 0) - lax.broadcasted_iota(jnp.int32, (tq, tk), 1)
        kr = lax.broadcasted_iota(jnp.int32, (tk, tk), 0)
        kc = lax.broadcasted_iota(jnp.int32, (tk, tk), 1)
        up_incl = jnp.where(kr <= kc, 1.0, 0.0).astype(BF16)
        up_strict = jnp.where(kr < kc, 1.0, 0.0).astype(BF16)
        dqacc[...] = jnp.zeros_like(dqacc)
        gc[...] = jnp.zeros_like(gc)
        totv = tot_ref[...]
        swapped = pltpu.roll(totv, HD, axis=1)
        rem[0] = jnp.where(heads[0], totv, swapped)
        rem[1] = jnp.where(heads[1], totv, swapped)

        def tile(kb, thr):
            rows = pl.ds(pl.multiple_of(kb * tk, tk), tk)
            k = k_ref[rows, :]
            v = v_ref[rows, :]
            remv = [rem[0], rem[1]]
            gcv = [gc[0], gc[1]]
            zs = [_dot(qs[a], k, NT) for a in range(2)]
            das = [_dot(dob[a], v, NT) for a in range(2)]
            bs, mbs = [], []
            for a in range(2):
                b, m = _log_terms(zs[a])
                if thr is not None:
                    m = jnp.where(dif > thr, m, 0.0)
                bs.append(b)
                mbs.append(m.astype(BF16))
            pl_ = [_dot(mbs[a], up_incl) for a in range(2)]
            ps, gs, gbs = [], [], []
            for a in range(2):
                p = jnp.exp(bs[a] + (jnp.tile(remv[a], (1, nrep)) - pl_[a]))
                if thr is not None:
                    p = jnp.where(dif > thr, p, 0.0)
                g = p * das[a]
                ps.append(p.astype(BF16))
                gs.append(g)
                gbs.append(g.astype(BF16))
            cl = [_dot(gbs[a], up_strict) for a in range(2)]
            dk_add = jnp.zeros((tk, LANES), F32)
            dv_add = jnp.zeros((tk, LANES), F32)
            for a in range(2):
                dz = gs[a] - jnp.exp(bs[a]) * (gs[a] + (jnp.tile(gcv[a], (1, nrep)) + cl[a]))
                if thr is not None:
                    dz = jnp.where(dif > thr, dz, 0.0)
                dzb = dz.astype(BF16)
                dqacc[a] += _dot(dzb, k)
                dk_add += _dot(dzb, qs[a], TN)
                dv_add += _dot(ps[a], dob[a], TN)
                rem[a] = remv[a] - pl_[a][:, tk - 1:tk]
                gc[a] = gcv[a] + (cl[a][:, tk - 1:tk] + gbs[a][:, tk - 1:tk].astype(F32))
            dk_ref[rows, :] += dk_add
            dv_ref[rows, :] += dv_add

        def step(kb, carry):
            tile(kb, None)
            return carry

        lax.fori_loop(0, i * ndiag, step, 0)
        for d in range(ndiag):
            tile(i * ndiag + d, d * tk)
        dq_ref[...] = jnp.where(heads[0], dqacc[0], dqacc[1]) * 0.125

    qspec = pl.BlockSpec((tq, LANES), lambda p, i: (i, p))
    full = pl.BlockSpec((s, LANES), lambda p, i: (0, p))
    out = jax.ShapeDtypeStruct((s, DA), F32)
    return pl.pallas_call(
        body, name=name, grid=(DA // LANES, s // tq),
        in_specs=[qspec, pl.BlockSpec((s, LANES), lambda p, i: (0, DA // LANES + p)),
                  pl.BlockSpec((s, LANES), lambda p, i: (0, 2 * DA // LANES + p)), qspec, qspec],
        out_specs=[qspec, full, full], out_shape=[out, out, out],
        scratch_shapes=[pltpu.VMEM((2, tq, LANES), F32)] * 3,
        compiler_params=_params("parallel", "arbitrary"),
    )(qkv, qkv, qkv, do, tot)


def _shift_rows(v, k):
    return pltpu.roll(v, k % v.shape[0], axis=0)


def _pooled(u, uh, i, g, w, ts):
    halo = jnp.where(i > 0, uh, 0.0)
    ue = jnp.concatenate([halo, u], axis=0)
    acc, span = ue, 1
    while span < w:
        acc = acc + _shift_rows(acc, span)
        span *= 2
    tpos = i * ts + lax.broadcasted_iota(jnp.int32, (ts, 1), 0)
    cnt = jnp.minimum(tpos + 1, w).astype(F32)
    return acc[HALO:] / cnt - u


def _pool_mix(proj, o, pw, pb, ps, ag, bd, *, ts, name):
    s = proj.shape[0]
    hb = ts // HALO

    def body(u_ref, uh_ref, o_ref, pw_ref, pb_ref, ps_ref, ag_ref, bd_ref, mix_ref):
        i = pl.program_id(0)
        for g, w in enumerate(POOL_WINDOWS):
            cols = slice(g * LANES, (g + 1) * LANES)
            pooled = _pooled(u_ref[:, cols], uh_ref[:, cols], i, g, w, ts)
            yv = (_dot(pooled.astype(BF16), pw_ref[g]) + pb_ref[:, cols]) * ps_ref[:, cols]
            mix_ref[:, cols] = yv.astype(BF16)
        ov = o_ref[...]
        mix_ref[:, DP:] = (ov * _group_rsqrt(ov, bd_ref[...]) * ag_ref[...]).astype(BF16)

    return pl.pallas_call(
        body, name=name, grid=(s // ts,),
        in_specs=[pl.BlockSpec((ts, DP), lambda i: (i, 0)),
                  pl.BlockSpec((HALO, DP), lambda i: (jnp.maximum(i * hb - 1, 0), 0)),
                  pl.BlockSpec((ts, DA), lambda i: (i, 0)),
                  _full((4, LANES, LANES)), _vec(DP), _vec(DP), _vec(DA), _full((DA, DA))],
        out_specs=pl.BlockSpec((ts, D), lambda i: (i, 0)),
        out_shape=jax.ShapeDtypeStruct((s, D), BF16), compiler_params=_params("parallel"),
    )(proj, proj, o, pw, pb, ps, ag, bd)


def _res_ln_mod(x, att, gate, g, scale, shift, *, ts, name):
    s = x.shape[0]
    row = pl.BlockSpec((ts, D), lambda i: (i, 0))

    def body(x_ref, a_ref, gt_ref, g_ref, sc_ref, sh_ref, x1_ref, h_ref):
        x1 = x_ref[...] + gt_ref[...] * a_ref[...]
        x1_ref[...] = x1
        r = lax.rsqrt(jnp.mean(x1 * x1, axis=-1, keepdims=True) + EPS)
        h = (x1 * r) * g_ref[...]
        h_ref[...] = (h * (1.0 + sc_ref[...]) + sh_ref[...]).astype(BF16)

    return pl.pallas_call(
        body, name=name, grid=(s // ts,), in_specs=[row, row] + [_vec(D)] * 4, out_specs=[row, row],
        out_shape=[jax.ShapeDtypeStruct((s, D), F32), jax.ShapeDtypeStruct((s, D), BF16)],
        compiler_params=_params("parallel"),
    )(x, att, gate, g, scale, shift)


CF = DFF // 2


def _conv(u, uh, w_ref, b_ref, i):
    halo = jnp.where(i > 0, uh, 0.0)
    ue = jnp.concatenate([halo, u], axis=0)
    y = w_ref[2:3, :] * ue + w_ref[1:2, :] * _shift_rows(ue, 1) + w_ref[0:1, :] * _shift_rows(ue, 2)
    return y[HALO:] + b_ref[...]


def _conv_gate(up, cw, cb, *, ts, name):
    s = up.shape[0]
    hb = ts // HALO

    def body(u_ref, uh_ref, w_ref, b_ref, a_ref):
        i = pl.program_id(0)
        c = _conv(u_ref[...], uh_ref[...], w_ref, b_ref, i)
        gt, vl = c[:, :CF], c[:, CF:]
        a_ref[...] = (gt / (1.0 + jnp.exp(-gt)) * vl).astype(BF16)

    return pl.pallas_call(
        body, name=name, grid=(s // ts, 2),
        in_specs=[pl.BlockSpec((ts, 2 * CF), lambda i, j: (i, j)),
                  pl.BlockSpec((HALO, 2 * CF), lambda i, j: (jnp.maximum(i * hb - 1, 0), j)),
                  pl.BlockSpec((3, 2 * CF), lambda i, j: (0, j)), pl.BlockSpec((1, 2 * CF), lambda i, j: (0, j))],
        out_specs=pl.BlockSpec((ts, CF), lambda i, j: (i, j)),
        out_shape=jax.ShapeDtypeStruct((s, DFF), BF16), compiler_params=_params("parallel", "parallel"),
    )(up, up, cw, cb)


def _loss_head(x1, ffn, tgt, gate2, *, ts, name):
    s = x1.shape[0]
    n = s // ts
    row = pl.BlockSpec((ts, D), lambda i: (i, 0))
    acc8 = pl.BlockSpec((SUBLANES, D), lambda i: (0, 0))

    def body(x_ref, f_ref, t_ref, g_ref, dy_ref, df_ref, dg_ref, loss_ref, lacc):
        i = pl.program_id(0)

        @pl.when(i == 0)
        def _():
            lacc[...] = jnp.zeros_like(lacc)
            dg_ref[...] = jnp.zeros_like(dg_ref)

        f = f_ref[...]
        diff = x_ref[...] + g_ref[...] * f - t_ref[...]
        lacc[...] += _colsum8(diff * diff)
        dy = diff * (1.0 / D)
        dy_ref[...] = dy
        df_ref[...] = (dy * g_ref[...]).astype(BF16)
        dg_ref[...] += _colsum8(dy * f)

        @pl.when(i == n - 1)
        def _():
            loss_ref[...] = jnp.full((SUBLANES, LANES), (0.5 / D) * jnp.sum(lacc[...]), F32)

    return pl.pallas_call(
        body, name=name, grid=(n,), in_specs=[row, row, row, _vec(D)],
        out_specs=[row, row, acc8, _full((SUBLANES, LANES))],
        out_shape=[jax.ShapeDtypeStruct((s, D), F32), jax.ShapeDtypeStruct((s, D), BF16),
                   jax.ShapeDtypeStruct((SUBLANES, D), F32), jax.ShapeDtypeStruct((SUBLANES, LANES), F32)],
        scratch_shapes=[pltpu.VMEM((SUBLANES, D), F32)], compiler_params=_params("arbitrary"),
    )(x1, ffn, tgt, gate2)


def _gate_bwd(da, up, cw, cb, *, ts, name):
    s = up.shape[0]
    hb = ts // HALO

    def body(da_ref, u_ref, uh_ref, w_ref, b_ref, d_ref, db_ref):
        i = pl.program_id(1)

        @pl.when(i == 0)
        def _():
            db_ref[...] = jnp.zeros_like(db_ref)

        c = _conv(u_ref[...], uh_ref[...], w_ref, b_ref, i)
        gt, vl = c[:, :CF], c[:, CF:]
        sg = 1.0 / (1.0 + jnp.exp(-gt))
        dav = da_ref[...]
        dgt = dav * vl * (sg * (1.0 + gt * (1.0 - sg)))
        dvl = dav * (gt * sg)
        d_ref[:, :CF] = dgt.astype(BF16)
        d_ref[:, CF:] = dvl.astype(BF16)
        db_ref[:, :CF] += _colsum8(dgt)
        db_ref[:, CF:] += _colsum8(dvl)

    return pl.pallas_call(
        body, name=name, grid=(2, s // ts),
        in_specs=[pl.BlockSpec((ts, CF), lambda j, i: (i, j)),
                  pl.BlockSpec((ts, 2 * CF), lambda j, i: (i, j)),
                  pl.BlockSpec((HALO, 2 * CF), lambda j, i: (jnp.maximum(i * hb - 1, 0), j)),
                  pl.BlockSpec((3, 2 * CF), lambda j, i: (0, j)), pl.BlockSpec((1, 2 * CF), lambda j, i: (0, j))],
        out_specs=[pl.BlockSpec((ts, 2 * CF), lambda j, i: (i, j)),
                   pl.BlockSpec((SUBLANES, 2 * CF), lambda j, i: (0, j))],
        out_shape=[jax.ShapeDtypeStruct((s, 2 * DFF), BF16), jax.ShapeDtypeStruct((SUBLANES, 2 * DFF), F32)],
        compiler_params=_params("parallel", "arbitrary"),
    )(da, up, up, cw, cb)


def _conv_bwd(dc, up, cw, *, ts, tc, name):
    s = up.shape[0]
    hb = ts // HALO
    nb = s // HALO

    def body(d_ref, dn_ref, u_ref, uh_ref, w_ref, du_ref, dw_ref):
        i = pl.program_id(1)
        n = s // ts

        @pl.when(i == 0)
        def _():
            dw_ref[...] = jnp.zeros_like(dw_ref)

        dcur = d_ref[...].astype(F32)
        nxt = jnp.where(i < n - 1, dn_ref[...].astype(F32), 0.0)
        de = jnp.concatenate([dcur, nxt], axis=0)
        du = w_ref[2:3, :] * de + w_ref[1:2, :] * _shift_rows(de, -1) + w_ref[0:1, :] * _shift_rows(de, -2)
        du_ref[...] = du[:ts].astype(BF16)
        u = u_ref[...]
        ue = jnp.concatenate([jnp.where(i > 0, uh_ref[...], 0.0), u], axis=0)
        dw_ref[16:24, :] += _colsum8(dcur * u)
        dw_ref[8:16, :] += _colsum8(dcur * _shift_rows(ue, 1)[HALO:])
        dw_ref[0:8, :] += _colsum8(dcur * _shift_rows(ue, 2)[HALO:])

    return pl.pallas_call(
        body, name=name, grid=(2 * DFF // tc, s // ts),
        in_specs=[pl.BlockSpec((ts, tc), lambda j, i: (i, j)),
                  pl.BlockSpec((HALO, tc), lambda j, i: (jnp.minimum((i + 1) * hb, nb - 1), j)),
                  pl.BlockSpec((ts, tc), lambda j, i: (i, j)),
                  pl.BlockSpec((HALO, tc), lambda j, i: (jnp.maximum(i * hb - 1, 0), j)),
                  pl.BlockSpec((3, tc), lambda j, i: (0, j))],
        out_specs=[pl.BlockSpec((ts, tc), lambda j, i: (i, j)), pl.BlockSpec((24, tc), lambda j, i: (0, j))],
        out_shape=[jax.ShapeDtypeStruct((s, 2 * DFF), BF16), jax.ShapeDtypeStruct((24, 2 * DFF), F32)],
        compiler_params=_params("parallel", "arbitrary"),
    )(dc, dc, up, up, cw)


def _ln_mod_bwd(dh, xin, g, scale, resid, extra, gate, *, ts, name):
    s = xin.shape[0]
    row = pl.BlockSpec((ts, D), lambda i: (i, 0))
    acc8 = pl.BlockSpec((SUBLANES, D), lambda i: (0, 0))
    with_gate = extra is not None

    def body(*refs):
        if with_gate:
            dh_ref, x_ref, g_ref, sc_ref, r_ref, e_ref, gt_ref, dx_ref, da_ref, dsh, dsc, dg, dgt = refs
        else:
            dh_ref, x_ref, g_ref, sc_ref, r_ref, dx_ref, dsh, dsc, dg = refs
        i = pl.program_id(0)

        @pl.when(i == 0)
        def _():
            for acc in (dsh, dsc, dg) + ((dgt,) if with_gate else ()):
                acc[...] = jnp.zeros_like(acc)

        xv, dhv = x_ref[...], dh_ref[...]
        r = lax.rsqrt(jnp.mean(xv * xv, axis=-1, keepdims=True) + EPS)
        xn = xv * r
        dsh[...] += _colsum8(dhv)
        dsc[...] += _colsum8(dhv * (xn * g_ref[...]))
        dhp = dhv * (1.0 + sc_ref[...])
        dg[...] += _colsum8(dhp * xn)
        dxn = dhp * g_ref[...]
        dx = r_ref[...] + r * (dxn - xn * jnp.mean(dxn * xn, axis=-1, keepdims=True))
        dx_ref[...] = dx
        if with_gate:
            da_ref[...] = (dx * gt_ref[...]).astype(BF16)
            dgt[...] += _colsum8(dx * e_ref[...])

    f32o, p8 = jax.ShapeDtypeStruct((s, D), F32), jax.ShapeDtypeStruct((SUBLANES, D), F32)
    if with_gate:
        ins, in_specs = (dh, xin, g, scale, resid, extra, gate), [row, row, _vec(D), _vec(D), row, row, _vec(D)]
        out_specs, out_shape = [row, row, acc8, acc8, acc8, acc8], [f32o, jax.ShapeDtypeStruct((s, D), BF16), p8, p8, p8, p8]
    else:
        ins, in_specs = (dh, xin, g, scale, resid), [row, row, _vec(D), _vec(D), row]
        out_specs, out_shape = [row, acc8, acc8, acc8], [f32o, p8, p8, p8]
    return pl.pallas_call(
        body, name=name, grid=(s // ts,), in_specs=in_specs, out_specs=out_specs, out_shape=out_shape,
        compiler_params=_params("arbitrary"),
    )(*ins)


def _group_norm_bwd(t, dn_out, gvec, bd):
    r = _group_rsqrt(t, bd)
    dg_terms = dn_out * t * r
    dn = dn_out * gvec
    dt = r * (dn - t * (r * r) * (_split_dot(dn * t, bd) * (1.0 / HD)))
    return dt, dg_terms


def _mix_bwd(dmix, proj, o, pw, pb, ps, ag, bd, *, ts, name):
    s = proj.shape[0]
    hb = ts // HALO
    nb = s // HALO

    def body(dm_ref, dmn_ref, u_ref, uh_ref, o_ref, pw_ref, pb_ref, ps_ref, ag_ref, bd_ref,
             du_ref, do_ref, dpw_ref, dpb_ref, dps_ref, dag_ref):
        i = pl.program_id(0)
        n = s // ts

        @pl.when(i == 0)
        def _():
            for acc in (dpw_ref, dpb_ref, dps_ref, dag_ref):
                acc[...] = jnp.zeros_like(acc)

        for g, w in enumerate(POOL_WINDOWS):
            cols = slice(g * LANES, (g + 1) * LANES)
            wg = pw_ref[g]
            psg = ps_ref[:, cols]
            pooled = _pooled(u_ref[:, cols], uh_ref[:, cols], i, g, w, ts).astype(BF16)
            dy = dm_ref[:, cols]
            dps_ref[:, cols] += _colsum8(dy * (_dot(pooled, wg) + pb_ref[:, cols]))
            dpre = dy * psg
            dpb_ref[:, cols] += _colsum8(dpre)
            dpreb = dpre.astype(BF16)
            dpw_ref[g * LANES:(g + 1) * LANES, :] += _dot(pooled, dpreb, TN)
            dpool = _dot(dpreb, wg, NT)
            dnext = _dot((dmn_ref[:, cols] * psg).astype(BF16), wg, NT)
            dpe = jnp.concatenate([dpool, jnp.where(i < n - 1, dnext, 0.0)], axis=0)
            tpos = i * ts + lax.broadcasted_iota(jnp.int32, (ts + HALO, 1), 0)
            acc = dpe / jnp.minimum(tpos + 1, w).astype(F32)
            span = 1
            while span < w:
                acc = acc + _shift_rows(acc, -span)
                span *= 2
            du_ref[:, cols] = acc[:ts] - dpool
        ov = o_ref[...]
        dov, dg_terms = _group_norm_bwd(ov, dm_ref[:, DP:], ag_ref[...], bd_ref[...])
        do_ref[...] = dov
        dag_ref[...] += _colsum8(dg_terms)

    p8 = jax.ShapeDtypeStruct((SUBLANES, DP), F32)
    acc8 = pl.BlockSpec((SUBLANES, DP), lambda i: (0, 0))
    half = pl.BlockSpec((ts, DP), lambda i: (i, 0))
    return pl.pallas_call(
        body, name=name, grid=(s // ts,),
        in_specs=[pl.BlockSpec((ts, D), lambda i: (i, 0)),
                  pl.BlockSpec((HALO, DP), lambda i: (jnp.minimum((i + 1) * hb, nb - 1), 0)),
                  half, pl.BlockSpec((HALO, DP), lambda i: (jnp.maximum(i * hb - 1, 0), 0)),
                  half, _full((4, LANES, LANES)), _vec(DP), _vec(DP), _vec(DA), _full((DA, DA))],
        out_specs=[half, half, _full((DP, LANES)), acc8, acc8, acc8],
        out_shape=[jax.ShapeDtypeStruct((s, DP), F32), jax.ShapeDtypeStruct((s, DA), F32),
                   jax.ShapeDtypeStruct((DP, LANES), F32), p8, p8, p8],
        compiler_params=_params("arbitrary"),
    )(dmix, dmix, proj, proj, o, pw, pb, ps, ag, bd)


def _qk_norm_bwd(du, dq, dk, dv, proj, qg, kg, bd, *, ts, name):
    s = proj.shape[0]

    def body(du_ref, dq_ref, dk_ref, dv_ref, q_ref, k_ref, qg_ref, kg_ref, bd_ref, dp_ref, dqg_ref, dkg_ref):
        i = pl.program_id(0)

        @pl.when(i == 0)
        def _():
            dqg_ref[...] = jnp.zeros_like(dqg_ref)
            dkg_ref[...] = jnp.zeros_like(dkg_ref)

        bdv = bd_ref[...]
        dqr, tq = _group_norm_bwd(q_ref[...], dq_ref[...], qg_ref[...], bdv)
        dkr, tk = _group_norm_bwd(k_ref[...], dk_ref[...], kg_ref[...], bdv)
        dqg_ref[...] += _colsum8(tq)
        dkg_ref[...] += _colsum8(tk)
        dp_ref[:, 0:DP] = du_ref[...].astype(BF16)
        dp_ref[:, DP:DP + DA] = dqr.astype(BF16)
        dp_ref[:, DP + DA:DP + 2 * DA] = dkr.astype(BF16)
        dp_ref[:, DP + 2 * DA:] = dv_ref[...].astype(BF16)

    half = pl.BlockSpec((ts, DA), lambda i: (i, 0))
    col = lambda j: pl.BlockSpec((ts, DA), lambda i: (i, j))
    acc8 = pl.BlockSpec((SUBLANES, DA), lambda i: (0, 0))
    p8 = jax.ShapeDtypeStruct((SUBLANES, DA), F32)
    return pl.pallas_call(
        body, name=name, grid=(s // ts,),
        in_specs=[half, half, half, half, col(1), col(2), _vec(DA), _vec(DA), _full((DA, DA))],
        out_specs=[pl.BlockSpec((ts, DIN), lambda i: (i, 0)), acc8, acc8],
        out_shape=[jax.ShapeDtypeStruct((s, DIN), BF16), p8, p8],
        compiler_params=_params("arbitrary"),
    )(du, dq, dk, dv, proj, proj, qg, kg, bd)


def _split3(a):
    hi = a.astype(BF16)
    return hi, (a - hi.astype(F32)).astype(BF16)


def _dot3(a, b, dn):
    ah, al = _split3(a)
    bh, bl = _split3(b)
    return _dot(ah, bh, dn) + (_dot(ah, bl, dn) + _dot(al, bh, dn))


def _ada_fwd(c_all, w, b, name):
    nw = w.shape[1]

    def body(c_ref, w_ref, b_ref, o_ref):
        cv = c_ref[...]
        act = cv / (1.0 + jnp.exp(-cv))
        o_ref[...] = _dot3(act, w_ref[...], NN) + b_ref[...]

    return pl.pallas_call(
        body, name=name, in_specs=[_full((NDEV, D)), _full(w.shape), _full((1, nw))], out_specs=_full((NDEV, nw)),
        out_shape=jax.ShapeDtypeStruct((NDEV, nw), F32), grid=(1,), compiler_params=_params("arbitrary"),
    )(c_all, w, b)


def _ada_bwd(c_all, dmod, name):
    nw = dmod.shape[1]

    def body(c_ref, d_ref, o_ref):
        cv = c_ref[...]
        act = cv / (1.0 + jnp.exp(-cv))
        o_ref[...] = _dot3(act, d_ref[...], TN)[None]

    return pl.pallas_call(
        body, name=name, in_specs=[_full((NDEV, D)), _full((NDEV, nw))], out_specs=_full((1, D, nw)),
        out_shape=jax.ShapeDtypeStruct((1, D, nw), F32), grid=(1,), compiler_params=_params("arbitrary"),
    )(c_all, dmod)


def _fold_heads(v):
    acc = v[:, 0:HD]
    for h in range(1, DA // HD):
        acc = acc + v[:, h * HD:(h + 1) * HD]
    return acc


def _small_update(gathered, gathered_pw, gathered_cw, specs, params, name):
    names = [sp[0] for sp in specs]
    flat = []
    for nme in names + ["pool_w", "conv_w"]:
        flat += list(params[nme])
    n_in = len(flat)

    def body(*refs):
        ga_ref, gp_ref, gc_ref = refs[0], refs[1], refs[2]
        prm = refs[3:3 + n_in]
        outs = refs[3 + n_in:]
        per_dev = [jnp.sum(ga_ref[dv], axis=0, keepdims=True) for dv in range(NDEV)]
        total = per_dev[0]
        for dv in range(1, NDEV):
            total = total + per_dev[dv]
        k = 0
        for idx, (nme, off, width, fold) in enumerate(specs):
            g = total[:, off:off + width]
            if fold:
                g = _fold_heads(g)
            w_ref, m_ref, v_ref = prm[3 * idx:3 * idx + 3]
            d, nm, nv = _adamw_math(w_ref[...], g, m_ref[...], v_ref[...])
            for val in (g, d, nm, nv):
                outs[k][...] = val
                k += 1
        gpw = gp_ref[0]
        for dv in range(1, NDEV):
            gpw = gpw + gp_ref[dv]
        w_ref, m_ref, v_ref = prm[3 * len(specs):3 * len(specs) + 3]
        d, nm, nv = _adamw_math(w_ref[...], gpw, m_ref[...], v_ref[...])
        for val in (gpw, d, nm, nv):
            outs[k][...] = val
            k += 1
        gcw = gc_ref[0]
        for dv in range(1, NDEV):
            gcw = gcw + gc_ref[dv]
        w_ref, m_ref, v_ref = prm[3 * len(specs) + 3:3 * len(specs) + 6]
        for tap in range(3):
            row = slice(tap, tap + 1)
            g = jnp.sum(gcw[SUBLANES * tap:SUBLANES * (tap + 1)], axis=0, keepdims=True)
            d, nm, nv = _adamw_math(w_ref[row, :], g, m_ref[row, :], v_ref[row, :])
            for q, val in enumerate((g, d, nm, nv)):
                outs[k + q][row, :] = val
        k += 4
        for dv in range(NDEV):
            outs[k][dv:dv + 1, :] = per_dev[dv][:, 0:6 * D]

    out_shape, out_specs = [], []
    for nme in names + ["pool_w", "conv_w"]:
        shp = params[nme][0].shape
        out_shape += [jax.ShapeDtypeStruct(shp, F32)] * 4
        out_specs += [_full(shp)] * 4
    out_shape.append(jax.ShapeDtypeStruct((NDEV, 6 * D), F32))
    out_specs.append(_full((NDEV, 6 * D)))
    res = pl.pallas_call(
        body, name=name, grid=(1,),
        in_specs=[_full(gathered.shape), _full(gathered_pw.shape), _full(gathered_cw.shape)] + [_full(a.shape) for a in flat],
        out_specs=out_specs, out_shape=out_shape, compiler_params=_params("arbitrary"),
    )(gathered, gathered_pw, gathered_cw, *flat)
    out = {nme: tuple(res[4 * i:4 * i + 4]) for i, nme in enumerate(names + ["pool_w", "conv_w"])}
    return out, res[-1]


def _row_tile(s):
    return 512 if s % 512 == 0 else s


def kernel(x, c, ada_w, ada_b, norm1_g, w_in, pool_w, pool_b, pool_scale, q_norm_g, k_norm_g, attn_out_g, w_out, norm2_g, w_up, conv_w, conv_b, w_down, loss_target, m_ada_w, m_ada_b, m_norm1_g, m_w_in, m_pool_w, m_pool_b, m_pool_scale, m_q_norm_g, m_k_norm_g, m_attn_out_g, m_w_out, m_norm2_g, m_w_up, m_conv_w, m_conv_b, m_w_down, v_ada_w, v_ada_b, v_norm1_g, v_w_in, v_pool_w, v_pool_b, v_pool_scale, v_q_norm_g, v_k_norm_g, v_attn_out_g, v_w_out, v_norm2_g, v_w_up, v_conv_w, v_conv_b, v_w_down):
    ax, ay, ac = lax.axis_index("x"), lax.axis_index("y"), lax.axis_index("c")
    me = 4 * ax + 2 * ay + ac
    me_swapped = 4 * ay + 2 * ax + ac
    xs, tgt = x[0], loss_target[0]
    s = xs.shape[0]
    ts = _row_tile(s)
    tq_attn, tk_attn = 512, 256
    bd = _block_diag_ones(DA, HD)

    c_all = _all_gather([jnp.broadcast_to(c, (SUBLANES, D))], [False], "gather_c")[0][:, 0, :]
    n_ada = ada_w.shape[2]
    ada_b_mine = lax.dynamic_slice_in_dim(ada_b, me * n_ada, n_ada, axis=1)
    mod_part = _ada_fwd(c_all, ada_w[0], ada_b_mine, "ada_fwd")
    mod_all = _all_gather([mod_part], [False], "gather_mod")[0]
    mod = lax.dynamic_index_in_dim(mod_all, me, axis=1, keepdims=False).reshape(1, 6 * D)
    shift1, scale1, gate1, shift2, scale2, gate2 = [mod[:, k * D:(k + 1) * D] for k in range(6)]

    w_in_t = w_in[0].T.astype(BF16)
    w_up_t = w_up[0].T.astype(BF16)
    gw_in, gw_out, gw_up, gw_down = _all_gather(
        [w_in_t, w_out[0].astype(BF16), w_up_t, w_down[0].astype(BF16)], [False, False, True, False], "gather_w")
    w_in_full = gw_in.reshape(DIN, D)
    w_out_full = gw_out.reshape(D, D)
    w_up_full = gw_up.reshape(2 * DFF, D)
    w_down_full = gw_down.reshape(DFF, D)
    gcw = _all_gather([jnp.pad(conv_w[0], ((0, 5), (0, 64)))], [True], "gather_conv")[0]
    cw_full = jnp.transpose(gcw[:, :3, :704], (1, 0, 2)).reshape(3, 2 * DFF)
    cb_full = jnp.transpose(conv_b.reshape(1, 2, 2, 2, 704), (0, 2, 1, 3, 4)).reshape(1, 2 * DFF)

    qg = jnp.tile(q_norm_g, (1, DA // HD))
    kg = jnp.tile(k_norm_g, (1, DA // HD))
    ag = attn_out_g.reshape(1, DA)
    pw = pool_w[0].astype(BF16)
    pb = pool_b.reshape(1, DP)
    h1 = _ln_mod(xs, norm1_g, scale1, shift1, ts=ts, name="ln1")
    proj = _matmul(h1, w_in_full, mode="nt", out_dtype=F32, tm=ts, tn=DIN, tk=D, name="in_proj")
    qkv = _qk_norm(proj, qg, kg, bd, ts=ts, name="qk_norm")
    o_raw, m_tot = _attn_fwd(qkv, tq=tq_attn, tk=tk_attn, name="attn_fwd")
    mix = _pool_mix(proj, o_raw, pw, pb, pool_scale, ag, bd, ts=ts, name="pool_mix")
    att = _matmul(mix, w_out_full, mode="nn", out_dtype=F32, tm=ts, tn=D, tk=D, name="out_proj")
    x1, h2 = _res_ln_mod(xs, att, gate1, norm2_g, scale2, shift2, ts=ts, name="res_ln2")
    up = _matmul(h2, w_up_full, mode="nt", out_dtype=F32, tm=ts, tn=CF, tk=D, name="up_proj", n_outer=True)
    act = _conv_gate(up, cw_full, cb_full, ts=ts // 2, name="conv_gate")
    ffn = _matmul(act, w_down_full, mode="nn", out_dtype=F32, tm=ts, tn=D, tk=DFF, name="down_proj")
    dy, dffn, dgate2_p, loss_p = _loss_head(x1, ffn, tgt, gate2, ts=ts, name="loss_head")
    loss = lax.psum(loss_p[0, 0], ("x", "y", "c"))

    da = _matmul(dffn, w_down_full, mode="nt", out_dtype=F32, tm=ts, tn=CF, tk=D, name="down_bwd")
    g_w_down = _matmul(act, dffn, mode="tn", out_dtype=F32, tm=CF, tn=D, tk=ts, name="down_wgrad")
    dconv, dcb_p = _gate_bwd(da, up, cw_full, cb_full, ts=ts // 2, name="gate_bwd")
    dup, dcw_p = _conv_bwd(dconv, up, cw_full, ts=ts, tc=CF, name="conv_bwd")
    dh2 = _matmul(dup, w_up_full, mode="nn", out_dtype=F32, tm=ts, tn=D, tk=CF, name="up_bwd")
    g_w_up_t = _matmul(dup, h2, mode="tn", out_dtype=F32, tm=CF, tn=D, tk=ts, name="up_wgrad")
    dx1, datt, dshift2_p, dscale2_p, dnorm2_p, dgate1_p = _ln_mod_bwd(
        dh2, x1, norm2_g, scale2, dy, att, gate1, ts=ts, name="ln2_bwd")

    dmix = _matmul(datt, w_out_full, mode="nt", out_dtype=F32, tm=ts, tn=D, tk=D, name="out_bwd")
    g_w_out = _matmul(mix, datt, mode="tn", out_dtype=F32, tm=D, tn=D, tk=ts, name="out_wgrad")
    du, do_raw, g_pw_p, dpb_p, dps_p, dag_p = _mix_bwd(dmix, proj, o_raw, pw, pb, pool_scale, ag, bd, ts=ts, name="mix_bwd")
    dqn, dkn, dvv = _attn_bwd(qkv, do_raw, m_tot, tq=tq_attn, tk=tk_attn, name="attn_bwd")
    dproj, dqg_p, dkg_p = _qk_norm_bwd(du, dqn, dkn, dvv, proj, qg, kg, bd, ts=ts, name="qk_norm_bwd")
    dh1 = _matmul(dproj, w_in_full, mode="nn", out_dtype=F32, tm=ts, tn=D, tk=DIN, name="in_bwd")
    g_w_in_t = _matmul(dproj, h1, mode="tn", out_dtype=F32, tm=DIN // 2, tn=D, tk=ts, name="in_wgrad")
    grad_x, dshift1_p, dscale1_p, dnorm1_p = _ln_mod_bwd(dh1, xs, norm1_g, scale1, dx1, None, None, ts=ts, name="ln1_bwd")

    big = [g_w_in_t.reshape(NDEV, DIN // NDEV, D), g_w_out.reshape(NDEV, D // NDEV, D),
           g_w_up_t.reshape(NDEV, 2 * DFF // NDEV, D), g_w_down.reshape(NDEV, DFF // NDEV, D)]
    swaps = [False, False, True, False]
    core = jnp.reshape(ac, (1,)).astype(jnp.int32)
    chip = jnp.reshape(2 * ax + ay, (1,)).astype(jnp.int32)
    gots = _pair_exchange(big, swaps, "rs_pair")
    sums = [_pair_sum(big[k], gots[k], swaps[k], core, "rs_pair_sum%d" % k) for k in range(4)]
    parts = _chip_exchange(sums, "rs_chip")
    tr = lambda a: a[0].T
    r_in = _adamw_reduce(tr(w_in), tr(m_w_in), tr(v_w_in), sums[0], parts[0], chip, "adamw_w_in")
    r_out = _adamw_reduce(w_out[0], m_w_out[0], v_w_out[0], sums[1], parts[1], chip, "adamw_w_out")
    r_up = _adamw_reduce(tr(w_up), tr(m_w_up), tr(v_w_up), sums[2], parts[2], chip, "adamw_w_up")
    r_down = _adamw_reduce(w_down[0], m_w_down[0], v_w_down[0], sums[3], parts[3], chip, "adamw_w_down")
    r_in = [a.T[None] for a in r_in]
    r_up = [a.T[None] for a in r_up]
    r_out = [a[None] for a in r_out]
    r_down = [a[None] for a in r_down]

    dcb_nat = jnp.transpose(dcb_p.reshape(SUBLANES, 2, 2, 2, 704), (0, 2, 1, 3, 4)).reshape(SUBLANES, 2 * DFF)
    pieces = [dshift1_p, dscale1_p, dgate1_p, dshift2_p, dscale2_p, dgate2_p,
              dnorm1_p, dnorm2_p, dcb_nat, dpb_p, dps_p, dag_p, dqg_p, dkg_p]
    packed = jnp.concatenate(pieces, axis=1)
    gathered, gathered_pw, gathered_cw = _all_gather([packed, g_pw_p, dcw_p], [False, False, False], "gather_small")
    gathered_cw = lax.dynamic_index_in_dim(gathered_cw.reshape(NDEV, 3 * SUBLANES, NDEV, 704), me_swapped, axis=2, keepdims=False)
    specs = [("ada_b", 0, 6 * D, False)]
    off = 6 * D
    for nme, width, fold in (("norm1_g", D, False), ("norm2_g", D, False), ("conv_b", 2 * DFF, False),
                             ("pool_b", DP, False), ("pool_scale", DP, False), ("attn_out_g", DA, False),
                             ("q_norm_g", DA, True), ("k_norm_g", DA, True)):
        specs.append((nme, off, width, fold))
        off += width
    small = {
        "ada_b": (ada_b, m_ada_b, v_ada_b),
        "norm1_g": (norm1_g, m_norm1_g, v_norm1_g), "norm2_g": (norm2_g, m_norm2_g, v_norm2_g),
        "conv_b": (conv_b, m_conv_b, v_conv_b),
        "pool_b": (pb, m_pool_b.reshape(1, DP), v_pool_b.reshape(1, DP)),
        "pool_scale": (pool_scale, m_pool_scale, v_pool_scale),
        "attn_out_g": (ag, m_attn_out_g.reshape(1, DA), v_attn_out_g.reshape(1, DA)),
        "q_norm_g": (q_norm_g, m_q_norm_g, v_q_norm_g), "k_norm_g": (k_norm_g, m_k_norm_g, v_k_norm_g),
        "pool_w": (pool_w.reshape(DP, LANES), m_pool_w.reshape(DP, LANES), v_pool_w.reshape(DP, LANES)),
        "conv_w": (conv_w[0], m_conv_w[0], v_conv_w[0]),
    }
    upd, dmod_all = _small_update(gathered, gathered_pw, gathered_cw, specs, small, "small_update")
    g_ada_w = _ada_bwd(c_all, lax.dynamic_slice_in_dim(dmod_all, me * n_ada, n_ada, axis=1), "ada_bwd")
    r_ada = [g_ada_w] + [a[None] for a in _adamw(ada_w[0], m_ada_w[0], v_ada_w[0], g_ada_w[0], "adamw_ada_w")]

    shapes = {"ada_b": ada_b.shape, "norm1_g": norm1_g.shape, "pool_w": pool_w.shape, "pool_b": pool_b.shape,
              "pool_scale": pool_scale.shape, "q_norm_g": q_norm_g.shape, "k_norm_g": k_norm_g.shape,
              "attn_out_g": attn_out_g.shape, "norm2_g": norm2_g.shape, "conv_w": conv_w.shape, "conv_b": conv_b.shape}
    res = {nme: [a.reshape(shapes[nme]) for a in upd[nme]] for nme in shapes}
    res.update(ada_w=r_ada, w_in=r_in, w_out=r_out, w_up=r_up, w_down=r_down)
    names = ["ada_w", "ada_b", "norm1_g", "w_in", "pool_w", "pool_b", "pool_scale", "q_norm_g", "k_norm_g",
             "attn_out_g", "w_out", "norm2_g", "w_up", "conv_w", "conv_b", "w_down"]
    outs = [loss, grad_x[None]]
    for q in range(4):
        outs += [res[nme][q] for nme in names]
    return tuple(outs)
```

```python
import functools
import math

import numpy as np
import jax
import jax.numpy as jnp
from jax import lax
from jax.experimental import pallas as pl
from jax.experimental.pallas import tpu as pltpu

F32, BF16 = jnp.float32, jnp.bfloat16
D = 1024
DP = 512
DA = 512
HD = 64
DIN = DP + 3 * DA
DFF = 2816
POOL_WINDOWS = (2, 4, 8, 16)
HALO = 16
EPS = 1e-6
LANES = 128
SUBLANES = 8
NDEV = 8
VMEM_LIMIT = 56 * 1024 * 1024
MESH = pl.DeviceIdType.MESH

ADAM_LR, ADAM_B1, ADAM_B2, ADAM_EPS, ADAM_WD, ADAM_STEP = 0.001, 0.9, 0.999, 1e-08, 0.01, 10

NN = (((1,), (0,)), ((), ()))
NT = (((1,), (1,)), ((), ()))
TN = (((0,), (0,)), ((), ()))


def _params(*sem):
    return pltpu.CompilerParams(dimension_semantics=sem, vmem_limit_bytes=VMEM_LIMIT)


def _full(shape):
    nd = len(shape)
    return pl.BlockSpec(shape, lambda *_: (0,) * nd)


def _dot(a, b, dn=NN):
    return lax.dot_general(a, b, dn, preferred_element_type=F32)


def _split_dot(a, b, dn=NN):
    hi = a.astype(BF16)
    lo = (a - hi.astype(F32)).astype(BF16)
    return _dot(hi, b, dn) + _dot(lo, b, dn)


def _colsum8(v):
    r, n = v.shape
    return v.reshape(r // SUBLANES, SUBLANES, n).sum(axis=0)


def _block_diag_ones(n, blk):
    i = np.arange(n) // blk
    return jnp.asarray((i[:, None] == i[None, :]).astype(np.float32), BF16)


def _matmul(a, b, *, mode, out_dtype, tm, tn, tk, name, n_outer=False):
    if mode == "tn":
        K, M = a.shape
        N = b.shape[1]
    elif mode == "nt":
        M, K = a.shape
        N = b.shape[0]
    else:
        M, K = a.shape
        N = b.shape[1]
    tm, tn, tk = min(tm, M), min(tn, N), min(tk, K)
    assert M % tm == 0 and N % tn == 0 and K % tk == 0, (name, M, N, K, tm, tn, tk)
    nk = K // tk
    dn = {"nn": NN, "nt": NT, "tn": TN}[mode]

    def body(a_ref, b_ref, o_ref, *acc):
        if nk == 1:
            o_ref[...] = _dot(a_ref[...], b_ref[...], dn).astype(o_ref.dtype)
            return
        acc_ref, = acc
        k = pl.program_id(2)

        @pl.when(k == 0)
        def _():
            acc_ref[...] = jnp.zeros_like(acc_ref)

        acc_ref[...] += _dot(a_ref[...], b_ref[...], dn)

        @pl.when(k == nk - 1)
        def _():
            o_ref[...] = acc_ref[...].astype(o_ref.dtype)

    if n_outer:
        gi = lambda g: (g[1], g[0], g[2])
        grid = (N // tn, M // tm, nk)
    else:
        gi = lambda g: g
        grid = (M // tm, N // tn, nk)

    def amap(*g):
        i, j, k = gi(g)
        return (k, i) if mode == "tn" else (i, k)

    def bmap(*g):
        i, j, k = gi(g)
        return (j, k) if mode == "nt" else (k, j)

    def omap(*g):
        i, j, k = gi(g)
        return (i, j)

    a_blk = (tk, tm) if mode == "tn" else (tm, tk)
    b_blk = (tn, tk) if mode == "nt" else (tk, tn)
    return pl.pallas_call(
        body, name=name, grid=grid,
        in_specs=[pl.BlockSpec(a_blk, amap), pl.BlockSpec(b_blk, bmap)],
        out_specs=pl.BlockSpec((tm, tn), omap),
        out_shape=jax.ShapeDtypeStruct((M, N), out_dtype),
        scratch_shapes=[] if nk == 1 else [pltpu.VMEM((tm, tn), F32)],
        compiler_params=_params("parallel", "parallel", "arbitrary"),
    )(a, b)


def _slot(swap, px, py, pc):
    return 4 * py + 2 * px + pc if swap else 4 * px + 2 * py + pc


def _all_gather(arrs, swaps, name):
    n = len(arrs)

    def body(*refs):
        ins, outs = refs[:n], refs[n:2 * n]
        send, recv, loc = refs[2 * n:]
        x, y, c = lax.axis_index("x"), lax.axis_index("y"), lax.axis_index("c")
        sib = (x, y, 1 - c)
        chips = [(1 - x, y), (x, 1 - y), (1 - x, 1 - y)]

        def copy(a, k, blk, to, src=None):
            rows = outs[a].at[_slot(swaps[a], *blk)]
            return pltpu.make_async_remote_copy(
                src_ref=rows if src is None else src, dst_ref=rows,
                send_sem=send.at[7 * a + k], recv_sem=recv.at[7 * a + k], device_id=to, device_id_type=MESH)

        started = []
        for a in range(n):
            mine = pltpu.make_async_copy(ins[a], outs[a].at[_slot(swaps[a], x, y, c)], loc.at[a])
            mine.start()
            started.append(mine)
        sends = []
        for a in range(n):
            first = [copy(a, 0, (x, y, c), sib, src=ins[a])]
            first += [copy(a, 1 + j, (x, y, c), (*chip, c), src=ins[a]) for j, chip in enumerate(chips)]
            for cp in first:
                cp.start()
            sends += first
        for a in range(n):
            for j, chip in enumerate(chips):
                copy(a, 1 + j, (*chip, c), (x, y, c)).wait_recv()
                fwd = copy(a, 4 + j, (*chip, c), sib)
                fwd.start()
                sends.append(fwd)
        for a in range(n):
            copy(a, 0, sib, (x, y, c)).wait_recv()
            for j, chip in enumerate(chips):
                copy(a, 4 + j, (*chip, 1 - c), (x, y, c)).wait_recv()
        for cp in sends:
            cp.wait_send()
        for mine in started:
            mine.wait()

    any_spec = pl.BlockSpec(memory_space=pl.ANY)
    return pl.pallas_call(
        body, name=name,
        in_specs=[any_spec] * n, out_specs=[any_spec] * n,
        out_shape=[jax.ShapeDtypeStruct((NDEV,) + a.shape, a.dtype) for a in arrs],
        scratch_shapes=[pltpu.SemaphoreType.DMA((7 * n,)), pltpu.SemaphoreType.DMA((7 * n,)),
                        pltpu.SemaphoreType.DMA((n,))],
    )(*arrs)


def _pair_exchange(arrs, swaps, name):
    n = len(arrs)

    def body(*refs):
        ins, gots = refs[:n], refs[n:2 * n]
        send, recv = refs[2 * n:]
        x, y, c = lax.axis_index("x"), lax.axis_index("y"), lax.axis_index("c")
        sib = (x, y, 1 - c)
        rems = []
        for a in range(n):
            for k in range(4):
                kx, ky = k // 2, k % 2
                rc = pltpu.make_async_remote_copy(
                    src_ref=ins[a].at[_slot(swaps[a], kx, ky, 1 - c)], dst_ref=gots[a].at[k],
                    send_sem=send.at[4 * a + k], recv_sem=recv.at[4 * a + k], device_id=sib, device_id_type=MESH)
                rc.start()
                rems.append(rc)
        for rc in rems:
            rc.wait_recv()
        for rc in rems:
            rc.wait_send()

    any_spec = pl.BlockSpec(memory_space=pl.ANY)
    return pl.pallas_call(
        body, name=name,
        in_specs=[any_spec] * n, out_specs=[any_spec] * n,
        out_shape=[jax.ShapeDtypeStruct((4,) + a.shape[1:], a.dtype) for a in arrs],
        scratch_shapes=[pltpu.SemaphoreType.DMA((4 * n,)), pltpu.SemaphoreType.DMA((4 * n,))],
    )(*arrs)


def _chip_exchange(arrs, name):
    n = len(arrs)

    def body(*refs):
        ins, outs = refs[:n], refs[n:2 * n]
        send, recv = refs[2 * n:]
        x, y, c = lax.axis_index("x"), lax.axis_index("y"), lax.axis_index("c")
        chips = [(1 - x, y), (x, 1 - y), (1 - x, 1 - y)]
        rems = []
        for a in range(n):
            for j, (px, py) in enumerate(chips):
                rc = pltpu.make_async_remote_copy(
                    src_ref=ins[a].at[2 * px + py], dst_ref=outs[a].at[j],
                    send_sem=send.at[3 * a + j], recv_sem=recv.at[3 * a + j], device_id=(px, py, c), device_id_type=MESH)
                rc.start()
                rems.append(rc)
        for rc in rems:
            rc.wait_recv()
        for rc in rems:
            rc.wait_send()

    any_spec = pl.BlockSpec(memory_space=pl.ANY)
    return pl.pallas_call(
        body, name=name,
        in_specs=[any_spec] * n, out_specs=[any_spec] * n,
        out_shape=[jax.ShapeDtypeStruct((3,) + a.shape[1:], a.dtype) for a in arrs],
        scratch_shapes=[pltpu.SemaphoreType.DMA((3 * n,)), pltpu.SemaphoreType.DMA((3 * n,))],
    )(*arrs)


def _pair_sum(grads, got, swap, core, name):
    _, r, c = got.shape
    tr = r if r <= 352 else r // 2

    def own_map(k, i, core_ref):
        return (_slot(swap, k // 2, k % 2, core_ref[0]), i, 0)

    def body(core_ref, a_ref, b_ref, o_ref):
        o_ref[...] = a_ref[...] + b_ref[...]

    spec = pl.BlockSpec((None, tr, c), lambda k, i, core_ref: (k, i, 0))
    return pl.pallas_call(
        body, name=name,
        grid_spec=pltpu.PrefetchScalarGridSpec(
            num_scalar_prefetch=1, grid=(4, r // tr),
            in_specs=[pl.BlockSpec((None, tr, c), own_map), spec], out_specs=spec),
        out_shape=jax.ShapeDtypeStruct(got.shape, got.dtype), compiler_params=_params("parallel", "parallel"),
    )(core, grads, got)


def _adamw_math(w, g, m, v):
    m = ADAM_B1 * m + (1.0 - ADAM_B1) * g
    v = ADAM_B2 * v + (1.0 - ADAM_B2) * (g * g)
    m_hat = m / (1.0 - ADAM_B1 ** ADAM_STEP)
    v_hat = v / (1.0 - ADAM_B2 ** ADAM_STEP)
    delta = -ADAM_LR * (m_hat / (jnp.sqrt(v_hat) + ADAM_EPS) + ADAM_WD * w)
    return delta, m, v


def _adamw_tile(r):
    for cand in (256, 352, 128):
        if r % cand == 0:
            return cand
    return r


def _adamw(w, m, v, g, name):
    r, c = w.shape
    tr = _adamw_tile(r)
    spec = pl.BlockSpec((tr, c), lambda i: (i, 0))

    def body(w_ref, m_ref, v_ref, g_ref, d_ref, nm_ref, nv_ref):
        d_ref[...], nm_ref[...], nv_ref[...] = _adamw_math(w_ref[...], g_ref[...], m_ref[...], v_ref[...])

    out = jax.ShapeDtypeStruct((r, c), F32)
    return pl.pallas_call(
        body, name=name, grid=(r // tr,), in_specs=[spec] * 4, out_specs=[spec] * 3, out_shape=[out] * 3,
        compiler_params=_params("parallel"),
    )(w, m, v, g)


def _adamw_reduce(w, m, v, sums, recv, chip, name):
    r, c = w.shape
    tr = _adamw_tile(r)
    spec = pl.BlockSpec((tr, c), lambda i, chip_ref: (i, 0))

    def body(chip_ref, w_ref, m_ref, v_ref, s_ref, p_ref, g_ref, d_ref, nm_ref, nv_ref):
        g = ((s_ref[...] + p_ref[0]) + p_ref[1]) + p_ref[2]
        g_ref[...] = g
        d_ref[...], nm_ref[...], nv_ref[...] = _adamw_math(w_ref[...], g, m_ref[...], v_ref[...])

    out = jax.ShapeDtypeStruct((r, c), F32)
    return pl.pallas_call(
        body, name=name,
        grid_spec=pltpu.PrefetchScalarGridSpec(
            num_scalar_prefetch=1, grid=(r // tr,),
            in_specs=[spec, spec, spec, pl.BlockSpec((None, tr, c), lambda i, chip_ref: (chip_ref[0], i, 0)),
                      pl.BlockSpec((3, tr, c), lambda i, chip_ref: (0, i, 0))],
            out_specs=[spec] * 4),
        out_shape=[out] * 4, compiler_params=_params("parallel"),
    )(chip, w, m, v, sums, recv)


def _vec(n):
    return pl.BlockSpec((1, n), lambda *_: (0, 0))


def _ln_mod(x, g, scale, shift, *, ts, name):
    s = x.shape[0]
    row = pl.BlockSpec((ts, D), lambda i: (i, 0))

    def body(x_ref, g_ref, sc_ref, sh_ref, h_ref):
        xv = x_ref[...]
        r = lax.rsqrt(jnp.mean(xv * xv, axis=-1, keepdims=True) + EPS)
        h = (xv * r) * g_ref[...]
        h_ref[...] = (h * (1.0 + sc_ref[...]) + sh_ref[...]).astype(BF16)

    return pl.pallas_call(
        body, name=name, grid=(s // ts,), in_specs=[row, _vec(D), _vec(D), _vec(D)], out_specs=row,
        out_shape=jax.ShapeDtypeStruct((s, D), BF16), compiler_params=_params("parallel"),
    )(x, g, scale, shift)


def _group_rsqrt(t, bd):
    return lax.rsqrt(_split_dot(t * t, bd) * (1.0 / HD) + EPS)


def _qk_norm(proj, qg, kg, bd, *, ts, name):
    s = proj.shape[0]

    def body(q_ref, k_ref, v_ref, qg_ref, kg_ref, bd_ref, o_ref):
        bdv = bd_ref[...]
        q, k = q_ref[...], k_ref[...]
        o_ref[:, 0:DA] = (q * _group_rsqrt(q, bdv) * qg_ref[...]).astype(BF16)
        o_ref[:, DA:2 * DA] = (k * _group_rsqrt(k, bdv) * kg_ref[...]).astype(BF16)
        o_ref[:, 2 * DA:] = v_ref[...].astype(BF16)

    col = lambda j: pl.BlockSpec((ts, DA), lambda i: (i, j))
    return pl.pallas_call(
        body, name=name, grid=(s // ts,),
        in_specs=[col(1), col(2), col(3), _vec(DA), _vec(DA), _full((DA, DA))],
        out_specs=pl.BlockSpec((ts, 3 * DA), lambda i: (i, 0)),
        out_shape=jax.ShapeDtypeStruct((s, 3 * DA), BF16), compiler_params=_params("parallel"),
    )(proj, proj, proj, qg, kg, bd)


EXP_UNDERFLOW = -120.0


def _log_terms(z):
    neg_abs = lax.bitcast_convert_type(lax.bitcast_convert_type(z, jnp.uint32) | jnp.uint32(0x80000000), F32)
    b = jnp.minimum(z, 0.0) - jnp.log(1.0 + jnp.exp(neg_abs))
    return b, b - z


def _head_masks(rows):
    lane = lax.broadcasted_iota(jnp.int32, (rows, LANES), 1)
    return [lane < HD, lane >= HD]


def _attn_fwd(qkv, *, tq, tk, name):
    s = qkv.shape[0]
    nrep = tk // LANES
    ndiag = tq // tk

    def body(q_ref, k_ref, v_ref, o_ref, tot_ref, first_ref, oacc, rc):
        i = pl.program_id(1)
        heads = _head_masks(tq)
        q = q_ref[...] * 0.125
        qs = [jnp.where(h, q, 0.0).astype(BF16) for h in heads]
        dif = lax.broadcasted_iota(jnp.int32, (tq, tk), 0) - lax.broadcasted_iota(jnp.int32, (tq, tk), 1)
        kr = lax.broadcasted_iota(jnp.int32, (tk, tk), 0)
        kc = lax.broadcasted_iota(jnp.int32, (tk, tk), 1)
        later = jnp.where(kr > kc, 1.0, 0.0).astype(BF16)
        oacc[...] = jnp.zeros_like(oacc)
        rc[...] = jnp.zeros_like(rc)

        def tile(kb, thr):
            rows = pl.ds(pl.multiple_of(kb * tk, tk), tk)
            k = k_ref[rows, :]
            v = v_ref[rows, :]
            rcv = [rc[0], rc[1]]
            zs = [_dot(qs[a], k, NT) for a in range(2)]
            bs, mbs = [], []
            for a in range(2):
                b, m = _log_terms(zs[a])
                if thr is not None:
                    m = jnp.where(dif > thr, m, 0.0)
                bs.append(b)
                mbs.append(m.astype(BF16))
            rl = [_dot(mbs[a], later) for a in range(2)]
            for a in range(2):
                p = jnp.exp(bs[a] + (rl[a] + jnp.tile(rcv[a], (1, nrep))))
                if thr is not None:
                    p = jnp.where(dif > thr, p, 0.0)
                oacc[a] += _dot(p.astype(BF16), v)
                rc[a] = rcv[a] + (rl[a][:, 0:1] + mbs[a][:, 0:1].astype(F32))

        for d in reversed(range(ndiag)):
            tile(i * ndiag + d, d * tk)

        def live():
            return jnp.max(jnp.maximum(rc[0], rc[1])) > EXP_UNDERFLOW

        def step(carry):
            kb, _ = carry
            tile(kb, None)
            return kb - 1, live()

        kb_end, _ = lax.while_loop(lambda cr: jnp.logical_and(cr[0] >= 0, cr[1]), step, (i * ndiag - 1, live()))
        first_ref[pl.program_id(0), i] = (kb_end + 1).astype(F32)
        o_ref[...] = jnp.where(heads[0], oacc[0], oacc[1])
        tot_ref[...] = jnp.where(heads[0], rc[0], rc[1])

    qspec = pl.BlockSpec((tq, LANES), lambda p, i: (i, p))
    return pl.pallas_call(
        body, name=name, grid=(DA // LANES, s // tq),
        in_specs=[qspec,
                  pl.BlockSpec((s, LANES), lambda p, i: (0, DA // LANES + p)),
                  pl.BlockSpec((s, LANES), lambda p, i: (0, 2 * DA // LANES + p))],
        out_specs=[qspec, qspec, pl.BlockSpec(memory_space=pltpu.SMEM)],
        out_shape=[jax.ShapeDtypeStruct((s, DA), F32), jax.ShapeDtypeStruct((s, DA), F32),
                   jax.ShapeDtypeStruct((DA // LANES, s // tq), F32)],
        scratch_shapes=[pltpu.VMEM((2, tq, LANES), F32), pltpu.VMEM((2, tq, LANES), F32)],
        compiler_params=_params("parallel", "arbitrary"),
    )(qkv, qkv, qkv)


def _attn_bwd(qkv, do, tot, first, *, tq, tk, name):
    s = qkv.shape[0]
    nrep = tk // LANES
    ndiag = tq // tk

    def body(q_ref, k_ref, v_ref, do_ref, tot_ref, first_ref, dq_ref, dk_ref, dv_ref, dqacc, rem, gc):
        i = pl.program_id(1)

        @pl.when(i == 0)
        def _():
            dk_ref[...] = jnp.zeros_like(dk_ref)
            dv_ref[...] = jnp.zeros_like(dv_ref)

        heads = _head_masks(tq)
        q = q_ref[...] * 0.125
        qs = [jnp.where(h, q, 0.0).astype(BF16) for h in heads]
        dov = do_ref[...]
        dob = [jnp.where(h, dov, 0.0).astype(BF16) for h in heads]
        dif = lax.broadcasted_iota(jnp.int32, (tq, tk), 0) - lax.broadcasted_iota(jnp.int32, (tq, tk), 1)
        kr = lax.broadcasted_iota(jnp.int32, (tk, tk), 0)
        kc = lax.broadcasted_iota(jnp.int32, (tk, tk), 1)
        up_incl = jnp.where(kr <= kc, 1.0, 0.0).astype(BF16)
        up_strict = jnp.where(kr < kc, 1.0, 0.0).astype(BF16)
        dqacc[...] = jnp.zeros_like(dqacc)
        gc[...] = jnp.zeros_like(gc)
        totv = tot_ref[...]
        swapped = pltpu.roll(totv, HD, axis=1)
        rem[0] = jnp.where(heads[0], totv, swapped)
        rem[1] = jnp.where(heads[1], totv, swapped)

        def tile(kb, thr):
            rows = pl.ds(pl.multiple_of(kb * tk, tk), tk)
            k = k_ref[rows, :]
            v = v_ref[rows, :]
            remv = [rem[0], rem[1]]
            gcv = [gc[0], gc[1]]
            zs = [_dot(qs[a], k, NT) for a in range(2)]
            das = [_dot(dob[a], v, NT) for a in range(2)]
            bs, mbs = [], []
            for a in range(2):
                b, m = _log_terms(zs[a])
                if thr is not None:
                    m = jnp.where(dif > thr, m, 0.0)
                bs.append(b)
                mbs.append(m.astype(BF16))
            pl_ = [_dot(mbs[a], up_incl) for a in range(2)]
            ps, gs, gbs = [], [], []
            for a in range(2):
                p = jnp.exp(bs[a] + (jnp.tile(remv[a], (1, nrep)) - pl_[a]))
                if thr is not None:
                    p = jnp.where(dif > thr, p, 0.0)
                g = p * das[a]
                ps.append(p.astype(BF16))
                gs.append(g)
                gbs.append(g.astype(BF16))
            cl = [_dot(gbs[a], up_strict) for a in range(2)]
            dk_add = jnp.zeros((tk, LANES), F32)
            dv_add = jnp.zeros((tk, LANES), F32)
            for a in range(2):
                dz = gs[a] - jnp.exp(bs[a]) * (gs[a] + (jnp.tile(gcv[a], (1, nrep)) + cl[a]))
                if thr is not None:
                    dz = jnp.where(dif > thr, dz, 0.0)
                dzb = dz.astype(BF16)
                dqacc[a] += _dot(dzb, k)
                dk_add += _dot(dzb, qs[a], TN)
                dv_add += _dot(ps[a], dob[a], TN)
                rem[a] = remv[a] - pl_[a][:, tk - 1:tk]
                gc[a] = gcv[a] + (cl[a][:, tk - 1:tk] + gbs[a][:, tk - 1:tk].astype(F32))
            dk_ref[rows, :] += dk_add
            dv_ref[rows, :] += dv_add

        def step(kb, carry):
            tile(kb, None)
            return carry

        lax.fori_loop(first_ref[pl.program_id(0), i].astype(jnp.int32), i * ndiag, step, 0)
        for d in range(ndiag):
            tile(i * ndiag + d, d * tk)
        dq_ref[...] = jnp.where(heads[0], dqacc[0], dqacc[1]) * 0.125

    qspec = pl.BlockSpec((tq, LANES), lambda p, i: (i, p))
    full = pl.BlockSpec((s, LANES), lambda p, i: (0, p))
    out = jax.ShapeDtypeStruct((s, DA), F32)
    return pl.pallas_call(
        body, name=name, grid=(DA // LANES, s // tq),
        in_specs=[qspec, pl.BlockSpec((s, LANES), lambda p, i: (0, DA // LANES + p)),
                  pl.BlockSpec((s, LANES), lambda p, i: (0, 2 * DA // LANES + p)), qspec, qspec,
                  pl.BlockSpec(memory_space=pltpu.SMEM)],
        out_specs=[qspec, full, full], out_shape=[out, out, out],
        scratch_shapes=[pltpu.VMEM((2, tq, LANES), F32)] * 3,
        compiler_params=_params("parallel", "arbitrary"),
    )(qkv, qkv, qkv, do, tot, first)


def _shift_rows(v, k):
    return pltpu.roll(v, k % v.shape[0], axis=0)


def _pooled(u, uh, i, g, w, ts):
    halo = jnp.where(i > 0, uh, 0.0)
    ue = jnp.concatenate([halo, u], axis=0)
    acc, span = ue, 1
    while span < w:
        acc = acc + _shift_rows(acc, span)
        span *= 2
    tpos = i * ts + lax.broadcasted_iota(jnp.int32, (ts, 1), 0)
    cnt = jnp.minimum(tpos + 1, w).astype(F32)
    return acc[HALO:] / cnt - u


def _pool_mix(proj, o, pw, pb, ps, ag, bd, *, ts, name):
    s = proj.shape[0]
    hb = ts // HALO

    def body(u_ref, uh_ref, o_ref, pw_ref, pb_ref, ps_ref, ag_ref, bd_ref, mix_ref):
        i = pl.program_id(0)
        for g, w in enumerate(POOL_WINDOWS):
            cols = slice(g * LANES, (g + 1) * LANES)
            pooled = _pooled(u_ref[:, cols], uh_ref[:, cols], i, g, w, ts)
            yv = (_dot(pooled.astype(BF16), pw_ref[g]) + pb_ref[:, cols]) * ps_ref[:, cols]
            mix_ref[:, cols] = yv.astype(BF16)
        ov = o_ref[...]
        mix_ref[:, DP:] = (ov * _group_rsqrt(ov, bd_ref[...]) * ag_ref[...]).astype(BF16)

    return pl.pallas_call(
        body, name=name, grid=(s // ts,),
        in_specs=[pl.BlockSpec((ts, DP), lambda i: (i, 0)),
                  pl.BlockSpec((HALO, DP), lambda i: (jnp.maximum(i * hb - 1, 0), 0)),
                  pl.BlockSpec((ts, DA), lambda i: (i, 0)),
                  _full((4, LANES, LANES)), _vec(DP), _vec(DP), _vec(DA), _full((DA, DA))],
        out_specs=pl.BlockSpec((ts, D), lambda i: (i, 0)),
        out_shape=jax.ShapeDtypeStruct((s, D), BF16), compiler_params=_params("parallel"),
    )(proj, proj, o, pw, pb, ps, ag, bd)


def _res_ln_mod(x, att, gate, g, scale, shift, *, ts, name):
    s = x.shape[0]
    row = pl.BlockSpec((ts, D), lambda i: (i, 0))

    def body(x_ref, a_ref, gt_ref, g_ref, sc_ref, sh_ref, x1_ref, h_ref):
        x1 = x_ref[...] + gt_ref[...] * a_ref[...]
        x1_ref[...] = x1
        r = lax.rsqrt(jnp.mean(x1 * x1, axis=-1, keepdims=True) + EPS)
        h = (x1 * r) * g_ref[...]
        h_ref[...] = (h * (1.0 + sc_ref[...]) + sh_ref[...]).astype(BF16)

    return pl.pallas_call(
        body, name=name, grid=(s // ts,), in_specs=[row, row] + [_vec(D)] * 4, out_specs=[row, row],
        out_shape=[jax.ShapeDtypeStruct((s, D), F32), jax.ShapeDtypeStruct((s, D), BF16)],
        compiler_params=_params("parallel"),
    )(x, att, gate, g, scale, shift)


CF = DFF // 2


def _conv(u, uh, w_ref, b_ref, i):
    halo = jnp.where(i > 0, uh, 0.0)
    ue = jnp.concatenate([halo, u], axis=0)
    y = w_ref[2:3, :] * ue + w_ref[1:2, :] * _shift_rows(ue, 1) + w_ref[0:1, :] * _shift_rows(ue, 2)
    return y[HALO:] + b_ref[...]


def _conv_gate(up, cw, cb, *, ts, name):
    s = up.shape[0]
    hb = ts // HALO

    def body(u_ref, uh_ref, w_ref, b_ref, a_ref):
        i = pl.program_id(0)
        c = _conv(u_ref[...], uh_ref[...], w_ref, b_ref, i)
        gt, vl = c[:, :CF], c[:, CF:]
        a_ref[...] = (gt / (1.0 + jnp.exp(-gt)) * vl).astype(BF16)

    return pl.pallas_call(
        body, name=name, grid=(s // ts, 2),
        in_specs=[pl.BlockSpec((ts, 2 * CF), lambda i, j: (i, j)),
                  pl.BlockSpec((HALO, 2 * CF), lambda i, j: (jnp.maximum(i * hb - 1, 0), j)),
                  pl.BlockSpec((3, 2 * CF), lambda i, j: (0, j)), pl.BlockSpec((1, 2 * CF), lambda i, j: (0, j))],
        out_specs=pl.BlockSpec((ts, CF), lambda i, j: (i, j)),
        out_shape=jax.ShapeDtypeStruct((s, DFF), BF16), compiler_params=_params("parallel", "parallel"),
    )(up, up, cw, cb)


def _loss_head(x1, ffn, tgt, gate2, *, ts, name):
    s = x1.shape[0]
    n = s // ts
    row = pl.BlockSpec((ts, D), lambda i: (i, 0))
    acc8 = pl.BlockSpec((SUBLANES, D), lambda i: (0, 0))

    def body(x_ref, f_ref, t_ref, g_ref, dy_ref, df_ref, dg_ref, loss_ref, lacc):
        i = pl.program_id(0)

        @pl.when(i == 0)
        def _():
            lacc[...] = jnp.zeros_like(lacc)
            dg_ref[...] = jnp.zeros_like(dg_ref)

        f = f_ref[...]
        diff = x_ref[...] + g_ref[...] * f - t_ref[...]
        lacc[...] += _colsum8(diff * diff)
        dy = diff * (1.0 / D)
        dy_ref[...] = dy
        df_ref[...] = (dy * g_ref[...]).astype(BF16)
        dg_ref[...] += _colsum8(dy * f)

        @pl.when(i == n - 1)
        def _():
            loss_ref[...] = jnp.full((SUBLANES, LANES), (0.5 / D) * jnp.sum(lacc[...]), F32)

    return pl.pallas_call(
        body, name=name, grid=(n,), in_specs=[row, row, row, _vec(D)],
        out_specs=[row, row, acc8, _full((SUBLANES, LANES))],
        out_shape=[jax.ShapeDtypeStruct((s, D), F32), jax.ShapeDtypeStruct((s, D), BF16),
                   jax.ShapeDtypeStruct((SUBLANES, D), F32), jax.ShapeDtypeStruct((SUBLANES, LANES), F32)],
        scratch_shapes=[pltpu.VMEM((SUBLANES, D), F32)], compiler_params=_params("arbitrary"),
    )(x1, ffn, tgt, gate2)


def _gate_bwd(da, up, cw, cb, *, ts, name):
    s = up.shape[0]
    hb = ts // HALO

    def body(da_ref, u_ref, uh_ref, w_ref, b_ref, d_ref, db_ref):
        i = pl.program_id(1)

        @pl.when(i == 0)
        def _():
            db_ref[...] = jnp.zeros_like(db_ref)

        c = _conv(u_ref[...], uh_ref[...], w_ref, b_ref, i)
        gt, vl = c[:, :CF], c[:, CF:]
        sg = 1.0 / (1.0 + jnp.exp(-gt))
        dav = da_ref[...]
        dgt = dav * vl * (sg * (1.0 + gt * (1.0 - sg)))
        dvl = dav * (gt * sg)
        d_ref[:, :CF] = dgt.astype(BF16)
        d_ref[:, CF:] = dvl.astype(BF16)
        db_ref[:, :CF] += _colsum8(dgt)
        db_ref[:, CF:] += _colsum8(dvl)

    return pl.pallas_call(
        body, name=name, grid=(2, s // ts),
        in_specs=[pl.BlockSpec((ts, CF), lambda j, i: (i, j)),
                  pl.BlockSpec((ts, 2 * CF), lambda j, i: (i, j)),
                  pl.BlockSpec((HALO, 2 * CF), lambda j, i: (jnp.maximum(i * hb - 1, 0), j)),
                  pl.BlockSpec((3, 2 * CF), lambda j, i: (0, j)), pl.BlockSpec((1, 2 * CF), lambda j, i: (0, j))],
        out_specs=[pl.BlockSpec((ts, 2 * CF), lambda j, i: (i, j)),
                   pl.BlockSpec((SUBLANES, 2 * CF), lambda j, i: (0, j))],
        out_shape=[jax.ShapeDtypeStruct((s, 2 * DFF), BF16), jax.ShapeDtypeStruct((SUBLANES, 2 * DFF), F32)],
        compiler_params=_params("parallel", "arbitrary"),
    )(da, up, up, cw, cb)


def _conv_bwd(dc, up, cw, *, ts, tc, name):
    s = up.shape[0]
    hb = ts // HALO
    nb = s // HALO

    def body(d_ref, dn_ref, u_ref, uh_ref, w_ref, du_ref, dw_ref):
        i = pl.program_id(1)
        n = s // ts

        @pl.when(i == 0)
        def _():
            dw_ref[...] = jnp.zeros_like(dw_ref)

        dcur = d_ref[...].astype(F32)
        nxt = jnp.where(i < n - 1, dn_ref[...].astype(F32), 0.0)
        de = jnp.concatenate([dcur, nxt], axis=0)
        du = w_ref[2:3, :] * de + w_ref[1:2, :] * _shift_rows(de, -1) + w_ref[0:1, :] * _shift_rows(de, -2)
        du_ref[...] = du[:ts].astype(BF16)
        u = u_ref[...]
        ue = jnp.concatenate([jnp.where(i > 0, uh_ref[...], 0.0), u], axis=0)
        dw_ref[16:24, :] += _colsum8(dcur * u)
        dw_ref[8:16, :] += _colsum8(dcur * _shift_rows(ue, 1)[HALO:])
        dw_ref[0:8, :] += _colsum8(dcur * _shift_rows(ue, 2)[HALO:])

    return pl.pallas_call(
        body, name=name, grid=(2 * DFF // tc, s // ts),
        in_specs=[pl.BlockSpec((ts, tc), lambda j, i: (i, j)),
                  pl.BlockSpec((HALO, tc), lambda j, i: (jnp.minimum((i + 1) * hb, nb - 1), j)),
                  pl.BlockSpec((ts, tc), lambda j, i: (i, j)),
                  pl.BlockSpec((HALO, tc), lambda j, i: (jnp.maximum(i * hb - 1, 0), j)),
                  pl.BlockSpec((3, tc), lambda j, i: (0, j))],
        out_specs=[pl.BlockSpec((ts, tc), lambda j, i: (i, j)), pl.BlockSpec((24, tc), lambda j, i: (0, j))],
        out_shape=[jax.ShapeDtypeStruct((s, 2 * DFF), BF16), jax.ShapeDtypeStruct((24, 2 * DFF), F32)],
        compiler_params=_params("parallel", "arbitrary"),
    )(dc, dc, up, up, cw)


def _ln_mod_bwd(dh, xin, g, scale, resid, extra, gate, *, ts, name):
    s = xin.shape[0]
    row = pl.BlockSpec((ts, D), lambda i: (i, 0))
    acc8 = pl.BlockSpec((SUBLANES, D), lambda i: (0, 0))
    with_gate = extra is not None

    def body(*refs):
        if with_gate:
            dh_ref, x_ref, g_ref, sc_ref, r_ref, e_ref, gt_ref, dx_ref, da_ref, dsh, dsc, dg, dgt = refs
        else:
            dh_ref, x_ref, g_ref, sc_ref, r_ref, dx_ref, dsh, dsc, dg = refs
        i = pl.program_id(0)

        @pl.when(i == 0)
        def _():
            for acc in (dsh, dsc, dg) + ((dgt,) if with_gate else ()):
                acc[...] = jnp.zeros_like(acc)

        xv, dhv = x_ref[...], dh_ref[...]
        r = lax.rsqrt(jnp.mean(xv * xv, axis=-1, keepdims=True) + EPS)
        xn = xv * r
        dsh[...] += _colsum8(dhv)
        dsc[...] += _colsum8(dhv * (xn * g_ref[...]))
        dhp = dhv * (1.0 + sc_ref[...])
        dg[...] += _colsum8(dhp * xn)
        dxn = dhp * g_ref[...]
        dx = r_ref[...] + r * (dxn - xn * jnp.mean(dxn * xn, axis=-1, keepdims=True))
        dx_ref[...] = dx
        if with_gate:
            da_ref[...] = (dx * gt_ref[...]).astype(BF16)
            dgt[...] += _colsum8(dx * e_ref[...])

    f32o, p8 = jax.ShapeDtypeStruct((s, D), F32), jax.ShapeDtypeStruct((SUBLANES, D), F32)
    if with_gate:
        ins, in_specs = (dh, xin, g, scale, resid, extra, gate), [row, row, _vec(D), _vec(D), row, row, _vec(D)]
        out_specs, out_shape = [row, row, acc8, acc8, acc8, acc8], [f32o, jax.ShapeDtypeStruct((s, D), BF16), p8, p8, p8, p8]
    else:
        ins, in_specs = (dh, xin, g, scale, resid), [row, row, _vec(D), _vec(D), row]
        out_specs, out_shape = [row, acc8, acc8, acc8], [f32o, p8, p8, p8]
    return pl.pallas_call(
        body, name=name, grid=(s // ts,), in_specs=in_specs, out_specs=out_specs, out_shape=out_shape,
        compiler_params=_params("arbitrary"),
    )(*ins)


def _group_norm_bwd(t, dn_out, gvec, bd):
    r = _group_rsqrt(t, bd)
    dg_terms = dn_out * t * r
    dn = dn_out * gvec
    dt = r * (dn - t * (r * r) * (_split_dot(dn * t, bd) * (1.0 / HD)))
    return dt, dg_terms


def _mix_bwd(dmix, proj, o, pw, pb, ps, ag, bd, *, ts, name):
    s = proj.shape[0]
    hb = ts // HALO
    nb = s // HALO

    def body(dm_ref, dmn_ref, u_ref, uh_ref, o_ref, pw_ref, pb_ref, ps_ref, ag_ref, bd_ref,
             du_ref, do_ref, dpw_ref, dpb_ref, dps_ref, dag_ref):
        i = pl.program_id(0)
        n = s // ts

        @pl.when(i == 0)
        def _():
            for acc in (dpw_ref, dpb_ref, dps_ref, dag_ref):
                acc[...] = jnp.zeros_like(acc)

        for g, w in enumerate(POOL_WINDOWS):
            cols = slice(g * LANES, (g + 1) * LANES)
            wg = pw_ref[g]
            psg = ps_ref[:, cols]
            pooled = _pooled(u_ref[:, cols], uh_ref[:, cols], i, g, w, ts).astype(BF16)
            dy = dm_ref[:, cols]
            dps_ref[:, cols] += _colsum8(dy * (_dot(pooled, wg) + pb_ref[:, cols]))
            dpre = dy * psg
            dpb_ref[:, cols] += _colsum8(dpre)
            dpreb = dpre.astype(BF16)
            dpw_ref[g * LANES:(g + 1) * LANES, :] += _dot(pooled, dpreb, TN)
            dpool = _dot(dpreb, wg, NT)
            dnext = _dot((dmn_ref[:, cols] * psg).astype(BF16), wg, NT)
            dpe = jnp.concatenate([dpool, jnp.where(i < n - 1, dnext, 0.0)], axis=0)
            tpos = i * ts + lax.broadcasted_iota(jnp.int32, (ts + HALO, 1), 0)
            acc = dpe / jnp.minimum(tpos + 1, w).astype(F32)
            span = 1
            while span < w:
                acc = acc + _shift_rows(acc, -span)
                span *= 2
            du_ref[:, cols] = acc[:ts] - dpool
        ov = o_ref[...]
        dov, dg_terms = _group_norm_bwd(ov, dm_ref[:, DP:], ag_ref[...], bd_ref[...])
        do_ref[...] = dov
        dag_ref[...] += _colsum8(dg_terms)

    p8 = jax.ShapeDtypeStruct((SUBLANES, DP), F32)
    acc8 = pl.BlockSpec((SUBLANES, DP), lambda i: (0, 0))
    half = pl.BlockSpec((ts, DP), lambda i: (i, 0))
    return pl.pallas_call(
        body, name=name, grid=(s // ts,),
        in_specs=[pl.BlockSpec((ts, D), lambda i: (i, 0)),
                  pl.BlockSpec((HALO, DP), lambda i: (jnp.minimum((i + 1) * hb, nb - 1), 0)),
                  half, pl.BlockSpec((HALO, DP), lambda i: (jnp.maximum(i * hb - 1, 0), 0)),
                  half, _full((4, LANES, LANES)), _vec(DP), _vec(DP), _vec(DA), _full((DA, DA))],
        out_specs=[half, half, _full((DP, LANES)), acc8, acc8, acc8],
        out_shape=[jax.ShapeDtypeStruct((s, DP), F32), jax.ShapeDtypeStruct((s, DA), F32),
                   jax.ShapeDtypeStruct((DP, LANES), F32), p8, p8, p8],
        compiler_params=_params("arbitrary"),
    )(dmix, dmix, proj, proj, o, pw, pb, ps, ag, bd)


def _qk_norm_bwd(du, dq, dk, dv, proj, qg, kg, bd, *, ts, name):
    s = proj.shape[0]

    def body(du_ref, dq_ref, dk_ref, dv_ref, q_ref, k_ref, qg_ref, kg_ref, bd_ref, dp_ref, dqg_ref, dkg_ref):
        i = pl.program_id(0)

        @pl.when(i == 0)
        def _():
            dqg_ref[...] = jnp.zeros_like(dqg_ref)
            dkg_ref[...] = jnp.zeros_like(dkg_ref)

        bdv = bd_ref[...]
        dqr, tq = _group_norm_bwd(q_ref[...], dq_ref[...], qg_ref[...], bdv)
        dkr, tk = _group_norm_bwd(k_ref[...], dk_ref[...], kg_ref[...], bdv)
        dqg_ref[...] += _colsum8(tq)
        dkg_ref[...] += _colsum8(tk)
        dp_ref[:, 0:DP] = du_ref[...].astype(BF16)
        dp_ref[:, DP:DP + DA] = dqr.astype(BF16)
        dp_ref[:, DP + DA:DP + 2 * DA] = dkr.astype(BF16)
        dp_ref[:, DP + 2 * DA:] = dv_ref[...].astype(BF16)

    half = pl.BlockSpec((ts, DA), lambda i: (i, 0))
    col = lambda j: pl.BlockSpec((ts, DA), lambda i: (i, j))
    acc8 = pl.BlockSpec((SUBLANES, DA), lambda i: (0, 0))
    p8 = jax.ShapeDtypeStruct((SUBLANES, DA), F32)
    return pl.pallas_call(
        body, name=name, grid=(s // ts,),
        in_specs=[half, half, half, half, col(1), col(2), _vec(DA), _vec(DA), _full((DA, DA))],
        out_specs=[pl.BlockSpec((ts, DIN), lambda i: (i, 0)), acc8, acc8],
        out_shape=[jax.ShapeDtypeStruct((s, DIN), BF16), p8, p8],
        compiler_params=_params("arbitrary"),
    )(du, dq, dk, dv, proj, proj, qg, kg, bd)


def _split3(a):
    hi = a.astype(BF16)
    return hi, (a - hi.astype(F32)).astype(BF16)


def _dot3(a, b, dn):
    ah, al = _split3(a)
    bh, bl = _split3(b)
    return _dot(ah, bh, dn) + (_dot(ah, bl, dn) + _dot(al, bh, dn))


def _ada_fwd(c_all, w, b, name):
    nw = w.shape[1]

    def body(c_ref, w_ref, b_ref, o_ref):
        cv = c_ref[...]
        act = cv / (1.0 + jnp.exp(-cv))
        o_ref[...] = _dot3(act, w_ref[...], NN) + b_ref[...]

    return pl.pallas_call(
        body, name=name, in_specs=[_full((NDEV, D)), _full(w.shape), _full((1, nw))], out_specs=_full((NDEV, nw)),
        out_shape=jax.ShapeDtypeStruct((NDEV, nw), F32), grid=(1,), compiler_params=_params("arbitrary"),
    )(c_all, w, b)


def _ada_bwd(c_all, dmod, name):
    nw = dmod.shape[1]

    def body(c_ref, d_ref, o_ref):
        cv = c_ref[...]
        act = cv / (1.0 + jnp.exp(-cv))
        o_ref[...] = _dot3(act, d_ref[...], TN)[None]

    return pl.pallas_call(
        body, name=name, in_specs=[_full((NDEV, D)), _full((NDEV, nw))], out_specs=_full((1, D, nw)),
        out_shape=jax.ShapeDtypeStruct((1, D, nw), F32), grid=(1,), compiler_params=_params("arbitrary"),
    )(c_all, dmod)


def _fold_heads(v):
    acc = v[:, 0:HD]
    for h in range(1, DA // HD):
        acc = acc + v[:, h * HD:(h + 1) * HD]
    return acc


def _small_update(gathered, gathered_pw, gathered_cw, specs, params, name):
    names = [sp[0] for sp in specs]
    flat = []
    for nme in names + ["pool_w", "conv_w"]:
        flat += list(params[nme])
    n_in = len(flat)

    def body(*refs):
        ga_ref, gp_ref, gc_ref = refs[0], refs[1], refs[2]
        prm = refs[3:3 + n_in]
        outs = refs[3 + n_in:]
        per_dev = [jnp.sum(ga_ref[dv], axis=0, keepdims=True) for dv in range(NDEV)]
        total = per_dev[0]
        for dv in range(1, NDEV):
            total = total + per_dev[dv]
        k = 0
        for idx, (nme, off, width, fold) in enumerate(specs):
            g = total[:, off:off + width]
            if fold:
                g = _fold_heads(g)
            w_ref, m_ref, v_ref = prm[3 * idx:3 * idx + 3]
            d, nm, nv = _adamw_math(w_ref[...], g, m_ref[...], v_ref[...])
            for val in (g, d, nm, nv):
                outs[k][...] = val
                k += 1
        gpw = gp_ref[0]
        for dv in range(1, NDEV):
            gpw = gpw + gp_ref[dv]
        w_ref, m_ref, v_ref = prm[3 * len(specs):3 * len(specs) + 3]
        d, nm, nv = _adamw_math(w_ref[...], gpw, m_ref[...], v_ref[...])
        for val in (gpw, d, nm, nv):
            outs[k][...] = val
            k += 1
        gcw = gc_ref[0]
        for dv in range(1, NDEV):
            gcw = gcw + gc_ref[dv]
        w_ref, m_ref, v_ref = prm[3 * len(specs) + 3:3 * len(specs) + 6]
        for tap in range(3):
            row = slice(tap, tap + 1)
            g = jnp.sum(gcw[SUBLANES * tap:SUBLANES * (tap + 1)], axis=0, keepdims=True)
            d, nm, nv = _adamw_math(w_ref[row, :], g, m_ref[row, :], v_ref[row, :])
            for q, val in enumerate((g, d, nm, nv)):
                outs[k + q][row, :] = val
        k += 4
        for dv in range(NDEV):
            outs[k][dv:dv + 1, :] = per_dev[dv][:, 0:6 * D]

    out_shape, out_specs = [], []
    for nme in names + ["pool_w", "conv_w"]:
        shp = params[nme][0].shape
        out_shape += [jax.ShapeDtypeStruct(shp, F32)] * 4
        out_specs += [_full(shp)] * 4
    out_shape.append(jax.ShapeDtypeStruct((NDEV, 6 * D), F32))
    out_specs.append(_full((NDEV, 6 * D)))
    res = pl.pallas_call(
        body, name=name, grid=(1,),
        in_specs=[_full(gathered.shape), _full(gathered_pw.shape), _full(gathered_cw.shape)] + [_full(a.shape) for a in flat],
        out_specs=out_specs, out_shape=out_shape, compiler_params=_params("arbitrary"),
    )(gathered, gathered_pw, gathered_cw, *flat)
    out = {nme: tuple(res[4 * i:4 * i + 4]) for i, nme in enumerate(names + ["pool_w", "conv_w"])}
    return out, res[-1]


def _row_tile(s):
    return 512 if s % 512 == 0 else s


def kernel(x, c, ada_w, ada_b, norm1_g, w_in, pool_w, pool_b, pool_scale, q_norm_g, k_norm_g, attn_out_g, w_out, norm2_g, w_up, conv_w, conv_b, w_down, loss_target, m_ada_w, m_ada_b, m_norm1_g, m_w_in, m_pool_w, m_pool_b, m_pool_scale, m_q_norm_g, m_k_norm_g, m_attn_out_g, m_w_out, m_norm2_g, m_w_up, m_conv_w, m_conv_b, m_w_down, v_ada_w, v_ada_b, v_norm1_g, v_w_in, v_pool_w, v_pool_b, v_pool_scale, v_q_norm_g, v_k_norm_g, v_attn_out_g, v_w_out, v_norm2_g, v_w_up, v_conv_w, v_conv_b, v_w_down):
    ax, ay, ac = lax.axis_index("x"), lax.axis_index("y"), lax.axis_index("c")
    me = 4 * ax + 2 * ay + ac
    me_swapped = 4 * ay + 2 * ax + ac
    xs, tgt = x[0], loss_target[0]
    s = xs.shape[0]
    ts = _row_tile(s)
    tq_attn, tk_attn = 512, 256
    bd = _block_diag_ones(DA, HD)

    c_all = _all_gather([jnp.broadcast_to(c, (SUBLANES, D))], [False], "gather_c")[0][:, 0, :]
    n_ada = ada_w.shape[2]
    ada_b_mine = lax.dynamic_slice_in_dim(ada_b, me * n_ada, n_ada, axis=1)
    mod_part = _ada_fwd(c_all, ada_w[0], ada_b_mine, "ada_fwd")
    mod_all = _all_gather([mod_part], [False], "gather_mod")[0]
    mod = lax.dynamic_index_in_dim(mod_all, me, axis=1, keepdims=False).reshape(1, 6 * D)
    shift1, scale1, gate1, shift2, scale2, gate2 = [mod[:, k * D:(k + 1) * D] for k in range(6)]

    w_in_t = w_in[0].T.astype(BF16)
    w_up_t = w_up[0].T.astype(BF16)
    gw_in, gw_out, gw_up, gw_down = _all_gather(
        [w_in_t, w_out[0].astype(BF16), w_up_t, w_down[0].astype(BF16)], [False, False, True, False], "gather_w")
    w_in_full = gw_in.reshape(DIN, D)
    w_out_full = gw_out.reshape(D, D)
    w_up_full = gw_up.reshape(2 * DFF, D)
    w_down_full = gw_down.reshape(DFF, D)
    gcw = _all_gather([jnp.pad(conv_w[0], ((0, 5), (0, 64)))], [True], "gather_conv")[0]
    cw_full = jnp.transpose(gcw[:, :3, :704], (1, 0, 2)).reshape(3, 2 * DFF)
    cb_full = jnp.transpose(conv_b.reshape(1, 2, 2, 2, 704), (0, 2, 1, 3, 4)).reshape(1, 2 * DFF)

    qg = jnp.tile(q_norm_g, (1, DA // HD))
    kg = jnp.tile(k_norm_g, (1, DA // HD))
    ag = attn_out_g.reshape(1, DA)
    pw = pool_w[0].astype(BF16)
    pb = pool_b.reshape(1, DP)
    h1 = _ln_mod(xs, norm1_g, scale1, shift1, ts=ts, name="ln1")
    proj = _matmul(h1, w_in_full, mode="nt", out_dtype=F32, tm=ts, tn=DIN, tk=D, name="in_proj")
    qkv = _qk_norm(proj, qg, kg, bd, ts=ts, name="qk_norm")
    o_raw, m_tot, kb_first = _attn_fwd(qkv, tq=tq_attn, tk=tk_attn, name="attn_fwd")
    mix = _pool_mix(proj, o_raw, pw, pb, pool_scale, ag, bd, ts=ts, name="pool_mix")
    att = _matmul(mix, w_out_full, mode="nn", out_dtype=F32, tm=ts, tn=D, tk=D, name="out_proj")
    x1, h2 = _res_ln_mod(xs, att, gate1, norm2_g, scale2, shift2, ts=ts, name="res_ln2")
    up = _matmul(h2, w_up_full, mode="nt", out_dtype=F32, tm=ts, tn=CF, tk=D, name="up_proj", n_outer=True)
    act = _conv_gate(up, cw_full, cb_full, ts=ts // 2, name="conv_gate")
    ffn = _matmul(act, w_down_full, mode="nn", out_dtype=F32, tm=ts, tn=D, tk=DFF, name="down_proj")
    dy, dffn, dgate2_p, loss_p = _loss_head(x1, ffn, tgt, gate2, ts=ts, name="loss_head")
    loss = lax.psum(loss_p[0, 0], ("x", "y", "c"))

    da = _matmul(dffn, w_down_full, mode="nt", out_dtype=F32, tm=ts, tn=CF, tk=D, name="down_bwd")
    g_w_down = _matmul(act, dffn, mode="tn", out_dtype=F32, tm=CF, tn=D, tk=ts, name="down_wgrad")
    dconv, dcb_p = _gate_bwd(da, up, cw_full, cb_full, ts=ts // 2, name="gate_bwd")
    dup, dcw_p = _conv_bwd(dconv, up, cw_full, ts=ts, tc=CF, name="conv_bwd")
    dh2 = _matmul(dup, w_up_full, mode="nn", out_dtype=F32, tm=ts, tn=D, tk=CF, name="up_bwd")
    g_w_up_t = _matmul(dup, h2, mode="tn", out_dtype=F32, tm=CF, tn=D, tk=ts, name="up_wgrad")
    dx1, datt, dshift2_p, dscale2_p, dnorm2_p, dgate1_p = _ln_mod_bwd(
        dh2, x1, norm2_g, scale2, dy, att, gate1, ts=ts, name="ln2_bwd")

    dmix = _matmul(datt, w_out_full, mode="nt", out_dtype=F32, tm=ts, tn=D, tk=D, name="out_bwd")
    g_w_out = _matmul(mix, datt, mode="tn", out_dtype=F32, tm=D, tn=D, tk=ts, name="out_wgrad")
    du, do_raw, g_pw_p, dpb_p, dps_p, dag_p = _mix_bwd(dmix, proj, o_raw, pw, pb, pool_scale, ag, bd, ts=ts, name="mix_bwd")
    dqn, dkn, dvv = _attn_bwd(qkv, do_raw, m_tot, kb_first, tq=tq_attn, tk=tk_attn, name="attn_bwd")
    dproj, dqg_p, dkg_p = _qk_norm_bwd(du, dqn, dkn, dvv, proj, qg, kg, bd, ts=ts, name="qk_norm_bwd")
    dh1 = _matmul(dproj, w_in_full, mode="nn", out_dtype=F32, tm=ts, tn=D, tk=DIN, name="in_bwd")
    g_w_in_t = _matmul(dproj, h1, mode="tn", out_dtype=F32, tm=DIN // 2, tn=D, tk=ts, name="in_wgrad")
    grad_x, dshift1_p, dscale1_p, dnorm1_p = _ln_mod_bwd(dh1, xs, norm1_g, scale1, dx1, None, None, ts=ts, name="ln1_bwd")

    big = [g_w_in_t.reshape(NDEV, DIN // NDEV, D), g_w_out.reshape(NDEV, D // NDEV, D),
           g_w_up_t.reshape(NDEV, 2 * DFF // NDEV, D), g_w_down.reshape(NDEV, DFF // NDEV, D)]
    swaps = [False, False, True, False]
    core = jnp.reshape(ac, (1,)).astype(jnp.int32)
    chip = jnp.reshape(2 * ax + ay, (1,)).astype(jnp.int32)
    gots = _pair_exchange(big, swaps, "rs_pair")
    sums = [_pair_sum(big[k], gots[k], swaps[k], core, "rs_pair_sum%d" % k) for k in range(4)]
    parts = _chip_exchange(sums, "rs_chip")
    tr = lambda a: a[0].T
    r_in = _adamw_reduce(tr(w_in), tr(m_w_in), tr(v_w_in), sums[0], parts[0], chip, "adamw_w_in")
    r_out = _adamw_reduce(w_out[0], m_w_out[0], v_w_out[0], sums[1], parts[1], chip, "adamw_w_out")
    r_up = _adamw_reduce(tr(w_up), tr(m_w_up), tr(v_w_up), sums[2], parts[2], chip, "adamw_w_up")
    r_down = _adamw_reduce(w_down[0], m_w_down[0], v_w_down[0], sums[3], parts[3], chip, "adamw_w_down")
    r_in = [a.T[None] for a in r_in]
    r_up = [a.T[None] for a in r_up]
    r_out = [a[None] for a in r_out]
    r_down = [a[None] for a in r_down]

    dcb_nat = jnp.transpose(dcb_p.reshape(SUBLANES, 2, 2, 2, 704), (0, 2, 1, 3, 4)).reshape(SUBLANES, 2 * DFF)
    pieces = [dshift1_p, dscale1_p, dgate1_p, dshift2_p, dscale2_p, dgate2_p,
              dnorm1_p, dnorm2_p, dcb_nat, dpb_p, dps_p, dag_p, dqg_p, dkg_p]
    packed = jnp.concatenate(pieces, axis=1)
    gathered, gathered_pw, gathered_cw = _all_gather([packed, g_pw_p, dcw_p], [False, False, False], "gather_small")
    gathered_cw = lax.dynamic_index_in_dim(gathered_cw.reshape(NDEV, 3 * SUBLANES, NDEV, 704), me_swapped, axis=2, keepdims=False)
    specs = [("ada_b", 0, 6 * D, False)]
    off = 6 * D
    for nme, width, fold in (("norm1_g", D, False), ("norm2_g", D, False), ("conv_b", 2 * DFF, False),
                             ("pool_b", DP, False), ("pool_scale", DP, False), ("attn_out_g", DA, False),
                             ("q_norm_g", DA, True), ("k_norm_g", DA, True)):
        specs.append((nme, off, width, fold))
        off += width
    small = {
        "ada_b": (ada_b, m_ada_b, v_ada_b),
        "norm1_g": (norm1_g, m_norm1_g, v_norm1_g), "norm2_g": (norm2_g, m_norm2_g, v_norm2_g),
        "conv_b": (conv_b, m_conv_b, v_conv_b),
        "pool_b": (pb, m_pool_b.reshape(1, DP), v_pool_b.reshape(1, DP)),
        "pool_scale": (pool_scale, m_pool_scale, v_pool_scale),
        "attn_out_g": (ag, m_attn_out_g.reshape(1, DA), v_attn_out_g.reshape(1, DA)),
        "q_norm_g": (q_norm_g, m_q_norm_g, v_q_norm_g), "k_norm_g": (k_norm_g, m_k_norm_g, v_k_norm_g),
        "pool_w": (pool_w.reshape(DP, LANES), m_pool_w.reshape(DP, LANES), v_pool_w.reshape(DP, LANES)),
        "conv_w": (conv_w[0], m_conv_w[0], v_conv_w[0]),
    }
    upd, dmod_all = _small_update(gathered, gathered_pw, gathered_cw, specs, small, "small_update")
    g_ada_w = _ada_bwd(c_all, lax.dynamic_slice_in_dim(dmod_all, me * n_ada, n_ada, axis=1), "ada_bwd")
    r_ada = [g_ada_w] + [a[None] for a in _adamw(ada_w[0], m_ada_w[0], v_ada_w[0], g_ada_w[0], "adamw_ada_w")]

    shapes = {"ada_b": ada_b.shape, "norm1_g": norm1_g.shape, "pool_w": pool_w.shape, "pool_b": pool_b.shape,
              "pool_scale": pool_scale.shape, "q_norm_g": q_norm_g.shape, "k_norm_g": k_norm_g.shape,
              "attn_out_g": attn_out_g.shape, "norm2_g": norm2_g.shape, "conv_w": conv_w.shape, "conv_b": conv_b.shape}
    res = {nme: [a.reshape(shapes[nme]) for a in upd[nme]] for nme in shapes}
    res.update(ada_w=r_ada, w_in=r_in, w_out=r_out, w_up=r_up, w_down=r_down)
    names = ["ada_w", "ada_b", "norm1_g", "w_in", "pool_w", "pool_b", "pool_scale", "q_norm_g", "k_norm_g",
             "attn_out_g", "w_out", "norm2_g", "w_up", "conv_w", "conv_b", "w_down"]
    outs = [loss, grad_x[None]]
    for q in range(4):
        outs += [res[nme][q] for nme in names]
    return tuple(outs)
```

```python
import functools
import math

import numpy as np
import jax
import jax.numpy as jnp
from jax import lax
from jax.experimental import pallas as pl
from jax.experimental.pallas import tpu as pltpu

F32, BF16 = jnp.float32, jnp.bfloat16
D = 1024
DP = 512
DA = 512
HD = 64
DIN = DP + 3 * DA
DFF = 2816
POOL_WINDOWS = (2, 4, 8, 16)
HALO = 16
EPS = 1e-6
LANES = 128
SUBLANES = 8
NDEV = 8
VMEM_LIMIT = 56 * 1024 * 1024
MESH = pl.DeviceIdType.MESH

ADAM_LR, ADAM_B1, ADAM_B2, ADAM_EPS, ADAM_WD, ADAM_STEP = 0.001, 0.9, 0.999, 1e-08, 0.01, 10

NN = (((1,), (0,)), ((), ()))
NT = (((1,), (1,)), ((), ()))
TN = (((0,), (0,)), ((), ()))


def _params(*sem):
    return pltpu.CompilerParams(dimension_semantics=sem, vmem_limit_bytes=VMEM_LIMIT)


def _full(shape):
    nd = len(shape)
    return pl.BlockSpec(shape, lambda *_: (0,) * nd)


def _dot(a, b, dn=NN):
    return lax.dot_general(a, b, dn, preferred_element_type=F32)


def _split_dot(a, b, dn=NN):
    hi = a.astype(BF16)
    lo = (a - hi.astype(F32)).astype(BF16)
    return _dot(hi, b, dn) + _dot(lo, b, dn)


def _colsum8(v):
    r, n = v.shape
    return v.reshape(r // SUBLANES, SUBLANES, n).sum(axis=0)


def _block_diag_ones(n, blk):
    i = np.arange(n) // blk
    return jnp.asarray((i[:, None] == i[None, :]).astype(np.float32), BF16)


def _matmul(a, b, *, mode, out_dtype, tm, tn, tk, name, n_outer=False):
    if mode == "tn":
        K, M = a.shape
        N = b.shape[1]
    elif mode == "nt":
        M, K = a.shape
        N = b.shape[0]
    else:
        M, K = a.shape
        N = b.shape[1]
    tm, tn, tk = min(tm, M), min(tn, N), min(tk, K)
    assert M % tm == 0 and N % tn == 0 and K % tk == 0, (name, M, N, K, tm, tn, tk)
    nk = K // tk
    dn = {"nn": NN, "nt": NT, "tn": TN}[mode]

    def body(a_ref, b_ref, o_ref, *acc):
        if nk == 1:
            o_ref[...] = _dot(a_ref[...], b_ref[...], dn).astype(o_ref.dtype)
            return
        acc_ref, = acc
        k = pl.program_id(2)

        @pl.when(k == 0)
        def _():
            acc_ref[...] = jnp.zeros_like(acc_ref)

        acc_ref[...] += _dot(a_ref[...], b_ref[...], dn)

        @pl.when(k == nk - 1)
        def _():
            o_ref[...] = acc_ref[...].astype(o_ref.dtype)

    if n_outer:
        gi = lambda g: (g[1], g[0], g[2])
        grid = (N // tn, M // tm, nk)
    else:
        gi = lambda g: g
        grid = (M // tm, N // tn, nk)

    def amap(*g):
        i, j, k = gi(g)
        return (k, i) if mode == "tn" else (i, k)

    def bmap(*g):
        i, j, k = gi(g)
        return (j, k) if mode == "nt" else (k, j)

    def omap(*g):
        i, j, k = gi(g)
        return (i, j)

    a_blk = (tk, tm) if mode == "tn" else (tm, tk)
    b_blk = (tn, tk) if mode == "nt" else (tk, tn)
    return pl.pallas_call(
        body, name=name, grid=grid,
        in_specs=[pl.BlockSpec(a_blk, amap), pl.BlockSpec(b_blk, bmap)],
        out_specs=pl.BlockSpec((tm, tn), omap),
        out_shape=jax.ShapeDtypeStruct((M, N), out_dtype),
        scratch_shapes=[] if nk == 1 else [pltpu.VMEM((tm, tn), F32)],
        compiler_params=_params("parallel", "parallel", "arbitrary"),
    )(a, b)


def _slot(swap, px, py, pc):
    return 4 * py + 2 * px + pc if swap else 4 * px + 2 * py + pc


class _Gather:
    def __init__(self, ins, outs, send, recv, loc, swaps):
        self.ins, self.outs, self.send, self.recv, self.loc, self.swaps = ins, outs, send, recv, loc, swaps
        x, y, c = lax.axis_index("x"), lax.axis_index("y"), lax.axis_index("c")
        self.me, self.sib = (x, y, c), (x, y, 1 - c)
        self.chips = [(1 - x, y), (x, 1 - y), (1 - x, 1 - y)]
        self.n = len(ins)

    @staticmethod
    def scratch(n):
        return [pltpu.SemaphoreType.DMA((7 * n,)), pltpu.SemaphoreType.DMA((7 * n,)), pltpu.SemaphoreType.DMA((n,))]

    def copy(self, a, k, blk, to, src=None):
        rows = self.outs[a].at[_slot(self.swaps[a], *blk)]
        return pltpu.make_async_remote_copy(
            src_ref=rows if src is None else src, dst_ref=rows,
            send_sem=self.send.at[7 * a + k], recv_sem=self.recv.at[7 * a + k], device_id=to, device_id_type=MESH)

    def mine(self, a):
        return pltpu.make_async_copy(self.ins[a], self.outs[a].at[_slot(self.swaps[a], *self.me)], self.loc.at[a])

    def first(self, a):
        c = self.me[2]
        return [self.copy(a, 0, self.me, self.sib, src=self.ins[a])] + [
            self.copy(a, 1 + j, self.me, (*chip, c), src=self.ins[a]) for j, chip in enumerate(self.chips)]

    def forwards(self, a):
        c = self.me[2]
        return [self.copy(a, 4 + j, (*chip, c), self.sib) for j, chip in enumerate(self.chips)]

    def start(self):
        for a in range(self.n):
            self.mine(a).start()
        for a in range(self.n):
            for cp in self.first(a):
                cp.start()

    def forward(self):
        c = self.me[2]
        for a in range(self.n):
            fwd = self.forwards(a)
            for j, chip in enumerate(self.chips):
                self.copy(a, 1 + j, (*chip, c), self.me).wait_recv()
                fwd[j].start()

    def finish(self):
        c = self.me[2]
        for a in range(self.n):
            self.copy(a, 0, self.sib, self.me).wait_recv()
            for j, chip in enumerate(self.chips):
                self.copy(a, 4 + j, (*chip, 1 - c), self.me).wait_recv()
        for a in range(self.n):
            for cp in self.first(a) + self.forwards(a):
                cp.wait_send()
            self.mine(a).wait()


def _all_gather(arrs, swaps, name):
    n = len(arrs)

    def body(*refs):
        g = _Gather(refs[:n], refs[n:2 * n], *refs[2 * n:], swaps)
        g.start()
        g.forward()
        g.finish()

    any_spec = pl.BlockSpec(memory_space=pl.ANY)
    return pl.pallas_call(
        body, name=name,
        in_specs=[any_spec] * n, out_specs=[any_spec] * n,
        out_shape=[jax.ShapeDtypeStruct((NDEV,) + a.shape, a.dtype) for a in arrs],
        scratch_shapes=_Gather.scratch(n),
    )(*arrs)


def _pair_exchange(arrs, swaps, name):
    n = len(arrs)

    def body(*refs):
        ins, gots = refs[:n], refs[n:2 * n]
        send, recv = refs[2 * n:]
        x, y, c = lax.axis_index("x"), lax.axis_index("y"), lax.axis_index("c")
        sib = (x, y, 1 - c)
        rems = []
        for a in range(n):
            for k in range(4):
                kx, ky = k // 2, k % 2
                rc = pltpu.make_async_remote_copy(
                    src_ref=ins[a].at[_slot(swaps[a], kx, ky, 1 - c)], dst_ref=gots[a].at[k],
                    send_sem=send.at[4 * a + k], recv_sem=recv.at[4 * a + k], device_id=sib, device_id_type=MESH)
                rc.start()
                rems.append(rc)
        for rc in rems:
            rc.wait_recv()
        for rc in rems:
            rc.wait_send()

    any_spec = pl.BlockSpec(memory_space=pl.ANY)
    return pl.pallas_call(
        body, name=name,
        in_specs=[any_spec] * n, out_specs=[any_spec] * n,
        out_shape=[jax.ShapeDtypeStruct((4,) + a.shape[1:], a.dtype) for a in arrs],
        scratch_shapes=[pltpu.SemaphoreType.DMA((4 * n,)), pltpu.SemaphoreType.DMA((4 * n,))],
    )(*arrs)


def _chip_copies(ins, outs, send, recv):
    x, y, c = lax.axis_index("x"), lax.axis_index("y"), lax.axis_index("c")
    chips = [(1 - x, y), (x, 1 - y), (1 - x, 1 - y)]
    return [pltpu.make_async_remote_copy(
        src_ref=ins[a].at[2 * px + py], dst_ref=outs[a].at[j], send_sem=send.at[3 * a + j], recv_sem=recv.at[3 * a + j],
        device_id=(px, py, c), device_id_type=MESH) for a in range(len(ins)) for j, (px, py) in enumerate(chips)]


def _chip_exchange(arrs, name):
    n = len(arrs)

    def body(*refs):
        rems = _chip_copies(refs[:n], refs[n:2 * n], *refs[2 * n:])
        for rc in rems:
            rc.start()
        for rc in rems:
            rc.wait_recv()
        for rc in rems:
            rc.wait_send()

    any_spec = pl.BlockSpec(memory_space=pl.ANY)
    return pl.pallas_call(
        body, name=name,
        in_specs=[any_spec] * n, out_specs=[any_spec] * n,
        out_shape=[jax.ShapeDtypeStruct((3,) + a.shape[1:], a.dtype) for a in arrs],
        scratch_shapes=[pltpu.SemaphoreType.DMA((3 * n,)), pltpu.SemaphoreType.DMA((3 * n,))],
    )(*arrs)


def _pair_sum(grads, got, swap, core, name):
    _, r, c = got.shape
    tr = r if r <= 352 else r // 2

    def own_map(k, i, core_ref):
        return (_slot(swap, k // 2, k % 2, core_ref[0]), i, 0)

    def body(core_ref, a_ref, b_ref, o_ref):
        o_ref[...] = a_ref[...] + b_ref[...]

    spec = pl.BlockSpec((None, tr, c), lambda k, i, core_ref: (k, i, 0))
    return pl.pallas_call(
        body, name=name,
        grid_spec=pltpu.PrefetchScalarGridSpec(
            num_scalar_prefetch=1, grid=(4, r // tr),
            in_specs=[pl.BlockSpec((None, tr, c), own_map), spec], out_specs=spec),
        out_shape=jax.ShapeDtypeStruct(got.shape, got.dtype), compiler_params=_params("parallel", "parallel"),
    )(core, grads, got)


def _adamw_math(w, g, m, v):
    m = ADAM_B1 * m + (1.0 - ADAM_B1) * g
    v = ADAM_B2 * v + (1.0 - ADAM_B2) * (g * g)
    m_hat = m / (1.0 - ADAM_B1 ** ADAM_STEP)
    v_hat = v / (1.0 - ADAM_B2 ** ADAM_STEP)
    delta = -ADAM_LR * (m_hat / (jnp.sqrt(v_hat) + ADAM_EPS) + ADAM_WD * w)
    return delta, m, v


def _adamw_tile(r):
    for cand in (256, 352, 128):
        if r % cand == 0:
            return cand
    return r


def _adamw(w, m, v, g, name):
    r, c = w.shape
    tr = _adamw_tile(r)
    spec = pl.BlockSpec((tr, c), lambda i: (i, 0))

    def body(w_ref, m_ref, v_ref, g_ref, d_ref, nm_ref, nv_ref):
        d_ref[...], nm_ref[...], nv_ref[...] = _adamw_math(w_ref[...], g_ref[...], m_ref[...], v_ref[...])

    out = jax.ShapeDtypeStruct((r, c), F32)
    return pl.pallas_call(
        body, name=name, grid=(r // tr,), in_specs=[spec] * 4, out_specs=[spec] * 3, out_shape=[out] * 3,
        compiler_params=_params("parallel"),
    )(w, m, v, g)


def _adamw_reduce(w, m, v, sums, recv, chip, name):
    r, c = w.shape
    tr = _adamw_tile(r)
    spec = pl.BlockSpec((tr, c), lambda i, chip_ref: (i, 0))

    def body(chip_ref, w_ref, m_ref, v_ref, s_ref, p_ref, g_ref, d_ref, nm_ref, nv_ref):
        g = ((s_ref[...] + p_ref[0]) + p_ref[1]) + p_ref[2]
        g_ref[...] = g
        d_ref[...], nm_ref[...], nv_ref[...] = _adamw_math(w_ref[...], g, m_ref[...], v_ref[...])

    out = jax.ShapeDtypeStruct((r, c), F32)
    return pl.pallas_call(
        body, name=name,
        grid_spec=pltpu.PrefetchScalarGridSpec(
            num_scalar_prefetch=1, grid=(r // tr,),
            in_specs=[spec, spec, spec, pl.BlockSpec((None, tr, c), lambda i, chip_ref: (chip_ref[0], i, 0)),
                      pl.BlockSpec((3, tr, c), lambda i, chip_ref: (0, i, 0))],
            out_specs=[spec] * 4),
        out_shape=[out] * 4, compiler_params=_params("parallel"),
    )(chip, w, m, v, sums, recv)


def _vec(n):
    return pl.BlockSpec((1, n), lambda *_: (0, 0))


def _ln_mod(x, g, scale, shift, *, ts, name):
    s = x.shape[0]
    row = pl.BlockSpec((ts, D), lambda i: (i, 0))

    def body(x_ref, g_ref, sc_ref, sh_ref, h_ref):
        xv = x_ref[...]
        r = lax.rsqrt(jnp.mean(xv * xv, axis=-1, keepdims=True) + EPS)
        h = (xv * r) * g_ref[...]
        h_ref[...] = (h * (1.0 + sc_ref[...]) + sh_ref[...]).astype(BF16)

    return pl.pallas_call(
        body, name=name, grid=(s // ts,), in_specs=[row, _vec(D), _vec(D), _vec(D)], out_specs=row,
        out_shape=jax.ShapeDtypeStruct((s, D), BF16), compiler_params=_params("parallel"),
    )(x, g, scale, shift)


def _group_rsqrt(t, bd):
    return lax.rsqrt(_split_dot(t * t, bd) * (1.0 / HD) + EPS)


def _qk_norm(proj, qg, kg, bd, *, ts, name):
    s = proj.shape[0]

    def body(q_ref, k_ref, v_ref, qg_ref, kg_ref, bd_ref, o_ref):
        bdv = bd_ref[...]
        q, k = q_ref[...], k_ref[...]
        o_ref[:, 0:DA] = (q * _group_rsqrt(q, bdv) * qg_ref[...]).astype(BF16)
        o_ref[:, DA:2 * DA] = (k * _group_rsqrt(k, bdv) * kg_ref[...]).astype(BF16)
        o_ref[:, 2 * DA:] = v_ref[...].astype(BF16)

    col = lambda j: pl.BlockSpec((ts, DA), lambda i: (i, j))
    return pl.pallas_call(
        body, name=name, grid=(s // ts,),
        in_specs=[col(1), col(2), col(3), _vec(DA), _vec(DA), _full((DA, DA))],
        out_specs=pl.BlockSpec((ts, 3 * DA), lambda i: (i, 0)),
        out_shape=jax.ShapeDtypeStruct((s, 3 * DA), BF16), compiler_params=_params("parallel"),
    )(proj, proj, proj, qg, kg, bd)


EXP_UNDERFLOW = -120.0


def _log_terms(z):
    neg_abs = lax.bitcast_convert_type(lax.bitcast_convert_type(z, jnp.uint32) | jnp.uint32(0x80000000), F32)
    b = jnp.minimum(z, 0.0) - jnp.log(1.0 + jnp.exp(neg_abs))
    return b, b - z


def _head_masks(rows):
    lane = lax.broadcasted_iota(jnp.int32, (rows, LANES), 1)
    return [lane < HD, lane >= HD]


def _attn_fwd(qkv, gather, swaps, *, tq, tk, name):
    s = qkv.shape[0]
    nrep = tk // LANES
    ndiag = tq // tk
    ng = len(gather)
    npair, nq = DA // LANES, s // tq

    def body(*refs):
        q_ref, k_ref, v_ref = refs[:3]
        g_in = refs[3:3 + ng]
        o_ref, tot_ref, first_ref = refs[3 + ng:6 + ng]
        g_out = refs[6 + ng:6 + 2 * ng]
        oacc, rc = refs[6 + 2 * ng:8 + 2 * ng]
        g_sems = refs[8 + 2 * ng:]
        i = pl.program_id(1)
        step_id = pl.program_id(0) * nq + i

        @pl.when(step_id == 0)
        def _():
            _Gather(g_in, g_out, *g_sems, swaps).start()

        @pl.when(step_id == (npair * nq * 3) // 4)
        def _():
            _Gather(g_in, g_out, *g_sems, swaps).forward()

        heads = _head_masks(tq)
        q = q_ref[...] * 0.125
        qs = [jnp.where(h, q, 0.0).astype(BF16) for h in heads]
        dif = lax.broadcasted_iota(jnp.int32, (tq, tk), 0) - lax.broadcasted_iota(jnp.int32, (tq, tk), 1)
        kr = lax.broadcasted_iota(jnp.int32, (tk, tk), 0)
        kc = lax.broadcasted_iota(jnp.int32, (tk, tk), 1)
        later = jnp.where(kr > kc, 1.0, 0.0).astype(BF16)
        oacc[...] = jnp.zeros_like(oacc)
        rc[...] = jnp.zeros_like(rc)

        def tile(kb, thr):
            rows = pl.ds(pl.multiple_of(kb * tk, tk), tk)
            k = k_ref[rows, :]
            v = v_ref[rows, :]
            rcv = [rc[0], rc[1]]
            zs = [_dot(qs[a], k, NT) for a in range(2)]
            bs, mbs = [], []
            for a in range(2):
                b, m = _log_terms(zs[a])
                if thr is not None:
                    m = jnp.where(dif > thr, m, 0.0)
                bs.append(b)
                mbs.append(m.astype(BF16))
            rl = [_dot(mbs[a], later) for a in range(2)]
            for a in range(2):
                p = jnp.exp(bs[a] + (rl[a] + jnp.tile(rcv[a], (1, nrep))))
                if thr is not None:
                    p = jnp.where(dif > thr, p, 0.0)
                oacc[a] += _dot(p.astype(BF16), v)
                rc[a] = rcv[a] + (rl[a][:, 0:1] + mbs[a][:, 0:1].astype(F32))

        for d in reversed(range(ndiag)):
            tile(i * ndiag + d, d * tk)

        def live():
            return jnp.max(jnp.maximum(rc[0], rc[1])) > EXP_UNDERFLOW

        def step(carry):
            kb, _ = carry
            tile(kb, None)
            return kb - 1, live()

        kb_end, _ = lax.while_loop(lambda cr: jnp.logical_and(cr[0] >= 0, cr[1]), step, (i * ndiag - 1, live()))
        first_ref[pl.program_id(0), i] = (kb_end + 1).astype(F32)
        o_ref[...] = jnp.where(heads[0], oacc[0], oacc[1])
        tot_ref[...] = jnp.where(heads[0], rc[0], rc[1])

        @pl.when(step_id == npair * nq - 1)
        def _():
            _Gather(g_in, g_out, *g_sems, swaps).finish()

    qspec = pl.BlockSpec((tq, LANES), lambda p, i: (i, p))
    any_spec = pl.BlockSpec(memory_space=pl.ANY)
    res = pl.pallas_call(
        body, name=name, grid=(npair, nq),
        in_specs=[qspec,
                  pl.BlockSpec((s, LANES), lambda p, i: (0, DA // LANES + p)),
                  pl.BlockSpec((s, LANES), lambda p, i: (0, 2 * DA // LANES + p))] + [any_spec] * ng,
        out_specs=[qspec, qspec, pl.BlockSpec(memory_space=pltpu.SMEM)] + [any_spec] * ng,
        out_shape=[jax.ShapeDtypeStruct((s, DA), F32), jax.ShapeDtypeStruct((s, DA), F32),
                   jax.ShapeDtypeStruct((npair, nq), F32)]
        + [jax.ShapeDtypeStruct((NDEV,) + a.shape, a.dtype) for a in gather],
        scratch_shapes=[pltpu.VMEM((2, tq, LANES), F32), pltpu.VMEM((2, tq, LANES), F32)] + _Gather.scratch(ng),
        compiler_params=_params("arbitrary", "arbitrary"),
    )(qkv, qkv, qkv, *gather)
    return res[0], res[1], res[2], res[3:]


def _attn_bwd(qkv, do, tot, first, exchange, *, tq, tk, name):
    s = qkv.shape[0]
    nrep = tk // LANES
    ndiag = tq // tk
    ne = len(exchange)
    npair, nq = DA // LANES, s // tq

    def body(*refs):
        q_ref, k_ref, v_ref, do_ref, tot_ref, first_ref = refs[:6]
        e_in = refs[6:6 + ne]
        dq_ref, dk_ref, dv_ref = refs[6 + ne:9 + ne]
        e_out = refs[9 + ne:9 + 2 * ne]
        dqacc, rem, gc = refs[9 + 2 * ne:12 + 2 * ne]
        e_sems = refs[12 + 2 * ne:]
        i = pl.program_id(1)
        step_id = pl.program_id(0) * nq + i

        @pl.when(step_id == 0)
        def _():
            for cp in _chip_copies(e_in, e_out, *e_sems):
                cp.start()

        @pl.when(i == 0)
        def _():
            dk_ref[...] = jnp.zeros_like(dk_ref)
            dv_ref[...] = jnp.zeros_like(dv_ref)

        heads = _head_masks(tq)
        q = q_ref[...] * 0.125
        qs = [jnp.where(h, q, 0.0).astype(BF16) for h in heads]
        dov = do_ref[...]
        dob = [jnp.where(h, dov, 0.0).astype(BF16) for h in heads]
        dif = lax.broadcasted_iota(jnp.int32, (tq, tk), 0) - lax.broadcasted_iota(jnp.int32, (tq, tk), 1)
        kr = lax.broadcasted_iota(jnp.int32, (tk, tk), 0)
        kc = lax.broadcasted_iota(jnp.int32, (tk, tk), 1)
        up_incl = jnp.where(kr <= kc, 1.0, 0.0).astype(BF16)
        up_strict = jnp.where(kr < kc, 1.0, 0.0).astype(BF16)
        dqacc[...] = jnp.zeros_like(dqacc)
        gc[...] = jnp.zeros_like(gc)
        totv = tot_ref[...]
        swapped = pltpu.roll(totv, HD, axis=1)
        rem[0] = jnp.where(heads[0], totv, swapped)
        rem[1] = jnp.where(heads[1], totv, swapped)

        def tile(kb, thr):
            rows = pl.ds(pl.multiple_of(kb * tk, tk), tk)
            k = k_ref[rows, :]
            v = v_ref[rows, :]
            remv = [rem[0], rem[1]]
            gcv = [gc[0], gc[1]]
            zs = [_dot(qs[a], k, NT) for a in range(2)]
            das = [_dot(dob[a], v, NT) for a in range(2)]
            bs, mbs = [], []
            for a in range(2):
                b, m = _log_terms(zs[a])
                if thr is not None:
                    m = jnp.where(dif > thr, m, 0.0)
                bs.append(b)
                mbs.append(m.astype(BF16))
            pl_ = [_dot(mbs[a], up_incl) for a in range(2)]
            ps, gs, gbs = [], [], []
            for a in range(2):
                p = jnp.exp(bs[a] + (jnp.tile(remv[a], (1, nrep)) - pl_[a]))
                if thr is not None:
                    p = jnp.where(dif > thr, p, 0.0)
                g = p * das[a]
                ps.append(p.astype(BF16))
                gs.append(g)
                gbs.append(g.astype(BF16))
            cl = [_dot(gbs[a], up_strict) for a in range(2)]
            dk_add = jnp.zeros((tk, LANES), F32)
            dv_add = jnp.zeros((tk, LANES), F32)
            for a in range(2):
                dz = gs[a] - jnp.exp(bs[a]) * (gs[a] + (jnp.tile(gcv[a], (1, nrep)) + cl[a]))
                if thr is not None:
                    dz = jnp.where(dif > thr, dz, 0.0)
                dzb = dz.astype(BF16)
                dqacc[a] += _dot(dzb, k)
                dk_add += _dot(dzb, qs[a], TN)
                dv_add += _dot(ps[a], dob[a], TN)
                rem[a] = remv[a] - pl_[a][:, tk - 1:tk]
                gc[a] = gcv[a] + (cl[a][:, tk - 1:tk] + gbs[a][:, tk - 1:tk].astype(F32))
            dk_ref[rows, :] += dk_add
            dv_ref[rows, :] += dv_add

        def step(kb, carry):
            tile(kb, None)
            return carry

        lax.fori_loop(first_ref[pl.program_id(0), i].astype(jnp.int32), i * ndiag, step, 0)
        for d in range(ndiag):
            tile(i * ndiag + d, d * tk)
        dq_ref[...] = jnp.where(heads[0], dqacc[0], dqacc[1]) * 0.125

        @pl.when(step_id == npair * nq - 1)
        def _():
            cps = _chip_copies(e_in, e_out, *e_sems)
            for cp in cps:
                cp.wait_recv()
            for cp in cps:
                cp.wait_send()

    qspec = pl.BlockSpec((tq, LANES), lambda p, i: (i, p))
    full = pl.BlockSpec((s, LANES), lambda p, i: (0, p))
    any_spec = pl.BlockSpec(memory_space=pl.ANY)
    out = jax.ShapeDtypeStruct((s, DA), F32)
    res = pl.pallas_call(
        body, name=name, grid=(npair, nq),
        in_specs=[qspec, pl.BlockSpec((s, LANES), lambda p, i: (0, DA // LANES + p)),
                  pl.BlockSpec((s, LANES), lambda p, i: (0, 2 * DA // LANES + p)), qspec, qspec,
                  pl.BlockSpec(memory_space=pltpu.SMEM)] + [any_spec] * ne,
        out_specs=[qspec, full, full] + [any_spec] * ne,
        out_shape=[out, out, out] + [jax.ShapeDtypeStruct((3,) + a.shape[1:], a.dtype) for a in exchange],
        scratch_shapes=[pltpu.VMEM((2, tq, LANES), F32)] * 3
        + [pltpu.SemaphoreType.DMA((3 * ne,)), pltpu.SemaphoreType.DMA((3 * ne,))],
        compiler_params=_params("arbitrary", "arbitrary"),
    )(qkv, qkv, qkv, do, tot, first, *exchange)
    return res[0], res[1], res[2], res[3:]


def _shift_rows(v, k):
    return pltpu.roll(v, k % v.shape[0], axis=0)


def _pooled(u, uh, i, g, w, ts):
    halo = jnp.where(i > 0, uh, 0.0)
    ue = jnp.concatenate([halo, u], axis=0)
    acc, span = ue, 1
    while span < w:
        acc = acc + _shift_rows(acc, span)
        span *= 2
    tpos = i * ts + lax.broadcasted_iota(jnp.int32, (ts, 1), 0)
    cnt = jnp.minimum(tpos + 1, w).astype(F32)
    return acc[HALO:] / cnt - u


def _pool_mix(proj, o, pw, pb, ps, ag, bd, *, ts, name):
    s = proj.shape[0]
    hb = ts // HALO

    def body(u_ref, uh_ref, o_ref, pw_ref, pb_ref, ps_ref, ag_ref, bd_ref, mix_ref):
        i = pl.program_id(0)
        for g, w in enumerate(POOL_WINDOWS):
            cols = slice(g * LANES, (g + 1) * LANES)
            pooled = _pooled(u_ref[:, cols], uh_ref[:, cols], i, g, w, ts)
            yv = (_dot(pooled.astype(BF16), pw_ref[g]) + pb_ref[:, cols]) * ps_ref[:, cols]
            mix_ref[:, cols] = yv.astype(BF16)
        ov = o_ref[...]
        mix_ref[:, DP:] = (ov * _group_rsqrt(ov, bd_ref[...]) * ag_ref[...]).astype(BF16)

    return pl.pallas_call(
        body, name=name, grid=(s // ts,),
        in_specs=[pl.BlockSpec((ts, DP), lambda i: (i, 0)),
                  pl.BlockSpec((HALO, DP), lambda i: (jnp.maximum(i * hb - 1, 0), 0)),
                  pl.BlockSpec((ts, DA), lambda i: (i, 0)),
                  _full((4, LANES, LANES)), _vec(DP), _vec(DP), _vec(DA), _full((DA, DA))],
        out_specs=pl.BlockSpec((ts, D), lambda i: (i, 0)),
        out_shape=jax.ShapeDtypeStruct((s, D), BF16), compiler_params=_params("parallel"),
    )(proj, proj, o, pw, pb, ps, ag, bd)


def _res_ln_mod(x, att, gate, g, scale, shift, *, ts, name):
    s = x.shape[0]
    row = pl.BlockSpec((ts, D), lambda i: (i, 0))

    def body(x_ref, a_ref, gt_ref, g_ref, sc_ref, sh_ref, x1_ref, h_ref):
        x1 = x_ref[...] + gt_ref[...] * a_ref[...]
        x1_ref[...] = x1
        r = lax.rsqrt(jnp.mean(x1 * x1, axis=-1, keepdims=True) + EPS)
        h = (x1 * r) * g_ref[...]
        h_ref[...] = (h * (1.0 + sc_ref[...]) + sh_ref[...]).astype(BF16)

    return pl.pallas_call(
        body, name=name, grid=(s // ts,), in_specs=[row, row] + [_vec(D)] * 4, out_specs=[row, row],
        out_shape=[jax.ShapeDtypeStruct((s, D), F32), jax.ShapeDtypeStruct((s, D), BF16)],
        compiler_params=_params("parallel"),
    )(x, att, gate, g, scale, shift)


CF = DFF // 2


def _conv(u, uh, w_ref, b_ref, i):
    halo = jnp.where(i > 0, uh.astype(F32), 0.0)
    ue = jnp.concatenate([halo, u.astype(F32)], axis=0)
    y = w_ref[2:3, :] * ue + w_ref[1:2, :] * _shift_rows(ue, 1) + w_ref[0:1, :] * _shift_rows(ue, 2)
    return y[HALO:] + b_ref[...]


def _conv_gate(up, cw, cb, *, ts, name):
    s = up.shape[0]
    hb = ts // HALO

    def body(u_ref, uh_ref, w_ref, b_ref, a_ref):
        i = pl.program_id(0)
        c = _conv(u_ref[...], uh_ref[...], w_ref, b_ref, i)
        gt, vl = c[:, :CF], c[:, CF:]
        a_ref[...] = (gt / (1.0 + jnp.exp(-gt)) * vl).astype(BF16)

    return pl.pallas_call(
        body, name=name, grid=(s // ts, 2),
        in_specs=[pl.BlockSpec((ts, 2 * CF), lambda i, j: (i, j)),
                  pl.BlockSpec((HALO, 2 * CF), lambda i, j: (jnp.maximum(i * hb - 1, 0), j)),
                  pl.BlockSpec((3, 2 * CF), lambda i, j: (0, j)), pl.BlockSpec((1, 2 * CF), lambda i, j: (0, j))],
        out_specs=pl.BlockSpec((ts, CF), lambda i, j: (i, j)),
        out_shape=jax.ShapeDtypeStruct((s, DFF), BF16), compiler_params=_params("parallel", "parallel"),
    )(up, up, cw, cb)


def _loss_head(x1, ffn, tgt, gate2, *, ts, name):
    s = x1.shape[0]
    n = s // ts
    row = pl.BlockSpec((ts, D), lambda i: (i, 0))
    acc8 = pl.BlockSpec((SUBLANES, D), lambda i: (0, 0))

    def body(x_ref, f_ref, t_ref, g_ref, dy_ref, df_ref, dg_ref, loss_ref, lacc):
        i = pl.program_id(0)

        @pl.when(i == 0)
        def _():
            lacc[...] = jnp.zeros_like(lacc)
            dg_ref[...] = jnp.zeros_like(dg_ref)

        f = f_ref[...]
        diff = x_ref[...] + g_ref[...] * f - t_ref[...]
        lacc[...] += _colsum8(diff * diff)
        dy = diff * (1.0 / D)
        dy_ref[...] = dy
        df_ref[...] = (dy * g_ref[...]).astype(BF16)
        dg_ref[...] += _colsum8(dy * f)

        @pl.when(i == n - 1)
        def _():
            loss_ref[...] = jnp.full((SUBLANES, LANES), (0.5 / D) * jnp.sum(lacc[...]), F32)

    return pl.pallas_call(
        body, name=name, grid=(n,), in_specs=[row, row, row, _vec(D)],
        out_specs=[row, row, acc8, _full((SUBLANES, LANES))],
        out_shape=[jax.ShapeDtypeStruct((s, D), F32), jax.ShapeDtypeStruct((s, D), BF16),
                   jax.ShapeDtypeStruct((SUBLANES, D), F32), jax.ShapeDtypeStruct((SUBLANES, LANES), F32)],
        scratch_shapes=[pltpu.VMEM((SUBLANES, D), F32)], compiler_params=_params("arbitrary"),
    )(x1, ffn, tgt, gate2)


def _gate_bwd(da, up, cw, cb, *, ts, name):
    s = up.shape[0]
    hb = ts // HALO

    def body(da_ref, u_ref, uh_ref, w_ref, b_ref, d_ref, db_ref):
        i = pl.program_id(1)

        @pl.when(i == 0)
        def _():
            db_ref[...] = jnp.zeros_like(db_ref)

        c = _conv(u_ref[...], uh_ref[...], w_ref, b_ref, i)
        gt, vl = c[:, :CF], c[:, CF:]
        sg = 1.0 / (1.0 + jnp.exp(-gt))
        dav = da_ref[...].astype(F32)
        dgt = dav * vl * (sg * (1.0 + gt * (1.0 - sg)))
        dvl = dav * (gt * sg)
        d_ref[:, :CF] = dgt.astype(BF16)
        d_ref[:, CF:] = dvl.astype(BF16)
        db_ref[:, :CF] += _colsum8(dgt)
        db_ref[:, CF:] += _colsum8(dvl)

    return pl.pallas_call(
        body, name=name, grid=(2, s // ts),
        in_specs=[pl.BlockSpec((ts, CF), lambda j, i: (i, j)),
                  pl.BlockSpec((ts, 2 * CF), lambda j, i: (i, j)),
                  pl.BlockSpec((HALO, 2 * CF), lambda j, i: (jnp.maximum(i * hb - 1, 0), j)),
                  pl.BlockSpec((3, 2 * CF), lambda j, i: (0, j)), pl.BlockSpec((1, 2 * CF), lambda j, i: (0, j))],
        out_specs=[pl.BlockSpec((ts, 2 * CF), lambda j, i: (i, j)),
                   pl.BlockSpec((SUBLANES, 2 * CF), lambda j, i: (0, j))],
        out_shape=[jax.ShapeDtypeStruct((s, 2 * DFF), BF16), jax.ShapeDtypeStruct((SUBLANES, 2 * DFF), F32)],
        compiler_params=_params("parallel", "arbitrary"),
    )(da, up, up, cw, cb)


def _conv_bwd(dc, up, cw, *, ts, tc, name):
    s = up.shape[0]
    hb = ts // HALO
    nb = s // HALO

    def body(d_ref, dn_ref, u_ref, uh_ref, w_ref, du_ref, dw_ref):
        i = pl.program_id(1)
        n = s // ts

        @pl.when(i == 0)
        def _():
            dw_ref[...] = jnp.zeros_like(dw_ref)

        dcur = d_ref[...].astype(F32)
        nxt = jnp.where(i < n - 1, dn_ref[...].astype(F32), 0.0)
        de = jnp.concatenate([dcur, nxt], axis=0)
        du = w_ref[2:3, :] * de + w_ref[1:2, :] * _shift_rows(de, -1) + w_ref[0:1, :] * _shift_rows(de, -2)
        du_ref[...] = du[:ts].astype(BF16)
        u = u_ref[...].astype(F32)
        ue = jnp.concatenate([jnp.where(i > 0, uh_ref[...].astype(F32), 0.0), u], axis=0)
        dw_ref[16:24, :] += _colsum8(dcur * u)
        dw_ref[8:16, :] += _colsum8(dcur * _shift_rows(ue, 1)[HALO:])
        dw_ref[0:8, :] += _colsum8(dcur * _shift_rows(ue, 2)[HALO:])

    return pl.pallas_call(
        body, name=name, grid=(2 * DFF // tc, s // ts),
        in_specs=[pl.BlockSpec((ts, tc), lambda j, i: (i, j)),
                  pl.BlockSpec((HALO, tc), lambda j, i: (jnp.minimum((i + 1) * hb, nb - 1), j)),
                  pl.BlockSpec((ts, tc), lambda j, i: (i, j)),
                  pl.BlockSpec((HALO, tc), lambda j, i: (jnp.maximum(i * hb - 1, 0), j)),
                  pl.BlockSpec((3, tc), lambda j, i: (0, j))],
        out_specs=[pl.BlockSpec((ts, tc), lambda j, i: (i, j)), pl.BlockSpec((24, tc), lambda j, i: (0, j))],
        out_shape=[jax.ShapeDtypeStruct((s, 2 * DFF), BF16), jax.ShapeDtypeStruct((24, 2 * DFF), F32)],
        compiler_params=_params("parallel", "arbitrary"),
    )(dc, dc, up, up, cw)


def _ln_mod_bwd(dh, xin, g, scale, resid, extra, gate, *, ts, name):
    s = xin.shape[0]
    row = pl.BlockSpec((ts, D), lambda i: (i, 0))
    acc8 = pl.BlockSpec((SUBLANES, D), lambda i: (0, 0))
    with_gate = extra is not None

    def body(*refs):
        if with_gate:
            dh_ref, x_ref, g_ref, sc_ref, r_ref, e_ref, gt_ref, dx_ref, da_ref, dsh, dsc, dg, dgt = refs
        else:
            dh_ref, x_ref, g_ref, sc_ref, r_ref, dx_ref, dsh, dsc, dg = refs
        i = pl.program_id(0)

        @pl.when(i == 0)
        def _():
            for acc in (dsh, dsc, dg) + ((dgt,) if with_gate else ()):
                acc[...] = jnp.zeros_like(acc)

        xv, dhv = x_ref[...], dh_ref[...]
        r = lax.rsqrt(jnp.mean(xv * xv, axis=-1, keepdims=True) + EPS)
        xn = xv * r
        dsh[...] += _colsum8(dhv)
        dsc[...] += _colsum8(dhv * (xn * g_ref[...]))
        dhp = dhv * (1.0 + sc_ref[...])
        dg[...] += _colsum8(dhp * xn)
        dxn = dhp * g_ref[...]
        dx = r_ref[...] + r * (dxn - xn * jnp.mean(dxn * xn, axis=-1, keepdims=True))
        dx_ref[...] = dx
        if with_gate:
            da_ref[...] = (dx * gt_ref[...]).astype(BF16)
            dgt[...] += _colsum8(dx * e_ref[...])

    f32o, p8 = jax.ShapeDtypeStruct((s, D), F32), jax.ShapeDtypeStruct((SUBLANES, D), F32)
    if with_gate:
        ins, in_specs = (dh, xin, g, scale, resid, extra, gate), [row, row, _vec(D), _vec(D), row, row, _vec(D)]
        out_specs, out_shape = [row, row, acc8, acc8, acc8, acc8], [f32o, jax.ShapeDtypeStruct((s, D), BF16), p8, p8, p8, p8]
    else:
        ins, in_specs = (dh, xin, g, scale, resid), [row, row, _vec(D), _vec(D), row]
        out_specs, out_shape = [row, acc8, acc8, acc8], [f32o, p8, p8, p8]
    return pl.pallas_call(
        body, name=name, grid=(s // ts,), in_specs=in_specs, out_specs=out_specs, out_shape=out_shape,
        compiler_params=_params("arbitrary"),
    )(*ins)


def _group_norm_bwd(t, dn_out, gvec, bd):
    r = _group_rsqrt(t, bd)
    dg_terms = dn_out * t * r
    dn = dn_out * gvec
    dt = r * (dn - t * (r * r) * (_split_dot(dn * t, bd) * (1.0 / HD)))
    return dt, dg_terms


def _mix_bwd(dmix, proj, o, pw, pb, ps, ag, bd, *, ts, name):
    s = proj.shape[0]
    hb = ts // HALO
    nb = s // HALO

    def body(dm_ref, dmn_ref, u_ref, uh_ref, o_ref, pw_ref, pb_ref, ps_ref, ag_ref, bd_ref,
             du_ref, do_ref, dpw_ref, dpb_ref, dps_ref, dag_ref):
        i = pl.program_id(0)
        n = s // ts

        @pl.when(i == 0)
        def _():
            for acc in (dpw_ref, dpb_ref, dps_ref, dag_ref):
                acc[...] = jnp.zeros_like(acc)

        for g, w in enumerate(POOL_WINDOWS):
            cols = slice(g * LANES, (g + 1) * LANES)
            wg = pw_ref[g]
            psg = ps_ref[:, cols]
            pooled = _pooled(u_ref[:, cols], uh_ref[:, cols], i, g, w, ts).astype(BF16)
            dy = dm_ref[:, cols]
            dps_ref[:, cols] += _colsum8(dy * (_dot(pooled, wg) + pb_ref[:, cols]))
            dpre = dy * psg
            dpb_ref[:, cols] += _colsum8(dpre)
            dpreb = dpre.astype(BF16)
            dpw_ref[g * LANES:(g + 1) * LANES, :] += _dot(pooled, dpreb, TN)
            dpool = _dot(dpreb, wg, NT)
            dnext = _dot((dmn_ref[:, cols] * psg).astype(BF16), wg, NT)
            dpe = jnp.concatenate([dpool, jnp.where(i < n - 1, dnext, 0.0)], axis=0)
            tpos = i * ts + lax.broadcasted_iota(jnp.int32, (ts + HALO, 1), 0)
            acc = dpe / jnp.minimum(tpos + 1, w).astype(F32)
            span = 1
            while span < w:
                acc = acc + _shift_rows(acc, -span)
                span *= 2
            du_ref[:, cols] = acc[:ts] - dpool
        ov = o_ref[...]
        dov, dg_terms = _group_norm_bwd(ov, dm_ref[:, DP:], ag_ref[...], bd_ref[...])
        do_ref[...] = dov
        dag_ref[...] += _colsum8(dg_terms)

    p8 = jax.ShapeDtypeStruct((SUBLANES, DP), F32)
    acc8 = pl.BlockSpec((SUBLANES, DP), lambda i: (0, 0))
    half = pl.BlockSpec((ts, DP), lambda i: (i, 0))
    return pl.pallas_call(
        body, name=name, grid=(s // ts,),
        in_specs=[pl.BlockSpec((ts, D), lambda i: (i, 0)),
                  pl.BlockSpec((HALO, DP), lambda i: (jnp.minimum((i + 1) * hb, nb - 1), 0)),
                  half, pl.BlockSpec((HALO, DP), lambda i: (jnp.maximum(i * hb - 1, 0), 0)),
                  half, _full((4, LANES, LANES)), _vec(DP), _vec(DP), _vec(DA), _full((DA, DA))],
        out_specs=[half, half, _full((DP, LANES)), acc8, acc8, acc8],
        out_shape=[jax.ShapeDtypeStruct((s, DP), F32), jax.ShapeDtypeStruct((s, DA), F32),
                   jax.ShapeDtypeStruct((DP, LANES), F32), p8, p8, p8],
        compiler_params=_params("arbitrary"),
    )(dmix, dmix, proj, proj, o, pw, pb, ps, ag, bd)


def _qk_norm_bwd(du, dq, dk, dv, proj, qg, kg, bd, *, ts, name):
    s = proj.shape[0]

    def body(du_ref, dq_ref, dk_ref, dv_ref, q_ref, k_ref, qg_ref, kg_ref, bd_ref, dp_ref, dqg_ref, dkg_ref):
        i = pl.program_id(0)

        @pl.when(i == 0)
        def _():
            dqg_ref[...] = jnp.zeros_like(dqg_ref)
            dkg_ref[...] = jnp.zeros_like(dkg_ref)

        bdv = bd_ref[...]
        dqr, tq = _group_norm_bwd(q_ref[...], dq_ref[...], qg_ref[...], bdv)
        dkr, tk = _group_norm_bwd(k_ref[...], dk_ref[...], kg_ref[...], bdv)
        dqg_ref[...] += _colsum8(tq)
        dkg_ref[...] += _colsum8(tk)
        dp_ref[:, 0:DP] = du_ref[...].astype(BF16)
        dp_ref[:, DP:DP + DA] = dqr.astype(BF16)
        dp_ref[:, DP + DA:DP + 2 * DA] = dkr.astype(BF16)
        dp_ref[:, DP + 2 * DA:] = dv_ref[...].astype(BF16)

    half = pl.BlockSpec((ts, DA), lambda i: (i, 0))
    col = lambda j: pl.BlockSpec((ts, DA), lambda i: (i, j))
    acc8 = pl.BlockSpec((SUBLANES, DA), lambda i: (0, 0))
    p8 = jax.ShapeDtypeStruct((SUBLANES, DA), F32)
    return pl.pallas_call(
        body, name=name, grid=(s // ts,),
        in_specs=[half, half, half, half, col(1), col(2), _vec(DA), _vec(DA), _full((DA, DA))],
        out_specs=[pl.BlockSpec((ts, DIN), lambda i: (i, 0)), acc8, acc8],
        out_shape=[jax.ShapeDtypeStruct((s, DIN), BF16), p8, p8],
        compiler_params=_params("arbitrary"),
    )(du, dq, dk, dv, proj, proj, qg, kg, bd)


def _split3(a):
    hi = a.astype(BF16)
    return hi, (a - hi.astype(F32)).astype(BF16)


def _dot3(a, b, dn):
    ah, al = _split3(a)
    bh, bl = _split3(b)
    return _dot(ah, bh, dn) + (_dot(ah, bl, dn) + _dot(al, bh, dn))


def _ada_fwd(c_all, w, b, name):
    nw = w.shape[1]

    def body(c_ref, w_ref, b_ref, o_ref):
        cv = c_ref[...]
        act = cv / (1.0 + jnp.exp(-cv))
        o_ref[...] = _dot3(act, w_ref[...], NN) + b_ref[...]

    return pl.pallas_call(
        body, name=name, in_specs=[_full((NDEV, D)), _full(w.shape), _full((1, nw))], out_specs=_full((NDEV, nw)),
        out_shape=jax.ShapeDtypeStruct((NDEV, nw), F32), grid=(1,), compiler_params=_params("arbitrary"),
    )(c_all, w, b)


def _ada_bwd(c_all, dmod, name):
    nw = dmod.shape[1]

    def body(c_ref, d_ref, o_ref):
        cv = c_ref[...]
        act = cv / (1.0 + jnp.exp(-cv))
        o_ref[...] = _dot3(act, d_ref[...], TN)[None]

    return pl.pallas_call(
        body, name=name, in_specs=[_full((NDEV, D)), _full((NDEV, nw))], out_specs=_full((1, D, nw)),
        out_shape=jax.ShapeDtypeStruct((1, D, nw), F32), grid=(1,), compiler_params=_params("arbitrary"),
    )(c_all, dmod)


def _fold_heads(v):
    acc = v[:, 0:HD]
    for h in range(1, DA // HD):
        acc = acc + v[:, h * HD:(h + 1) * HD]
    return acc


def _small_update(gathered, gathered_pw, gathered_cw, specs, params, name):
    names = [sp[0] for sp in specs]
    flat = []
    for nme in names + ["pool_w", "conv_w"]:
        flat += list(params[nme])
    n_in = len(flat)

    def body(*refs):
        ga_ref, gp_ref, gc_ref = refs[0], refs[1], refs[2]
        prm = refs[3:3 + n_in]
        outs = refs[3 + n_in:]
        per_dev = [jnp.sum(ga_ref[dv], axis=0, keepdims=True) for dv in range(NDEV)]
        total = per_dev[0]
        for dv in range(1, NDEV):
            total = total + per_dev[dv]
        k = 0
        for idx, (nme, off, width, fold) in enumerate(specs):
            g = total[:, off:off + width]
            if fold:
                g = _fold_heads(g)
            w_ref, m_ref, v_ref = prm[3 * idx:3 * idx + 3]
            d, nm, nv = _adamw_math(w_ref[...], g, m_ref[...], v_ref[...])
            for val in (g, d, nm, nv):
                outs[k][...] = val
                k += 1
        gpw = gp_ref[0]
        for dv in range(1, NDEV):
            gpw = gpw + gp_ref[dv]
        w_ref, m_ref, v_ref = prm[3 * len(specs):3 * len(specs) + 3]
        d, nm, nv = _adamw_math(w_ref[...], gpw, m_ref[...], v_ref[...])
        for val in (gpw, d, nm, nv):
            outs[k][...] = val
            k += 1
        gcw = gc_ref[0]
        for dv in range(1, NDEV):
            gcw = gcw + gc_ref[dv]
        w_ref, m_ref, v_ref = prm[3 * len(specs) + 3:3 * len(specs) + 6]
        for tap in range(3):
            row = slice(tap, tap + 1)
            g = jnp.sum(gcw[SUBLANES * tap:SUBLANES * (tap + 1)], axis=0, keepdims=True)
            d, nm, nv = _adamw_math(w_ref[row, :], g, m_ref[row, :], v_ref[row, :])
            for q, val in enumerate((g, d, nm, nv)):
                outs[k + q][row, :] = val
        k += 4
        for dv in range(NDEV):
            outs[k][dv:dv + 1, :] = per_dev[dv][:, 0:6 * D]

    out_shape, out_specs = [], []
    for nme in names + ["pool_w", "conv_w"]:
        shp = params[nme][0].shape
        out_shape += [jax.ShapeDtypeStruct(shp, F32)] * 4
        out_specs += [_full(shp)] * 4
    out_shape.append(jax.ShapeDtypeStruct((NDEV, 6 * D), F32))
    out_specs.append(_full((NDEV, 6 * D)))
    res = pl.pallas_call(
        body, name=name, grid=(1,),
        in_specs=[_full(gathered.shape), _full(gathered_pw.shape), _full(gathered_cw.shape)] + [_full(a.shape) for a in flat],
        out_specs=out_specs, out_shape=out_shape, compiler_params=_params("arbitrary"),
    )(gathered, gathered_pw, gathered_cw, *flat)
    out = {nme: tuple(res[4 * i:4 * i + 4]) for i, nme in enumerate(names + ["pool_w", "conv_w"])}
    return out, res[-1]


def _row_tile(s):
    return 512 if s % 512 == 0 else s


def kernel(x, c, ada_w, ada_b, norm1_g, w_in, pool_w, pool_b, pool_scale, q_norm_g, k_norm_g, attn_out_g, w_out, norm2_g, w_up, conv_w, conv_b, w_down, loss_target, m_ada_w, m_ada_b, m_norm1_g, m_w_in, m_pool_w, m_pool_b, m_pool_scale, m_q_norm_g, m_k_norm_g, m_attn_out_g, m_w_out, m_norm2_g, m_w_up, m_conv_w, m_conv_b, m_w_down, v_ada_w, v_ada_b, v_norm1_g, v_w_in, v_pool_w, v_pool_b, v_pool_scale, v_q_norm_g, v_k_norm_g, v_attn_out_g, v_w_out, v_norm2_g, v_w_up, v_conv_w, v_conv_b, v_w_down):
    ax, ay, ac = lax.axis_index("x"), lax.axis_index("y"), lax.axis_index("c")
    me = 4 * ax + 2 * ay + ac
    me_swapped = 4 * ay + 2 * ax + ac
    xs, tgt = x[0], loss_target[0]
    s = xs.shape[0]
    ts = _row_tile(s)
    tq_attn, tk_attn = 512, 256
    bd = _block_diag_ones(DA, HD)

    c_all = _all_gather([jnp.broadcast_to(c, (SUBLANES, D))], [False], "gather_c")[0][:, 0, :]
    n_ada = ada_w.shape[2]
    ada_b_mine = lax.dynamic_slice_in_dim(ada_b, me * n_ada, n_ada, axis=1)
    mod_part = _ada_fwd(c_all, ada_w[0], ada_b_mine, "ada_fwd")
    mod_all = _all_gather([mod_part], [False], "gather_mod")[0]
    mod = lax.dynamic_index_in_dim(mod_all, me, axis=1, keepdims=False).reshape(1, 6 * D)
    shift1, scale1, gate1, shift2, scale2, gate2 = [mod[:, k * D:(k + 1) * D] for k in range(6)]

    w_in_t = w_in[0].T.astype(BF16)
    w_up_t = w_up[0].T.astype(BF16)
    gw_in, gcw = _all_gather([w_in_t, jnp.pad(conv_w[0], ((0, 5), (0, 64)))], [False, True], "gather_w_in")
    w_in_full = gw_in.reshape(DIN, D)
    later_w = [w_out[0].astype(BF16), w_up_t, w_down[0].astype(BF16)]
    cw_full = jnp.transpose(gcw[:, :3, :704], (1, 0, 2)).reshape(3, 2 * DFF)
    cb_full = jnp.transpose(conv_b.reshape(1, 2, 2, 2, 704), (0, 2, 1, 3, 4)).reshape(1, 2 * DFF)

    qg = jnp.tile(q_norm_g, (1, DA // HD))
    kg = jnp.tile(k_norm_g, (1, DA // HD))
    ag = attn_out_g.reshape(1, DA)
    pw = pool_w[0].astype(BF16)
    pb = pool_b.reshape(1, DP)
    h1 = _ln_mod(xs, norm1_g, scale1, shift1, ts=ts, name="ln1")
    proj = _matmul(h1, w_in_full, mode="nt", out_dtype=F32, tm=ts, tn=DIN, tk=D, name="in_proj")
    qkv = _qk_norm(proj, qg, kg, bd, ts=ts, name="qk_norm")
    o_raw, m_tot, kb_first, (gw_out, gw_up, gw_down) = _attn_fwd(
        qkv, later_w, [False, True, False], tq=tq_attn, tk=tk_attn, name="attn_fwd")
    w_out_full = gw_out.reshape(D, D)
    w_up_full = gw_up.reshape(2 * DFF, D)
    w_down_full = gw_down.reshape(DFF, D)
    mix = _pool_mix(proj, o_raw, pw, pb, pool_scale, ag, bd, ts=ts, name="pool_mix")
    att = _matmul(mix, w_out_full, mode="nn", out_dtype=F32, tm=ts, tn=D, tk=D, name="out_proj")
    x1, h2 = _res_ln_mod(xs, att, gate1, norm2_g, scale2, shift2, ts=ts, name="res_ln2")
    up = _matmul(h2, w_up_full, mode="nt", out_dtype=BF16, tm=ts, tn=CF, tk=D, name="up_proj", n_outer=True)
    act = _conv_gate(up, cw_full, cb_full, ts=ts // 2, name="conv_gate")
    ffn = _matmul(act, w_down_full, mode="nn", out_dtype=F32, tm=ts, tn=D, tk=DFF, name="down_proj")
    dy, dffn, dgate2_p, loss_p = _loss_head(x1, ffn, tgt, gate2, ts=ts, name="loss_head")
    loss = lax.psum(loss_p[0, 0], ("x", "y", "c"))

    da = _matmul(dffn, w_down_full, mode="nt", out_dtype=BF16, tm=ts, tn=CF, tk=D, name="down_bwd")
    g_w_down = _matmul(act, dffn, mode="tn", out_dtype=F32, tm=CF, tn=D, tk=ts, name="down_wgrad")
    dconv, dcb_p = _gate_bwd(da, up, cw_full, cb_full, ts=ts // 2, name="gate_bwd")
    dup, dcw_p = _conv_bwd(dconv, up, cw_full, ts=ts, tc=CF, name="conv_bwd")
    dh2 = _matmul(dup, w_up_full, mode="nn", out_dtype=F32, tm=ts, tn=D, tk=CF, name="up_bwd")
    g_w_up_t = _matmul(dup, h2, mode="tn", out_dtype=F32, tm=CF, tn=D, tk=ts, name="up_wgrad")
    core = jnp.reshape(ac, (1,)).astype(jnp.int32)
    chip = jnp.reshape(2 * ax + ay, (1,)).astype(jnp.int32)
    big_ffn = [g_w_up_t.reshape(NDEV, 2 * DFF // NDEV, D), g_w_down.reshape(NDEV, DFF // NDEV, D)]
    gots_ffn = _pair_exchange(big_ffn, [True, False], "rs_pair_ffn")
    sums_ffn = [_pair_sum(big_ffn[k], gots_ffn[k], sw, core, "rs_pair_sum_ffn%d" % k) for k, sw in enumerate([True, False])]
    dx1, datt, dshift2_p, dscale2_p, dnorm2_p, dgate1_p = _ln_mod_bwd(
        dh2, x1, norm2_g, scale2, dy, att, gate1, ts=ts, name="ln2_bwd")

    dmix = _matmul(datt, w_out_full, mode="nt", out_dtype=F32, tm=ts, tn=D, tk=D, name="out_bwd")
    g_w_out = _matmul(mix, datt, mode="tn", out_dtype=F32, tm=D, tn=D, tk=ts, name="out_wgrad")
    du, do_raw, g_pw_p, dpb_p, dps_p, dag_p = _mix_bwd(dmix, proj, o_raw, pw, pb, pool_scale, ag, bd, ts=ts, name="mix_bwd")
    dqn, dkn, dvv, parts_ffn = _attn_bwd(qkv, do_raw, m_tot, kb_first, sums_ffn, tq=tq_attn, tk=tk_attn, name="attn_bwd")
    dproj, dqg_p, dkg_p = _qk_norm_bwd(du, dqn, dkn, dvv, proj, qg, kg, bd, ts=ts, name="qk_norm_bwd")
    dh1 = _matmul(dproj, w_in_full, mode="nn", out_dtype=F32, tm=ts, tn=D, tk=DIN, name="in_bwd")
    g_w_in_t = _matmul(dproj, h1, mode="tn", out_dtype=F32, tm=DIN // 2, tn=D, tk=ts, name="in_wgrad")
    grad_x, dshift1_p, dscale1_p, dnorm1_p = _ln_mod_bwd(dh1, xs, norm1_g, scale1, dx1, None, None, ts=ts, name="ln1_bwd")

    big = [g_w_in_t.reshape(NDEV, DIN // NDEV, D), g_w_out.reshape(NDEV, D // NDEV, D)]
    gots = _pair_exchange(big, [False, False], "rs_pair")
    sums = [_pair_sum(big[k], gots[k], False, core, "rs_pair_sum%d" % k) for k in range(2)]
    parts = _chip_exchange(sums, "rs_chip")
    tr = lambda a: a[0].T
    r_in = _adamw_reduce(tr(w_in), tr(m_w_in), tr(v_w_in), sums[0], parts[0], chip, "adamw_w_in")
    r_out = _adamw_reduce(w_out[0], m_w_out[0], v_w_out[0], sums[1], parts[1], chip, "adamw_w_out")
    r_up = _adamw_reduce(tr(w_up), tr(m_w_up), tr(v_w_up), sums_ffn[0], parts_ffn[0], chip, "adamw_w_up")
    r_down = _adamw_reduce(w_down[0], m_w_down[0], v_w_down[0], sums_ffn[1], parts_ffn[1], chip, "adamw_w_down")
    r_in = [a.T[None] for a in r_in]
    r_up = [a.T[None] for a in r_up]
    r_out = [a[None] for a in r_out]
    r_down = [a[None] for a in r_down]

    dcb_nat = jnp.transpose(dcb_p.reshape(SUBLANES, 2, 2, 2, 704), (0, 2, 1, 3, 4)).reshape(SUBLANES, 2 * DFF)
    pieces = [dshift1_p, dscale1_p, dgate1_p, dshift2_p, dscale2_p, dgate2_p,
              dnorm1_p, dnorm2_p, dcb_nat, dpb_p, dps_p, dag_p, dqg_p, dkg_p]
    packed = jnp.concatenate(pieces, axis=1)
    gathered, gathered_pw, gathered_cw = _all_gather([packed, g_pw_p, dcw_p], [False, False, False], "gather_small")
    gathered_cw = lax.dynamic_index_in_dim(gathered_cw.reshape(NDEV, 3 * SUBLANES, NDEV, 704), me_swapped, axis=2, keepdims=False)
    specs = [("ada_b", 0, 6 * D, False)]
    off = 6 * D
    for nme, width, fold in (("norm1_g", D, False), ("norm2_g", D, False), ("conv_b", 2 * DFF, False),
                             ("pool_b", DP, False), ("pool_scale", DP, False), ("attn_out_g", DA, False),
                             ("q_norm_g", DA, True), ("k_norm_g", DA, True)):
        specs.append((nme, off, width, fold))
        off += width
    small = {
        "ada_b": (ada_b, m_ada_b, v_ada_b),
        "norm1_g": (norm1_g, m_norm1_g, v_norm1_g), "norm2_g": (norm2_g, m_norm2_g, v_norm2_g),
        "conv_b": (conv_b, m_conv_b, v_conv_b),
        "pool_b": (pb, m_pool_b.reshape(1, DP), v_pool_b.reshape(1, DP)),
        "pool_scale": (pool_scale, m_pool_scale, v_pool_scale),
        "attn_out_g": (ag, m_attn_out_g.reshape(1, DA), v_attn_out_g.reshape(1, DA)),
        "q_norm_g": (q_norm_g, m_q_norm_g, v_q_norm_g), "k_norm_g": (k_norm_g, m_k_norm_g, v_k_norm_g),
        "pool_w": (pool_w.reshape(DP, LANES), m_pool_w.reshape(DP, LANES), v_pool_w.reshape(DP, LANES)),
        "conv_w": (conv_w[0], m_conv_w[0], v_conv_w[0]),
    }
    upd, dmod_all = _small_update(gathered, gathered_pw, gathered_cw, specs, small, "small_update")
    g_ada_w = _ada_bwd(c_all, lax.dynamic_slice_in_dim(dmod_all, me * n_ada, n_ada, axis=1), "ada_bwd")
    r_ada = [g_ada_w] + [a[None] for a in _adamw(ada_w[0], m_ada_w[0], v_ada_w[0], g_ada_w[0], "adamw_ada_w")]

    shapes = {"ada_b": ada_b.shape, "norm1_g": norm1_g.shape, "pool_w": pool_w.shape, "pool_b": pool_b.shape,
              "pool_scale": pool_scale.shape, "q_norm_g": q_norm_g.shape, "k_norm_g": k_norm_g.shape,
              "attn_out_g": attn_out_g.shape, "norm2_g": norm2_g.shape, "conv_w": conv_w.shape, "conv_b": conv_b.shape}
    res = {nme: [a.reshape(shapes[nme]) for a in upd[nme]] for nme in shapes}
    res.update(ada_w=r_ada, w_in=r_in, w_out=r_out, w_up=r_up, w_down=r_down)
    names = ["ada_w", "ada_b", "norm1_g", "w_in", "pool_w", "pool_b", "pool_scale", "q_norm_g", "k_norm_g",
             "attn_out_g", "w_out", "norm2_g", "w_up", "conv_w", "conv_b", "w_down"]
    outs = [loss, grad_x[None]]
    for q in range(4):
        outs += [res[nme][q] for nme in names]
    return tuple(outs)
```

```python
import functools
import math

import numpy as np
import jax
import jax.numpy as jnp
from jax import lax
from jax.experimental import pallas as pl
from jax.experimental.pallas import tpu as pltpu

F32, BF16 = jnp.float32, jnp.bfloat16
D = 1024
DP = 512
DA = 512
HD = 64
DIN = DP + 3 * DA
DFF = 2816
POOL_WINDOWS = (2, 4, 8, 16)
HALO = 16
EPS = 1e-6
LANES = 128
SUBLANES = 8
NDEV = 8
VMEM_LIMIT = 56 * 1024 * 1024
MESH = pl.DeviceIdType.MESH

ADAM_LR, ADAM_B1, ADAM_B2, ADAM_EPS, ADAM_WD, ADAM_STEP = 0.001, 0.9, 0.999, 1e-08, 0.01, 10

NN = (((1,), (0,)), ((), ()))
NT = (((1,), (1,)), ((), ()))
TN = (((0,), (0,)), ((), ()))


def _params(*sem):
    return pltpu.CompilerParams(dimension_semantics=sem, vmem_limit_bytes=VMEM_LIMIT)


def _full(shape):
    nd = len(shape)
    return pl.BlockSpec(shape, lambda *_: (0,) * nd)


def _dot(a, b, dn=NN):
    return lax.dot_general(a, b, dn, preferred_element_type=F32)


def _split_dot(a, b, dn=NN):
    hi = a.astype(BF16)
    lo = (a - hi.astype(F32)).astype(BF16)
    return _dot(hi, b, dn) + _dot(lo, b, dn)


def _colsum8(v):
    r, n = v.shape
    return v.reshape(r // SUBLANES, SUBLANES, n).sum(axis=0)


def _block_diag_ones(n, blk):
    i = np.arange(n) // blk
    return jnp.asarray((i[:, None] == i[None, :]).astype(np.float32), BF16)


def _matmul(a, b, *, mode, out_dtype, tm, tn, tk, name, n_outer=False):
    if mode == "tn":
        K, M = a.shape
        N = b.shape[1]
    elif mode == "nt":
        M, K = a.shape
        N = b.shape[0]
    else:
        M, K = a.shape
        N = b.shape[1]
    tm, tn, tk = min(tm, M), min(tn, N), min(tk, K)
    assert M % tm == 0 and N % tn == 0 and K % tk == 0, (name, M, N, K, tm, tn, tk)
    nk = K // tk
    dn = {"nn": NN, "nt": NT, "tn": TN}[mode]

    def body(a_ref, b_ref, o_ref, *acc):
        if nk == 1:
            o_ref[...] = _dot(a_ref[...], b_ref[...], dn).astype(o_ref.dtype)
            return
        acc_ref, = acc
        k = pl.program_id(2)

        @pl.when(k == 0)
        def _():
            acc_ref[...] = jnp.zeros_like(acc_ref)

        acc_ref[...] += _dot(a_ref[...], b_ref[...], dn)

        @pl.when(k == nk - 1)
        def _():
            o_ref[...] = acc_ref[...].astype(o_ref.dtype)

    if n_outer:
        gi = lambda g: (g[1], g[0], g[2])
        grid = (N // tn, M // tm, nk)
    else:
        gi = lambda g: g
        grid = (M // tm, N // tn, nk)

    def amap(*g):
        i, j, k = gi(g)
        return (k, i) if mode == "tn" else (i, k)

    def bmap(*g):
        i, j, k = gi(g)
        return (j, k) if mode == "nt" else (k, j)

    def omap(*g):
        i, j, k = gi(g)
        return (i, j)

    a_blk = (tk, tm) if mode == "tn" else (tm, tk)
    b_blk = (tn, tk) if mode == "nt" else (tk, tn)
    return pl.pallas_call(
        body, name=name, grid=grid,
        in_specs=[pl.BlockSpec(a_blk, amap), pl.BlockSpec(b_blk, bmap)],
        out_specs=pl.BlockSpec((tm, tn), omap),
        out_shape=jax.ShapeDtypeStruct((M, N), out_dtype),
        scratch_shapes=[] if nk == 1 else [pltpu.VMEM((tm, tn), F32)],
        compiler_params=_params("parallel", "parallel", "arbitrary"),
    )(a, b)


def _slot(swap, px, py, pc):
    return 4 * py + 2 * px + pc if swap else 4 * px + 2 * py + pc


class _Gather:
    def __init__(self, ins, outs, send, recv, loc, swaps):
        self.ins, self.outs, self.send, self.recv, self.loc, self.swaps = ins, outs, send, recv, loc, swaps
        x, y, c = lax.axis_index("x"), lax.axis_index("y"), lax.axis_index("c")
        self.me, self.sib = (x, y, c), (x, y, 1 - c)
        self.chips = [(1 - x, y), (x, 1 - y), (1 - x, 1 - y)]
        self.n = len(ins)

    @staticmethod
    def scratch(n):
        return [pltpu.SemaphoreType.DMA((7 * n,)), pltpu.SemaphoreType.DMA((7 * n,)), pltpu.SemaphoreType.DMA((n,))]

    def copy(self, a, k, blk, to, src=None):
        rows = self.outs[a].at[_slot(self.swaps[a], *blk)]
        return pltpu.make_async_remote_copy(
            src_ref=rows if src is None else src, dst_ref=rows,
            send_sem=self.send.at[7 * a + k], recv_sem=self.recv.at[7 * a + k], device_id=to, device_id_type=MESH)

    def mine(self, a):
        return pltpu.make_async_copy(self.ins[a], self.outs[a].at[_slot(self.swaps[a], *self.me)], self.loc.at[a])

    def first(self, a):
        c = self.me[2]
        return [self.copy(a, 0, self.me, self.sib, src=self.ins[a])] + [
            self.copy(a, 1 + j, self.me, (*chip, c), src=self.ins[a]) for j, chip in enumerate(self.chips)]

    def forwards(self, a):
        c = self.me[2]
        return [self.copy(a, 4 + j, (*chip, c), self.sib) for j, chip in enumerate(self.chips)]

    def start(self):
        for a in range(self.n):
            self.mine(a).start()
        for a in range(self.n):
            for cp in self.first(a):
                cp.start()

    def forward(self):
        c = self.me[2]
        for a in range(self.n):
            fwd = self.forwards(a)
            for j, chip in enumerate(self.chips):
                self.copy(a, 1 + j, (*chip, c), self.me).wait_recv()
                fwd[j].start()

    def finish(self):
        c = self.me[2]
        for a in range(self.n):
            self.copy(a, 0, self.sib, self.me).wait_recv()
            for j, chip in enumerate(self.chips):
                self.copy(a, 4 + j, (*chip, 1 - c), self.me).wait_recv()
        for a in range(self.n):
            for cp in self.first(a) + self.forwards(a):
                cp.wait_send()
            self.mine(a).wait()


def _all_gather(arrs, swaps, name):
    n = len(arrs)

    def body(*refs):
        g = _Gather(refs[:n], refs[n:2 * n], *refs[2 * n:], swaps)
        g.start()
        g.forward()
        g.finish()

    any_spec = pl.BlockSpec(memory_space=pl.ANY)
    return pl.pallas_call(
        body, name=name,
        in_specs=[any_spec] * n, out_specs=[any_spec] * n,
        out_shape=[jax.ShapeDtypeStruct((NDEV,) + a.shape, a.dtype) for a in arrs],
        scratch_shapes=_Gather.scratch(n),
    )(*arrs)


def _pair_exchange(arrs, swaps, name):
    n = len(arrs)

    def body(*refs):
        ins, gots = refs[:n], refs[n:2 * n]
        send, recv = refs[2 * n:]
        x, y, c = lax.axis_index("x"), lax.axis_index("y"), lax.axis_index("c")
        sib = (x, y, 1 - c)
        rems = []
        for a in range(n):
            for k in range(4):
                kx, ky = k // 2, k % 2
                rc = pltpu.make_async_remote_copy(
                    src_ref=ins[a].at[_slot(swaps[a], kx, ky, 1 - c)], dst_ref=gots[a].at[k],
                    send_sem=send.at[4 * a + k], recv_sem=recv.at[4 * a + k], device_id=sib, device_id_type=MESH)
                rc.start()
                rems.append(rc)
        for rc in rems:
            rc.wait_recv()
        for rc in rems:
            rc.wait_send()

    any_spec = pl.BlockSpec(memory_space=pl.ANY)
    return pl.pallas_call(
        body, name=name,
        in_specs=[any_spec] * n, out_specs=[any_spec] * n,
        out_shape=[jax.ShapeDtypeStruct((4,) + a.shape[1:], a.dtype) for a in arrs],
        scratch_shapes=[pltpu.SemaphoreType.DMA((4 * n,)), pltpu.SemaphoreType.DMA((4 * n,))],
    )(*arrs)


def _chip_copies(ins, outs, send, recv):
    x, y, c = lax.axis_index("x"), lax.axis_index("y"), lax.axis_index("c")
    chips = [(1 - x, y), (x, 1 - y), (1 - x, 1 - y)]
    return [pltpu.make_async_remote_copy(
        src_ref=ins[a].at[2 * px + py], dst_ref=outs[a].at[j], send_sem=send.at[3 * a + j], recv_sem=recv.at[3 * a + j],
        device_id=(px, py, c), device_id_type=MESH) for a in range(len(ins)) for j, (px, py) in enumerate(chips)]


def _chip_exchange(arrs, name):
    n = len(arrs)

    def body(*refs):
        rems = _chip_copies(refs[:n], refs[n:2 * n], *refs[2 * n:])
        for rc in rems:
            rc.start()
        for rc in rems:
            rc.wait_recv()
        for rc in rems:
            rc.wait_send()

    any_spec = pl.BlockSpec(memory_space=pl.ANY)
    return pl.pallas_call(
        body, name=name,
        in_specs=[any_spec] * n, out_specs=[any_spec] * n,
        out_shape=[jax.ShapeDtypeStruct((3,) + a.shape[1:], a.dtype) for a in arrs],
        scratch_shapes=[pltpu.SemaphoreType.DMA((3 * n,)), pltpu.SemaphoreType.DMA((3 * n,))],
    )(*arrs)


def _pair_sum(grads, got, swap, core, name):
    _, r, c = got.shape
    tr = r if r <= 352 else r // 2

    def own_map(k, i, core_ref):
        return (_slot(swap, k // 2, k % 2, core_ref[0]), i, 0)

    def body(core_ref, a_ref, b_ref, o_ref):
        o_ref[...] = a_ref[...] + b_ref[...]

    spec = pl.BlockSpec((None, tr, c), lambda k, i, core_ref: (k, i, 0))
    return pl.pallas_call(
        body, name=name,
        grid_spec=pltpu.PrefetchScalarGridSpec(
            num_scalar_prefetch=1, grid=(4, r // tr),
            in_specs=[pl.BlockSpec((None, tr, c), own_map), spec], out_specs=spec),
        out_shape=jax.ShapeDtypeStruct(got.shape, got.dtype), compiler_params=_params("parallel", "parallel"),
    )(core, grads, got)


def _adamw_math(w, g, m, v):
    m = ADAM_B1 * m + (1.0 - ADAM_B1) * g
    v = ADAM_B2 * v + (1.0 - ADAM_B2) * (g * g)
    m_hat = m / (1.0 - ADAM_B1 ** ADAM_STEP)
    v_hat = v / (1.0 - ADAM_B2 ** ADAM_STEP)
    delta = -ADAM_LR * (m_hat / (jnp.sqrt(v_hat) + ADAM_EPS) + ADAM_WD * w)
    return delta, m, v


def _adamw_tile(r):
    for cand in (256, 352, 128):
        if r % cand == 0:
            return cand
    return r


def _adamw(w, m, v, g, name):
    r, c = w.shape
    tr = _adamw_tile(r)
    spec = pl.BlockSpec((tr, c), lambda i: (i, 0))

    def body(w_ref, m_ref, v_ref, g_ref, d_ref, nm_ref, nv_ref):
        d_ref[...], nm_ref[...], nv_ref[...] = _adamw_math(w_ref[...], g_ref[...], m_ref[...], v_ref[...])

    out = jax.ShapeDtypeStruct((r, c), F32)
    return pl.pallas_call(
        body, name=name, grid=(r // tr,), in_specs=[spec] * 4, out_specs=[spec] * 3, out_shape=[out] * 3,
        compiler_params=_params("parallel"),
    )(w, m, v, g)


def _adamw_reduce(w, m, v, sums, recv, chip, name):
    r, c = w.shape
    tr = _adamw_tile(r)
    spec = pl.BlockSpec((tr, c), lambda i, chip_ref: (i, 0))

    def body(chip_ref, w_ref, m_ref, v_ref, s_ref, p_ref, g_ref, d_ref, nm_ref, nv_ref):
        g = ((s_ref[...] + p_ref[0]) + p_ref[1]) + p_ref[2]
        g_ref[...] = g
        d_ref[...], nm_ref[...], nv_ref[...] = _adamw_math(w_ref[...], g, m_ref[...], v_ref[...])

    out = jax.ShapeDtypeStruct((r, c), F32)
    return pl.pallas_call(
        body, name=name,
        grid_spec=pltpu.PrefetchScalarGridSpec(
            num_scalar_prefetch=1, grid=(r // tr,),
            in_specs=[spec, spec, spec, pl.BlockSpec((None, tr, c), lambda i, chip_ref: (chip_ref[0], i, 0)),
                      pl.BlockSpec((3, tr, c), lambda i, chip_ref: (0, i, 0))],
            out_specs=[spec] * 4),
        out_shape=[out] * 4, compiler_params=_params("parallel"),
    )(chip, w, m, v, sums, recv)


def _vec(n):
    return pl.BlockSpec((1, n), lambda *_: (0, 0))


def _ln_mod(x, g, scale, shift, *, ts, name):
    s = x.shape[0]
    row = pl.BlockSpec((ts, D), lambda i: (i, 0))

    def body(x_ref, g_ref, sc_ref, sh_ref, h_ref):
        xv = x_ref[...]
        r = lax.rsqrt(jnp.mean(xv * xv, axis=-1, keepdims=True) + EPS)
        h = (xv * r) * g_ref[...]
        h_ref[...] = (h * (1.0 + sc_ref[...]) + sh_ref[...]).astype(BF16)

    return pl.pallas_call(
        body, name=name, grid=(s // ts,), in_specs=[row, _vec(D), _vec(D), _vec(D)], out_specs=row,
        out_shape=jax.ShapeDtypeStruct((s, D), BF16), compiler_params=_params("parallel"),
    )(x, g, scale, shift)


def _group_rsqrt(t, bd):
    return lax.rsqrt(_split_dot(t * t, bd) * (1.0 / HD) + EPS)


def _qk_norm(proj, qg, kg, bd, *, ts, name):
    s = proj.shape[0]

    def body(q_ref, k_ref, v_ref, qg_ref, kg_ref, bd_ref, o_ref):
        bdv = bd_ref[...]
        q, k = q_ref[...], k_ref[...]
        o_ref[:, 0:DA] = (q * _group_rsqrt(q, bdv) * qg_ref[...]).astype(BF16)
        o_ref[:, DA:2 * DA] = (k * _group_rsqrt(k, bdv) * kg_ref[...]).astype(BF16)
        o_ref[:, 2 * DA:] = v_ref[...].astype(BF16)

    col = lambda j: pl.BlockSpec((ts, DA), lambda i: (i, j))
    return pl.pallas_call(
        body, name=name, grid=(s // ts,),
        in_specs=[col(1), col(2), col(3), _vec(DA), _vec(DA), _full((DA, DA))],
        out_specs=pl.BlockSpec((ts, 3 * DA), lambda i: (i, 0)),
        out_shape=jax.ShapeDtypeStruct((s, 3 * DA), BF16), compiler_params=_params("parallel"),
    )(proj, proj, proj, qg, kg, bd)


EXP_UNDERFLOW = -120.0


def _log_terms(z):
    neg_abs = lax.bitcast_convert_type(lax.bitcast_convert_type(z, jnp.uint32) | jnp.uint32(0x80000000), F32)
    b = jnp.minimum(z, 0.0) - jnp.log(1.0 + jnp.exp(neg_abs))
    return b, b - z


def _head_masks(rows):
    lane = lax.broadcasted_iota(jnp.int32, (rows, LANES), 1)
    return [lane < HD, lane >= HD]


def _attn_fwd(qkv, gather, swaps, *, tq, tk, name):
    s = qkv.shape[0]
    nrep = tk // LANES
    ndiag = tq // tk
    ng = len(gather)
    npair, nq = DA // LANES, s // tq

    def body(*refs):
        q_ref, k_ref, v_ref = refs[:3]
        g_in = refs[3:3 + ng]
        o_ref, tot_ref, first_ref = refs[3 + ng:6 + ng]
        g_out = refs[6 + ng:6 + 2 * ng]
        oacc, rc = refs[6 + 2 * ng:8 + 2 * ng]
        g_sems = refs[8 + 2 * ng:]
        i = pl.program_id(1)
        step_id = pl.program_id(0) * nq + i

        @pl.when(step_id == 0)
        def _():
            _Gather(g_in, g_out, *g_sems, swaps).start()

        @pl.when(step_id == (npair * nq * 3) // 4)
        def _():
            _Gather(g_in, g_out, *g_sems, swaps).forward()

        heads = _head_masks(tq)
        q = q_ref[...] * 0.125
        qs = [jnp.where(h, q, 0.0).astype(BF16) for h in heads]
        dif = lax.broadcasted_iota(jnp.int32, (tq, tk), 0) - lax.broadcasted_iota(jnp.int32, (tq, tk), 1)
        kr = lax.broadcasted_iota(jnp.int32, (tk, tk), 0)
        kc = lax.broadcasted_iota(jnp.int32, (tk, tk), 1)
        later = jnp.where(kr > kc, 1.0, 0.0).astype(BF16)
        oacc[...] = jnp.zeros_like(oacc)
        rc[...] = jnp.zeros_like(rc)

        def tile(kb, thr):
            rows = pl.ds(pl.multiple_of(kb * tk, tk), tk)
            k = k_ref[rows, :]
            v = v_ref[rows, :]
            rcv = [rc[0], rc[1]]
            zs = [_dot(qs[a], k, NT) for a in range(2)]
            bs, mbs = [], []
            for a in range(2):
                b, m = _log_terms(zs[a])
                if thr is not None:
                    m = jnp.where(dif > thr, m, 0.0)
                bs.append(b)
                mbs.append(m.astype(BF16))
            rl = [_dot(mbs[a], later) for a in range(2)]
            for a in range(2):
                p = jnp.exp(bs[a] + (rl[a] + jnp.tile(rcv[a], (1, nrep))))
                if thr is not None:
                    p = jnp.where(dif > thr, p, 0.0)
                oacc[a] += _dot(p.astype(BF16), v)
                rc[a] = rcv[a] + (rl[a][:, 0:1] + mbs[a][:, 0:1].astype(F32))

        for d in reversed(range(ndiag)):
            tile(i * ndiag + d, d * tk)

        def live():
            return jnp.max(jnp.maximum(rc[0], rc[1])) > EXP_UNDERFLOW

        def step(carry):
            kb, _ = carry
            tile(kb, None)
            return kb - 1, live()

        kb_end, _ = lax.while_loop(lambda cr: jnp.logical_and(cr[0] >= 0, cr[1]), step, (i * ndiag - 1, live()))
        first_ref[pl.program_id(0), i] = (kb_end + 1).astype(F32)
        o_ref[...] = jnp.where(heads[0], oacc[0], oacc[1])
        tot_ref[...] = jnp.where(heads[0], rc[0], rc[1])

        @pl.when(step_id == npair * nq - 1)
        def _():
            _Gather(g_in, g_out, *g_sems, swaps).finish()

    qspec = pl.BlockSpec((tq, LANES), lambda p, i: (i, p))
    any_spec = pl.BlockSpec(memory_space=pl.ANY)
    res = pl.pallas_call(
        body, name=name, grid=(npair, nq),
        in_specs=[qspec,
                  pl.BlockSpec((s, LANES), lambda p, i: (0, DA // LANES + p)),
                  pl.BlockSpec((s, LANES), lambda p, i: (0, 2 * DA // LANES + p))] + [any_spec] * ng,
        out_specs=[qspec, qspec, pl.BlockSpec(memory_space=pltpu.SMEM)] + [any_spec] * ng,
        out_shape=[jax.ShapeDtypeStruct((s, DA), F32), jax.ShapeDtypeStruct((s, DA), F32),
                   jax.ShapeDtypeStruct((npair, nq), F32)]
        + [jax.ShapeDtypeStruct((NDEV,) + a.shape, a.dtype) for a in gather],
        scratch_shapes=[pltpu.VMEM((2, tq, LANES), F32), pltpu.VMEM((2, tq, LANES), F32)] + _Gather.scratch(ng),
        compiler_params=_params("arbitrary", "arbitrary"),
    )(qkv, qkv, qkv, *gather)
    return res[0], res[1], res[2], res[3:]


def _attn_bwd(qkv, do, tot, first, exchange, *, tq, tk, name):
    s = qkv.shape[0]
    nrep = tk // LANES
    ndiag = tq // tk
    ne = len(exchange)
    npair, nq = DA // LANES, s // tq

    def body(*refs):
        q_ref, k_ref, v_ref, do_ref, tot_ref, first_ref = refs[:6]
        e_in = refs[6:6 + ne]
        dq_ref, dk_ref, dv_ref = refs[6 + ne:9 + ne]
        e_out = refs[9 + ne:9 + 2 * ne]
        dqacc, rem, gc = refs[9 + 2 * ne:12 + 2 * ne]
        e_sems = refs[12 + 2 * ne:]
        i = pl.program_id(1)
        step_id = pl.program_id(0) * nq + i

        @pl.when(step_id == 0)
        def _():
            for cp in _chip_copies(e_in, e_out, *e_sems):
                cp.start()

        @pl.when(i == 0)
        def _():
            dk_ref[...] = jnp.zeros_like(dk_ref)
            dv_ref[...] = jnp.zeros_like(dv_ref)

        heads = _head_masks(tq)
        q = q_ref[...] * 0.125
        qs = [jnp.where(h, q, 0.0).astype(BF16) for h in heads]
        dov = do_ref[...]
        dob = [jnp.where(h, dov, 0.0).astype(BF16) for h in heads]
        dif = lax.broadcasted_iota(jnp.int32, (tq, tk), 0) - lax.broadcasted_iota(jnp.int32, (tq, tk), 1)
        kr = lax.broadcasted_iota(jnp.int32, (tk, tk), 0)
        kc = lax.broadcasted_iota(jnp.int32, (tk, tk), 1)
        up_incl = jnp.where(kr <= kc, 1.0, 0.0).astype(BF16)
        up_strict = jnp.where(kr < kc, 1.0, 0.0).astype(BF16)
        dqacc[...] = jnp.zeros_like(dqacc)
        gc[...] = jnp.zeros_like(gc)
        totv = tot_ref[...]
        swapped = pltpu.roll(totv, HD, axis=1)
        rem[0] = jnp.where(heads[0], totv, swapped)
        rem[1] = jnp.where(heads[1], totv, swapped)

        def tile(kb, thr):
            rows = pl.ds(pl.multiple_of(kb * tk, tk), tk)
            k = k_ref[rows, :]
            v = v_ref[rows, :]
            remv = [rem[0], rem[1]]
            gcv = [gc[0], gc[1]]
            zs = [_dot(qs[a], k, NT) for a in range(2)]
            das = [_dot(dob[a], v, NT) for a in range(2)]
            bs, mbs = [], []
            for a in range(2):
                b, m = _log_terms(zs[a])
                if thr is not None:
                    m = jnp.where(dif > thr, m, 0.0)
                bs.append(b)
                mbs.append(m.astype(BF16))
            pl_ = [_dot(mbs[a], up_incl) for a in range(2)]
            ps, gs, gbs = [], [], []
            for a in range(2):
                p = jnp.exp(bs[a] + (jnp.tile(remv[a], (1, nrep)) - pl_[a]))
                if thr is not None:
                    p = jnp.where(dif > thr, p, 0.0)
                g = p * das[a]
                ps.append(p.astype(BF16))
                gs.append(g)
                gbs.append(g.astype(BF16))
            cl = [_dot(gbs[a], up_strict) for a in range(2)]
            dk_add = jnp.zeros((tk, LANES), F32)
            dv_add = jnp.zeros((tk, LANES), F32)
            for a in range(2):
                dz = gs[a] - jnp.exp(bs[a]) * (gs[a] + (jnp.tile(gcv[a], (1, nrep)) + cl[a]))
                if thr is not None:
                    dz = jnp.where(dif > thr, dz, 0.0)
                dzb = dz.astype(BF16)
                dqacc[a] += _dot(dzb, k)
                dk_add += _dot(dzb, qs[a], TN)
                dv_add += _dot(ps[a], dob[a], TN)
                rem[a] = remv[a] - pl_[a][:, tk - 1:tk]
                gc[a] = gcv[a] + (cl[a][:, tk - 1:tk] + gbs[a][:, tk - 1:tk].astype(F32))
            dk_ref[rows, :] += dk_add
            dv_ref[rows, :] += dv_add

        def step(kb, carry):
            tile(kb, None)
            return carry

        lax.fori_loop(first_ref[pl.program_id(0), i].astype(jnp.int32), i * ndiag, step, 0)
        for d in range(ndiag):
            tile(i * ndiag + d, d * tk)
        dq_ref[...] = jnp.where(heads[0], dqacc[0], dqacc[1]) * 0.125

        @pl.when(step_id == npair * nq - 1)
        def _():
            cps = _chip_copies(e_in, e_out, *e_sems)
            for cp in cps:
                cp.wait_recv()
            for cp in cps:
                cp.wait_send()

    qspec = pl.BlockSpec((tq, LANES), lambda p, i: (i, p))
    full = pl.BlockSpec((s, LANES), lambda p, i: (0, p))
    any_spec = pl.BlockSpec(memory_space=pl.ANY)
    out = jax.ShapeDtypeStruct((s, DA), F32)
    res = pl.pallas_call(
        body, name=name, grid=(npair, nq),
        in_specs=[qspec, pl.BlockSpec((s, LANES), lambda p, i: (0, DA // LANES + p)),
                  pl.BlockSpec((s, LANES), lambda p, i: (0, 2 * DA // LANES + p)), qspec, qspec,
                  pl.BlockSpec(memory_space=pltpu.SMEM)] + [any_spec] * ne,
        out_specs=[qspec, full, full] + [any_spec] * ne,
        out_shape=[out, out, out] + [jax.ShapeDtypeStruct((3,) + a.shape[1:], a.dtype) for a in exchange],
        scratch_shapes=[pltpu.VMEM((2, tq, LANES), F32)] * 3
        + [pltpu.SemaphoreType.DMA((3 * ne,)), pltpu.SemaphoreType.DMA((3 * ne,))],
        compiler_params=_params("arbitrary", "arbitrary"),
    )(qkv, qkv, qkv, do, tot, first, *exchange)
    return res[0], res[1], res[2], res[3:]


def _shift_rows(v, k):
    return pltpu.roll(v, k % v.shape[0], axis=0)


def _pooled(u, uh, i, g, w, ts):
    halo = jnp.where(i > 0, uh, 0.0)
    ue = jnp.concatenate([halo, u], axis=0)
    acc, span = ue, 1
    while span < w:
        acc = acc + _shift_rows(acc, span)
        span *= 2
    tpos = i * ts + lax.broadcasted_iota(jnp.int32, (ts, 1), 0)
    cnt = jnp.minimum(tpos + 1, w).astype(F32)
    return acc[HALO:] / cnt - u


def _pool_mix(proj, o, pw, pb, ps, ag, bd, *, ts, name):
    s = proj.shape[0]
    hb = ts // HALO

    def body(u_ref, uh_ref, o_ref, pw_ref, pb_ref, ps_ref, ag_ref, bd_ref, mix_ref):
        i = pl.program_id(0)
        for g, w in enumerate(POOL_WINDOWS):
            cols = slice(g * LANES, (g + 1) * LANES)
            pooled = _pooled(u_ref[:, cols], uh_ref[:, cols], i, g, w, ts)
            yv = (_dot(pooled.astype(BF16), pw_ref[g]) + pb_ref[:, cols]) * ps_ref[:, cols]
            mix_ref[:, cols] = yv.astype(BF16)
        ov = o_ref[...]
        mix_ref[:, DP:] = (ov * _group_rsqrt(ov, bd_ref[...]) * ag_ref[...]).astype(BF16)

    return pl.pallas_call(
        body, name=name, grid=(s // ts,),
        in_specs=[pl.BlockSpec((ts, DP), lambda i: (i, 0)),
                  pl.BlockSpec((HALO, DP), lambda i: (jnp.maximum(i * hb - 1, 0), 0)),
                  pl.BlockSpec((ts, DA), lambda i: (i, 0)),
                  _full((4, LANES, LANES)), _vec(DP), _vec(DP), _vec(DA), _full((DA, DA))],
        out_specs=pl.BlockSpec((ts, D), lambda i: (i, 0)),
        out_shape=jax.ShapeDtypeStruct((s, D), BF16), compiler_params=_params("parallel"),
    )(proj, proj, o, pw, pb, ps, ag, bd)


def _res_ln_mod(x, att, gate, g, scale, shift, *, ts, name):
    s = x.shape[0]
    row = pl.BlockSpec((ts, D), lambda i: (i, 0))

    def body(x_ref, a_ref, gt_ref, g_ref, sc_ref, sh_ref, x1_ref, h_ref):
        x1 = x_ref[...] + gt_ref[...] * a_ref[...]
        x1_ref[...] = x1
        r = lax.rsqrt(jnp.mean(x1 * x1, axis=-1, keepdims=True) + EPS)
        h = (x1 * r) * g_ref[...]
        h_ref[...] = (h * (1.0 + sc_ref[...]) + sh_ref[...]).astype(BF16)

    return pl.pallas_call(
        body, name=name, grid=(s // ts,), in_specs=[row, row] + [_vec(D)] * 4, out_specs=[row, row],
        out_shape=[jax.ShapeDtypeStruct((s, D), F32), jax.ShapeDtypeStruct((s, D), BF16)],
        compiler_params=_params("parallel"),
    )(x, att, gate, g, scale, shift)


CF = DFF // 2


def _conv(u, uh, w_ref, b_ref, i):
    halo = jnp.where(i > 0, uh.astype(F32), 0.0)
    ue = jnp.concatenate([halo, u.astype(F32)], axis=0)
    y = w_ref[2:3, :] * ue + w_ref[1:2, :] * _shift_rows(ue, 1) + w_ref[0:1, :] * _shift_rows(ue, 2)
    return y[HALO:] + b_ref[...]


def _conv_gate(up, cw, cb, *, ts, name):
    s = up.shape[0]
    hb = ts // HALO

    def body(u_ref, uh_ref, w_ref, b_ref, a_ref):
        i = pl.program_id(0)
        c = _conv(u_ref[...], uh_ref[...], w_ref, b_ref, i)
        gt, vl = c[:, :CF], c[:, CF:]
        a_ref[...] = (gt / (1.0 + jnp.exp(-gt)) * vl).astype(BF16)

    return pl.pallas_call(
        body, name=name, grid=(s // ts, 2),
        in_specs=[pl.BlockSpec((ts, 2 * CF), lambda i, j: (i, j)),
                  pl.BlockSpec((HALO, 2 * CF), lambda i, j: (jnp.maximum(i * hb - 1, 0), j)),
                  pl.BlockSpec((3, 2 * CF), lambda i, j: (0, j)), pl.BlockSpec((1, 2 * CF), lambda i, j: (0, j))],
        out_specs=pl.BlockSpec((ts, CF), lambda i, j: (i, j)),
        out_shape=jax.ShapeDtypeStruct((s, DFF), BF16), compiler_params=_params("parallel", "parallel"),
    )(up, up, cw, cb)


def _loss_head(x1, ffn, tgt, gate2, *, ts, name):
    s = x1.shape[0]
    n = s // ts
    row = pl.BlockSpec((ts, D), lambda i: (i, 0))
    acc8 = pl.BlockSpec((SUBLANES, D), lambda i: (0, 0))

    def body(x_ref, f_ref, t_ref, g_ref, dy_ref, df_ref, dg_ref, loss_ref, lacc):
        i = pl.program_id(0)

        @pl.when(i == 0)
        def _():
            lacc[...] = jnp.zeros_like(lacc)
            dg_ref[...] = jnp.zeros_like(dg_ref)

        f = f_ref[...]
        diff = x_ref[...] + g_ref[...] * f - t_ref[...]
        lacc[...] += _colsum8(diff * diff)
        dy = diff * (1.0 / D)
        dy_ref[...] = dy
        df_ref[...] = (dy * g_ref[...]).astype(BF16)
        dg_ref[...] += _colsum8(dy * f)

        @pl.when(i == n - 1)
        def _():
            loss_ref[...] = jnp.full((SUBLANES, LANES), (0.5 / D) * jnp.sum(lacc[...]), F32)

    return pl.pallas_call(
        body, name=name, grid=(n,), in_specs=[row, row, row, _vec(D)],
        out_specs=[row, row, acc8, _full((SUBLANES, LANES))],
        out_shape=[jax.ShapeDtypeStruct((s, D), F32), jax.ShapeDtypeStruct((s, D), BF16),
                   jax.ShapeDtypeStruct((SUBLANES, D), F32), jax.ShapeDtypeStruct((SUBLANES, LANES), F32)],
        scratch_shapes=[pltpu.VMEM((SUBLANES, D), F32)], compiler_params=_params("arbitrary"),
    )(x1, ffn, tgt, gate2)


def _gate_bwd(da, up, cw, cb, *, ts, name):
    s = up.shape[0]
    hb = ts // HALO

    def body(da_ref, u_ref, uh_ref, w_ref, b_ref, d_ref, db_ref):
        i = pl.program_id(1)

        @pl.when(i == 0)
        def _():
            db_ref[...] = jnp.zeros_like(db_ref)

        c = _conv(u_ref[...], uh_ref[...], w_ref, b_ref, i)
        gt, vl = c[:, :CF], c[:, CF:]
        sg = 1.0 / (1.0 + jnp.exp(-gt))
        dav = da_ref[...].astype(F32)
        dgt = dav * vl * (sg * (1.0 + gt * (1.0 - sg)))
        dvl = dav * (gt * sg)
        d_ref[:, :CF] = dgt.astype(BF16)
        d_ref[:, CF:] = dvl.astype(BF16)
        db_ref[:, :CF] += _colsum8(dgt)
        db_ref[:, CF:] += _colsum8(dvl)

    return pl.pallas_call(
        body, name=name, grid=(2, s // ts),
        in_specs=[pl.BlockSpec((ts, CF), lambda j, i: (i, j)),
                  pl.BlockSpec((ts, 2 * CF), lambda j, i: (i, j)),
                  pl.BlockSpec((HALO, 2 * CF), lambda j, i: (jnp.maximum(i * hb - 1, 0), j)),
                  pl.BlockSpec((3, 2 * CF), lambda j, i: (0, j)), pl.BlockSpec((1, 2 * CF), lambda j, i: (0, j))],
        out_specs=[pl.BlockSpec((ts, 2 * CF), lambda j, i: (i, j)),
                   pl.BlockSpec((SUBLANES, 2 * CF), lambda j, i: (0, j))],
        out_shape=[jax.ShapeDtypeStruct((s, 2 * DFF), BF16), jax.ShapeDtypeStruct((SUBLANES, 2 * DFF), F32)],
        compiler_params=_params("parallel", "arbitrary"),
    )(da, up, up, cw, cb)


def _conv_bwd(dc, up, cw, *, ts, tc, name):
    s = up.shape[0]
    hb = ts // HALO
    nb = s // HALO

    def body(d_ref, dn_ref, u_ref, uh_ref, w_ref, du_ref, dw_ref):
        i = pl.program_id(1)
        n = s // ts

        @pl.when(i == 0)
        def _():
            dw_ref[...] = jnp.zeros_like(dw_ref)

        dcur = d_ref[...].astype(F32)
        nxt = jnp.where(i < n - 1, dn_ref[...].astype(F32), 0.0)
        de = jnp.concatenate([dcur, nxt], axis=0)
        du = w_ref[2:3, :] * de + w_ref[1:2, :] * _shift_rows(de, -1) + w_ref[0:1, :] * _shift_rows(de, -2)
        du_ref[...] = du[:ts].astype(BF16)
        u = u_ref[...].astype(F32)
        ue = jnp.concatenate([jnp.where(i > 0, uh_ref[...].astype(F32), 0.0), u], axis=0)
        dw_ref[16:24, :] += _colsum8(dcur * u)
        dw_ref[8:16, :] += _colsum8(dcur * _shift_rows(ue, 1)[HALO:])
        dw_ref[0:8, :] += _colsum8(dcur * _shift_rows(ue, 2)[HALO:])

    return pl.pallas_call(
        body, name=name, grid=(2 * DFF // tc, s // ts),
        in_specs=[pl.BlockSpec((ts, tc), lambda j, i: (i, j)),
                  pl.BlockSpec((HALO, tc), lambda j, i: (jnp.minimum((i + 1) * hb, nb - 1), j)),
                  pl.BlockSpec((ts, tc), lambda j, i: (i, j)),
                  pl.BlockSpec((HALO, tc), lambda j, i: (jnp.maximum(i * hb - 1, 0), j)),
                  pl.BlockSpec((3, tc), lambda j, i: (0, j))],
        out_specs=[pl.BlockSpec((ts, tc), lambda j, i: (i, j)), pl.BlockSpec((24, tc), lambda j, i: (0, j))],
        out_shape=[jax.ShapeDtypeStruct((s, 2 * DFF), BF16), jax.ShapeDtypeStruct((24, 2 * DFF), F32)],
        compiler_params=_params("parallel", "arbitrary"),
    )(dc, dc, up, up, cw)


def _ln_mod_bwd(dh, xin, g, scale, resid, extra, gate, *, ts, name):
    s = xin.shape[0]
    row = pl.BlockSpec((ts, D), lambda i: (i, 0))
    acc8 = pl.BlockSpec((SUBLANES, D), lambda i: (0, 0))
    with_gate = extra is not None

    def body(*refs):
        if with_gate:
            dh_ref, x_ref, g_ref, sc_ref, r_ref, e_ref, gt_ref, dx_ref, da_ref, dsh, dsc, dg, dgt = refs
        else:
            dh_ref, x_ref, g_ref, sc_ref, r_ref, dx_ref, dsh, dsc, dg = refs
        i = pl.program_id(0)

        @pl.when(i == 0)
        def _():
            for acc in (dsh, dsc, dg) + ((dgt,) if with_gate else ()):
                acc[...] = jnp.zeros_like(acc)

        xv, dhv = x_ref[...], dh_ref[...]
        r = lax.rsqrt(jnp.mean(xv * xv, axis=-1, keepdims=True) + EPS)
        xn = xv * r
        dsh[...] += _colsum8(dhv)
        dsc[...] += _colsum8(dhv * (xn * g_ref[...]))
        dhp = dhv * (1.0 + sc_ref[...])
        dg[...] += _colsum8(dhp * xn)
        dxn = dhp * g_ref[...]
        dx = r_ref[...] + r * (dxn - xn * jnp.mean(dxn * xn, axis=-1, keepdims=True))
        dx_ref[...] = dx
        if with_gate:
            da_ref[...] = (dx * gt_ref[...]).astype(BF16)
            dgt[...] += _colsum8(dx * e_ref[...])

    f32o, p8 = jax.ShapeDtypeStruct((s, D), F32), jax.ShapeDtypeStruct((SUBLANES, D), F32)
    if with_gate:
        ins, in_specs = (dh, xin, g, scale, resid, extra, gate), [row, row, _vec(D), _vec(D), row, row, _vec(D)]
        out_specs, out_shape = [row, row, acc8, acc8, acc8, acc8], [f32o, jax.ShapeDtypeStruct((s, D), BF16), p8, p8, p8, p8]
    else:
        ins, in_specs = (dh, xin, g, scale, resid), [row, row, _vec(D), _vec(D), row]
        out_specs, out_shape = [row, acc8, acc8, acc8], [f32o, p8, p8, p8]
    return pl.pallas_call(
        body, name=name, grid=(s // ts,), in_specs=in_specs, out_specs=out_specs, out_shape=out_shape,
        compiler_params=_params("arbitrary"),
    )(*ins)


def _group_norm_bwd(t, dn_out, gvec, bd):
    r = _group_rsqrt(t, bd)
    dg_terms = dn_out * t * r
    dn = dn_out * gvec
    dt = r * (dn - t * (r * r) * (_split_dot(dn * t, bd) * (1.0 / HD)))
    return dt, dg_terms


def _mix_bwd(dmix, proj, o, pw, pb, ps, ag, bd, *, ts, name):
    s = proj.shape[0]
    hb = ts // HALO
    nb = s // HALO

    def body(dm_ref, dmn_ref, u_ref, uh_ref, o_ref, pw_ref, pb_ref, ps_ref, ag_ref, bd_ref,
             du_ref, do_ref, dpw_ref, dpb_ref, dps_ref, dag_ref):
        i = pl.program_id(0)
        n = s // ts

        @pl.when(i == 0)
        def _():
            for acc in (dpw_ref, dpb_ref, dps_ref, dag_ref):
                acc[...] = jnp.zeros_like(acc)

        for g, w in enumerate(POOL_WINDOWS):
            cols = slice(g * LANES, (g + 1) * LANES)
            wg = pw_ref[g]
            psg = ps_ref[:, cols]
            pooled = _pooled(u_ref[:, cols], uh_ref[:, cols], i, g, w, ts).astype(BF16)
            dy = dm_ref[:, cols]
            dps_ref[:, cols] += _colsum8(dy * (_dot(pooled, wg) + pb_ref[:, cols]))
            dpre = dy * psg
            dpb_ref[:, cols] += _colsum8(dpre)
            dpreb = dpre.astype(BF16)
            dpw_ref[g * LANES:(g + 1) * LANES, :] += _dot(pooled, dpreb, TN)
            dpool = _dot(dpreb, wg, NT)
            dnext = _dot((dmn_ref[:, cols] * psg).astype(BF16), wg, NT)
            dpe = jnp.concatenate([dpool, jnp.where(i < n - 1, dnext, 0.0)], axis=0)
            tpos = i * ts + lax.broadcasted_iota(jnp.int32, (ts + HALO, 1), 0)
            acc = dpe / jnp.minimum(tpos + 1, w).astype(F32)
            span = 1
            while span < w:
                acc = acc + _shift_rows(acc, -span)
                span *= 2
            du_ref[:, cols] = acc[:ts] - dpool
        ov = o_ref[...]
        dov, dg_terms = _group_norm_bwd(ov, dm_ref[:, DP:], ag_ref[...], bd_ref[...])
        do_ref[...] = dov
        dag_ref[...] += _colsum8(dg_terms)

    p8 = jax.ShapeDtypeStruct((SUBLANES, DP), F32)
    acc8 = pl.BlockSpec((SUBLANES, DP), lambda i: (0, 0))
    half = pl.BlockSpec((ts, DP), lambda i: (i, 0))
    return pl.pallas_call(
        body, name=name, grid=(s // ts,),
        in_specs=[pl.BlockSpec((ts, D), lambda i: (i, 0)),
                  pl.BlockSpec((HALO, DP), lambda i: (jnp.minimum((i + 1) * hb, nb - 1), 0)),
                  half, pl.BlockSpec((HALO, DP), lambda i: (jnp.maximum(i * hb - 1, 0), 0)),
                  half, _full((4, LANES, LANES)), _vec(DP), _vec(DP), _vec(DA), _full((DA, DA))],
        out_specs=[half, half, _full((DP, LANES)), acc8, acc8, acc8],
        out_shape=[jax.ShapeDtypeStruct((s, DP), F32), jax.ShapeDtypeStruct((s, DA), F32),
                   jax.ShapeDtypeStruct((DP, LANES), F32), p8, p8, p8],
        compiler_params=_params("arbitrary"),
    )(dmix, dmix, proj, proj, o, pw, pb, ps, ag, bd)


def _qk_norm_bwd(du, dq, dk, dv, proj, qg, kg, bd, *, ts, name):
    s = proj.shape[0]

    def body(du_ref, dq_ref, dk_ref, dv_ref, q_ref, k_ref, qg_ref, kg_ref, bd_ref, dp_ref, dqg_ref, dkg_ref):
        i = pl.program_id(0)

        @pl.when(i == 0)
        def _():
            dqg_ref[...] = jnp.zeros_like(dqg_ref)
            dkg_ref[...] = jnp.zeros_like(dkg_ref)

        bdv = bd_ref[...]
        dqr, tq = _group_norm_bwd(q_ref[...], dq_ref[...], qg_ref[...], bdv)
        dkr, tk = _group_norm_bwd(k_ref[...], dk_ref[...], kg_ref[...], bdv)
        dqg_ref[...] += _colsum8(tq)
        dkg_ref[...] += _colsum8(tk)
        dp_ref[:, 0:DP] = du_ref[...].astype(BF16)
        dp_ref[:, DP:DP + DA] = dqr.astype(BF16)
        dp_ref[:, DP + DA:DP + 2 * DA] = dkr.astype(BF16)
        dp_ref[:, DP + 2 * DA:] = dv_ref[...].astype(BF16)

    half = pl.BlockSpec((ts, DA), lambda i: (i, 0))
    col = lambda j: pl.BlockSpec((ts, DA), lambda i: (i, j))
    acc8 = pl.BlockSpec((SUBLANES, DA), lambda i: (0, 0))
    p8 = jax.ShapeDtypeStruct((SUBLANES, DA), F32)
    return pl.pallas_call(
        body, name=name, grid=(s // ts,),
        in_specs=[half, half, half, half, col(1), col(2), _vec(DA), _vec(DA), _full((DA, DA))],
        out_specs=[pl.BlockSpec((ts, DIN), lambda i: (i, 0)), acc8, acc8],
        out_shape=[jax.ShapeDtypeStruct((s, DIN), BF16), p8, p8],
        compiler_params=_params("arbitrary"),
    )(du, dq, dk, dv, proj, proj, qg, kg, bd)


def _split3(a):
    hi = a.astype(BF16)
    return hi, (a - hi.astype(F32)).astype(BF16)


def _dot3(a, b, dn):
    ah, al = _split3(a)
    bh, bl = _split3(b)
    return _dot(ah, bh, dn) + (_dot(ah, bl, dn) + _dot(al, bh, dn))


def _ada_fwd(c_all, w, b, name):
    nw = w.shape[1]

    def body(c_ref, w_ref, b_ref, o_ref):
        cv = c_ref[...]
        act = cv / (1.0 + jnp.exp(-cv))
        o_ref[...] = _dot3(act, w_ref[...], NN) + b_ref[...]

    return pl.pallas_call(
        body, name=name, in_specs=[_full((NDEV, D)), _full(w.shape), _full((1, nw))], out_specs=_full((NDEV, nw)),
        out_shape=jax.ShapeDtypeStruct((NDEV, nw), F32), grid=(1,), compiler_params=_params("arbitrary"),
    )(c_all, w, b)


def _ada_bwd(c_all, dmod, name):
    nw = dmod.shape[1]

    def body(c_ref, d_ref, o_ref):
        cv = c_ref[...]
        act = cv / (1.0 + jnp.exp(-cv))
        o_ref[...] = _dot3(act, d_ref[...], TN)[None]

    return pl.pallas_call(
        body, name=name, in_specs=[_full((NDEV, D)), _full((NDEV, nw))], out_specs=_full((1, D, nw)),
        out_shape=jax.ShapeDtypeStruct((1, D, nw), F32), grid=(1,), compiler_params=_params("arbitrary"),
    )(c_all, dmod)


def _fold_heads(v):
    acc = v[:, 0:HD]
    for h in range(1, DA // HD):
        acc = acc + v[:, h * HD:(h + 1) * HD]
    return acc


def _small_update(gathered, gathered_pw, gathered_cw, specs, params, name):
    names = [sp[0] for sp in specs]
    flat = []
    for nme in names + ["pool_w", "conv_w"]:
        flat += list(params[nme])
    n_in = len(flat)

    def body(*refs):
        ga_ref, gp_ref, gc_ref = refs[0], refs[1], refs[2]
        prm = refs[3:3 + n_in]
        outs = refs[3 + n_in:]
        per_dev = [jnp.sum(ga_ref[dv], axis=0, keepdims=True) for dv in range(NDEV)]
        total = per_dev[0]
        for dv in range(1, NDEV):
            total = total + per_dev[dv]
        k = 0
        for idx, (nme, off, width, fold) in enumerate(specs):
            g = total[:, off:off + width]
            if fold:
                g = _fold_heads(g)
            w_ref, m_ref, v_ref = prm[3 * idx:3 * idx + 3]
            d, nm, nv = _adamw_math(w_ref[...], g, m_ref[...], v_ref[...])
            for val in (g, d, nm, nv):
                outs[k][...] = val
                k += 1
        gpw = gp_ref[0]
        for dv in range(1, NDEV):
            gpw = gpw + gp_ref[dv]
        w_ref, m_ref, v_ref = prm[3 * len(specs):3 * len(specs) + 3]
        d, nm, nv = _adamw_math(w_ref[...], gpw, m_ref[...], v_ref[...])
        for val in (gpw, d, nm, nv):
            outs[k][...] = val
            k += 1
        gcw = gc_ref[0]
        for dv in range(1, NDEV):
            gcw = gcw + gc_ref[dv]
        w_ref, m_ref, v_ref = prm[3 * len(specs) + 3:3 * len(specs) + 6]
        for tap in range(3):
            row = slice(tap, tap + 1)
            g = jnp.sum(gcw[SUBLANES * tap:SUBLANES * (tap + 1)], axis=0, keepdims=True)
            d, nm, nv = _adamw_math(w_ref[row, :], g, m_ref[row, :], v_ref[row, :])
            for q, val in enumerate((g, d, nm, nv)):
                outs[k + q][row, :] = val
        k += 4
        for dv in range(NDEV):
            outs[k][dv:dv + 1, :] = per_dev[dv][:, 0:6 * D]

    out_shape, out_specs = [], []
    for nme in names + ["pool_w", "conv_w"]:
        shp = params[nme][0].shape
        out_shape += [jax.ShapeDtypeStruct(shp, F32)] * 4
        out_specs += [_full(shp)] * 4
    out_shape.append(jax.ShapeDtypeStruct((NDEV, 6 * D), F32))
    out_specs.append(_full((NDEV, 6 * D)))
    res = pl.pallas_call(
        body, name=name, grid=(1,),
        in_specs=[_full(gathered.shape), _full(gathered_pw.shape), _full(gathered_cw.shape)] + [_full(a.shape) for a in flat],
        out_specs=out_specs, out_shape=out_shape, compiler_params=_params("arbitrary"),
    )(gathered, gathered_pw, gathered_cw, *flat)
    out = {nme: tuple(res[4 * i:4 * i + 4]) for i, nme in enumerate(names + ["pool_w", "conv_w"])}
    return out, res[-1]


def _row_tile(s):
    return 512 if s % 512 == 0 else s


def kernel(x, c, ada_w, ada_b, norm1_g, w_in, pool_w, pool_b, pool_scale, q_norm_g, k_norm_g, attn_out_g, w_out, norm2_g, w_up, conv_w, conv_b, w_down, loss_target, m_ada_w, m_ada_b, m_norm1_g, m_w_in, m_pool_w, m_pool_b, m_pool_scale, m_q_norm_g, m_k_norm_g, m_attn_out_g, m_w_out, m_norm2_g, m_w_up, m_conv_w, m_conv_b, m_w_down, v_ada_w, v_ada_b, v_norm1_g, v_w_in, v_pool_w, v_pool_b, v_pool_scale, v_q_norm_g, v_k_norm_g, v_attn_out_g, v_w_out, v_norm2_g, v_w_up, v_conv_w, v_conv_b, v_w_down):
    ax, ay, ac = lax.axis_index("x"), lax.axis_index("y"), lax.axis_index("c")
    me = 4 * ax + 2 * ay + ac
    me_swapped = 4 * ay + 2 * ax + ac
    xs, tgt = x[0], loss_target[0]
    s = xs.shape[0]
    ts = _row_tile(s)
    tq_attn, tk_attn = 512, 256
    tmm = 2 * ts
    bd = _block_diag_ones(DA, HD)

    c_all = _all_gather([jnp.broadcast_to(c, (SUBLANES, D))], [False], "gather_c")[0][:, 0, :]
    n_ada = ada_w.shape[2]
    ada_b_mine = lax.dynamic_slice_in_dim(ada_b, me * n_ada, n_ada, axis=1)
    mod_part = _ada_fwd(c_all, ada_w[0], ada_b_mine, "ada_fwd")
    mod_all = _all_gather([mod_part], [False], "gather_mod")[0]
    mod = lax.dynamic_index_in_dim(mod_all, me, axis=1, keepdims=False).reshape(1, 6 * D)
    shift1, scale1, gate1, shift2, scale2, gate2 = [mod[:, k * D:(k + 1) * D] for k in range(6)]

    w_in_t = w_in[0].T.astype(BF16)
    w_up_t = w_up[0].T.astype(BF16)
    gw_in, gcw = _all_gather([w_in_t, jnp.pad(conv_w[0], ((0, 5), (0, 64)))], [False, True], "gather_w_in")
    w_in_full = gw_in.reshape(DIN, D)
    later_w = [w_out[0].astype(BF16), w_up_t, w_down[0].astype(BF16)]
    cw_full = jnp.transpose(gcw[:, :3, :704], (1, 0, 2)).reshape(3, 2 * DFF)
    cb_full = jnp.transpose(conv_b.reshape(1, 2, 2, 2, 704), (0, 2, 1, 3, 4)).reshape(1, 2 * DFF)

    qg = jnp.tile(q_norm_g, (1, DA // HD))
    kg = jnp.tile(k_norm_g, (1, DA // HD))
    ag = attn_out_g.reshape(1, DA)
    pw = pool_w[0].astype(BF16)
    pb = pool_b.reshape(1, DP)
    h1 = _ln_mod(xs, norm1_g, scale1, shift1, ts=ts, name="ln1")
    proj = _matmul(h1, w_in_full, mode="nt", out_dtype=F32, tm=tmm,tn=DIN, tk=D, name="in_proj")
    qkv = _qk_norm(proj, qg, kg, bd, ts=ts, name="qk_norm")
    o_raw, m_tot, kb_first, (gw_out, gw_up, gw_down) = _attn_fwd(
        qkv, later_w, [False, True, False], tq=tq_attn, tk=tk_attn, name="attn_fwd")
    w_out_full = gw_out.reshape(D, D)
    w_up_full = gw_up.reshape(2 * DFF, D)
    w_down_full = gw_down.reshape(DFF, D)
    mix = _pool_mix(proj, o_raw, pw, pb, pool_scale, ag, bd, ts=ts, name="pool_mix")
    att = _matmul(mix, w_out_full, mode="nn", out_dtype=F32, tm=tmm,tn=D, tk=D, name="out_proj")
    x1, h2 = _res_ln_mod(xs, att, gate1, norm2_g, scale2, shift2, ts=ts, name="res_ln2")
    up = _matmul(h2, w_up_full, mode="nt", out_dtype=BF16, tm=tmm,tn=CF, tk=D, name="up_proj", n_outer=True)
    act = _conv_gate(up, cw_full, cb_full, ts=ts // 2, name="conv_gate")
    ffn = _matmul(act, w_down_full, mode="nn", out_dtype=F32, tm=tmm,tn=D, tk=DFF, name="down_proj")
    dy, dffn, dgate2_p, loss_p = _loss_head(x1, ffn, tgt, gate2, ts=ts, name="loss_head")
    loss = lax.psum(loss_p[0, 0], ("x", "y", "c"))

    da = _matmul(dffn, w_down_full, mode="nt", out_dtype=BF16, tm=tmm,tn=CF, tk=D, name="down_bwd")
    g_w_down = _matmul(act, dffn, mode="tn", out_dtype=F32, tm=CF, tn=D, tk=tmm,name="down_wgrad")
    dconv, dcb_p = _gate_bwd(da, up, cw_full, cb_full, ts=ts // 2, name="gate_bwd")
    dup, dcw_p = _conv_bwd(dconv, up, cw_full, ts=ts, tc=CF, name="conv_bwd")
    dh2 = _matmul(dup, w_up_full, mode="nn", out_dtype=F32, tm=tmm,tn=D, tk=CF, name="up_bwd")
    g_w_up_t = _matmul(dup, h2, mode="tn", out_dtype=F32, tm=CF, tn=D, tk=tmm,name="up_wgrad")
    dx1, datt, dshift2_p, dscale2_p, dnorm2_p, dgate1_p = _ln_mod_bwd(
        dh2, x1, norm2_g, scale2, dy, att, gate1, ts=ts, name="ln2_bwd")

    dmix = _matmul(datt, w_out_full, mode="nt", out_dtype=F32, tm=tmm,tn=D, tk=D, name="out_bwd")
    g_w_out = _matmul(mix, datt, mode="tn", out_dtype=F32, tm=D, tn=D, tk=tmm,name="out_wgrad")
    core = jnp.reshape(ac, (1,)).astype(jnp.int32)
    chip = jnp.reshape(2 * ax + ay, (1,)).astype(jnp.int32)
    big_ffn = [g_w_up_t.reshape(NDEV, 2 * DFF // NDEV, D), g_w_down.reshape(NDEV, DFF // NDEV, D),
               g_w_out.reshape(NDEV, D // NDEV, D)]
    swaps_ffn = [True, False, False]
    gots_ffn = _pair_exchange(big_ffn, swaps_ffn, "rs_pair_ffn")
    sums_ffn = [_pair_sum(big_ffn[k], gots_ffn[k], swaps_ffn[k], core, "rs_pair_sum_ffn%d" % k) for k in range(3)]
    du, do_raw, g_pw_p, dpb_p, dps_p, dag_p = _mix_bwd(dmix, proj, o_raw, pw, pb, pool_scale, ag, bd, ts=ts, name="mix_bwd")
    dqn, dkn, dvv, parts_ffn = _attn_bwd(qkv, do_raw, m_tot, kb_first, sums_ffn, tq=tq_attn, tk=tk_attn, name="attn_bwd")
    dproj, dqg_p, dkg_p = _qk_norm_bwd(du, dqn, dkn, dvv, proj, qg, kg, bd, ts=ts, name="qk_norm_bwd")
    dh1 = _matmul(dproj, w_in_full, mode="nn", out_dtype=F32, tm=tmm,tn=D, tk=DIN, name="in_bwd")
    g_w_in_t = _matmul(dproj, h1, mode="tn", out_dtype=F32, tm=DIN // 2, tn=D, tk=tmm,name="in_wgrad")
    grad_x, dshift1_p, dscale1_p, dnorm1_p = _ln_mod_bwd(dh1, xs, norm1_g, scale1, dx1, None, None, ts=ts, name="ln1_bwd")

    big = [g_w_in_t.reshape(NDEV, DIN // NDEV, D)]
    gots = _pair_exchange(big, [False], "rs_pair")
    sums = [_pair_sum(big[0], gots[0], False, core, "rs_pair_sum")]
    parts = _chip_exchange(sums, "rs_chip")
    tr = lambda a: a[0].T
    r_in = _adamw_reduce(tr(w_in), tr(m_w_in), tr(v_w_in), sums[0], parts[0], chip, "adamw_w_in")
    r_out = _adamw_reduce(w_out[0], m_w_out[0], v_w_out[0], sums_ffn[2], parts_ffn[2], chip, "adamw_w_out")
    r_up = _adamw_reduce(tr(w_up), tr(m_w_up), tr(v_w_up), sums_ffn[0], parts_ffn[0], chip, "adamw_w_up")
    r_down = _adamw_reduce(w_down[0], m_w_down[0], v_w_down[0], sums_ffn[1], parts_ffn[1], chip, "adamw_w_down")
    r_in = [a.T[None] for a in r_in]
    r_up = [a.T[None] for a in r_up]
    r_out = [a[None] for a in r_out]
    r_down = [a[None] for a in r_down]

    dcb_nat = jnp.transpose(dcb_p.reshape(SUBLANES, 2, 2, 2, 704), (0, 2, 1, 3, 4)).reshape(SUBLANES, 2 * DFF)
    pieces = [dshift1_p, dscale1_p, dgate1_p, dshift2_p, dscale2_p, dgate2_p,
              dnorm1_p, dnorm2_p, dcb_nat, dpb_p, dps_p, dag_p, dqg_p, dkg_p]
    packed = jnp.concatenate(pieces, axis=1)
    gathered, gathered_pw, gathered_cw = _all_gather([packed, g_pw_p, dcw_p], [False, False, False], "gather_small")
    gathered_cw = lax.dynamic_index_in_dim(gathered_cw.reshape(NDEV, 3 * SUBLANES, NDEV, 704), me_swapped, axis=2, keepdims=False)
    specs = [("ada_b", 0, 6 * D, False)]
    off = 6 * D
    for nme, width, fold in (("norm1_g", D, False), ("norm2_g", D, False), ("conv_b", 2 * DFF, False),
                             ("pool_b", DP, False), ("pool_scale", DP, False), ("attn_out_g", DA, False),
                             ("q_norm_g", DA, True), ("k_norm_g", DA, True)):
        specs.append((nme, off, width, fold))
        off += width
    small = {
        "ada_b": (ada_b, m_ada_b, v_ada_b),
        "norm1_g": (norm1_g, m_norm1_g, v_norm1_g), "norm2_g": (norm2_g, m_norm2_g, v_norm2_g),
        "conv_b": (conv_b, m_conv_b, v_conv_b),
        "pool_b": (pb, m_pool_b.reshape(1, DP), v_pool_b.reshape(1, DP)),
        "pool_scale": (pool_scale, m_pool_scale, v_pool_scale),
        "attn_out_g": (ag, m_attn_out_g.reshape(1, DA), v_attn_out_g.reshape(1, DA)),
        "q_norm_g": (q_norm_g, m_q_norm_g, v_q_norm_g), "k_norm_g": (k_norm_g, m_k_norm_g, v_k_norm_g),
        "pool_w": (pool_w.reshape(DP, LANES), m_pool_w.reshape(DP, LANES), v_pool_w.reshape(DP, LANES)),
        "conv_w": (conv_w[0], m_conv_w[0], v_conv_w[0]),
    }
    upd, dmod_all = _small_update(gathered, gathered_pw, gathered_cw, specs, small, "small_update")
    g_ada_w = _ada_bwd(c_all, lax.dynamic_slice_in_dim(dmod_all, me * n_ada, n_ada, axis=1), "ada_bwd")
    r_ada = [g_ada_w] + [a[None] for a in _adamw(ada_w[0], m_ada_w[0], v_ada_w[0], g_ada_w[0], "adamw_ada_w")]

    shapes = {"ada_b": ada_b.shape, "norm1_g": norm1_g.shape, "pool_w": pool_w.shape, "pool_b": pool_b.shape,
              "pool_scale": pool_scale.shape, "q_norm_g": q_norm_g.shape, "k_norm_g": k_norm_g.shape,
              "attn_out_g": attn_out_g.shape, "norm2_g": norm2_g.shape, "conv_w": conv_w.shape, "conv_b": conv_b.shape}
    res = {nme: [a.reshape(shapes[nme]) for a in upd[nme]] for nme in shapes}
    res.update(ada_w=r_ada, w_in=r_in, w_out=r_out, w_up=r_up, w_down=r_down)
    names = ["ada_w", "ada_b", "norm1_g", "w_in", "pool_w", "pool_b", "pool_scale", "q_norm_g", "k_norm_g",
             "attn_out_g", "w_out", "norm2_g", "w_up", "conv_w", "conv_b", "w_down"]
    outs = [loss, grad_x[None]]
    for q in range(4):
        outs += [res[nme][q] for nme in names]
    return tuple(outs)
```

```python
import functools
import math

import numpy as np
import jax
import jax.numpy as jnp
from jax import lax
from jax.experimental import pallas as pl
from jax.experimental.pallas import tpu as pltpu

F32, BF16 = jnp.float32, jnp.bfloat16
D = 1024
DP = 512
DA = 512
HD = 64
DIN = DP + 3 * DA
DFF = 2816
POOL_WINDOWS = (2, 4, 8, 16)
HALO = 16
EPS = 1e-6
LANES = 128
SUBLANES = 8
NDEV = 8
VMEM_LIMIT = 56 * 1024 * 1024
MESH = pl.DeviceIdType.MESH

ADAM_LR, ADAM_B1, ADAM_B2, ADAM_EPS, ADAM_WD, ADAM_STEP = 0.001, 0.9, 0.999, 1e-08, 0.01, 10

NN = (((1,), (0,)), ((), ()))
NT = (((1,), (1,)), ((), ()))
TN = (((0,), (0,)), ((), ()))


def _params(*sem):
    return pltpu.CompilerParams(dimension_semantics=sem, vmem_limit_bytes=VMEM_LIMIT)


def _full(shape):
    nd = len(shape)
    return pl.BlockSpec(shape, lambda *_: (0,) * nd)


def _dot(a, b, dn=NN):
    return lax.dot_general(a, b, dn, preferred_element_type=F32)


def _split_dot(a, b, dn=NN):
    hi = a.astype(BF16)
    lo = (a - hi.astype(F32)).astype(BF16)
    return _dot(hi, b, dn) + _dot(lo, b, dn)


def _colsum8(v):
    r, n = v.shape
    return v.reshape(r // SUBLANES, SUBLANES, n).sum(axis=0)


def _block_diag_ones(n, blk):
    i = np.arange(n) // blk
    return jnp.asarray((i[:, None] == i[None, :]).astype(np.float32), BF16)


def _matmul(a, b, *, mode, out_dtype, tm, tn, tk, name, n_outer=False):
    if mode == "tn":
        K, M = a.shape
        N = b.shape[1]
    elif mode == "nt":
        M, K = a.shape
        N = b.shape[0]
    else:
        M, K = a.shape
        N = b.shape[1]
    tm, tn, tk = min(tm, M), min(tn, N), min(tk, K)
    assert M % tm == 0 and N % tn == 0 and K % tk == 0, (name, M, N, K, tm, tn, tk)
    nk = K // tk
    dn = {"nn": NN, "nt": NT, "tn": TN}[mode]

    def body(a_ref, b_ref, o_ref, *acc):
        if nk == 1:
            o_ref[...] = _dot(a_ref[...], b_ref[...], dn).astype(o_ref.dtype)
            return
        acc_ref, = acc
        k = pl.program_id(2)

        @pl.when(k == 0)
        def _():
            acc_ref[...] = jnp.zeros_like(acc_ref)

        acc_ref[...] += _dot(a_ref[...], b_ref[...], dn)

        @pl.when(k == nk - 1)
        def _():
            o_ref[...] = acc_ref[...].astype(o_ref.dtype)

    if n_outer:
        gi = lambda g: (g[1], g[0], g[2])
        grid = (N // tn, M // tm, nk)
    else:
        gi = lambda g: g
        grid = (M // tm, N // tn, nk)

    def amap(*g):
        i, j, k = gi(g)
        return (k, i) if mode == "tn" else (i, k)

    def bmap(*g):
        i, j, k = gi(g)
        return (j, k) if mode == "nt" else (k, j)

    def omap(*g):
        i, j, k = gi(g)
        return (i, j)

    a_blk = (tk, tm) if mode == "tn" else (tm, tk)
    b_blk = (tn, tk) if mode == "nt" else (tk, tn)
    return pl.pallas_call(
        body, name=name, grid=grid,
        in_specs=[pl.BlockSpec(a_blk, amap), pl.BlockSpec(b_blk, bmap)],
        out_specs=pl.BlockSpec((tm, tn), omap),
        out_shape=jax.ShapeDtypeStruct((M, N), out_dtype),
        scratch_shapes=[] if nk == 1 else [pltpu.VMEM((tm, tn), F32)],
        compiler_params=_params("parallel", "parallel", "arbitrary"),
    )(a, b)


def _slot(swap, px, py, pc):
    return 4 * py + 2 * px + pc if swap else 4 * px + 2 * py + pc


class _Gather:
    def __init__(self, ins, outs, send, recv, loc, swaps):
        self.ins, self.outs, self.send, self.recv, self.loc, self.swaps = ins, outs, send, recv, loc, swaps
        x, y, c = lax.axis_index("x"), lax.axis_index("y"), lax.axis_index("c")
        self.me, self.sib = (x, y, c), (x, y, 1 - c)
        self.chips = [(1 - x, y), (x, 1 - y), (1 - x, 1 - y)]
        self.n = len(ins)

    @staticmethod
    def scratch(n):
        return [pltpu.SemaphoreType.DMA((7 * n,)), pltpu.SemaphoreType.DMA((7 * n,)), pltpu.SemaphoreType.DMA((n,))]

    def copy(self, a, k, blk, to, src=None):
        rows = self.outs[a].at[_slot(self.swaps[a], *blk)]
        return pltpu.make_async_remote_copy(
            src_ref=rows if src is None else src, dst_ref=rows,
            send_sem=self.send.at[7 * a + k], recv_sem=self.recv.at[7 * a + k], device_id=to, device_id_type=MESH)

    def mine(self, a):
        return pltpu.make_async_copy(self.ins[a], self.outs[a].at[_slot(self.swaps[a], *self.me)], self.loc.at[a])

    def first(self, a):
        c = self.me[2]
        return [self.copy(a, 0, self.me, self.sib, src=self.ins[a])] + [
            self.copy(a, 1 + j, self.me, (*chip, c), src=self.ins[a]) for j, chip in enumerate(self.chips)]

    def forwards(self, a):
        c = self.me[2]
        return [self.copy(a, 4 + j, (*chip, c), self.sib) for j, chip in enumerate(self.chips)]

    def start(self):
        for a in range(self.n):
            self.mine(a).start()
        for a in range(self.n):
            for cp in self.first(a):
                cp.start()

    def forward(self):
        c = self.me[2]
        for a in range(self.n):
            fwd = self.forwards(a)
            for j, chip in enumerate(self.chips):
                self.copy(a, 1 + j, (*chip, c), self.me).wait_recv()
                fwd[j].start()

    def finish(self):
        c = self.me[2]
        for a in range(self.n):
            self.copy(a, 0, self.sib, self.me).wait_recv()
            for j, chip in enumerate(self.chips):
                self.copy(a, 4 + j, (*chip, 1 - c), self.me).wait_recv()
        for a in range(self.n):
            for cp in self.first(a) + self.forwards(a):
                cp.wait_send()
            self.mine(a).wait()


def _all_gather(arrs, swaps, name):
    n = len(arrs)

    def body(*refs):
        g = _Gather(refs[:n], refs[n:2 * n], *refs[2 * n:], swaps)
        g.start()
        g.forward()
        g.finish()

    any_spec = pl.BlockSpec(memory_space=pl.ANY)
    return pl.pallas_call(
        body, name=name,
        in_specs=[any_spec] * n, out_specs=[any_spec] * n,
        out_shape=[jax.ShapeDtypeStruct((NDEV,) + a.shape, a.dtype) for a in arrs],
        scratch_shapes=_Gather.scratch(n),
    )(*arrs)


def _pair_copies(ins, gots, send, recv, swaps):
    x, y, c = lax.axis_index("x"), lax.axis_index("y"), lax.axis_index("c")
    return [pltpu.make_async_remote_copy(
        src_ref=ins[a].at[_slot(swaps[a], k // 2, k % 2, 1 - c)], dst_ref=gots[a].at[k],
        send_sem=send.at[4 * a + k], recv_sem=recv.at[4 * a + k], device_id=(x, y, 1 - c), device_id_type=MESH)
        for a in range(len(ins)) for k in range(4)]


def _pair_exchange(arrs, swaps, name):
    n = len(arrs)

    def body(*refs):
        rems = _pair_copies(refs[:n], refs[n:2 * n], *refs[2 * n:], swaps)
        for rc in rems:
            rc.start()
        for rc in rems:
            rc.wait_recv()
        for rc in rems:
            rc.wait_send()

    any_spec = pl.BlockSpec(memory_space=pl.ANY)
    return pl.pallas_call(
        body, name=name,
        in_specs=[any_spec] * n, out_specs=[any_spec] * n,
        out_shape=[jax.ShapeDtypeStruct((4,) + a.shape[1:], a.dtype) for a in arrs],
        scratch_shapes=[pltpu.SemaphoreType.DMA((4 * n,)), pltpu.SemaphoreType.DMA((4 * n,))],
    )(*arrs)


def _chip_copies(ins, outs, send, recv):
    x, y, c = lax.axis_index("x"), lax.axis_index("y"), lax.axis_index("c")
    chips = [(1 - x, y), (x, 1 - y), (1 - x, 1 - y)]
    return [pltpu.make_async_remote_copy(
        src_ref=ins[a].at[2 * px + py], dst_ref=outs[a].at[j], send_sem=send.at[3 * a + j], recv_sem=recv.at[3 * a + j],
        device_id=(px, py, c), device_id_type=MESH) for a in range(len(ins)) for j, (px, py) in enumerate(chips)]


def _chip_exchange(arrs, name):
    n = len(arrs)

    def body(*refs):
        rems = _chip_copies(refs[:n], refs[n:2 * n], *refs[2 * n:])
        for rc in rems:
            rc.start()
        for rc in rems:
            rc.wait_recv()
        for rc in rems:
            rc.wait_send()

    any_spec = pl.BlockSpec(memory_space=pl.ANY)
    return pl.pallas_call(
        body, name=name,
        in_specs=[any_spec] * n, out_specs=[any_spec] * n,
        out_shape=[jax.ShapeDtypeStruct((3,) + a.shape[1:], a.dtype) for a in arrs],
        scratch_shapes=[pltpu.SemaphoreType.DMA((3 * n,)), pltpu.SemaphoreType.DMA((3 * n,))],
    )(*arrs)


class _Side:
    def __init__(self, arrs, out_shapes, n_copies, make):
        self.arrs, self.out_shapes, self.n_copies, self.make = list(arrs), list(out_shapes), n_copies, make


def _pair_side(arrs, swaps):
    return _Side(arrs, [jax.ShapeDtypeStruct((4,) + a.shape[1:], a.dtype) for a in arrs], 4 * len(arrs),
                 functools.partial(_pair_copies, swaps=swaps))


def _chip_side(arrs):
    return _Side(arrs, [jax.ShapeDtypeStruct((3,) + a.shape[1:], a.dtype) for a in arrs], 3 * len(arrs), _chip_copies)


def _row_call(body, side, *, name, steps, in_specs, out_specs, out_shape, ins):
    if side is None:
        res = pl.pallas_call(body, name=name, grid=(steps,), in_specs=in_specs, out_specs=out_specs, out_shape=out_shape,
                             compiler_params=_params("arbitrary"))(*ins)
        return list(res), []
    n_in, n_out, ne = len(in_specs), len(out_specs), len(side.arrs)

    def wrapped(*refs):
        e_in = refs[n_in:n_in + ne]
        e_out = refs[n_in + ne + n_out:n_in + 2 * ne + n_out]
        sems = refs[n_in + 2 * ne + n_out:]
        i = pl.program_id(0)

        @pl.when(i == 0)
        def _():
            for cp in side.make(e_in, e_out, *sems):
                cp.start()

        body(*refs[:n_in], *refs[n_in + ne:n_in + ne + n_out])

        @pl.when(i == steps - 1)
        def _():
            cps = side.make(e_in, e_out, *sems)
            for cp in cps:
                cp.wait_recv()
            for cp in cps:
                cp.wait_send()

    any_spec = pl.BlockSpec(memory_space=pl.ANY)
    res = pl.pallas_call(
        wrapped, name=name, grid=(steps,), in_specs=list(in_specs) + [any_spec] * ne,
        out_specs=list(out_specs) + [any_spec] * ne, out_shape=list(out_shape) + side.out_shapes,
        scratch_shapes=[pltpu.SemaphoreType.DMA((side.n_copies,)), pltpu.SemaphoreType.DMA((side.n_copies,))],
        compiler_params=_params("arbitrary"))(*ins, *side.arrs)
    return list(res[:n_out]), list(res[n_out:])


def _pair_sum(grads, got, swap, core, name):
    _, r, c = got.shape
    tr = r if r <= 352 else r // 2

    def own_map(k, i, core_ref):
        return (_slot(swap, k // 2, k % 2, core_ref[0]), i, 0)

    def body(core_ref, a_ref, b_ref, o_ref):
        o_ref[...] = a_ref[...] + b_ref[...]

    spec = pl.BlockSpec((None, tr, c), lambda k, i, core_ref: (k, i, 0))
    return pl.pallas_call(
        body, name=name,
        grid_spec=pltpu.PrefetchScalarGridSpec(
            num_scalar_prefetch=1, grid=(4, r // tr),
            in_specs=[pl.BlockSpec((None, tr, c), own_map), spec], out_specs=spec),
        out_shape=jax.ShapeDtypeStruct(got.shape, got.dtype), compiler_params=_params("parallel", "parallel"),
    )(core, grads, got)


def _adamw_math(w, g, m, v):
    m = ADAM_B1 * m + (1.0 - ADAM_B1) * g
    v = ADAM_B2 * v + (1.0 - ADAM_B2) * (g * g)
    m_hat = m / (1.0 - ADAM_B1 ** ADAM_STEP)
    v_hat = v / (1.0 - ADAM_B2 ** ADAM_STEP)
    delta = -ADAM_LR * (m_hat / (jnp.sqrt(v_hat) + ADAM_EPS) + ADAM_WD * w)
    return delta, m, v


def _adamw_tile(r):
    for cand in (256, 352, 128):
        if r % cand == 0:
            return cand
    return r


def _adamw(w, m, v, g, name):
    r, c = w.shape
    tr = _adamw_tile(r)
    spec = pl.BlockSpec((tr, c), lambda i: (i, 0))

    def body(w_ref, m_ref, v_ref, g_ref, d_ref, nm_ref, nv_ref):
        d_ref[...], nm_ref[...], nv_ref[...] = _adamw_math(w_ref[...], g_ref[...], m_ref[...], v_ref[...])

    out = jax.ShapeDtypeStruct((r, c), F32)
    return pl.pallas_call(
        body, name=name, grid=(r // tr,), in_specs=[spec] * 4, out_specs=[spec] * 3, out_shape=[out] * 3,
        compiler_params=_params("parallel"),
    )(w, m, v, g)


def _adamw_reduce(w, m, v, sums, recv, chip, name):
    r, c = w.shape
    tr = _adamw_tile(r)
    spec = pl.BlockSpec((tr, c), lambda i, chip_ref: (i, 0))

    def body(chip_ref, w_ref, m_ref, v_ref, s_ref, p_ref, g_ref, d_ref, nm_ref, nv_ref):
        g = ((s_ref[...] + p_ref[0]) + p_ref[1]) + p_ref[2]
        g_ref[...] = g
        d_ref[...], nm_ref[...], nv_ref[...] = _adamw_math(w_ref[...], g, m_ref[...], v_ref[...])

    out = jax.ShapeDtypeStruct((r, c), F32)
    return pl.pallas_call(
        body, name=name,
        grid_spec=pltpu.PrefetchScalarGridSpec(
            num_scalar_prefetch=1, grid=(r // tr,),
            in_specs=[spec, spec, spec, pl.BlockSpec((None, tr, c), lambda i, chip_ref: (chip_ref[0], i, 0)),
                      pl.BlockSpec((3, tr, c), lambda i, chip_ref: (0, i, 0))],
            out_specs=[spec] * 4),
        out_shape=[out] * 4, compiler_params=_params("parallel"),
    )(chip, w, m, v, sums, recv)


def _vec(n):
    return pl.BlockSpec((1, n), lambda *_: (0, 0))


def _ln_mod(x, g, scale, shift, *, ts, name):
    s = x.shape[0]
    row = pl.BlockSpec((ts, D), lambda i: (i, 0))

    def body(x_ref, g_ref, sc_ref, sh_ref, h_ref):
        xv = x_ref[...]
        r = lax.rsqrt(jnp.mean(xv * xv, axis=-1, keepdims=True) + EPS)
        h = (xv * r) * g_ref[...]
        h_ref[...] = (h * (1.0 + sc_ref[...]) + sh_ref[...]).astype(BF16)

    return pl.pallas_call(
        body, name=name, grid=(s // ts,), in_specs=[row, _vec(D), _vec(D), _vec(D)], out_specs=row,
        out_shape=jax.ShapeDtypeStruct((s, D), BF16), compiler_params=_params("parallel"),
    )(x, g, scale, shift)


def _group_rsqrt(t, bd):
    return lax.rsqrt(_split_dot(t * t, bd) * (1.0 / HD) + EPS)


def _qk_norm(proj, qg, kg, bd, *, ts, name):
    s = proj.shape[0]

    def body(q_ref, k_ref, v_ref, qg_ref, kg_ref, bd_ref, o_ref):
        bdv = bd_ref[...]
        q, k = q_ref[...], k_ref[...]
        o_ref[:, 0:DA] = (q * _group_rsqrt(q, bdv) * qg_ref[...]).astype(BF16)
        o_ref[:, DA:2 * DA] = (k * _group_rsqrt(k, bdv) * kg_ref[...]).astype(BF16)
        o_ref[:, 2 * DA:] = v_ref[...].astype(BF16)

    col = lambda j: pl.BlockSpec((ts, DA), lambda i: (i, j))
    return pl.pallas_call(
        body, name=name, grid=(s // ts,),
        in_specs=[col(1), col(2), col(3), _vec(DA), _vec(DA), _full((DA, DA))],
        out_specs=pl.BlockSpec((ts, 3 * DA), lambda i: (i, 0)),
        out_shape=jax.ShapeDtypeStruct((s, 3 * DA), BF16), compiler_params=_params("parallel"),
    )(proj, proj, proj, qg, kg, bd)


EXP_UNDERFLOW = -120.0


def _log_terms(z):
    neg_abs = lax.bitcast_convert_type(lax.bitcast_convert_type(z, jnp.uint32) | jnp.uint32(0x80000000), F32)
    b = jnp.minimum(z, 0.0) - jnp.log(1.0 + jnp.exp(neg_abs))
    return b, b - z


def _head_masks(rows):
    lane = lax.broadcasted_iota(jnp.int32, (rows, LANES), 1)
    return [lane < HD, lane >= HD]


def _attn_fwd(qkv, gather, swaps, *, tq, tk, name):
    s = qkv.shape[0]
    nrep = tk // LANES
    ndiag = tq // tk
    ng = len(gather)
    npair, nq = DA // LANES, s // tq

    def body(*refs):
        q_ref, k_ref, v_ref = refs[:3]
        g_in = refs[3:3 + ng]
        o_ref, tot_ref, first_ref = refs[3 + ng:6 + ng]
        g_out = refs[6 + ng:6 + 2 * ng]
        oacc, rc = refs[6 + 2 * ng:8 + 2 * ng]
        g_sems = refs[8 + 2 * ng:]
        i = pl.program_id(1)
        step_id = pl.program_id(0) * nq + i

        @pl.when(step_id == 0)
        def _():
            _Gather(g_in, g_out, *g_sems, swaps).start()

        @pl.when(step_id == (npair * nq * 3) // 4)
        def _():
            _Gather(g_in, g_out, *g_sems, swaps).forward()

        heads = _head_masks(tq)
        q = q_ref[...] * 0.125
        qs = [jnp.where(h, q, 0.0).astype(BF16) for h in heads]
        dif = lax.broadcasted_iota(jnp.int32, (tq, tk), 0) - lax.broadcasted_iota(jnp.int32, (tq, tk), 1)
        kr = lax.broadcasted_iota(jnp.int32, (tk, tk), 0)
        kc = lax.broadcasted_iota(jnp.int32, (tk, tk), 1)
        later = jnp.where(kr > kc, 1.0, 0.0).astype(BF16)
        oacc[...] = jnp.zeros_like(oacc)
        rc[...] = jnp.zeros_like(rc)

        def tile(kb, thr):
            rows = pl.ds(pl.multiple_of(kb * tk, tk), tk)
            k = k_ref[rows, :]
            v = v_ref[rows, :]
            rcv = [rc[0], rc[1]]
            zs = [_dot(qs[a], k, NT) for a in range(2)]
            bs, mbs = [], []
            for a in range(2):
                b, m = _log_terms(zs[a])
                if thr is not None:
                    m = jnp.where(dif > thr, m, 0.0)
                bs.append(b)
                mbs.append(m.astype(BF16))
            rl = [_dot(mbs[a], later) for a in range(2)]
            for a in range(2):
                p = jnp.exp(bs[a] + (rl[a] + jnp.tile(rcv[a], (1, nrep))))
                if thr is not None:
                    p = jnp.where(dif > thr, p, 0.0)
                oacc[a] += _dot(p.astype(BF16), v)
                rc[a] = rcv[a] + (rl[a][:, 0:1] + mbs[a][:, 0:1].astype(F32))

        for d in reversed(range(ndiag)):
            tile(i * ndiag + d, d * tk)

        def live():
            return jnp.max(jnp.maximum(rc[0], rc[1])) > EXP_UNDERFLOW

        def step(carry):
            kb, _ = carry
            tile(kb, None)
            return kb - 1, live()

        kb_end, _ = lax.while_loop(lambda cr: jnp.logical_and(cr[0] >= 0, cr[1]), step, (i * ndiag - 1, live()))
        first_ref[pl.program_id(0), i] = (kb_end + 1).astype(F32)
        o_ref[...] = jnp.where(heads[0], oacc[0], oacc[1])
        tot_ref[...] = jnp.where(heads[0], rc[0], rc[1])

        @pl.when(step_id == npair * nq - 1)
        def _():
            _Gather(g_in, g_out, *g_sems, swaps).finish()

    qspec = pl.BlockSpec((tq, LANES), lambda p, i: (i, p))
    any_spec = pl.BlockSpec(memory_space=pl.ANY)
    res = pl.pallas_call(
        body, name=name, grid=(npair, nq),
        in_specs=[qspec,
                  pl.BlockSpec((s, LANES), lambda p, i: (0, DA // LANES + p)),
                  pl.BlockSpec((s, LANES), lambda p, i: (0, 2 * DA // LANES + p))] + [any_spec] * ng,
        out_specs=[qspec, qspec, pl.BlockSpec(memory_space=pltpu.SMEM)] + [any_spec] * ng,
        out_shape=[jax.ShapeDtypeStruct((s, DA), F32), jax.ShapeDtypeStruct((s, DA), F32),
                   jax.ShapeDtypeStruct((npair, nq), F32)]
        + [jax.ShapeDtypeStruct((NDEV,) + a.shape, a.dtype) for a in gather],
        scratch_shapes=[pltpu.VMEM((2, tq, LANES), F32), pltpu.VMEM((2, tq, LANES), F32)] + _Gather.scratch(ng),
        compiler_params=_params("arbitrary", "arbitrary"),
    )(qkv, qkv, qkv, *gather)
    return res[0], res[1], res[2], res[3:]


def _attn_bwd(qkv, do, tot, first, exchange, *, tq, tk, name):
    s = qkv.shape[0]
    nrep = tk // LANES
    ndiag = tq // tk
    ne = len(exchange)
    npair, nq = DA // LANES, s // tq

    def body(*refs):
        q_ref, k_ref, v_ref, do_ref, tot_ref, first_ref = refs[:6]
        e_in = refs[6:6 + ne]
        dq_ref, dk_ref, dv_ref = refs[6 + ne:9 + ne]
        e_out = refs[9 + ne:9 + 2 * ne]
        dqacc, rem, gc = refs[9 + 2 * ne:12 + 2 * ne]
        e_sems = refs[12 + 2 * ne:]
        i = pl.program_id(1)
        step_id = pl.program_id(0) * nq + i

        @pl.when(step_id == 0)
        def _():
            for cp in _chip_copies(e_in, e_out, *e_sems):
                cp.start()

        @pl.when(i == 0)
        def _():
            dk_ref[...] = jnp.zeros_like(dk_ref)
            dv_ref[...] = jnp.zeros_like(dv_ref)

        heads = _head_masks(tq)
        q = q_ref[...] * 0.125
        qs = [jnp.where(h, q, 0.0).astype(BF16) for h in heads]
        dov = do_ref[...]
        dob = [jnp.where(h, dov, 0.0).astype(BF16) for h in heads]
        dif = lax.broadcasted_iota(jnp.int32, (tq, tk), 0) - lax.broadcasted_iota(jnp.int32, (tq, tk), 1)
        kr = lax.broadcasted_iota(jnp.int32, (tk, tk), 0)
        kc = lax.broadcasted_iota(jnp.int32, (tk, tk), 1)
        up_incl = jnp.where(kr <= kc, 1.0, 0.0).astype(BF16)
        up_strict = jnp.where(kr < kc, 1.0, 0.0).astype(BF16)
        dqacc[...] = jnp.zeros_like(dqacc)
        gc[...] = jnp.zeros_like(gc)
        totv = tot_ref[...]
        swapped = pltpu.roll(totv, HD, axis=1)
        rem[0] = jnp.where(heads[0], totv, swapped)
        rem[1] = jnp.where(heads[1], totv, swapped)

        def tile(kb, thr):
            rows = pl.ds(pl.multiple_of(kb * tk, tk), tk)
            k = k_ref[rows, :]
            v = v_ref[rows, :]
            remv = [rem[0], rem[1]]
            gcv = [gc[0], gc[1]]
            zs = [_dot(qs[a], k, NT) for a in range(2)]
            das = [_dot(dob[a], v, NT) for a in range(2)]
            bs, mbs = [], []
            for a in range(2):
                b, m = _log_terms(zs[a])
                if thr is not None:
                    m = jnp.where(dif > thr, m, 0.0)
                bs.append(b)
                mbs.append(m.astype(BF16))
            pl_ = [_dot(mbs[a], up_incl) for a in range(2)]
            ps, gs, gbs = [], [], []
            for a in range(2):
                p = jnp.exp(bs[a] + (jnp.tile(remv[a], (1, nrep)) - pl_[a]))
                if thr is not None:
                    p = jnp.where(dif > thr, p, 0.0)
                g = p * das[a]
                ps.append(p.astype(BF16))
                gs.append(g)
                gbs.append(g.astype(BF16))
            cl = [_dot(gbs[a], up_strict) for a in range(2)]
            dk_add = jnp.zeros((tk, LANES), F32)
            dv_add = jnp.zeros((tk, LANES), F32)
            for a in range(2):
                dz = gs[a] - jnp.exp(bs[a]) * (gs[a] + (jnp.tile(gcv[a], (1, nrep)) + cl[a]))
                if thr is not None:
                    dz = jnp.where(dif > thr, dz, 0.0)
                dzb = dz.astype(BF16)
                dqacc[a] += _dot(dzb, k)
                dk_add += _dot(dzb, qs[a], TN)
                dv_add += _dot(ps[a], dob[a], TN)
                rem[a] = remv[a] - pl_[a][:, tk - 1:tk]
                gc[a] = gcv[a] + (cl[a][:, tk - 1:tk] + gbs[a][:, tk - 1:tk].astype(F32))
            dk_ref[rows, :] += dk_add
            dv_ref[rows, :] += dv_add

        def step(kb, carry):
            tile(kb, None)
            return carry

        lax.fori_loop(first_ref[pl.program_id(0), i].astype(jnp.int32), i * ndiag, step, 0)
        for d in range(ndiag):
            tile(i * ndiag + d, d * tk)
        dq_ref[...] = jnp.where(heads[0], dqacc[0], dqacc[1]) * 0.125

        @pl.when(step_id == npair * nq - 1)
        def _():
            cps = _chip_copies(e_in, e_out, *e_sems)
            for cp in cps:
                cp.wait_recv()
            for cp in cps:
                cp.wait_send()

    qspec = pl.BlockSpec((tq, LANES), lambda p, i: (i, p))
    full = pl.BlockSpec((s, LANES), lambda p, i: (0, p))
    any_spec = pl.BlockSpec(memory_space=pl.ANY)
    out = jax.ShapeDtypeStruct((s, DA), F32)
    res = pl.pallas_call(
        body, name=name, grid=(npair, nq),
        in_specs=[qspec, pl.BlockSpec((s, LANES), lambda p, i: (0, DA // LANES + p)),
                  pl.BlockSpec((s, LANES), lambda p, i: (0, 2 * DA // LANES + p)), qspec, qspec,
                  pl.BlockSpec(memory_space=pltpu.SMEM)] + [any_spec] * ne,
        out_specs=[qspec, full, full] + [any_spec] * ne,
        out_shape=[out, out, out] + [jax.ShapeDtypeStruct((3,) + a.shape[1:], a.dtype) for a in exchange],
        scratch_shapes=[pltpu.VMEM((2, tq, LANES), F32)] * 3
        + [pltpu.SemaphoreType.DMA((3 * ne,)), pltpu.SemaphoreType.DMA((3 * ne,))],
        compiler_params=_params("arbitrary", "arbitrary"),
    )(qkv, qkv, qkv, do, tot, first, *exchange)
    return res[0], res[1], res[2], res[3:]


def _shift_rows(v, k):
    return pltpu.roll(v, k % v.shape[0], axis=0)


def _pooled(u, uh, i, g, w, ts):
    halo = jnp.where(i > 0, uh, 0.0)
    ue = jnp.concatenate([halo, u], axis=0)
    acc, span = ue, 1
    while span < w:
        acc = acc + _shift_rows(acc, span)
        span *= 2
    tpos = i * ts + lax.broadcasted_iota(jnp.int32, (ts, 1), 0)
    cnt = jnp.minimum(tpos + 1, w).astype(F32)
    return acc[HALO:] / cnt - u


def _pool_mix(proj, o, pw, pb, ps, ag, bd, *, ts, name):
    s = proj.shape[0]
    hb = ts // HALO

    def body(u_ref, uh_ref, o_ref, pw_ref, pb_ref, ps_ref, ag_ref, bd_ref, mix_ref):
        i = pl.program_id(0)
        for g, w in enumerate(POOL_WINDOWS):
            cols = slice(g * LANES, (g + 1) * LANES)
            pooled = _pooled(u_ref[:, cols], uh_ref[:, cols], i, g, w, ts)
            yv = (_dot(pooled.astype(BF16), pw_ref[g]) + pb_ref[:, cols]) * ps_ref[:, cols]
            mix_ref[:, cols] = yv.astype(BF16)
        ov = o_ref[...]
        mix_ref[:, DP:] = (ov * _group_rsqrt(ov, bd_ref[...]) * ag_ref[...]).astype(BF16)

    return pl.pallas_call(
        body, name=name, grid=(s // ts,),
        in_specs=[pl.BlockSpec((ts, DP), lambda i: (i, 0)),
                  pl.BlockSpec((HALO, DP), lambda i: (jnp.maximum(i * hb - 1, 0), 0)),
                  pl.BlockSpec((ts, DA), lambda i: (i, 0)),
                  _full((4, LANES, LANES)), _vec(DP), _vec(DP), _vec(DA), _full((DA, DA))],
        out_specs=pl.BlockSpec((ts, D), lambda i: (i, 0)),
        out_shape=jax.ShapeDtypeStruct((s, D), BF16), compiler_params=_params("parallel"),
    )(proj, proj, o, pw, pb, ps, ag, bd)


def _res_ln_mod(x, att, gate, g, scale, shift, *, ts, name):
    s = x.shape[0]
    row = pl.BlockSpec((ts, D), lambda i: (i, 0))

    def body(x_ref, a_ref, gt_ref, g_ref, sc_ref, sh_ref, x1_ref, h_ref):
        x1 = x_ref[...] + gt_ref[...] * a_ref[...]
        x1_ref[...] = x1
        r = lax.rsqrt(jnp.mean(x1 * x1, axis=-1, keepdims=True) + EPS)
        h = (x1 * r) * g_ref[...]
        h_ref[...] = (h * (1.0 + sc_ref[...]) + sh_ref[...]).astype(BF16)

    return pl.pallas_call(
        body, name=name, grid=(s // ts,), in_specs=[row, row] + [_vec(D)] * 4, out_specs=[row, row],
        out_shape=[jax.ShapeDtypeStruct((s, D), F32), jax.ShapeDtypeStruct((s, D), BF16)],
        compiler_params=_params("parallel"),
    )(x, att, gate, g, scale, shift)


CF = DFF // 2


def _conv(u, uh, w_ref, b_ref, i):
    halo = jnp.where(i > 0, uh.astype(F32), 0.0)
    ue = jnp.concatenate([halo, u.astype(F32)], axis=0)
    y = w_ref[2:3, :] * ue + w_ref[1:2, :] * _shift_rows(ue, 1) + w_ref[0:1, :] * _shift_rows(ue, 2)
    return y[HALO:] + b_ref[...]


def _conv_gate(up, cw, cb, *, ts, name):
    s = up.shape[0]
    hb = ts // HALO

    def body(u_ref, uh_ref, w_ref, b_ref, a_ref, c_ref):
        i = pl.program_id(0)
        c = _conv(u_ref[...], uh_ref[...], w_ref, b_ref, i)
        gt, vl = c[:, :CF], c[:, CF:]
        a_ref[...] = (gt / (1.0 + jnp.exp(-gt)) * vl).astype(BF16)
        c_ref[...] = c.astype(BF16)

    return pl.pallas_call(
        body, name=name, grid=(s // ts, 2),
        in_specs=[pl.BlockSpec((ts, 2 * CF), lambda i, j: (i, j)),
                  pl.BlockSpec((HALO, 2 * CF), lambda i, j: (jnp.maximum(i * hb - 1, 0), j)),
                  pl.BlockSpec((3, 2 * CF), lambda i, j: (0, j)), pl.BlockSpec((1, 2 * CF), lambda i, j: (0, j))],
        out_specs=[pl.BlockSpec((ts, CF), lambda i, j: (i, j)), pl.BlockSpec((ts, 2 * CF), lambda i, j: (i, j))],
        out_shape=[jax.ShapeDtypeStruct((s, DFF), BF16), jax.ShapeDtypeStruct((s, 2 * DFF), BF16)],
        compiler_params=_params("parallel", "parallel"),
    )(up, up, cw, cb)


def _loss_head(x1, ffn, tgt, gate2, *, ts, name):
    s = x1.shape[0]
    n = s // ts
    row = pl.BlockSpec((ts, D), lambda i: (i, 0))
    acc8 = pl.BlockSpec((SUBLANES, D), lambda i: (0, 0))

    def body(x_ref, f_ref, t_ref, g_ref, dy_ref, df_ref, dg_ref, loss_ref, lacc):
        i = pl.program_id(0)

        @pl.when(i == 0)
        def _():
            lacc[...] = jnp.zeros_like(lacc)
            dg_ref[...] = jnp.zeros_like(dg_ref)

        f = f_ref[...]
        diff = x_ref[...] + g_ref[...] * f - t_ref[...]
        lacc[...] += _colsum8(diff * diff)
        dy = diff * (1.0 / D)
        dy_ref[...] = dy
        df_ref[...] = (dy * g_ref[...]).astype(BF16)
        dg_ref[...] += _colsum8(dy * f)

        @pl.when(i == n - 1)
        def _():
            loss_ref[...] = jnp.full((SUBLANES, LANES), (0.5 / D) * jnp.sum(lacc[...]), F32)

    return pl.pallas_call(
        body, name=name, grid=(n,), in_specs=[row, row, row, _vec(D)],
        out_specs=[row, row, acc8, _full((SUBLANES, LANES))],
        out_shape=[jax.ShapeDtypeStruct((s, D), F32), jax.ShapeDtypeStruct((s, D), BF16),
                   jax.ShapeDtypeStruct((SUBLANES, D), F32), jax.ShapeDtypeStruct((SUBLANES, LANES), F32)],
        scratch_shapes=[pltpu.VMEM((SUBLANES, D), F32)], compiler_params=_params("arbitrary"),
    )(x1, ffn, tgt, gate2)


def _gate_bwd(da, conv, *, ts, name):
    s = conv.shape[0]

    def body(da_ref, c_ref, d_ref, db_ref):
        i = pl.program_id(1)

        @pl.when(i == 0)
        def _():
            db_ref[...] = jnp.zeros_like(db_ref)

        gt, vl = c_ref[:, :CF].astype(F32), c_ref[:, CF:].astype(F32)
        sg = 1.0 / (1.0 + jnp.exp(-gt))
        dav = da_ref[...].astype(F32)
        dgt = dav * vl * (sg * (1.0 + gt * (1.0 - sg)))
        dvl = dav * (gt * sg)
        d_ref[:, :CF] = dgt.astype(BF16)
        d_ref[:, CF:] = dvl.astype(BF16)
        db_ref[:, :CF] += _colsum8(dgt)
        db_ref[:, CF:] += _colsum8(dvl)

    return pl.pallas_call(
        body, name=name, grid=(2, s // ts),
        in_specs=[pl.BlockSpec((ts, CF), lambda j, i: (i, j)),
                  pl.BlockSpec((ts, 2 * CF), lambda j, i: (i, j))],
        out_specs=[pl.BlockSpec((ts, 2 * CF), lambda j, i: (i, j)),
                   pl.BlockSpec((SUBLANES, 2 * CF), lambda j, i: (0, j))],
        out_shape=[jax.ShapeDtypeStruct((s, 2 * DFF), BF16), jax.ShapeDtypeStruct((SUBLANES, 2 * DFF), F32)],
        compiler_params=_params("parallel", "arbitrary"),
    )(da, conv)


def _conv_bwd(dc, up, cw, *, ts, tc, name):
    s = up.shape[0]
    hb = ts // HALO
    nb = s // HALO

    def body(d_ref, dn_ref, u_ref, w_ref, du_ref, dw_ref):
        i = pl.program_id(1)
        n = s // ts

        @pl.when(i == 0)
        def _():
            dw_ref[...] = jnp.zeros_like(dw_ref)

        dcur = d_ref[...].astype(F32)
        nxt = jnp.where(i < n - 1, dn_ref[...].astype(F32), 0.0)
        de = jnp.concatenate([dcur, nxt], axis=0)
        d1 = _shift_rows(de, -1)[:ts]
        d2 = _shift_rows(de, -2)[:ts]
        du_ref[...] = (w_ref[2:3, :] * dcur + w_ref[1:2, :] * d1 + w_ref[0:1, :] * d2).astype(BF16)
        u = u_ref[...].astype(F32)
        dw_ref[16:24, :] += _colsum8(dcur * u)
        dw_ref[8:16, :] += _colsum8(d1 * u)
        dw_ref[0:8, :] += _colsum8(d2 * u)

    return pl.pallas_call(
        body, name=name, grid=(2 * DFF // tc, s // ts),
        in_specs=[pl.BlockSpec((ts, tc), lambda j, i: (i, j)),
                  pl.BlockSpec((HALO, tc), lambda j, i: (jnp.minimum((i + 1) * hb, nb - 1), j)),
                  pl.BlockSpec((ts, tc), lambda j, i: (i, j)),
                  pl.BlockSpec((3, tc), lambda j, i: (0, j))],
        out_specs=[pl.BlockSpec((ts, tc), lambda j, i: (i, j)), pl.BlockSpec((24, tc), lambda j, i: (0, j))],
        out_shape=[jax.ShapeDtypeStruct((s, 2 * DFF), BF16), jax.ShapeDtypeStruct((24, 2 * DFF), F32)],
        compiler_params=_params("parallel", "arbitrary"),
    )(dc, dc, up, cw)


def _ln_mod_bwd(dh, xin, g, scale, resid, extra, gate, *, ts, name, side=None):
    s = xin.shape[0]
    row = pl.BlockSpec((ts, D), lambda i: (i, 0))
    acc8 = pl.BlockSpec((SUBLANES, D), lambda i: (0, 0))
    with_gate = extra is not None

    def body(*refs):
        if with_gate:
            dh_ref, x_ref, g_ref, sc_ref, r_ref, e_ref, gt_ref, dx_ref, da_ref, dsh, dsc, dg, dgt = refs
        else:
            dh_ref, x_ref, g_ref, sc_ref, r_ref, dx_ref, dsh, dsc, dg = refs
        i = pl.program_id(0)

        @pl.when(i == 0)
        def _():
            for acc in (dsh, dsc, dg) + ((dgt,) if with_gate else ()):
                acc[...] = jnp.zeros_like(acc)

        xv, dhv = x_ref[...], dh_ref[...]
        r = lax.rsqrt(jnp.mean(xv * xv, axis=-1, keepdims=True) + EPS)
        xn = xv * r
        dsh[...] += _colsum8(dhv)
        dsc[...] += _colsum8(dhv * (xn * g_ref[...]))
        dhp = dhv * (1.0 + sc_ref[...])
        dg[...] += _colsum8(dhp * xn)
        dxn = dhp * g_ref[...]
        dx = r_ref[...] + r * (dxn - xn * jnp.mean(dxn * xn, axis=-1, keepdims=True))
        dx_ref[...] = dx
        if with_gate:
            da_ref[...] = (dx * gt_ref[...]).astype(BF16)
            dgt[...] += _colsum8(dx * e_ref[...])

    f32o, p8 = jax.ShapeDtypeStruct((s, D), F32), jax.ShapeDtypeStruct((SUBLANES, D), F32)
    if with_gate:
        ins, in_specs = (dh, xin, g, scale, resid, extra, gate), [row, row, _vec(D), _vec(D), row, row, _vec(D)]
        out_specs, out_shape = [row, row, acc8, acc8, acc8, acc8], [f32o, jax.ShapeDtypeStruct((s, D), BF16), p8, p8, p8, p8]
    else:
        ins, in_specs = (dh, xin, g, scale, resid), [row, row, _vec(D), _vec(D), row]
        out_specs, out_shape = [row, acc8, acc8, acc8], [f32o, p8, p8, p8]
    return _row_call(body, side, name=name, steps=s // ts, in_specs=in_specs, out_specs=out_specs,
                     out_shape=out_shape, ins=ins)


def _group_norm_bwd(t, dn_out, gvec, bd):
    r = _group_rsqrt(t, bd)
    dg_terms = dn_out * t * r
    dn = dn_out * gvec
    dt = r * (dn - t * (r * r) * (_split_dot(dn * t, bd) * (1.0 / HD)))
    return dt, dg_terms


def _mix_bwd(dmix, proj, o, pw, pb, ps, ag, bd, *, ts, name, side=None):
    s = proj.shape[0]
    hb = ts // HALO
    nb = s // HALO

    def body(dm_ref, dmn_ref, u_ref, uh_ref, o_ref, pw_ref, pb_ref, ps_ref, ag_ref, bd_ref,
             du_ref, do_ref, dpw_ref, dpb_ref, dps_ref, dag_ref):
        i = pl.program_id(0)
        n = s // ts

        @pl.when(i == 0)
        def _():
            for acc in (dpw_ref, dpb_ref, dps_ref, dag_ref):
                acc[...] = jnp.zeros_like(acc)

        for g, w in enumerate(POOL_WINDOWS):
            cols = slice(g * LANES, (g + 1) * LANES)
            wg = pw_ref[g]
            psg = ps_ref[:, cols]
            pooled = _pooled(u_ref[:, cols], uh_ref[:, cols], i, g, w, ts).astype(BF16)
            dy = dm_ref[:, cols]
            dps_ref[:, cols] += _colsum8(dy * (_dot(pooled, wg) + pb_ref[:, cols]))
            dpre = dy * psg
            dpb_ref[:, cols] += _colsum8(dpre)
            dpreb = dpre.astype(BF16)
            dpw_ref[g * LANES:(g + 1) * LANES, :] += _dot(pooled, dpreb, TN)
            dpool = _dot(dpreb, wg, NT)
            dnext = _dot((dmn_ref[:, cols] * psg).astype(BF16), wg, NT)
            dpe = jnp.concatenate([dpool, jnp.where(i < n - 1, dnext, 0.0)], axis=0)
            tpos = i * ts + lax.broadcasted_iota(jnp.int32, (ts + HALO, 1), 0)
            acc = dpe / jnp.minimum(tpos + 1, w).astype(F32)
            span = 1
            while span < w:
                acc = acc + _shift_rows(acc, -span)
                span *= 2
            du_ref[:, cols] = acc[:ts] - dpool
        ov = o_ref[...]
        dov, dg_terms = _group_norm_bwd(ov, dm_ref[:, DP:], ag_ref[...], bd_ref[...])
        do_ref[...] = dov
        dag_ref[...] += _colsum8(dg_terms)

    p8 = jax.ShapeDtypeStruct((SUBLANES, DP), F32)
    acc8 = pl.BlockSpec((SUBLANES, DP), lambda i: (0, 0))
    half = pl.BlockSpec((ts, DP), lambda i: (i, 0))
    return _row_call(
        body, side, name=name, steps=s // ts,
        in_specs=[pl.BlockSpec((ts, D), lambda i: (i, 0)),
                  pl.BlockSpec((HALO, DP), lambda i: (jnp.minimum((i + 1) * hb, nb - 1), 0)),
                  half, pl.BlockSpec((HALO, DP), lambda i: (jnp.maximum(i * hb - 1, 0), 0)),
                  half, _full((4, LANES, LANES)), _vec(DP), _vec(DP), _vec(DA), _full((DA, DA))],
        out_specs=[half, half, _full((DP, LANES)), acc8, acc8, acc8],
        out_shape=[jax.ShapeDtypeStruct((s, DP), F32), jax.ShapeDtypeStruct((s, DA), F32),
                   jax.ShapeDtypeStruct((DP, LANES), F32), p8, p8, p8],
        ins=(dmix, dmix, proj, proj, o, pw, pb, ps, ag, bd))


def _qk_norm_bwd(du, dq, dk, dv, proj, qg, kg, bd, *, ts, name):
    s = proj.shape[0]

    def body(du_ref, dq_ref, dk_ref, dv_ref, q_ref, k_ref, qg_ref, kg_ref, bd_ref, dp_ref, dqg_ref, dkg_ref):
        i = pl.program_id(0)

        @pl.when(i == 0)
        def _():
            dqg_ref[...] = jnp.zeros_like(dqg_ref)
            dkg_ref[...] = jnp.zeros_like(dkg_ref)

        bdv = bd_ref[...]
        dqr, tq = _group_norm_bwd(q_ref[...], dq_ref[...], qg_ref[...], bdv)
        dkr, tk = _group_norm_bwd(k_ref[...], dk_ref[...], kg_ref[...], bdv)
        dqg_ref[...] += _colsum8(tq)
        dkg_ref[...] += _colsum8(tk)
        dp_ref[:, 0:DP] = du_ref[...].astype(BF16)
        dp_ref[:, DP:DP + DA] = dqr.astype(BF16)
        dp_ref[:, DP + DA:DP + 2 * DA] = dkr.astype(BF16)
        dp_ref[:, DP + 2 * DA:] = dv_ref[...].astype(BF16)

    half = pl.BlockSpec((ts, DA), lambda i: (i, 0))
    col = lambda j: pl.BlockSpec((ts, DA), lambda i: (i, j))
    acc8 = pl.BlockSpec((SUBLANES, DA), lambda i: (0, 0))
    p8 = jax.ShapeDtypeStruct((SUBLANES, DA), F32)
    return pl.pallas_call(
        body, name=name, grid=(s // ts,),
        in_specs=[half, half, half, half, col(1), col(2), _vec(DA), _vec(DA), _full((DA, DA))],
        out_specs=[pl.BlockSpec((ts, DIN), lambda i: (i, 0)), acc8, acc8],
        out_shape=[jax.ShapeDtypeStruct((s, DIN), BF16), p8, p8],
        compiler_params=_params("arbitrary"),
    )(du, dq, dk, dv, proj, proj, qg, kg, bd)


def _split3(a):
    hi = a.astype(BF16)
    return hi, (a - hi.astype(F32)).astype(BF16)


def _dot3(a, b, dn):
    ah, al = _split3(a)
    bh, bl = _split3(b)
    return _dot(ah, bh, dn) + (_dot(ah, bl, dn) + _dot(al, bh, dn))


def _ada_fwd(c_all, w, b, name):
    nw = w.shape[1]

    def body(c_ref, w_ref, b_ref, o_ref):
        cv = c_ref[...]
        act = cv / (1.0 + jnp.exp(-cv))
        o_ref[...] = _dot3(act, w_ref[...], NN) + b_ref[...]

    return pl.pallas_call(
        body, name=name, in_specs=[_full((NDEV, D)), _full(w.shape), _full((1, nw))], out_specs=_full((NDEV, nw)),
        out_shape=jax.ShapeDtypeStruct((NDEV, nw), F32), grid=(1,), compiler_params=_params("arbitrary"),
    )(c_all, w, b)


def _ada_bwd(c_all, dmod, name):
    nw = dmod.shape[1]

    def body(c_ref, d_ref, o_ref):
        cv = c_ref[...]
        act = cv / (1.0 + jnp.exp(-cv))
        o_ref[...] = _dot3(act, d_ref[...], TN)[None]

    return pl.pallas_call(
        body, name=name, in_specs=[_full((NDEV, D)), _full((NDEV, nw))], out_specs=_full((1, D, nw)),
        out_shape=jax.ShapeDtypeStruct((1, D, nw), F32), grid=(1,), compiler_params=_params("arbitrary"),
    )(c_all, dmod)


def _fold_heads(v):
    acc = v[:, 0:HD]
    for h in range(1, DA // HD):
        acc = acc + v[:, h * HD:(h + 1) * HD]
    return acc


def _pack_partials(pieces, dcw_p, name):
    n_p = len(pieces)
    total = sum(p.shape[1] for p in pieces) + 3 * dcw_p.shape[1]
    npack = -(-total // (SUBLANES * LANES)) * (SUBLANES * LANES)

    def body(*refs):
        out = refs[-1]
        off = 0
        for r in refs[:n_p]:
            out[:, off:off + r.shape[1]] = jnp.sum(r[...], axis=0, keepdims=True)
            off += r.shape[1]
        dw = refs[n_p]
        for tap in range(3):
            out[:, off:off + dw.shape[1]] = jnp.sum(dw[SUBLANES * tap:SUBLANES * (tap + 1), :], axis=0, keepdims=True)
            off += dw.shape[1]
        if off < npack:
            out[:, off:] = jnp.zeros((1, npack - off), F32)

    arrs = list(pieces) + [dcw_p]
    return pl.pallas_call(
        body, name=name, grid=(1,), in_specs=[_full(a.shape) for a in arrs], out_specs=_full((1, npack)),
        out_shape=jax.ShapeDtypeStruct((1, npack), F32), compiler_params=_params("arbitrary"),
    )(*arrs)


def _small_update(gathered, gathered_pw, gathered_cw, specs, params, name):
    names = [sp[0] for sp in specs]
    flat = []
    for nme in names + ["pool_w", "conv_w"]:
        flat += list(params[nme])
    n_in = len(flat)

    def body(*refs):
        ga_ref, gp_ref, gc_ref = refs[0], refs[1], refs[2]
        prm = refs[3:3 + n_in]
        outs = refs[3 + n_in:]
        total = ga_ref[0:1, :]
        for dv in range(1, NDEV):
            total = total + ga_ref[dv:dv + 1, :]
        k = 0
        for idx, (nme, off, width, fold) in enumerate(specs):
            g = total[:, off:off + width]
            if fold:
                g = _fold_heads(g)
            w_ref, m_ref, v_ref = prm[3 * idx:3 * idx + 3]
            d, nm, nv = _adamw_math(w_ref[...], g, m_ref[...], v_ref[...])
            for val in (g, d, nm, nv):
                outs[k][...] = val
                k += 1
        gpw = gp_ref[0]
        for dv in range(1, NDEV):
            gpw = gpw + gp_ref[dv]
        w_ref, m_ref, v_ref = prm[3 * len(specs):3 * len(specs) + 3]
        d, nm, nv = _adamw_math(w_ref[...], gpw, m_ref[...], v_ref[...])
        for val in (gpw, d, nm, nv):
            outs[k][...] = val
            k += 1
        gcw = gc_ref[0]
        for dv in range(1, NDEV):
            gcw = gcw + gc_ref[dv]
        w_ref, m_ref, v_ref = prm[3 * len(specs) + 3:3 * len(specs) + 6]
        d, nm, nv = _adamw_math(w_ref[...], gcw, m_ref[...], v_ref[...])
        for val in (gcw, d, nm, nv):
            outs[k][...] = val
            k += 1
        outs[k][...] = ga_ref[:, 0:6 * D]

    out_shape, out_specs = [], []
    for nme in names + ["pool_w", "conv_w"]:
        shp = params[nme][0].shape
        out_shape += [jax.ShapeDtypeStruct(shp, F32)] * 4
        out_specs += [_full(shp)] * 4
    out_shape.append(jax.ShapeDtypeStruct((NDEV, 6 * D), F32))
    out_specs.append(_full((NDEV, 6 * D)))
    res = pl.pallas_call(
        body, name=name, grid=(1,),
        in_specs=[_full(gathered.shape), _full(gathered_pw.shape), _full(gathered_cw.shape)] + [_full(a.shape) for a in flat],
        out_specs=out_specs, out_shape=out_shape, compiler_params=_params("arbitrary"),
    )(gathered, gathered_pw, gathered_cw, *flat)
    out = {nme: tuple(res[4 * i:4 * i + 4]) for i, nme in enumerate(names + ["pool_w", "conv_w"])}
    return out, res[-1]


def _row_tile(s):
    return 512 if s % 512 == 0 else s


def kernel(x, c, ada_w, ada_b, norm1_g, w_in, pool_w, pool_b, pool_scale, q_norm_g, k_norm_g, attn_out_g, w_out, norm2_g, w_up, conv_w, conv_b, w_down, loss_target, m_ada_w, m_ada_b, m_norm1_g, m_w_in, m_pool_w, m_pool_b, m_pool_scale, m_q_norm_g, m_k_norm_g, m_attn_out_g, m_w_out, m_norm2_g, m_w_up, m_conv_w, m_conv_b, m_w_down, v_ada_w, v_ada_b, v_norm1_g, v_w_in, v_pool_w, v_pool_b, v_pool_scale, v_q_norm_g, v_k_norm_g, v_attn_out_g, v_w_out, v_norm2_g, v_w_up, v_conv_w, v_conv_b, v_w_down):
    ax, ay, ac = lax.axis_index("x"), lax.axis_index("y"), lax.axis_index("c")
    me = 4 * ax + 2 * ay + ac
    me_swapped = 4 * ay + 2 * ax + ac
    xs, tgt = x[0], loss_target[0]
    s = xs.shape[0]
    ts = _row_tile(s)
    tq_attn, tk_attn = 512, 256
    tmm = 2 * ts
    bd = _block_diag_ones(DA, HD)

    w_in_t = w_in[0].T.astype(BF16)
    w_up_t = w_up[0].T.astype(BF16)
    c_gath, gw_in, gcw = _all_gather([jnp.broadcast_to(c, (SUBLANES, D)), w_in_t, jnp.pad(conv_w[0], ((0, 5), (0, 64)))],
                                     [False, False, True], "gather_in")
    c_all = c_gath[:, 0, :]
    n_ada = ada_w.shape[2]
    ada_b_mine = lax.dynamic_slice_in_dim(ada_b, me * n_ada, n_ada, axis=1)
    mod_part = _ada_fwd(c_all, ada_w[0], ada_b_mine, "ada_fwd")
    mod_all = _all_gather([mod_part], [False], "gather_mod")[0]
    mod = lax.dynamic_index_in_dim(mod_all, me, axis=1, keepdims=False).reshape(1, 6 * D)
    shift1, scale1, gate1, shift2, scale2, gate2 = [mod[:, k * D:(k + 1) * D] for k in range(6)]

    w_in_full = gw_in.reshape(DIN, D)
    later_w = [w_out[0].astype(BF16), w_up_t, w_down[0].astype(BF16)]
    cw_full = jnp.transpose(gcw[:, :3, :704], (1, 0, 2)).reshape(3, 2 * DFF)
    cb_full = jnp.transpose(conv_b.reshape(1, 2, 2, 2, 704), (0, 2, 1, 3, 4)).reshape(1, 2 * DFF)

    qg = jnp.tile(q_norm_g, (1, DA // HD))
    kg = jnp.tile(k_norm_g, (1, DA // HD))
    ag = attn_out_g.reshape(1, DA)
    pw = pool_w[0].astype(BF16)
    pb = pool_b.reshape(1, DP)
    h1 = _ln_mod(xs, norm1_g, scale1, shift1, ts=ts, name="ln1")
    proj = _matmul(h1, w_in_full, mode="nt", out_dtype=F32, tm=tmm,tn=DIN, tk=D, name="in_proj")
    qkv = _qk_norm(proj, qg, kg, bd, ts=ts, name="qk_norm")
    o_raw, m_tot, kb_first, (gw_out, gw_up, gw_down) = _attn_fwd(
        qkv, later_w, [False, True, False], tq=tq_attn, tk=tk_attn, name="attn_fwd")
    w_out_full = gw_out.reshape(D, D)
    w_up_full = gw_up.reshape(2 * DFF, D)
    w_down_full = gw_down.reshape(DFF, D)
    mix = _pool_mix(proj, o_raw, pw, pb, pool_scale, ag, bd, ts=ts, name="pool_mix")
    att = _matmul(mix, w_out_full, mode="nn", out_dtype=F32, tm=tmm,tn=D, tk=D, name="out_proj")
    x1, h2 = _res_ln_mod(xs, att, gate1, norm2_g, scale2, shift2, ts=ts, name="res_ln2")
    up = _matmul(h2, w_up_full, mode="nt", out_dtype=BF16, tm=tmm,tn=CF, tk=D, name="up_proj", n_outer=True)
    act, conv = _conv_gate(up, cw_full, cb_full, ts=ts // 2, name="conv_gate")
    ffn = _matmul(act, w_down_full, mode="nn", out_dtype=F32, tm=tmm,tn=D, tk=DFF, name="down_proj")
    dy, dffn, dgate2_p, loss_p = _loss_head(x1, ffn, tgt, gate2, ts=ts, name="loss_head")
    loss = lax.psum(loss_p[0, 0], ("x", "y", "c"))

    da = _matmul(dffn, w_down_full, mode="nt", out_dtype=BF16, tm=tmm,tn=CF, tk=D, name="down_bwd")
    g_w_down = _matmul(act, dffn, mode="tn", out_dtype=F32, tm=CF, tn=D, tk=tmm,name="down_wgrad")
    dconv, dcb_p = _gate_bwd(da, conv, ts=ts // 2, name="gate_bwd")
    dup, dcw_p = _conv_bwd(dconv, up, cw_full, ts=ts, tc=CF, name="conv_bwd")
    dh2 = _matmul(dup, w_up_full, mode="nn", out_dtype=F32, tm=tmm,tn=D, tk=CF, name="up_bwd")
    g_w_up_t = _matmul(dup, h2, mode="tn", out_dtype=F32, tm=CF, tn=D, tk=tmm,name="up_wgrad")
    (dx1, datt, dshift2_p, dscale2_p, dnorm2_p, dgate1_p), _ = _ln_mod_bwd(
        dh2, x1, norm2_g, scale2, dy, att, gate1, ts=ts, name="ln2_bwd")

    dmix = _matmul(datt, w_out_full, mode="nt", out_dtype=F32, tm=tmm,tn=D, tk=D, name="out_bwd")
    g_w_out = _matmul(mix, datt, mode="tn", out_dtype=F32, tm=D, tn=D, tk=tmm,name="out_wgrad")
    core = jnp.reshape(ac, (1,)).astype(jnp.int32)
    chip = jnp.reshape(2 * ax + ay, (1,)).astype(jnp.int32)
    big_ffn = [g_w_up_t.reshape(NDEV, 2 * DFF // NDEV, D), g_w_down.reshape(NDEV, DFF // NDEV, D),
               g_w_out.reshape(NDEV, D // NDEV, D)]
    swaps_ffn = [True, False, False]
    (du, do_raw, g_pw_p, dpb_p, dps_p, dag_p), gots_ffn = _mix_bwd(
        dmix, proj, o_raw, pw, pb, pool_scale, ag, bd, ts=ts, name="mix_bwd", side=_pair_side(big_ffn, swaps_ffn))
    sums_ffn = [_pair_sum(big_ffn[k], gots_ffn[k], swaps_ffn[k], core, "rs_pair_sum_ffn%d" % k) for k in range(3)]
    dqn, dkn, dvv, parts_ffn = _attn_bwd(qkv, do_raw, m_tot, kb_first, sums_ffn, tq=tq_attn, tk=tk_attn, name="attn_bwd")
    dproj, dqg_p, dkg_p = _qk_norm_bwd(du, dqn, dkn, dvv, proj, qg, kg, bd, ts=ts, name="qk_norm_bwd")
    g_w_in_t = _matmul(dproj, h1, mode="tn", out_dtype=F32, tm=DIN // 2, tn=D, tk=tmm,name="in_wgrad")
    big = [g_w_in_t.reshape(NDEV, DIN // NDEV, D)]
    gots = _pair_exchange(big, [False], "rs_pair")
    sums = [_pair_sum(big[0], gots[0], False, core, "rs_pair_sum")]
    dh1 = _matmul(dproj, w_in_full, mode="nn", out_dtype=F32, tm=tmm,tn=D, tk=DIN, name="in_bwd")
    (grad_x, dshift1_p, dscale1_p, dnorm1_p), parts = _ln_mod_bwd(
        dh1, xs, norm1_g, scale1, dx1, None, None, ts=ts, name="ln1_bwd", side=_chip_side(sums))

    tr = lambda a: a[0].T
    r_in = _adamw_reduce(tr(w_in), tr(m_w_in), tr(v_w_in), sums[0], parts[0], chip, "adamw_w_in")
    r_out = _adamw_reduce(w_out[0], m_w_out[0], v_w_out[0], sums_ffn[2], parts_ffn[2], chip, "adamw_w_out")
    r_up = _adamw_reduce(tr(w_up), tr(m_w_up), tr(v_w_up), sums_ffn[0], parts_ffn[0], chip, "adamw_w_up")
    r_down = _adamw_reduce(w_down[0], m_w_down[0], v_w_down[0], sums_ffn[1], parts_ffn[1], chip, "adamw_w_down")
    r_in = [a.T[None] for a in r_in]
    r_up = [a.T[None] for a in r_up]
    r_out = [a[None] for a in r_out]
    r_down = [a[None] for a in r_down]

    dcb_nat = jnp.transpose(dcb_p.reshape(SUBLANES, 2, 2, 2, 704), (0, 2, 1, 3, 4)).reshape(SUBLANES, 2 * DFF)
    pieces = [dshift1_p, dscale1_p, dgate1_p, dshift2_p, dscale2_p, dgate2_p,
              dnorm1_p, dnorm2_p, dcb_nat, dpb_p, dps_p, dag_p, dqg_p, dkg_p]
    n_vec = sum(p.shape[1] for p in pieces)
    packed = _pack_partials(pieces, dcw_p, "pack_partials")
    npack = packed.shape[1]
    gathered, gathered_pw = _all_gather([packed.reshape(SUBLANES, npack // SUBLANES), g_pw_p], [False, False], "gather_small")
    gathered = gathered.reshape(NDEV, npack)
    gathered_cw = lax.dynamic_index_in_dim(
        gathered[:, n_vec:n_vec + 6 * DFF].reshape(NDEV, 3, NDEV, 704), me_swapped, axis=2, keepdims=False)
    specs = [("ada_b", 0, 6 * D, False)]
    off = 6 * D
    for nme, width, fold in (("norm1_g", D, False), ("norm2_g", D, False), ("conv_b", 2 * DFF, False),
                             ("pool_b", DP, False), ("pool_scale", DP, False), ("attn_out_g", DA, False),
                             ("q_norm_g", DA, True), ("k_norm_g", DA, True)):
        specs.append((nme, off, width, fold))
        off += width
    small = {
        "ada_b": (ada_b, m_ada_b, v_ada_b),
        "norm1_g": (norm1_g, m_norm1_g, v_norm1_g), "norm2_g": (norm2_g, m_norm2_g, v_norm2_g),
        "conv_b": (conv_b, m_conv_b, v_conv_b),
        "pool_b": (pb, m_pool_b.reshape(1, DP), v_pool_b.reshape(1, DP)),
        "pool_scale": (pool_scale, m_pool_scale, v_pool_scale),
        "attn_out_g": (ag, m_attn_out_g.reshape(1, DA), v_attn_out_g.reshape(1, DA)),
        "q_norm_g": (q_norm_g, m_q_norm_g, v_q_norm_g), "k_norm_g": (k_norm_g, m_k_norm_g, v_k_norm_g),
        "pool_w": (pool_w.reshape(DP, LANES), m_pool_w.reshape(DP, LANES), v_pool_w.reshape(DP, LANES)),
        "conv_w": (conv_w[0], m_conv_w[0], v_conv_w[0]),
    }
    upd, dmod_all = _small_update(gathered, gathered_pw, gathered_cw, specs, small, "small_update")
    g_ada_w = _ada_bwd(c_all, lax.dynamic_slice_in_dim(dmod_all, me * n_ada, n_ada, axis=1), "ada_bwd")
    r_ada = [g_ada_w] + [a[None] for a in _adamw(ada_w[0], m_ada_w[0], v_ada_w[0], g_ada_w[0], "adamw_ada_w")]

    shapes = {"ada_b": ada_b.shape, "norm1_g": norm1_g.shape, "pool_w": pool_w.shape, "pool_b": pool_b.shape,
              "pool_scale": pool_scale.shape, "q_norm_g": q_norm_g.shape, "k_norm_g": k_norm_g.shape,
              "attn_out_g": attn_out_g.shape, "norm2_g": norm2_g.shape, "conv_w": conv_w.shape, "conv_b": conv_b.shape}
    res = {nme: [a.reshape(shapes[nme]) for a in upd[nme]] for nme in shapes}
    res.update(ada_w=r_ada, w_in=r_in, w_out=r_out, w_up=r_up, w_down=r_down)
    names = ["ada_w", "ada_b", "norm1_g", "w_in", "pool_w", "pool_b", "pool_scale", "q_norm_g", "k_norm_g",
             "attn_out_g", "w_out", "norm2_g", "w_up", "conv_w", "conv_b", "w_down"]
    outs = [loss, grad_x[None]]
    for q in range(4):
        outs += [res[nme][q] for nme in names]
    return tuple(outs)
```

```python
import functools
import math

import numpy as np
import jax
import jax.numpy as jnp
from jax import lax
from jax.experimental import pallas as pl
from jax.experimental.pallas import tpu as pltpu

F32, BF16 = jnp.float32, jnp.bfloat16
D = 1024
DP = 512
DA = 512
HD = 64
DIN = DP + 3 * DA
DFF = 2816
POOL_WINDOWS = (2, 4, 8, 16)
HALO = 16
EPS = 1e-6
LANES = 128
SUBLANES = 8
NDEV = 8
VMEM_LIMIT = 56 * 1024 * 1024
MESH = pl.DeviceIdType.MESH

ADAM_LR, ADAM_B1, ADAM_B2, ADAM_EPS, ADAM_WD, ADAM_STEP = 0.001, 0.9, 0.999, 1e-08, 0.01, 10

NN = (((1,), (0,)), ((), ()))
NT = (((1,), (1,)), ((), ()))
TN = (((0,), (0,)), ((), ()))


def _params(*sem):
    return pltpu.CompilerParams(dimension_semantics=sem, vmem_limit_bytes=VMEM_LIMIT)


def _full(shape):
    nd = len(shape)
    return pl.BlockSpec(shape, lambda *_: (0,) * nd)


def _dot(a, b, dn=NN):
    return lax.dot_general(a, b, dn, preferred_element_type=F32)


def _split_dot(a, b, dn=NN):
    hi = a.astype(BF16)
    lo = (a - hi.astype(F32)).astype(BF16)
    return _dot(hi, b, dn) + _dot(lo, b, dn)


def _colsum8(v):
    r, n = v.shape
    return v.reshape(r // SUBLANES, SUBLANES, n).sum(axis=0)


def _block_diag_ones(n, blk):
    i = np.arange(n) // blk
    return jnp.asarray((i[:, None] == i[None, :]).astype(np.float32), BF16)


def _matmul(a, b, *, mode, out_dtype, tm, tn, tk, name, n_outer=False, side=None):
    if mode == "tn":
        K, M = a.shape
        N = b.shape[1]
    elif mode == "nt":
        M, K = a.shape
        N = b.shape[0]
    else:
        M, K = a.shape
        N = b.shape[1]
    tm, tn, tk = min(tm, M), min(tn, N), min(tk, K)
    assert M % tm == 0 and N % tn == 0 and K % tk == 0, (name, M, N, K, tm, tn, tk)
    nk = K // tk
    dn = {"nn": NN, "nt": NT, "tn": TN}[mode]

    def body(a_ref, b_ref, o_ref, *acc):
        if nk == 1:
            o_ref[...] = _dot(a_ref[...], b_ref[...], dn).astype(o_ref.dtype)
            return
        acc_ref, = acc
        k = pl.program_id(2)

        @pl.when(k == 0)
        def _():
            acc_ref[...] = jnp.zeros_like(acc_ref)

        acc_ref[...] += _dot(a_ref[...], b_ref[...], dn)

        @pl.when(k == nk - 1)
        def _():
            o_ref[...] = acc_ref[...].astype(o_ref.dtype)

    if n_outer:
        gi = lambda g: (g[1], g[0], g[2])
        grid = (N // tn, M // tm, nk)
    else:
        gi = lambda g: g
        grid = (M // tm, N // tn, nk)

    def amap(*g):
        i, j, k = gi(g)
        return (k, i) if mode == "tn" else (i, k)

    def bmap(*g):
        i, j, k = gi(g)
        return (j, k) if mode == "nt" else (k, j)

    def omap(*g):
        i, j, k = gi(g)
        return (i, j)

    a_blk = (tk, tm) if mode == "tn" else (tm, tk)
    b_blk = (tn, tk) if mode == "nt" else (tk, tn)
    acc_scratch = [] if nk == 1 else [pltpu.VMEM((tm, tn), F32)]
    if side is None:
        return pl.pallas_call(
            body, name=name, grid=grid,
            in_specs=[pl.BlockSpec(a_blk, amap), pl.BlockSpec(b_blk, bmap)],
            out_specs=pl.BlockSpec((tm, tn), omap),
            out_shape=jax.ShapeDtypeStruct((M, N), out_dtype),
            scratch_shapes=acc_scratch,
            compiler_params=_params("parallel", "parallel", "arbitrary"),
        )(a, b)

    ne = len(side.arrs)
    steps = grid[0] * grid[1] * grid[2]

    def with_side(*refs):
        e_in, e_out = refs[2:2 + ne], refs[3 + ne:3 + 2 * ne]
        sems = refs[len(refs) - 2:]
        step = (pl.program_id(0) * grid[1] + pl.program_id(1)) * grid[2] + pl.program_id(2)

        @pl.when(step == 0)
        def _():
            for cp in side.make(e_in, e_out, *sems):
                cp.start()

        body(refs[0], refs[1], refs[2 + ne], *refs[3 + 2 * ne:len(refs) - 2])

        @pl.when(step == steps - 1)
        def _():
            cps = side.make(e_in, e_out, *sems)
            for cp in cps:
                cp.wait_recv()
            for cp in cps:
                cp.wait_send()

    any_spec = pl.BlockSpec(memory_space=pl.ANY)
    res = pl.pallas_call(
        with_side, name=name, grid=grid,
        in_specs=[pl.BlockSpec(a_blk, amap), pl.BlockSpec(b_blk, bmap)] + [any_spec] * ne,
        out_specs=[pl.BlockSpec((tm, tn), omap)] + [any_spec] * ne,
        out_shape=[jax.ShapeDtypeStruct((M, N), out_dtype)] + side.out_shapes,
        scratch_shapes=acc_scratch + [pltpu.SemaphoreType.DMA((side.n_copies,)), pltpu.SemaphoreType.DMA((side.n_copies,))],
        compiler_params=_params("arbitrary", "arbitrary", "arbitrary"),
    )(a, b, *side.arrs)
    return res[0], list(res[1:])


def _slot(swap, px, py, pc):
    return 4 * py + 2 * px + pc if swap else 4 * px + 2 * py + pc


class _Gather:
    def __init__(self, ins, outs, send, recv, loc, swaps):
        self.ins, self.outs, self.send, self.recv, self.loc, self.swaps = ins, outs, send, recv, loc, swaps
        x, y, c = lax.axis_index("x"), lax.axis_index("y"), lax.axis_index("c")
        self.me, self.sib = (x, y, c), (x, y, 1 - c)
        self.chips = [(1 - x, y), (x, 1 - y), (1 - x, 1 - y)]
        self.n = len(ins)

    @staticmethod
    def scratch(n):
        return [pltpu.SemaphoreType.DMA((7 * n,)), pltpu.SemaphoreType.DMA((7 * n,)), pltpu.SemaphoreType.DMA((n,))]

    def copy(self, a, k, blk, to, src=None):
        rows = self.outs[a].at[_slot(self.swaps[a], *blk)]
        return pltpu.make_async_remote_copy(
            src_ref=rows if src is None else src, dst_ref=rows,
            send_sem=self.send.at[7 * a + k], recv_sem=self.recv.at[7 * a + k], device_id=to, device_id_type=MESH)

    def mine(self, a):
        return pltpu.make_async_copy(self.ins[a], self.outs[a].at[_slot(self.swaps[a], *self.me)], self.loc.at[a])

    def first(self, a):
        c = self.me[2]
        return [self.copy(a, 0, self.me, self.sib, src=self.ins[a])] + [
            self.copy(a, 1 + j, self.me, (*chip, c), src=self.ins[a]) for j, chip in enumerate(self.chips)]

    def forwards(self, a):
        c = self.me[2]
        return [self.copy(a, 4 + j, (*chip, c), self.sib) for j, chip in enumerate(self.chips)]

    def start(self):
        for a in range(self.n):
            self.mine(a).start()
        for a in range(self.n):
            for cp in self.first(a):
                cp.start()

    def forward(self):
        c = self.me[2]
        for a in range(self.n):
            fwd = self.forwards(a)
            for j, chip in enumerate(self.chips):
                self.copy(a, 1 + j, (*chip, c), self.me).wait_recv()
                fwd[j].start()

    def finish(self):
        c = self.me[2]
        for a in range(self.n):
            self.copy(a, 0, self.sib, self.me).wait_recv()
            for j, chip in enumerate(self.chips):
                self.copy(a, 4 + j, (*chip, 1 - c), self.me).wait_recv()
        for a in range(self.n):
            for cp in self.first(a) + self.forwards(a):
                cp.wait_send()
            self.mine(a).wait()


def _all_gather(arrs, swaps, name):
    n = len(arrs)

    def body(*refs):
        g = _Gather(refs[:n], refs[n:2 * n], *refs[2 * n:], swaps)
        g.start()
        g.forward()
        g.finish()

    any_spec = pl.BlockSpec(memory_space=pl.ANY)
    return pl.pallas_call(
        body, name=name,
        in_specs=[any_spec] * n, out_specs=[any_spec] * n,
        out_shape=[jax.ShapeDtypeStruct((NDEV,) + a.shape, a.dtype) for a in arrs],
        scratch_shapes=_Gather.scratch(n),
    )(*arrs)


def _pair_copies(ins, gots, send, recv, swaps):
    x, y, c = lax.axis_index("x"), lax.axis_index("y"), lax.axis_index("c")
    return [pltpu.make_async_remote_copy(
        src_ref=ins[a].at[_slot(swaps[a], k // 2, k % 2, 1 - c)], dst_ref=gots[a].at[k],
        send_sem=send.at[4 * a + k], recv_sem=recv.at[4 * a + k], device_id=(x, y, 1 - c), device_id_type=MESH)
        for a in range(len(ins)) for k in range(4)]


def _pair_exchange(arrs, swaps, name):
    n = len(arrs)

    def body(*refs):
        rems = _pair_copies(refs[:n], refs[n:2 * n], *refs[2 * n:], swaps)
        for rc in rems:
            rc.start()
        for rc in rems:
            rc.wait_recv()
        for rc in rems:
            rc.wait_send()

    any_spec = pl.BlockSpec(memory_space=pl.ANY)
    return pl.pallas_call(
        body, name=name,
        in_specs=[any_spec] * n, out_specs=[any_spec] * n,
        out_shape=[jax.ShapeDtypeStruct((4,) + a.shape[1:], a.dtype) for a in arrs],
        scratch_shapes=[pltpu.SemaphoreType.DMA((4 * n,)), pltpu.SemaphoreType.DMA((4 * n,))],
    )(*arrs)


def _chip_copies(ins, outs, send, recv):
    x, y, c = lax.axis_index("x"), lax.axis_index("y"), lax.axis_index("c")
    chips = [(1 - x, y), (x, 1 - y), (1 - x, 1 - y)]
    return [pltpu.make_async_remote_copy(
        src_ref=ins[a].at[2 * px + py], dst_ref=outs[a].at[j], send_sem=send.at[3 * a + j], recv_sem=recv.at[3 * a + j],
        device_id=(px, py, c), device_id_type=MESH) for a in range(len(ins)) for j, (px, py) in enumerate(chips)]


def _chip_exchange(arrs, name):
    n = len(arrs)

    def body(*refs):
        rems = _chip_copies(refs[:n], refs[n:2 * n], *refs[2 * n:])
        for rc in rems:
            rc.start()
        for rc in rems:
            rc.wait_recv()
        for rc in rems:
            rc.wait_send()

    any_spec = pl.BlockSpec(memory_space=pl.ANY)
    return pl.pallas_call(
        body, name=name,
        in_specs=[any_spec] * n, out_specs=[any_spec] * n,
        out_shape=[jax.ShapeDtypeStruct((3,) + a.shape[1:], a.dtype) for a in arrs],
        scratch_shapes=[pltpu.SemaphoreType.DMA((3 * n,)), pltpu.SemaphoreType.DMA((3 * n,))],
    )(*arrs)


class _Side:
    def __init__(self, arrs, out_shapes, n_copies, make):
        self.arrs, self.out_shapes, self.n_copies, self.make = list(arrs), list(out_shapes), n_copies, make


def _pair_side(arrs, swaps):
    return _Side(arrs, [jax.ShapeDtypeStruct((4,) + a.shape[1:], a.dtype) for a in arrs], 4 * len(arrs),
                 functools.partial(_pair_copies, swaps=swaps))


def _chip_side(arrs):
    return _Side(arrs, [jax.ShapeDtypeStruct((3,) + a.shape[1:], a.dtype) for a in arrs], 3 * len(arrs), _chip_copies)


def _row_call(body, side, *, name, steps, in_specs, out_specs, out_shape, ins):
    if side is None:
        res = pl.pallas_call(body, name=name, grid=(steps,), in_specs=in_specs, out_specs=out_specs, out_shape=out_shape,
                             compiler_params=_params("arbitrary"))(*ins)
        return list(res), []
    n_in, n_out, ne = len(in_specs), len(out_specs), len(side.arrs)

    def wrapped(*refs):
        e_in = refs[n_in:n_in + ne]
        e_out = refs[n_in + ne + n_out:n_in + 2 * ne + n_out]
        sems = refs[n_in + 2 * ne + n_out:]
        i = pl.program_id(0)

        @pl.when(i == 0)
        def _():
            for cp in side.make(e_in, e_out, *sems):
                cp.start()

        body(*refs[:n_in], *refs[n_in + ne:n_in + ne + n_out])

        @pl.when(i == steps - 1)
        def _():
            cps = side.make(e_in, e_out, *sems)
            for cp in cps:
                cp.wait_recv()
            for cp in cps:
                cp.wait_send()

    any_spec = pl.BlockSpec(memory_space=pl.ANY)
    res = pl.pallas_call(
        wrapped, name=name, grid=(steps,), in_specs=list(in_specs) + [any_spec] * ne,
        out_specs=list(out_specs) + [any_spec] * ne, out_shape=list(out_shape) + side.out_shapes,
        scratch_shapes=[pltpu.SemaphoreType.DMA((side.n_copies,)), pltpu.SemaphoreType.DMA((side.n_copies,))],
        compiler_params=_params("arbitrary"))(*ins, *side.arrs)
    return list(res[:n_out]), list(res[n_out:])


def _pair_sum(grads, got, swap, core, name):
    _, r, c = got.shape
    tr = r if r <= 352 else r // 2

    def own_map(k, i, core_ref):
        return (_slot(swap, k // 2, k % 2, core_ref[0]), i, 0)

    def body(core_ref, a_ref, b_ref, o_ref):
        o_ref[...] = a_ref[...] + b_ref[...]

    spec = pl.BlockSpec((None, tr, c), lambda k, i, core_ref: (k, i, 0))
    return pl.pallas_call(
        body, name=name,
        grid_spec=pltpu.PrefetchScalarGridSpec(
            num_scalar_prefetch=1, grid=(4, r // tr),
            in_specs=[pl.BlockSpec((None, tr, c), own_map), spec], out_specs=spec),
        out_shape=jax.ShapeDtypeStruct(got.shape, got.dtype), compiler_params=_params("parallel", "parallel"),
    )(core, grads, got)


def _adamw_math(w, g, m, v):
    m = ADAM_B1 * m + (1.0 - ADAM_B1) * g
    v = ADAM_B2 * v + (1.0 - ADAM_B2) * (g * g)
    m_hat = m / (1.0 - ADAM_B1 ** ADAM_STEP)
    v_hat = v / (1.0 - ADAM_B2 ** ADAM_STEP)
    delta = -ADAM_LR * (m_hat / (jnp.sqrt(v_hat) + ADAM_EPS) + ADAM_WD * w)
    return delta, m, v


def _adamw_tile(r):
    for cand in (256, 352, 128):
        if r % cand == 0:
            return cand
    return r


def _adamw(w, m, v, g, name):
    r, c = w.shape
    tr = _adamw_tile(r)
    spec = pl.BlockSpec((tr, c), lambda i: (i, 0))

    def body(w_ref, m_ref, v_ref, g_ref, d_ref, nm_ref, nv_ref):
        d_ref[...], nm_ref[...], nv_ref[...] = _adamw_math(w_ref[...], g_ref[...], m_ref[...], v_ref[...])

    out = jax.ShapeDtypeStruct((r, c), F32)
    return pl.pallas_call(
        body, name=name, grid=(r // tr,), in_specs=[spec] * 4, out_specs=[spec] * 3, out_shape=[out] * 3,
        compiler_params=_params("parallel"),
    )(w, m, v, g)


def _adamw_reduce(w, m, v, sums, recv, chip, name):
    r, c = w.shape
    tr = _adamw_tile(r)
    spec = pl.BlockSpec((tr, c), lambda i, chip_ref: (i, 0))

    def body(chip_ref, w_ref, m_ref, v_ref, s_ref, p_ref, g_ref, d_ref, nm_ref, nv_ref):
        g = ((s_ref[...] + p_ref[0]) + p_ref[1]) + p_ref[2]
        g_ref[...] = g
        d_ref[...], nm_ref[...], nv_ref[...] = _adamw_math(w_ref[...], g, m_ref[...], v_ref[...])

    out = jax.ShapeDtypeStruct((r, c), F32)
    return pl.pallas_call(
        body, name=name,
        grid_spec=pltpu.PrefetchScalarGridSpec(
            num_scalar_prefetch=1, grid=(r // tr,),
            in_specs=[spec, spec, spec, pl.BlockSpec((None, tr, c), lambda i, chip_ref: (chip_ref[0], i, 0)),
                      pl.BlockSpec((3, tr, c), lambda i, chip_ref: (0, i, 0))],
            out_specs=[spec] * 4),
        out_shape=[out] * 4, compiler_params=_params("parallel"),
    )(chip, w, m, v, sums, recv)


def _vec(n):
    return pl.BlockSpec((1, n), lambda *_: (0, 0))


def _ln_mod(x, g, scale, shift, *, ts, name):
    s = x.shape[0]
    row = pl.BlockSpec((ts, D), lambda i: (i, 0))

    def body(x_ref, g_ref, sc_ref, sh_ref, h_ref):
        xv = x_ref[...]
        r = lax.rsqrt(jnp.mean(xv * xv, axis=-1, keepdims=True) + EPS)
        h = (xv * r) * g_ref[...]
        h_ref[...] = (h * (1.0 + sc_ref[...]) + sh_ref[...]).astype(BF16)

    return pl.pallas_call(
        body, name=name, grid=(s // ts,), in_specs=[row, _vec(D), _vec(D), _vec(D)], out_specs=row,
        out_shape=jax.ShapeDtypeStruct((s, D), BF16), compiler_params=_params("parallel"),
    )(x, g, scale, shift)


def _group_rsqrt(t, bd):
    return lax.rsqrt(_split_dot(t * t, bd) * (1.0 / HD) + EPS)


def _qk_norm(proj, qg, kg, bd, *, ts, name):
    s = proj.shape[0]

    def body(q_ref, k_ref, v_ref, qg_ref, kg_ref, bd_ref, o_ref):
        bdv = bd_ref[...]
        q, k = q_ref[...], k_ref[...]
        o_ref[:, 0:DA] = (q * _group_rsqrt(q, bdv) * qg_ref[...]).astype(BF16)
        o_ref[:, DA:2 * DA] = (k * _group_rsqrt(k, bdv) * kg_ref[...]).astype(BF16)
        o_ref[:, 2 * DA:] = v_ref[...].astype(BF16)

    col = lambda j: pl.BlockSpec((ts, DA), lambda i: (i, j))
    return pl.pallas_call(
        body, name=name, grid=(s // ts,),
        in_specs=[col(1), col(2), col(3), _vec(DA), _vec(DA), _full((DA, DA))],
        out_specs=pl.BlockSpec((ts, 3 * DA), lambda i: (i, 0)),
        out_shape=jax.ShapeDtypeStruct((s, 3 * DA), BF16), compiler_params=_params("parallel"),
    )(proj, proj, proj, qg, kg, bd)


EXP_UNDERFLOW = -120.0


def _log_terms(z):
    neg_abs = lax.bitcast_convert_type(lax.bitcast_convert_type(z, jnp.uint32) | jnp.uint32(0x80000000), F32)
    b = jnp.minimum(z, 0.0) - jnp.log(1.0 + jnp.exp(neg_abs))
    return b, b - z


def _head_masks(rows):
    lane = lax.broadcasted_iota(jnp.int32, (rows, LANES), 1)
    return [lane < HD, lane >= HD]


def _attn_fwd(qkv, gather, swaps, *, tq, tk, name):
    s = qkv.shape[0]
    nrep = tk // LANES
    ndiag = tq // tk
    ng = len(gather)
    npair, nq = DA // LANES, s // tq

    def body(*refs):
        q_ref, k_ref, v_ref = refs[:3]
        g_in = refs[3:3 + ng]
        o_ref, tot_ref, first_ref = refs[3 + ng:6 + ng]
        g_out = refs[6 + ng:6 + 2 * ng]
        oacc, rc = refs[6 + 2 * ng:8 + 2 * ng]
        g_sems = refs[8 + 2 * ng:]
        i = pl.program_id(1)
        step_id = pl.program_id(0) * nq + i

        @pl.when(step_id == 0)
        def _():
            _Gather(g_in, g_out, *g_sems, swaps).start()

        @pl.when(step_id == (npair * nq * 3) // 4)
        def _():
            _Gather(g_in, g_out, *g_sems, swaps).forward()

        heads = _head_masks(tq)
        q = q_ref[...] * 0.125
        qs = [jnp.where(h, q, 0.0).astype(BF16) for h in heads]
        dif = lax.broadcasted_iota(jnp.int32, (tq, tk), 0) - lax.broadcasted_iota(jnp.int32, (tq, tk), 1)
        kr = lax.broadcasted_iota(jnp.int32, (tk, tk), 0)
        kc = lax.broadcasted_iota(jnp.int32, (tk, tk), 1)
        later = jnp.where(kr > kc, 1.0, 0.0).astype(BF16)
        oacc[...] = jnp.zeros_like(oacc)
        rc[...] = jnp.zeros_like(rc)

        def tile(kb, thr):
            rows = pl.ds(pl.multiple_of(kb * tk, tk), tk)
            k = k_ref[rows, :]
            v = v_ref[rows, :]
            rcv = [rc[0], rc[1]]
            zs = [_dot(qs[a], k, NT) for a in range(2)]
            bs, mbs = [], []
            for a in range(2):
                b, m = _log_terms(zs[a])
                if thr is not None:
                    m = jnp.where(dif > thr, m, 0.0)
                bs.append(b)
                mbs.append(m.astype(BF16))
            rl = [_dot(mbs[a], later) for a in range(2)]
            for a in range(2):
                p = jnp.exp(bs[a] + (rl[a] + jnp.tile(rcv[a], (1, nrep))))
                if thr is not None:
                    p = jnp.where(dif > thr, p, 0.0)
                oacc[a] += _dot(p.astype(BF16), v)
                rc[a] = rcv[a] + (rl[a][:, 0:1] + mbs[a][:, 0:1].astype(F32))

        for d in reversed(range(ndiag)):
            tile(i * ndiag + d, d * tk)

        def live():
            return jnp.max(jnp.maximum(rc[0], rc[1])) > EXP_UNDERFLOW

        def step(carry):
            kb, _ = carry
            tile(kb, None)
            return kb - 1, live()

        kb_end, _ = lax.while_loop(lambda cr: jnp.logical_and(cr[0] >= 0, cr[1]), step, (i * ndiag - 1, live()))
        first_ref[pl.program_id(0), i] = (kb_end + 1).astype(F32)
        o_ref[...] = jnp.where(heads[0], oacc[0], oacc[1])
        tot_ref[...] = jnp.where(heads[0], rc[0], rc[1])

        @pl.when(step_id == npair * nq - 1)
        def _():
            _Gather(g_in, g_out, *g_sems, swaps).finish()

    qspec = pl.BlockSpec((tq, LANES), lambda p, i: (i, p))
    any_spec = pl.BlockSpec(memory_space=pl.ANY)
    res = pl.pallas_call(
        body, name=name, grid=(npair, nq),
        in_specs=[qspec,
                  pl.BlockSpec((s, LANES), lambda p, i: (0, DA // LANES + p)),
                  pl.BlockSpec((s, LANES), lambda p, i: (0, 2 * DA // LANES + p))] + [any_spec] * ng,
        out_specs=[qspec, qspec, pl.BlockSpec(memory_space=pltpu.SMEM)] + [any_spec] * ng,
        out_shape=[jax.ShapeDtypeStruct((s, DA), F32), jax.ShapeDtypeStruct((s, DA), F32),
                   jax.ShapeDtypeStruct((npair, nq), F32)]
        + [jax.ShapeDtypeStruct((NDEV,) + a.shape, a.dtype) for a in gather],
        scratch_shapes=[pltpu.VMEM((2, tq, LANES), F32), pltpu.VMEM((2, tq, LANES), F32)] + _Gather.scratch(ng),
        compiler_params=_params("arbitrary", "arbitrary"),
    )(qkv, qkv, qkv, *gather)
    return res[0], res[1], res[2], res[3:]


def _attn_bwd(qkv, do, tot, first, exchange, *, tq, tk, name):
    s = qkv.shape[0]
    nrep = tk // LANES
    ndiag = tq // tk
    ne = len(exchange)
    npair, nq = DA // LANES, s // tq

    def body(*refs):
        q_ref, k_ref, v_ref, do_ref, tot_ref, first_ref = refs[:6]
        e_in = refs[6:6 + ne]
        dq_ref, dk_ref, dv_ref = refs[6 + ne:9 + ne]
        e_out = refs[9 + ne:9 + 2 * ne]
        dqacc, rem, gc = refs[9 + 2 * ne:12 + 2 * ne]
        e_sems = refs[12 + 2 * ne:]
        i = pl.program_id(1)
        step_id = pl.program_id(0) * nq + i

        @pl.when(step_id == 0)
        def _():
            for cp in _chip_copies(e_in, e_out, *e_sems):
                cp.start()

        @pl.when(i == 0)
        def _():
            dk_ref[...] = jnp.zeros_like(dk_ref)
            dv_ref[...] = jnp.zeros_like(dv_ref)

        heads = _head_masks(tq)
        q = q_ref[...] * 0.125
        qs = [jnp.where(h, q, 0.0).astype(BF16) for h in heads]
        dov = do_ref[...]
        dob = [jnp.where(h, dov, 0.0).astype(BF16) for h in heads]
        dif = lax.broadcasted_iota(jnp.int32, (tq, tk), 0) - lax.broadcasted_iota(jnp.int32, (tq, tk), 1)
        kr = lax.broadcasted_iota(jnp.int32, (tk, tk), 0)
        kc = lax.broadcasted_iota(jnp.int32, (tk, tk), 1)
        up_incl = jnp.where(kr <= kc, 1.0, 0.0).astype(BF16)
        up_strict = jnp.where(kr < kc, 1.0, 0.0).astype(BF16)
        dqacc[...] = jnp.zeros_like(dqacc)
        gc[...] = jnp.zeros_like(gc)
        totv = tot_ref[...]
        swapped = pltpu.roll(totv, HD, axis=1)
        rem[0] = jnp.where(heads[0], totv, swapped)
        rem[1] = jnp.where(heads[1], totv, swapped)

        def tile(kb, thr):
            rows = pl.ds(pl.multiple_of(kb * tk, tk), tk)
            k = k_ref[rows, :]
            v = v_ref[rows, :]
            remv = [rem[0], rem[1]]
            gcv = [gc[0], gc[1]]
            zs = [_dot(qs[a], k, NT) for a in range(2)]
            das = [_dot(dob[a], v, NT) for a in range(2)]
            bs, mbs = [], []
            for a in range(2):
                b, m = _log_terms(zs[a])
                if thr is not None:
                    m = jnp.where(dif > thr, m, 0.0)
                bs.append(b)
                mbs.append(m.astype(BF16))
            pl_ = [_dot(mbs[a], up_incl) for a in range(2)]
            ps, gs, gbs = [], [], []
            for a in range(2):
                p = jnp.exp(bs[a] + (jnp.tile(remv[a], (1, nrep)) - pl_[a]))
                if thr is not None:
                    p = jnp.where(dif > thr, p, 0.0)
                g = p * das[a]
                ps.append(p.astype(BF16))
                gs.append(g)
                gbs.append(g.astype(BF16))
            cl = [_dot(gbs[a], up_strict) for a in range(2)]
            dk_add = jnp.zeros((tk, LANES), F32)
            dv_add = jnp.zeros((tk, LANES), F32)
            for a in range(2):
                dz = gs[a] - jnp.exp(bs[a]) * (gs[a] + (jnp.tile(gcv[a], (1, nrep)) + cl[a]))
                if thr is not None:
                    dz = jnp.where(dif > thr, dz, 0.0)
                dzb = dz.astype(BF16)
                dqacc[a] += _dot(dzb, k)
                dk_add += _dot(dzb, qs[a], TN)
                dv_add += _dot(ps[a], dob[a], TN)
                rem[a] = remv[a] - pl_[a][:, tk - 1:tk]
                gc[a] = gcv[a] + (cl[a][:, tk - 1:tk] + gbs[a][:, tk - 1:tk].astype(F32))
            dk_ref[rows, :] += dk_add
            dv_ref[rows, :] += dv_add

        def step(kb, carry):
            tile(kb, None)
            return carry

        lax.fori_loop(first_ref[pl.program_id(0), i].astype(jnp.int32), i * ndiag, step, 0)
        for d in range(ndiag):
            tile(i * ndiag + d, d * tk)
        dq_ref[...] = jnp.where(heads[0], dqacc[0], dqacc[1]) * 0.125

        @pl.when(step_id == npair * nq - 1)
        def _():
            cps = _chip_copies(e_in, e_out, *e_sems)
            for cp in cps:
                cp.wait_recv()
            for cp in cps:
                cp.wait_send()

    qspec = pl.BlockSpec((tq, LANES), lambda p, i: (i, p))
    full = pl.BlockSpec((s, LANES), lambda p, i: (0, p))
    any_spec = pl.BlockSpec(memory_space=pl.ANY)
    out = jax.ShapeDtypeStruct((s, DA), F32)
    res = pl.pallas_call(
        body, name=name, grid=(npair, nq),
        in_specs=[qspec, pl.BlockSpec((s, LANES), lambda p, i: (0, DA // LANES + p)),
                  pl.BlockSpec((s, LANES), lambda p, i: (0, 2 * DA // LANES + p)), qspec, qspec,
                  pl.BlockSpec(memory_space=pltpu.SMEM)] + [any_spec] * ne,
        out_specs=[qspec, full, full] + [any_spec] * ne,
        out_shape=[out, out, out] + [jax.ShapeDtypeStruct((3,) + a.shape[1:], a.dtype) for a in exchange],
        scratch_shapes=[pltpu.VMEM((2, tq, LANES), F32)] * 3
        + [pltpu.SemaphoreType.DMA((3 * ne,)), pltpu.SemaphoreType.DMA((3 * ne,))],
        compiler_params=_params("arbitrary", "arbitrary"),
    )(qkv, qkv, qkv, do, tot, first, *exchange)
    return res[0], res[1], res[2], res[3:]


def _shift_rows(v, k):
    return pltpu.roll(v, k % v.shape[0], axis=0)


def _pooled(u, uh, i, g, w, ts):
    halo = jnp.where(i > 0, uh, 0.0)
    ue = jnp.concatenate([halo, u], axis=0)
    acc, span = ue, 1
    while span < w:
        acc = acc + _shift_rows(acc, span)
        span *= 2
    tpos = i * ts + lax.broadcasted_iota(jnp.int32, (ts, 1), 0)
    cnt = jnp.minimum(tpos + 1, w).astype(F32)
    return acc[HALO:] / cnt - u


def _pool_mix(proj, o, pw, pb, ps, ag, bd, *, ts, name):
    s = proj.shape[0]
    hb = ts // HALO

    def body(u_ref, uh_ref, o_ref, pw_ref, pb_ref, ps_ref, ag_ref, bd_ref, mix_ref):
        i = pl.program_id(0)
        for g, w in enumerate(POOL_WINDOWS):
            cols = slice(g * LANES, (g + 1) * LANES)
            pooled = _pooled(u_ref[:, cols], uh_ref[:, cols], i, g, w, ts)
            yv = (_dot(pooled.astype(BF16), pw_ref[g]) + pb_ref[:, cols]) * ps_ref[:, cols]
            mix_ref[:, cols] = yv.astype(BF16)
        ov = o_ref[...]
        mix_ref[:, DP:] = (ov * _group_rsqrt(ov, bd_ref[...]) * ag_ref[...]).astype(BF16)

    return pl.pallas_call(
        body, name=name, grid=(s // ts,),
        in_specs=[pl.BlockSpec((ts, DP), lambda i: (i, 0)),
                  pl.BlockSpec((HALO, DP), lambda i: (jnp.maximum(i * hb - 1, 0), 0)),
                  pl.BlockSpec((ts, DA), lambda i: (i, 0)),
                  _full((4, LANES, LANES)), _vec(DP), _vec(DP), _vec(DA), _full((DA, DA))],
        out_specs=pl.BlockSpec((ts, D), lambda i: (i, 0)),
        out_shape=jax.ShapeDtypeStruct((s, D), BF16), compiler_params=_params("parallel"),
    )(proj, proj, o, pw, pb, ps, ag, bd)


def _res_ln_mod(x, att, gate, g, scale, shift, *, ts, name):
    s = x.shape[0]
    row = pl.BlockSpec((ts, D), lambda i: (i, 0))

    def body(x_ref, a_ref, gt_ref, g_ref, sc_ref, sh_ref, x1_ref, h_ref):
        x1 = x_ref[...] + gt_ref[...] * a_ref[...]
        x1_ref[...] = x1
        r = lax.rsqrt(jnp.mean(x1 * x1, axis=-1, keepdims=True) + EPS)
        h = (x1 * r) * g_ref[...]
        h_ref[...] = (h * (1.0 + sc_ref[...]) + sh_ref[...]).astype(BF16)

    return pl.pallas_call(
        body, name=name, grid=(s // ts,), in_specs=[row, row] + [_vec(D)] * 4, out_specs=[row, row],
        out_shape=[jax.ShapeDtypeStruct((s, D), F32), jax.ShapeDtypeStruct((s, D), BF16)],
        compiler_params=_params("parallel"),
    )(x, att, gate, g, scale, shift)


CF = DFF // 2


def _conv(u, uh, w_ref, b_ref, i):
    halo = jnp.where(i > 0, uh.astype(F32), 0.0)
    ue = jnp.concatenate([halo, u.astype(F32)], axis=0)
    y = w_ref[2:3, :] * ue + w_ref[1:2, :] * _shift_rows(ue, 1) + w_ref[0:1, :] * _shift_rows(ue, 2)
    return y[HALO:] + b_ref[...]


def _conv_gate(up, cw, cb, *, ts, name):
    s = up.shape[0]
    hb = ts // HALO

    def body(u_ref, uh_ref, w_ref, b_ref, a_ref, c_ref):
        i = pl.program_id(0)
        c = _conv(u_ref[...], uh_ref[...], w_ref, b_ref, i)
        gt, vl = c[:, :CF], c[:, CF:]
        a_ref[...] = (gt / (1.0 + jnp.exp(-gt)) * vl).astype(BF16)
        c_ref[...] = c.astype(BF16)

    return pl.pallas_call(
        body, name=name, grid=(s // ts, 2),
        in_specs=[pl.BlockSpec((ts, 2 * CF), lambda i, j: (i, j)),
                  pl.BlockSpec((HALO, 2 * CF), lambda i, j: (jnp.maximum(i * hb - 1, 0), j)),
                  pl.BlockSpec((3, 2 * CF), lambda i, j: (0, j)), pl.BlockSpec((1, 2 * CF), lambda i, j: (0, j))],
        out_specs=[pl.BlockSpec((ts, CF), lambda i, j: (i, j)), pl.BlockSpec((ts, 2 * CF), lambda i, j: (i, j))],
        out_shape=[jax.ShapeDtypeStruct((s, DFF), BF16), jax.ShapeDtypeStruct((s, 2 * DFF), BF16)],
        compiler_params=_params("parallel", "parallel"),
    )(up, up, cw, cb)


def _loss_head(x1, ffn, tgt, gate2, *, ts, name):
    s = x1.shape[0]
    n = s // ts
    row = pl.BlockSpec((ts, D), lambda i: (i, 0))
    acc8 = pl.BlockSpec((SUBLANES, D), lambda i: (0, 0))

    def body(x_ref, f_ref, t_ref, g_ref, dy_ref, df_ref, dg_ref, loss_ref, lacc):
        i = pl.program_id(0)

        @pl.when(i == 0)
        def _():
            lacc[...] = jnp.zeros_like(lacc)
            dg_ref[...] = jnp.zeros_like(dg_ref)

        f = f_ref[...]
        diff = x_ref[...] + g_ref[...] * f - t_ref[...]
        lacc[...] += _colsum8(diff * diff)
        dy = diff * (1.0 / D)
        dy_ref[...] = dy
        df_ref[...] = (dy * g_ref[...]).astype(BF16)
        dg_ref[...] += _colsum8(dy * f)

        @pl.when(i == n - 1)
        def _():
            loss_ref[...] = jnp.full((SUBLANES, LANES), (0.5 / D) * jnp.sum(lacc[...]), F32)

    return pl.pallas_call(
        body, name=name, grid=(n,), in_specs=[row, row, row, _vec(D)],
        out_specs=[row, row, acc8, _full((SUBLANES, LANES))],
        out_shape=[jax.ShapeDtypeStruct((s, D), F32), jax.ShapeDtypeStruct((s, D), BF16),
                   jax.ShapeDtypeStruct((SUBLANES, D), F32), jax.ShapeDtypeStruct((SUBLANES, LANES), F32)],
        scratch_shapes=[pltpu.VMEM((SUBLANES, D), F32)], compiler_params=_params("arbitrary"),
    )(x1, ffn, tgt, gate2)


def _gate_bwd(da, conv, *, ts, name):
    s = conv.shape[0]

    def body(da_ref, c_ref, d_ref, db_ref):
        i = pl.program_id(1)

        @pl.when(i == 0)
        def _():
            db_ref[...] = jnp.zeros_like(db_ref)

        gt, vl = c_ref[:, :CF].astype(F32), c_ref[:, CF:].astype(F32)
        sg = 1.0 / (1.0 + jnp.exp(-gt))
        dav = da_ref[...].astype(F32)
        dgt = dav * vl * (sg * (1.0 + gt * (1.0 - sg)))
        dvl = dav * (gt * sg)
        d_ref[:, :CF] = dgt.astype(BF16)
        d_ref[:, CF:] = dvl.astype(BF16)
        db_ref[:, :CF] += _colsum8(dgt)
        db_ref[:, CF:] += _colsum8(dvl)

    return pl.pallas_call(
        body, name=name, grid=(2, s // ts),
        in_specs=[pl.BlockSpec((ts, CF), lambda j, i: (i, j)),
                  pl.BlockSpec((ts, 2 * CF), lambda j, i: (i, j))],
        out_specs=[pl.BlockSpec((ts, 2 * CF), lambda j, i: (i, j)),
                   pl.BlockSpec((SUBLANES, 2 * CF), lambda j, i: (0, j))],
        out_shape=[jax.ShapeDtypeStruct((s, 2 * DFF), BF16), jax.ShapeDtypeStruct((SUBLANES, 2 * DFF), F32)],
        compiler_params=_params("parallel", "arbitrary"),
    )(da, conv)


def _conv_bwd(dc, up, cw, *, ts, tc, name):
    s = up.shape[0]
    hb = ts // HALO
    nb = s // HALO

    def body(d_ref, dn_ref, u_ref, w_ref, du_ref, dw_ref):
        i = pl.program_id(1)
        n = s // ts

        @pl.when(i == 0)
        def _():
            dw_ref[...] = jnp.zeros_like(dw_ref)

        dcur = d_ref[...].astype(F32)
        nxt = jnp.where(i < n - 1, dn_ref[...].astype(F32), 0.0)
        de = jnp.concatenate([dcur, nxt], axis=0)
        d1 = _shift_rows(de, -1)[:ts]
        d2 = _shift_rows(de, -2)[:ts]
        du_ref[...] = (w_ref[2:3, :] * dcur + w_ref[1:2, :] * d1 + w_ref[0:1, :] * d2).astype(BF16)
        u = u_ref[...].astype(F32)
        dw_ref[16:24, :] += _colsum8(dcur * u)
        dw_ref[8:16, :] += _colsum8(d1 * u)
        dw_ref[0:8, :] += _colsum8(d2 * u)

    return pl.pallas_call(
        body, name=name, grid=(2 * DFF // tc, s // ts),
        in_specs=[pl.BlockSpec((ts, tc), lambda j, i: (i, j)),
                  pl.BlockSpec((HALO, tc), lambda j, i: (jnp.minimum((i + 1) * hb, nb - 1), j)),
                  pl.BlockSpec((ts, tc), lambda j, i: (i, j)),
                  pl.BlockSpec((3, tc), lambda j, i: (0, j))],
        out_specs=[pl.BlockSpec((ts, tc), lambda j, i: (i, j)), pl.BlockSpec((24, tc), lambda j, i: (0, j))],
        out_shape=[jax.ShapeDtypeStruct((s, 2 * DFF), BF16), jax.ShapeDtypeStruct((24, 2 * DFF), F32)],
        compiler_params=_params("parallel", "arbitrary"),
    )(dc, dc, up, cw)


MXU_COLS = 256


def _sub_chunks(width):
    return [(c0, min(MXU_COLS, width - c0)) for c0 in range(0, width, MXU_COLS)]


def _up_conv_gate(h2, w_up, cw, cb, *, tm, name):
    s = h2.shape[0]
    hb = tm // HALO

    def body(a_ref, ah_ref, w_ref, cw_ref, cb_ref, up_ref, c_ref, act_ref):
        i = pl.program_id(1)
        ext = jnp.concatenate([ah_ref[...], a_ref[...]], axis=0)
        live_halo = i > 0
        for c0, cwid in _sub_chunks(CF):
            conv = []
            for off in (c0, CF + c0):
                cols = slice(off, off + cwid)
                u = _dot(ext, w_ref[cols, :], NT)
                up_ref[:, cols] = u[HALO:].astype(BF16)
                row = lax.broadcasted_iota(jnp.int32, (HALO + tm, 1), 0)
                ue = jnp.where(jnp.logical_or(row >= HALO, live_halo), u, 0.0)
                y = cw_ref[2:3, cols] * ue + cw_ref[1:2, cols] * _shift_rows(ue, 1) + cw_ref[0:1, cols] * _shift_rows(ue, 2)
                cv = y[HALO:] + cb_ref[:, cols]
                c_ref[:, cols] = cv.astype(BF16)
                conv.append(cv)
            gt, vl = conv
            act_ref[:, c0:c0 + cwid] = (gt / (1.0 + jnp.exp(-gt)) * vl).astype(BF16)

    return pl.pallas_call(
        body, name=name, grid=(2, s // tm),
        in_specs=[pl.BlockSpec((tm, D), lambda j, i: (i, 0)),
                  pl.BlockSpec((HALO, D), lambda j, i: (jnp.maximum(i * hb - 1, 0), 0)),
                  pl.BlockSpec((2 * CF, D), lambda j, i: (j, 0)),
                  pl.BlockSpec((3, 2 * CF), lambda j, i: (0, j)), pl.BlockSpec((1, 2 * CF), lambda j, i: (0, j))],
        out_specs=[pl.BlockSpec((tm, 2 * CF), lambda j, i: (i, j)), pl.BlockSpec((tm, 2 * CF), lambda j, i: (i, j)),
                   pl.BlockSpec((tm, CF), lambda j, i: (i, j))],
        out_shape=[jax.ShapeDtypeStruct((s, 2 * DFF), BF16), jax.ShapeDtypeStruct((s, 2 * DFF), BF16),
                   jax.ShapeDtypeStruct((s, DFF), BF16)],
        compiler_params=_params("parallel", "parallel"),
    )(h2, h2, w_up, cw, cb)


def _down_bwd_gate(dffn, w_down, conv, *, tm, name):
    s = dffn.shape[0]

    def body(a_ref, w_ref, c_ref, d_ref, db_ref):
        i = pl.program_id(1)

        @pl.when(i == 0)
        def _():
            db_ref[...] = jnp.zeros_like(db_ref)

        a = a_ref[...]
        for c0, cwid in _sub_chunks(CF):
            gcols, vcols = slice(c0, c0 + cwid), slice(CF + c0, CF + c0 + cwid)
            da = _dot(a, w_ref[gcols, :], NT)
            gt, vl = c_ref[:, gcols].astype(F32), c_ref[:, vcols].astype(F32)
            sg = 1.0 / (1.0 + jnp.exp(-gt))
            dgt = da * vl * (sg * (1.0 + gt * (1.0 - sg)))
            dvl = da * (gt * sg)
            d_ref[:, gcols] = dgt.astype(BF16)
            d_ref[:, vcols] = dvl.astype(BF16)
            db_ref[:, gcols] += _colsum8(dgt)
            db_ref[:, vcols] += _colsum8(dvl)

    return pl.pallas_call(
        body, name=name, grid=(2, s // tm),
        in_specs=[pl.BlockSpec((tm, D), lambda j, i: (i, 0)), pl.BlockSpec((CF, D), lambda j, i: (j, 0)),
                  pl.BlockSpec((tm, 2 * CF), lambda j, i: (i, j))],
        out_specs=[pl.BlockSpec((tm, 2 * CF), lambda j, i: (i, j)), pl.BlockSpec((SUBLANES, 2 * CF), lambda j, i: (0, j))],
        out_shape=[jax.ShapeDtypeStruct((s, 2 * DFF), BF16), jax.ShapeDtypeStruct((SUBLANES, 2 * DFF), F32)],
        compiler_params=_params("parallel", "arbitrary"),
    )(dffn, w_down, conv)


def _conv_bwd_up_bwd(dc, up, cw, w_up, *, tm, name):
    s = up.shape[0]
    hb = tm // HALO
    nb = s // HALO
    nk = 2 * DFF // CF
    n = s // tm

    def body(d_ref, dn_ref, u_ref, cw_ref, w_ref, du_ref, dh_ref, dw_ref, acc, dwacc):
        i, k = pl.program_id(0), pl.program_id(1)

        @pl.when(jnp.logical_and(i == 0, k == 0))
        def _():
            dwacc[...] = jnp.zeros_like(dwacc)

        @pl.when(k == 0)
        def _():
            acc[...] = jnp.zeros_like(acc)

        live_next = i < n - 1
        part = None
        for c0, cwid in _sub_chunks(CF):
            cols = slice(c0, c0 + cwid)
            dcur = d_ref[:, cols].astype(F32)
            de = jnp.concatenate([dcur, jnp.where(live_next, dn_ref[:, cols].astype(F32), 0.0)], axis=0)
            d1 = _shift_rows(de, -1)[:tm]
            d2 = _shift_rows(de, -2)[:tm]
            du = (cw_ref[2:3, cols] * dcur + cw_ref[1:2, cols] * d1 + cw_ref[0:1, cols] * d2).astype(BF16)
            du_ref[:, cols] = du
            prod = _dot(du, w_ref[cols, :])
            part = prod if part is None else part + prod
            u = u_ref[:, cols].astype(F32)
            for tap, dsh in ((2, dcur), (1, d1), (0, d2)):
                dwacc[k, SUBLANES * tap:SUBLANES * (tap + 1), cols] += _colsum8(dsh * u)
        acc[...] += part

        @pl.when(k == nk - 1)
        def _():
            dh_ref[...] = acc[...]

        @pl.when(jnp.logical_and(i == n - 1, k == nk - 1))
        def _():
            dw_ref[...] = dwacc[...]

    res = pl.pallas_call(
        body, name=name, grid=(n, nk),
        in_specs=[pl.BlockSpec((tm, CF), lambda i, k: (i, k)),
                  pl.BlockSpec((HALO, CF), lambda i, k: (jnp.minimum((i + 1) * hb, nb - 1), k)),
                  pl.BlockSpec((tm, CF), lambda i, k: (i, k)),
                  pl.BlockSpec((3, CF), lambda i, k: (0, k)),
                  pl.BlockSpec((CF, D), lambda i, k: (k, 0))],
        out_specs=[pl.BlockSpec((tm, CF), lambda i, k: (i, k)), pl.BlockSpec((tm, D), lambda i, k: (i, 0)),
                   _full((nk, 24, CF))],
        out_shape=[jax.ShapeDtypeStruct((s, 2 * DFF), BF16), jax.ShapeDtypeStruct((s, D), F32),
                   jax.ShapeDtypeStruct((nk, 24, CF), F32)],
        scratch_shapes=[pltpu.VMEM((tm, D), F32), pltpu.VMEM((nk, 24, CF), F32)],
        compiler_params=_params("arbitrary", "arbitrary"),
    )(dc, dc, up, cw, w_up)
    return res[0], res[1], jnp.transpose(res[2], (1, 0, 2)).reshape(24, 2 * DFF)


def _ln_mod_bwd(dh, xin, g, scale, resid, extra, gate, *, ts, name, side=None):
    s = xin.shape[0]
    row = pl.BlockSpec((ts, D), lambda i: (i, 0))
    acc8 = pl.BlockSpec((SUBLANES, D), lambda i: (0, 0))
    with_gate = extra is not None

    def body(*refs):
        if with_gate:
            dh_ref, x_ref, g_ref, sc_ref, r_ref, e_ref, gt_ref, dx_ref, da_ref, dsh, dsc, dg, dgt = refs
        else:
            dh_ref, x_ref, g_ref, sc_ref, r_ref, dx_ref, dsh, dsc, dg = refs
        i = pl.program_id(0)

        @pl.when(i == 0)
        def _():
            for acc in (dsh, dsc, dg) + ((dgt,) if with_gate else ()):
                acc[...] = jnp.zeros_like(acc)

        xv, dhv = x_ref[...], dh_ref[...]
        r = lax.rsqrt(jnp.mean(xv * xv, axis=-1, keepdims=True) + EPS)
        xn = xv * r
        dsh[...] += _colsum8(dhv)
        dsc[...] += _colsum8(dhv * (xn * g_ref[...]))
        dhp = dhv * (1.0 + sc_ref[...])
        dg[...] += _colsum8(dhp * xn)
        dxn = dhp * g_ref[...]
        dx = r_ref[...] + r * (dxn - xn * jnp.mean(dxn * xn, axis=-1, keepdims=True))
        dx_ref[...] = dx
        if with_gate:
            da_ref[...] = (dx * gt_ref[...]).astype(BF16)
            dgt[...] += _colsum8(dx * e_ref[...])

    f32o, p8 = jax.ShapeDtypeStruct((s, D), F32), jax.ShapeDtypeStruct((SUBLANES, D), F32)
    if with_gate:
        ins, in_specs = (dh, xin, g, scale, resid, extra, gate), [row, row, _vec(D), _vec(D), row, row, _vec(D)]
        out_specs, out_shape = [row, row, acc8, acc8, acc8, acc8], [f32o, jax.ShapeDtypeStruct((s, D), BF16), p8, p8, p8, p8]
    else:
        ins, in_specs = (dh, xin, g, scale, resid), [row, row, _vec(D), _vec(D), row]
        out_specs, out_shape = [row, acc8, acc8, acc8], [f32o, p8, p8, p8]
    return _row_call(body, side, name=name, steps=s // ts, in_specs=in_specs, out_specs=out_specs,
                     out_shape=out_shape, ins=ins)


def _group_norm_bwd(t, dn_out, gvec, bd):
    r = _group_rsqrt(t, bd)
    dg_terms = dn_out * t * r
    dn = dn_out * gvec
    dt = r * (dn - t * (r * r) * (_split_dot(dn * t, bd) * (1.0 / HD)))
    return dt, dg_terms


def _mix_bwd(dmix, proj, o, pw, pb, ps, ag, bd, *, ts, name, side=None):
    s = proj.shape[0]
    hb = ts // HALO
    nb = s // HALO

    def body(dm_ref, dmn_ref, u_ref, uh_ref, o_ref, pw_ref, pb_ref, ps_ref, ag_ref, bd_ref,
             du_ref, do_ref, dpw_ref, dpb_ref, dps_ref, dag_ref):
        i = pl.program_id(0)
        n = s // ts

        @pl.when(i == 0)
        def _():
            for acc in (dpw_ref, dpb_ref, dps_ref, dag_ref):
                acc[...] = jnp.zeros_like(acc)

        for g, w in enumerate(POOL_WINDOWS):
            cols = slice(g * LANES, (g + 1) * LANES)
            wg = pw_ref[g]
            psg = ps_ref[:, cols]
            pooled = _pooled(u_ref[:, cols], uh_ref[:, cols], i, g, w, ts).astype(BF16)
            dy = dm_ref[:, cols]
            dps_ref[:, cols] += _colsum8(dy * (_dot(pooled, wg) + pb_ref[:, cols]))
            dpre = dy * psg
            dpb_ref[:, cols] += _colsum8(dpre)
            dpreb = dpre.astype(BF16)
            dpw_ref[g * LANES:(g + 1) * LANES, :] += _dot(pooled, dpreb, TN)
            dpool = _dot(dpreb, wg, NT)
            dnext = _dot((dmn_ref[:, cols] * psg).astype(BF16), wg, NT)
            dpe = jnp.concatenate([dpool, jnp.where(i < n - 1, dnext, 0.0)], axis=0)
            tpos = i * ts + lax.broadcasted_iota(jnp.int32, (ts + HALO, 1), 0)
            acc = dpe / jnp.minimum(tpos + 1, w).astype(F32)
            span = 1
            while span < w:
                acc = acc + _shift_rows(acc, -span)
                span *= 2
            du_ref[:, cols] = acc[:ts] - dpool
        ov = o_ref[...]
        dov, dg_terms = _group_norm_bwd(ov, dm_ref[:, DP:], ag_ref[...], bd_ref[...])
        do_ref[...] = dov
        dag_ref[...] += _colsum8(dg_terms)

    p8 = jax.ShapeDtypeStruct((SUBLANES, DP), F32)
    acc8 = pl.BlockSpec((SUBLANES, DP), lambda i: (0, 0))
    half = pl.BlockSpec((ts, DP), lambda i: (i, 0))
    return _row_call(
        body, side, name=name, steps=s // ts,
        in_specs=[pl.BlockSpec((ts, D), lambda i: (i, 0)),
                  pl.BlockSpec((HALO, DP), lambda i: (jnp.minimum((i + 1) * hb, nb - 1), 0)),
                  half, pl.BlockSpec((HALO, DP), lambda i: (jnp.maximum(i * hb - 1, 0), 0)),
                  half, _full((4, LANES, LANES)), _vec(DP), _vec(DP), _vec(DA), _full((DA, DA))],
        out_specs=[half, half, _full((DP, LANES)), acc8, acc8, acc8],
        out_shape=[jax.ShapeDtypeStruct((s, DP), F32), jax.ShapeDtypeStruct((s, DA), F32),
                   jax.ShapeDtypeStruct((DP, LANES), F32), p8, p8, p8],
        ins=(dmix, dmix, proj, proj, o, pw, pb, ps, ag, bd))


def _qk_norm_bwd(du, dq, dk, dv, proj, qg, kg, bd, *, ts, name):
    s = proj.shape[0]

    def body(du_ref, dq_ref, dk_ref, dv_ref, q_ref, k_ref, qg_ref, kg_ref, bd_ref, dp_ref, dqg_ref, dkg_ref):
        i = pl.program_id(0)

        @pl.when(i == 0)
        def _():
            dqg_ref[...] = jnp.zeros_like(dqg_ref)
            dkg_ref[...] = jnp.zeros_like(dkg_ref)

        bdv = bd_ref[...]
        dqr, tq = _group_norm_bwd(q_ref[...], dq_ref[...], qg_ref[...], bdv)
        dkr, tk = _group_norm_bwd(k_ref[...], dk_ref[...], kg_ref[...], bdv)
        dqg_ref[...] += _colsum8(tq)
        dkg_ref[...] += _colsum8(tk)
        dp_ref[:, 0:DP] = du_ref[...].astype(BF16)
        dp_ref[:, DP:DP + DA] = dqr.astype(BF16)
        dp_ref[:, DP + DA:DP + 2 * DA] = dkr.astype(BF16)
        dp_ref[:, DP + 2 * DA:] = dv_ref[...].astype(BF16)

    half = pl.BlockSpec((ts, DA), lambda i: (i, 0))
    col = lambda j: pl.BlockSpec((ts, DA), lambda i: (i, j))
    acc8 = pl.BlockSpec((SUBLANES, DA), lambda i: (0, 0))
    p8 = jax.ShapeDtypeStruct((SUBLANES, DA), F32)
    return pl.pallas_call(
        body, name=name, grid=(s // ts,),
        in_specs=[half, half, half, half, col(1), col(2), _vec(DA), _vec(DA), _full((DA, DA))],
        out_specs=[pl.BlockSpec((ts, DIN), lambda i: (i, 0)), acc8, acc8],
        out_shape=[jax.ShapeDtypeStruct((s, DIN), BF16), p8, p8],
        compiler_params=_params("arbitrary"),
    )(du, dq, dk, dv, proj, proj, qg, kg, bd)


def _split3(a):
    hi = a.astype(BF16)
    return hi, (a - hi.astype(F32)).astype(BF16)


def _dot3(a, b, dn):
    ah, al = _split3(a)
    bh, bl = _split3(b)
    return _dot(ah, bh, dn) + (_dot(ah, bl, dn) + _dot(al, bh, dn))


def _ada_fwd(c_all, w, b, name):
    nw = w.shape[1]

    def body(c_ref, w_ref, b_ref, o_ref):
        cv = c_ref[...]
        act = cv / (1.0 + jnp.exp(-cv))
        o_ref[...] = _dot3(act, w_ref[...], NN) + b_ref[...]

    return pl.pallas_call(
        body, name=name, in_specs=[_full((NDEV, D)), _full(w.shape), _full((1, nw))], out_specs=_full((NDEV, nw)),
        out_shape=jax.ShapeDtypeStruct((NDEV, nw), F32), grid=(1,), compiler_params=_params("arbitrary"),
    )(c_all, w, b)


def _ada_bwd(c_all, dmod, name):
    nw = dmod.shape[1]

    def body(c_ref, d_ref, o_ref):
        cv = c_ref[...]
        act = cv / (1.0 + jnp.exp(-cv))
        o_ref[...] = _dot3(act, d_ref[...], TN)[None]

    return pl.pallas_call(
        body, name=name, in_specs=[_full((NDEV, D)), _full((NDEV, nw))], out_specs=_full((1, D, nw)),
        out_shape=jax.ShapeDtypeStruct((1, D, nw), F32), grid=(1,), compiler_params=_params("arbitrary"),
    )(c_all, dmod)


def _fold_heads(v):
    acc = v[:, 0:HD]
    for h in range(1, DA // HD):
        acc = acc + v[:, h * HD:(h + 1) * HD]
    return acc


def _pack_partials(pieces, dcw_p, name):
    n_p = len(pieces)
    total = sum(p.shape[1] for p in pieces) + 3 * dcw_p.shape[1]
    npack = -(-total // (SUBLANES * LANES)) * (SUBLANES * LANES)

    def body(*refs):
        out = refs[-1]
        off = 0
        for r in refs[:n_p]:
            out[:, off:off + r.shape[1]] = jnp.sum(r[...], axis=0, keepdims=True)
            off += r.shape[1]
        dw = refs[n_p]
        for tap in range(3):
            out[:, off:off + dw.shape[1]] = jnp.sum(dw[SUBLANES * tap:SUBLANES * (tap + 1), :], axis=0, keepdims=True)
            off += dw.shape[1]
        if off < npack:
            out[:, off:] = jnp.zeros((1, npack - off), F32)

    arrs = list(pieces) + [dcw_p]
    return pl.pallas_call(
        body, name=name, grid=(1,), in_specs=[_full(a.shape) for a in arrs], out_specs=_full((1, npack)),
        out_shape=jax.ShapeDtypeStruct((1, npack), F32), compiler_params=_params("arbitrary"),
    )(*arrs)


def _small_update(gathered, gathered_pw, gathered_cw, specs, params, name):
    names = [sp[0] for sp in specs]
    flat = []
    for nme in names + ["pool_w", "conv_w"]:
        flat += list(params[nme])
    n_in = len(flat)

    def body(*refs):
        ga_ref, gp_ref, gc_ref = refs[0], refs[1], refs[2]
        prm = refs[3:3 + n_in]
        outs = refs[3 + n_in:]
        total = ga_ref[0:1, :]
        for dv in range(1, NDEV):
            total = total + ga_ref[dv:dv + 1, :]
        k = 0
        for idx, (nme, off, width, fold) in enumerate(specs):
            g = total[:, off:off + width]
            if fold:
                g = _fold_heads(g)
            w_ref, m_ref, v_ref = prm[3 * idx:3 * idx + 3]
            d, nm, nv = _adamw_math(w_ref[...], g, m_ref[...], v_ref[...])
            for val in (g, d, nm, nv):
                outs[k][...] = val
                k += 1
        gpw = gp_ref[0]
        for dv in range(1, NDEV):
            gpw = gpw + gp_ref[dv]
        w_ref, m_ref, v_ref = prm[3 * len(specs):3 * len(specs) + 3]
        d, nm, nv = _adamw_math(w_ref[...], gpw, m_ref[...], v_ref[...])
        for val in (gpw, d, nm, nv):
            outs[k][...] = val
            k += 1
        gcw = gc_ref[0]
        for dv in range(1, NDEV):
            gcw = gcw + gc_ref[dv]
        w_ref, m_ref, v_ref = prm[3 * len(specs) + 3:3 * len(specs) + 6]
        d, nm, nv = _adamw_math(w_ref[...], gcw, m_ref[...], v_ref[...])
        for val in (gcw, d, nm, nv):
            outs[k][...] = val
            k += 1
        outs[k][...] = ga_ref[:, 0:6 * D]

    out_shape, out_specs = [], []
    for nme in names + ["pool_w", "conv_w"]:
        shp = params[nme][0].shape
        out_shape += [jax.ShapeDtypeStruct(shp, F32)] * 4
        out_specs += [_full(shp)] * 4
    out_shape.append(jax.ShapeDtypeStruct((NDEV, 6 * D), F32))
    out_specs.append(_full((NDEV, 6 * D)))
    res = pl.pallas_call(
        body, name=name, grid=(1,),
        in_specs=[_full(gathered.shape), _full(gathered_pw.shape), _full(gathered_cw.shape)] + [_full(a.shape) for a in flat],
        out_specs=out_specs, out_shape=out_shape, compiler_params=_params("arbitrary"),
    )(gathered, gathered_pw, gathered_cw, *flat)
    out = {nme: tuple(res[4 * i:4 * i + 4]) for i, nme in enumerate(names + ["pool_w", "conv_w"])}
    return out, res[-1]


def _row_tile(s):
    return 512 if s % 512 == 0 else s


def kernel(x, c, ada_w, ada_b, norm1_g, w_in, pool_w, pool_b, pool_scale, q_norm_g, k_norm_g, attn_out_g, w_out, norm2_g, w_up, conv_w, conv_b, w_down, loss_target, m_ada_w, m_ada_b, m_norm1_g, m_w_in, m_pool_w, m_pool_b, m_pool_scale, m_q_norm_g, m_k_norm_g, m_attn_out_g, m_w_out, m_norm2_g, m_w_up, m_conv_w, m_conv_b, m_w_down, v_ada_w, v_ada_b, v_norm1_g, v_w_in, v_pool_w, v_pool_b, v_pool_scale, v_q_norm_g, v_k_norm_g, v_attn_out_g, v_w_out, v_norm2_g, v_w_up, v_conv_w, v_conv_b, v_w_down):
    ax, ay, ac = lax.axis_index("x"), lax.axis_index("y"), lax.axis_index("c")
    me = 4 * ax + 2 * ay + ac
    me_swapped = 4 * ay + 2 * ax + ac
    xs, tgt = x[0], loss_target[0]
    s = xs.shape[0]
    ts = _row_tile(s)
    tq_attn, tk_attn = 512, 256
    tmm = 2 * ts
    bd = _block_diag_ones(DA, HD)

    w_in_t = w_in[0].T.astype(BF16)
    w_up_t = w_up[0].T.astype(BF16)
    c_gath, gw_in, gcw = _all_gather([jnp.broadcast_to(c, (SUBLANES, D)), w_in_t, jnp.pad(conv_w[0], ((0, 5), (0, 64)))],
                                     [False, False, True], "gather_in")
    c_all = c_gath[:, 0, :]
    n_ada = ada_w.shape[2]
    ada_b_mine = lax.dynamic_slice_in_dim(ada_b, me * n_ada, n_ada, axis=1)
    mod_part = _ada_fwd(c_all, ada_w[0], ada_b_mine, "ada_fwd")
    mod_all = _all_gather([mod_part], [False], "gather_mod")[0]
    mod = lax.dynamic_index_in_dim(mod_all, me, axis=1, keepdims=False).reshape(1, 6 * D)
    shift1, scale1, gate1, shift2, scale2, gate2 = [mod[:, k * D:(k + 1) * D] for k in range(6)]

    w_in_full = gw_in.reshape(DIN, D)
    later_w = [w_out[0].astype(BF16), w_up_t, w_down[0].astype(BF16)]
    cw_full = jnp.transpose(gcw[:, :3, :704], (1, 0, 2)).reshape(3, 2 * DFF)
    cb_full = jnp.transpose(conv_b.reshape(1, 2, 2, 2, 704), (0, 2, 1, 3, 4)).reshape(1, 2 * DFF)

    qg = jnp.tile(q_norm_g, (1, DA // HD))
    kg = jnp.tile(k_norm_g, (1, DA // HD))
    ag = attn_out_g.reshape(1, DA)
    pw = pool_w[0].astype(BF16)
    pb = pool_b.reshape(1, DP)
    h1 = _ln_mod(xs, norm1_g, scale1, shift1, ts=ts, name="ln1")
    proj = _matmul(h1, w_in_full, mode="nt", out_dtype=F32, tm=tmm,tn=DIN, tk=D, name="in_proj")
    qkv = _qk_norm(proj, qg, kg, bd, ts=ts, name="qk_norm")
    o_raw, m_tot, kb_first, (gw_out, gw_up, gw_down) = _attn_fwd(
        qkv, later_w, [False, True, False], tq=tq_attn, tk=tk_attn, name="attn_fwd")
    w_out_full = gw_out.reshape(D, D)
    w_up_full = gw_up.reshape(2 * DFF, D)
    w_down_full = gw_down.reshape(DFF, D)
    mix = _pool_mix(proj, o_raw, pw, pb, pool_scale, ag, bd, ts=ts, name="pool_mix")
    att = _matmul(mix, w_out_full, mode="nn", out_dtype=F32, tm=tmm,tn=D, tk=D, name="out_proj")
    x1, h2 = _res_ln_mod(xs, att, gate1, norm2_g, scale2, shift2, ts=ts, name="res_ln2")
    up, conv, act = _up_conv_gate(h2, w_up_full, cw_full, cb_full, tm=ts, name="up_conv_gate")
    ffn = _matmul(act, w_down_full, mode="nn", out_dtype=F32, tm=tmm,tn=D, tk=DFF, name="down_proj")
    dy, dffn, dgate2_p, loss_p = _loss_head(x1, ffn, tgt, gate2, ts=ts, name="loss_head")
    loss = lax.psum(loss_p[0, 0], ("x", "y", "c"))

    g_w_down = _matmul(act, dffn, mode="tn", out_dtype=F32, tm=CF, tn=D, tk=tmm,name="down_wgrad")
    dconv, dcb_p = _down_bwd_gate(dffn, w_down_full, conv, tm=tmm, name="down_bwd_gate")
    dup, dh2, dcw_p = _conv_bwd_up_bwd(dconv, up, cw_full, w_up_full, tm=tmm, name="conv_bwd_up_bwd")
    g_w_up_t = _matmul(dup, h2, mode="tn", out_dtype=F32, tm=CF, tn=D, tk=tmm,name="up_wgrad")
    (dx1, datt, dshift2_p, dscale2_p, dnorm2_p, dgate1_p), _ = _ln_mod_bwd(
        dh2, x1, norm2_g, scale2, dy, att, gate1, ts=ts, name="ln2_bwd")

    dmix = _matmul(datt, w_out_full, mode="nt", out_dtype=F32, tm=tmm,tn=D, tk=D, name="out_bwd")
    g_w_out = _matmul(mix, datt, mode="tn", out_dtype=F32, tm=D, tn=D, tk=tmm,name="out_wgrad")
    core = jnp.reshape(ac, (1,)).astype(jnp.int32)
    chip = jnp.reshape(2 * ax + ay, (1,)).astype(jnp.int32)
    big_ffn = [g_w_up_t.reshape(NDEV, 2 * DFF // NDEV, D), g_w_down.reshape(NDEV, DFF // NDEV, D),
               g_w_out.reshape(NDEV, D // NDEV, D)]
    swaps_ffn = [True, False, False]
    (du, do_raw, g_pw_p, dpb_p, dps_p, dag_p), gots_ffn = _mix_bwd(
        dmix, proj, o_raw, pw, pb, pool_scale, ag, bd, ts=ts, name="mix_bwd", side=_pair_side(big_ffn, swaps_ffn))
    sums_ffn = [_pair_sum(big_ffn[k], gots_ffn[k], swaps_ffn[k], core, "rs_pair_sum_ffn%d" % k) for k in range(3)]
    dqn, dkn, dvv, parts_ffn = _attn_bwd(qkv, do_raw, m_tot, kb_first, sums_ffn, tq=tq_attn, tk=tk_attn, name="attn_bwd")
    dproj, dqg_p, dkg_p = _qk_norm_bwd(du, dqn, dkn, dvv, proj, qg, kg, bd, ts=ts, name="qk_norm_bwd")
    g_w_in_t = _matmul(dproj, h1, mode="tn", out_dtype=F32, tm=DIN // 2, tn=D, tk=tmm,name="in_wgrad")
    big = [g_w_in_t.reshape(NDEV, DIN // NDEV, D)]
    gots = _pair_exchange(big, [False], "rs_pair")
    sums = [_pair_sum(big[0], gots[0], False, core, "rs_pair_sum")]
    dh1, parts = _matmul(dproj, w_in_full, mode="nn", out_dtype=F32, tm=tmm,tn=D, tk=DIN, name="in_bwd",
                         side=_chip_side(sums))
    (grad_x, dshift1_p, dscale1_p, dnorm1_p), _ = _ln_mod_bwd(
        dh1, xs, norm1_g, scale1, dx1, None, None, ts=ts, name="ln1_bwd")

    tr = lambda a: a[0].T
    r_in = _adamw_reduce(tr(w_in), tr(m_w_in), tr(v_w_in), sums[0], parts[0], chip, "adamw_w_in")
    r_out = _adamw_reduce(w_out[0], m_w_out[0], v_w_out[0], sums_ffn[2], parts_ffn[2], chip, "adamw_w_out")
    r_up = _adamw_reduce(tr(w_up), tr(m_w_up), tr(v_w_up), sums_ffn[0], parts_ffn[0], chip, "adamw_w_up")
    r_down = _adamw_reduce(w_down[0], m_w_down[0], v_w_down[0], sums_ffn[1], parts_ffn[1], chip, "adamw_w_down")
    r_in = [a.T[None] for a in r_in]
    r_up = [a.T[None] for a in r_up]
    r_out = [a[None] for a in r_out]
    r_down = [a[None] for a in r_down]

    dcb_nat = jnp.transpose(dcb_p.reshape(SUBLANES, 2, 2, 2, 704), (0, 2, 1, 3, 4)).reshape(SUBLANES, 2 * DFF)
    pieces = [dshift1_p, dscale1_p, dgate1_p, dshift2_p, dscale2_p, dgate2_p,
              dnorm1_p, dnorm2_p, dcb_nat, dpb_p, dps_p, dag_p, dqg_p, dkg_p]
    n_vec = sum(p.shape[1] for p in pieces)
    packed = _pack_partials(pieces, dcw_p, "pack_partials")
    npack = packed.shape[1]
    gathered, gathered_pw = _all_gather([packed.reshape(SUBLANES, npack // SUBLANES), g_pw_p], [False, False], "gather_small")
    gathered = gathered.reshape(NDEV, npack)
    gathered_cw = lax.dynamic_index_in_dim(
        gathered[:, n_vec:n_vec + 6 * DFF].reshape(NDEV, 3, NDEV, 704), me_swapped, axis=2, keepdims=False)
    specs = [("ada_b", 0, 6 * D, False)]
    off = 6 * D
    for nme, width, fold in (("norm1_g", D, False), ("norm2_g", D, False), ("conv_b", 2 * DFF, False),
                             ("pool_b", DP, False), ("pool_scale", DP, False), ("attn_out_g", DA, False),
                             ("q_norm_g", DA, True), ("k_norm_g", DA, True)):
        specs.append((nme, off, width, fold))
        off += width
    small = {
        "ada_b": (ada_b, m_ada_b, v_ada_b),
        "norm1_g": (norm1_g, m_norm1_g, v_norm1_g), "norm2_g": (norm2_g, m_norm2_g, v_norm2_g),
        "conv_b": (conv_b, m_conv_b, v_conv_b),
        "pool_b": (pb, m_pool_b.reshape(1, DP), v_pool_b.reshape(1, DP)),
        "pool_scale": (pool_scale, m_pool_scale, v_pool_scale),
        "attn_out_g": (ag, m_attn_out_g.reshape(1, DA), v_attn_out_g.reshape(1, DA)),
        "q_norm_g": (q_norm_g, m_q_norm_g, v_q_norm_g), "k_norm_g": (k_norm_g, m_k_norm_g, v_k_norm_g),
        "pool_w": (pool_w.reshape(DP, LANES), m_pool_w.reshape(DP, LANES), v_pool_w.reshape(DP, LANES)),
        "conv_w": (conv_w[0], m_conv_w[0], v_conv_w[0]),
    }
    upd, dmod_all = _small_update(gathered, gathered_pw, gathered_cw, specs, small, "small_update")
    g_ada_w = _ada_bwd(c_all, lax.dynamic_slice_in_dim(dmod_all, me * n_ada, n_ada, axis=1), "ada_bwd")
    r_ada = [g_ada_w] + [a[None] for a in _adamw(ada_w[0], m_ada_w[0], v_ada_w[0], g_ada_w[0], "adamw_ada_w")]

    shapes = {"ada_b": ada_b.shape, "norm1_g": norm1_g.shape, "pool_w": pool_w.shape, "pool_b": pool_b.shape,
              "pool_scale": pool_scale.shape, "q_norm_g": q_norm_g.shape, "k_norm_g": k_norm_g.shape,
              "attn_out_g": attn_out_g.shape, "norm2_g": norm2_g.shape, "conv_w": conv_w.shape, "conv_b": conv_b.shape}
    res = {nme: [a.reshape(shapes[nme]) for a in upd[nme]] for nme in shapes}
    res.update(ada_w=r_ada, w_in=r_in, w_out=r_out, w_up=r_up, w_down=r_down)
    names = ["ada_w", "ada_b", "norm1_g", "w_in", "pool_w", "pool_b", "pool_scale", "q_norm_g", "k_norm_g",
             "attn_out_g", "w_out", "norm2_g", "w_up", "conv_w", "conv_b", "w_down"]
    outs = [loss, grad_x[None]]
    for q in range(4):
        outs += [res[nme][q] for nme in names]
    return tuple(outs)
```

```python
import functools
import math

import numpy as np
import jax
import jax.numpy as jnp
from jax import lax
from jax.experimental import pallas as pl
from jax.experimental.pallas import tpu as pltpu

F32, BF16 = jnp.float32, jnp.bfloat16
D = 1024
DP = 512
DA = 512
HD = 64
DIN = DP + 3 * DA
DFF = 2816
POOL_WINDOWS = (2, 4, 8, 16)
HALO = 16
EPS = 1e-6
LANES = 128
SUBLANES = 8
NDEV = 8
VMEM_LIMIT = 56 * 1024 * 1024
MESH = pl.DeviceIdType.MESH

ADAM_LR, ADAM_B1, ADAM_B2, ADAM_EPS, ADAM_WD, ADAM_STEP = 0.001, 0.9, 0.999, 1e-08, 0.01, 10

NN = (((1,), (0,)), ((), ()))
NT = (((1,), (1,)), ((), ()))
TN = (((0,), (0,)), ((), ()))


def _params(*sem):
    return pltpu.CompilerParams(dimension_semantics=sem, vmem_limit_bytes=VMEM_LIMIT)


def _full(shape):
    nd = len(shape)
    return pl.BlockSpec(shape, lambda *_: (0,) * nd)


def _dot(a, b, dn=NN):
    return lax.dot_general(a, b, dn, preferred_element_type=F32)


def _split_dot(a, b, dn=NN):
    hi = a.astype(BF16)
    lo = (a - hi.astype(F32)).astype(BF16)
    return _dot(hi, b, dn) + _dot(lo, b, dn)


def _colsum8(v):
    r, n = v.shape
    return v.reshape(r // SUBLANES, SUBLANES, n).sum(axis=0)


def _block_diag_ones(n, blk):
    i = np.arange(n) // blk
    return jnp.asarray((i[:, None] == i[None, :]).astype(np.float32), BF16)


def _matmul(a, b, *, mode, out_dtype, tm, tn, tk, name, n_outer=False, side=None):
    if mode == "tn":
        K, M = a.shape
        N = b.shape[1]
    elif mode == "nt":
        M, K = a.shape
        N = b.shape[0]
    else:
        M, K = a.shape
        N = b.shape[1]
    tm, tn, tk = min(tm, M), min(tn, N), min(tk, K)
    assert M % tm == 0 and N % tn == 0 and K % tk == 0, (name, M, N, K, tm, tn, tk)
    nk = K // tk
    dn = {"nn": NN, "nt": NT, "tn": TN}[mode]

    def body(a_ref, b_ref, o_ref, *acc):
        if nk == 1:
            o_ref[...] = _dot(a_ref[...], b_ref[...], dn).astype(o_ref.dtype)
            return
        acc_ref, = acc
        k = pl.program_id(2)

        @pl.when(k == 0)
        def _():
            acc_ref[...] = jnp.zeros_like(acc_ref)

        acc_ref[...] += _dot(a_ref[...], b_ref[...], dn)

        @pl.when(k == nk - 1)
        def _():
            o_ref[...] = acc_ref[...].astype(o_ref.dtype)

    if n_outer:
        gi = lambda g: (g[1], g[0], g[2])
        grid = (N // tn, M // tm, nk)
    else:
        gi = lambda g: g
        grid = (M // tm, N // tn, nk)

    def amap(*g):
        i, j, k = gi(g)
        return (k, i) if mode == "tn" else (i, k)

    def bmap(*g):
        i, j, k = gi(g)
        return (j, k) if mode == "nt" else (k, j)

    def omap(*g):
        i, j, k = gi(g)
        return (i, j)

    a_blk = (tk, tm) if mode == "tn" else (tm, tk)
    b_blk = (tn, tk) if mode == "nt" else (tk, tn)
    acc_scratch = [] if nk == 1 else [pltpu.VMEM((tm, tn), F32)]
    if side is None:
        return pl.pallas_call(
            body, name=name, grid=grid,
            in_specs=[pl.BlockSpec(a_blk, amap), pl.BlockSpec(b_blk, bmap)],
            out_specs=pl.BlockSpec((tm, tn), omap),
            out_shape=jax.ShapeDtypeStruct((M, N), out_dtype),
            scratch_shapes=acc_scratch,
            compiler_params=_params("parallel", "parallel", "arbitrary"),
        )(a, b)

    ne = len(side.arrs)
    steps = grid[0] * grid[1] * grid[2]

    def with_side(*refs):
        e_in, e_out = refs[2:2 + ne], refs[3 + ne:3 + 2 * ne]
        sems = refs[len(refs) - 2:]
        step = (pl.program_id(0) * grid[1] + pl.program_id(1)) * grid[2] + pl.program_id(2)

        @pl.when(step == 0)
        def _():
            for cp in side.make(e_in, e_out, *sems):
                cp.start()

        body(refs[0], refs[1], refs[2 + ne], *refs[3 + 2 * ne:len(refs) - 2])

        @pl.when(step == steps - 1)
        def _():
            cps = side.make(e_in, e_out, *sems)
            for cp in cps:
                cp.wait_recv()
            for cp in cps:
                cp.wait_send()

    any_spec = pl.BlockSpec(memory_space=pl.ANY)
    res = pl.pallas_call(
        with_side, name=name, grid=grid,
        in_specs=[pl.BlockSpec(a_blk, amap), pl.BlockSpec(b_blk, bmap)] + [any_spec] * ne,
        out_specs=[pl.BlockSpec((tm, tn), omap)] + [any_spec] * ne,
        out_shape=[jax.ShapeDtypeStruct((M, N), out_dtype)] + side.out_shapes,
        scratch_shapes=acc_scratch + [pltpu.SemaphoreType.DMA((side.n_copies,)), pltpu.SemaphoreType.DMA((side.n_copies,))],
        compiler_params=_params("arbitrary", "arbitrary", "arbitrary"),
    )(a, b, *side.arrs)
    return res[0], list(res[1:])


def _slot(swap, px, py, pc):
    return 4 * py + 2 * px + pc if swap else 4 * px + 2 * py + pc


class _Gather:
    def __init__(self, ins, outs, send, recv, loc, swaps):
        self.ins, self.outs, self.send, self.recv, self.loc, self.swaps = ins, outs, send, recv, loc, swaps
        x, y, c = lax.axis_index("x"), lax.axis_index("y"), lax.axis_index("c")
        self.me, self.sib = (x, y, c), (x, y, 1 - c)
        self.chips = [(1 - x, y), (x, 1 - y), (1 - x, 1 - y)]
        self.n = len(ins)

    @staticmethod
    def scratch(n):
        return [pltpu.SemaphoreType.DMA((7 * n,)), pltpu.SemaphoreType.DMA((7 * n,)), pltpu.SemaphoreType.DMA((n,))]

    def copy(self, a, k, blk, to, src=None):
        rows = self.outs[a].at[_slot(self.swaps[a], *blk)]
        return pltpu.make_async_remote_copy(
            src_ref=rows if src is None else src, dst_ref=rows,
            send_sem=self.send.at[7 * a + k], recv_sem=self.recv.at[7 * a + k], device_id=to, device_id_type=MESH)

    def mine(self, a):
        return pltpu.make_async_copy(self.ins[a], self.outs[a].at[_slot(self.swaps[a], *self.me)], self.loc.at[a])

    def first(self, a):
        c = self.me[2]
        return [self.copy(a, 0, self.me, self.sib, src=self.ins[a])] + [
            self.copy(a, 1 + j, self.me, (*chip, c), src=self.ins[a]) for j, chip in enumerate(self.chips)]

    def forwards(self, a):
        c = self.me[2]
        return [self.copy(a, 4 + j, (*chip, c), self.sib) for j, chip in enumerate(self.chips)]

    def start(self):
        for a in range(self.n):
            self.mine(a).start()
        for a in range(self.n):
            for cp in self.first(a):
                cp.start()

    def forward(self):
        c = self.me[2]
        for a in range(self.n):
            fwd = self.forwards(a)
            for j, chip in enumerate(self.chips):
                self.copy(a, 1 + j, (*chip, c), self.me).wait_recv()
                fwd[j].start()

    def finish(self):
        c = self.me[2]
        for a in range(self.n):
            self.copy(a, 0, self.sib, self.me).wait_recv()
            for j, chip in enumerate(self.chips):
                self.copy(a, 4 + j, (*chip, 1 - c), self.me).wait_recv()
        for a in range(self.n):
            for cp in self.first(a) + self.forwards(a):
                cp.wait_send()
            self.mine(a).wait()


def _all_gather(arrs, swaps, name):
    n = len(arrs)

    def body(*refs):
        g = _Gather(refs[:n], refs[n:2 * n], *refs[2 * n:], swaps)
        g.start()
        g.forward()
        g.finish()

    any_spec = pl.BlockSpec(memory_space=pl.ANY)
    return pl.pallas_call(
        body, name=name,
        in_specs=[any_spec] * n, out_specs=[any_spec] * n,
        out_shape=[jax.ShapeDtypeStruct((NDEV,) + a.shape, a.dtype) for a in arrs],
        scratch_shapes=_Gather.scratch(n),
    )(*arrs)


def _pair_copies(ins, gots, send, recv, swaps):
    x, y, c = lax.axis_index("x"), lax.axis_index("y"), lax.axis_index("c")
    return [pltpu.make_async_remote_copy(
        src_ref=ins[a].at[_slot(swaps[a], k // 2, k % 2, 1 - c)], dst_ref=gots[a].at[k],
        send_sem=send.at[4 * a + k], recv_sem=recv.at[4 * a + k], device_id=(x, y, 1 - c), device_id_type=MESH)
        for a in range(len(ins)) for k in range(4)]


def _pair_exchange(arrs, swaps, name):
    n = len(arrs)

    def body(*refs):
        rems = _pair_copies(refs[:n], refs[n:2 * n], *refs[2 * n:], swaps)
        for rc in rems:
            rc.start()
        for rc in rems:
            rc.wait_recv()
        for rc in rems:
            rc.wait_send()

    any_spec = pl.BlockSpec(memory_space=pl.ANY)
    return pl.pallas_call(
        body, name=name,
        in_specs=[any_spec] * n, out_specs=[any_spec] * n,
        out_shape=[jax.ShapeDtypeStruct((4,) + a.shape[1:], a.dtype) for a in arrs],
        scratch_shapes=[pltpu.SemaphoreType.DMA((4 * n,)), pltpu.SemaphoreType.DMA((4 * n,))],
    )(*arrs)


def _chip_copies(ins, outs, send, recv):
    x, y, c = lax.axis_index("x"), lax.axis_index("y"), lax.axis_index("c")
    chips = [(1 - x, y), (x, 1 - y), (1 - x, 1 - y)]
    return [pltpu.make_async_remote_copy(
        src_ref=ins[a].at[2 * px + py], dst_ref=outs[a].at[j], send_sem=send.at[3 * a + j], recv_sem=recv.at[3 * a + j],
        device_id=(px, py, c), device_id_type=MESH) for a in range(len(ins)) for j, (px, py) in enumerate(chips)]


def _chip_exchange(arrs, name):
    n = len(arrs)

    def body(*refs):
        rems = _chip_copies(refs[:n], refs[n:2 * n], *refs[2 * n:])
        for rc in rems:
            rc.start()
        for rc in rems:
            rc.wait_recv()
        for rc in rems:
            rc.wait_send()

    any_spec = pl.BlockSpec(memory_space=pl.ANY)
    return pl.pallas_call(
        body, name=name,
        in_specs=[any_spec] * n, out_specs=[any_spec] * n,
        out_shape=[jax.ShapeDtypeStruct((3,) + a.shape[1:], a.dtype) for a in arrs],
        scratch_shapes=[pltpu.SemaphoreType.DMA((3 * n,)), pltpu.SemaphoreType.DMA((3 * n,))],
    )(*arrs)


class _Side:
    def __init__(self, arrs, out_shapes, n_copies, make):
        self.arrs, self.out_shapes, self.n_copies, self.make = list(arrs), list(out_shapes), n_copies, make


def _pair_side(arrs, swaps):
    return _Side(arrs, [jax.ShapeDtypeStruct((4,) + a.shape[1:], a.dtype) for a in arrs], 4 * len(arrs),
                 functools.partial(_pair_copies, swaps=swaps))


def _chip_side(arrs):
    return _Side(arrs, [jax.ShapeDtypeStruct((3,) + a.shape[1:], a.dtype) for a in arrs], 3 * len(arrs), _chip_copies)


def _row_call(body, side, *, name, steps, in_specs, out_specs, out_shape, ins):
    if side is None:
        res = pl.pallas_call(body, name=name, grid=(steps,), in_specs=in_specs, out_specs=out_specs, out_shape=out_shape,
                             compiler_params=_params("arbitrary"))(*ins)
        return list(res), []
    n_in, n_out, ne = len(in_specs), len(out_specs), len(side.arrs)

    def wrapped(*refs):
        e_in = refs[n_in:n_in + ne]
        e_out = refs[n_in + ne + n_out:n_in + 2 * ne + n_out]
        sems = refs[n_in + 2 * ne + n_out:]
        i = pl.program_id(0)

        @pl.when(i == 0)
        def _():
            for cp in side.make(e_in, e_out, *sems):
                cp.start()

        body(*refs[:n_in], *refs[n_in + ne:n_in + ne + n_out])

        @pl.when(i == steps - 1)
        def _():
            cps = side.make(e_in, e_out, *sems)
            for cp in cps:
                cp.wait_recv()
            for cp in cps:
                cp.wait_send()

    any_spec = pl.BlockSpec(memory_space=pl.ANY)
    res = pl.pallas_call(
        wrapped, name=name, grid=(steps,), in_specs=list(in_specs) + [any_spec] * ne,
        out_specs=list(out_specs) + [any_spec] * ne, out_shape=list(out_shape) + side.out_shapes,
        scratch_shapes=[pltpu.SemaphoreType.DMA((side.n_copies,)), pltpu.SemaphoreType.DMA((side.n_copies,))],
        compiler_params=_params("arbitrary"))(*ins, *side.arrs)
    return list(res[:n_out]), list(res[n_out:])


def _pair_sum(grads, got, swap, core, name):
    _, r, c = got.shape
    tr = r if r <= 352 else r // 2

    def own_map(k, i, core_ref):
        return (_slot(swap, k // 2, k % 2, core_ref[0]), i, 0)

    def body(core_ref, a_ref, b_ref, o_ref):
        o_ref[...] = a_ref[...] + b_ref[...]

    spec = pl.BlockSpec((None, tr, c), lambda k, i, core_ref: (k, i, 0))
    return pl.pallas_call(
        body, name=name,
        grid_spec=pltpu.PrefetchScalarGridSpec(
            num_scalar_prefetch=1, grid=(4, r // tr),
            in_specs=[pl.BlockSpec((None, tr, c), own_map), spec], out_specs=spec),
        out_shape=jax.ShapeDtypeStruct(got.shape, got.dtype), compiler_params=_params("parallel", "parallel"),
    )(core, grads, got)


def _adamw_math(w, g, m, v):
    m = ADAM_B1 * m + (1.0 - ADAM_B1) * g
    v = ADAM_B2 * v + (1.0 - ADAM_B2) * (g * g)
    m_hat = m / (1.0 - ADAM_B1 ** ADAM_STEP)
    v_hat = v / (1.0 - ADAM_B2 ** ADAM_STEP)
    delta = -ADAM_LR * (m_hat / (jnp.sqrt(v_hat) + ADAM_EPS) + ADAM_WD * w)
    return delta, m, v


def _adamw_tile(r):
    for cand in (256, 352, 128):
        if r % cand == 0:
            return cand
    return r


def _adamw(w, m, v, g, name):
    r, c = w.shape
    tr = _adamw_tile(r)
    spec = pl.BlockSpec((tr, c), lambda i: (i, 0))

    def body(w_ref, m_ref, v_ref, g_ref, d_ref, nm_ref, nv_ref):
        d_ref[...], nm_ref[...], nv_ref[...] = _adamw_math(w_ref[...], g_ref[...], m_ref[...], v_ref[...])

    out = jax.ShapeDtypeStruct((r, c), F32)
    return pl.pallas_call(
        body, name=name, grid=(r // tr,), in_specs=[spec] * 4, out_specs=[spec] * 3, out_shape=[out] * 3,
        compiler_params=_params("parallel"),
    )(w, m, v, g)


def _adamw_reduce(w, m, v, sums, recv, chip, name):
    r, c = w.shape
    tr = _adamw_tile(r)
    spec = pl.BlockSpec((tr, c), lambda i, chip_ref: (i, 0))

    def body(chip_ref, w_ref, m_ref, v_ref, s_ref, p_ref, g_ref, d_ref, nm_ref, nv_ref):
        g = ((s_ref[...] + p_ref[0]) + p_ref[1]) + p_ref[2]
        g_ref[...] = g
        d_ref[...], nm_ref[...], nv_ref[...] = _adamw_math(w_ref[...], g, m_ref[...], v_ref[...])

    out = jax.ShapeDtypeStruct((r, c), F32)
    return pl.pallas_call(
        body, name=name,
        grid_spec=pltpu.PrefetchScalarGridSpec(
            num_scalar_prefetch=1, grid=(r // tr,),
            in_specs=[spec, spec, spec, pl.BlockSpec((None, tr, c), lambda i, chip_ref: (chip_ref[0], i, 0)),
                      pl.BlockSpec((3, tr, c), lambda i, chip_ref: (0, i, 0))],
            out_specs=[spec] * 4),
        out_shape=[out] * 4, compiler_params=_params("parallel"),
    )(chip, w, m, v, sums, recv)


def _vec(n):
    return pl.BlockSpec((1, n), lambda *_: (0, 0))


def _ln_mod(x, g, scale, shift, *, ts, name):
    s = x.shape[0]
    row = pl.BlockSpec((ts, D), lambda i: (i, 0))

    def body(x_ref, g_ref, sc_ref, sh_ref, h_ref):
        xv = x_ref[...]
        r = lax.rsqrt(jnp.mean(xv * xv, axis=-1, keepdims=True) + EPS)
        h = (xv * r) * g_ref[...]
        h_ref[...] = (h * (1.0 + sc_ref[...]) + sh_ref[...]).astype(BF16)

    return pl.pallas_call(
        body, name=name, grid=(s // ts,), in_specs=[row, _vec(D), _vec(D), _vec(D)], out_specs=row,
        out_shape=jax.ShapeDtypeStruct((s, D), BF16), compiler_params=_params("parallel"),
    )(x, g, scale, shift)


def _group_rsqrt(t, bd):
    return lax.rsqrt(_split_dot(t * t, bd) * (1.0 / HD) + EPS)


def _qk_norm(proj, qg, kg, bd, *, ts, name):
    s = proj.shape[0]

    def body(q_ref, k_ref, v_ref, qg_ref, kg_ref, bd_ref, o_ref):
        bdv = bd_ref[...]
        q, k = q_ref[...], k_ref[...]
        o_ref[:, 0:DA] = (q * _group_rsqrt(q, bdv) * qg_ref[...]).astype(BF16)
        o_ref[:, DA:2 * DA] = (k * _group_rsqrt(k, bdv) * kg_ref[...]).astype(BF16)
        o_ref[:, 2 * DA:] = v_ref[...].astype(BF16)

    col = lambda j: pl.BlockSpec((ts, DA), lambda i: (i, j))
    return pl.pallas_call(
        body, name=name, grid=(s // ts,),
        in_specs=[col(1), col(2), col(3), _vec(DA), _vec(DA), _full((DA, DA))],
        out_specs=pl.BlockSpec((ts, 3 * DA), lambda i: (i, 0)),
        out_shape=jax.ShapeDtypeStruct((s, 3 * DA), BF16), compiler_params=_params("parallel"),
    )(proj, proj, proj, qg, kg, bd)


EXP_UNDERFLOW = -120.0


def _log_terms(z):
    neg_abs = lax.bitcast_convert_type(lax.bitcast_convert_type(z, jnp.uint32) | jnp.uint32(0x80000000), F32)
    b = jnp.minimum(z, 0.0) - jnp.log(1.0 + jnp.exp(neg_abs))
    return b, b - z


def _head_masks(rows):
    lane = lax.broadcasted_iota(jnp.int32, (rows, LANES), 1)
    return [lane < HD, lane >= HD]


def _attn_fwd(qkv, gather, swaps, *, tq, tk, name):
    s = qkv.shape[0]
    nrep = tk // LANES
    ndiag = tq // tk
    ng = len(gather)
    npair, nq = DA // LANES, s // tq

    def body(*refs):
        q_ref, k_ref, v_ref = refs[:3]
        g_in = refs[3:3 + ng]
        o_ref, tot_ref, first_ref = refs[3 + ng:6 + ng]
        g_out = refs[6 + ng:6 + 2 * ng]
        oacc, rc = refs[6 + 2 * ng:8 + 2 * ng]
        g_sems = refs[8 + 2 * ng:]
        i = pl.program_id(1)
        step_id = pl.program_id(0) * nq + i

        @pl.when(step_id == 0)
        def _():
            _Gather(g_in, g_out, *g_sems, swaps).start()

        @pl.when(step_id == (npair * nq * 3) // 4)
        def _():
            _Gather(g_in, g_out, *g_sems, swaps).forward()

        heads = _head_masks(tq)
        q = q_ref[...] * 0.125
        qs = [jnp.where(h, q, 0.0).astype(BF16) for h in heads]
        dif = lax.broadcasted_iota(jnp.int32, (tq, tk), 0) - lax.broadcasted_iota(jnp.int32, (tq, tk), 1)
        kr = lax.broadcasted_iota(jnp.int32, (tk, tk), 0)
        kc = lax.broadcasted_iota(jnp.int32, (tk, tk), 1)
        later = jnp.where(kr > kc, 1.0, 0.0).astype(BF16)
        oacc[...] = jnp.zeros_like(oacc)
        rc[...] = jnp.zeros_like(rc)

        def tile(kb, thr):
            rows = pl.ds(pl.multiple_of(kb * tk, tk), tk)
            k = k_ref[rows, :]
            v = v_ref[rows, :]
            qr = slice(0 if thr is None else thr, tq)
            rcv = [rc[0, qr, :], rc[1, qr, :]]
            zs = [_dot(qs[a][qr], k, NT) for a in range(2)]
            bs, mbs = [], []
            for a in range(2):
                b, m = _log_terms(zs[a])
                if thr is not None:
                    m = jnp.where(dif[qr] > thr, m, 0.0)
                bs.append(b)
                mbs.append(m.astype(BF16))
            rl = [_dot(mbs[a], later) for a in range(2)]
            for a in range(2):
                p = jnp.exp(bs[a] + (rl[a] + jnp.tile(rcv[a], (1, nrep))))
                if thr is not None:
                    p = jnp.where(dif[qr] > thr, p, 0.0)
                oacc[a, qr, :] += _dot(p.astype(BF16), v)
                rc[a, qr, :] = rcv[a] + (rl[a][:, 0:1] + mbs[a][:, 0:1].astype(F32))

        for d in reversed(range(ndiag)):
            tile(i * ndiag + d, d * tk)

        def live():
            return jnp.max(jnp.maximum(rc[0], rc[1])) > EXP_UNDERFLOW

        def step(carry):
            kb, _ = carry
            tile(kb, None)
            return kb - 1, live()

        kb_end, _ = lax.while_loop(lambda cr: jnp.logical_and(cr[0] >= 0, cr[1]), step, (i * ndiag - 1, live()))
        first_ref[pl.program_id(0), i] = (kb_end + 1).astype(F32)
        o_ref[...] = jnp.where(heads[0], oacc[0], oacc[1])
        tot_ref[...] = jnp.where(heads[0], rc[0], rc[1])

        @pl.when(step_id == npair * nq - 1)
        def _():
            _Gather(g_in, g_out, *g_sems, swaps).finish()

    qspec = pl.BlockSpec((tq, LANES), lambda p, i: (i, p))
    any_spec = pl.BlockSpec(memory_space=pl.ANY)
    res = pl.pallas_call(
        body, name=name, grid=(npair, nq),
        in_specs=[qspec,
                  pl.BlockSpec((s, LANES), lambda p, i: (0, DA // LANES + p)),
                  pl.BlockSpec((s, LANES), lambda p, i: (0, 2 * DA // LANES + p))] + [any_spec] * ng,
        out_specs=[qspec, qspec, pl.BlockSpec(memory_space=pltpu.SMEM)] + [any_spec] * ng,
        out_shape=[jax.ShapeDtypeStruct((s, DA), F32), jax.ShapeDtypeStruct((s, DA), F32),
                   jax.ShapeDtypeStruct((npair, nq), F32)]
        + [jax.ShapeDtypeStruct((NDEV,) + a.shape, a.dtype) for a in gather],
        scratch_shapes=[pltpu.VMEM((2, tq, LANES), F32), pltpu.VMEM((2, tq, LANES), F32)] + _Gather.scratch(ng),
        compiler_params=_params("arbitrary", "arbitrary"),
    )(qkv, qkv, qkv, *gather)
    return res[0], res[1], res[2], res[3:]


def _attn_bwd(qkv, do, tot, first, exchange, *, tq, tk, name):
    s = qkv.shape[0]
    nrep = tk // LANES
    ndiag = tq // tk
    ne = len(exchange)
    npair, nq = DA // LANES, s // tq

    def body(*refs):
        q_ref, k_ref, v_ref, do_ref, tot_ref, first_ref = refs[:6]
        e_in = refs[6:6 + ne]
        dq_ref, dk_ref, dv_ref = refs[6 + ne:9 + ne]
        e_out = refs[9 + ne:9 + 2 * ne]
        dqacc, rem, gc = refs[9 + 2 * ne:12 + 2 * ne]
        e_sems = refs[12 + 2 * ne:]
        i = pl.program_id(1)
        step_id = pl.program_id(0) * nq + i

        @pl.when(step_id == 0)
        def _():
            for cp in _chip_copies(e_in, e_out, *e_sems):
                cp.start()

        @pl.when(i == 0)
        def _():
            dk_ref[...] = jnp.zeros_like(dk_ref)
            dv_ref[...] = jnp.zeros_like(dv_ref)

        heads = _head_masks(tq)
        q = q_ref[...] * 0.125
        qs = [jnp.where(h, q, 0.0).astype(BF16) for h in heads]
        dov = do_ref[...]
        dob = [jnp.where(h, dov, 0.0).astype(BF16) for h in heads]
        dif = lax.broadcasted_iota(jnp.int32, (tq, tk), 0) - lax.broadcasted_iota(jnp.int32, (tq, tk), 1)
        kr = lax.broadcasted_iota(jnp.int32, (tk, tk), 0)
        kc = lax.broadcasted_iota(jnp.int32, (tk, tk), 1)
        up_incl = jnp.where(kr <= kc, 1.0, 0.0).astype(BF16)
        up_strict = jnp.where(kr < kc, 1.0, 0.0).astype(BF16)
        dqacc[...] = jnp.zeros_like(dqacc)
        gc[...] = jnp.zeros_like(gc)
        totv = tot_ref[...]
        swapped = pltpu.roll(totv, HD, axis=1)
        rem[0] = jnp.where(heads[0], totv, swapped)
        rem[1] = jnp.where(heads[1], totv, swapped)

        def tile(kb, thr):
            rows = pl.ds(pl.multiple_of(kb * tk, tk), tk)
            k = k_ref[rows, :]
            v = v_ref[rows, :]
            qr = slice(0 if thr is None else thr, tq)
            remv = [rem[0, qr, :], rem[1, qr, :]]
            gcv = [gc[0, qr, :], gc[1, qr, :]]
            zs = [_dot(qs[a][qr], k, NT) for a in range(2)]
            das = [_dot(dob[a][qr], v, NT) for a in range(2)]
            bs, mbs = [], []
            for a in range(2):
                b, m = _log_terms(zs[a])
                if thr is not None:
                    m = jnp.where(dif[qr] > thr, m, 0.0)
                bs.append(b)
                mbs.append(m.astype(BF16))
            pl_ = [_dot(mbs[a], up_incl) for a in range(2)]
            ps, gs, gbs = [], [], []
            for a in range(2):
                p = jnp.exp(bs[a] + (jnp.tile(remv[a], (1, nrep)) - pl_[a]))
                if thr is not None:
                    p = jnp.where(dif[qr] > thr, p, 0.0)
                g = p * das[a]
                ps.append(p.astype(BF16))
                gs.append(g)
                gbs.append(g.astype(BF16))
            cl = [_dot(gbs[a], up_strict) for a in range(2)]
            dk_add = jnp.zeros((tk, LANES), F32)
            dv_add = jnp.zeros((tk, LANES), F32)
            for a in range(2):
                dz = gs[a] - jnp.exp(bs[a]) * (gs[a] + (jnp.tile(gcv[a], (1, nrep)) + cl[a]))
                if thr is not None:
                    dz = jnp.where(dif[qr] > thr, dz, 0.0)
                dzb = dz.astype(BF16)
                dqacc[a, qr, :] += _dot(dzb, k)
                dk_add += _dot(dzb, qs[a][qr], TN)
                dv_add += _dot(ps[a], dob[a][qr], TN)
                rem[a, qr, :] = remv[a] - pl_[a][:, tk - 1:tk]
                gc[a, qr, :] = gcv[a] + (cl[a][:, tk - 1:tk] + gbs[a][:, tk - 1:tk].astype(F32))
            dk_ref[rows, :] += dk_add
            dv_ref[rows, :] += dv_add

        def step(kb, carry):
            tile(kb, None)
            return carry

        lax.fori_loop(first_ref[pl.program_id(0), i].astype(jnp.int32), i * ndiag, step, 0)
        for d in range(ndiag):
            tile(i * ndiag + d, d * tk)
        dq_ref[...] = jnp.where(heads[0], dqacc[0], dqacc[1]) * 0.125

        @pl.when(step_id == npair * nq - 1)
        def _():
            cps = _chip_copies(e_in, e_out, *e_sems)
            for cp in cps:
                cp.wait_recv()
            for cp in cps:
                cp.wait_send()

    qspec = pl.BlockSpec((tq, LANES), lambda p, i: (i, p))
    full = pl.BlockSpec((s, LANES), lambda p, i: (0, p))
    any_spec = pl.BlockSpec(memory_space=pl.ANY)
    out = jax.ShapeDtypeStruct((s, DA), F32)
    res = pl.pallas_call(
        body, name=name, grid=(npair, nq),
        in_specs=[qspec, pl.BlockSpec((s, LANES), lambda p, i: (0, DA // LANES + p)),
                  pl.BlockSpec((s, LANES), lambda p, i: (0, 2 * DA // LANES + p)), qspec, qspec,
                  pl.BlockSpec(memory_space=pltpu.SMEM)] + [any_spec] * ne,
        out_specs=[qspec, full, full] + [any_spec] * ne,
        out_shape=[out, out, out] + [jax.ShapeDtypeStruct((3,) + a.shape[1:], a.dtype) for a in exchange],
        scratch_shapes=[pltpu.VMEM((2, tq, LANES), F32)] * 3
        + [pltpu.SemaphoreType.DMA((3 * ne,)), pltpu.SemaphoreType.DMA((3 * ne,))],
        compiler_params=_params("arbitrary", "arbitrary"),
    )(qkv, qkv, qkv, do, tot, first, *exchange)
    return res[0], res[1], res[2], res[3:]


def _shift_rows(v, k):
    return pltpu.roll(v, k % v.shape[0], axis=0)


def _pooled(u, uh, i, g, w, ts):
    halo = jnp.where(i > 0, uh, 0.0)
    ue = jnp.concatenate([halo, u], axis=0)
    acc, span = ue, 1
    while span < w:
        acc = acc + _shift_rows(acc, span)
        span *= 2
    tpos = i * ts + lax.broadcasted_iota(jnp.int32, (ts, 1), 0)
    cnt = jnp.minimum(tpos + 1, w).astype(F32)
    return acc[HALO:] / cnt - u


def _pool_mix(proj, o, pw, pb, ps, ag, bd, *, ts, name):
    s = proj.shape[0]
    hb = ts // HALO

    def body(u_ref, uh_ref, o_ref, pw_ref, pb_ref, ps_ref, ag_ref, bd_ref, mix_ref):
        i = pl.program_id(0)
        for g, w in enumerate(POOL_WINDOWS):
            cols = slice(g * LANES, (g + 1) * LANES)
            pooled = _pooled(u_ref[:, cols], uh_ref[:, cols], i, g, w, ts)
            yv = (_dot(pooled.astype(BF16), pw_ref[g]) + pb_ref[:, cols]) * ps_ref[:, cols]
            mix_ref[:, cols] = yv.astype(BF16)
        ov = o_ref[...]
        mix_ref[:, DP:] = (ov * _group_rsqrt(ov, bd_ref[...]) * ag_ref[...]).astype(BF16)

    return pl.pallas_call(
        body, name=name, grid=(s // ts,),
        in_specs=[pl.BlockSpec((ts, DP), lambda i: (i, 0)),
                  pl.BlockSpec((HALO, DP), lambda i: (jnp.maximum(i * hb - 1, 0), 0)),
                  pl.BlockSpec((ts, DA), lambda i: (i, 0)),
                  _full((4, LANES, LANES)), _vec(DP), _vec(DP), _vec(DA), _full((DA, DA))],
        out_specs=pl.BlockSpec((ts, D), lambda i: (i, 0)),
        out_shape=jax.ShapeDtypeStruct((s, D), BF16), compiler_params=_params("parallel"),
    )(proj, proj, o, pw, pb, ps, ag, bd)


def _res_ln_mod(x, att, gate, g, scale, shift, *, ts, name):
    s = x.shape[0]
    row = pl.BlockSpec((ts, D), lambda i: (i, 0))

    def body(x_ref, a_ref, gt_ref, g_ref, sc_ref, sh_ref, x1_ref, h_ref):
        x1 = x_ref[...] + gt_ref[...] * a_ref[...]
        x1_ref[...] = x1
        r = lax.rsqrt(jnp.mean(x1 * x1, axis=-1, keepdims=True) + EPS)
        h = (x1 * r) * g_ref[...]
        h_ref[...] = (h * (1.0 + sc_ref[...]) + sh_ref[...]).astype(BF16)

    return pl.pallas_call(
        body, name=name, grid=(s // ts,), in_specs=[row, row] + [_vec(D)] * 4, out_specs=[row, row],
        out_shape=[jax.ShapeDtypeStruct((s, D), F32), jax.ShapeDtypeStruct((s, D), BF16)],
        compiler_params=_params("parallel"),
    )(x, att, gate, g, scale, shift)


CF = DFF // 2


def _conv(u, uh, w_ref, b_ref, i):
    halo = jnp.where(i > 0, uh.astype(F32), 0.0)
    ue = jnp.concatenate([halo, u.astype(F32)], axis=0)
    y = w_ref[2:3, :] * ue + w_ref[1:2, :] * _shift_rows(ue, 1) + w_ref[0:1, :] * _shift_rows(ue, 2)
    return y[HALO:] + b_ref[...]


def _conv_gate(up, cw, cb, *, ts, name):
    s = up.shape[0]
    hb = ts // HALO

    def body(u_ref, uh_ref, w_ref, b_ref, a_ref, c_ref):
        i = pl.program_id(0)
        c = _conv(u_ref[...], uh_ref[...], w_ref, b_ref, i)
        gt, vl = c[:, :CF], c[:, CF:]
        a_ref[...] = (gt / (1.0 + jnp.exp(-gt)) * vl).astype(BF16)
        c_ref[...] = c.astype(BF16)

    return pl.pallas_call(
        body, name=name, grid=(s // ts, 2),
        in_specs=[pl.BlockSpec((ts, 2 * CF), lambda i, j: (i, j)),
                  pl.BlockSpec((HALO, 2 * CF), lambda i, j: (jnp.maximum(i * hb - 1, 0), j)),
                  pl.BlockSpec((3, 2 * CF), lambda i, j: (0, j)), pl.BlockSpec((1, 2 * CF), lambda i, j: (0, j))],
        out_specs=[pl.BlockSpec((ts, CF), lambda i, j: (i, j)), pl.BlockSpec((ts, 2 * CF), lambda i, j: (i, j))],
        out_shape=[jax.ShapeDtypeStruct((s, DFF), BF16), jax.ShapeDtypeStruct((s, 2 * DFF), BF16)],
        compiler_params=_params("parallel", "parallel"),
    )(up, up, cw, cb)


def _loss_head(x1, ffn, tgt, gate2, *, ts, name):
    s = x1.shape[0]
    n = s // ts
    row = pl.BlockSpec((ts, D), lambda i: (i, 0))
    acc8 = pl.BlockSpec((SUBLANES, D), lambda i: (0, 0))

    def body(x_ref, f_ref, t_ref, g_ref, dy_ref, df_ref, dg_ref, loss_ref, lacc):
        i = pl.program_id(0)

        @pl.when(i == 0)
        def _():
            lacc[...] = jnp.zeros_like(lacc)
            dg_ref[...] = jnp.zeros_like(dg_ref)

        f = f_ref[...]
        diff = x_ref[...] + g_ref[...] * f - t_ref[...]
        lacc[...] += _colsum8(diff * diff)
        dy = diff * (1.0 / D)
        dy_ref[...] = dy
        df_ref[...] = (dy * g_ref[...]).astype(BF16)
        dg_ref[...] += _colsum8(dy * f)

        @pl.when(i == n - 1)
        def _():
            loss_ref[...] = jnp.full((SUBLANES, LANES), (0.5 / D) * jnp.sum(lacc[...]), F32)

    return pl.pallas_call(
        body, name=name, grid=(n,), in_specs=[row, row, row, _vec(D)],
        out_specs=[row, row, acc8, _full((SUBLANES, LANES))],
        out_shape=[jax.ShapeDtypeStruct((s, D), F32), jax.ShapeDtypeStruct((s, D), BF16),
                   jax.ShapeDtypeStruct((SUBLANES, D), F32), jax.ShapeDtypeStruct((SUBLANES, LANES), F32)],
        scratch_shapes=[pltpu.VMEM((SUBLANES, D), F32)], compiler_params=_params("arbitrary"),
    )(x1, ffn, tgt, gate2)


def _gate_bwd(da, conv, *, ts, name):
    s = conv.shape[0]

    def body(da_ref, c_ref, d_ref, db_ref):
        i = pl.program_id(1)

        @pl.when(i == 0)
        def _():
            db_ref[...] = jnp.zeros_like(db_ref)

        gt, vl = c_ref[:, :CF].astype(F32), c_ref[:, CF:].astype(F32)
        sg = 1.0 / (1.0 + jnp.exp(-gt))
        dav = da_ref[...].astype(F32)
        dgt = dav * vl * (sg * (1.0 + gt * (1.0 - sg)))
        dvl = dav * (gt * sg)
        d_ref[:, :CF] = dgt.astype(BF16)
        d_ref[:, CF:] = dvl.astype(BF16)
        db_ref[:, :CF] += _colsum8(dgt)
        db_ref[:, CF:] += _colsum8(dvl)

    return pl.pallas_call(
        body, name=name, grid=(2, s // ts),
        in_specs=[pl.BlockSpec((ts, CF), lambda j, i: (i, j)),
                  pl.BlockSpec((ts, 2 * CF), lambda j, i: (i, j))],
        out_specs=[pl.BlockSpec((ts, 2 * CF), lambda j, i: (i, j)),
                   pl.BlockSpec((SUBLANES, 2 * CF), lambda j, i: (0, j))],
        out_shape=[jax.ShapeDtypeStruct((s, 2 * DFF), BF16), jax.ShapeDtypeStruct((SUBLANES, 2 * DFF), F32)],
        compiler_params=_params("parallel", "arbitrary"),
    )(da, conv)


def _conv_bwd(dc, up, cw, *, ts, tc, name):
    s = up.shape[0]
    hb = ts // HALO
    nb = s // HALO

    def body(d_ref, dn_ref, u_ref, w_ref, du_ref, dw_ref):
        i = pl.program_id(1)
        n = s // ts

        @pl.when(i == 0)
        def _():
            dw_ref[...] = jnp.zeros_like(dw_ref)

        dcur = d_ref[...].astype(F32)
        nxt = jnp.where(i < n - 1, dn_ref[...].astype(F32), 0.0)
        de = jnp.concatenate([dcur, nxt], axis=0)
        d1 = _shift_rows(de, -1)[:ts]
        d2 = _shift_rows(de, -2)[:ts]
        du_ref[...] = (w_ref[2:3, :] * dcur + w_ref[1:2, :] * d1 + w_ref[0:1, :] * d2).astype(BF16)
        u = u_ref[...].astype(F32)
        dw_ref[16:24, :] += _colsum8(dcur * u)
        dw_ref[8:16, :] += _colsum8(d1 * u)
        dw_ref[0:8, :] += _colsum8(d2 * u)

    return pl.pallas_call(
        body, name=name, grid=(2 * DFF // tc, s // ts),
        in_specs=[pl.BlockSpec((ts, tc), lambda j, i: (i, j)),
                  pl.BlockSpec((HALO, tc), lambda j, i: (jnp.minimum((i + 1) * hb, nb - 1), j)),
                  pl.BlockSpec((ts, tc), lambda j, i: (i, j)),
                  pl.BlockSpec((3, tc), lambda j, i: (0, j))],
        out_specs=[pl.BlockSpec((ts, tc), lambda j, i: (i, j)), pl.BlockSpec((24, tc), lambda j, i: (0, j))],
        out_shape=[jax.ShapeDtypeStruct((s, 2 * DFF), BF16), jax.ShapeDtypeStruct((24, 2 * DFF), F32)],
        compiler_params=_params("parallel", "arbitrary"),
    )(dc, dc, up, cw)


MXU_COLS = 256


def _sub_chunks(width):
    return [(c0, min(MXU_COLS, width - c0)) for c0 in range(0, width, MXU_COLS)]


def _up_conv_gate(h2, w_up, cw, cb, *, tm, name):
    s = h2.shape[0]
    hb = tm // HALO

    def body(a_ref, ah_ref, w_ref, cw_ref, cb_ref, up_ref, c_ref, act_ref):
        i = pl.program_id(1)
        ext = jnp.concatenate([ah_ref[...], a_ref[...]], axis=0)
        live_halo = i > 0
        for c0, cwid in _sub_chunks(CF):
            conv = []
            for off in (c0, CF + c0):
                cols = slice(off, off + cwid)
                u = _dot(ext, w_ref[cols, :], NT)
                up_ref[:, cols] = u[HALO:].astype(BF16)
                row = lax.broadcasted_iota(jnp.int32, (HALO + tm, 1), 0)
                ue = jnp.where(jnp.logical_or(row >= HALO, live_halo), u, 0.0)
                y = cw_ref[2:3, cols] * ue + cw_ref[1:2, cols] * _shift_rows(ue, 1) + cw_ref[0:1, cols] * _shift_rows(ue, 2)
                cv = y[HALO:] + cb_ref[:, cols]
                c_ref[:, cols] = cv.astype(BF16)
                conv.append(cv)
            gt, vl = conv
            act_ref[:, c0:c0 + cwid] = (gt / (1.0 + jnp.exp(-gt)) * vl).astype(BF16)

    return pl.pallas_call(
        body, name=name, grid=(2, s // tm),
        in_specs=[pl.BlockSpec((tm, D), lambda j, i: (i, 0)),
                  pl.BlockSpec((HALO, D), lambda j, i: (jnp.maximum(i * hb - 1, 0), 0)),
                  pl.BlockSpec((2 * CF, D), lambda j, i: (j, 0)),
                  pl.BlockSpec((3, 2 * CF), lambda j, i: (0, j)), pl.BlockSpec((1, 2 * CF), lambda j, i: (0, j))],
        out_specs=[pl.BlockSpec((tm, 2 * CF), lambda j, i: (i, j)), pl.BlockSpec((tm, 2 * CF), lambda j, i: (i, j)),
                   pl.BlockSpec((tm, CF), lambda j, i: (i, j))],
        out_shape=[jax.ShapeDtypeStruct((s, 2 * DFF), BF16), jax.ShapeDtypeStruct((s, 2 * DFF), BF16),
                   jax.ShapeDtypeStruct((s, DFF), BF16)],
        compiler_params=_params("parallel", "parallel"),
    )(h2, h2, w_up, cw, cb)


def _down_bwd_gate(dffn, w_down, conv, *, tm, name):
    s = dffn.shape[0]

    def body(a_ref, w_ref, c_ref, d_ref, db_ref):
        i = pl.program_id(1)

        @pl.when(i == 0)
        def _():
            db_ref[...] = jnp.zeros_like(db_ref)

        a = a_ref[...]
        for c0, cwid in _sub_chunks(CF):
            gcols, vcols = slice(c0, c0 + cwid), slice(CF + c0, CF + c0 + cwid)
            da = _dot(a, w_ref[gcols, :], NT)
            gt, vl = c_ref[:, gcols].astype(F32), c_ref[:, vcols].astype(F32)
            sg = 1.0 / (1.0 + jnp.exp(-gt))
            dgt = da * vl * (sg * (1.0 + gt * (1.0 - sg)))
            dvl = da * (gt * sg)
            d_ref[:, gcols] = dgt.astype(BF16)
            d_ref[:, vcols] = dvl.astype(BF16)
            db_ref[:, gcols] += _colsum8(dgt)
            db_ref[:, vcols] += _colsum8(dvl)

    return pl.pallas_call(
        body, name=name, grid=(2, s // tm),
        in_specs=[pl.BlockSpec((tm, D), lambda j, i: (i, 0)), pl.BlockSpec((CF, D), lambda j, i: (j, 0)),
                  pl.BlockSpec((tm, 2 * CF), lambda j, i: (i, j))],
        out_specs=[pl.BlockSpec((tm, 2 * CF), lambda j, i: (i, j)), pl.BlockSpec((SUBLANES, 2 * CF), lambda j, i: (0, j))],
        out_shape=[jax.ShapeDtypeStruct((s, 2 * DFF), BF16), jax.ShapeDtypeStruct((SUBLANES, 2 * DFF), F32)],
        compiler_params=_params("parallel", "arbitrary"),
    )(dffn, w_down, conv)


def _conv_bwd_up_bwd(dc, up, cw, w_up, *, tm, name):
    s = up.shape[0]
    hb = tm // HALO
    nb = s // HALO
    nk = 2 * DFF // CF
    n = s // tm

    def body(d_ref, dn_ref, u_ref, cw_ref, w_ref, du_ref, dh_ref, dw_ref, acc, dwacc):
        i, k = pl.program_id(0), pl.program_id(1)

        @pl.when(jnp.logical_and(i == 0, k == 0))
        def _():
            dwacc[...] = jnp.zeros_like(dwacc)

        @pl.when(k == 0)
        def _():
            acc[...] = jnp.zeros_like(acc)

        live_next = i < n - 1
        part = None
        for c0, cwid in _sub_chunks(CF):
            cols = slice(c0, c0 + cwid)
            dcur = d_ref[:, cols].astype(F32)
            de = jnp.concatenate([dcur, jnp.where(live_next, dn_ref[:, cols].astype(F32), 0.0)], axis=0)
            d1 = _shift_rows(de, -1)[:tm]
            d2 = _shift_rows(de, -2)[:tm]
            du = (cw_ref[2:3, cols] * dcur + cw_ref[1:2, cols] * d1 + cw_ref[0:1, cols] * d2).astype(BF16)
            du_ref[:, cols] = du
            prod = _dot(du, w_ref[cols, :])
            part = prod if part is None else part + prod
            u = u_ref[:, cols].astype(F32)
            for tap, dsh in ((2, dcur), (1, d1), (0, d2)):
                dwacc[k, SUBLANES * tap:SUBLANES * (tap + 1), cols] += _colsum8(dsh * u)
        acc[...] += part

        @pl.when(k == nk - 1)
        def _():
            dh_ref[...] = acc[...].astype(dh_ref.dtype)

        @pl.when(jnp.logical_and(i == n - 1, k == nk - 1))
        def _():
            dw_ref[...] = dwacc[...]

    res = pl.pallas_call(
        body, name=name, grid=(n, nk),
        in_specs=[pl.BlockSpec((tm, CF), lambda i, k: (i, k)),
                  pl.BlockSpec((HALO, CF), lambda i, k: (jnp.minimum((i + 1) * hb, nb - 1), k)),
                  pl.BlockSpec((tm, CF), lambda i, k: (i, k)),
                  pl.BlockSpec((3, CF), lambda i, k: (0, k)),
                  pl.BlockSpec((CF, D), lambda i, k: (k, 0))],
        out_specs=[pl.BlockSpec((tm, CF), lambda i, k: (i, k)), pl.BlockSpec((tm, D), lambda i, k: (i, 0)),
                   _full((nk, 24, CF))],
        out_shape=[jax.ShapeDtypeStruct((s, 2 * DFF), BF16), jax.ShapeDtypeStruct((s, D), BF16),
                   jax.ShapeDtypeStruct((nk, 24, CF), F32)],
        scratch_shapes=[pltpu.VMEM((tm, D), F32), pltpu.VMEM((nk, 24, CF), F32)],
        compiler_params=_params("arbitrary", "arbitrary"),
    )(dc, dc, up, cw, w_up)
    return res[0], res[1], jnp.transpose(res[2], (1, 0, 2)).reshape(24, 2 * DFF)


def _ln_mod_bwd(dh, xin, g, scale, resid, extra, gate, *, ts, name, side=None):
    s = xin.shape[0]
    row = pl.BlockSpec((ts, D), lambda i: (i, 0))
    acc8 = pl.BlockSpec((SUBLANES, D), lambda i: (0, 0))
    with_gate = extra is not None

    def body(*refs):
        if with_gate:
            dh_ref, x_ref, g_ref, sc_ref, r_ref, e_ref, gt_ref, dx_ref, da_ref, dsh, dsc, dg, dgt = refs
        else:
            dh_ref, x_ref, g_ref, sc_ref, r_ref, dx_ref, dsh, dsc, dg = refs
        i = pl.program_id(0)

        @pl.when(i == 0)
        def _():
            for acc in (dsh, dsc, dg) + ((dgt,) if with_gate else ()):
                acc[...] = jnp.zeros_like(acc)

        xv, dhv = x_ref[...], dh_ref[...].astype(F32)
        r = lax.rsqrt(jnp.mean(xv * xv, axis=-1, keepdims=True) + EPS)
        xn = xv * r
        dsh[...] += _colsum8(dhv)
        dsc[...] += _colsum8(dhv * (xn * g_ref[...]))
        dhp = dhv * (1.0 + sc_ref[...])
        dg[...] += _colsum8(dhp * xn)
        dxn = dhp * g_ref[...]
        dx = r_ref[...] + r * (dxn - xn * jnp.mean(dxn * xn, axis=-1, keepdims=True))
        dx_ref[...] = dx
        if with_gate:
            da_ref[...] = (dx * gt_ref[...]).astype(BF16)
            dgt[...] += _colsum8(dx * e_ref[...])

    f32o, p8 = jax.ShapeDtypeStruct((s, D), F32), jax.ShapeDtypeStruct((SUBLANES, D), F32)
    if with_gate:
        ins, in_specs = (dh, xin, g, scale, resid, extra, gate), [row, row, _vec(D), _vec(D), row, row, _vec(D)]
        out_specs, out_shape = [row, row, acc8, acc8, acc8, acc8], [f32o, jax.ShapeDtypeStruct((s, D), BF16), p8, p8, p8, p8]
    else:
        ins, in_specs = (dh, xin, g, scale, resid), [row, row, _vec(D), _vec(D), row]
        out_specs, out_shape = [row, acc8, acc8, acc8], [f32o, p8, p8, p8]
    return _row_call(body, side, name=name, steps=s // ts, in_specs=in_specs, out_specs=out_specs,
                     out_shape=out_shape, ins=ins)


def _group_norm_bwd(t, dn_out, gvec, bd):
    r = _group_rsqrt(t, bd)
    dg_terms = dn_out * t * r
    dn = dn_out * gvec
    dt = r * (dn - t * (r * r) * (_split_dot(dn * t, bd) * (1.0 / HD)))
    return dt, dg_terms


def _mix_bwd(dmix, proj, o, pw, pb, ps, ag, bd, *, ts, name, side=None):
    s = proj.shape[0]
    hb = ts // HALO
    nb = s // HALO

    def body(dm_ref, dmn_ref, u_ref, uh_ref, o_ref, pw_ref, pb_ref, ps_ref, ag_ref, bd_ref,
             du_ref, do_ref, dpw_ref, dpb_ref, dps_ref, dag_ref):
        i = pl.program_id(0)
        n = s // ts

        @pl.when(i == 0)
        def _():
            for acc in (dpw_ref, dpb_ref, dps_ref, dag_ref):
                acc[...] = jnp.zeros_like(acc)

        for g, w in enumerate(POOL_WINDOWS):
            cols = slice(g * LANES, (g + 1) * LANES)
            wg = pw_ref[g]
            psg = ps_ref[:, cols]
            pooled = _pooled(u_ref[:, cols], uh_ref[:, cols], i, g, w, ts).astype(BF16)
            dy = dm_ref[:, cols].astype(F32)
            dps_ref[:, cols] += _colsum8(dy * (_dot(pooled, wg) + pb_ref[:, cols]))
            dpre = dy * psg
            dpb_ref[:, cols] += _colsum8(dpre)
            dpreb = dpre.astype(BF16)
            dpw_ref[g * LANES:(g + 1) * LANES, :] += _dot(pooled, dpreb, TN)
            dpool = _dot(dpreb, wg, NT)
            dnext = _dot((dmn_ref[:, cols].astype(F32) * psg).astype(BF16), wg, NT)
            dpe = jnp.concatenate([dpool, jnp.where(i < n - 1, dnext, 0.0)], axis=0)
            tpos = i * ts + lax.broadcasted_iota(jnp.int32, (ts + HALO, 1), 0)
            acc = dpe / jnp.minimum(tpos + 1, w).astype(F32)
            span = 1
            while span < w:
                acc = acc + _shift_rows(acc, -span)
                span *= 2
            du_ref[:, cols] = acc[:ts] - dpool
        ov = o_ref[...]
        dov, dg_terms = _group_norm_bwd(ov, dm_ref[:, DP:].astype(F32), ag_ref[...], bd_ref[...])
        do_ref[...] = dov
        dag_ref[...] += _colsum8(dg_terms)

    p8 = jax.ShapeDtypeStruct((SUBLANES, DP), F32)
    acc8 = pl.BlockSpec((SUBLANES, DP), lambda i: (0, 0))
    half = pl.BlockSpec((ts, DP), lambda i: (i, 0))
    return _row_call(
        body, side, name=name, steps=s // ts,
        in_specs=[pl.BlockSpec((ts, D), lambda i: (i, 0)),
                  pl.BlockSpec((HALO, DP), lambda i: (jnp.minimum((i + 1) * hb, nb - 1), 0)),
                  half, pl.BlockSpec((HALO, DP), lambda i: (jnp.maximum(i * hb - 1, 0), 0)),
                  half, _full((4, LANES, LANES)), _vec(DP), _vec(DP), _vec(DA), _full((DA, DA))],
        out_specs=[half, half, _full((DP, LANES)), acc8, acc8, acc8],
        out_shape=[jax.ShapeDtypeStruct((s, DP), F32), jax.ShapeDtypeStruct((s, DA), F32),
                   jax.ShapeDtypeStruct((DP, LANES), F32), p8, p8, p8],
        ins=(dmix, dmix, proj, proj, o, pw, pb, ps, ag, bd))


def _qk_norm_bwd(du, dq, dk, dv, proj, qg, kg, bd, *, ts, name):
    s = proj.shape[0]

    def body(du_ref, dq_ref, dk_ref, dv_ref, q_ref, k_ref, qg_ref, kg_ref, bd_ref, dp_ref, dqg_ref, dkg_ref):
        i = pl.program_id(0)

        @pl.when(i == 0)
        def _():
            dqg_ref[...] = jnp.zeros_like(dqg_ref)
            dkg_ref[...] = jnp.zeros_like(dkg_ref)

        bdv = bd_ref[...]
        dqr, tq = _group_norm_bwd(q_ref[...], dq_ref[...], qg_ref[...], bdv)
        dkr, tk = _group_norm_bwd(k_ref[...], dk_ref[...], kg_ref[...], bdv)
        dqg_ref[...] += _colsum8(tq)
        dkg_ref[...] += _colsum8(tk)
        dp_ref[:, 0:DP] = du_ref[...].astype(BF16)
        dp_ref[:, DP:DP + DA] = dqr.astype(BF16)
        dp_ref[:, DP + DA:DP + 2 * DA] = dkr.astype(BF16)
        dp_ref[:, DP + 2 * DA:] = dv_ref[...].astype(BF16)

    half = pl.BlockSpec((ts, DA), lambda i: (i, 0))
    col = lambda j: pl.BlockSpec((ts, DA), lambda i: (i, j))
    acc8 = pl.BlockSpec((SUBLANES, DA), lambda i: (0, 0))
    p8 = jax.ShapeDtypeStruct((SUBLANES, DA), F32)
    return pl.pallas_call(
        body, name=name, grid=(s // ts,),
        in_specs=[half, half, half, half, col(1), col(2), _vec(DA), _vec(DA), _full((DA, DA))],
        out_specs=[pl.BlockSpec((ts, DIN), lambda i: (i, 0)), acc8, acc8],
        out_shape=[jax.ShapeDtypeStruct((s, DIN), BF16), p8, p8],
        compiler_params=_params("arbitrary"),
    )(du, dq, dk, dv, proj, proj, qg, kg, bd)


def _split3(a):
    hi = a.astype(BF16)
    return hi, (a - hi.astype(F32)).astype(BF16)


def _dot3(a, b, dn):
    ah, al = _split3(a)
    bh, bl = _split3(b)
    return _dot(ah, bh, dn) + (_dot(ah, bl, dn) + _dot(al, bh, dn))


def _ada_fwd(c_all, w, b, name):
    nw = w.shape[1]

    def body(c_ref, w_ref, b_ref, o_ref):
        cv = c_ref[...]
        act = cv / (1.0 + jnp.exp(-cv))
        o_ref[...] = _dot3(act, w_ref[...], NN) + b_ref[...]

    return pl.pallas_call(
        body, name=name, in_specs=[_full((NDEV, D)), _full(w.shape), _full((1, nw))], out_specs=_full((NDEV, nw)),
        out_shape=jax.ShapeDtypeStruct((NDEV, nw), F32), grid=(1,), compiler_params=_params("arbitrary"),
    )(c_all, w, b)


def _ada_bwd(c_all, dmod, name):
    nw = dmod.shape[1]

    def body(c_ref, d_ref, o_ref):
        cv = c_ref[...]
        act = cv / (1.0 + jnp.exp(-cv))
        o_ref[...] = _dot3(act, d_ref[...], TN)[None]

    return pl.pallas_call(
        body, name=name, in_specs=[_full((NDEV, D)), _full((NDEV, nw))], out_specs=_full((1, D, nw)),
        out_shape=jax.ShapeDtypeStruct((1, D, nw), F32), grid=(1,), compiler_params=_params("arbitrary"),
    )(c_all, dmod)


def _fold_heads(v):
    acc = v[:, 0:HD]
    for h in range(1, DA // HD):
        acc = acc + v[:, h * HD:(h + 1) * HD]
    return acc


def _pack_partials(pieces, dcw_p, name):
    n_p = len(pieces)
    total = sum(p.shape[1] for p in pieces) + 3 * dcw_p.shape[1]
    npack = -(-total // (SUBLANES * LANES)) * (SUBLANES * LANES)

    def body(*refs):
        out = refs[-1]
        off = 0
        for r in refs[:n_p]:
            out[:, off:off + r.shape[1]] = jnp.sum(r[...], axis=0, keepdims=True)
            off += r.shape[1]
        dw = refs[n_p]
        for tap in range(3):
            out[:, off:off + dw.shape[1]] = jnp.sum(dw[SUBLANES * tap:SUBLANES * (tap + 1), :], axis=0, keepdims=True)
            off += dw.shape[1]
        if off < npack:
            out[:, off:] = jnp.zeros((1, npack - off), F32)

    arrs = list(pieces) + [dcw_p]
    return pl.pallas_call(
        body, name=name, grid=(1,), in_specs=[_full(a.shape) for a in arrs], out_specs=_full((1, npack)),
        out_shape=jax.ShapeDtypeStruct((1, npack), F32), compiler_params=_params("arbitrary"),
    )(*arrs)


def _small_update(gathered, gathered_pw, gathered_cw, specs, params, loss_off, name):
    names = [sp[0] for sp in specs]
    flat = []
    for nme in names + ["pool_w", "conv_w"]:
        flat += list(params[nme])
    n_in = len(flat)

    def body(*refs):
        ga_ref, gp_ref, gc_ref = refs[0], refs[1], refs[2]
        prm = refs[3:3 + n_in]
        outs = refs[3 + n_in:]
        total = ga_ref[0:1, :]
        for dv in range(1, NDEV):
            total = total + ga_ref[dv:dv + 1, :]
        k = 0
        for idx, (nme, off, width, fold) in enumerate(specs):
            g = total[:, off:off + width]
            if fold:
                g = _fold_heads(g)
            w_ref, m_ref, v_ref = prm[3 * idx:3 * idx + 3]
            d, nm, nv = _adamw_math(w_ref[...], g, m_ref[...], v_ref[...])
            for val in (g, d, nm, nv):
                outs[k][...] = val
                k += 1
        gpw = gp_ref[0]
        for dv in range(1, NDEV):
            gpw = gpw + gp_ref[dv]
        w_ref, m_ref, v_ref = prm[3 * len(specs):3 * len(specs) + 3]
        d, nm, nv = _adamw_math(w_ref[...], gpw, m_ref[...], v_ref[...])
        for val in (gpw, d, nm, nv):
            outs[k][...] = val
            k += 1
        gcw = gc_ref[0]
        for dv in range(1, NDEV):
            gcw = gcw + gc_ref[dv]
        w_ref, m_ref, v_ref = prm[3 * len(specs) + 3:3 * len(specs) + 6]
        d, nm, nv = _adamw_math(w_ref[...], gcw, m_ref[...], v_ref[...])
        for val in (gcw, d, nm, nv):
            outs[k][...] = val
            k += 1
        outs[k][...] = ga_ref[:, 0:6 * D]
        outs[k + 1][...] = total[:, loss_off:loss_off + LANES] * (1.0 / SUBLANES)

    out_shape, out_specs = [], []
    for nme in names + ["pool_w", "conv_w"]:
        shp = params[nme][0].shape
        out_shape += [jax.ShapeDtypeStruct(shp, F32)] * 4
        out_specs += [_full(shp)] * 4
    out_shape += [jax.ShapeDtypeStruct((NDEV, 6 * D), F32), jax.ShapeDtypeStruct((1, LANES), F32)]
    out_specs += [_full((NDEV, 6 * D)), _full((1, LANES))]
    res = pl.pallas_call(
        body, name=name, grid=(1,),
        in_specs=[_full(gathered.shape), _full(gathered_pw.shape), _full(gathered_cw.shape)] + [_full(a.shape) for a in flat],
        out_specs=out_specs, out_shape=out_shape, compiler_params=_params("arbitrary"),
    )(gathered, gathered_pw, gathered_cw, *flat)
    out = {nme: tuple(res[4 * i:4 * i + 4]) for i, nme in enumerate(names + ["pool_w", "conv_w"])}
    return out, res[-2], res[-1][0, 0]


def _row_tile(s):
    return 512 if s % 512 == 0 else s


def kernel(x, c, ada_w, ada_b, norm1_g, w_in, pool_w, pool_b, pool_scale, q_norm_g, k_norm_g, attn_out_g, w_out, norm2_g, w_up, conv_w, conv_b, w_down, loss_target, m_ada_w, m_ada_b, m_norm1_g, m_w_in, m_pool_w, m_pool_b, m_pool_scale, m_q_norm_g, m_k_norm_g, m_attn_out_g, m_w_out, m_norm2_g, m_w_up, m_conv_w, m_conv_b, m_w_down, v_ada_w, v_ada_b, v_norm1_g, v_w_in, v_pool_w, v_pool_b, v_pool_scale, v_q_norm_g, v_k_norm_g, v_attn_out_g, v_w_out, v_norm2_g, v_w_up, v_conv_w, v_conv_b, v_w_down):
    ax, ay, ac = lax.axis_index("x"), lax.axis_index("y"), lax.axis_index("c")
    me = 4 * ax + 2 * ay + ac
    me_swapped = 4 * ay + 2 * ax + ac
    xs, tgt = x[0], loss_target[0]
    s = xs.shape[0]
    ts = _row_tile(s)
    tq_attn, tk_attn = 512, 256
    tmm = 2 * ts
    bd = _block_diag_ones(DA, HD)

    w_in_t = w_in[0].T.astype(BF16)
    w_up_t = w_up[0].T.astype(BF16)
    c_gath, gw_in, gcw = _all_gather([jnp.broadcast_to(c, (SUBLANES, D)), w_in_t, jnp.pad(conv_w[0], ((0, 5), (0, 64)))],
                                     [False, False, True], "gather_in")
    c_all = c_gath[:, 0, :]
    n_ada = ada_w.shape[2]
    ada_b_mine = lax.dynamic_slice_in_dim(ada_b, me * n_ada, n_ada, axis=1)
    mod_part = _ada_fwd(c_all, ada_w[0], ada_b_mine, "ada_fwd")
    mod_all = _all_gather([mod_part], [False], "gather_mod")[0]
    mod = lax.dynamic_index_in_dim(mod_all, me, axis=1, keepdims=False).reshape(1, 6 * D)
    shift1, scale1, gate1, shift2, scale2, gate2 = [mod[:, k * D:(k + 1) * D] for k in range(6)]

    w_in_full = gw_in.reshape(DIN, D)
    later_w = [w_out[0].astype(BF16), w_up_t, w_down[0].astype(BF16)]
    cw_full = jnp.transpose(gcw[:, :3, :704], (1, 0, 2)).reshape(3, 2 * DFF)
    cb_full = jnp.transpose(conv_b.reshape(1, 2, 2, 2, 704), (0, 2, 1, 3, 4)).reshape(1, 2 * DFF)

    qg = jnp.tile(q_norm_g, (1, DA // HD))
    kg = jnp.tile(k_norm_g, (1, DA // HD))
    ag = attn_out_g.reshape(1, DA)
    pw = pool_w[0].astype(BF16)
    pb = pool_b.reshape(1, DP)
    h1 = _ln_mod(xs, norm1_g, scale1, shift1, ts=ts, name="ln1")
    proj = _matmul(h1, w_in_full, mode="nt", out_dtype=F32, tm=tmm,tn=DIN, tk=D, name="in_proj")
    qkv = _qk_norm(proj, qg, kg, bd, ts=ts, name="qk_norm")
    o_raw, m_tot, kb_first, (gw_out, gw_up, gw_down) = _attn_fwd(
        qkv, later_w, [False, True, False], tq=tq_attn, tk=tk_attn, name="attn_fwd")
    w_out_full = gw_out.reshape(D, D)
    w_up_full = gw_up.reshape(2 * DFF, D)
    w_down_full = gw_down.reshape(DFF, D)
    mix = _pool_mix(proj, o_raw, pw, pb, pool_scale, ag, bd, ts=ts, name="pool_mix")
    att = _matmul(mix, w_out_full, mode="nn", out_dtype=F32, tm=tmm,tn=D, tk=D, name="out_proj")
    x1, h2 = _res_ln_mod(xs, att, gate1, norm2_g, scale2, shift2, ts=ts, name="res_ln2")
    up, conv, act = _up_conv_gate(h2, w_up_full, cw_full, cb_full, tm=ts, name="up_conv_gate")
    ffn = _matmul(act, w_down_full, mode="nn", out_dtype=F32, tm=tmm,tn=D, tk=DFF, name="down_proj")
    dy, dffn, dgate2_p, loss_p = _loss_head(x1, ffn, tgt, gate2, ts=ts, name="loss_head")

    g_w_down = _matmul(act, dffn, mode="tn", out_dtype=F32, tm=CF, tn=D, tk=tmm,name="down_wgrad")
    dconv, dcb_p = _down_bwd_gate(dffn, w_down_full, conv, tm=tmm, name="down_bwd_gate")
    dup, dh2, dcw_p = _conv_bwd_up_bwd(dconv, up, cw_full, w_up_full, tm=tmm, name="conv_bwd_up_bwd")
    g_w_up_t = _matmul(dup, h2, mode="tn", out_dtype=F32, tm=CF, tn=D, tk=tmm,name="up_wgrad")
    (dx1, datt, dshift2_p, dscale2_p, dnorm2_p, dgate1_p), _ = _ln_mod_bwd(
        dh2, x1, norm2_g, scale2, dy, att, gate1, ts=ts, name="ln2_bwd")

    dmix = _matmul(datt, w_out_full, mode="nt", out_dtype=BF16, tm=tmm,tn=D, tk=D, name="out_bwd")
    g_w_out = _matmul(mix, datt, mode="tn", out_dtype=F32, tm=D, tn=D, tk=tmm,name="out_wgrad")
    core = jnp.reshape(ac, (1,)).astype(jnp.int32)
    chip = jnp.reshape(2 * ax + ay, (1,)).astype(jnp.int32)
    big_ffn = [g_w_up_t.reshape(NDEV, 2 * DFF // NDEV, D), g_w_down.reshape(NDEV, DFF // NDEV, D),
               g_w_out.reshape(NDEV, D // NDEV, D)]
    swaps_ffn = [True, False, False]
    (du, do_raw, g_pw_p, dpb_p, dps_p, dag_p), gots_ffn = _mix_bwd(
        dmix, proj, o_raw, pw, pb, pool_scale, ag, bd, ts=ts, name="mix_bwd", side=_pair_side(big_ffn, swaps_ffn))
    sums_ffn = [_pair_sum(big_ffn[k], gots_ffn[k], swaps_ffn[k], core, "rs_pair_sum_ffn%d" % k) for k in range(3)]
    dqn, dkn, dvv, parts_ffn = _attn_bwd(qkv, do_raw, m_tot, kb_first, sums_ffn, tq=tq_attn, tk=tk_attn, name="attn_bwd")
    dproj, dqg_p, dkg_p = _qk_norm_bwd(du, dqn, dkn, dvv, proj, qg, kg, bd, ts=ts, name="qk_norm_bwd")
    g_w_in_t = _matmul(dproj, h1, mode="tn", out_dtype=F32, tm=DIN // 2, tn=D, tk=tmm,name="in_wgrad")
    big = [g_w_in_t.reshape(NDEV, DIN // NDEV, D)]
    gots = _pair_exchange(big, [False], "rs_pair")
    sums = [_pair_sum(big[0], gots[0], False, core, "rs_pair_sum")]
    dh1, parts = _matmul(dproj, w_in_full, mode="nn", out_dtype=BF16, tm=tmm,tn=D, tk=DIN, name="in_bwd",
                         side=_chip_side(sums))
    (grad_x, dshift1_p, dscale1_p, dnorm1_p), _ = _ln_mod_bwd(
        dh1, xs, norm1_g, scale1, dx1, None, None, ts=ts, name="ln1_bwd")

    tr = lambda a: a[0].T
    r_in = _adamw_reduce(tr(w_in), tr(m_w_in), tr(v_w_in), sums[0], parts[0], chip, "adamw_w_in")
    r_out = _adamw_reduce(w_out[0], m_w_out[0], v_w_out[0], sums_ffn[2], parts_ffn[2], chip, "adamw_w_out")
    r_up = _adamw_reduce(tr(w_up), tr(m_w_up), tr(v_w_up), sums_ffn[0], parts_ffn[0], chip, "adamw_w_up")
    r_down = _adamw_reduce(w_down[0], m_w_down[0], v_w_down[0], sums_ffn[1], parts_ffn[1], chip, "adamw_w_down")
    r_in = [a.T[None] for a in r_in]
    r_up = [a.T[None] for a in r_up]
    r_out = [a[None] for a in r_out]
    r_down = [a[None] for a in r_down]

    dcb_nat = jnp.transpose(dcb_p.reshape(SUBLANES, 2, 2, 2, 704), (0, 2, 1, 3, 4)).reshape(SUBLANES, 2 * DFF)
    pieces = [dshift1_p, dscale1_p, dgate1_p, dshift2_p, dscale2_p, dgate2_p,
              dnorm1_p, dnorm2_p, dcb_nat, dpb_p, dps_p, dag_p, dqg_p, dkg_p, loss_p]
    n_vec = sum(p.shape[1] for p in pieces)
    packed = _pack_partials(pieces, dcw_p, "pack_partials")
    npack = packed.shape[1]
    gathered, gathered_pw = _all_gather([packed.reshape(SUBLANES, npack // SUBLANES), g_pw_p], [False, False], "gather_small")
    gathered = gathered.reshape(NDEV, npack)
    gathered_cw = lax.dynamic_index_in_dim(
        gathered[:, n_vec:n_vec + 6 * DFF].reshape(NDEV, 3, NDEV, 704), me_swapped, axis=2, keepdims=False)
    specs = [("ada_b", 0, 6 * D, False)]
    off = 6 * D
    for nme, width, fold in (("norm1_g", D, False), ("norm2_g", D, False), ("conv_b", 2 * DFF, False),
                             ("pool_b", DP, False), ("pool_scale", DP, False), ("attn_out_g", DA, False),
                             ("q_norm_g", DA, True), ("k_norm_g", DA, True)):
        specs.append((nme, off, width, fold))
        off += width
    small = {
        "ada_b": (ada_b, m_ada_b, v_ada_b),
        "norm1_g": (norm1_g, m_norm1_g, v_norm1_g), "norm2_g": (norm2_g, m_norm2_g, v_norm2_g),
        "conv_b": (conv_b, m_conv_b, v_conv_b),
        "pool_b": (pb, m_pool_b.reshape(1, DP), v_pool_b.reshape(1, DP)),
        "pool_scale": (pool_scale, m_pool_scale, v_pool_scale),
        "attn_out_g": (ag, m_attn_out_g.reshape(1, DA), v_attn_out_g.reshape(1, DA)),
        "q_norm_g": (q_norm_g, m_q_norm_g, v_q_norm_g), "k_norm_g": (k_norm_g, m_k_norm_g, v_k_norm_g),
        "pool_w": (pool_w.reshape(DP, LANES), m_pool_w.reshape(DP, LANES), v_pool_w.reshape(DP, LANES)),
        "conv_w": (conv_w[0], m_conv_w[0], v_conv_w[0]),
    }
    upd, dmod_all, loss = _small_update(gathered, gathered_pw, gathered_cw, specs, small, off, "small_update")
    g_ada_w = _ada_bwd(c_all, lax.dynamic_slice_in_dim(dmod_all, me * n_ada, n_ada, axis=1), "ada_bwd")
    r_ada = [g_ada_w] + [a[None] for a in _adamw(ada_w[0], m_ada_w[0], v_ada_w[0], g_ada_w[0], "adamw_ada_w")]

    shapes = {"ada_b": ada_b.shape, "norm1_g": norm1_g.shape, "pool_w": pool_w.shape, "pool_b": pool_b.shape,
              "pool_scale": pool_scale.shape, "q_norm_g": q_norm_g.shape, "k_norm_g": k_norm_g.shape,
              "attn_out_g": attn_out_g.shape, "norm2_g": norm2_g.shape, "conv_w": conv_w.shape, "conv_b": conv_b.shape}
    res = {nme: [a.reshape(shapes[nme]) for a in upd[nme]] for nme in shapes}
    res.update(ada_w=r_ada, w_in=r_in, w_out=r_out, w_up=r_up, w_down=r_down)
    names = ["ada_w", "ada_b", "norm1_g", "w_in", "pool_w", "pool_b", "pool_scale", "q_norm_g", "k_norm_g",
             "attn_out_g", "w_out", "norm2_g", "w_up", "conv_w", "conv_b", "w_down"]
    outs = [loss, grad_x[None]]
    for q in range(4):
        outs += [res[nme][q] for nme in names]
    return tuple(outs)
```

```python
import functools
import math

import numpy as np
import jax
import jax.numpy as jnp
from jax import lax
from jax.experimental import pallas as pl
from jax.experimental.pallas import tpu as pltpu

F32, BF16 = jnp.float32, jnp.bfloat16
D = 1024
DP = 512
DA = 512
HD = 64
DIN = DP + 3 * DA
DFF = 2816
POOL_WINDOWS = (2, 4, 8, 16)
HALO = 16
EPS = 1e-6
LANES = 128
SUBLANES = 8
NDEV = 8
VMEM_LIMIT = 56 * 1024 * 1024
MESH = pl.DeviceIdType.MESH

ADAM_LR, ADAM_B1, ADAM_B2, ADAM_EPS, ADAM_WD, ADAM_STEP = 0.001, 0.9, 0.999, 1e-08, 0.01, 10

NN = (((1,), (0,)), ((), ()))
NT = (((1,), (1,)), ((), ()))
TN = (((0,), (0,)), ((), ()))


def _params(*sem):
    return pltpu.CompilerParams(dimension_semantics=sem, vmem_limit_bytes=VMEM_LIMIT)


def _full(shape):
    nd = len(shape)
    return pl.BlockSpec(shape, lambda *_: (0,) * nd)


def _dot(a, b, dn=NN):
    return lax.dot_general(a, b, dn, preferred_element_type=F32)


def _split_dot(a, b, dn=NN):
    hi = a.astype(BF16)
    lo = (a - hi.astype(F32)).astype(BF16)
    return _dot(hi, b, dn) + _dot(lo, b, dn)


def _colsum8(v):
    r, n = v.shape
    return v.reshape(r // SUBLANES, SUBLANES, n).sum(axis=0)


def _block_diag_ones(n, blk):
    i = np.arange(n) // blk
    return jnp.asarray((i[:, None] == i[None, :]).astype(np.float32), BF16)


def _matmul(a, b, *, mode, out_dtype, tm, tn, tk, name, n_outer=False, side=None):
    if mode == "tn":
        K, M = a.shape
        N = b.shape[1]
    elif mode == "nt":
        M, K = a.shape
        N = b.shape[0]
    else:
        M, K = a.shape
        N = b.shape[1]
    tm, tn, tk = min(tm, M), min(tn, N), min(tk, K)
    assert M % tm == 0 and N % tn == 0 and K % tk == 0, (name, M, N, K, tm, tn, tk)
    nk = K // tk
    dn = {"nn": NN, "nt": NT, "tn": TN}[mode]

    def body(a_ref, b_ref, o_ref, *acc):
        if nk == 1:
            o_ref[...] = _dot(a_ref[...], b_ref[...], dn).astype(o_ref.dtype)
            return
        acc_ref, = acc
        k = pl.program_id(2)

        @pl.when(k == 0)
        def _():
            acc_ref[...] = jnp.zeros_like(acc_ref)

        acc_ref[...] += _dot(a_ref[...], b_ref[...], dn)

        @pl.when(k == nk - 1)
        def _():
            o_ref[...] = acc_ref[...].astype(o_ref.dtype)

    if n_outer:
        gi = lambda g: (g[1], g[0], g[2])
        grid = (N // tn, M // tm, nk)
    else:
        gi = lambda g: g
        grid = (M // tm, N // tn, nk)

    def amap(*g):
        i, j, k = gi(g)
        return (k, i) if mode == "tn" else (i, k)

    def bmap(*g):
        i, j, k = gi(g)
        return (j, k) if mode == "nt" else (k, j)

    def omap(*g):
        i, j, k = gi(g)
        return (i, j)

    a_blk = (tk, tm) if mode == "tn" else (tm, tk)
    b_blk = (tn, tk) if mode == "nt" else (tk, tn)
    acc_scratch = [] if nk == 1 else [pltpu.VMEM((tm, tn), F32)]
    if side is None:
        return pl.pallas_call(
            body, name=name, grid=grid,
            in_specs=[pl.BlockSpec(a_blk, amap), pl.BlockSpec(b_blk, bmap)],
            out_specs=pl.BlockSpec((tm, tn), omap),
            out_shape=jax.ShapeDtypeStruct((M, N), out_dtype),
            scratch_shapes=acc_scratch,
            compiler_params=_params("parallel", "parallel", "arbitrary"),
        )(a, b)

    ne = len(side.arrs)
    steps = grid[0] * grid[1] * grid[2]

    def with_side(*refs):
        e_in, e_out = refs[2:2 + ne], refs[3 + ne:3 + 2 * ne]
        sems = refs[len(refs) - 2:]
        step = (pl.program_id(0) * grid[1] + pl.program_id(1)) * grid[2] + pl.program_id(2)

        @pl.when(step == 0)
        def _():
            for cp in side.make(e_in, e_out, *sems):
                cp.start()

        body(refs[0], refs[1], refs[2 + ne], *refs[3 + 2 * ne:len(refs) - 2])

        @pl.when(step == steps - 1)
        def _():
            cps = side.make(e_in, e_out, *sems)
            for cp in cps:
                cp.wait_recv()
            for cp in cps:
                cp.wait_send()

    any_spec = pl.BlockSpec(memory_space=pl.ANY)
    res = pl.pallas_call(
        with_side, name=name, grid=grid,
        in_specs=[pl.BlockSpec(a_blk, amap), pl.BlockSpec(b_blk, bmap)] + [any_spec] * ne,
        out_specs=[pl.BlockSpec((tm, tn), omap)] + [any_spec] * ne,
        out_shape=[jax.ShapeDtypeStruct((M, N), out_dtype)] + side.out_shapes,
        scratch_shapes=acc_scratch + [pltpu.SemaphoreType.DMA((side.n_copies,)), pltpu.SemaphoreType.DMA((side.n_copies,))],
        compiler_params=_params("arbitrary", "arbitrary", "arbitrary"),
    )(a, b, *side.arrs)
    return res[0], list(res[1:])


def _slot(swap, px, py, pc):
    return 4 * py + 2 * px + pc if swap else 4 * px + 2 * py + pc


class _Gather:
    def __init__(self, ins, outs, send, recv, loc, swaps):
        self.ins, self.outs, self.send, self.recv, self.loc, self.swaps = ins, outs, send, recv, loc, swaps
        x, y, c = lax.axis_index("x"), lax.axis_index("y"), lax.axis_index("c")
        self.me, self.sib = (x, y, c), (x, y, 1 - c)
        self.chips = [(1 - x, y), (x, 1 - y), (1 - x, 1 - y)]
        self.n = len(ins)

    @staticmethod
    def scratch(n):
        return [pltpu.SemaphoreType.DMA((7 * n,)), pltpu.SemaphoreType.DMA((7 * n,)), pltpu.SemaphoreType.DMA((n,))]

    def copy(self, a, k, blk, to, src=None):
        rows = self.outs[a].at[_slot(self.swaps[a], *blk)]
        return pltpu.make_async_remote_copy(
            src_ref=rows if src is None else src, dst_ref=rows,
            send_sem=self.send.at[7 * a + k], recv_sem=self.recv.at[7 * a + k], device_id=to, device_id_type=MESH)

    def mine(self, a):
        return pltpu.make_async_copy(self.ins[a], self.outs[a].at[_slot(self.swaps[a], *self.me)], self.loc.at[a])

    def first(self, a):
        c = self.me[2]
        return [self.copy(a, 0, self.me, self.sib, src=self.ins[a])] + [
            self.copy(a, 1 + j, self.me, (*chip, c), src=self.ins[a]) for j, chip in enumerate(self.chips)]

    def forwards(self, a):
        c = self.me[2]
        return [self.copy(a, 4 + j, (*chip, c), self.sib) for j, chip in enumerate(self.chips)]

    def start(self):
        for a in range(self.n):
            self.mine(a).start()
        for a in range(self.n):
            for cp in self.first(a):
                cp.start()

    def forward(self):
        c = self.me[2]
        for a in range(self.n):
            fwd = self.forwards(a)
            for j, chip in enumerate(self.chips):
                self.copy(a, 1 + j, (*chip, c), self.me).wait_recv()
                fwd[j].start()

    def finish(self):
        c = self.me[2]
        for a in range(self.n):
            self.copy(a, 0, self.sib, self.me).wait_recv()
            for j, chip in enumerate(self.chips):
                self.copy(a, 4 + j, (*chip, 1 - c), self.me).wait_recv()
        for a in range(self.n):
            for cp in self.first(a) + self.forwards(a):
                cp.wait_send()
            self.mine(a).wait()


def _all_gather(arrs, swaps, name):
    n = len(arrs)

    def body(*refs):
        g = _Gather(refs[:n], refs[n:2 * n], *refs[2 * n:], swaps)
        g.start()
        g.forward()
        g.finish()

    any_spec = pl.BlockSpec(memory_space=pl.ANY)
    return pl.pallas_call(
        body, name=name,
        in_specs=[any_spec] * n, out_specs=[any_spec] * n,
        out_shape=[jax.ShapeDtypeStruct((NDEV,) + a.shape, a.dtype) for a in arrs],
        scratch_shapes=_Gather.scratch(n),
    )(*arrs)


def _pair_copies(ins, gots, send, recv, swaps):
    x, y, c = lax.axis_index("x"), lax.axis_index("y"), lax.axis_index("c")
    return [pltpu.make_async_remote_copy(
        src_ref=ins[a].at[_slot(swaps[a], k // 2, k % 2, 1 - c)], dst_ref=gots[a].at[k],
        send_sem=send.at[4 * a + k], recv_sem=recv.at[4 * a + k], device_id=(x, y, 1 - c), device_id_type=MESH)
        for a in range(len(ins)) for k in range(4)]


def _pair_exchange(arrs, swaps, name):
    n = len(arrs)

    def body(*refs):
        rems = _pair_copies(refs[:n], refs[n:2 * n], *refs[2 * n:], swaps)
        for rc in rems:
            rc.start()
        for rc in rems:
            rc.wait_recv()
        for rc in rems:
            rc.wait_send()

    any_spec = pl.BlockSpec(memory_space=pl.ANY)
    return pl.pallas_call(
        body, name=name,
        in_specs=[any_spec] * n, out_specs=[any_spec] * n,
        out_shape=[jax.ShapeDtypeStruct((4,) + a.shape[1:], a.dtype) for a in arrs],
        scratch_shapes=[pltpu.SemaphoreType.DMA((4 * n,)), pltpu.SemaphoreType.DMA((4 * n,))],
    )(*arrs)


def _chip_copies(ins, outs, send, recv):
    x, y, c = lax.axis_index("x"), lax.axis_index("y"), lax.axis_index("c")
    chips = [(1 - x, y), (x, 1 - y), (1 - x, 1 - y)]
    return [pltpu.make_async_remote_copy(
        src_ref=ins[a].at[2 * px + py], dst_ref=outs[a].at[j], send_sem=send.at[3 * a + j], recv_sem=recv.at[3 * a + j],
        device_id=(px, py, c), device_id_type=MESH) for a in range(len(ins)) for j, (px, py) in enumerate(chips)]


def _chip_exchange(arrs, name):
    n = len(arrs)

    def body(*refs):
        rems = _chip_copies(refs[:n], refs[n:2 * n], *refs[2 * n:])
        for rc in rems:
            rc.start()
        for rc in rems:
            rc.wait_recv()
        for rc in rems:
            rc.wait_send()

    any_spec = pl.BlockSpec(memory_space=pl.ANY)
    return pl.pallas_call(
        body, name=name,
        in_specs=[any_spec] * n, out_specs=[any_spec] * n,
        out_shape=[jax.ShapeDtypeStruct((3,) + a.shape[1:], a.dtype) for a in arrs],
        scratch_shapes=[pltpu.SemaphoreType.DMA((3 * n,)), pltpu.SemaphoreType.DMA((3 * n,))],
    )(*arrs)


class _Side:
    def __init__(self, arrs, out_shapes, n_copies, make):
        self.arrs, self.out_shapes, self.n_copies, self.make = list(arrs), list(out_shapes), n_copies, make


def _pair_side(arrs, swaps):
    return _Side(arrs, [jax.ShapeDtypeStruct((4,) + a.shape[1:], a.dtype) for a in arrs], 4 * len(arrs),
                 functools.partial(_pair_copies, swaps=swaps))


def _chip_side(arrs):
    return _Side(arrs, [jax.ShapeDtypeStruct((3,) + a.shape[1:], a.dtype) for a in arrs], 3 * len(arrs), _chip_copies)


def _row_call(body, side, *, name, steps, in_specs, out_specs, out_shape, ins):
    if side is None:
        res = pl.pallas_call(body, name=name, grid=(steps,), in_specs=in_specs, out_specs=out_specs, out_shape=out_shape,
                             compiler_params=_params("arbitrary"))(*ins)
        return list(res), []
    n_in, n_out, ne = len(in_specs), len(out_specs), len(side.arrs)

    def wrapped(*refs):
        e_in = refs[n_in:n_in + ne]
        e_out = refs[n_in + ne + n_out:n_in + 2 * ne + n_out]
        sems = refs[n_in + 2 * ne + n_out:]
        i = pl.program_id(0)

        @pl.when(i == 0)
        def _():
            for cp in side.make(e_in, e_out, *sems):
                cp.start()

        body(*refs[:n_in], *refs[n_in + ne:n_in + ne + n_out])

        @pl.when(i == steps - 1)
        def _():
            cps = side.make(e_in, e_out, *sems)
            for cp in cps:
                cp.wait_recv()
            for cp in cps:
                cp.wait_send()

    any_spec = pl.BlockSpec(memory_space=pl.ANY)
    res = pl.pallas_call(
        wrapped, name=name, grid=(steps,), in_specs=list(in_specs) + [any_spec] * ne,
        out_specs=list(out_specs) + [any_spec] * ne, out_shape=list(out_shape) + side.out_shapes,
        scratch_shapes=[pltpu.SemaphoreType.DMA((side.n_copies,)), pltpu.SemaphoreType.DMA((side.n_copies,))],
        compiler_params=_params("arbitrary"))(*ins, *side.arrs)
    return list(res[:n_out]), list(res[n_out:])


def _pair_sum(grads, got, swap, core, name):
    _, r, c = got.shape
    tr = r if r <= 352 else r // 2

    def own_map(k, i, core_ref):
        return (_slot(swap, k // 2, k % 2, core_ref[0]), i, 0)

    def body(core_ref, a_ref, b_ref, o_ref):
        o_ref[...] = a_ref[...] + b_ref[...]

    spec = pl.BlockSpec((None, tr, c), lambda k, i, core_ref: (k, i, 0))
    return pl.pallas_call(
        body, name=name,
        grid_spec=pltpu.PrefetchScalarGridSpec(
            num_scalar_prefetch=1, grid=(4, r // tr),
            in_specs=[pl.BlockSpec((None, tr, c), own_map), spec], out_specs=spec),
        out_shape=jax.ShapeDtypeStruct(got.shape, got.dtype), compiler_params=_params("parallel", "parallel"),
    )(core, grads, got)


def _adamw_math(w, g, m, v):
    m = ADAM_B1 * m + (1.0 - ADAM_B1) * g
    v = ADAM_B2 * v + (1.0 - ADAM_B2) * (g * g)
    m_hat = m / (1.0 - ADAM_B1 ** ADAM_STEP)
    v_hat = v / (1.0 - ADAM_B2 ** ADAM_STEP)
    delta = -ADAM_LR * (m_hat / (jnp.sqrt(v_hat) + ADAM_EPS) + ADAM_WD * w)
    return delta, m, v


def _adamw_tile(r):
    for cand in (256, 352, 128):
        if r % cand == 0:
            return cand
    return r


def _adamw(w, m, v, g, name):
    r, c = w.shape
    tr = _adamw_tile(r)
    spec = pl.BlockSpec((tr, c), lambda i: (i, 0))

    def body(w_ref, m_ref, v_ref, g_ref, d_ref, nm_ref, nv_ref):
        d_ref[...], nm_ref[...], nv_ref[...] = _adamw_math(w_ref[...], g_ref[...], m_ref[...], v_ref[...])

    out = jax.ShapeDtypeStruct((r, c), F32)
    return pl.pallas_call(
        body, name=name, grid=(r // tr,), in_specs=[spec] * 4, out_specs=[spec] * 3, out_shape=[out] * 3,
        compiler_params=_params("parallel"),
    )(w, m, v, g)


def _adamw_reduce(w, m, v, sums, recv, chip, name):
    r, c = w.shape
    tr = _adamw_tile(r)
    spec = pl.BlockSpec((tr, c), lambda i, chip_ref: (i, 0))

    def body(chip_ref, w_ref, m_ref, v_ref, s_ref, p_ref, g_ref, d_ref, nm_ref, nv_ref):
        g = ((s_ref[...] + p_ref[0]) + p_ref[1]) + p_ref[2]
        g_ref[...] = g
        d_ref[...], nm_ref[...], nv_ref[...] = _adamw_math(w_ref[...], g, m_ref[...], v_ref[...])

    out = jax.ShapeDtypeStruct((r, c), F32)
    return pl.pallas_call(
        body, name=name,
        grid_spec=pltpu.PrefetchScalarGridSpec(
            num_scalar_prefetch=1, grid=(r // tr,),
            in_specs=[spec, spec, spec, pl.BlockSpec((None, tr, c), lambda i, chip_ref: (chip_ref[0], i, 0)),
                      pl.BlockSpec((3, tr, c), lambda i, chip_ref: (0, i, 0))],
            out_specs=[spec] * 4),
        out_shape=[out] * 4, compiler_params=_params("parallel"),
    )(chip, w, m, v, sums, recv)


def _vec(n):
    return pl.BlockSpec((1, n), lambda *_: (0, 0))


def _ln_mod(x, g, scale, shift, *, ts, name):
    s = x.shape[0]
    row = pl.BlockSpec((ts, D), lambda i: (i, 0))

    def body(x_ref, g_ref, sc_ref, sh_ref, h_ref):
        xv = x_ref[...]
        r = lax.rsqrt(jnp.mean(xv * xv, axis=-1, keepdims=True) + EPS)
        h = (xv * r) * g_ref[...]
        h_ref[...] = (h * (1.0 + sc_ref[...]) + sh_ref[...]).astype(BF16)

    return pl.pallas_call(
        body, name=name, grid=(s // ts,), in_specs=[row, _vec(D), _vec(D), _vec(D)], out_specs=row,
        out_shape=jax.ShapeDtypeStruct((s, D), BF16), compiler_params=_params("parallel"),
    )(x, g, scale, shift)


def _group_rsqrt(t, bd):
    return lax.rsqrt(_split_dot(t * t, bd) * (1.0 / HD) + EPS)


def _qk_norm(proj, qg, kg, bd, *, ts, name):
    s = proj.shape[0]

    def body(q_ref, k_ref, v_ref, qg_ref, kg_ref, bd_ref, o_ref):
        bdv = bd_ref[...]
        q, k = q_ref[...], k_ref[...]
        o_ref[:, 0:DA] = (q * _group_rsqrt(q, bdv) * qg_ref[...]).astype(BF16)
        o_ref[:, DA:2 * DA] = (k * _group_rsqrt(k, bdv) * kg_ref[...]).astype(BF16)
        o_ref[:, 2 * DA:] = v_ref[...].astype(BF16)

    col = lambda j: pl.BlockSpec((ts, DA), lambda i: (i, j))
    return pl.pallas_call(
        body, name=name, grid=(s // ts,),
        in_specs=[col(1), col(2), col(3), _vec(DA), _vec(DA), _full((DA, DA))],
        out_specs=pl.BlockSpec((ts, 3 * DA), lambda i: (i, 0)),
        out_shape=jax.ShapeDtypeStruct((s, 3 * DA), BF16), compiler_params=_params("parallel"),
    )(proj, proj, proj, qg, kg, bd)


EXP_UNDERFLOW = -120.0


def _log_terms(z):
    neg_abs = lax.bitcast_convert_type(lax.bitcast_convert_type(z, jnp.uint32) | jnp.uint32(0x80000000), F32)
    b = jnp.minimum(z, 0.0) - jnp.log(1.0 + jnp.exp(neg_abs))
    return b, b - z


def _head_masks(rows):
    lane = lax.broadcasted_iota(jnp.int32, (rows, LANES), 1)
    return [lane < HD, lane >= HD]


def _attn_fwd(qkv, gather, swaps, *, tq, tk, hp, name):
    s = qkv.shape[0]
    nrep = tk // LANES
    ndiag = tq // tk
    ng = len(gather)
    nh, wl = 2 * hp, LANES * hp
    ngrp, nq = DA // wl, s // tq
    lanes = [slice(LANES * pp, LANES * (pp + 1)) for pp in range(hp)]

    def body(*refs):
        q_ref, k_ref, v_ref = refs[:3]
        g_in = refs[3:3 + ng]
        o_ref, tot_ref, first_ref = refs[3 + ng:6 + ng]
        g_out = refs[6 + ng:6 + 2 * ng]
        oacc, rc = refs[6 + 2 * ng:8 + 2 * ng]
        g_sems = refs[8 + 2 * ng:]
        i = pl.program_id(1)
        step_id = pl.program_id(0) * nq + i

        @pl.when(step_id == 0)
        def _():
            _Gather(g_in, g_out, *g_sems, swaps).start()

        @pl.when(step_id == (ngrp * nq * 3) // 4)
        def _():
            _Gather(g_in, g_out, *g_sems, swaps).forward()

        heads = _head_masks(tq)
        qs = [jnp.where(heads[a % 2], q_ref[:, lanes[a // 2]] * 0.125, 0.0).astype(BF16) for a in range(nh)]
        dif = lax.broadcasted_iota(jnp.int32, (tq, tk), 0) - lax.broadcasted_iota(jnp.int32, (tq, tk), 1)
        kr = lax.broadcasted_iota(jnp.int32, (tk, tk), 0)
        kc = lax.broadcasted_iota(jnp.int32, (tk, tk), 1)
        later =jnp.where(kr > kc, 1.0, 0.0).astype(BF16)
        oacc[...] = jnp.zeros_like(oacc)
        rc[...] = jnp.zeros_like(rc)

        def tile(kb, thr):
            rows = pl.ds(pl.multiple_of(kb * tk, tk), tk)
            ks = [k_ref[rows, ln] for ln in lanes]
            vs = [v_ref[rows, ln] for ln in lanes]
            qr = slice(0 if thr is None else thr, tq)
            rcv = [rc[a, qr, :] for a in range(nh)]
            zs = [_dot(qs[a][qr], ks[a // 2], NT) for a in range(nh)]
            bs, mbs = [], []
            for a in range(nh):
                b, m = _log_terms(zs[a])
                if thr is not None:
                    m = jnp.where(dif[qr] > thr, m, 0.0)
                bs.append(b)
                mbs.append(m.astype(BF16))
            rl = [_dot(mbs[a], later) for a in range(nh)]
            for a in range(nh):
                p = jnp.exp(bs[a] + (rl[a] + jnp.tile(rcv[a], (1, nrep))))
                if thr is not None:
                    p = jnp.where(dif[qr] > thr, p, 0.0)
                oacc[a, qr, :] += _dot(p.astype(BF16), vs[a // 2])
                rc[a, qr, :] = rcv[a] + (rl[a][:, 0:1] + mbs[a][:, 0:1].astype(F32))

        for d in reversed(range(ndiag)):
            tile(i * ndiag + d, d * tk)

        def live():
            top = rc[0]
            for a in range(1, nh):
                top = jnp.maximum(top, rc[a])
            return jnp.max(top) > EXP_UNDERFLOW

        def step(carry):
            kb, _ = carry
            tile(kb, None)
            return kb - 1, live()

        kb_end, _ = lax.while_loop(lambda cr: jnp.logical_and(cr[0] >= 0, cr[1]), step, (i * ndiag - 1, live()))
        first_ref[pl.program_id(0), i] = (kb_end + 1).astype(F32)
        for pp, ln in enumerate(lanes):
            o_ref[:, ln] = jnp.where(heads[0], oacc[2 * pp], oacc[2 * pp + 1])
            tot_ref[:, ln] = jnp.where(heads[0], rc[2 * pp], rc[2 * pp + 1])

        @pl.when(step_id == ngrp * nq - 1)
        def _():
            _Gather(g_in, g_out, *g_sems, swaps).finish()

    qspec = pl.BlockSpec((tq, wl), lambda p, i: (i, p))
    any_spec = pl.BlockSpec(memory_space=pl.ANY)
    res = pl.pallas_call(
        body, name=name, grid=(ngrp, nq),
        in_specs=[qspec,
                  pl.BlockSpec((s, wl), lambda p, i: (0, ngrp + p)),
                  pl.BlockSpec((s, wl), lambda p, i: (0, 2 * ngrp + p))] + [any_spec] * ng,
        out_specs=[qspec, qspec, pl.BlockSpec(memory_space=pltpu.SMEM)] + [any_spec] * ng,
        out_shape=[jax.ShapeDtypeStruct((s, DA), F32), jax.ShapeDtypeStruct((s, DA), F32),
                   jax.ShapeDtypeStruct((ngrp, nq), F32)]
        + [jax.ShapeDtypeStruct((NDEV,) + a.shape, a.dtype) for a in gather],
        scratch_shapes=[pltpu.VMEM((nh, tq, LANES), F32), pltpu.VMEM((nh, tq, LANES), F32)] + _Gather.scratch(ng),
        compiler_params=_params("arbitrary", "arbitrary"),
    )(qkv, qkv, qkv, *gather)
    return res[0], res[1], res[2], res[3:]


def _attn_bwd(qkv, do, tot, first, exchange, *, tq, tk, hp, name):
    s = qkv.shape[0]
    nrep = tk // LANES
    ndiag = tq // tk
    ne = len(exchange)
    nh, wl = 2 * hp, LANES * hp
    ngrp, nq = DA // wl, s // tq
    lanes = [slice(LANES * pp, LANES * (pp + 1)) for pp in range(hp)]

    def body(*refs):
        q_ref, k_ref, v_ref, do_ref, tot_ref, first_ref = refs[:6]
        e_in = refs[6:6 + ne]
        dq_ref, dk_ref, dv_ref = refs[6 + ne:9 + ne]
        e_out = refs[9 + ne:9 + 2 * ne]
        dqacc, rem, gc = refs[9 + 2 * ne:12 + 2 * ne]
        e_sems = refs[12 + 2 * ne:]
        i = pl.program_id(1)
        step_id = pl.program_id(0) * nq + i

        @pl.when(step_id == 0)
        def _():
            for cp in _chip_copies(e_in, e_out, *e_sems):
                cp.start()

        @pl.when(i == 0)
        def _():
            dk_ref[...] = jnp.zeros_like(dk_ref)
            dv_ref[...] = jnp.zeros_like(dv_ref)

        heads = _head_masks(tq)
        qs = [jnp.where(heads[a % 2], q_ref[:, lanes[a // 2]] * 0.125, 0.0).astype(BF16) for a in range(nh)]
        dob = [jnp.where(heads[a % 2], do_ref[:, lanes[a // 2]], 0.0).astype(BF16) for a in range(nh)]
        dif = lax.broadcasted_iota(jnp.int32, (tq, tk), 0) - lax.broadcasted_iota(jnp.int32, (tq, tk), 1)
        kr = lax.broadcasted_iota(jnp.int32, (tk, tk), 0)
        kc = lax.broadcasted_iota(jnp.int32, (tk, tk), 1)
        up_incl = jnp.where(kr <= kc, 1.0, 0.0).astype(BF16)
        up_strict = jnp.where(kr < kc, 1.0, 0.0).astype(BF16)
        dqacc[...] = jnp.zeros_like(dqacc)
        gc[...] = jnp.zeros_like(gc)
        for pp, ln in enumerate(lanes):
            totv = tot_ref[:, ln]
            swapped = pltpu.roll(totv, HD, axis=1)
            rem[2 * pp] = jnp.where(heads[0], totv, swapped)
            rem[2 * pp + 1] = jnp.where(heads[1], totv, swapped)

        def tile(kb, thr):
            rows = pl.ds(pl.multiple_of(kb * tk, tk), tk)
            ks = [k_ref[rows, ln] for ln in lanes]
            vs = [v_ref[rows, ln] for ln in lanes]
            qr = slice(0 if thr is None else thr, tq)
            remv = [rem[a, qr, :] for a in range(nh)]
            gcv = [gc[a, qr, :] for a in range(nh)]
            zs = [_dot(qs[a][qr], ks[a // 2], NT) for a in range(nh)]
            das = [_dot(dob[a][qr], vs[a // 2], NT) for a in range(nh)]
            bs, mbs = [], []
            for a in range(nh):
                b, m = _log_terms(zs[a])
                if thr is not None:
                    m = jnp.where(dif[qr] > thr, m, 0.0)
                bs.append(b)
                mbs.append(m.astype(BF16))
            pl_ = [_dot(mbs[a], up_incl) for a in range(nh)]
            ps, gs, gbs = [], [], []
            for a in range(nh):
                p = jnp.exp(bs[a] + (jnp.tile(remv[a], (1, nrep)) - pl_[a]))
                if thr is not None:
                    p = jnp.where(dif[qr] > thr, p, 0.0)
                g = p * das[a]
                ps.append(p.astype(BF16))
                gs.append(g)
                gbs.append(g.astype(BF16))
            cl = [_dot(gbs[a], up_strict) for a in range(nh)]
            dk_add = [jnp.zeros((tk, LANES), F32) for _ in range(hp)]
            dv_add = [jnp.zeros((tk, LANES), F32) for _ in range(hp)]
            for a in range(nh):
                dz = gs[a] - jnp.exp(bs[a]) * (gs[a] + (jnp.tile(gcv[a], (1, nrep)) + cl[a]))
                if thr is not None:
                    dz = jnp.where(dif[qr] > thr, dz, 0.0)
                dzb = dz.astype(BF16)
                dqacc[a, qr, :] += _dot(dzb, ks[a // 2])
                dk_add[a // 2] += _dot(dzb, qs[a][qr], TN)
                dv_add[a // 2] += _dot(ps[a], dob[a][qr], TN)
                rem[a, qr, :] = remv[a] - pl_[a][:, tk - 1:tk]
                gc[a, qr, :] = gcv[a] + (cl[a][:, tk - 1:tk] + gbs[a][:, tk - 1:tk].astype(F32))
            for pp, ln in enumerate(lanes):
                dk_ref[rows, ln] += dk_add[pp]
                dv_ref[rows, ln] += dv_add[pp]

        def step(kb, carry):
            tile(kb, None)
            return carry

        lax.fori_loop(first_ref[pl.program_id(0), i].astype(jnp.int32), i * ndiag, step, 0)
        for d in range(ndiag):
            tile(i * ndiag + d, d * tk)
        for pp, ln in enumerate(lanes):
            dq_ref[:, ln] = jnp.where(heads[0], dqacc[2 * pp], dqacc[2 * pp + 1]) * 0.125

        @pl.when(step_id == ngrp * nq - 1)
        def _():
            cps = _chip_copies(e_in, e_out, *e_sems)
            for cp in cps:
                cp.wait_recv()
            for cp in cps:
                cp.wait_send()

    qspec = pl.BlockSpec((tq, wl), lambda p, i: (i, p))
    full = pl.BlockSpec((s, wl), lambda p, i: (0, p))
    any_spec = pl.BlockSpec(memory_space=pl.ANY)
    out = jax.ShapeDtypeStruct((s, DA), F32)
    res = pl.pallas_call(
        body, name=name, grid=(ngrp, nq),
        in_specs=[qspec, pl.BlockSpec((s, wl), lambda p, i: (0, ngrp + p), pipeline_mode=pl.Buffered(1)),
                  pl.BlockSpec((s, wl), lambda p, i: (0, 2 * ngrp + p), pipeline_mode=pl.Buffered(1)), qspec, qspec,
                  pl.BlockSpec(memory_space=pltpu.SMEM)] + [any_spec] * ne,
        out_specs=[qspec, full, full] + [any_spec] * ne,
        out_shape=[out, out, out] + [jax.ShapeDtypeStruct((3,) + a.shape[1:], a.dtype) for a in exchange],
        scratch_shapes=[pltpu.VMEM((nh, tq, LANES), F32)] * 3
        + [pltpu.SemaphoreType.DMA((3 * ne,)), pltpu.SemaphoreType.DMA((3 * ne,))],
        compiler_params=_params("arbitrary", "arbitrary"),
    )(qkv, qkv, qkv, do, tot, first, *exchange)
    return res[0], res[1], res[2], res[3:]


def _shift_rows(v, k):
    return pltpu.roll(v, k % v.shape[0], axis=0)


def _pooled(u, uh, i, g, w, ts):
    halo = jnp.where(i > 0, uh, 0.0)
    ue = jnp.concatenate([halo, u], axis=0)
    acc, span = ue, 1
    while span < w:
        acc = acc + _shift_rows(acc, span)
        span *= 2
    tpos = i * ts + lax.broadcasted_iota(jnp.int32, (ts, 1), 0)
    cnt = jnp.minimum(tpos + 1, w).astype(F32)
    return acc[HALO:] / cnt - u


def _pool_mix(proj, o, pw, pb, ps, ag, bd, *, ts, name):
    s = proj.shape[0]
    hb = ts // HALO

    def body(u_ref, uh_ref, o_ref, pw_ref, pb_ref, ps_ref, ag_ref, bd_ref, mix_ref):
        i = pl.program_id(0)
        for g, w in enumerate(POOL_WINDOWS):
            cols = slice(g * LANES, (g + 1) * LANES)
            pooled = _pooled(u_ref[:, cols], uh_ref[:, cols], i, g, w, ts)
            yv = (_dot(pooled.astype(BF16), pw_ref[g]) + pb_ref[:, cols]) * ps_ref[:, cols]
            mix_ref[:, cols] = yv.astype(BF16)
        ov = o_ref[...]
        mix_ref[:, DP:] = (ov * _group_rsqrt(ov, bd_ref[...]) * ag_ref[...]).astype(BF16)

    return pl.pallas_call(
        body, name=name, grid=(s // ts,),
        in_specs=[pl.BlockSpec((ts, DP), lambda i: (i, 0)),
                  pl.BlockSpec((HALO, DP), lambda i: (jnp.maximum(i * hb - 1, 0), 0)),
                  pl.BlockSpec((ts, DA), lambda i: (i, 0)),
                  _full((4, LANES, LANES)), _vec(DP), _vec(DP), _vec(DA), _full((DA, DA))],
        out_specs=pl.BlockSpec((ts, D), lambda i: (i, 0)),
        out_shape=jax.ShapeDtypeStruct((s, D), BF16), compiler_params=_params("parallel"),
    )(proj, proj, o, pw, pb, ps, ag, bd)


def _res_ln_mod(x, att, gate, g, scale, shift, *, ts, name):
    s = x.shape[0]
    row = pl.BlockSpec((ts, D), lambda i: (i, 0))

    def body(x_ref, a_ref, gt_ref, g_ref, sc_ref, sh_ref, x1_ref, h_ref):
        x1 = x_ref[...] + gt_ref[...] * a_ref[...]
        x1_ref[...] = x1
        r = lax.rsqrt(jnp.mean(x1 * x1, axis=-1, keepdims=True) + EPS)
        h = (x1 * r) * g_ref[...]
        h_ref[...] = (h * (1.0 + sc_ref[...]) + sh_ref[...]).astype(BF16)

    return pl.pallas_call(
        body, name=name, grid=(s // ts,), in_specs=[row, row] + [_vec(D)] * 4, out_specs=[row, row],
        out_shape=[jax.ShapeDtypeStruct((s, D), F32), jax.ShapeDtypeStruct((s, D), BF16)],
        compiler_params=_params("parallel"),
    )(x, att, gate, g, scale, shift)


CF = DFF // 2


def _conv(u, uh, w_ref, b_ref, i):
    halo = jnp.where(i > 0, uh.astype(F32), 0.0)
    ue = jnp.concatenate([halo, u.astype(F32)], axis=0)
    y = w_ref[2:3, :] * ue + w_ref[1:2, :] * _shift_rows(ue, 1) + w_ref[0:1, :] * _shift_rows(ue, 2)
    return y[HALO:] + b_ref[...]


def _conv_gate(up, cw, cb, *, ts, name):
    s = up.shape[0]
    hb = ts // HALO

    def body(u_ref, uh_ref, w_ref, b_ref, a_ref, c_ref):
        i = pl.program_id(0)
        c = _conv(u_ref[...], uh_ref[...], w_ref, b_ref, i)
        gt, vl = c[:, :CF], c[:, CF:]
        a_ref[...] = (gt / (1.0 + jnp.exp(-gt)) * vl).astype(BF16)
        c_ref[...] = c.astype(BF16)

    return pl.pallas_call(
        body, name=name, grid=(s // ts, 2),
        in_specs=[pl.BlockSpec((ts, 2 * CF), lambda i, j: (i, j)),
                  pl.BlockSpec((HALO, 2 * CF), lambda i, j: (jnp.maximum(i * hb - 1, 0), j)),
                  pl.BlockSpec((3, 2 * CF), lambda i, j: (0, j)), pl.BlockSpec((1, 2 * CF), lambda i, j: (0, j))],
        out_specs=[pl.BlockSpec((ts, CF), lambda i, j: (i, j)), pl.BlockSpec((ts, 2 * CF), lambda i, j: (i, j))],
        out_shape=[jax.ShapeDtypeStruct((s, DFF), BF16), jax.ShapeDtypeStruct((s, 2 * DFF), BF16)],
        compiler_params=_params("parallel", "parallel"),
    )(up, up, cw, cb)


def _loss_head(x1, ffn, tgt, gate2, *, ts, name):
    s = x1.shape[0]
    n = s // ts
    row = pl.BlockSpec((ts, D), lambda i: (i, 0))
    acc8 = pl.BlockSpec((SUBLANES, D), lambda i: (0, 0))

    def body(x_ref, f_ref, t_ref, g_ref, dy_ref, df_ref, dg_ref, loss_ref, lacc):
        i = pl.program_id(0)

        @pl.when(i == 0)
        def _():
            lacc[...] = jnp.zeros_like(lacc)
            dg_ref[...] = jnp.zeros_like(dg_ref)

        f = f_ref[...]
        diff = x_ref[...] + g_ref[...] * f - t_ref[...]
        lacc[...] += _colsum8(diff * diff)
        dy = diff * (1.0 / D)
        dy_ref[...] = dy
        df_ref[...] = (dy * g_ref[...]).astype(BF16)
        dg_ref[...] += _colsum8(dy * f)

        @pl.when(i == n - 1)
        def _():
            loss_ref[...] = jnp.full((SUBLANES, LANES), (0.5 / D) * jnp.sum(lacc[...]), F32)

    return pl.pallas_call(
        body, name=name, grid=(n,), in_specs=[row, row, row, _vec(D)],
        out_specs=[row, row, acc8, _full((SUBLANES, LANES))],
        out_shape=[jax.ShapeDtypeStruct((s, D), F32), jax.ShapeDtypeStruct((s, D), BF16),
                   jax.ShapeDtypeStruct((SUBLANES, D), F32), jax.ShapeDtypeStruct((SUBLANES, LANES), F32)],
        scratch_shapes=[pltpu.VMEM((SUBLANES, D), F32)], compiler_params=_params("arbitrary"),
    )(x1, ffn, tgt, gate2)


def _gate_bwd(da, conv, *, ts, name):
    s = conv.shape[0]

    def body(da_ref, c_ref, d_ref, db_ref):
        i = pl.program_id(1)

        @pl.when(i == 0)
        def _():
            db_ref[...] = jnp.zeros_like(db_ref)

        gt, vl = c_ref[:, :CF].astype(F32), c_ref[:, CF:].astype(F32)
        sg = 1.0 / (1.0 + jnp.exp(-gt))
        dav = da_ref[...].astype(F32)
        dgt = dav * vl * (sg * (1.0 + gt * (1.0 - sg)))
        dvl = dav * (gt * sg)
        d_ref[:, :CF] = dgt.astype(BF16)
        d_ref[:, CF:] = dvl.astype(BF16)
        db_ref[:, :CF] += _colsum8(dgt)
        db_ref[:, CF:] += _colsum8(dvl)

    return pl.pallas_call(
        body, name=name, grid=(2, s // ts),
        in_specs=[pl.BlockSpec((ts, CF), lambda j, i: (i, j)),
                  pl.BlockSpec((ts, 2 * CF), lambda j, i: (i, j))],
        out_specs=[pl.BlockSpec((ts, 2 * CF), lambda j, i: (i, j)),
                   pl.BlockSpec((SUBLANES, 2 * CF), lambda j, i: (0, j))],
        out_shape=[jax.ShapeDtypeStruct((s, 2 * DFF), BF16), jax.ShapeDtypeStruct((SUBLANES, 2 * DFF), F32)],
        compiler_params=_params("parallel", "arbitrary"),
    )(da, conv)


def _conv_bwd(dc, up, cw, *, ts, tc, name):
    s = up.shape[0]
    hb = ts // HALO
    nb = s // HALO

    def body(d_ref, dn_ref, u_ref, w_ref, du_ref, dw_ref):
        i = pl.program_id(1)
        n = s // ts

        @pl.when(i == 0)
        def _():
            dw_ref[...] = jnp.zeros_like(dw_ref)

        dcur = d_ref[...].astype(F32)
        nxt = jnp.where(i < n - 1, dn_ref[...].astype(F32), 0.0)
        de = jnp.concatenate([dcur, nxt], axis=0)
        d1 = _shift_rows(de, -1)[:ts]
        d2 = _shift_rows(de, -2)[:ts]
        du_ref[...] = (w_ref[2:3, :] * dcur + w_ref[1:2, :] * d1 + w_ref[0:1, :] * d2).astype(BF16)
        u = u_ref[...].astype(F32)
        dw_ref[16:24, :] += _colsum8(dcur * u)
        dw_ref[8:16, :] += _colsum8(d1 * u)
        dw_ref[0:8, :] += _colsum8(d2 * u)

    return pl.pallas_call(
        body, name=name, grid=(2 * DFF // tc, s // ts),
        in_specs=[pl.BlockSpec((ts, tc), lambda j, i: (i, j)),
                  pl.BlockSpec((HALO, tc), lambda j, i: (jnp.minimum((i + 1) * hb, nb - 1), j)),
                  pl.BlockSpec((ts, tc), lambda j, i: (i, j)),
                  pl.BlockSpec((3, tc), lambda j, i: (0, j))],
        out_specs=[pl.BlockSpec((ts, tc), lambda j, i: (i, j)), pl.BlockSpec((24, tc), lambda j, i: (0, j))],
        out_shape=[jax.ShapeDtypeStruct((s, 2 * DFF), BF16), jax.ShapeDtypeStruct((24, 2 * DFF), F32)],
        compiler_params=_params("parallel", "arbitrary"),
    )(dc, dc, up, cw)


MXU_COLS = 256


def _sub_chunks(width):
    return [(c0, min(MXU_COLS, width - c0)) for c0 in range(0, width, MXU_COLS)]


def _up_conv_gate(h2, w_up, cw, cb, *, tm, name):
    s = h2.shape[0]
    hb = tm // HALO

    def body(a_ref, ah_ref, w_ref, cw_ref, cb_ref, up_ref, c_ref, act_ref):
        i = pl.program_id(1)
        ext = jnp.concatenate([ah_ref[...], a_ref[...]], axis=0)
        live_halo = i > 0
        for c0, cwid in _sub_chunks(CF):
            conv = []
            for off in (c0, CF + c0):
                cols = slice(off, off + cwid)
                u = _dot(ext, w_ref[cols, :], NT)
                up_ref[:, cols] = u[HALO:].astype(BF16)
                row = lax.broadcasted_iota(jnp.int32, (HALO + tm, 1), 0)
                ue = jnp.where(jnp.logical_or(row >= HALO, live_halo), u, 0.0)
                y = cw_ref[2:3, cols] * ue + cw_ref[1:2, cols] * _shift_rows(ue, 1) + cw_ref[0:1, cols] * _shift_rows(ue, 2)
                cv = y[HALO:] + cb_ref[:, cols]
                c_ref[:, cols] = cv.astype(BF16)
                conv.append(cv)
            gt, vl = conv
            act_ref[:, c0:c0 + cwid] = (gt / (1.0 + jnp.exp(-gt)) * vl).astype(BF16)

    return pl.pallas_call(
        body, name=name, grid=(2, s // tm),
        in_specs=[pl.BlockSpec((tm, D), lambda j, i: (i, 0)),
                  pl.BlockSpec((HALO, D), lambda j, i: (jnp.maximum(i * hb - 1, 0), 0)),
                  pl.BlockSpec((2 * CF, D), lambda j, i: (j, 0)),
                  pl.BlockSpec((3, 2 * CF), lambda j, i: (0, j)), pl.BlockSpec((1, 2 * CF), lambda j, i: (0, j))],
        out_specs=[pl.BlockSpec((tm, 2 * CF), lambda j, i: (i, j)), pl.BlockSpec((tm, 2 * CF), lambda j, i: (i, j)),
                   pl.BlockSpec((tm, CF), lambda j, i: (i, j))],
        out_shape=[jax.ShapeDtypeStruct((s, 2 * DFF), BF16), jax.ShapeDtypeStruct((s, 2 * DFF), BF16),
                   jax.ShapeDtypeStruct((s, DFF), BF16)],
        compiler_params=_params("parallel", "parallel"),
    )(h2, h2, w_up, cw, cb)


def _down_bwd_gate(dffn, w_down, conv, *, tm, name):
    s = dffn.shape[0]

    def body(a_ref, w_ref, c_ref, d_ref, db_ref):
        i = pl.program_id(1)

        @pl.when(i == 0)
        def _():
            db_ref[...] = jnp.zeros_like(db_ref)

        a = a_ref[...]
        for c0, cwid in _sub_chunks(CF):
            gcols, vcols = slice(c0, c0 + cwid), slice(CF + c0, CF + c0 + cwid)
            da = _dot(a, w_ref[gcols, :], NT)
            gt, vl = c_ref[:, gcols].astype(F32), c_ref[:, vcols].astype(F32)
            sg = 1.0 / (1.0 + jnp.exp(-gt))
            dgt = da * vl * (sg * (1.0 + gt * (1.0 - sg)))
            dvl = da * (gt * sg)
            d_ref[:, gcols] = dgt.astype(BF16)
            d_ref[:, vcols] = dvl.astype(BF16)
            db_ref[:, gcols] += _colsum8(dgt)
            db_ref[:, vcols] += _colsum8(dvl)

    return pl.pallas_call(
        body, name=name, grid=(2, s // tm),
        in_specs=[pl.BlockSpec((tm, D), lambda j, i: (i, 0)), pl.BlockSpec((CF, D), lambda j, i: (j, 0)),
                  pl.BlockSpec((tm, 2 * CF), lambda j, i: (i, j))],
        out_specs=[pl.BlockSpec((tm, 2 * CF), lambda j, i: (i, j)), pl.BlockSpec((SUBLANES, 2 * CF), lambda j, i: (0, j))],
        out_shape=[jax.ShapeDtypeStruct((s, 2 * DFF), BF16), jax.ShapeDtypeStruct((SUBLANES, 2 * DFF), F32)],
        compiler_params=_params("parallel", "arbitrary"),
    )(dffn, w_down, conv)


def _conv_bwd_up_bwd(dc, up, cw, w_up, *, tm, name):
    s = up.shape[0]
    hb = tm // HALO
    nb = s // HALO
    nk = 2 * DFF // CF
    n = s // tm

    def body(d_ref, dn_ref, u_ref, cw_ref, w_ref, du_ref, dh_ref, dw_ref, acc, dwacc):
        i, k = pl.program_id(0), pl.program_id(1)

        @pl.when(jnp.logical_and(i == 0, k == 0))
        def _():
            dwacc[...] = jnp.zeros_like(dwacc)

        @pl.when(k == 0)
        def _():
            acc[...] = jnp.zeros_like(acc)

        live_next = i < n - 1
        part = None
        for c0, cwid in _sub_chunks(CF):
            cols = slice(c0, c0 + cwid)
            dcur = d_ref[:, cols].astype(F32)
            de = jnp.concatenate([dcur, jnp.where(live_next, dn_ref[:, cols].astype(F32), 0.0)], axis=0)
            d1 = _shift_rows(de, -1)[:tm]
            d2 = _shift_rows(de, -2)[:tm]
            du = (cw_ref[2:3, cols] * dcur + cw_ref[1:2, cols] * d1 + cw_ref[0:1, cols] * d2).astype(BF16)
            du_ref[:, cols] = du
            prod = _dot(du, w_ref[cols, :])
            part = prod if part is None else part + prod
            u = u_ref[:, cols].astype(F32)
            for tap, dsh in ((2, dcur), (1, d1), (0, d2)):
                dwacc[k, SUBLANES * tap:SUBLANES * (tap + 1), cols] += _colsum8(dsh * u)
        acc[...] += part

        @pl.when(k == nk - 1)
        def _():
            dh_ref[...] = acc[...].astype(dh_ref.dtype)

        @pl.when(jnp.logical_and(i == n - 1, k == nk - 1))
        def _():
            dw_ref[...] = dwacc[...]

    res = pl.pallas_call(
        body, name=name, grid=(n, nk),
        in_specs=[pl.BlockSpec((tm, CF), lambda i, k: (i, k)),
                  pl.BlockSpec((HALO, CF), lambda i, k: (jnp.minimum((i + 1) * hb, nb - 1), k)),
                  pl.BlockSpec((tm, CF), lambda i, k: (i, k)),
                  pl.BlockSpec((3, CF), lambda i, k: (0, k)),
                  pl.BlockSpec((CF, D), lambda i, k: (k, 0))],
        out_specs=[pl.BlockSpec((tm, CF), lambda i, k: (i, k)), pl.BlockSpec((tm, D), lambda i, k: (i, 0)),
                   _full((nk, 24, CF))],
        out_shape=[jax.ShapeDtypeStruct((s, 2 * DFF), BF16), jax.ShapeDtypeStruct((s, D), BF16),
                   jax.ShapeDtypeStruct((nk, 24, CF), F32)],
        scratch_shapes=[pltpu.VMEM((tm, D), F32), pltpu.VMEM((nk, 24, CF), F32)],
        compiler_params=_params("arbitrary", "arbitrary"),
    )(dc, dc, up, cw, w_up)
    return res[0], res[1], jnp.transpose(res[2], (1, 0, 2)).reshape(24, 2 * DFF)


def _ln_mod_bwd(dh, xin, g, scale, resid, extra, gate, *, ts, name, side=None):
    s = xin.shape[0]
    row = pl.BlockSpec((ts, D), lambda i: (i, 0))
    acc8 = pl.BlockSpec((SUBLANES, D), lambda i: (0, 0))
    with_gate = extra is not None

    def body(*refs):
        if with_gate:
            dh_ref, x_ref, g_ref, sc_ref, r_ref, e_ref, gt_ref, dx_ref, da_ref, dsh, dsc, dg, dgt = refs
        else:
            dh_ref, x_ref, g_ref, sc_ref, r_ref, dx_ref, dsh, dsc, dg = refs
        i = pl.program_id(0)

        @pl.when(i == 0)
        def _():
            for acc in (dsh, dsc, dg) + ((dgt,) if with_gate else ()):
                acc[...] = jnp.zeros_like(acc)

        xv, dhv = x_ref[...], dh_ref[...].astype(F32)
        r = lax.rsqrt(jnp.mean(xv * xv, axis=-1, keepdims=True) + EPS)
        xn = xv * r
        dsh[...] += _colsum8(dhv)
        dsc[...] += _colsum8(dhv * (xn * g_ref[...]))
        dhp = dhv * (1.0 + sc_ref[...])
        dg[...] += _colsum8(dhp * xn)
        dxn = dhp * g_ref[...]
        dx = r_ref[...] + r * (dxn - xn * jnp.mean(dxn * xn, axis=-1, keepdims=True))
        dx_ref[...] = dx
        if with_gate:
            da_ref[...] = (dx * gt_ref[...]).astype(BF16)
            dgt[...] += _colsum8(dx * e_ref[...])

    f32o, p8 = jax.ShapeDtypeStruct((s, D), F32), jax.ShapeDtypeStruct((SUBLANES, D), F32)
    if with_gate:
        ins, in_specs = (dh, xin, g, scale, resid, extra, gate), [row, row, _vec(D), _vec(D), row, row, _vec(D)]
        out_specs, out_shape = [row, row, acc8, acc8, acc8, acc8], [f32o, jax.ShapeDtypeStruct((s, D), BF16), p8, p8, p8, p8]
    else:
        ins, in_specs = (dh, xin, g, scale, resid), [row, row, _vec(D), _vec(D), row]
        out_specs, out_shape = [row, acc8, acc8, acc8], [f32o, p8, p8, p8]
    return _row_call(body, side, name=name, steps=s // ts, in_specs=in_specs, out_specs=out_specs,
                     out_shape=out_shape, ins=ins)


def _group_norm_bwd(t, dn_out, gvec, bd):
    r = _group_rsqrt(t, bd)
    dg_terms = dn_out * t * r
    dn = dn_out * gvec
    dt = r * (dn - t * (r * r) * (_split_dot(dn * t, bd) * (1.0 / HD)))
    return dt, dg_terms


def _mix_bwd(dmix, proj, o, pw, pb, ps, ag, bd, *, ts, name, side=None):
    s = proj.shape[0]
    hb = ts // HALO
    nb = s // HALO

    def body(dm_ref, dmn_ref, u_ref, uh_ref, o_ref, pw_ref, pb_ref, ps_ref, ag_ref, bd_ref,
             du_ref, do_ref, dpw_ref, dpb_ref, dps_ref, dag_ref):
        i = pl.program_id(0)
        n = s // ts

        @pl.when(i == 0)
        def _():
            for acc in (dpw_ref, dpb_ref, dps_ref, dag_ref):
                acc[...] = jnp.zeros_like(acc)

        for g, w in enumerate(POOL_WINDOWS):
            cols = slice(g * LANES, (g + 1) * LANES)
            wg = pw_ref[g]
            psg = ps_ref[:, cols]
            pooled = _pooled(u_ref[:, cols], uh_ref[:, cols], i, g, w, ts).astype(BF16)
            dy = dm_ref[:, cols].astype(F32)
            dps_ref[:, cols] += _colsum8(dy * (_dot(pooled, wg) + pb_ref[:, cols]))
            dpre = dy * psg
            dpb_ref[:, cols] += _colsum8(dpre)
            dpreb = dpre.astype(BF16)
            dpw_ref[g * LANES:(g + 1) * LANES, :] += _dot(pooled, dpreb, TN)
            dpool = _dot(dpreb, wg, NT)
            dnext = _dot((dmn_ref[:, cols].astype(F32) * psg).astype(BF16), wg, NT)
            dpe = jnp.concatenate([dpool, jnp.where(i < n - 1, dnext, 0.0)], axis=0)
            tpos = i * ts + lax.broadcasted_iota(jnp.int32, (ts + HALO, 1), 0)
            acc = dpe / jnp.minimum(tpos + 1, w).astype(F32)
            span = 1
            while span < w:
                acc = acc + _shift_rows(acc, -span)
                span *= 2
            du_ref[:, cols] = acc[:ts] - dpool
        ov = o_ref[...]
        dov, dg_terms = _group_norm_bwd(ov, dm_ref[:, DP:].astype(F32), ag_ref[...], bd_ref[...])
        do_ref[...] = dov
        dag_ref[...] += _colsum8(dg_terms)

    p8 = jax.ShapeDtypeStruct((SUBLANES, DP), F32)
    acc8 = pl.BlockSpec((SUBLANES, DP), lambda i: (0, 0))
    half = pl.BlockSpec((ts, DP), lambda i: (i, 0))
    return _row_call(
        body, side, name=name, steps=s // ts,
        in_specs=[pl.BlockSpec((ts, D), lambda i: (i, 0)),
                  pl.BlockSpec((HALO, DP), lambda i: (jnp.minimum((i + 1) * hb, nb - 1), 0)),
                  half, pl.BlockSpec((HALO, DP), lambda i: (jnp.maximum(i * hb - 1, 0), 0)),
                  half, _full((4, LANES, LANES)), _vec(DP), _vec(DP), _vec(DA), _full((DA, DA))],
        out_specs=[half, half, _full((DP, LANES)), acc8, acc8, acc8],
        out_shape=[jax.ShapeDtypeStruct((s, DP), F32), jax.ShapeDtypeStruct((s, DA), F32),
                   jax.ShapeDtypeStruct((DP, LANES), F32), p8, p8, p8],
        ins=(dmix, dmix, proj, proj, o, pw, pb, ps, ag, bd))


def _qk_norm_bwd(du, dq, dk, dv, proj, qg, kg, bd, *, ts, name):
    s = proj.shape[0]

    def body(du_ref, dq_ref, dk_ref, dv_ref, q_ref, k_ref, qg_ref, kg_ref, bd_ref, dp_ref, dqg_ref, dkg_ref):
        i = pl.program_id(0)

        @pl.when(i == 0)
        def _():
            dqg_ref[...] = jnp.zeros_like(dqg_ref)
            dkg_ref[...] = jnp.zeros_like(dkg_ref)

        bdv = bd_ref[...]
        dqr, tq = _group_norm_bwd(q_ref[...], dq_ref[...], qg_ref[...], bdv)
        dkr, tk = _group_norm_bwd(k_ref[...], dk_ref[...], kg_ref[...], bdv)
        dqg_ref[...] += _colsum8(tq)
        dkg_ref[...] += _colsum8(tk)
        dp_ref[:, 0:DP] = du_ref[...].astype(BF16)
        dp_ref[:, DP:DP + DA] = dqr.astype(BF16)
        dp_ref[:, DP + DA:DP + 2 * DA] = dkr.astype(BF16)
        dp_ref[:, DP + 2 * DA:] = dv_ref[...].astype(BF16)

    half = pl.BlockSpec((ts, DA), lambda i: (i, 0))
    col = lambda j: pl.BlockSpec((ts, DA), lambda i: (i, j))
    acc8 = pl.BlockSpec((SUBLANES, DA), lambda i: (0, 0))
    p8 = jax.ShapeDtypeStruct((SUBLANES, DA), F32)
    return pl.pallas_call(
        body, name=name, grid=(s // ts,),
        in_specs=[half, half, half, half, col(1), col(2), _vec(DA), _vec(DA), _full((DA, DA))],
        out_specs=[pl.BlockSpec((ts, DIN), lambda i: (i, 0)), acc8, acc8],
        out_shape=[jax.ShapeDtypeStruct((s, DIN), BF16), p8, p8],
        compiler_params=_params("arbitrary"),
    )(du, dq, dk, dv, proj, proj, qg, kg, bd)


def _split3(a):
    hi = a.astype(BF16)
    return hi, (a - hi.astype(F32)).astype(BF16)


def _dot3(a, b, dn):
    ah, al = _split3(a)
    bh, bl = _split3(b)
    return _dot(ah, bh, dn) + (_dot(ah, bl, dn) + _dot(al, bh, dn))


def _ada_fwd(c_all, w, b, name):
    nw = w.shape[1]

    def body(c_ref, w_ref, b_ref, o_ref):
        cv = c_ref[...]
        act = cv / (1.0 + jnp.exp(-cv))
        o_ref[...] = _dot3(act, w_ref[...], NN) + b_ref[...]

    return pl.pallas_call(
        body, name=name, in_specs=[_full((NDEV, D)), _full(w.shape), _full((1, nw))], out_specs=_full((NDEV, nw)),
        out_shape=jax.ShapeDtypeStruct((NDEV, nw), F32), grid=(1,), compiler_params=_params("arbitrary"),
    )(c_all, w, b)


def _ada_bwd(c_all, dmod, name):
    nw = dmod.shape[1]

    def body(c_ref, d_ref, o_ref):
        cv = c_ref[...]
        act = cv / (1.0 + jnp.exp(-cv))
        o_ref[...] = _dot3(act, d_ref[...], TN)[None]

    return pl.pallas_call(
        body, name=name, in_specs=[_full((NDEV, D)), _full((NDEV, nw))], out_specs=_full((1, D, nw)),
        out_shape=jax.ShapeDtypeStruct((1, D, nw), F32), grid=(1,), compiler_params=_params("arbitrary"),
    )(c_all, dmod)


def _fold_heads(v):
    acc = v[:, 0:HD]
    for h in range(1, DA // HD):
        acc = acc + v[:, h * HD:(h + 1) * HD]
    return acc


def _pack_partials(pieces, dcw_p, name):
    n_p = len(pieces)
    total = sum(p.shape[1] for p in pieces) + 3 * dcw_p.shape[1]
    npack = -(-total // (SUBLANES * LANES)) * (SUBLANES * LANES)

    def body(*refs):
        out = refs[-1]
        off = 0
        for r in refs[:n_p]:
            out[:, off:off + r.shape[1]] = jnp.sum(r[...], axis=0, keepdims=True)
            off += r.shape[1]
        dw = refs[n_p]
        for tap in range(3):
            out[:, off:off + dw.shape[1]] = jnp.sum(dw[SUBLANES * tap:SUBLANES * (tap + 1), :], axis=0, keepdims=True)
            off += dw.shape[1]
        if off < npack:
            out[:, off:] = jnp.zeros((1, npack - off), F32)

    arrs = list(pieces) + [dcw_p]
    return pl.pallas_call(
        body, name=name, grid=(1,), in_specs=[_full(a.shape) for a in arrs], out_specs=_full((1, npack)),
        out_shape=jax.ShapeDtypeStruct((1, npack), F32), compiler_params=_params("arbitrary"),
    )(*arrs)


def _small_update(gathered, gathered_pw, gathered_cw, specs, params, loss_off, name):
    names = [sp[0] for sp in specs]
    flat = []
    for nme in names + ["pool_w", "conv_w"]:
        flat += list(params[nme])
    n_in = len(flat)

    def body(*refs):
        ga_ref, gp_ref, gc_ref = refs[0], refs[1], refs[2]
        prm = refs[3:3 + n_in]
        outs = refs[3 + n_in:]
        total = ga_ref[0:1, :]
        for dv in range(1, NDEV):
            total = total + ga_ref[dv:dv + 1, :]
        k = 0
        for idx, (nme, off, width, fold) in enumerate(specs):
            g = total[:, off:off + width]
            if fold:
                g = _fold_heads(g)
            w_ref, m_ref, v_ref = prm[3 * idx:3 * idx + 3]
            d, nm, nv = _adamw_math(w_ref[...], g, m_ref[...], v_ref[...])
            for val in (g, d, nm, nv):
                outs[k][...] = val
                k += 1
        gpw = gp_ref[0]
        for dv in range(1, NDEV):
            gpw = gpw + gp_ref[dv]
        w_ref, m_ref, v_ref = prm[3 * len(specs):3 * len(specs) + 3]
        d, nm, nv = _adamw_math(w_ref[...], gpw, m_ref[...], v_ref[...])
        for val in (gpw, d, nm, nv):
            outs[k][...] = val
            k += 1
        gcw = gc_ref[0]
        for dv in range(1, NDEV):
            gcw = gcw + gc_ref[dv]
        w_ref, m_ref, v_ref = prm[3 * len(specs) + 3:3 * len(specs) + 6]
        d, nm, nv = _adamw_math(w_ref[...], gcw, m_ref[...], v_ref[...])
        for val in (gcw, d, nm, nv):
            outs[k][...] = val
            k += 1
        outs[k][...] = ga_ref[:, 0:6 * D]
        outs[k + 1][...] = total[:, loss_off:loss_off + LANES] * (1.0 / SUBLANES)

    out_shape, out_specs = [], []
    for nme in names + ["pool_w", "conv_w"]:
        shp = params[nme][0].shape
        out_shape += [jax.ShapeDtypeStruct(shp, F32)] * 4
        out_specs += [_full(shp)] * 4
    out_shape += [jax.ShapeDtypeStruct((NDEV, 6 * D), F32), jax.ShapeDtypeStruct((1, LANES), F32)]
    out_specs += [_full((NDEV, 6 * D)), _full((1, LANES))]
    res = pl.pallas_call(
        body, name=name, grid=(1,),
        in_specs=[_full(gathered.shape), _full(gathered_pw.shape), _full(gathered_cw.shape)] + [_full(a.shape) for a in flat],
        out_specs=out_specs, out_shape=out_shape, compiler_params=_params("arbitrary"),
    )(gathered, gathered_pw, gathered_cw, *flat)
    out = {nme: tuple(res[4 * i:4 * i + 4]) for i, nme in enumerate(names + ["pool_w", "conv_w"])}
    return out, res[-2], res[-1][0, 0]


def _row_tile(s):
    return 512 if s % 512 == 0 else s


def kernel(x, c, ada_w, ada_b, norm1_g, w_in, pool_w, pool_b, pool_scale, q_norm_g, k_norm_g, attn_out_g, w_out, norm2_g, w_up, conv_w, conv_b, w_down, loss_target, m_ada_w, m_ada_b, m_norm1_g, m_w_in, m_pool_w, m_pool_b, m_pool_scale, m_q_norm_g, m_k_norm_g, m_attn_out_g, m_w_out, m_norm2_g, m_w_up, m_conv_w, m_conv_b, m_w_down, v_ada_w, v_ada_b, v_norm1_g, v_w_in, v_pool_w, v_pool_b, v_pool_scale, v_q_norm_g, v_k_norm_g, v_attn_out_g, v_w_out, v_norm2_g, v_w_up, v_conv_w, v_conv_b, v_w_down):
    ax, ay, ac = lax.axis_index("x"), lax.axis_index("y"), lax.axis_index("c")
    me = 4 * ax + 2 * ay + ac
    me_swapped = 4 * ay + 2 * ax + ac
    xs, tgt = x[0], loss_target[0]
    s = xs.shape[0]
    ts = _row_tile(s)
    tq_attn, tk_attn, hp_attn = 256, 256, 2
    tmm = 2 * ts
    bd = _block_diag_ones(DA, HD)

    w_in_t = w_in[0].T.astype(BF16)
    w_up_t = w_up[0].T.astype(BF16)
    c_gath, gw_in, gcw = _all_gather([jnp.broadcast_to(c, (SUBLANES, D)), w_in_t, jnp.pad(conv_w[0], ((0, 5), (0, 64)))],
                                     [False, False, True], "gather_in")
    c_all = c_gath[:, 0, :]
    n_ada = ada_w.shape[2]
    ada_b_mine = lax.dynamic_slice_in_dim(ada_b, me * n_ada, n_ada, axis=1)
    mod_part = _ada_fwd(c_all, ada_w[0], ada_b_mine, "ada_fwd")
    mod_all = _all_gather([mod_part], [False], "gather_mod")[0]
    mod = lax.dynamic_index_in_dim(mod_all, me, axis=1, keepdims=False).reshape(1, 6 * D)
    shift1, scale1, gate1, shift2, scale2, gate2 = [mod[:, k * D:(k + 1) * D] for k in range(6)]

    w_in_full = gw_in.reshape(DIN, D)
    later_w = [w_out[0].astype(BF16), w_up_t, w_down[0].astype(BF16)]
    cw_full = jnp.transpose(gcw[:, :3, :704], (1, 0, 2)).reshape(3, 2 * DFF)
    cb_full = jnp.transpose(conv_b.reshape(1, 2, 2, 2, 704), (0, 2, 1, 3, 4)).reshape(1, 2 * DFF)

    qg = jnp.tile(q_norm_g, (1, DA // HD))
    kg = jnp.tile(k_norm_g, (1, DA // HD))
    ag = attn_out_g.reshape(1, DA)
    pw = pool_w[0].astype(BF16)
    pb = pool_b.reshape(1, DP)
    h1 = _ln_mod(xs, norm1_g, scale1, shift1, ts=ts, name="ln1")
    proj = _matmul(h1, w_in_full, mode="nt", out_dtype=F32, tm=tmm,tn=DIN, tk=D, name="in_proj")
    qkv = _qk_norm(proj, qg, kg, bd, ts=ts, name="qk_norm")
    o_raw, m_tot, kb_first, (gw_out, gw_up, gw_down) = _attn_fwd(
        qkv, later_w, [False, True, False], tq=tq_attn, tk=tk_attn, hp=hp_attn, name="attn_fwd")
    w_out_full = gw_out.reshape(D, D)
    w_up_full = gw_up.reshape(2 * DFF, D)
    w_down_full = gw_down.reshape(DFF, D)
    mix = _pool_mix(proj, o_raw, pw, pb, pool_scale, ag, bd, ts=ts, name="pool_mix")
    att = _matmul(mix, w_out_full, mode="nn", out_dtype=F32, tm=tmm,tn=D, tk=D, name="out_proj")
    x1, h2 = _res_ln_mod(xs, att, gate1, norm2_g, scale2, shift2, ts=ts, name="res_ln2")
    up, conv, act = _up_conv_gate(h2, w_up_full, cw_full, cb_full, tm=ts, name="up_conv_gate")
    ffn = _matmul(act, w_down_full, mode="nn", out_dtype=F32, tm=tmm,tn=D, tk=DFF, name="down_proj")
    dy, dffn, dgate2_p, loss_p = _loss_head(x1, ffn, tgt, gate2, ts=ts, name="loss_head")

    g_w_down = _matmul(act, dffn, mode="tn", out_dtype=F32, tm=CF, tn=D, tk=tmm,name="down_wgrad")
    dconv, dcb_p = _down_bwd_gate(dffn, w_down_full, conv, tm=tmm, name="down_bwd_gate")
    dup, dh2, dcw_p = _conv_bwd_up_bwd(dconv, up, cw_full, w_up_full, tm=tmm, name="conv_bwd_up_bwd")
    g_w_up_t = _matmul(dup, h2, mode="tn", out_dtype=F32, tm=CF, tn=D, tk=tmm,name="up_wgrad")
    (dx1, datt, dshift2_p, dscale2_p, dnorm2_p, dgate1_p), _ = _ln_mod_bwd(
        dh2, x1, norm2_g, scale2, dy, att, gate1, ts=ts, name="ln2_bwd")

    dmix = _matmul(datt, w_out_full, mode="nt", out_dtype=BF16, tm=tmm,tn=D, tk=D, name="out_bwd")
    g_w_out = _matmul(mix, datt, mode="tn", out_dtype=F32, tm=D, tn=D, tk=tmm,name="out_wgrad")
    core = jnp.reshape(ac, (1,)).astype(jnp.int32)
    chip = jnp.reshape(2 * ax + ay, (1,)).astype(jnp.int32)
    big_ffn = [g_w_up_t.reshape(NDEV, 2 * DFF // NDEV, D), g_w_down.reshape(NDEV, DFF // NDEV, D),
               g_w_out.reshape(NDEV, D // NDEV, D)]
    swaps_ffn = [True, False, False]
    (du, do_raw, g_pw_p, dpb_p, dps_p, dag_p), gots_ffn = _mix_bwd(
        dmix, proj, o_raw, pw, pb, pool_scale, ag, bd, ts=ts, name="mix_bwd", side=_pair_side(big_ffn, swaps_ffn))
    sums_ffn = [_pair_sum(big_ffn[k], gots_ffn[k], swaps_ffn[k], core, "rs_pair_sum_ffn%d" % k) for k in range(3)]
    dqn, dkn, dvv, parts_ffn = _attn_bwd(qkv, do_raw, m_tot, kb_first, sums_ffn, tq=tq_attn, tk=tk_attn, hp=hp_attn, name="attn_bwd")
    dproj, dqg_p, dkg_p = _qk_norm_bwd(du, dqn, dkn, dvv, proj, qg, kg, bd, ts=ts, name="qk_norm_bwd")
    g_w_in_t = _matmul(dproj, h1, mode="tn", out_dtype=F32, tm=DIN // 2, tn=D, tk=tmm,name="in_wgrad")
    big = [g_w_in_t.reshape(NDEV, DIN // NDEV, D)]
    gots = _pair_exchange(big, [False], "rs_pair")
    sums = [_pair_sum(big[0], gots[0], False, core, "rs_pair_sum")]
    dh1, parts = _matmul(dproj, w_in_full, mode="nn", out_dtype=BF16, tm=tmm,tn=D, tk=DIN, name="in_bwd",
                         side=_chip_side(sums))
    (grad_x, dshift1_p, dscale1_p, dnorm1_p), _ = _ln_mod_bwd(
        dh1, xs, norm1_g, scale1, dx1, None, None, ts=ts, name="ln1_bwd")

    tr = lambda a: a[0].T
    r_in = _adamw_reduce(tr(w_in), tr(m_w_in), tr(v_w_in), sums[0], parts[0], chip, "adamw_w_in")
    r_out = _adamw_reduce(w_out[0], m_w_out[0], v_w_out[0], sums_ffn[2], parts_ffn[2], chip, "adamw_w_out")
    r_up = _adamw_reduce(tr(w_up), tr(m_w_up), tr(v_w_up), sums_ffn[0], parts_ffn[0], chip, "adamw_w_up")
    r_down = _adamw_reduce(w_down[0], m_w_down[0], v_w_down[0], sums_ffn[1], parts_ffn[1], chip, "adamw_w_down")
    r_in = [a.T[None] for a in r_in]
    r_up = [a.T[None] for a in r_up]
    r_out = [a[None] for a in r_out]
    r_down = [a[None] for a in r_down]

    dcb_nat = jnp.transpose(dcb_p.reshape(SUBLANES, 2, 2, 2, 704), (0, 2, 1, 3, 4)).reshape(SUBLANES, 2 * DFF)
    pieces = [dshift1_p, dscale1_p, dgate1_p, dshift2_p, dscale2_p, dgate2_p,
              dnorm1_p, dnorm2_p, dcb_nat, dpb_p, dps_p, dag_p, dqg_p, dkg_p, loss_p]
    n_vec = sum(p.shape[1] for p in pieces)
    packed = _pack_partials(pieces, dcw_p, "pack_partials")
    npack = packed.shape[1]
    gathered, gathered_pw = _all_gather([packed.reshape(SUBLANES, npack // SUBLANES), g_pw_p], [False, False], "gather_small")
    gathered = gathered.reshape(NDEV, npack)
    gathered_cw = lax.dynamic_index_in_dim(
        gathered[:, n_vec:n_vec + 6 * DFF].reshape(NDEV, 3, NDEV, 704), me_swapped, axis=2, keepdims=False)
    specs = [("ada_b", 0, 6 * D, False)]
    off = 6 * D
    for nme, width, fold in (("norm1_g", D, False), ("norm2_g", D, False), ("conv_b", 2 * DFF, False),
                             ("pool_b", DP, False), ("pool_scale", DP, False), ("attn_out_g", DA, False),
                             ("q_norm_g", DA, True), ("k_norm_g", DA, True)):
        specs.append((nme, off, width, fold))
        off += width
    small = {
        "ada_b": (ada_b, m_ada_b, v_ada_b),
        "norm1_g": (norm1_g, m_norm1_g, v_norm1_g), "norm2_g": (norm2_g, m_norm2_g, v_norm2_g),
        "conv_b": (conv_b, m_conv_b, v_conv_b),
        "pool_b": (pb, m_pool_b.reshape(1, DP), v_pool_b.reshape(1, DP)),
        "pool_scale": (pool_scale, m_pool_scale, v_pool_scale),
        "attn_out_g": (ag, m_attn_out_g.reshape(1, DA), v_attn_out_g.reshape(1, DA)),
        "q_norm_g": (q_norm_g, m_q_norm_g, v_q_norm_g), "k_norm_g": (k_norm_g, m_k_norm_g, v_k_norm_g),
        "pool_w": (pool_w.reshape(DP, LANES), m_pool_w.reshape(DP, LANES), v_pool_w.reshape(DP, LANES)),
        "conv_w": (conv_w[0], m_conv_w[0], v_conv_w[0]),
    }
    upd, dmod_all, loss = _small_update(gathered, gathered_pw, gathered_cw, specs, small, off, "small_update")
    g_ada_w = _ada_bwd(c_all, lax.dynamic_slice_in_dim(dmod_all, me * n_ada, n_ada, axis=1), "ada_bwd")
    r_ada = [g_ada_w] + [a[None] for a in _adamw(ada_w[0], m_ada_w[0], v_ada_w[0], g_ada_w[0], "adamw_ada_w")]

    shapes = {"ada_b": ada_b.shape, "norm1_g": norm1_g.shape, "pool_w": pool_w.shape, "pool_b": pool_b.shape,
              "pool_scale": pool_scale.shape, "q_norm_g": q_norm_g.shape, "k_norm_g": k_norm_g.shape,
              "attn_out_g": attn_out_g.shape, "norm2_g": norm2_g.shape, "conv_w": conv_w.shape, "conv_b": conv_b.shape}
    res = {nme: [a.reshape(shapes[nme]) for a in upd[nme]] for nme in shapes}
    res.update(ada_w=r_ada, w_in=r_in, w_out=r_out, w_up=r_up, w_down=r_down)
    names = ["ada_w", "ada_b", "norm1_g", "w_in", "pool_w", "pool_b", "pool_scale", "q_norm_g", "k_norm_g",
             "attn_out_g", "w_out", "norm2_g", "w_up", "conv_w", "conv_b", "w_down"]
    outs = [loss, grad_x[None]]
    for q in range(4):
        outs += [res[nme][q] for nme in names]
    return tuple(outs)
```

```python
import functools
import math

import numpy as np
import jax
import jax.numpy as jnp
from jax import lax
from jax.experimental import pallas as pl
from jax.experimental.pallas import tpu as pltpu

F32, BF16 = jnp.float32, jnp.bfloat16
D = 1024
DP = 512
DA = 512
HD = 64
DIN = DP + 3 * DA
DFF = 2816
POOL_WINDOWS = (2, 4, 8, 16)
HALO = 16
EPS = 1e-6
LANES = 128
SUBLANES = 8
NDEV = 8
VMEM_LIMIT = 56 * 1024 * 1024
MESH = pl.DeviceIdType.MESH

ADAM_LR, ADAM_B1, ADAM_B2, ADAM_EPS, ADAM_WD, ADAM_STEP = 0.001, 0.9, 0.999, 1e-08, 0.01, 10

NN = (((1,), (0,)), ((), ()))
NT = (((1,), (1,)), ((), ()))
TN = (((0,), (0,)), ((), ()))


def _params(*sem):
    return pltpu.CompilerParams(dimension_semantics=sem, vmem_limit_bytes=VMEM_LIMIT)


def _full(shape):
    nd = len(shape)
    return pl.BlockSpec(shape, lambda *_: (0,) * nd)


def _dot(a, b, dn=NN):
    return lax.dot_general(a, b, dn, preferred_element_type=F32)


def _split_dot(a, b, dn=NN):
    hi = a.astype(BF16)
    lo = (a - hi.astype(F32)).astype(BF16)
    return _dot(hi, b, dn) + _dot(lo, b, dn)


def _colsum8(v):
    r, n = v.shape
    return v.reshape(r // SUBLANES, SUBLANES, n).sum(axis=0)


def _block_diag_ones(n, blk):
    i = np.arange(n) // blk
    return jnp.asarray((i[:, None] == i[None, :]).astype(np.float32), BF16)


def _matmul(a, b, *, mode, out_dtype, tm, tn, tk, name, n_outer=False, side=None):
    if mode == "tn":
        K, M = a.shape
        N = b.shape[1]
    elif mode == "nt":
        M, K = a.shape
        N = b.shape[0]
    else:
        M, K = a.shape
        N = b.shape[1]
    tm, tn, tk = min(tm, M), min(tn, N), min(tk, K)
    assert M % tm == 0 and N % tn == 0 and K % tk == 0, (name, M, N, K, tm, tn, tk)
    nk = K // tk
    dn = {"nn": NN, "nt": NT, "tn": TN}[mode]

    def body(a_ref, b_ref, o_ref, *acc):
        if nk == 1:
            o_ref[...] = _dot(a_ref[...], b_ref[...], dn).astype(o_ref.dtype)
            return
        acc_ref, = acc
        k = pl.program_id(2)

        @pl.when(k == 0)
        def _():
            acc_ref[...] = jnp.zeros_like(acc_ref)

        acc_ref[...] += _dot(a_ref[...], b_ref[...], dn)

        @pl.when(k == nk - 1)
        def _():
            o_ref[...] = acc_ref[...].astype(o_ref.dtype)

    if n_outer:
        gi = lambda g: (g[1], g[0], g[2])
        grid = (N // tn, M // tm, nk)
    else:
        gi = lambda g: g
        grid = (M // tm, N // tn, nk)

    def amap(*g):
        i, j, k = gi(g)
        return (k, i) if mode == "tn" else (i, k)

    def bmap(*g):
        i, j, k = gi(g)
        return (j, k) if mode == "nt" else (k, j)

    def omap(*g):
        i, j, k = gi(g)
        return (i, j)

    a_blk = (tk, tm) if mode == "tn" else (tm, tk)
    b_blk = (tn, tk) if mode == "nt" else (tk, tn)
    acc_scratch = [] if nk == 1 else [pltpu.VMEM((tm, tn), F32)]
    if side is None:
        return pl.pallas_call(
            body, name=name, grid=grid,
            in_specs=[pl.BlockSpec(a_blk, amap), pl.BlockSpec(b_blk, bmap)],
            out_specs=pl.BlockSpec((tm, tn), omap),
            out_shape=jax.ShapeDtypeStruct((M, N), out_dtype),
            scratch_shapes=acc_scratch,
            compiler_params=_params("parallel", "parallel", "arbitrary"),
        )(a, b)

    ne = len(side.arrs)
    steps = grid[0] * grid[1] * grid[2]

    nsem = len(side.scratch)

    def with_side(*refs):
        e_in, e_out = refs[2:2 + ne], refs[3 + ne:3 + 2 * ne]
        sems = refs[len(refs) - nsem:]
        step = (pl.program_id(0) * grid[1] + pl.program_id(1)) * grid[2] + pl.program_id(2)

        @pl.when(step == 0)
        def _():
            side.start(e_in, e_out, *sems)

        body(refs[0], refs[1], refs[2 + ne], *refs[3 + 2 * ne:len(refs) - nsem])

        @pl.when(step == steps - 1)
        def _():
            side.finish(e_in, e_out, *sems)

    any_spec = pl.BlockSpec(memory_space=pl.ANY)
    res = pl.pallas_call(
        with_side, name=name, grid=grid,
        in_specs=[pl.BlockSpec(a_blk, amap), pl.BlockSpec(b_blk, bmap)] + [any_spec] * ne,
        out_specs=[pl.BlockSpec((tm, tn), omap)] + [any_spec] * ne,
        out_shape=[jax.ShapeDtypeStruct((M, N), out_dtype)] + side.out_shapes,
        scratch_shapes=acc_scratch + side.scratch,
        compiler_params=_params("arbitrary", "arbitrary", "arbitrary"),
    )(a, b, *side.arrs)
    return res[0], list(res[1:])


def _slot(swap, px, py, pc):
    return 4 * py + 2 * px + pc if swap else 4 * px + 2 * py + pc


class _Gather:
    def __init__(self, ins, outs, send, recv, loc, swaps):
        self.ins, self.outs, self.send, self.recv, self.loc, self.swaps = ins, outs, send, recv, loc, swaps
        x, y, c = lax.axis_index("x"), lax.axis_index("y"), lax.axis_index("c")
        self.me, self.sib = (x, y, c), (x, y, 1 - c)
        self.chips = [(1 - x, y), (x, 1 - y), (1 - x, 1 - y)]
        self.n = len(ins)

    @staticmethod
    def scratch(n):
        return [pltpu.SemaphoreType.DMA((7 * n,)), pltpu.SemaphoreType.DMA((7 * n,)), pltpu.SemaphoreType.DMA((n,))]

    def copy(self, a, k, blk, to, src=None):
        rows = self.outs[a].at[_slot(self.swaps[a], *blk)]
        return pltpu.make_async_remote_copy(
            src_ref=rows if src is None else src, dst_ref=rows,
            send_sem=self.send.at[7 * a + k], recv_sem=self.recv.at[7 * a + k], device_id=to, device_id_type=MESH)

    def mine(self, a):
        return pltpu.make_async_copy(self.ins[a], self.outs[a].at[_slot(self.swaps[a], *self.me)], self.loc.at[a])

    def first(self, a):
        c = self.me[2]
        return [self.copy(a, 0, self.me, self.sib, src=self.ins[a])] + [
            self.copy(a, 1 + j, self.me, (*chip, c), src=self.ins[a]) for j, chip in enumerate(self.chips)]

    def forwards(self, a):
        c = self.me[2]
        return [self.copy(a, 4 + j, (*chip, c), self.sib) for j, chip in enumerate(self.chips)]

    def start(self):
        for a in range(self.n):
            self.mine(a).start()
        for a in range(self.n):
            for cp in self.first(a):
                cp.start()

    def forward(self):
        c = self.me[2]
        for a in range(self.n):
            fwd = self.forwards(a)
            for j, chip in enumerate(self.chips):
                self.copy(a, 1 + j, (*chip, c), self.me).wait_recv()
                fwd[j].start()

    def finish(self):
        c = self.me[2]
        for a in range(self.n):
            self.copy(a, 0, self.sib, self.me).wait_recv()
            for j, chip in enumerate(self.chips):
                self.copy(a, 4 + j, (*chip, 1 - c), self.me).wait_recv()
        for a in range(self.n):
            for cp in self.first(a) + self.forwards(a):
                cp.wait_send()
            self.mine(a).wait()


def _all_gather(arrs, swaps, name):
    n = len(arrs)

    def body(*refs):
        g = _Gather(refs[:n], refs[n:2 * n], *refs[2 * n:], swaps)
        g.start()
        g.forward()
        g.finish()

    any_spec = pl.BlockSpec(memory_space=pl.ANY)
    return pl.pallas_call(
        body, name=name,
        in_specs=[any_spec] * n, out_specs=[any_spec] * n,
        out_shape=[jax.ShapeDtypeStruct((NDEV,) + a.shape, a.dtype) for a in arrs],
        scratch_shapes=_Gather.scratch(n),
    )(*arrs)


def _pair_copies(ins, gots, send, recv, swaps):
    x, y, c = lax.axis_index("x"), lax.axis_index("y"), lax.axis_index("c")
    return [pltpu.make_async_remote_copy(
        src_ref=ins[a].at[_slot(swaps[a], k // 2, k % 2, 1 - c)], dst_ref=gots[a].at[k],
        send_sem=send.at[4 * a + k], recv_sem=recv.at[4 * a + k], device_id=(x, y, 1 - c), device_id_type=MESH)
        for a in range(len(ins)) for k in range(4)]


def _pair_exchange(arrs, swaps, name):
    n = len(arrs)

    def body(*refs):
        rems = _pair_copies(refs[:n], refs[n:2 * n], *refs[2 * n:], swaps)
        for rc in rems:
            rc.start()
        for rc in rems:
            rc.wait_recv()
        for rc in rems:
            rc.wait_send()

    any_spec = pl.BlockSpec(memory_space=pl.ANY)
    return pl.pallas_call(
        body, name=name,
        in_specs=[any_spec] * n, out_specs=[any_spec] * n,
        out_shape=[jax.ShapeDtypeStruct((4,) + a.shape[1:], a.dtype) for a in arrs],
        scratch_shapes=[pltpu.SemaphoreType.DMA((4 * n,)), pltpu.SemaphoreType.DMA((4 * n,))],
    )(*arrs)


def _chip_copies(ins, outs, send, recv):
    x, y, c = lax.axis_index("x"), lax.axis_index("y"), lax.axis_index("c")
    chips = [(1 - x, y), (x, 1 - y), (1 - x, 1 - y)]
    return [pltpu.make_async_remote_copy(
        src_ref=ins[a].at[2 * px + py], dst_ref=outs[a].at[j], send_sem=send.at[3 * a + j], recv_sem=recv.at[3 * a + j],
        device_id=(px, py, c), device_id_type=MESH) for a in range(len(ins)) for j, (px, py) in enumerate(chips)]


def _chip_exchange(arrs, name):
    n = len(arrs)

    def body(*refs):
        rems = _chip_copies(refs[:n], refs[n:2 * n], *refs[2 * n:])
        for rc in rems:
            rc.start()
        for rc in rems:
            rc.wait_recv()
        for rc in rems:
            rc.wait_send()

    any_spec = pl.BlockSpec(memory_space=pl.ANY)
    return pl.pallas_call(
        body, name=name,
        in_specs=[any_spec] * n, out_specs=[any_spec] * n,
        out_shape=[jax.ShapeDtypeStruct((3,) + a.shape[1:], a.dtype) for a in arrs],
        scratch_shapes=[pltpu.SemaphoreType.DMA((3 * n,)), pltpu.SemaphoreType.DMA((3 * n,))],
    )(*arrs)


class _Side:
    def __init__(self, arrs, out_shapes, scratch, start, finish, mid=None):
        self.arrs, self.out_shapes, self.scratch = list(arrs), list(out_shapes), list(scratch)
        self.start, self.finish, self.mid = start, finish, mid


def _copies_side(arrs, out_shapes, n_copies, make):
    def start(ins, outs, *sems):
        for cp in make(ins, outs, *sems):
            cp.start()

    def finish(ins, outs, *sems):
        cps = make(ins, outs, *sems)
        for cp in cps:
            cp.wait_recv()
        for cp in cps:
            cp.wait_send()

    return _Side(arrs, out_shapes, [pltpu.SemaphoreType.DMA((n_copies,)), pltpu.SemaphoreType.DMA((n_copies,))], start, finish)


def _pair_side(arrs, swaps):
    return _copies_side(arrs, [jax.ShapeDtypeStruct((4,) + a.shape[1:], a.dtype) for a in arrs], 4 * len(arrs),
                        functools.partial(_pair_copies, swaps=swaps))


def _chip_side(arrs):
    return _copies_side(arrs, [jax.ShapeDtypeStruct((3,) + a.shape[1:], a.dtype) for a in arrs], 3 * len(arrs), _chip_copies)


def _gather_side(arrs, swaps):
    return _Side(arrs, [jax.ShapeDtypeStruct((NDEV,) + a.shape, a.dtype) for a in arrs], _Gather.scratch(len(arrs)),
                 start=lambda ins, outs, *sems: _Gather(ins, outs, *sems, swaps).start(),
                 mid=lambda ins, outs, *sems: _Gather(ins, outs, *sems, swaps).forward(),
                 finish=lambda ins, outs, *sems: _Gather(ins, outs, *sems, swaps).finish())


def _row_call(body, side, *, name, steps, in_specs, out_specs, out_shape, ins):
    if side is None:
        res = pl.pallas_call(body, name=name, grid=(steps,), in_specs=in_specs, out_specs=out_specs, out_shape=out_shape,
                             compiler_params=_params("arbitrary"))(*ins)
        return list(res), []
    n_in, n_out, ne = len(in_specs), len(out_specs), len(side.arrs)

    def wrapped(*refs):
        e_in = refs[n_in:n_in + ne]
        e_out = refs[n_in + ne + n_out:n_in + 2 * ne + n_out]
        sems = refs[n_in + 2 * ne + n_out:]
        i = pl.program_id(0)

        @pl.when(i == 0)
        def _():
            side.start(e_in, e_out, *sems)

        if side.mid is not None:
            @pl.when(i == steps // 2)
            def _():
                side.mid(e_in, e_out, *sems)

        body(*refs[:n_in], *refs[n_in + ne:n_in + ne + n_out])

        @pl.when(i == steps - 1)
        def _():
            side.finish(e_in, e_out, *sems)

    any_spec = pl.BlockSpec(memory_space=pl.ANY)
    res = pl.pallas_call(
        wrapped, name=name, grid=(steps,), in_specs=list(in_specs) + [any_spec] * ne,
        out_specs=list(out_specs) + [any_spec] * ne, out_shape=list(out_shape) + side.out_shapes,
        scratch_shapes=side.scratch,
        compiler_params=_params("arbitrary"))(*ins, *side.arrs)
    return list(res[:n_out]), list(res[n_out:])


def _pair_sum(grads, got, swap, core, name):
    _, r, c = got.shape
    tr = r if r <= 352 else r // 2

    def own_map(k, i, core_ref):
        return (_slot(swap, k // 2, k % 2, core_ref[0]), i, 0)

    def body(core_ref, a_ref, b_ref, o_ref):
        o_ref[...] = a_ref[...] + b_ref[...]

    spec = pl.BlockSpec((None, tr, c), lambda k, i, core_ref: (k, i, 0))
    return pl.pallas_call(
        body, name=name,
        grid_spec=pltpu.PrefetchScalarGridSpec(
            num_scalar_prefetch=1, grid=(4, r // tr),
            in_specs=[pl.BlockSpec((None, tr, c), own_map), spec], out_specs=spec),
        out_shape=jax.ShapeDtypeStruct(got.shape, got.dtype), compiler_params=_params("parallel", "parallel"),
    )(core, grads, got)


def _adamw_math(w, g, m, v):
    m = ADAM_B1 * m + (1.0 - ADAM_B1) * g
    v = ADAM_B2 * v + (1.0 - ADAM_B2) * (g * g)
    m_hat = m / (1.0 - ADAM_B1 ** ADAM_STEP)
    v_hat = v / (1.0 - ADAM_B2 ** ADAM_STEP)
    delta = -ADAM_LR * (m_hat / (jnp.sqrt(v_hat) + ADAM_EPS) + ADAM_WD * w)
    return delta, m, v


def _adamw_tile(r):
    for cand in (256, 352, 128):
        if r % cand == 0:
            return cand
    return r


def _adamw(w, m, v, g, name):
    r, c = w.shape
    tr = _adamw_tile(r)
    spec = pl.BlockSpec((tr, c), lambda i: (i, 0))

    def body(w_ref, m_ref, v_ref, g_ref, d_ref, nm_ref, nv_ref):
        d_ref[...], nm_ref[...], nv_ref[...] = _adamw_math(w_ref[...], g_ref[...], m_ref[...], v_ref[...])

    out = jax.ShapeDtypeStruct((r, c), F32)
    return pl.pallas_call(
        body, name=name, grid=(r // tr,), in_specs=[spec] * 4, out_specs=[spec] * 3, out_shape=[out] * 3,
        compiler_params=_params("parallel"),
    )(w, m, v, g)


def _adamw_reduce(w, m, v, sums, recv, chip, name):
    r, c = w.shape
    tr = _adamw_tile(r)
    spec = pl.BlockSpec((tr, c), lambda i, chip_ref: (i, 0))

    def body(chip_ref, w_ref, m_ref, v_ref, s_ref, p_ref, g_ref, d_ref, nm_ref, nv_ref):
        g = ((s_ref[...] + p_ref[0]) + p_ref[1]) + p_ref[2]
        g_ref[...] = g
        d_ref[...], nm_ref[...], nv_ref[...] = _adamw_math(w_ref[...], g, m_ref[...], v_ref[...])

    out = jax.ShapeDtypeStruct((r, c), F32)
    return pl.pallas_call(
        body, name=name,
        grid_spec=pltpu.PrefetchScalarGridSpec(
            num_scalar_prefetch=1, grid=(r // tr,),
            in_specs=[spec, spec, spec, pl.BlockSpec((None, tr, c), lambda i, chip_ref: (chip_ref[0], i, 0)),
                      pl.BlockSpec((3, tr, c), lambda i, chip_ref: (0, i, 0))],
            out_specs=[spec] * 4),
        out_shape=[out] * 4, compiler_params=_params("parallel"),
    )(chip, w, m, v, sums, recv)


def _vec(n):
    return pl.BlockSpec((1, n), lambda *_: (0, 0))


def _ln_mod(x, g, scale, shift, *, ts, name, side=None):
    s = x.shape[0]
    row = pl.BlockSpec((ts, D), lambda i: (i, 0))

    def body(x_ref, g_ref, sc_ref, sh_ref, h_ref):
        xv = x_ref[...]
        r = lax.rsqrt(jnp.mean(xv * xv, axis=-1, keepdims=True) + EPS)
        h = (xv * r) * g_ref[...]
        h_ref[...] = (h * (1.0 + sc_ref[...]) + sh_ref[...]).astype(BF16)

    (h,), extra = _row_call(body, side, name=name, steps=s // ts, in_specs=[row, _vec(D), _vec(D), _vec(D)],
                            out_specs=[row], out_shape=[jax.ShapeDtypeStruct((s, D), BF16)], ins=(x, g, scale, shift))
    return h, extra


def _proj_res_ln_mod(mix, w, x, gate, g, scale, shift, *, tm, name):
    s = x.shape[0]
    row = pl.BlockSpec((tm, D), lambda i: (i, 0))

    def body(m_ref, w_ref, x_ref, gt_ref, g_ref, sc_ref, sh_ref, a_ref, x1_ref, h_ref):
        att = _dot(m_ref[...], w_ref[...])
        a_ref[...] = att
        x1 = x_ref[...] + gt_ref[...] * att
        x1_ref[...] = x1
        r = lax.rsqrt(jnp.mean(x1 * x1, axis=-1, keepdims=True) + EPS)
        h = (x1 * r) * g_ref[...]
        h_ref[...] = (h * (1.0 + sc_ref[...]) + sh_ref[...]).astype(BF16)

    return pl.pallas_call(
        body, name=name, grid=(s // tm,), in_specs=[row, _full(w.shape), row] + [_vec(D)] * 4, out_specs=[row, row, row],
        out_shape=[jax.ShapeDtypeStruct((s, D), F32), jax.ShapeDtypeStruct((s, D), F32), jax.ShapeDtypeStruct((s, D), BF16)],
        compiler_params=_params("parallel"),
    )(mix, w, x, gate, g, scale, shift)


def _proj_loss_head(act, w, x1, tgt, gate2, *, tm, name):
    s = x1.shape[0]
    n = s // tm
    row = pl.BlockSpec((tm, D), lambda i: (i, 0))
    acc8 = pl.BlockSpec((SUBLANES, D), lambda i: (0, 0))

    def body(a_ref, w_ref, x_ref, t_ref, g_ref, dy_ref, df_ref, dg_ref, loss_ref, lacc):
        i = pl.program_id(0)

        @pl.when(i == 0)
        def _():
            lacc[...] = jnp.zeros_like(lacc)
            dg_ref[...] = jnp.zeros_like(dg_ref)

        f = _dot(a_ref[...], w_ref[...])
        diff = x_ref[...] + g_ref[...] * f - t_ref[...]
        lacc[...] += _colsum8(diff * diff)
        dy = diff * (1.0 / D)
        dy_ref[...] = dy
        df_ref[...] = (dy * g_ref[...]).astype(BF16)
        dg_ref[...] += _colsum8(dy * f)

        @pl.when(i == n - 1)
        def _():
            loss_ref[...] = jnp.full((SUBLANES, LANES), (0.5 / D) * jnp.sum(lacc[...]), F32)

    return pl.pallas_call(
        body, name=name, grid=(n,),
        in_specs=[pl.BlockSpec((tm, act.shape[1]), lambda i: (i, 0)), _full(w.shape), row, row, _vec(D)],
        out_specs=[row, row, acc8, _full((SUBLANES, LANES))],
        out_shape=[jax.ShapeDtypeStruct((s, D), F32), jax.ShapeDtypeStruct((s, D), BF16),
                   jax.ShapeDtypeStruct((SUBLANES, D), F32), jax.ShapeDtypeStruct((SUBLANES, LANES), F32)],
        scratch_shapes=[pltpu.VMEM((SUBLANES, D), F32)], compiler_params=_params("arbitrary"),
    )(act, w, x1, tgt, gate2)


def _group_rsqrt(t, bd):
    return lax.rsqrt(_split_dot(t * t, bd) * (1.0 / HD) + EPS)


def _qk_norm(proj, qg, kg, bd, *, ts, name):
    s = proj.shape[0]

    def body(q_ref, k_ref, v_ref, qg_ref, kg_ref, bd_ref, o_ref):
        bdv = bd_ref[...]
        q, k = q_ref[...], k_ref[...]
        o_ref[:, 0:DA] = (q * _group_rsqrt(q, bdv) * qg_ref[...]).astype(BF16)
        o_ref[:, DA:2 * DA] = (k * _group_rsqrt(k, bdv) * kg_ref[...]).astype(BF16)
        o_ref[:, 2 * DA:] = v_ref[...].astype(BF16)

    col = lambda j: pl.BlockSpec((ts, DA), lambda i: (i, j))
    return pl.pallas_call(
        body, name=name, grid=(s // ts,),
        in_specs=[col(1), col(2), col(3), _vec(DA), _vec(DA), _full((DA, DA))],
        out_specs=pl.BlockSpec((ts, 3 * DA), lambda i: (i, 0)),
        out_shape=jax.ShapeDtypeStruct((s, 3 * DA), BF16), compiler_params=_params("parallel"),
    )(proj, proj, proj, qg, kg, bd)


EXP_UNDERFLOW = -120.0


def _log_terms(z):
    neg_abs = lax.bitcast_convert_type(lax.bitcast_convert_type(z, jnp.uint32) | jnp.uint32(0x80000000), F32)
    b = jnp.minimum(z, 0.0) - jnp.log(1.0 + jnp.exp(neg_abs))
    return b, b - z


def _head_masks(rows):
    lane = lax.broadcasted_iota(jnp.int32, (rows, LANES), 1)
    return [lane < HD, lane >= HD]


def _attn_fwd(qkv, gather, swaps, *, tq, tk, hp, name):
    s = qkv.shape[0]
    nrep = tk // LANES
    ndiag = tq // tk
    ng = len(gather)
    nh, wl = 2 * hp, LANES * hp
    ngrp, nq = DA // wl, s // tq
    lanes = [slice(LANES * pp, LANES * (pp + 1)) for pp in range(hp)]

    def body(*refs):
        q_ref, k_ref, v_ref = refs[:3]
        g_in = refs[3:3 + ng]
        o_ref, tot_ref, first_ref = refs[3 + ng:6 + ng]
        g_out = refs[6 + ng:6 + 2 * ng]
        oacc, rc = refs[6 + 2 * ng:8 + 2 * ng]
        g_sems = refs[8 + 2 * ng:]
        i = pl.program_id(1)
        step_id = pl.program_id(0) * nq + i

        @pl.when(step_id == 0)
        def _():
            _Gather(g_in, g_out, *g_sems, swaps).start()

        @pl.when(step_id == (ngrp * nq * 3) // 4)
        def _():
            _Gather(g_in, g_out, *g_sems, swaps).forward()

        heads = _head_masks(tq)
        qs = [jnp.where(heads[a % 2], q_ref[:, lanes[a // 2]] * 0.125, 0.0).astype(BF16) for a in range(nh)]
        dif = lax.broadcasted_iota(jnp.int32, (tq, tk), 0) - lax.broadcasted_iota(jnp.int32, (tq, tk), 1)
        kr = lax.broadcasted_iota(jnp.int32, (tk, tk), 0)
        kc = lax.broadcasted_iota(jnp.int32, (tk, tk), 1)
        later =jnp.where(kr > kc, 1.0, 0.0).astype(BF16)
        oacc[...] = jnp.zeros_like(oacc)
        rc[...] = jnp.zeros_like(rc)

        def tile(kb, thr):
            rows = pl.ds(pl.multiple_of(kb * tk, tk), tk)
            ks = [k_ref[rows, ln] for ln in lanes]
            vs = [v_ref[rows, ln] for ln in lanes]
            qr = slice(0 if thr is None else thr, tq)
            rcv = [rc[a, qr, :] for a in range(nh)]
            zs = [_dot(qs[a][qr], ks[a // 2], NT) for a in range(nh)]
            bs, mbs = [], []
            for a in range(nh):
                b, m = _log_terms(zs[a])
                if thr is not None:
                    m = jnp.where(dif[qr] > thr, m, 0.0)
                bs.append(b)
                mbs.append(m.astype(BF16))
            rl = [_dot(mbs[a], later) for a in range(nh)]
            for a in range(nh):
                p = jnp.exp(bs[a] + (rl[a] + jnp.tile(rcv[a], (1, nrep))))
                if thr is not None:
                    p = jnp.where(dif[qr] > thr, p, 0.0)
                oacc[a, qr, :] += _dot(p.astype(BF16), vs[a // 2])
                rc[a, qr, :] = rcv[a] + (rl[a][:, 0:1] + mbs[a][:, 0:1].astype(F32))

        for d in reversed(range(ndiag)):
            tile(i * ndiag + d, d * tk)

        def live():
            top = rc[0]
            for a in range(1, nh):
                top = jnp.maximum(top, rc[a])
            return jnp.max(top) > EXP_UNDERFLOW

        def step(carry):
            kb, _ = carry
            tile(kb, None)
            return kb - 1, live()

        kb_end, _ = lax.while_loop(lambda cr: jnp.logical_and(cr[0] >= 0, cr[1]), step, (i * ndiag - 1, live()))
        first_ref[pl.program_id(0), i] = (kb_end + 1).astype(F32)
        for pp, ln in enumerate(lanes):
            o_ref[:, ln] = jnp.where(heads[0], oacc[2 * pp], oacc[2 * pp + 1])
            tot_ref[:, ln] = jnp.where(heads[0], rc[2 * pp], rc[2 * pp + 1])

        @pl.when(step_id == ngrp * nq - 1)
        def _():
            _Gather(g_in, g_out, *g_sems, swaps).finish()

    qspec = pl.BlockSpec((tq, wl), lambda p, i: (i, p))
    any_spec = pl.BlockSpec(memory_space=pl.ANY)
    res = pl.pallas_call(
        body, name=name, grid=(ngrp, nq),
        in_specs=[qspec,
                  pl.BlockSpec((s, wl), lambda p, i: (0, ngrp + p)),
                  pl.BlockSpec((s, wl), lambda p, i: (0, 2 * ngrp + p))] + [any_spec] * ng,
        out_specs=[qspec, qspec, pl.BlockSpec(memory_space=pltpu.SMEM)] + [any_spec] * ng,
        out_shape=[jax.ShapeDtypeStruct((s, DA), F32), jax.ShapeDtypeStruct((s, DA), F32),
                   jax.ShapeDtypeStruct((ngrp, nq), F32)]
        + [jax.ShapeDtypeStruct((NDEV,) + a.shape, a.dtype) for a in gather],
        scratch_shapes=[pltpu.VMEM((nh, tq, LANES), F32), pltpu.VMEM((nh, tq, LANES), F32)] + _Gather.scratch(ng),
        compiler_params=_params("arbitrary", "arbitrary"),
    )(qkv, qkv, qkv, *gather)
    return res[0], res[1], res[2], res[3:]


def _attn_bwd(qkv, do, tot, first, exchange, *, tq, tk, hp, name):
    s = qkv.shape[0]
    nrep = tk // LANES
    ndiag = tq // tk
    ne = len(exchange)
    nh, wl = 2 * hp, LANES * hp
    ngrp, nq = DA // wl, s // tq
    lanes = [slice(LANES * pp, LANES * (pp + 1)) for pp in range(hp)]

    def body(*refs):
        q_ref, k_ref, v_ref, do_ref, tot_ref, first_ref = refs[:6]
        e_in = refs[6:6 + ne]
        dq_ref, dk_ref, dv_ref = refs[6 + ne:9 + ne]
        e_out = refs[9 + ne:9 + 2 * ne]
        dqacc, rem, gc = refs[9 + 2 * ne:12 + 2 * ne]
        e_sems = refs[12 + 2 * ne:]
        i = pl.program_id(1)
        step_id = pl.program_id(0) * nq + i

        @pl.when(step_id == 0)
        def _():
            for cp in _chip_copies(e_in, e_out, *e_sems):
                cp.start()

        @pl.when(i == 0)
        def _():
            dk_ref[...] = jnp.zeros_like(dk_ref)
            dv_ref[...] = jnp.zeros_like(dv_ref)

        heads = _head_masks(tq)
        qs = [jnp.where(heads[a % 2], q_ref[:, lanes[a // 2]] * 0.125, 0.0).astype(BF16) for a in range(nh)]
        dob = [jnp.where(heads[a % 2], do_ref[:, lanes[a // 2]], 0.0).astype(BF16) for a in range(nh)]
        dif = lax.broadcasted_iota(jnp.int32, (tq, tk), 0) - lax.broadcasted_iota(jnp.int32, (tq, tk), 1)
        kr = lax.broadcasted_iota(jnp.int32, (tk, tk), 0)
        kc = lax.broadcasted_iota(jnp.int32, (tk, tk), 1)
        up_incl = jnp.where(kr <= kc, 1.0, 0.0).astype(BF16)
        up_strict = jnp.where(kr < kc, 1.0, 0.0).astype(BF16)
        dqacc[...] = jnp.zeros_like(dqacc)
        gc[...] = jnp.zeros_like(gc)
        for pp, ln in enumerate(lanes):
            totv = tot_ref[:, ln]
            swapped = pltpu.roll(totv, HD, axis=1)
            rem[2 * pp] = jnp.where(heads[0], totv, swapped)
            rem[2 * pp + 1] = jnp.where(heads[1], totv, swapped)

        def tile(kb, thr):
            rows = pl.ds(pl.multiple_of(kb * tk, tk), tk)
            ks = [k_ref[rows, ln] for ln in lanes]
            vs = [v_ref[rows, ln] for ln in lanes]
            qr = slice(0 if thr is None else thr, tq)
            remv = [rem[a, qr, :] for a in range(nh)]
            gcv = [gc[a, qr, :] for a in range(nh)]
            zs = [_dot(qs[a][qr], ks[a // 2], NT) for a in range(nh)]
            das = [_dot(dob[a][qr], vs[a // 2], NT) for a in range(nh)]
            bs, mbs = [], []
            for a in range(nh):
                b, m = _log_terms(zs[a])
                if thr is not None:
                    m = jnp.where(dif[qr] > thr, m, 0.0)
                bs.append(b)
                mbs.append(m.astype(BF16))
            pl_ = [_dot(mbs[a], up_incl) for a in range(nh)]
            ps, gs, gbs = [], [], []
            for a in range(nh):
                p = jnp.exp(bs[a] + (jnp.tile(remv[a], (1, nrep)) - pl_[a]))
                if thr is not None:
                    p = jnp.where(dif[qr] > thr, p, 0.0)
                g = p * das[a]
                ps.append(p.astype(BF16))
                gs.append(g)
                gbs.append(g.astype(BF16))
            cl = [_dot(gbs[a], up_strict) for a in range(nh)]
            dk_add = [jnp.zeros((tk, LANES), F32) for _ in range(hp)]
            dv_add = [jnp.zeros((tk, LANES), F32) for _ in range(hp)]
            for a in range(nh):
                dz = gs[a] - jnp.exp(bs[a]) * (gs[a] + (jnp.tile(gcv[a], (1, nrep)) + cl[a]))
                if thr is not None:
                    dz = jnp.where(dif[qr] > thr, dz, 0.0)
                dzb = dz.astype(BF16)
                dqacc[a, qr, :] += _dot(dzb, ks[a // 2])
                dk_add[a // 2] += _dot(dzb, qs[a][qr], TN)
                dv_add[a // 2] += _dot(ps[a], dob[a][qr], TN)
                rem[a, qr, :] = remv[a] - pl_[a][:, tk - 1:tk]
                gc[a, qr, :] = gcv[a] + (cl[a][:, tk - 1:tk] + gbs[a][:, tk - 1:tk].astype(F32))
            for pp, ln in enumerate(lanes):
                dk_ref[rows, ln] += dk_add[pp]
                dv_ref[rows, ln] += dv_add[pp]

        def step(kb, carry):
            tile(kb, None)
            return carry

        lax.fori_loop(first_ref[pl.program_id(0), i].astype(jnp.int32), i * ndiag, step, 0)
        for d in range(ndiag):
            tile(i * ndiag + d, d * tk)
        for pp, ln in enumerate(lanes):
            dq_ref[:, ln] = jnp.where(heads[0], dqacc[2 * pp], dqacc[2 * pp + 1]) * 0.125

        @pl.when(step_id == ngrp * nq - 1)
        def _():
            cps = _chip_copies(e_in, e_out, *e_sems)
            for cp in cps:
                cp.wait_recv()
            for cp in cps:
                cp.wait_send()

    qspec = pl.BlockSpec((tq, wl), lambda p, i: (i, p))
    full = pl.BlockSpec((s, wl), lambda p, i: (0, p))
    any_spec = pl.BlockSpec(memory_space=pl.ANY)
    out = jax.ShapeDtypeStruct((s, DA), F32)
    res = pl.pallas_call(
        body, name=name, grid=(ngrp, nq),
        in_specs=[qspec, pl.BlockSpec((s, wl), lambda p, i: (0, ngrp + p), pipeline_mode=pl.Buffered(1)),
                  pl.BlockSpec((s, wl), lambda p, i: (0, 2 * ngrp + p), pipeline_mode=pl.Buffered(1)), qspec, qspec,
                  pl.BlockSpec(memory_space=pltpu.SMEM)] + [any_spec] * ne,
        out_specs=[qspec, full, full] + [any_spec] * ne,
        out_shape=[out, out, out] + [jax.ShapeDtypeStruct((3,) + a.shape[1:], a.dtype) for a in exchange],
        scratch_shapes=[pltpu.VMEM((nh, tq, LANES), F32)] * 3
        + [pltpu.SemaphoreType.DMA((3 * ne,)), pltpu.SemaphoreType.DMA((3 * ne,))],
        compiler_params=_params("arbitrary", "arbitrary"),
    )(qkv, qkv, qkv, do, tot, first, *exchange)
    return res[0], res[1], res[2], res[3:]


def _shift_rows(v, k):
    return pltpu.roll(v, k % v.shape[0], axis=0)


def _pooled(u, uh, i, g, w, ts):
    halo = jnp.where(i > 0, uh, 0.0)
    ue = jnp.concatenate([halo, u], axis=0)
    acc, span = ue, 1
    while span < w:
        acc = acc + _shift_rows(acc, span)
        span *= 2
    tpos = i * ts + lax.broadcasted_iota(jnp.int32, (ts, 1), 0)
    cnt = jnp.minimum(tpos + 1, w).astype(F32)
    return acc[HALO:] / cnt - u


def _pool_mix(proj, o, pw, pb, ps, ag, bd, *, ts, name):
    s = proj.shape[0]
    hb = ts // HALO

    def body(u_ref, uh_ref, o_ref, pw_ref, pb_ref, ps_ref, ag_ref, bd_ref, mix_ref):
        i = pl.program_id(0)
        for g, w in enumerate(POOL_WINDOWS):
            cols = slice(g * LANES, (g + 1) * LANES)
            pooled = _pooled(u_ref[:, cols], uh_ref[:, cols], i, g, w, ts)
            yv = (_dot(pooled.astype(BF16), pw_ref[g]) + pb_ref[:, cols]) * ps_ref[:, cols]
            mix_ref[:, cols] = yv.astype(BF16)
        ov = o_ref[...]
        mix_ref[:, DP:] = (ov * _group_rsqrt(ov, bd_ref[...]) * ag_ref[...]).astype(BF16)

    return pl.pallas_call(
        body, name=name, grid=(s // ts,),
        in_specs=[pl.BlockSpec((ts, DP), lambda i: (i, 0)),
                  pl.BlockSpec((HALO, DP), lambda i: (jnp.maximum(i * hb - 1, 0), 0)),
                  pl.BlockSpec((ts, DA), lambda i: (i, 0)),
                  _full((4, LANES, LANES)), _vec(DP), _vec(DP), _vec(DA), _full((DA, DA))],
        out_specs=pl.BlockSpec((ts, D), lambda i: (i, 0)),
        out_shape=jax.ShapeDtypeStruct((s, D), BF16), compiler_params=_params("parallel"),
    )(proj, proj, o, pw, pb, ps, ag, bd)


def _res_ln_mod(x, att, gate, g, scale, shift, *, ts, name):
    s = x.shape[0]
    row = pl.BlockSpec((ts, D), lambda i: (i, 0))

    def body(x_ref, a_ref, gt_ref, g_ref, sc_ref, sh_ref, x1_ref, h_ref):
        x1 = x_ref[...] + gt_ref[...] * a_ref[...]
        x1_ref[...] = x1
        r = lax.rsqrt(jnp.mean(x1 * x1, axis=-1, keepdims=True) + EPS)
        h = (x1 * r) * g_ref[...]
        h_ref[...] = (h * (1.0 + sc_ref[...]) + sh_ref[...]).astype(BF16)

    return pl.pallas_call(
        body, name=name, grid=(s // ts,), in_specs=[row, row] + [_vec(D)] * 4, out_specs=[row, row],
        out_shape=[jax.ShapeDtypeStruct((s, D), F32), jax.ShapeDtypeStruct((s, D), BF16)],
        compiler_params=_params("parallel"),
    )(x, att, gate, g, scale, shift)


CF = DFF // 2


def _conv(u, uh, w_ref, b_ref, i):
    halo = jnp.where(i > 0, uh.astype(F32), 0.0)
    ue = jnp.concatenate([halo, u.astype(F32)], axis=0)
    y = w_ref[2:3, :] * ue + w_ref[1:2, :] * _shift_rows(ue, 1) + w_ref[0:1, :] * _shift_rows(ue, 2)
    return y[HALO:] + b_ref[...]


def _conv_gate(up, cw, cb, *, ts, name):
    s = up.shape[0]
    hb = ts // HALO

    def body(u_ref, uh_ref, w_ref, b_ref, a_ref, c_ref):
        i = pl.program_id(0)
        c = _conv(u_ref[...], uh_ref[...], w_ref, b_ref, i)
        gt, vl = c[:, :CF], c[:, CF:]
        a_ref[...] = (gt / (1.0 + jnp.exp(-gt)) * vl).astype(BF16)
        c_ref[...] = c.astype(BF16)

    return pl.pallas_call(
        body, name=name, grid=(s // ts, 2),
        in_specs=[pl.BlockSpec((ts, 2 * CF), lambda i, j: (i, j)),
                  pl.BlockSpec((HALO, 2 * CF), lambda i, j: (jnp.maximum(i * hb - 1, 0), j)),
                  pl.BlockSpec((3, 2 * CF), lambda i, j: (0, j)), pl.BlockSpec((1, 2 * CF), lambda i, j: (0, j))],
        out_specs=[pl.BlockSpec((ts, CF), lambda i, j: (i, j)), pl.BlockSpec((ts, 2 * CF), lambda i, j: (i, j))],
        out_shape=[jax.ShapeDtypeStruct((s, DFF), BF16), jax.ShapeDtypeStruct((s, 2 * DFF), BF16)],
        compiler_params=_params("parallel", "parallel"),
    )(up, up, cw, cb)


def _loss_head(x1, ffn, tgt, gate2, *, ts, name):
    s = x1.shape[0]
    n = s // ts
    row = pl.BlockSpec((ts, D), lambda i: (i, 0))
    acc8 = pl.BlockSpec((SUBLANES, D), lambda i: (0, 0))

    def body(x_ref, f_ref, t_ref, g_ref, dy_ref, df_ref, dg_ref, loss_ref, lacc):
        i = pl.program_id(0)

        @pl.when(i == 0)
        def _():
            lacc[...] = jnp.zeros_like(lacc)
            dg_ref[...] = jnp.zeros_like(dg_ref)

        f = f_ref[...]
        diff = x_ref[...] + g_ref[...] * f - t_ref[...]
        lacc[...] += _colsum8(diff * diff)
        dy = diff * (1.0 / D)
        dy_ref[...] = dy
        df_ref[...] = (dy * g_ref[...]).astype(BF16)
        dg_ref[...] += _colsum8(dy * f)

        @pl.when(i == n - 1)
        def _():
            loss_ref[...] = jnp.full((SUBLANES, LANES), (0.5 / D) * jnp.sum(lacc[...]), F32)

    return pl.pallas_call(
        body, name=name, grid=(n,), in_specs=[row, row, row, _vec(D)],
        out_specs=[row, row, acc8, _full((SUBLANES, LANES))],
        out_shape=[jax.ShapeDtypeStruct((s, D), F32), jax.ShapeDtypeStruct((s, D), BF16),
                   jax.ShapeDtypeStruct((SUBLANES, D), F32), jax.ShapeDtypeStruct((SUBLANES, LANES), F32)],
        scratch_shapes=[pltpu.VMEM((SUBLANES, D), F32)], compiler_params=_params("arbitrary"),
    )(x1, ffn, tgt, gate2)


def _gate_bwd(da, conv, *, ts, name):
    s = conv.shape[0]

    def body(da_ref, c_ref, d_ref, db_ref):
        i = pl.program_id(1)

        @pl.when(i == 0)
        def _():
            db_ref[...] = jnp.zeros_like(db_ref)

        gt, vl = c_ref[:, :CF].astype(F32), c_ref[:, CF:].astype(F32)
        sg = 1.0 / (1.0 + jnp.exp(-gt))
        dav = da_ref[...].astype(F32)
        dgt = dav * vl * (sg * (1.0 + gt * (1.0 - sg)))
        dvl = dav * (gt * sg)
        d_ref[:, :CF] = dgt.astype(BF16)
        d_ref[:, CF:] = dvl.astype(BF16)
        db_ref[:, :CF] += _colsum8(dgt)
        db_ref[:, CF:] += _colsum8(dvl)

    return pl.pallas_call(
        body, name=name, grid=(2, s // ts),
        in_specs=[pl.BlockSpec((ts, CF), lambda j, i: (i, j)),
                  pl.BlockSpec((ts, 2 * CF), lambda j, i: (i, j))],
        out_specs=[pl.BlockSpec((ts, 2 * CF), lambda j, i: (i, j)),
                   pl.BlockSpec((SUBLANES, 2 * CF), lambda j, i: (0, j))],
        out_shape=[jax.ShapeDtypeStruct((s, 2 * DFF), BF16), jax.ShapeDtypeStruct((SUBLANES, 2 * DFF), F32)],
        compiler_params=_params("parallel", "arbitrary"),
    )(da, conv)


def _conv_bwd(dc, up, cw, *, ts, tc, name):
    s = up.shape[0]
    hb = ts // HALO
    nb = s // HALO

    def body(d_ref, dn_ref, u_ref, w_ref, du_ref, dw_ref):
        i = pl.program_id(1)
        n = s // ts

        @pl.when(i == 0)
        def _():
            dw_ref[...] = jnp.zeros_like(dw_ref)

        dcur = d_ref[...].astype(F32)
        nxt = jnp.where(i < n - 1, dn_ref[...].astype(F32), 0.0)
        de = jnp.concatenate([dcur, nxt], axis=0)
        d1 = _shift_rows(de, -1)[:ts]
        d2 = _shift_rows(de, -2)[:ts]
        du_ref[...] = (w_ref[2:3, :] * dcur + w_ref[1:2, :] * d1 + w_ref[0:1, :] * d2).astype(BF16)
        u = u_ref[...].astype(F32)
        dw_ref[16:24, :] += _colsum8(dcur * u)
        dw_ref[8:16, :] += _colsum8(d1 * u)
        dw_ref[0:8, :] += _colsum8(d2 * u)

    return pl.pallas_call(
        body, name=name, grid=(2 * DFF // tc, s // ts),
        in_specs=[pl.BlockSpec((ts, tc), lambda j, i: (i, j)),
                  pl.BlockSpec((HALO, tc), lambda j, i: (jnp.minimum((i + 1) * hb, nb - 1), j)),
                  pl.BlockSpec((ts, tc), lambda j, i: (i, j)),
                  pl.BlockSpec((3, tc), lambda j, i: (0, j))],
        out_specs=[pl.BlockSpec((ts, tc), lambda j, i: (i, j)), pl.BlockSpec((24, tc), lambda j, i: (0, j))],
        out_shape=[jax.ShapeDtypeStruct((s, 2 * DFF), BF16), jax.ShapeDtypeStruct((24, 2 * DFF), F32)],
        compiler_params=_params("parallel", "arbitrary"),
    )(dc, dc, up, cw)


MXU_COLS = 256


def _sub_chunks(width):
    return [(c0, min(MXU_COLS, width - c0)) for c0 in range(0, width, MXU_COLS)]


def _up_conv_gate(h2, w_up, cw, cb, *, tm, name):
    s = h2.shape[0]
    hb = tm // HALO

    def body(a_ref, ah_ref, w_ref, cw_ref, cb_ref, up_ref, c_ref, act_ref):
        i = pl.program_id(1)
        ext = jnp.concatenate([ah_ref[...], a_ref[...]], axis=0)
        live_halo = i > 0
        for c0, cwid in _sub_chunks(CF):
            conv = []
            for off in (c0, CF + c0):
                cols = slice(off, off + cwid)
                u = _dot(ext, w_ref[cols, :], NT)
                up_ref[:, cols] = u[HALO:].astype(BF16)
                row = lax.broadcasted_iota(jnp.int32, (HALO + tm, 1), 0)
                ue = jnp.where(jnp.logical_or(row >= HALO, live_halo), u, 0.0)
                y = cw_ref[2:3, cols] * ue + cw_ref[1:2, cols] * _shift_rows(ue, 1) + cw_ref[0:1, cols] * _shift_rows(ue, 2)
                cv = y[HALO:] + cb_ref[:, cols]
                c_ref[:, cols] = cv.astype(BF16)
                conv.append(cv)
            gt, vl = conv
            act_ref[:, c0:c0 + cwid] = (gt / (1.0 + jnp.exp(-gt)) * vl).astype(BF16)

    return pl.pallas_call(
        body, name=name, grid=(2, s // tm),
        in_specs=[pl.BlockSpec((tm, D), lambda j, i: (i, 0)),
                  pl.BlockSpec((HALO, D), lambda j, i: (jnp.maximum(i * hb - 1, 0), 0)),
                  pl.BlockSpec((2 * CF, D), lambda j, i: (j, 0)),
                  pl.BlockSpec((3, 2 * CF), lambda j, i: (0, j)), pl.BlockSpec((1, 2 * CF), lambda j, i: (0, j))],
        out_specs=[pl.BlockSpec((tm, 2 * CF), lambda j, i: (i, j)), pl.BlockSpec((tm, 2 * CF), lambda j, i: (i, j)),
                   pl.BlockSpec((tm, CF), lambda j, i: (i, j))],
        out_shape=[jax.ShapeDtypeStruct((s, 2 * DFF), BF16), jax.ShapeDtypeStruct((s, 2 * DFF), BF16),
                   jax.ShapeDtypeStruct((s, DFF), BF16)],
        compiler_params=_params("parallel", "parallel"),
    )(h2, h2, w_up, cw, cb)


def _down_bwd_gate(dffn, w_down, conv, *, tm, name):
    s = dffn.shape[0]

    def body(a_ref, w_ref, c_ref, d_ref, db_ref):
        i = pl.program_id(1)

        @pl.when(i == 0)
        def _():
            db_ref[...] = jnp.zeros_like(db_ref)

        a = a_ref[...]
        for c0, cwid in _sub_chunks(CF):
            gcols, vcols = slice(c0, c0 + cwid), slice(CF + c0, CF + c0 + cwid)
            da = _dot(a, w_ref[gcols, :], NT)
            gt, vl = c_ref[:, gcols].astype(F32), c_ref[:, vcols].astype(F32)
            sg = 1.0 / (1.0 + jnp.exp(-gt))
            dgt = da * vl * (sg * (1.0 + gt * (1.0 - sg)))
            dvl = da * (gt * sg)
            d_ref[:, gcols] = dgt.astype(BF16)
            d_ref[:, vcols] = dvl.astype(BF16)
            db_ref[:, gcols] += _colsum8(dgt)
            db_ref[:, vcols] += _colsum8(dvl)

    return pl.pallas_call(
        body, name=name, grid=(2, s // tm),
        in_specs=[pl.BlockSpec((tm, D), lambda j, i: (i, 0)), pl.BlockSpec((CF, D), lambda j, i: (j, 0)),
                  pl.BlockSpec((tm, 2 * CF), lambda j, i: (i, j))],
        out_specs=[pl.BlockSpec((tm, 2 * CF), lambda j, i: (i, j)), pl.BlockSpec((SUBLANES, 2 * CF), lambda j, i: (0, j))],
        out_shape=[jax.ShapeDtypeStruct((s, 2 * DFF), BF16), jax.ShapeDtypeStruct((SUBLANES, 2 * DFF), F32)],
        compiler_params=_params("parallel", "arbitrary"),
    )(dffn, w_down, conv)


def _conv_bwd_up_bwd(dc, up, cw, w_up, *, tm, name):
    s = up.shape[0]
    hb = tm // HALO
    nb = s // HALO
    nk = 2 * DFF // CF
    n = s // tm

    def body(d_ref, dn_ref, u_ref, cw_ref, w_ref, du_ref, dh_ref, dw_ref, acc, dwacc):
        i, k = pl.program_id(0), pl.program_id(1)

        @pl.when(jnp.logical_and(i == 0, k == 0))
        def _():
            dwacc[...] = jnp.zeros_like(dwacc)

        @pl.when(k == 0)
        def _():
            acc[...] = jnp.zeros_like(acc)

        live_next = i < n - 1
        part = None
        for c0, cwid in _sub_chunks(CF):
            cols = slice(c0, c0 + cwid)
            dcur = d_ref[:, cols].astype(F32)
            de = jnp.concatenate([dcur, jnp.where(live_next, dn_ref[:, cols].astype(F32), 0.0)], axis=0)
            d1 = _shift_rows(de, -1)[:tm]
            d2 = _shift_rows(de, -2)[:tm]
            du = (cw_ref[2:3, cols] * dcur + cw_ref[1:2, cols] * d1 + cw_ref[0:1, cols] * d2).astype(BF16)
            du_ref[:, cols] = du
            prod = _dot(du, w_ref[cols, :])
            part = prod if part is None else part + prod
            u = u_ref[:, cols].astype(F32)
            for tap, dsh in ((2, dcur), (1, d1), (0, d2)):
                dwacc[k, SUBLANES * tap:SUBLANES * (tap + 1), cols] += _colsum8(dsh * u)
        acc[...] += part

        @pl.when(k == nk - 1)
        def _():
            dh_ref[...] = acc[...].astype(dh_ref.dtype)

        @pl.when(jnp.logical_and(i == n - 1, k == nk - 1))
        def _():
            dw_ref[...] = dwacc[...]

    res = pl.pallas_call(
        body, name=name, grid=(n, nk),
        in_specs=[pl.BlockSpec((tm, CF), lambda i, k: (i, k)),
                  pl.BlockSpec((HALO, CF), lambda i, k: (jnp.minimum((i + 1) * hb, nb - 1), k)),
                  pl.BlockSpec((tm, CF), lambda i, k: (i, k)),
                  pl.BlockSpec((3, CF), lambda i, k: (0, k)),
                  pl.BlockSpec((CF, D), lambda i, k: (k, 0))],
        out_specs=[pl.BlockSpec((tm, CF), lambda i, k: (i, k)), pl.BlockSpec((tm, D), lambda i, k: (i, 0)),
                   _full((nk, 24, CF))],
        out_shape=[jax.ShapeDtypeStruct((s, 2 * DFF), BF16), jax.ShapeDtypeStruct((s, D), BF16),
                   jax.ShapeDtypeStruct((nk, 24, CF), F32)],
        scratch_shapes=[pltpu.VMEM((tm, D), F32), pltpu.VMEM((nk, 24, CF), F32)],
        compiler_params=_params("arbitrary", "arbitrary"),
    )(dc, dc, up, cw, w_up)
    return res[0], res[1], jnp.transpose(res[2], (1, 0, 2)).reshape(24, 2 * DFF)


def _ln_mod_bwd(dh, xin, g, scale, resid, extra, gate, *, ts, name, side=None):
    s = xin.shape[0]
    row = pl.BlockSpec((ts, D), lambda i: (i, 0))
    acc8 = pl.BlockSpec((SUBLANES, D), lambda i: (0, 0))
    with_gate = extra is not None

    def body(*refs):
        if with_gate:
            dh_ref, x_ref, g_ref, sc_ref, r_ref, e_ref, gt_ref, dx_ref, da_ref, dsh, dsc, dg, dgt = refs
        else:
            dh_ref, x_ref, g_ref, sc_ref, r_ref, dx_ref, dsh, dsc, dg = refs
        i = pl.program_id(0)

        @pl.when(i == 0)
        def _():
            for acc in (dsh, dsc, dg) + ((dgt,) if with_gate else ()):
                acc[...] = jnp.zeros_like(acc)

        xv, dhv = x_ref[...], dh_ref[...].astype(F32)
        r = lax.rsqrt(jnp.mean(xv * xv, axis=-1, keepdims=True) + EPS)
        xn = xv * r
        dsh[...] += _colsum8(dhv)
        dsc[...] += _colsum8(dhv * (xn * g_ref[...]))
        dhp = dhv * (1.0 + sc_ref[...])
        dg[...] += _colsum8(dhp * xn)
        dxn = dhp * g_ref[...]
        dx = r_ref[...] + r * (dxn - xn * jnp.mean(dxn * xn, axis=-1, keepdims=True))
        dx_ref[...] = dx
        if with_gate:
            da_ref[...] = (dx * gt_ref[...]).astype(BF16)
            dgt[...] += _colsum8(dx * e_ref[...])

    f32o, p8 = jax.ShapeDtypeStruct((s, D), F32), jax.ShapeDtypeStruct((SUBLANES, D), F32)
    if with_gate:
        ins, in_specs = (dh, xin, g, scale, resid, extra, gate), [row, row, _vec(D), _vec(D), row, row, _vec(D)]
        out_specs, out_shape = [row, row, acc8, acc8, acc8, acc8], [f32o, jax.ShapeDtypeStruct((s, D), BF16), p8, p8, p8, p8]
    else:
        ins, in_specs = (dh, xin, g, scale, resid), [row, row, _vec(D), _vec(D), row]
        out_specs, out_shape = [row, acc8, acc8, acc8], [f32o, p8, p8, p8]
    return _row_call(body, side, name=name, steps=s // ts, in_specs=in_specs, out_specs=out_specs,
                     out_shape=out_shape, ins=ins)


def _group_norm_bwd(t, dn_out, gvec, bd):
    r = _group_rsqrt(t, bd)
    dg_terms = dn_out * t * r
    dn = dn_out * gvec
    dt = r * (dn - t * (r * r) * (_split_dot(dn * t, bd) * (1.0 / HD)))
    return dt, dg_terms


def _mix_bwd(dmix, proj, o, pw, pb, ps, ag, bd, *, ts, name, side=None):
    s = proj.shape[0]
    hb = ts // HALO
    nb = s // HALO

    def body(dm_ref, dmn_ref, u_ref, uh_ref, o_ref, pw_ref, pb_ref, ps_ref, ag_ref, bd_ref,
             du_ref, do_ref, dpw_ref, dpb_ref, dps_ref, dag_ref):
        i = pl.program_id(0)
        n = s // ts

        @pl.when(i == 0)
        def _():
            for acc in (dpw_ref, dpb_ref, dps_ref, dag_ref):
                acc[...] = jnp.zeros_like(acc)

        for g, w in enumerate(POOL_WINDOWS):
            cols = slice(g * LANES, (g + 1) * LANES)
            wg = pw_ref[g]
            psg = ps_ref[:, cols]
            pooled = _pooled(u_ref[:, cols], uh_ref[:, cols], i, g, w, ts).astype(BF16)
            dy = dm_ref[:, cols].astype(F32)
            dps_ref[:, cols] += _colsum8(dy * (_dot(pooled, wg) + pb_ref[:, cols]))
            dpre = dy * psg
            dpb_ref[:, cols] += _colsum8(dpre)
            dpreb = dpre.astype(BF16)
            dpw_ref[g * LANES:(g + 1) * LANES, :] += _dot(pooled, dpreb, TN)
            dpool = _dot(dpreb, wg, NT)
            dnext = _dot((dmn_ref[:, cols].astype(F32) * psg).astype(BF16), wg, NT)
            dpe = jnp.concatenate([dpool, jnp.where(i < n - 1, dnext, 0.0)], axis=0)
            tpos = i * ts + lax.broadcasted_iota(jnp.int32, (ts + HALO, 1), 0)
            acc = dpe / jnp.minimum(tpos + 1, w).astype(F32)
            span = 1
            while span < w:
                acc = acc + _shift_rows(acc, -span)
                span *= 2
            du_ref[:, cols] = acc[:ts] - dpool
        ov = o_ref[...]
        dov, dg_terms = _group_norm_bwd(ov, dm_ref[:, DP:].astype(F32), ag_ref[...], bd_ref[...])
        do_ref[...] = dov
        dag_ref[...] += _colsum8(dg_terms)

    p8 = jax.ShapeDtypeStruct((SUBLANES, DP), F32)
    acc8 = pl.BlockSpec((SUBLANES, DP), lambda i: (0, 0))
    half = pl.BlockSpec((ts, DP), lambda i: (i, 0))
    return _row_call(
        body, side, name=name, steps=s // ts,
        in_specs=[pl.BlockSpec((ts, D), lambda i: (i, 0)),
                  pl.BlockSpec((HALO, DP), lambda i: (jnp.minimum((i + 1) * hb, nb - 1), 0)),
                  half, pl.BlockSpec((HALO, DP), lambda i: (jnp.maximum(i * hb - 1, 0), 0)),
                  half, _full((4, LANES, LANES)), _vec(DP), _vec(DP), _vec(DA), _full((DA, DA))],
        out_specs=[half, half, _full((DP, LANES)), acc8, acc8, acc8],
        out_shape=[jax.ShapeDtypeStruct((s, DP), F32), jax.ShapeDtypeStruct((s, DA), F32),
                   jax.ShapeDtypeStruct((DP, LANES), F32), p8, p8, p8],
        ins=(dmix, dmix, proj, proj, o, pw, pb, ps, ag, bd))


def _qk_norm_bwd(du, dq, dk, dv, proj, qg, kg, bd, *, ts, name):
    s = proj.shape[0]

    def body(du_ref, dq_ref, dk_ref, dv_ref, q_ref, k_ref, qg_ref, kg_ref, bd_ref, dp_ref, dqg_ref, dkg_ref):
        i = pl.program_id(0)

        @pl.when(i == 0)
        def _():
            dqg_ref[...] = jnp.zeros_like(dqg_ref)
            dkg_ref[...] = jnp.zeros_like(dkg_ref)

        bdv = bd_ref[...]
        dqr, tq = _group_norm_bwd(q_ref[...], dq_ref[...], qg_ref[...], bdv)
        dkr, tk = _group_norm_bwd(k_ref[...], dk_ref[...], kg_ref[...], bdv)
        dqg_ref[...] += _colsum8(tq)
        dkg_ref[...] += _colsum8(tk)
        dp_ref[:, 0:DP] = du_ref[...].astype(BF16)
        dp_ref[:, DP:DP + DA] = dqr.astype(BF16)
        dp_ref[:, DP + DA:DP + 2 * DA] = dkr.astype(BF16)
        dp_ref[:, DP + 2 * DA:] = dv_ref[...].astype(BF16)

    half = pl.BlockSpec((ts, DA), lambda i: (i, 0))
    col = lambda j: pl.BlockSpec((ts, DA), lambda i: (i, j))
    acc8 = pl.BlockSpec((SUBLANES, DA), lambda i: (0, 0))
    p8 = jax.ShapeDtypeStruct((SUBLANES, DA), F32)
    return pl.pallas_call(
        body, name=name, grid=(s // ts,),
        in_specs=[half, half, half, half, col(1), col(2), _vec(DA), _vec(DA), _full((DA, DA))],
        out_specs=[pl.BlockSpec((ts, DIN), lambda i: (i, 0)), acc8, acc8],
        out_shape=[jax.ShapeDtypeStruct((s, DIN), BF16), p8, p8],
        compiler_params=_params("arbitrary"),
    )(du, dq, dk, dv, proj, proj, qg, kg, bd)


def _split3(a):
    hi = a.astype(BF16)
    return hi, (a - hi.astype(F32)).astype(BF16)


def _dot3(a, b, dn):
    ah, al = _split3(a)
    bh, bl = _split3(b)
    return _dot(ah, bh, dn) + (_dot(ah, bl, dn) + _dot(al, bh, dn))


def _ada_fwd(c_all, w, b, name):
    nw = w.shape[1]

    def body(c_ref, w_ref, b_ref, o_ref):
        cv = c_ref[...]
        act = cv / (1.0 + jnp.exp(-cv))
        o_ref[...] = _dot3(act, w_ref[...], NN) + b_ref[...]

    return pl.pallas_call(
        body, name=name, in_specs=[_full((NDEV, D)), _full(w.shape), _full((1, nw))], out_specs=_full((NDEV, nw)),
        out_shape=jax.ShapeDtypeStruct((NDEV, nw), F32), grid=(1,), compiler_params=_params("arbitrary"),
    )(c_all, w, b)


def _ada_bwd(c_all, dmod, name):
    nw = dmod.shape[1]

    def body(c_ref, d_ref, o_ref):
        cv = c_ref[...]
        act = cv / (1.0 + jnp.exp(-cv))
        o_ref[...] = _dot3(act, d_ref[...], TN)[None]

    return pl.pallas_call(
        body, name=name, in_specs=[_full((NDEV, D)), _full((NDEV, nw))], out_specs=_full((1, D, nw)),
        out_shape=jax.ShapeDtypeStruct((1, D, nw), F32), grid=(1,), compiler_params=_params("arbitrary"),
    )(c_all, dmod)


def _fold_heads(v):
    acc = v[:, 0:HD]
    for h in range(1, DA // HD):
        acc = acc + v[:, h * HD:(h + 1) * HD]
    return acc


def _pack_partials(pieces, dcw_p, name):
    n_p = len(pieces)
    total = sum(p.shape[1] for p in pieces) + 3 * dcw_p.shape[1]
    npack = -(-total // (SUBLANES * LANES)) * (SUBLANES * LANES)

    def body(*refs):
        out = refs[-1]
        off = 0
        for r in refs[:n_p]:
            out[:, off:off + r.shape[1]] = jnp.sum(r[...], axis=0, keepdims=True)
            off += r.shape[1]
        dw = refs[n_p]
        for tap in range(3):
            out[:, off:off + dw.shape[1]] = jnp.sum(dw[SUBLANES * tap:SUBLANES * (tap + 1), :], axis=0, keepdims=True)
            off += dw.shape[1]
        if off < npack:
            out[:, off:] = jnp.zeros((1, npack - off), F32)

    arrs = list(pieces) + [dcw_p]
    return pl.pallas_call(
        body, name=name, grid=(1,), in_specs=[_full(a.shape) for a in arrs], out_specs=_full((1, npack)),
        out_shape=jax.ShapeDtypeStruct((1, npack), F32), compiler_params=_params("arbitrary"),
    )(*arrs)


def _small_update(gathered, gathered_pw, gathered_cw, specs, params, loss_off, name):
    names = [sp[0] for sp in specs]
    flat = []
    for nme in names + ["pool_w", "conv_w"]:
        flat += list(params[nme])
    n_in = len(flat)

    def body(*refs):
        ga_ref, gp_ref, gc_ref = refs[0], refs[1], refs[2]
        prm = refs[3:3 + n_in]
        outs = refs[3 + n_in:]
        total = ga_ref[0:1, :]
        for dv in range(1, NDEV):
            total = total + ga_ref[dv:dv + 1, :]
        k = 0
        for idx, (nme, off, width, fold) in enumerate(specs):
            g = total[:, off:off + width]
            if fold:
                g = _fold_heads(g)
            w_ref, m_ref, v_ref = prm[3 * idx:3 * idx + 3]
            d, nm, nv = _adamw_math(w_ref[...], g, m_ref[...], v_ref[...])
            for val in (g, d, nm, nv):
                outs[k][...] = val
                k += 1
        gpw = gp_ref[0]
        for dv in range(1, NDEV):
            gpw = gpw + gp_ref[dv]
        w_ref, m_ref, v_ref = prm[3 * len(specs):3 * len(specs) + 3]
        d, nm, nv = _adamw_math(w_ref[...], gpw, m_ref[...], v_ref[...])
        for val in (gpw, d, nm, nv):
            outs[k][...] = val
            k += 1
        gcw = gc_ref[0]
        for dv in range(1, NDEV):
            gcw = gcw + gc_ref[dv]
        w_ref, m_ref, v_ref = prm[3 * len(specs) + 3:3 * len(specs) + 6]
        d, nm, nv = _adamw_math(w_ref[...], gcw, m_ref[...], v_ref[...])
        for val in (gcw, d, nm, nv):
            outs[k][...] = val
            k += 1
        outs[k][...] = ga_ref[:, 0:6 * D]
        outs[k + 1][...] = total[:, loss_off:loss_off + LANES] * (1.0 / SUBLANES)

    out_shape, out_specs = [], []
    for nme in names + ["pool_w", "conv_w"]:
        shp = params[nme][0].shape
        out_shape += [jax.ShapeDtypeStruct(shp, F32)] * 4
        out_specs += [_full(shp)] * 4
    out_shape += [jax.ShapeDtypeStruct((NDEV, 6 * D), F32), jax.ShapeDtypeStruct((1, LANES), F32)]
    out_specs += [_full((NDEV, 6 * D)), _full((1, LANES))]
    res = pl.pallas_call(
        body, name=name, grid=(1,),
        in_specs=[_full(gathered.shape), _full(gathered_pw.shape), _full(gathered_cw.shape)] + [_full(a.shape) for a in flat],
        out_specs=out_specs, out_shape=out_shape, compiler_params=_params("arbitrary"),
    )(gathered, gathered_pw, gathered_cw, *flat)
    out = {nme: tuple(res[4 * i:4 * i + 4]) for i, nme in enumerate(names + ["pool_w", "conv_w"])}
    return out, res[-2], res[-1][0, 0]


def _row_tile(s):
    return 512 if s % 512 == 0 else s


def kernel(x, c, ada_w, ada_b, norm1_g, w_in, pool_w, pool_b, pool_scale, q_norm_g, k_norm_g, attn_out_g, w_out, norm2_g, w_up, conv_w, conv_b, w_down, loss_target, m_ada_w, m_ada_b, m_norm1_g, m_w_in, m_pool_w, m_pool_b, m_pool_scale, m_q_norm_g, m_k_norm_g, m_attn_out_g, m_w_out, m_norm2_g, m_w_up, m_conv_w, m_conv_b, m_w_down, v_ada_w, v_ada_b, v_norm1_g, v_w_in, v_pool_w, v_pool_b, v_pool_scale, v_q_norm_g, v_k_norm_g, v_attn_out_g, v_w_out, v_norm2_g, v_w_up, v_conv_w, v_conv_b, v_w_down):
    ax, ay, ac = lax.axis_index("x"), lax.axis_index("y"), lax.axis_index("c")
    me = 4 * ax + 2 * ay + ac
    me_swapped = 4 * ay + 2 * ax + ac
    xs, tgt = x[0], loss_target[0]
    s = xs.shape[0]
    ts = _row_tile(s)
    tq_attn, tk_attn, hp_attn = 256, 256, 2
    tmm = 2 * ts
    bd = _block_diag_ones(DA, HD)

    w_in_t = w_in[0].T.astype(BF16)
    w_up_t = w_up[0].T.astype(BF16)
    c_all = _all_gather([jnp.broadcast_to(c, (SUBLANES, D))], [False], "gather_c")[0][:, 0, :]
    n_ada = ada_w.shape[2]
    ada_b_mine = lax.dynamic_slice_in_dim(ada_b, me * n_ada, n_ada, axis=1)
    mod_part = _ada_fwd(c_all, ada_w[0], ada_b_mine, "ada_fwd")
    mod_all = _all_gather([mod_part], [False], "gather_mod")[0]
    mod = lax.dynamic_index_in_dim(mod_all, me, axis=1, keepdims=False).reshape(1, 6 * D)
    shift1, scale1, gate1, shift2, scale2, gate2 = [mod[:, k * D:(k + 1) * D] for k in range(6)]

    later_w = [w_out[0].astype(BF16), w_up_t, w_down[0].astype(BF16)]
    cb_full = jnp.transpose(conv_b.reshape(1, 2, 2, 2, 704), (0, 2, 1, 3, 4)).reshape(1, 2 * DFF)

    qg = jnp.tile(q_norm_g, (1, DA // HD))
    kg = jnp.tile(k_norm_g, (1, DA // HD))
    ag = attn_out_g.reshape(1, DA)
    pw = pool_w[0].astype(BF16)
    pb = pool_b.reshape(1, DP)
    h1, (gw_in, gcw) = _ln_mod(xs, norm1_g, scale1, shift1, ts=ts, name="ln1",
                               side=_gather_side([w_in_t, jnp.pad(conv_w[0], ((0, 5), (0, 64)))], [False, True]))
    w_in_full = gw_in.reshape(DIN, D)
    cw_full = jnp.transpose(gcw[:, :3, :704], (1, 0, 2)).reshape(3, 2 * DFF)
    proj = _matmul(h1, w_in_full, mode="nt", out_dtype=F32, tm=tmm,tn=DIN, tk=D, name="in_proj")
    qkv = _qk_norm(proj, qg, kg, bd, ts=ts, name="qk_norm")
    o_raw, m_tot, kb_first, (gw_out, gw_up, gw_down) = _attn_fwd(
        qkv, later_w, [False, True, False], tq=tq_attn, tk=tk_attn, hp=hp_attn, name="attn_fwd")
    w_out_full = gw_out.reshape(D, D)
    w_up_full = gw_up.reshape(2 * DFF, D)
    w_down_full = gw_down.reshape(DFF, D)
    mix = _pool_mix(proj, o_raw, pw, pb, pool_scale, ag, bd, ts=ts, name="pool_mix")
    att, x1, h2 = _proj_res_ln_mod(mix, w_out_full, xs, gate1, norm2_g, scale2, shift2, tm=ts, name="out_proj_ln2")
    up, conv, act = _up_conv_gate(h2, w_up_full, cw_full, cb_full, tm=ts, name="up_conv_gate")
    dy, dffn, dgate2_p, loss_p = _proj_loss_head(act, w_down_full, x1, tgt, gate2, tm=ts, name="down_proj_loss")

    g_w_down = _matmul(act, dffn, mode="tn", out_dtype=F32, tm=CF, tn=D, tk=tmm,name="down_wgrad")
    dconv, dcb_p = _down_bwd_gate(dffn, w_down_full, conv, tm=tmm, name="down_bwd_gate")
    dup, dh2, dcw_p = _conv_bwd_up_bwd(dconv, up, cw_full, w_up_full, tm=tmm, name="conv_bwd_up_bwd")
    g_w_up_t = _matmul(dup, h2, mode="tn", out_dtype=F32, tm=CF, tn=D, tk=tmm,name="up_wgrad")
    (dx1, datt, dshift2_p, dscale2_p, dnorm2_p, dgate1_p), _ = _ln_mod_bwd(
        dh2, x1, norm2_g, scale2, dy, att, gate1, ts=ts, name="ln2_bwd")

    dmix = _matmul(datt, w_out_full, mode="nt", out_dtype=BF16, tm=tmm,tn=D, tk=D, name="out_bwd")
    g_w_out = _matmul(mix, datt, mode="tn", out_dtype=F32, tm=D, tn=D, tk=tmm,name="out_wgrad")
    core = jnp.reshape(ac, (1,)).astype(jnp.int32)
    chip = jnp.reshape(2 * ax + ay, (1,)).astype(jnp.int32)
    big_ffn = [g_w_up_t.reshape(NDEV, 2 * DFF // NDEV, D), g_w_down.reshape(NDEV, DFF // NDEV, D),
               g_w_out.reshape(NDEV, D // NDEV, D)]
    swaps_ffn = [True, False, False]
    (du, do_raw, g_pw_p, dpb_p, dps_p, dag_p), gots_ffn = _mix_bwd(
        dmix, proj, o_raw, pw, pb, pool_scale, ag, bd, ts=ts, name="mix_bwd", side=_pair_side(big_ffn, swaps_ffn))
    sums_ffn = [_pair_sum(big_ffn[k], gots_ffn[k], swaps_ffn[k], core, "rs_pair_sum_ffn%d" % k) for k in range(3)]
    dqn, dkn, dvv, parts_ffn = _attn_bwd(qkv, do_raw, m_tot, kb_first, sums_ffn, tq=tq_attn, tk=tk_attn, hp=hp_attn, name="attn_bwd")
    dproj, dqg_p, dkg_p = _qk_norm_bwd(du, dqn, dkn, dvv, proj, qg, kg, bd, ts=ts, name="qk_norm_bwd")
    g_w_in_t = _matmul(dproj, h1, mode="tn", out_dtype=F32, tm=DIN // 2, tn=D, tk=tmm,name="in_wgrad")
    big = [g_w_in_t.reshape(NDEV, DIN // NDEV, D)]
    gots = _pair_exchange(big, [False], "rs_pair")
    sums = [_pair_sum(big[0], gots[0], False, core, "rs_pair_sum")]
    dh1, parts = _matmul(dproj, w_in_full, mode="nn", out_dtype=BF16, tm=tmm,tn=D, tk=DIN, name="in_bwd",
                         side=_chip_side(sums))
    (grad_x, dshift1_p, dscale1_p, dnorm1_p), _ = _ln_mod_bwd(
        dh1, xs, norm1_g, scale1, dx1, None, None, ts=ts, name="ln1_bwd")

    tr = lambda a: a[0].T
    r_in = _adamw_reduce(tr(w_in), tr(m_w_in), tr(v_w_in), sums[0], parts[0], chip, "adamw_w_in")
    r_out = _adamw_reduce(w_out[0], m_w_out[0], v_w_out[0], sums_ffn[2], parts_ffn[2], chip, "adamw_w_out")
    r_up = _adamw_reduce(tr(w_up), tr(m_w_up), tr(v_w_up), sums_ffn[0], parts_ffn[0], chip, "adamw_w_up")
    r_down = _adamw_reduce(w_down[0], m_w_down[0], v_w_down[0], sums_ffn[1], parts_ffn[1], chip, "adamw_w_down")
    r_in = [a.T[None] for a in r_in]
    r_up = [a.T[None] for a in r_up]
    r_out = [a[None] for a in r_out]
    r_down = [a[None] for a in r_down]

    dcb_nat = jnp.transpose(dcb_p.reshape(SUBLANES, 2, 2, 2, 704), (0, 2, 1, 3, 4)).reshape(SUBLANES, 2 * DFF)
    pieces = [dshift1_p, dscale1_p, dgate1_p, dshift2_p, dscale2_p, dgate2_p,
              dnorm1_p, dnorm2_p, dcb_nat, dpb_p, dps_p, dag_p, dqg_p, dkg_p, loss_p]
    n_vec = sum(p.shape[1] for p in pieces)
    packed = _pack_partials(pieces, dcw_p, "pack_partials")
    npack = packed.shape[1]
    gathered, gathered_pw = _all_gather([packed.reshape(SUBLANES, npack // SUBLANES), g_pw_p], [False, False], "gather_small")
    gathered = gathered.reshape(NDEV, npack)
    gathered_cw = lax.dynamic_index_in_dim(
        gathered[:, n_vec:n_vec + 6 * DFF].reshape(NDEV, 3, NDEV, 704), me_swapped, axis=2, keepdims=False)
    specs = [("ada_b", 0, 6 * D, False)]
    off = 6 * D
    for nme, width, fold in (("norm1_g", D, False), ("norm2_g", D, False), ("conv_b", 2 * DFF, False),
                             ("pool_b", DP, False), ("pool_scale", DP, False), ("attn_out_g", DA, False),
                             ("q_norm_g", DA, True), ("k_norm_g", DA, True)):
        specs.append((nme, off, width, fold))
        off += width
    small = {
        "ada_b": (ada_b, m_ada_b, v_ada_b),
        "norm1_g": (norm1_g, m_norm1_g, v_norm1_g), "norm2_g": (norm2_g, m_norm2_g, v_norm2_g),
        "conv_b": (conv_b, m_conv_b, v_conv_b),
        "pool_b": (pb, m_pool_b.reshape(1, DP), v_pool_b.reshape(1, DP)),
        "pool_scale": (pool_scale, m_pool_scale, v_pool_scale),
        "attn_out_g": (ag, m_attn_out_g.reshape(1, DA), v_attn_out_g.reshape(1, DA)),
        "q_norm_g": (q_norm_g, m_q_norm_g, v_q_norm_g), "k_norm_g": (k_norm_g, m_k_norm_g, v_k_norm_g),
        "pool_w": (pool_w.reshape(DP, LANES), m_pool_w.reshape(DP, LANES), v_pool_w.reshape(DP, LANES)),
        "conv_w": (conv_w[0], m_conv_w[0], v_conv_w[0]),
    }
    upd, dmod_all, loss = _small_update(gathered, gathered_pw, gathered_cw, specs, small, off, "small_update")
    g_ada_w = _ada_bwd(c_all, lax.dynamic_slice_in_dim(dmod_all, me * n_ada, n_ada, axis=1), "ada_bwd")
    r_ada = [g_ada_w] + [a[None] for a in _adamw(ada_w[0], m_ada_w[0], v_ada_w[0], g_ada_w[0], "adamw_ada_w")]

    shapes = {"ada_b": ada_b.shape, "norm1_g": norm1_g.shape, "pool_w": pool_w.shape, "pool_b": pool_b.shape,
              "pool_scale": pool_scale.shape, "q_norm_g": q_norm_g.shape, "k_norm_g": k_norm_g.shape,
              "attn_out_g": attn_out_g.shape, "norm2_g": norm2_g.shape, "conv_w": conv_w.shape, "conv_b": conv_b.shape}
    res = {nme: [a.reshape(shapes[nme]) for a in upd[nme]] for nme in shapes}
    res.update(ada_w=r_ada, w_in=r_in, w_out=r_out, w_up=r_up, w_down=r_down)
    names = ["ada_w", "ada_b", "norm1_g", "w_in", "pool_w", "pool_b", "pool_scale", "q_norm_g", "k_norm_g",
             "attn_out_g", "w_out", "norm2_g", "w_up", "conv_w", "conv_b", "w_down"]
    outs = [loss, grad_x[None]]
    for q in range(4):
        outs += [res[nme][q] for nme in names]
    return tuple(outs)
```

```python
import functools
import math

import numpy as np
import jax
import jax.numpy as jnp
from jax import lax
from jax.experimental import pallas as pl
from jax.experimental.pallas import tpu as pltpu

F32, BF16 = jnp.float32, jnp.bfloat16
D = 1024
DP = 512
DA = 512
HD = 64
DIN = DP + 3 * DA
DFF = 2816
POOL_WINDOWS = (2, 4, 8, 16)
HALO = 16
EPS = 1e-6
LANES = 128
SUBLANES = 8
NDEV = 8
VMEM_LIMIT = 56 * 1024 * 1024
MESH = pl.DeviceIdType.MESH

ADAM_LR, ADAM_B1, ADAM_B2, ADAM_EPS, ADAM_WD, ADAM_STEP = 0.001, 0.9, 0.999, 1e-08, 0.01, 10

NN = (((1,), (0,)), ((), ()))
NT = (((1,), (1,)), ((), ()))
TN = (((0,), (0,)), ((), ()))


def _params(*sem):
    return pltpu.CompilerParams(dimension_semantics=sem, vmem_limit_bytes=VMEM_LIMIT)


def _full(shape):
    nd = len(shape)
    return pl.BlockSpec(shape, lambda *_: (0,) * nd)


def _dot(a, b, dn=NN):
    return lax.dot_general(a, b, dn, preferred_element_type=F32)


def _split_dot(a, b, dn=NN):
    hi = a.astype(BF16)
    lo = (a - hi.astype(F32)).astype(BF16)
    return _dot(hi, b, dn) + _dot(lo, b, dn)


def _colsum8(v):
    r, n = v.shape
    return v.reshape(r // SUBLANES, SUBLANES, n).sum(axis=0)


def _block_diag_ones(n, blk):
    i = np.arange(n) // blk
    return jnp.asarray((i[:, None] == i[None, :]).astype(np.float32), BF16)


def _matmul(a, b, *, mode, out_dtype, tm, tn, tk, name, n_outer=False, side=None):
    if mode == "tn":
        K, M = a.shape
        N = b.shape[1]
    elif mode == "nt":
        M, K = a.shape
        N = b.shape[0]
    else:
        M, K = a.shape
        N = b.shape[1]
    tm, tn, tk = min(tm, M), min(tn, N), min(tk, K)
    assert M % tm == 0 and N % tn == 0 and K % tk == 0, (name, M, N, K, tm, tn, tk)
    nk = K // tk
    dn = {"nn": NN, "nt": NT, "tn": TN}[mode]

    def body(a_ref, b_ref, o_ref, *acc):
        if nk == 1:
            o_ref[...] = _dot(a_ref[...], b_ref[...], dn).astype(o_ref.dtype)
            return
        acc_ref, = acc
        k = pl.program_id(2)

        @pl.when(k == 0)
        def _():
            acc_ref[...] = jnp.zeros_like(acc_ref)

        acc_ref[...] += _dot(a_ref[...], b_ref[...], dn)

        @pl.when(k == nk - 1)
        def _():
            o_ref[...] = acc_ref[...].astype(o_ref.dtype)

    if n_outer:
        gi = lambda g: (g[1], g[0], g[2])
        grid = (N // tn, M // tm, nk)
    else:
        gi = lambda g: g
        grid = (M // tm, N // tn, nk)

    def amap(*g):
        i, j, k = gi(g)
        return (k, i) if mode == "tn" else (i, k)

    def bmap(*g):
        i, j, k = gi(g)
        return (j, k) if mode == "nt" else (k, j)

    def omap(*g):
        i, j, k = gi(g)
        return (i, j)

    a_blk = (tk, tm) if mode == "tn" else (tm, tk)
    b_blk = (tn, tk) if mode == "nt" else (tk, tn)
    acc_scratch = [] if nk == 1 else [pltpu.VMEM((tm, tn), F32)]
    if side is None:
        return pl.pallas_call(
            body, name=name, grid=grid,
            in_specs=[pl.BlockSpec(a_blk, amap), pl.BlockSpec(b_blk, bmap)],
            out_specs=pl.BlockSpec((tm, tn), omap),
            out_shape=jax.ShapeDtypeStruct((M, N), out_dtype),
            scratch_shapes=acc_scratch,
            compiler_params=_params("parallel", "parallel", "arbitrary"),
        )(a, b)

    ne = len(side.arrs)
    steps = grid[0] * grid[1] * grid[2]

    nsem = len(side.scratch)

    def with_side(*refs):
        e_in, e_out = refs[2:2 + ne], refs[3 + ne:3 + 2 * ne]
        sems = refs[len(refs) - nsem:]
        step = (pl.program_id(0) * grid[1] + pl.program_id(1)) * grid[2] + pl.program_id(2)

        @pl.when(step == 0)
        def _():
            side.start(e_in, e_out, *sems)

        body(refs[0], refs[1], refs[2 + ne], *refs[3 + 2 * ne:len(refs) - nsem])

        @pl.when(step == steps - 1)
        def _():
            side.finish(e_in, e_out, *sems)

    any_spec = pl.BlockSpec(memory_space=pl.ANY)
    res = pl.pallas_call(
        with_side, name=name, grid=grid,
        in_specs=[pl.BlockSpec(a_blk, amap), pl.BlockSpec(b_blk, bmap)] + [any_spec] * ne,
        out_specs=[pl.BlockSpec((tm, tn), omap)] + [any_spec] * ne,
        out_shape=[jax.ShapeDtypeStruct((M, N), out_dtype)] + side.out_shapes,
        scratch_shapes=acc_scratch + side.scratch,
        compiler_params=_params("arbitrary", "arbitrary", "arbitrary"),
    )(a, b, *side.arrs)
    return res[0], list(res[1:])


def _slot(swap, px, py, pc):
    return 4 * py + 2 * px + pc if swap else 4 * px + 2 * py + pc


class _Gather:
    def __init__(self, ins, outs, send, recv, loc, swaps):
        self.ins, self.outs, self.send, self.recv, self.loc, self.swaps = ins, outs, send, recv, loc, swaps
        x, y, c = lax.axis_index("x"), lax.axis_index("y"), lax.axis_index("c")
        self.me, self.sib = (x, y, c), (x, y, 1 - c)
        self.chips = [(1 - x, y), (x, 1 - y), (1 - x, 1 - y)]
        self.n = len(ins)

    @staticmethod
    def scratch(n):
        return [pltpu.SemaphoreType.DMA((7 * n,)), pltpu.SemaphoreType.DMA((7 * n,)), pltpu.SemaphoreType.DMA((n,))]

    def copy(self, a, k, blk, to, src=None):
        rows = self.outs[a].at[_slot(self.swaps[a], *blk)]
        return pltpu.make_async_remote_copy(
            src_ref=rows if src is None else src, dst_ref=rows,
            send_sem=self.send.at[7 * a + k], recv_sem=self.recv.at[7 * a + k], device_id=to, device_id_type=MESH)

    def mine(self, a):
        return pltpu.make_async_copy(self.ins[a], self.outs[a].at[_slot(self.swaps[a], *self.me)], self.loc.at[a])

    def first(self, a):
        c = self.me[2]
        return [self.copy(a, 0, self.me, self.sib, src=self.ins[a])] + [
            self.copy(a, 1 + j, self.me, (*chip, c), src=self.ins[a]) for j, chip in enumerate(self.chips)]

    def forwards(self, a):
        c = self.me[2]
        return [self.copy(a, 4 + j, (*chip, c), self.sib) for j, chip in enumerate(self.chips)]

    def start(self):
        for a in range(self.n):
            self.mine(a).start()
        for a in range(self.n):
            for cp in self.first(a):
                cp.start()

    def forward(self):
        c = self.me[2]
        for a in range(self.n):
            fwd = self.forwards(a)
            for j, chip in enumerate(self.chips):
                self.copy(a, 1 + j, (*chip, c), self.me).wait_recv()
                fwd[j].start()

    def finish(self):
        c = self.me[2]
        for a in range(self.n):
            self.copy(a, 0, self.sib, self.me).wait_recv()
            for j, chip in enumerate(self.chips):
                self.copy(a, 4 + j, (*chip, 1 - c), self.me).wait_recv()
        for a in range(self.n):
            for cp in self.first(a) + self.forwards(a):
                cp.wait_send()
            self.mine(a).wait()


def _all_gather(arrs, swaps, name):
    n = len(arrs)

    def body(*refs):
        g = _Gather(refs[:n], refs[n:2 * n], *refs[2 * n:], swaps)
        g.start()
        g.forward()
        g.finish()

    any_spec = pl.BlockSpec(memory_space=pl.ANY)
    return pl.pallas_call(
        body, name=name,
        in_specs=[any_spec] * n, out_specs=[any_spec] * n,
        out_shape=[jax.ShapeDtypeStruct((NDEV,) + a.shape, a.dtype) for a in arrs],
        scratch_shapes=_Gather.scratch(n),
    )(*arrs)


def _pair_copies(ins, gots, send, recv, swaps):
    x, y, c = lax.axis_index("x"), lax.axis_index("y"), lax.axis_index("c")
    return [pltpu.make_async_remote_copy(
        src_ref=ins[a].at[_slot(swaps[a], k // 2, k % 2, 1 - c)], dst_ref=gots[a].at[k],
        send_sem=send.at[4 * a + k], recv_sem=recv.at[4 * a + k], device_id=(x, y, 1 - c), device_id_type=MESH)
        for a in range(len(ins)) for k in range(4)]


def _pair_exchange(arrs, swaps, name):
    n = len(arrs)

    def body(*refs):
        rems = _pair_copies(refs[:n], refs[n:2 * n], *refs[2 * n:], swaps)
        for rc in rems:
            rc.start()
        for rc in rems:
            rc.wait_recv()
        for rc in rems:
            rc.wait_send()

    any_spec = pl.BlockSpec(memory_space=pl.ANY)
    return pl.pallas_call(
        body, name=name,
        in_specs=[any_spec] * n, out_specs=[any_spec] * n,
        out_shape=[jax.ShapeDtypeStruct((4,) + a.shape[1:], a.dtype) for a in arrs],
        scratch_shapes=[pltpu.SemaphoreType.DMA((4 * n,)), pltpu.SemaphoreType.DMA((4 * n,))],
    )(*arrs)


def _chip_copies(ins, outs, send, recv):
    x, y, c = lax.axis_index("x"), lax.axis_index("y"), lax.axis_index("c")
    chips = [(1 - x, y), (x, 1 - y), (1 - x, 1 - y)]
    return [pltpu.make_async_remote_copy(
        src_ref=ins[a].at[2 * px + py], dst_ref=outs[a].at[j], send_sem=send.at[3 * a + j], recv_sem=recv.at[3 * a + j],
        device_id=(px, py, c), device_id_type=MESH) for a in range(len(ins)) for j, (px, py) in enumerate(chips)]


def _chip_exchange(arrs, name):
    n = len(arrs)

    def body(*refs):
        rems = _chip_copies(refs[:n], refs[n:2 * n], *refs[2 * n:])
        for rc in rems:
            rc.start()
        for rc in rems:
            rc.wait_recv()
        for rc in rems:
            rc.wait_send()

    any_spec = pl.BlockSpec(memory_space=pl.ANY)
    return pl.pallas_call(
        body, name=name,
        in_specs=[any_spec] * n, out_specs=[any_spec] * n,
        out_shape=[jax.ShapeDtypeStruct((3,) + a.shape[1:], a.dtype) for a in arrs],
        scratch_shapes=[pltpu.SemaphoreType.DMA((3 * n,)), pltpu.SemaphoreType.DMA((3 * n,))],
    )(*arrs)


class _Side:
    def __init__(self, arrs, out_shapes, scratch, start, finish, mid=None):
        self.arrs, self.out_shapes, self.scratch = list(arrs), list(out_shapes), list(scratch)
        self.start, self.finish, self.mid = start, finish, mid


def _copies_side(arrs, out_shapes, n_copies, make):
    def start(ins, outs, *sems):
        for cp in make(ins, outs, *sems):
            cp.start()

    def finish(ins, outs, *sems):
        cps = make(ins, outs, *sems)
        for cp in cps:
            cp.wait_recv()
        for cp in cps:
            cp.wait_send()

    return _Side(arrs, out_shapes, [pltpu.SemaphoreType.DMA((n_copies,)), pltpu.SemaphoreType.DMA((n_copies,))], start, finish)


def _pair_side(arrs, swaps):
    return _copies_side(arrs, [jax.ShapeDtypeStruct((4,) + a.shape[1:], a.dtype) for a in arrs], 4 * len(arrs),
                        functools.partial(_pair_copies, swaps=swaps))


def _chip_side(arrs):
    return _copies_side(arrs, [jax.ShapeDtypeStruct((3,) + a.shape[1:], a.dtype) for a in arrs], 3 * len(arrs), _chip_copies)


def _gather_side(arrs, swaps):
    return _Side(arrs, [jax.ShapeDtypeStruct((NDEV,) + a.shape, a.dtype) for a in arrs], _Gather.scratch(len(arrs)),
                 start=lambda ins, outs, *sems: _Gather(ins, outs, *sems, swaps).start(),
                 mid=lambda ins, outs, *sems: _Gather(ins, outs, *sems, swaps).forward(),
                 finish=lambda ins, outs, *sems: _Gather(ins, outs, *sems, swaps).finish())


def _row_call(body, side, *, name, steps, in_specs, out_specs, out_shape, ins):
    if side is None:
        res = pl.pallas_call(body, name=name, grid=(steps,), in_specs=in_specs, out_specs=out_specs, out_shape=out_shape,
                             compiler_params=_params("arbitrary"))(*ins)
        return list(res), []
    n_in, n_out, ne = len(in_specs), len(out_specs), len(side.arrs)

    def wrapped(*refs):
        e_in = refs[n_in:n_in + ne]
        e_out = refs[n_in + ne + n_out:n_in + 2 * ne + n_out]
        sems = refs[n_in + 2 * ne + n_out:]
        i = pl.program_id(0)

        @pl.when(i == 0)
        def _():
            side.start(e_in, e_out, *sems)

        if side.mid is not None:
            @pl.when(i == steps // 2)
            def _():
                side.mid(e_in, e_out, *sems)

        body(*refs[:n_in], *refs[n_in + ne:n_in + ne + n_out])

        @pl.when(i == steps - 1)
        def _():
            side.finish(e_in, e_out, *sems)

    any_spec = pl.BlockSpec(memory_space=pl.ANY)
    res = pl.pallas_call(
        wrapped, name=name, grid=(steps,), in_specs=list(in_specs) + [any_spec] * ne,
        out_specs=list(out_specs) + [any_spec] * ne, out_shape=list(out_shape) + side.out_shapes,
        scratch_shapes=side.scratch,
        compiler_params=_params("arbitrary"))(*ins, *side.arrs)
    return list(res[:n_out]), list(res[n_out:])


def _pair_sum(grads, got, swap, core, name):
    _, r, c = got.shape
    tr = r if r <= 352 else r // 2

    def own_map(k, i, core_ref):
        return (_slot(swap, k // 2, k % 2, core_ref[0]), i, 0)

    def body(core_ref, a_ref, b_ref, o_ref):
        o_ref[...] = a_ref[...] + b_ref[...]

    spec = pl.BlockSpec((None, tr, c), lambda k, i, core_ref: (k, i, 0))
    return pl.pallas_call(
        body, name=name,
        grid_spec=pltpu.PrefetchScalarGridSpec(
            num_scalar_prefetch=1, grid=(4, r // tr),
            in_specs=[pl.BlockSpec((None, tr, c), own_map), spec], out_specs=spec),
        out_shape=jax.ShapeDtypeStruct(got.shape, got.dtype), compiler_params=_params("parallel", "parallel"),
    )(core, grads, got)


def _adamw_math(w, g, m, v):
    m = ADAM_B1 * m + (1.0 - ADAM_B1) * g
    v = ADAM_B2 * v + (1.0 - ADAM_B2) * (g * g)
    m_hat = m / (1.0 - ADAM_B1 ** ADAM_STEP)
    v_hat = v / (1.0 - ADAM_B2 ** ADAM_STEP)
    delta = -ADAM_LR * (m_hat / (jnp.sqrt(v_hat) + ADAM_EPS) + ADAM_WD * w)
    return delta, m, v


def _adamw_tile(r):
    for cand in (256, 352, 128):
        if r % cand == 0:
            return cand
    return r


def _adamw(w, m, v, g, name):
    r, c = w.shape
    tr = _adamw_tile(r)
    spec = pl.BlockSpec((tr, c), lambda i: (i, 0))

    def body(w_ref, m_ref, v_ref, g_ref, d_ref, nm_ref, nv_ref):
        d_ref[...], nm_ref[...], nv_ref[...] = _adamw_math(w_ref[...], g_ref[...], m_ref[...], v_ref[...])

    out = jax.ShapeDtypeStruct((r, c), F32)
    return pl.pallas_call(
        body, name=name, grid=(r // tr,), in_specs=[spec] * 4, out_specs=[spec] * 3, out_shape=[out] * 3,
        compiler_params=_params("parallel"),
    )(w, m, v, g)


def _adamw_reduce(w, m, v, sums, recv, chip, name):
    r, c = w.shape
    tr = _adamw_tile(r)
    spec = pl.BlockSpec((tr, c), lambda i, chip_ref: (i, 0))

    def body(chip_ref, w_ref, m_ref, v_ref, s_ref, p_ref, g_ref, d_ref, nm_ref, nv_ref):
        g = ((s_ref[...] + p_ref[0]) + p_ref[1]) + p_ref[2]
        g_ref[...] = g
        d_ref[...], nm_ref[...], nv_ref[...] = _adamw_math(w_ref[...], g, m_ref[...], v_ref[...])

    out = jax.ShapeDtypeStruct((r, c), F32)
    return pl.pallas_call(
        body, name=name,
        grid_spec=pltpu.PrefetchScalarGridSpec(
            num_scalar_prefetch=1, grid=(r // tr,),
            in_specs=[spec, spec, spec, pl.BlockSpec((None, tr, c), lambda i, chip_ref: (chip_ref[0], i, 0)),
                      pl.BlockSpec((3, tr, c), lambda i, chip_ref: (0, i, 0))],
            out_specs=[spec] * 4),
        out_shape=[out] * 4, compiler_params=_params("parallel"),
    )(chip, w, m, v, sums, recv)


def _vec(n):
    return pl.BlockSpec((1, n), lambda *_: (0, 0))


def _ln_mod(x, g, scale, shift, *, ts, name, side=None):
    s = x.shape[0]
    row = pl.BlockSpec((ts, D), lambda i: (i, 0))

    def body(x_ref, g_ref, sc_ref, sh_ref, h_ref):
        xv = x_ref[...]
        r = lax.rsqrt(jnp.mean(xv * xv, axis=-1, keepdims=True) + EPS)
        h = (xv * r) * g_ref[...]
        h_ref[...] = (h * (1.0 + sc_ref[...]) + sh_ref[...]).astype(BF16)

    (h,), extra = _row_call(body, side, name=name, steps=s // ts, in_specs=[row, _vec(D), _vec(D), _vec(D)],
                            out_specs=[row], out_shape=[jax.ShapeDtypeStruct((s, D), BF16)], ins=(x, g, scale, shift))
    return h, extra


def _proj_res_ln_mod(mix, w, x, gate, g, scale, shift, *, tm, name):
    s = x.shape[0]
    row = pl.BlockSpec((tm, D), lambda i: (i, 0))

    def body(m_ref, w_ref, x_ref, gt_ref, g_ref, sc_ref, sh_ref, a_ref, x1_ref, h_ref):
        att = _dot(m_ref[...], w_ref[...])
        a_ref[...] = att.astype(BF16)
        x1 = x_ref[...] + gt_ref[...] * att
        x1_ref[...] = x1
        r = lax.rsqrt(jnp.mean(x1 * x1, axis=-1, keepdims=True) + EPS)
        h = (x1 * r) * g_ref[...]
        h_ref[...] = (h * (1.0 + sc_ref[...]) + sh_ref[...]).astype(BF16)

    return pl.pallas_call(
        body, name=name, grid=(s // tm,), in_specs=[row, _full(w.shape), row] + [_vec(D)] * 4, out_specs=[row, row, row],
        out_shape=[jax.ShapeDtypeStruct((s, D), BF16), jax.ShapeDtypeStruct((s, D), F32), jax.ShapeDtypeStruct((s, D), BF16)],
        compiler_params=_params("parallel"),
    )(mix, w, x, gate, g, scale, shift)


def _proj_loss_head(act, w, x1, tgt, gate2, *, tm, name):
    s = x1.shape[0]
    n = s // tm
    row = pl.BlockSpec((tm, D), lambda i: (i, 0))
    acc8 = pl.BlockSpec((SUBLANES, D), lambda i: (0, 0))

    def body(a_ref, w_ref, x_ref, t_ref, g_ref, dy_ref, df_ref, dg_ref, loss_ref, lacc):
        i = pl.program_id(0)

        @pl.when(i == 0)
        def _():
            lacc[...] = jnp.zeros_like(lacc)
            dg_ref[...] = jnp.zeros_like(dg_ref)

        f = _dot(a_ref[...], w_ref[...])
        diff = x_ref[...] + g_ref[...] * f - t_ref[...]
        lacc[...] += _colsum8(diff * diff)
        dy = diff * (1.0 / D)
        dy_ref[...] = dy
        df_ref[...] = (dy * g_ref[...]).astype(BF16)
        dg_ref[...] += _colsum8(dy * f)

        @pl.when(i == n - 1)
        def _():
            loss_ref[...] = jnp.full((SUBLANES, LANES), (0.5 / D) * jnp.sum(lacc[...]), F32)

    return pl.pallas_call(
        body, name=name, grid=(n,),
        in_specs=[pl.BlockSpec((tm, act.shape[1]), lambda i: (i, 0)), _full(w.shape), row, row, _vec(D)],
        out_specs=[row, row, acc8, _full((SUBLANES, LANES))],
        out_shape=[jax.ShapeDtypeStruct((s, D), F32), jax.ShapeDtypeStruct((s, D), BF16),
                   jax.ShapeDtypeStruct((SUBLANES, D), F32), jax.ShapeDtypeStruct((SUBLANES, LANES), F32)],
        scratch_shapes=[pltpu.VMEM((SUBLANES, D), F32)], compiler_params=_params("arbitrary"),
    )(act, w, x1, tgt, gate2)


def _group_rsqrt(t, bd):
    return lax.rsqrt(_split_dot(t * t, bd) * (1.0 / HD) + EPS)


def _in_proj_qk_norm(h, w, qg, kg, bd, *, tm, name):
    s = h.shape[0]

    def body(h_ref, w_ref, qg_ref, kg_ref, bd_ref, p_ref, o_ref):
        bdv = bd_ref[...]
        hv = h_ref[...]
        p_ref[:, 0:DP] = _dot(hv, w_ref[0:DP, :], NT)
        q = _dot(hv, w_ref[DP:DP + DA, :], NT)
        p_ref[:, DP:DP + DA] = q
        o_ref[:, 0:DA] = (q * _group_rsqrt(q, bdv) * qg_ref[...]).astype(BF16)
        k = _dot(hv, w_ref[DP + DA:DP + 2 * DA, :], NT)
        p_ref[:, DP + DA:DP + 2 * DA] = k
        o_ref[:, DA:2 * DA] = (k * _group_rsqrt(k, bdv) * kg_ref[...]).astype(BF16)
        v = _dot(hv, w_ref[DP + 2 * DA:, :], NT)
        p_ref[:, DP + 2 * DA:] = v
        o_ref[:, 2 * DA:] = v.astype(BF16)

    return pl.pallas_call(
        body, name=name, grid=(s // tm,),
        in_specs=[pl.BlockSpec((tm, D), lambda i: (i, 0)), _full(w.shape), _vec(DA), _vec(DA), _full((DA, DA))],
        out_specs=[pl.BlockSpec((tm, DIN), lambda i: (i, 0)), pl.BlockSpec((tm, 3 * DA), lambda i: (i, 0))],
        out_shape=[jax.ShapeDtypeStruct((s, DIN), F32), jax.ShapeDtypeStruct((s, 3 * DA), BF16)],
        compiler_params=_params("parallel"),
    )(h, w, qg, kg, bd)


EXP_UNDERFLOW = -120.0


def _log_terms(z):
    neg_abs = lax.bitcast_convert_type(lax.bitcast_convert_type(z, jnp.uint32) | jnp.uint32(0x80000000), F32)
    b = jnp.minimum(z, 0.0) - jnp.log(1.0 + jnp.exp(neg_abs))
    return b, b - z


def _head_masks(rows):
    lane = lax.broadcasted_iota(jnp.int32, (rows, LANES), 1)
    return [lane < HD, lane >= HD]


def _attn_fwd(qkv, gather, swaps, *, tq, tk, hp, name):
    s = qkv.shape[0]
    nrep = tk // LANES
    ndiag = tq // tk
    ng = len(gather)
    nh, wl = 2 * hp, LANES * hp
    ngrp, nq = DA // wl, s // tq
    lanes = [slice(LANES * pp, LANES * (pp + 1)) for pp in range(hp)]

    def body(*refs):
        q_ref, k_ref, v_ref = refs[:3]
        g_in = refs[3:3 + ng]
        o_ref, tot_ref, first_ref = refs[3 + ng:6 + ng]
        g_out = refs[6 + ng:6 + 2 * ng]
        oacc, rc = refs[6 + 2 * ng:8 + 2 * ng]
        g_sems = refs[8 + 2 * ng:]
        i = pl.program_id(1)
        step_id = pl.program_id(0) * nq + i

        @pl.when(step_id == 0)
        def _():
            _Gather(g_in, g_out, *g_sems, swaps).start()

        @pl.when(step_id == (ngrp * nq * 3) // 4)
        def _():
            _Gather(g_in, g_out, *g_sems, swaps).forward()

        heads = _head_masks(tq)
        qs = [jnp.where(heads[a % 2], q_ref[:, lanes[a // 2]] * 0.125, 0.0).astype(BF16) for a in range(nh)]
        dif = lax.broadcasted_iota(jnp.int32, (tq, tk), 0) - lax.broadcasted_iota(jnp.int32, (tq, tk), 1)
        kr = lax.broadcasted_iota(jnp.int32, (tk, tk), 0)
        kc = lax.broadcasted_iota(jnp.int32, (tk, tk), 1)
        later =jnp.where(kr > kc, 1.0, 0.0).astype(BF16)
        oacc[...] = jnp.zeros_like(oacc)
        rc[...] = jnp.zeros_like(rc)

        def tile(kb, thr):
            rows = pl.ds(pl.multiple_of(kb * tk, tk), tk)
            ks = [k_ref[rows, ln] for ln in lanes]
            vs = [v_ref[rows, ln] for ln in lanes]
            qr = slice(0 if thr is None else thr, tq)
            rcv = [rc[a, qr, :] for a in range(nh)]
            zs = [_dot(qs[a][qr], ks[a // 2], NT) for a in range(nh)]
            bs, mbs = [], []
            for a in range(nh):
                b, m = _log_terms(zs[a])
                if thr is not None:
                    m = jnp.where(dif[qr] > thr, m, 0.0)
                bs.append(b)
                mbs.append(m.astype(BF16))
            rl = [_dot(mbs[a], later) for a in range(nh)]
            for a in range(nh):
                p = jnp.exp(bs[a] + (rl[a] + jnp.tile(rcv[a], (1, nrep))))
                if thr is not None:
                    p = jnp.where(dif[qr] > thr, p, 0.0)
                oacc[a, qr, :] += _dot(p.astype(BF16), vs[a // 2])
                rc[a, qr, :] = rcv[a] + (rl[a][:, 0:1] + mbs[a][:, 0:1].astype(F32))

        for d in reversed(range(ndiag)):
            tile(i * ndiag + d, d * tk)

        def live():
            top = rc[0]
            for a in range(1, nh):
                top = jnp.maximum(top, rc[a])
            return jnp.max(top) > EXP_UNDERFLOW

        def step(carry):
            kb, _ = carry
            tile(kb, None)
            return kb - 1, live()

        kb_end, _ = lax.while_loop(lambda cr: jnp.logical_and(cr[0] >= 0, cr[1]), step, (i * ndiag - 1, live()))
        first_ref[pl.program_id(0), i] = (kb_end + 1).astype(F32)
        for pp, ln in enumerate(lanes):
            o_ref[:, ln] = jnp.where(heads[0], oacc[2 * pp], oacc[2 * pp + 1])
            tot_ref[:, ln] = jnp.where(heads[0], rc[2 * pp], rc[2 * pp + 1])

        @pl.when(step_id == ngrp * nq - 1)
        def _():
            _Gather(g_in, g_out, *g_sems, swaps).finish()

    qspec = pl.BlockSpec((tq, wl), lambda p, i: (i, p))
    any_spec = pl.BlockSpec(memory_space=pl.ANY)
    res = pl.pallas_call(
        body, name=name, grid=(ngrp, nq),
        in_specs=[qspec,
                  pl.BlockSpec((s, wl), lambda p, i: (0, ngrp + p)),
                  pl.BlockSpec((s, wl), lambda p, i: (0, 2 * ngrp + p))] + [any_spec] * ng,
        out_specs=[qspec, qspec, pl.BlockSpec(memory_space=pltpu.SMEM)] + [any_spec] * ng,
        out_shape=[jax.ShapeDtypeStruct((s, DA), F32), jax.ShapeDtypeStruct((s, DA), F32),
                   jax.ShapeDtypeStruct((ngrp, nq), F32)]
        + [jax.ShapeDtypeStruct((NDEV,) + a.shape, a.dtype) for a in gather],
        scratch_shapes=[pltpu.VMEM((nh, tq, LANES), F32), pltpu.VMEM((nh, tq, LANES), F32)] + _Gather.scratch(ng),
        compiler_params=_params("arbitrary", "arbitrary"),
    )(qkv, qkv, qkv, *gather)
    return res[0], res[1], res[2], res[3:]


def _attn_bwd(qkv, do, tot, first, exchange, *, tq, tk, hp, name):
    s = qkv.shape[0]
    nrep = tk // LANES
    ndiag = tq // tk
    ne = len(exchange)
    nh, wl = 2 * hp, LANES * hp
    ngrp, nq = DA // wl, s // tq
    lanes = [slice(LANES * pp, LANES * (pp + 1)) for pp in range(hp)]

    def body(*refs):
        q_ref, k_ref, v_ref, do_ref, tot_ref, first_ref = refs[:6]
        e_in = refs[6:6 + ne]
        dq_ref, dk_ref, dv_ref = refs[6 + ne:9 + ne]
        e_out = refs[9 + ne:9 + 2 * ne]
        dqacc, rem, gc = refs[9 + 2 * ne:12 + 2 * ne]
        e_sems = refs[12 + 2 * ne:]
        i = pl.program_id(1)
        step_id = pl.program_id(0) * nq + i

        @pl.when(step_id == 0)
        def _():
            for cp in _chip_copies(e_in, e_out, *e_sems):
                cp.start()

        @pl.when(i == 0)
        def _():
            dk_ref[...] = jnp.zeros_like(dk_ref)
            dv_ref[...] = jnp.zeros_like(dv_ref)

        heads = _head_masks(tq)
        qs = [jnp.where(heads[a % 2], q_ref[:, lanes[a // 2]] * 0.125, 0.0).astype(BF16) for a in range(nh)]
        dob = [jnp.where(heads[a % 2], do_ref[:, lanes[a // 2]], 0.0).astype(BF16) for a in range(nh)]
        dif = lax.broadcasted_iota(jnp.int32, (tq, tk), 0) - lax.broadcasted_iota(jnp.int32, (tq, tk), 1)
        kr = lax.broadcasted_iota(jnp.int32, (tk, tk), 0)
        kc = lax.broadcasted_iota(jnp.int32, (tk, tk), 1)
        up_incl = jnp.where(kr <= kc, 1.0, 0.0).astype(BF16)
        up_strict = jnp.where(kr < kc, 1.0, 0.0).astype(BF16)
        dqacc[...] = jnp.zeros_like(dqacc)
        gc[...] = jnp.zeros_like(gc)
        for pp, ln in enumerate(lanes):
            totv = tot_ref[:, ln]
            swapped = pltpu.roll(totv, HD, axis=1)
            rem[2 * pp] = jnp.where(heads[0], totv, swapped)
            rem[2 * pp + 1] = jnp.where(heads[1], totv, swapped)

        def tile(kb, thr):
            rows = pl.ds(pl.multiple_of(kb * tk, tk), tk)
            ks = [k_ref[rows, ln] for ln in lanes]
            vs = [v_ref[rows, ln] for ln in lanes]
            qr = slice(0 if thr is None else thr, tq)
            remv = [rem[a, qr, :] for a in range(nh)]
            gcv = [gc[a, qr, :] for a in range(nh)]
            zs = [_dot(qs[a][qr], ks[a // 2], NT) for a in range(nh)]
            das = [_dot(dob[a][qr], vs[a // 2], NT) for a in range(nh)]
            bs, mbs = [], []
            for a in range(nh):
                b, m = _log_terms(zs[a])
                if thr is not None:
                    m = jnp.where(dif[qr] > thr, m, 0.0)
                bs.append(b)
                mbs.append(m.astype(BF16))
            pl_ = [_dot(mbs[a], up_incl) for a in range(nh)]
            ps, gs, gbs = [], [], []
            for a in range(nh):
                p = jnp.exp(bs[a] + (jnp.tile(remv[a], (1, nrep)) - pl_[a]))
                if thr is not None:
                    p = jnp.where(dif[qr] > thr, p, 0.0)
                g = p * das[a]
                ps.append(p.astype(BF16))
                gs.append(g)
                gbs.append(g.astype(BF16))
            cl = [_dot(gbs[a], up_strict) for a in range(nh)]
            dk_add = [jnp.zeros((tk, LANES), F32) for _ in range(hp)]
            dv_add = [jnp.zeros((tk, LANES), F32) for _ in range(hp)]
            for a in range(nh):
                dz = gs[a] - jnp.exp(bs[a]) * (gs[a] + (jnp.tile(gcv[a], (1, nrep)) + cl[a]))
                if thr is not None:
                    dz = jnp.where(dif[qr] > thr, dz, 0.0)
                dzb = dz.astype(BF16)
                dqacc[a, qr, :] += _dot(dzb, ks[a // 2])
                dk_add[a // 2] += _dot(dzb, qs[a][qr], TN)
                dv_add[a // 2] += _dot(ps[a], dob[a][qr], TN)
                rem[a, qr, :] = remv[a] - pl_[a][:, tk - 1:tk]
                gc[a, qr, :] = gcv[a] + (cl[a][:, tk - 1:tk] + gbs[a][:, tk - 1:tk].astype(F32))
            for pp, ln in enumerate(lanes):
                dk_ref[rows, ln] += dk_add[pp]
                dv_ref[rows, ln] += dv_add[pp]

        def step(kb, carry):
            tile(kb, None)
            return carry

        lax.fori_loop(first_ref[pl.program_id(0), i].astype(jnp.int32), i * ndiag, step, 0)
        for d in range(ndiag):
            tile(i * ndiag + d, d * tk)
        for pp, ln in enumerate(lanes):
            dq_ref[:, ln] = jnp.where(heads[0], dqacc[2 * pp], dqacc[2 * pp + 1]) * 0.125

        @pl.when(step_id == ngrp * nq - 1)
        def _():
            cps = _chip_copies(e_in, e_out, *e_sems)
            for cp in cps:
                cp.wait_recv()
            for cp in cps:
                cp.wait_send()

    qspec = pl.BlockSpec((tq, wl), lambda p, i: (i, p))
    full = pl.BlockSpec((s, wl), lambda p, i: (0, p))
    any_spec = pl.BlockSpec(memory_space=pl.ANY)
    out = jax.ShapeDtypeStruct((s, DA), F32)
    res = pl.pallas_call(
        body, name=name, grid=(ngrp, nq),
        in_specs=[qspec, pl.BlockSpec((s, wl), lambda p, i: (0, ngrp + p), pipeline_mode=pl.Buffered(1)),
                  pl.BlockSpec((s, wl), lambda p, i: (0, 2 * ngrp + p), pipeline_mode=pl.Buffered(1)), qspec, qspec,
                  pl.BlockSpec(memory_space=pltpu.SMEM)] + [any_spec] * ne,
        out_specs=[qspec, full, full] + [any_spec] * ne,
        out_shape=[out, out, out] + [jax.ShapeDtypeStruct((3,) + a.shape[1:], a.dtype) for a in exchange],
        scratch_shapes=[pltpu.VMEM((nh, tq, LANES), F32)] * 3
        + [pltpu.SemaphoreType.DMA((3 * ne,)), pltpu.SemaphoreType.DMA((3 * ne,))],
        compiler_params=_params("arbitrary", "arbitrary"),
    )(qkv, qkv, qkv, do, tot, first, *exchange)
    return res[0], res[1], res[2], res[3:]


def _shift_rows(v, k):
    return pltpu.roll(v, k % v.shape[0], axis=0)


def _pooled(u, uh, i, g, w, ts):
    halo = jnp.where(i > 0, uh, 0.0)
    ue = jnp.concatenate([halo, u], axis=0)
    acc, span = ue, 1
    while span < w:
        acc = acc + _shift_rows(acc, span)
        span *= 2
    tpos = i * ts + lax.broadcasted_iota(jnp.int32, (ts, 1), 0)
    cnt = jnp.minimum(tpos + 1, w).astype(F32)
    return acc[HALO:] / cnt - u


def _pool_mix(proj, o, pw, pb, ps, ag, bd, *, ts, name):
    s = proj.shape[0]
    hb = ts // HALO

    def body(u_ref, uh_ref, o_ref, pw_ref, pb_ref, ps_ref, ag_ref, bd_ref, mix_ref):
        i = pl.program_id(0)
        for g, w in enumerate(POOL_WINDOWS):
            cols = slice(g * LANES, (g + 1) * LANES)
            pooled = _pooled(u_ref[:, cols], uh_ref[:, cols], i, g, w, ts)
            yv = (_dot(pooled.astype(BF16), pw_ref[g]) + pb_ref[:, cols]) * ps_ref[:, cols]
            mix_ref[:, cols] = yv.astype(BF16)
        ov = o_ref[...]
        mix_ref[:, DP:] = (ov * _group_rsqrt(ov, bd_ref[...]) * ag_ref[...]).astype(BF16)

    return pl.pallas_call(
        body, name=name, grid=(s // ts,),
        in_specs=[pl.BlockSpec((ts, DP), lambda i: (i, 0)),
                  pl.BlockSpec((HALO, DP), lambda i: (jnp.maximum(i * hb - 1, 0), 0)),
                  pl.BlockSpec((ts, DA), lambda i: (i, 0)),
                  _full((4, LANES, LANES)), _vec(DP), _vec(DP), _vec(DA), _full((DA, DA))],
        out_specs=pl.BlockSpec((ts, D), lambda i: (i, 0)),
        out_shape=jax.ShapeDtypeStruct((s, D), BF16), compiler_params=_params("parallel"),
    )(proj, proj, o, pw, pb, ps, ag, bd)


CF = DFF // 2
MXU_COLS = 256


def _sub_chunks(width):
    return [(c0, min(MXU_COLS, width - c0)) for c0 in range(0, width, MXU_COLS)]


def _up_conv_gate(h2, w_up, cw, cb, *, tm, name):
    s = h2.shape[0]
    hb = tm // HALO

    def body(a_ref, ah_ref, w_ref, cw_ref, cb_ref, up_ref, c_ref, act_ref):
        i = pl.program_id(1)
        ext = jnp.concatenate([ah_ref[...], a_ref[...]], axis=0)
        live_halo = i > 0
        for c0, cwid in _sub_chunks(CF):
            conv = []
            for off in (c0, CF + c0):
                cols = slice(off, off + cwid)
                u = _dot(ext, w_ref[cols, :], NT)
                up_ref[:, cols] = u[HALO:].astype(BF16)
                row = lax.broadcasted_iota(jnp.int32, (HALO + tm, 1), 0)
                ue = jnp.where(jnp.logical_or(row >= HALO, live_halo), u, 0.0)
                y = cw_ref[2:3, cols] * ue + cw_ref[1:2, cols] * _shift_rows(ue, 1) + cw_ref[0:1, cols] * _shift_rows(ue, 2)
                cv = y[HALO:] + cb_ref[:, cols]
                c_ref[:, cols] = cv.astype(BF16)
                conv.append(cv)
            gt, vl = conv
            act_ref[:, c0:c0 + cwid] = (gt / (1.0 + jnp.exp(-gt)) * vl).astype(BF16)

    return pl.pallas_call(
        body, name=name, grid=(2, s // tm),
        in_specs=[pl.BlockSpec((tm, D), lambda j, i: (i, 0)),
                  pl.BlockSpec((HALO, D), lambda j, i: (jnp.maximum(i * hb - 1, 0), 0)),
                  pl.BlockSpec((2 * CF, D), lambda j, i: (j, 0)),
                  pl.BlockSpec((3, 2 * CF), lambda j, i: (0, j)), pl.BlockSpec((1, 2 * CF), lambda j, i: (0, j))],
        out_specs=[pl.BlockSpec((tm, 2 * CF), lambda j, i: (i, j)), pl.BlockSpec((tm, 2 * CF), lambda j, i: (i, j)),
                   pl.BlockSpec((tm, CF), lambda j, i: (i, j))],
        out_shape=[jax.ShapeDtypeStruct((s, 2 * DFF), BF16), jax.ShapeDtypeStruct((s, 2 * DFF), BF16),
                   jax.ShapeDtypeStruct((s, DFF), BF16)],
        compiler_params=_params("parallel", "parallel"),
    )(h2, h2, w_up, cw, cb)


def _down_bwd_gate(dffn, w_down, conv, *, tm, name):
    s = dffn.shape[0]

    def body(a_ref, w_ref, c_ref, d_ref, db_ref):
        i = pl.program_id(1)

        @pl.when(i == 0)
        def _():
            db_ref[...] = jnp.zeros_like(db_ref)

        a = a_ref[...]
        for c0, cwid in _sub_chunks(CF):
            gcols, vcols = slice(c0, c0 + cwid), slice(CF + c0, CF + c0 + cwid)
            da = _dot(a, w_ref[gcols, :], NT)
            gt, vl = c_ref[:, gcols].astype(F32), c_ref[:, vcols].astype(F32)
            sg = 1.0 / (1.0 + jnp.exp(-gt))
            dgt = da * vl * (sg * (1.0 + gt * (1.0 - sg)))
            dvl = da * (gt * sg)
            d_ref[:, gcols] = dgt.astype(BF16)
            d_ref[:, vcols] = dvl.astype(BF16)
            db_ref[:, gcols] += _colsum8(dgt)
            db_ref[:, vcols] += _colsum8(dvl)

    return pl.pallas_call(
        body, name=name, grid=(2, s // tm),
        in_specs=[pl.BlockSpec((tm, D), lambda j, i: (i, 0)), pl.BlockSpec((CF, D), lambda j, i: (j, 0)),
                  pl.BlockSpec((tm, 2 * CF), lambda j, i: (i, j))],
        out_specs=[pl.BlockSpec((tm, 2 * CF), lambda j, i: (i, j)), pl.BlockSpec((SUBLANES, 2 * CF), lambda j, i: (0, j))],
        out_shape=[jax.ShapeDtypeStruct((s, 2 * DFF), BF16), jax.ShapeDtypeStruct((SUBLANES, 2 * DFF), F32)],
        compiler_params=_params("parallel", "arbitrary"),
    )(dffn, w_down, conv)


def _conv_bwd_up_bwd(dc, up, cw, w_up, *, tm, name):
    s = up.shape[0]
    hb = tm // HALO
    nb = s // HALO
    nk = 2 * DFF // CF
    n = s // tm

    def body(d_ref, dn_ref, u_ref, cw_ref, w_ref, du_ref, dh_ref, dw_ref, acc, dwacc):
        i, k = pl.program_id(0), pl.program_id(1)

        @pl.when(jnp.logical_and(i == 0, k == 0))
        def _():
            dwacc[...] = jnp.zeros_like(dwacc)

        @pl.when(k == 0)
        def _():
            acc[...] = jnp.zeros_like(acc)

        live_next = i < n - 1
        part = None
        for c0, cwid in _sub_chunks(CF):
            cols = slice(c0, c0 + cwid)
            dcur = d_ref[:, cols].astype(F32)
            de = jnp.concatenate([dcur, jnp.where(live_next, dn_ref[:, cols].astype(F32), 0.0)], axis=0)
            d1 = _shift_rows(de, -1)[:tm]
            d2 = _shift_rows(de, -2)[:tm]
            du = (cw_ref[2:3, cols] * dcur + cw_ref[1:2, cols] * d1 + cw_ref[0:1, cols] * d2).astype(BF16)
            du_ref[:, cols] = du
            prod = _dot(du, w_ref[cols, :])
            part = prod if part is None else part + prod
            u = u_ref[:, cols].astype(F32)
            for tap, dsh in ((2, dcur), (1, d1), (0, d2)):
                dwacc[k, SUBLANES * tap:SUBLANES * (tap + 1), cols] += _colsum8(dsh * u)
        acc[...] += part

        @pl.when(k == nk - 1)
        def _():
            dh_ref[...] = acc[...].astype(dh_ref.dtype)

        @pl.when(jnp.logical_and(i == n - 1, k == nk - 1))
        def _():
            dw_ref[...] = dwacc[...]

    res = pl.pallas_call(
        body, name=name, grid=(n, nk),
        in_specs=[pl.BlockSpec((tm, CF), lambda i, k: (i, k)),
                  pl.BlockSpec((HALO, CF), lambda i, k: (jnp.minimum((i + 1) * hb, nb - 1), k)),
                  pl.BlockSpec((tm, CF), lambda i, k: (i, k)),
                  pl.BlockSpec((3, CF), lambda i, k: (0, k)),
                  pl.BlockSpec((CF, D), lambda i, k: (k, 0))],
        out_specs=[pl.BlockSpec((tm, CF), lambda i, k: (i, k)), pl.BlockSpec((tm, D), lambda i, k: (i, 0)),
                   _full((nk, 24, CF))],
        out_shape=[jax.ShapeDtypeStruct((s, 2 * DFF), BF16), jax.ShapeDtypeStruct((s, D), BF16),
                   jax.ShapeDtypeStruct((nk, 24, CF), F32)],
        scratch_shapes=[pltpu.VMEM((tm, D), F32), pltpu.VMEM((nk, 24, CF), F32)],
        compiler_params=_params("arbitrary", "arbitrary"),
    )(dc, dc, up, cw, w_up)
    return res[0], res[1], jnp.transpose(res[2], (1, 0, 2)).reshape(24, 2 * DFF)


def _ln_mod_bwd(dh, xin, g, scale, resid, extra, gate, *, ts, name, side=None):
    s = xin.shape[0]
    row = pl.BlockSpec((ts, D), lambda i: (i, 0))
    acc8 = pl.BlockSpec((SUBLANES, D), lambda i: (0, 0))
    with_gate = extra is not None

    def body(*refs):
        if with_gate:
            dh_ref, x_ref, g_ref, sc_ref, r_ref, e_ref, gt_ref, dx_ref, da_ref, dsh, dsc, dg, dgt = refs
        else:
            dh_ref, x_ref, g_ref, sc_ref, r_ref, dx_ref, dsh, dsc, dg = refs
        i = pl.program_id(0)

        @pl.when(i == 0)
        def _():
            for acc in (dsh, dsc, dg) + ((dgt,) if with_gate else ()):
                acc[...] = jnp.zeros_like(acc)

        xv, dhv = x_ref[...], dh_ref[...].astype(F32)
        r = lax.rsqrt(jnp.mean(xv * xv, axis=-1, keepdims=True) + EPS)
        xn = xv * r
        dsh[...] += _colsum8(dhv)
        dsc[...] += _colsum8(dhv * (xn * g_ref[...]))
        dhp = dhv * (1.0 + sc_ref[...])
        dg[...] += _colsum8(dhp * xn)
        dxn = dhp * g_ref[...]
        dx = r_ref[...] + r * (dxn - xn * jnp.mean(dxn * xn, axis=-1, keepdims=True))
        dx_ref[...] = dx
        if with_gate:
            da_ref[...] = (dx * gt_ref[...]).astype(BF16)
            dgt[...] += _colsum8(dx * e_ref[...].astype(F32))

    f32o, p8 = jax.ShapeDtypeStruct((s, D), F32), jax.ShapeDtypeStruct((SUBLANES, D), F32)
    if with_gate:
        ins, in_specs = (dh, xin, g, scale, resid, extra, gate), [row, row, _vec(D), _vec(D), row, row, _vec(D)]
        out_specs, out_shape = [row, row, acc8, acc8, acc8, acc8], [f32o, jax.ShapeDtypeStruct((s, D), BF16), p8, p8, p8, p8]
    else:
        ins, in_specs = (dh, xin, g, scale, resid), [row, row, _vec(D), _vec(D), row]
        out_specs, out_shape = [row, acc8, acc8, acc8], [f32o, p8, p8, p8]
    return _row_call(body, side, name=name, steps=s // ts, in_specs=in_specs, out_specs=out_specs,
                     out_shape=out_shape, ins=ins)


def _group_norm_bwd(t, dn_out, gvec, bd):
    r = _group_rsqrt(t, bd)
    dg_terms = dn_out * t * r
    dn = dn_out * gvec
    dt = r * (dn - t * (r * r) * (_split_dot(dn * t, bd) * (1.0 / HD)))
    return dt, dg_terms


def _mix_bwd(dmix, proj, o, pw, pb, ps, ag, bd, *, ts, name, side=None):
    s = proj.shape[0]
    hb = ts // HALO
    nb = s // HALO

    def body(dm_ref, dmn_ref, u_ref, uh_ref, o_ref, pw_ref, pb_ref, ps_ref, ag_ref, bd_ref,
             du_ref, do_ref, dpw_ref, dpb_ref, dps_ref, dag_ref):
        i = pl.program_id(0)
        n = s // ts

        @pl.when(i == 0)
        def _():
            for acc in (dpw_ref, dpb_ref, dps_ref, dag_ref):
                acc[...] = jnp.zeros_like(acc)

        for g, w in enumerate(POOL_WINDOWS):
            cols = slice(g * LANES, (g + 1) * LANES)
            wg = pw_ref[g]
            psg = ps_ref[:, cols]
            pooled = _pooled(u_ref[:, cols], uh_ref[:, cols], i, g, w, ts).astype(BF16)
            dy = dm_ref[:, cols].astype(F32)
            dps_ref[:, cols] += _colsum8(dy * (_dot(pooled, wg) + pb_ref[:, cols]))
            dpre = dy * psg
            dpb_ref[:, cols] += _colsum8(dpre)
            dpreb = dpre.astype(BF16)
            dpw_ref[g * LANES:(g + 1) * LANES, :] += _dot(pooled, dpreb, TN)
            dpool = _dot(dpreb, wg, NT)
            dnext = _dot((dmn_ref[:, cols].astype(F32) * psg).astype(BF16), wg, NT)
            dpe = jnp.concatenate([dpool, jnp.where(i < n - 1, dnext, 0.0)], axis=0)
            tpos = i * ts + lax.broadcasted_iota(jnp.int32, (ts + HALO, 1), 0)
            acc = dpe / jnp.minimum(tpos + 1, w).astype(F32)
            span = 1
            while span < w:
                acc = acc + _shift_rows(acc, -span)
                span *= 2
            du_ref[:, cols] = acc[:ts] - dpool
        ov = o_ref[...]
        dov, dg_terms = _group_norm_bwd(ov, dm_ref[:, DP:].astype(F32), ag_ref[...], bd_ref[...])
        do_ref[...] = dov
        dag_ref[...] += _colsum8(dg_terms)

    p8 = jax.ShapeDtypeStruct((SUBLANES, DP), F32)
    acc8 = pl.BlockSpec((SUBLANES, DP), lambda i: (0, 0))
    half = pl.BlockSpec((ts, DP), lambda i: (i, 0))
    return _row_call(
        body, side, name=name, steps=s // ts,
        in_specs=[pl.BlockSpec((ts, D), lambda i: (i, 0)),
                  pl.BlockSpec((HALO, DP), lambda i: (jnp.minimum((i + 1) * hb, nb - 1), 0)),
                  half, pl.BlockSpec((HALO, DP), lambda i: (jnp.maximum(i * hb - 1, 0), 0)),
                  half, _full((4, LANES, LANES)), _vec(DP), _vec(DP), _vec(DA), _full((DA, DA))],
        out_specs=[half, half, _full((DP, LANES)), acc8, acc8, acc8],
        out_shape=[jax.ShapeDtypeStruct((s, DP), F32), jax.ShapeDtypeStruct((s, DA), F32),
                   jax.ShapeDtypeStruct((DP, LANES), F32), p8, p8, p8],
        ins=(dmix, dmix, proj, proj, o, pw, pb, ps, ag, bd))


def _qk_norm_bwd(du, dq, dk, dv, proj, qg, kg, bd, *, ts, name):
    s = proj.shape[0]

    def body(du_ref, dq_ref, dk_ref, dv_ref, q_ref, k_ref, qg_ref, kg_ref, bd_ref, dp_ref, dqg_ref, dkg_ref):
        i = pl.program_id(0)

        @pl.when(i == 0)
        def _():
            dqg_ref[...] = jnp.zeros_like(dqg_ref)
            dkg_ref[...] = jnp.zeros_like(dkg_ref)

        bdv = bd_ref[...]
        dqr, tq = _group_norm_bwd(q_ref[...], dq_ref[...], qg_ref[...], bdv)
        dkr, tk = _group_norm_bwd(k_ref[...], dk_ref[...], kg_ref[...], bdv)
        dqg_ref[...] += _colsum8(tq)
        dkg_ref[...] += _colsum8(tk)
        dp_ref[:, 0:DP] = du_ref[...].astype(BF16)
        dp_ref[:, DP:DP + DA] = dqr.astype(BF16)
        dp_ref[:, DP + DA:DP + 2 * DA] = dkr.astype(BF16)
        dp_ref[:, DP + 2 * DA:] = dv_ref[...].astype(BF16)

    half = pl.BlockSpec((ts, DA), lambda i: (i, 0))
    col = lambda j: pl.BlockSpec((ts, DA), lambda i: (i, j))
    acc8 = pl.BlockSpec((SUBLANES, DA), lambda i: (0, 0))
    p8 = jax.ShapeDtypeStruct((SUBLANES, DA), F32)
    return pl.pallas_call(
        body, name=name, grid=(s // ts,),
        in_specs=[half, half, half, half, col(1), col(2), _vec(DA), _vec(DA), _full((DA, DA))],
        out_specs=[pl.BlockSpec((ts, DIN), lambda i: (i, 0)), acc8, acc8],
        out_shape=[jax.ShapeDtypeStruct((s, DIN), BF16), p8, p8],
        compiler_params=_params("arbitrary"),
    )(du, dq, dk, dv, proj, proj, qg, kg, bd)


def _split3(a):
    hi = a.astype(BF16)
    return hi, (a - hi.astype(F32)).astype(BF16)


def _dot3(a, b, dn):
    ah, al = _split3(a)
    bh, bl = _split3(b)
    return _dot(ah, bh, dn) + (_dot(ah, bl, dn) + _dot(al, bh, dn))


def _ada_fwd(c_all, w, b, name):
    nw = w.shape[1]

    def body(c_ref, w_ref, b_ref, o_ref):
        cv = c_ref[...]
        act = cv / (1.0 + jnp.exp(-cv))
        o_ref[...] = _dot3(act, w_ref[...], NN) + b_ref[...]

    return pl.pallas_call(
        body, name=name, in_specs=[_full((NDEV, D)), _full(w.shape), _full((1, nw))], out_specs=_full((NDEV, nw)),
        out_shape=jax.ShapeDtypeStruct((NDEV, nw), F32), grid=(1,), compiler_params=_params("arbitrary"),
    )(c_all, w, b)


def _ada_bwd(c_all, dmod, name):
    nw = dmod.shape[1]

    def body(c_ref, d_ref, o_ref):
        cv = c_ref[...]
        act = cv / (1.0 + jnp.exp(-cv))
        o_ref[...] = _dot3(act, d_ref[...], TN)[None]

    return pl.pallas_call(
        body, name=name, in_specs=[_full((NDEV, D)), _full((NDEV, nw))], out_specs=_full((1, D, nw)),
        out_shape=jax.ShapeDtypeStruct((1, D, nw), F32), grid=(1,), compiler_params=_params("arbitrary"),
    )(c_all, dmod)


def _fold_heads(v):
    acc = v[:, 0:HD]
    for h in range(1, DA // HD):
        acc = acc + v[:, h * HD:(h + 1) * HD]
    return acc


def _pack_partials(pieces, dcw_p, name):
    n_p = len(pieces)
    total = sum(p.shape[1] for p in pieces) + 3 * dcw_p.shape[1]
    npack = -(-total // (SUBLANES * LANES)) * (SUBLANES * LANES)

    def body(*refs):
        out = refs[-1]
        off = 0
        for r in refs[:n_p]:
            out[:, off:off + r.shape[1]] = jnp.sum(r[...], axis=0, keepdims=True)
            off += r.shape[1]
        dw = refs[n_p]
        for tap in range(3):
            out[:, off:off + dw.shape[1]] = jnp.sum(dw[SUBLANES * tap:SUBLANES * (tap + 1), :], axis=0, keepdims=True)
            off += dw.shape[1]
        if off < npack:
            out[:, off:] = jnp.zeros((1, npack - off), F32)

    arrs = list(pieces) + [dcw_p]
    return pl.pallas_call(
        body, name=name, grid=(1,), in_specs=[_full(a.shape) for a in arrs], out_specs=_full((1, npack)),
        out_shape=jax.ShapeDtypeStruct((1, npack), F32), compiler_params=_params("arbitrary"),
    )(*arrs)


def _small_update(gathered, gathered_pw, gathered_cw, specs, params, loss_off, name):
    names = [sp[0] for sp in specs]
    flat = []
    for nme in names + ["pool_w", "conv_w"]:
        flat += list(params[nme])
    n_in = len(flat)

    def body(*refs):
        ga_ref, gp_ref, gc_ref = refs[0], refs[1], refs[2]
        prm = refs[3:3 + n_in]
        outs = refs[3 + n_in:]
        total = ga_ref[0:1, :]
        for dv in range(1, NDEV):
            total = total + ga_ref[dv:dv + 1, :]
        k = 0
        for idx, (nme, off, width, fold) in enumerate(specs):
            g = total[:, off:off + width]
            if fold:
                g = _fold_heads(g)
            w_ref, m_ref, v_ref = prm[3 * idx:3 * idx + 3]
            d, nm, nv = _adamw_math(w_ref[...], g, m_ref[...], v_ref[...])
            for val in (g, d, nm, nv):
                outs[k][...] = val
                k += 1
        gpw = gp_ref[0]
        for dv in range(1, NDEV):
            gpw = gpw + gp_ref[dv]
        w_ref, m_ref, v_ref = prm[3 * len(specs):3 * len(specs) + 3]
        d, nm, nv = _adamw_math(w_ref[...], gpw, m_ref[...], v_ref[...])
        for val in (gpw, d, nm, nv):
            outs[k][...] = val
            k += 1
        gcw = gc_ref[0]
        for dv in range(1, NDEV):
            gcw = gcw + gc_ref[dv]
        w_ref, m_ref, v_ref = prm[3 * len(specs) + 3:3 * len(specs) + 6]
        d, nm, nv = _adamw_math(w_ref[...], gcw, m_ref[...], v_ref[...])
        for val in (gcw, d, nm, nv):
            outs[k][...] = val
            k += 1
        outs[k][...] = ga_ref[:, 0:6 * D]
        outs[k + 1][...] = total[:, loss_off:loss_off + LANES] * (1.0 / SUBLANES)

    out_shape, out_specs = [], []
    for nme in names + ["pool_w", "conv_w"]:
        shp = params[nme][0].shape
        out_shape += [jax.ShapeDtypeStruct(shp, F32)] * 4
        out_specs += [_full(shp)] * 4
    out_shape += [jax.ShapeDtypeStruct((NDEV, 6 * D), F32), jax.ShapeDtypeStruct((1, LANES), F32)]
    out_specs += [_full((NDEV, 6 * D)), _full((1, LANES))]
    res = pl.pallas_call(
        body, name=name, grid=(1,),
        in_specs=[_full(gathered.shape), _full(gathered_pw.shape), _full(gathered_cw.shape)] + [_full(a.shape) for a in flat],
        out_specs=out_specs, out_shape=out_shape, compiler_params=_params("arbitrary"),
    )(gathered, gathered_pw, gathered_cw, *flat)
    out = {nme: tuple(res[4 * i:4 * i + 4]) for i, nme in enumerate(names + ["pool_w", "conv_w"])}
    return out, res[-2], res[-1][0, 0]


def _row_tile(s):
    return 512 if s % 512 == 0 else s


def kernel(x, c, ada_w, ada_b, norm1_g, w_in, pool_w, pool_b, pool_scale, q_norm_g, k_norm_g, attn_out_g, w_out, norm2_g, w_up, conv_w, conv_b, w_down, loss_target, m_ada_w, m_ada_b, m_norm1_g, m_w_in, m_pool_w, m_pool_b, m_pool_scale, m_q_norm_g, m_k_norm_g, m_attn_out_g, m_w_out, m_norm2_g, m_w_up, m_conv_w, m_conv_b, m_w_down, v_ada_w, v_ada_b, v_norm1_g, v_w_in, v_pool_w, v_pool_b, v_pool_scale, v_q_norm_g, v_k_norm_g, v_attn_out_g, v_w_out, v_norm2_g, v_w_up, v_conv_w, v_conv_b, v_w_down):
    ax, ay, ac = lax.axis_index("x"), lax.axis_index("y"), lax.axis_index("c")
    me = 4 * ax + 2 * ay + ac
    me_swapped = 4 * ay + 2 * ax + ac
    xs, tgt = x[0], loss_target[0]
    s = xs.shape[0]
    ts = _row_tile(s)
    tq_attn, tk_attn, hp_attn = 256, 256, 2
    tmm = 2 * ts
    bd = _block_diag_ones(DA, HD)

    w_in_t = w_in[0].T.astype(BF16)
    w_up_t = w_up[0].T.astype(BF16)
    c_all = _all_gather([jnp.broadcast_to(c, (SUBLANES, D))], [False], "gather_c")[0][:, 0, :]
    n_ada = ada_w.shape[2]
    ada_b_mine = lax.dynamic_slice_in_dim(ada_b, me * n_ada, n_ada, axis=1)
    mod_part = _ada_fwd(c_all, ada_w[0], ada_b_mine, "ada_fwd")
    mod_all = _all_gather([mod_part], [False], "gather_mod")[0]
    mod = lax.dynamic_index_in_dim(mod_all, me, axis=1, keepdims=False).reshape(1, 6 * D)
    shift1, scale1, gate1, shift2, scale2, gate2 = [mod[:, k * D:(k + 1) * D] for k in range(6)]

    later_w = [w_out[0].astype(BF16), w_up_t, w_down[0].astype(BF16)]
    cb_full = jnp.transpose(conv_b.reshape(1, 2, 2, 2, 704), (0, 2, 1, 3, 4)).reshape(1, 2 * DFF)

    qg = jnp.tile(q_norm_g, (1, DA // HD))
    kg = jnp.tile(k_norm_g, (1, DA // HD))
    ag = attn_out_g.reshape(1, DA)
    pw = pool_w[0].astype(BF16)
    pb = pool_b.reshape(1, DP)
    h1, (gw_in, gcw) = _ln_mod(xs, norm1_g, scale1, shift1, ts=ts, name="ln1",
                               side=_gather_side([w_in_t, jnp.pad(conv_w[0], ((0, 5), (0, 64)))], [False, True]))
    w_in_full = gw_in.reshape(DIN, D)
    cw_full = jnp.transpose(gcw[:, :3, :704], (1, 0, 2)).reshape(3, 2 * DFF)
    proj, qkv = _in_proj_qk_norm(h1, w_in_full, qg, kg, bd, tm=ts, name="in_proj_qk_norm")
    o_raw, m_tot, kb_first, (gw_out, gw_up, gw_down) = _attn_fwd(
        qkv, later_w, [False, True, False], tq=tq_attn, tk=tk_attn, hp=hp_attn, name="attn_fwd")
    w_out_full = gw_out.reshape(D, D)
    w_up_full = gw_up.reshape(2 * DFF, D)
    w_down_full = gw_down.reshape(DFF, D)
    mix = _pool_mix(proj, o_raw, pw, pb, pool_scale, ag, bd, ts=ts, name="pool_mix")
    att, x1, h2 = _proj_res_ln_mod(mix, w_out_full, xs, gate1, norm2_g, scale2, shift2, tm=ts, name="out_proj_ln2")
    up, conv, act = _up_conv_gate(h2, w_up_full, cw_full, cb_full, tm=ts, name="up_conv_gate")
    dy, dffn, dgate2_p, loss_p = _proj_loss_head(act, w_down_full, x1, tgt, gate2, tm=ts, name="down_proj_loss")

    g_w_down = _matmul(act, dffn, mode="tn", out_dtype=F32, tm=CF, tn=D, tk=tmm,name="down_wgrad")
    dconv, dcb_p = _down_bwd_gate(dffn, w_down_full, conv, tm=tmm, name="down_bwd_gate")
    dup, dh2, dcw_p = _conv_bwd_up_bwd(dconv, up, cw_full, w_up_full, tm=tmm, name="conv_bwd_up_bwd")
    g_w_up_t = _matmul(dup, h2, mode="tn", out_dtype=F32, tm=CF, tn=D, tk=tmm,name="up_wgrad")
    (dx1, datt, dshift2_p, dscale2_p, dnorm2_p, dgate1_p), _ = _ln_mod_bwd(
        dh2, x1, norm2_g, scale2, dy, att, gate1, ts=ts, name="ln2_bwd")

    dmix = _matmul(datt, w_out_full, mode="nt", out_dtype=BF16, tm=tmm,tn=D, tk=D, name="out_bwd")
    g_w_out = _matmul(mix, datt, mode="tn", out_dtype=F32, tm=D, tn=D, tk=tmm,name="out_wgrad")
    core = jnp.reshape(ac, (1,)).astype(jnp.int32)
    chip = jnp.reshape(2 * ax + ay, (1,)).astype(jnp.int32)
    big_ffn = [g_w_up_t.reshape(NDEV, 2 * DFF // NDEV, D), g_w_down.reshape(NDEV, DFF // NDEV, D),
               g_w_out.reshape(NDEV, D // NDEV, D)]
    swaps_ffn = [True, False, False]
    (du, do_raw, g_pw_p, dpb_p, dps_p, dag_p), gots_ffn = _mix_bwd(
        dmix, proj, o_raw, pw, pb, pool_scale, ag, bd, ts=ts, name="mix_bwd", side=_pair_side(big_ffn, swaps_ffn))
    sums_ffn = [_pair_sum(big_ffn[k], gots_ffn[k], swaps_ffn[k], core, "rs_pair_sum_ffn%d" % k) for k in range(3)]
    dqn, dkn, dvv, parts_ffn = _attn_bwd(qkv, do_raw, m_tot, kb_first, sums_ffn, tq=tq_attn, tk=tk_attn, hp=hp_attn, name="attn_bwd")
    dproj, dqg_p, dkg_p = _qk_norm_bwd(du, dqn, dkn, dvv, proj, qg, kg, bd, ts=ts, name="qk_norm_bwd")
    g_w_in_t = _matmul(dproj, h1, mode="tn", out_dtype=F32, tm=DIN // 2, tn=D, tk=tmm,name="in_wgrad")
    big = [g_w_in_t.reshape(NDEV, DIN // NDEV, D)]
    gots = _pair_exchange(big, [False], "rs_pair")
    sums = [_pair_sum(big[0], gots[0], False, core, "rs_pair_sum")]
    dh1, parts = _matmul(dproj, w_in_full, mode="nn", out_dtype=BF16, tm=tmm,tn=D, tk=DIN, name="in_bwd",
                         side=_chip_side(sums))
    (grad_x, dshift1_p, dscale1_p, dnorm1_p), _ = _ln_mod_bwd(
        dh1, xs, norm1_g, scale1, dx1, None, None, ts=ts, name="ln1_bwd")

    tr = lambda a: a[0].T
    r_in = _adamw_reduce(tr(w_in), tr(m_w_in), tr(v_w_in), sums[0], parts[0], chip, "adamw_w_in")
    r_out = _adamw_reduce(w_out[0], m_w_out[0], v_w_out[0], sums_ffn[2], parts_ffn[2], chip, "adamw_w_out")
    r_up = _adamw_reduce(tr(w_up), tr(m_w_up), tr(v_w_up), sums_ffn[0], parts_ffn[0], chip, "adamw_w_up")
    r_down = _adamw_reduce(w_down[0], m_w_down[0], v_w_down[0], sums_ffn[1], parts_ffn[1], chip, "adamw_w_down")
    r_in = [a.T[None] for a in r_in]
    r_up = [a.T[None] for a in r_up]
    r_out = [a[None] for a in r_out]
    r_down = [a[None] for a in r_down]

    dcb_nat = jnp.transpose(dcb_p.reshape(SUBLANES, 2, 2, 2, 704), (0, 2, 1, 3, 4)).reshape(SUBLANES, 2 * DFF)
    pieces = [dshift1_p, dscale1_p, dgate1_p, dshift2_p, dscale2_p, dgate2_p,
              dnorm1_p, dnorm2_p, dcb_nat, dpb_p, dps_p, dag_p, dqg_p, dkg_p, loss_p]
    n_vec = sum(p.shape[1] for p in pieces)
    packed = _pack_partials(pieces, dcw_p, "pack_partials")
    npack = packed.shape[1]
    gathered, gathered_pw = _all_gather([packed.reshape(SUBLANES, npack // SUBLANES), g_pw_p], [False, False], "gather_small")
    gathered = gathered.reshape(NDEV, npack)
    gathered_cw = lax.dynamic_index_in_dim(
        gathered[:, n_vec:n_vec + 6 * DFF].reshape(NDEV, 3, NDEV, 704), me_swapped, axis=2, keepdims=False)
    specs = [("ada_b", 0, 6 * D, False)]
    off = 6 * D
    for nme, width, fold in (("norm1_g", D, False), ("norm2_g", D, False), ("conv_b", 2 * DFF, False),
                             ("pool_b", DP, False), ("pool_scale", DP, False), ("attn_out_g", DA, False),
                             ("q_norm_g", DA, True), ("k_norm_g", DA, True)):
        specs.append((nme, off, width, fold))
        off += width
    small = {
        "ada_b": (ada_b, m_ada_b, v_ada_b),
        "norm1_g": (norm1_g, m_norm1_g, v_norm1_g), "norm2_g": (norm2_g, m_norm2_g, v_norm2_g),
        "conv_b": (conv_b, m_conv_b, v_conv_b),
        "pool_b": (pb, m_pool_b.reshape(1, DP), v_pool_b.reshape(1, DP)),
        "pool_scale": (pool_scale, m_pool_scale, v_pool_scale),
        "attn_out_g": (ag, m_attn_out_g.reshape(1, DA), v_attn_out_g.reshape(1, DA)),
        "q_norm_g": (q_norm_g, m_q_norm_g, v_q_norm_g), "k_norm_g": (k_norm_g, m_k_norm_g, v_k_norm_g),
        "pool_w": (pool_w.reshape(DP, LANES), m_pool_w.reshape(DP, LANES), v_pool_w.reshape(DP, LANES)),
        "conv_w": (conv_w[0], m_conv_w[0], v_conv_w[0]),
    }
    upd, dmod_all, loss = _small_update(gathered, gathered_pw, gathered_cw, specs, small, off, "small_update")
    g_ada_w = _ada_bwd(c_all, lax.dynamic_slice_in_dim(dmod_all, me * n_ada, n_ada, axis=1), "ada_bwd")
    r_ada = [g_ada_w] + [a[None] for a in _adamw(ada_w[0], m_ada_w[0], v_ada_w[0], g_ada_w[0], "adamw_ada_w")]

    shapes = {"ada_b": ada_b.shape, "norm1_g": norm1_g.shape, "pool_w": pool_w.shape, "pool_b": pool_b.shape,
              "pool_scale": pool_scale.shape, "q_norm_g": q_norm_g.shape, "k_norm_g": k_norm_g.shape,
              "attn_out_g": attn_out_g.shape, "norm2_g": norm2_g.shape, "conv_w": conv_w.shape, "conv_b": conv_b.shape}
    res = {nme: [a.reshape(shapes[nme]) for a in upd[nme]] for nme in shapes}
    res.update(ada_w=r_ada, w_in=r_in, w_out=r_out, w_up=r_up, w_down=r_down)
    names = ["ada_w", "ada_b", "norm1_g", "w_in", "pool_w", "pool_b", "pool_scale", "q_norm_g", "k_norm_g",
             "attn_out_g", "w_out", "norm2_g", "w_up", "conv_w", "conv_b", "w_down"]
    outs = [loss, grad_x[None]]
    for q in range(4):
        outs += [res[nme][q] for nme in names]
    return tuple(outs)
```

```python
import functools
import math

import numpy as np
import jax
import jax.numpy as jnp
from jax import lax
from jax.experimental import pallas as pl
from jax.experimental.pallas import tpu as pltpu

F32, BF16 = jnp.float32, jnp.bfloat16
D = 1024
DP = 512
DA = 512
HD = 64
DIN = DP + 3 * DA
DFF = 2816
POOL_WINDOWS = (2, 4, 8, 16)
HALO = 16
EPS = 1e-6
LANES = 128
SUBLANES = 8
NDEV = 8
VMEM_LIMIT = 56 * 1024 * 1024
MESH = pl.DeviceIdType.MESH

ADAM_LR, ADAM_B1, ADAM_B2, ADAM_EPS, ADAM_WD, ADAM_STEP = 0.001, 0.9, 0.999, 1e-08, 0.01, 10

NN = (((1,), (0,)), ((), ()))
NT = (((1,), (1,)), ((), ()))
TN = (((0,), (0,)), ((), ()))


def _params(*sem):
    return pltpu.CompilerParams(dimension_semantics=sem, vmem_limit_bytes=VMEM_LIMIT)


def _full(shape):
    nd = len(shape)
    return pl.BlockSpec(shape, lambda *_: (0,) * nd)


def _dot(a, b, dn=NN):
    return lax.dot_general(a, b, dn, preferred_element_type=F32)


def _split_dot(a, b, dn=NN):
    hi = a.astype(BF16)
    lo = (a - hi.astype(F32)).astype(BF16)
    return _dot(hi, b, dn) + _dot(lo, b, dn)


def _colsum8(v):
    r, n = v.shape
    return v.reshape(r // SUBLANES, SUBLANES, n).sum(axis=0)


def _block_diag_ones(n, blk):
    i = np.arange(n) // blk
    return jnp.asarray((i[:, None] == i[None, :]).astype(np.float32), BF16)


def _matmul(a, b, *, mode, out_dtype, tm, tn, tk, name, n_outer=False, side=None):
    if mode == "tn":
        K, M = a.shape
        N = b.shape[1]
    elif mode == "nt":
        M, K = a.shape
        N = b.shape[0]
    else:
        M, K = a.shape
        N = b.shape[1]
    tm, tn, tk = min(tm, M), min(tn, N), min(tk, K)
    assert M % tm == 0 and N % tn == 0 and K % tk == 0, (name, M, N, K, tm, tn, tk)
    nk = K // tk
    dn = {"nn": NN, "nt": NT, "tn": TN}[mode]

    def body(a_ref, b_ref, o_ref, *acc):
        if nk == 1:
            o_ref[...] = _dot(a_ref[...], b_ref[...], dn).astype(o_ref.dtype)
            return
        acc_ref, = acc
        k = pl.program_id(2)

        @pl.when(k == 0)
        def _():
            acc_ref[...] = jnp.zeros_like(acc_ref)

        acc_ref[...] += _dot(a_ref[...], b_ref[...], dn)

        @pl.when(k == nk - 1)
        def _():
            o_ref[...] = acc_ref[...].astype(o_ref.dtype)

    if n_outer:
        gi = lambda g: (g[1], g[0], g[2])
        grid = (N // tn, M // tm, nk)
    else:
        gi = lambda g: g
        grid = (M // tm, N // tn, nk)

    def amap(*g):
        i, j, k = gi(g)
        return (k, i) if mode == "tn" else (i, k)

    def bmap(*g):
        i, j, k = gi(g)
        return (j, k) if mode == "nt" else (k, j)

    def omap(*g):
        i, j, k = gi(g)
        return (i, j)

    a_blk = (tk, tm) if mode == "tn" else (tm, tk)
    b_blk = (tn, tk) if mode == "nt" else (tk, tn)
    acc_scratch = [] if nk == 1 else [pltpu.VMEM((tm, tn), F32)]
    if side is None:
        return pl.pallas_call(
            body, name=name, grid=grid,
            in_specs=[pl.BlockSpec(a_blk, amap), pl.BlockSpec(b_blk, bmap)],
            out_specs=pl.BlockSpec((tm, tn), omap),
            out_shape=jax.ShapeDtypeStruct((M, N), out_dtype),
            scratch_shapes=acc_scratch,
            compiler_params=_params("parallel", "parallel", "arbitrary"),
        )(a, b)

    ne = len(side.arrs)
    steps = grid[0] * grid[1] * grid[2]

    nsem = len(side.scratch)

    def with_side(*refs):
        e_in, e_out = refs[2:2 + ne], refs[3 + ne:3 + 2 * ne]
        sems = refs[len(refs) - nsem:]
        step = (pl.program_id(0) * grid[1] + pl.program_id(1)) * grid[2] + pl.program_id(2)

        @pl.when(step == 0)
        def _():
            side.start(e_in, e_out, *sems)

        body(refs[0], refs[1], refs[2 + ne], *refs[3 + 2 * ne:len(refs) - nsem])

        @pl.when(step == steps - 1)
        def _():
            side.finish(e_in, e_out, *sems)

    any_spec = pl.BlockSpec(memory_space=pl.ANY)
    res = pl.pallas_call(
        with_side, name=name, grid=grid,
        in_specs=[pl.BlockSpec(a_blk, amap), pl.BlockSpec(b_blk, bmap)] + [any_spec] * ne,
        out_specs=[pl.BlockSpec((tm, tn), omap)] + [any_spec] * ne,
        out_shape=[jax.ShapeDtypeStruct((M, N), out_dtype)] + side.out_shapes,
        scratch_shapes=acc_scratch + side.scratch,
        compiler_params=_params("arbitrary", "arbitrary", "arbitrary"),
    )(a, b, *side.arrs)
    return res[0], list(res[1:])


def _slot(swap, px, py, pc):
    return 4 * py + 2 * px + pc if swap else 4 * px + 2 * py + pc


class _Gather:
    def __init__(self, ins, outs, send, recv, loc, swaps):
        self.ins, self.outs, self.send, self.recv, self.loc, self.swaps = ins, outs, send, recv, loc, swaps
        x, y, c = lax.axis_index("x"), lax.axis_index("y"), lax.axis_index("c")
        self.me, self.sib = (x, y, c), (x, y, 1 - c)
        self.chips = [(1 - x, y), (x, 1 - y), (1 - x, 1 - y)]
        self.n = len(ins)

    @staticmethod
    def scratch(n):
        return [pltpu.SemaphoreType.DMA((7 * n,)), pltpu.SemaphoreType.DMA((7 * n,)), pltpu.SemaphoreType.DMA((n,))]

    def copy(self, a, k, blk, to, src=None):
        rows = self.outs[a].at[_slot(self.swaps[a], *blk)]
        return pltpu.make_async_remote_copy(
            src_ref=rows if src is None else src, dst_ref=rows,
            send_sem=self.send.at[7 * a + k], recv_sem=self.recv.at[7 * a + k], device_id=to, device_id_type=MESH)

    def mine(self, a):
        return pltpu.make_async_copy(self.ins[a], self.outs[a].at[_slot(self.swaps[a], *self.me)], self.loc.at[a])

    def first(self, a):
        c = self.me[2]
        return [self.copy(a, 0, self.me, self.sib, src=self.ins[a])] + [
            self.copy(a, 1 + j, self.me, (*chip, c), src=self.ins[a]) for j, chip in enumerate(self.chips)]

    def forwards(self, a):
        c = self.me[2]
        return [self.copy(a, 4 + j, (*chip, c), self.sib) for j, chip in enumerate(self.chips)]

    def start(self):
        for a in range(self.n):
            self.mine(a).start()
        for a in range(self.n):
            for cp in self.first(a):
                cp.start()

    def forward(self):
        c = self.me[2]
        for a in range(self.n):
            fwd = self.forwards(a)
            for j, chip in enumerate(self.chips):
                self.copy(a, 1 + j, (*chip, c), self.me).wait_recv()
                fwd[j].start()

    def finish(self):
        c = self.me[2]
        for a in range(self.n):
            self.copy(a, 0, self.sib, self.me).wait_recv()
            for j, chip in enumerate(self.chips):
                self.copy(a, 4 + j, (*chip, 1 - c), self.me).wait_recv()
        for a in range(self.n):
            for cp in self.first(a) + self.forwards(a):
                cp.wait_send()
            self.mine(a).wait()


def _all_gather(arrs, swaps, name):
    n = len(arrs)

    def body(*refs):
        g = _Gather(refs[:n], refs[n:2 * n], *refs[2 * n:], swaps)
        g.start()
        g.forward()
        g.finish()

    any_spec = pl.BlockSpec(memory_space=pl.ANY)
    return pl.pallas_call(
        body, name=name,
        in_specs=[any_spec] * n, out_specs=[any_spec] * n,
        out_shape=[jax.ShapeDtypeStruct((NDEV,) + a.shape, a.dtype) for a in arrs],
        scratch_shapes=_Gather.scratch(n),
    )(*arrs)


def _pair_copies(ins, gots, send, recv, swaps):
    x, y, c = lax.axis_index("x"), lax.axis_index("y"), lax.axis_index("c")
    return [pltpu.make_async_remote_copy(
        src_ref=ins[a].at[_slot(swaps[a], k // 2, k % 2, 1 - c)], dst_ref=gots[a].at[k],
        send_sem=send.at[4 * a + k], recv_sem=recv.at[4 * a + k], device_id=(x, y, 1 - c), device_id_type=MESH)
        for a in range(len(ins)) for k in range(4)]


def _pair_exchange(arrs, swaps, name):
    n = len(arrs)

    def body(*refs):
        rems = _pair_copies(refs[:n], refs[n:2 * n], *refs[2 * n:], swaps)
        for rc in rems:
            rc.start()
        for rc in rems:
            rc.wait_recv()
        for rc in rems:
            rc.wait_send()

    any_spec = pl.BlockSpec(memory_space=pl.ANY)
    return pl.pallas_call(
        body, name=name,
        in_specs=[any_spec] * n, out_specs=[any_spec] * n,
        out_shape=[jax.ShapeDtypeStruct((4,) + a.shape[1:], a.dtype) for a in arrs],
        scratch_shapes=[pltpu.SemaphoreType.DMA((4 * n,)), pltpu.SemaphoreType.DMA((4 * n,))],
    )(*arrs)


def _chip_copies(ins, outs, send, recv):
    x, y, c = lax.axis_index("x"), lax.axis_index("y"), lax.axis_index("c")
    chips = [(1 - x, y), (x, 1 - y), (1 - x, 1 - y)]
    return [pltpu.make_async_remote_copy(
        src_ref=ins[a].at[2 * px + py], dst_ref=outs[a].at[j], send_sem=send.at[3 * a + j], recv_sem=recv.at[3 * a + j],
        device_id=(px, py, c), device_id_type=MESH) for a in range(len(ins)) for j, (px, py) in enumerate(chips)]


def _chip_exchange(arrs, name):
    n = len(arrs)

    def body(*refs):
        rems = _chip_copies(refs[:n], refs[n:2 * n], *refs[2 * n:])
        for rc in rems:
            rc.start()
        for rc in rems:
            rc.wait_recv()
        for rc in rems:
            rc.wait_send()

    any_spec = pl.BlockSpec(memory_space=pl.ANY)
    return pl.pallas_call(
        body, name=name,
        in_specs=[any_spec] * n, out_specs=[any_spec] * n,
        out_shape=[jax.ShapeDtypeStruct((3,) + a.shape[1:], a.dtype) for a in arrs],
        scratch_shapes=[pltpu.SemaphoreType.DMA((3 * n,)), pltpu.SemaphoreType.DMA((3 * n,))],
    )(*arrs)


class _Side:
    def __init__(self, arrs, out_shapes, scratch, start, finish, mid=None):
        self.arrs, self.out_shapes, self.scratch = list(arrs), list(out_shapes), list(scratch)
        self.start, self.finish, self.mid = start, finish, mid


def _copies_side(arrs, out_shapes, n_copies, make):
    def start(ins, outs, *sems):
        for cp in make(ins, outs, *sems):
            cp.start()

    def finish(ins, outs, *sems):
        cps = make(ins, outs, *sems)
        for cp in cps:
            cp.wait_recv()
        for cp in cps:
            cp.wait_send()

    return _Side(arrs, out_shapes, [pltpu.SemaphoreType.DMA((n_copies,)), pltpu.SemaphoreType.DMA((n_copies,))], start, finish)


def _pair_side(arrs, swaps):
    return _copies_side(arrs, [jax.ShapeDtypeStruct((4,) + a.shape[1:], a.dtype) for a in arrs], 4 * len(arrs),
                        functools.partial(_pair_copies, swaps=swaps))


def _chip_side(arrs):
    return _copies_side(arrs, [jax.ShapeDtypeStruct((3,) + a.shape[1:], a.dtype) for a in arrs], 3 * len(arrs), _chip_copies)


def _gather_side(arrs, swaps):
    return _Side(arrs, [jax.ShapeDtypeStruct((NDEV,) + a.shape, a.dtype) for a in arrs], _Gather.scratch(len(arrs)),
                 start=lambda ins, outs, *sems: _Gather(ins, outs, *sems, swaps).start(),
                 mid=lambda ins, outs, *sems: _Gather(ins, outs, *sems, swaps).forward(),
                 finish=lambda ins, outs, *sems: _Gather(ins, outs, *sems, swaps).finish())


def _row_call(body, side, *, name, steps, in_specs, out_specs, out_shape, ins):
    if side is None:
        res = pl.pallas_call(body, name=name, grid=(steps,), in_specs=in_specs, out_specs=out_specs, out_shape=out_shape,
                             compiler_params=_params("arbitrary"))(*ins)
        return list(res), []
    n_in, n_out, ne = len(in_specs), len(out_specs), len(side.arrs)

    def wrapped(*refs):
        e_in = refs[n_in:n_in + ne]
        e_out = refs[n_in + ne + n_out:n_in + 2 * ne + n_out]
        sems = refs[n_in + 2 * ne + n_out:]
        i = pl.program_id(0)

        @pl.when(i == 0)
        def _():
            side.start(e_in, e_out, *sems)

        if side.mid is not None:
            @pl.when(i == steps // 2)
            def _():
                side.mid(e_in, e_out, *sems)

        body(*refs[:n_in], *refs[n_in + ne:n_in + ne + n_out])

        @pl.when(i == steps - 1)
        def _():
            side.finish(e_in, e_out, *sems)

    any_spec = pl.BlockSpec(memory_space=pl.ANY)
    res = pl.pallas_call(
        wrapped, name=name, grid=(steps,), in_specs=list(in_specs) + [any_spec] * ne,
        out_specs=list(out_specs) + [any_spec] * ne, out_shape=list(out_shape) + side.out_shapes,
        scratch_shapes=side.scratch,
        compiler_params=_params("arbitrary"))(*ins, *side.arrs)
    return list(res[:n_out]), list(res[n_out:])


def _pair_sum(grads, got, swap, core, name):
    _, r, c = got.shape
    tr = r if r <= 352 else r // 2

    def own_map(k, i, core_ref):
        return (_slot(swap, k // 2, k % 2, core_ref[0]), i, 0)

    def body(core_ref, a_ref, b_ref, o_ref):
        o_ref[...] = a_ref[...] + b_ref[...]

    spec = pl.BlockSpec((None, tr, c), lambda k, i, core_ref: (k, i, 0))
    return pl.pallas_call(
        body, name=name,
        grid_spec=pltpu.PrefetchScalarGridSpec(
            num_scalar_prefetch=1, grid=(4, r // tr),
            in_specs=[pl.BlockSpec((None, tr, c), own_map), spec], out_specs=spec),
        out_shape=jax.ShapeDtypeStruct(got.shape, got.dtype), compiler_params=_params("parallel", "parallel"),
    )(core, grads, got)


def _adamw_math(w, g, m, v):
    m = ADAM_B1 * m + (1.0 - ADAM_B1) * g
    v = ADAM_B2 * v + (1.0 - ADAM_B2) * (g * g)
    m_hat = m / (1.0 - ADAM_B1 ** ADAM_STEP)
    v_hat = v / (1.0 - ADAM_B2 ** ADAM_STEP)
    delta = -ADAM_LR * (m_hat / (jnp.sqrt(v_hat) + ADAM_EPS) + ADAM_WD * w)
    return delta, m, v


def _adamw_tile(r):
    for cand in (256, 352, 128):
        if r % cand == 0:
            return cand
    return r


def _adamw(w, m, v, g, name):
    r, c = w.shape
    tr = _adamw_tile(r)
    spec = pl.BlockSpec((tr, c), lambda i: (i, 0))

    def body(w_ref, m_ref, v_ref, g_ref, d_ref, nm_ref, nv_ref):
        d_ref[...], nm_ref[...], nv_ref[...] = _adamw_math(w_ref[...], g_ref[...], m_ref[...], v_ref[...])

    out = jax.ShapeDtypeStruct((r, c), F32)
    return pl.pallas_call(
        body, name=name, grid=(r // tr,), in_specs=[spec] * 4, out_specs=[spec] * 3, out_shape=[out] * 3,
        compiler_params=_params("parallel"),
    )(w, m, v, g)


def _adamw_reduce(w, m, v, sums, recv, chip, name):
    r, c = w.shape
    tr = _adamw_tile(r)
    spec = pl.BlockSpec((tr, c), lambda i, chip_ref: (i, 0))

    def body(chip_ref, w_ref, m_ref, v_ref, s_ref, p_ref, g_ref, d_ref, nm_ref, nv_ref):
        g = ((s_ref[...] + p_ref[0]) + p_ref[1]) + p_ref[2]
        g_ref[...] = g
        d_ref[...], nm_ref[...], nv_ref[...] = _adamw_math(w_ref[...], g, m_ref[...], v_ref[...])

    out = jax.ShapeDtypeStruct((r, c), F32)
    return pl.pallas_call(
        body, name=name,
        grid_spec=pltpu.PrefetchScalarGridSpec(
            num_scalar_prefetch=1, grid=(r // tr,),
            in_specs=[spec, spec, spec, pl.BlockSpec((None, tr, c), lambda i, chip_ref: (chip_ref[0], i, 0)),
                      pl.BlockSpec((3, tr, c), lambda i, chip_ref: (0, i, 0))],
            out_specs=[spec] * 4),
        out_shape=[out] * 4, compiler_params=_params("parallel"),
    )(chip, w, m, v, sums, recv)


def _vec(n):
    return pl.BlockSpec((1, n), lambda *_: (0, 0))


def _ln_mod(x, g, scale, shift, *, ts, name, side=None):
    s = x.shape[0]
    row = pl.BlockSpec((ts, D), lambda i: (i, 0))

    def body(x_ref, g_ref, sc_ref, sh_ref, h_ref):
        xv = x_ref[...]
        r = lax.rsqrt(jnp.mean(xv * xv, axis=-1, keepdims=True) + EPS)
        h = (xv * r) * g_ref[...]
        h_ref[...] = (h * (1.0 + sc_ref[...]) + sh_ref[...]).astype(BF16)

    (h,), extra = _row_call(body, side, name=name, steps=s // ts, in_specs=[row, _vec(D), _vec(D), _vec(D)],
                            out_specs=[row], out_shape=[jax.ShapeDtypeStruct((s, D), BF16)], ins=(x, g, scale, shift))
    return h, extra


def _proj_res_ln_mod(mix, w, x, gate, g, scale, shift, *, tm, name):
    s = x.shape[0]
    row = pl.BlockSpec((tm, D), lambda i: (i, 0))

    def body(m_ref, w_ref, x_ref, gt_ref, g_ref, sc_ref, sh_ref, a_ref, x1_ref, h_ref):
        att = _dot(m_ref[...], w_ref[...])
        a_ref[...] = att.astype(BF16)
        x1 = x_ref[...] + gt_ref[...] * att
        x1_ref[...] = x1
        r = lax.rsqrt(jnp.mean(x1 * x1, axis=-1, keepdims=True) + EPS)
        h = (x1 * r) * g_ref[...]
        h_ref[...] = (h * (1.0 + sc_ref[...]) + sh_ref[...]).astype(BF16)

    return pl.pallas_call(
        body, name=name, grid=(s // tm,), in_specs=[row, _full(w.shape), row] + [_vec(D)] * 4, out_specs=[row, row, row],
        out_shape=[jax.ShapeDtypeStruct((s, D), BF16), jax.ShapeDtypeStruct((s, D), F32), jax.ShapeDtypeStruct((s, D), BF16)],
        compiler_params=_params("parallel"),
    )(mix, w, x, gate, g, scale, shift)


def _proj_loss_head(act, w, x1, tgt, gate2, *, tm, name):
    s = x1.shape[0]
    n = s // tm
    row = pl.BlockSpec((tm, D), lambda i: (i, 0))
    acc8 = pl.BlockSpec((SUBLANES, D), lambda i: (0, 0))

    def body(a_ref, w_ref, x_ref, t_ref, g_ref, dy_ref, df_ref, dg_ref, loss_ref, lacc):
        i = pl.program_id(0)

        @pl.when(i == 0)
        def _():
            lacc[...] = jnp.zeros_like(lacc)
            dg_ref[...] = jnp.zeros_like(dg_ref)

        f = _dot(a_ref[...], w_ref[...])
        diff = x_ref[...] + g_ref[...] * f - t_ref[...]
        lacc[...] += _colsum8(diff * diff)
        dy = diff * (1.0 / D)
        dy_ref[...] = dy
        df_ref[...] = (dy * g_ref[...]).astype(BF16)
        dg_ref[...] += _colsum8(dy * f)

        @pl.when(i == n - 1)
        def _():
            loss_ref[...] = jnp.full((SUBLANES, LANES), (0.5 / D) * jnp.sum(lacc[...]), F32)

    return pl.pallas_call(
        body, name=name, grid=(n,),
        in_specs=[pl.BlockSpec((tm, act.shape[1]), lambda i: (i, 0)), _full(w.shape), row, row, _vec(D)],
        out_specs=[row, row, acc8, _full((SUBLANES, LANES))],
        out_shape=[jax.ShapeDtypeStruct((s, D), F32), jax.ShapeDtypeStruct((s, D), BF16),
                   jax.ShapeDtypeStruct((SUBLANES, D), F32), jax.ShapeDtypeStruct((SUBLANES, LANES), F32)],
        scratch_shapes=[pltpu.VMEM((SUBLANES, D), F32)], compiler_params=_params("arbitrary"),
    )(act, w, x1, tgt, gate2)


def _group_rsqrt(t, bd):
    return lax.rsqrt(_split_dot(t * t, bd) * (1.0 / HD) + EPS)


def _in_proj_qk_norm(h, w, qg, kg, bd, *, tm, name):
    s = h.shape[0]

    def body(h_ref, w_ref, qg_ref, kg_ref, bd_ref, p_ref, o_ref):
        bdv = bd_ref[...]
        hv = h_ref[...]
        p_ref[:, 0:DP] = _dot(hv, w_ref[0:DP, :], NT)
        q = _dot(hv, w_ref[DP:DP + DA, :], NT)
        p_ref[:, DP:DP + DA] = q
        o_ref[:, 0:DA] = (q * _group_rsqrt(q, bdv) * qg_ref[...]).astype(BF16)
        k = _dot(hv, w_ref[DP + DA:DP + 2 * DA, :], NT)
        p_ref[:, DP + DA:DP + 2 * DA] = k
        o_ref[:, DA:2 * DA] = (k * _group_rsqrt(k, bdv) * kg_ref[...]).astype(BF16)
        v = _dot(hv, w_ref[DP + 2 * DA:, :], NT)
        p_ref[:, DP + 2 * DA:] = v
        o_ref[:, 2 * DA:] = v.astype(BF16)

    return pl.pallas_call(
        body, name=name, grid=(s // tm,),
        in_specs=[pl.BlockSpec((tm, D), lambda i: (i, 0)), _full(w.shape), _vec(DA), _vec(DA), _full((DA, DA))],
        out_specs=[pl.BlockSpec((tm, DIN), lambda i: (i, 0)), pl.BlockSpec((tm, 3 * DA), lambda i: (i, 0))],
        out_shape=[jax.ShapeDtypeStruct((s, DIN), F32), jax.ShapeDtypeStruct((s, 3 * DA), BF16)],
        compiler_params=_params("parallel"),
    )(h, w, qg, kg, bd)


EXP_UNDERFLOW = -120.0


def _log_terms(z):
    neg_abs = lax.bitcast_convert_type(lax.bitcast_convert_type(z, jnp.uint32) | jnp.uint32(0x80000000), F32)
    b = jnp.minimum(z, 0.0) - jnp.log(1.0 + jnp.exp(neg_abs))
    return b, b - z


def _head_masks(rows):
    lane = lax.broadcasted_iota(jnp.int32, (rows, LANES), 1)
    return [lane < HD, lane >= HD]


def _attn_fwd(qkv, gather, swaps, *, tq, tk, hp, name):
    s = qkv.shape[0]
    nrep = tk // LANES
    ndiag = tq // tk
    ng = len(gather)
    nh, wl = 2 * hp, LANES * hp
    ngrp, nq = DA // wl, s // tq
    lanes = [slice(LANES * pp, LANES * (pp + 1)) for pp in range(hp)]

    def body(*refs):
        q_ref, k_ref, v_ref = refs[:3]
        g_in = refs[3:3 + ng]
        o_ref, tot_ref, first_ref = refs[3 + ng:6 + ng]
        g_out = refs[6 + ng:6 + 2 * ng]
        oacc, rc = refs[6 + 2 * ng:8 + 2 * ng]
        g_sems = refs[8 + 2 * ng:]
        i = pl.program_id(1)
        step_id = pl.program_id(0) * nq + i

        @pl.when(step_id == 0)
        def _():
            _Gather(g_in, g_out, *g_sems, swaps).start()

        @pl.when(step_id == (ngrp * nq * 3) // 4)
        def _():
            _Gather(g_in, g_out, *g_sems, swaps).forward()

        heads = _head_masks(tq)
        qs = [jnp.where(heads[a % 2], q_ref[:, lanes[a // 2]] * 0.125, 0.0).astype(BF16) for a in range(nh)]
        dif = lax.broadcasted_iota(jnp.int32, (tq, tk), 0) - lax.broadcasted_iota(jnp.int32, (tq, tk), 1)
        kr = lax.broadcasted_iota(jnp.int32, (tk, tk), 0)
        kc = lax.broadcasted_iota(jnp.int32, (tk, tk), 1)
        later =jnp.where(kr > kc, 1.0, 0.0).astype(BF16)
        oacc[...] = jnp.zeros_like(oacc)
        rc[...] = jnp.zeros_like(rc)

        def tile(kb, thr):
            rows = pl.ds(pl.multiple_of(kb * tk, tk), tk)
            ks = [k_ref[rows, ln] for ln in lanes]
            vs = [v_ref[rows, ln] for ln in lanes]
            qr = slice(0 if thr is None else thr, tq)
            rcv = [rc[a, qr, :] for a in range(nh)]
            zs = [_dot(qs[a][qr], ks[a // 2], NT) for a in range(nh)]
            bs, mbs = [], []
            for a in range(nh):
                b, m = _log_terms(zs[a])
                if thr is not None:
                    m = jnp.where(dif[qr] > thr, m, 0.0)
                bs.append(b)
                mbs.append(m.astype(BF16))
            rl = [_dot(mbs[a], later) for a in range(nh)]
            for a in range(nh):
                p = jnp.exp(bs[a] + (rl[a] + jnp.tile(rcv[a], (1, nrep))))
                if thr is not None:
                    p = jnp.where(dif[qr] > thr, p, 0.0)
                oacc[a, qr, :] += _dot(p.astype(BF16), vs[a // 2])
                rc[a, qr, :] = rcv[a] + (rl[a][:, 0:1] + mbs[a][:, 0:1].astype(F32))

        for d in reversed(range(ndiag)):
            tile(i * ndiag + d, d * tk)

        def live():
            top = rc[0]
            for a in range(1, nh):
                top = jnp.maximum(top, rc[a])
            return jnp.max(top) > EXP_UNDERFLOW

        def step(carry):
            kb, _ = carry
            tile(kb, None)
            return kb - 1, live()

        kb_end, _ = lax.while_loop(lambda cr: jnp.logical_and(cr[0] >= 0, cr[1]), step, (i * ndiag - 1, live()))
        first_ref[pl.program_id(0), i] = (kb_end + 1).astype(F32)
        for pp, ln in enumerate(lanes):
            o_ref[:, ln] = jnp.where(heads[0], oacc[2 * pp], oacc[2 * pp + 1])
            tot_ref[:, ln] = jnp.where(heads[0], rc[2 * pp], rc[2 * pp + 1])

        @pl.when(step_id == ngrp * nq - 1)
        def _():
            _Gather(g_in, g_out, *g_sems, swaps).finish()

    qspec = pl.BlockSpec((tq, wl), lambda p, i: (i, p))
    any_spec = pl.BlockSpec(memory_space=pl.ANY)
    res = pl.pallas_call(
        body, name=name, grid=(ngrp, nq),
        in_specs=[qspec,
                  pl.BlockSpec((s, wl), lambda p, i: (0, ngrp + p)),
                  pl.BlockSpec((s, wl), lambda p, i: (0, 2 * ngrp + p))] + [any_spec] * ng,
        out_specs=[qspec, qspec, pl.BlockSpec(memory_space=pltpu.SMEM)] + [any_spec] * ng,
        out_shape=[jax.ShapeDtypeStruct((s, DA), F32), jax.ShapeDtypeStruct((s, DA), F32),
                   jax.ShapeDtypeStruct((ngrp, nq), F32)]
        + [jax.ShapeDtypeStruct((NDEV,) + a.shape, a.dtype) for a in gather],
        scratch_shapes=[pltpu.VMEM((nh, tq, LANES), F32), pltpu.VMEM((nh, tq, LANES), F32)] + _Gather.scratch(ng),
        compiler_params=_params("arbitrary", "arbitrary"),
    )(qkv, qkv, qkv, *gather)
    return res[0], res[1], res[2], res[3:]


def _attn_bwd(qkv, do, tot, first, exchange, *, tq, tk, hp, name):
    s = qkv.shape[0]
    nrep = tk // LANES
    ndiag = tq // tk
    ne = len(exchange)
    nh, wl = 2 * hp, LANES * hp
    ngrp, nq = DA // wl, s // tq
    lanes = [slice(LANES * pp, LANES * (pp + 1)) for pp in range(hp)]

    def body(*refs):
        q_ref, k_ref, v_ref, do_ref, tot_ref, first_ref = refs[:6]
        e_in = refs[6:6 + ne]
        dq_ref, dk_ref, dv_ref = refs[6 + ne:9 + ne]
        e_out = refs[9 + ne:9 + 2 * ne]
        dqacc, rem, gc = refs[9 + 2 * ne:12 + 2 * ne]
        e_sems = refs[12 + 2 * ne:]
        i = pl.program_id(1)
        step_id = pl.program_id(0) * nq + i

        @pl.when(step_id == 0)
        def _():
            for cp in _chip_copies(e_in, e_out, *e_sems):
                cp.start()

        @pl.when(i == 0)
        def _():
            dk_ref[...] = jnp.zeros_like(dk_ref)
            dv_ref[...] = jnp.zeros_like(dv_ref)

        heads = _head_masks(tq)
        qs = [jnp.where(heads[a % 2], q_ref[:, lanes[a // 2]] * 0.125, 0.0).astype(BF16) for a in range(nh)]
        dob = [jnp.where(heads[a % 2], do_ref[:, lanes[a // 2]], 0.0).astype(BF16) for a in range(nh)]
        dif = lax.broadcasted_iota(jnp.int32, (tq, tk), 0) - lax.broadcasted_iota(jnp.int32, (tq, tk), 1)
        kr = lax.broadcasted_iota(jnp.int32, (tk, tk), 0)
        kc = lax.broadcasted_iota(jnp.int32, (tk, tk), 1)
        up_incl = jnp.where(kr <= kc, 1.0, 0.0).astype(BF16)
        up_strict = jnp.where(kr < kc, 1.0, 0.0).astype(BF16)
        dqacc[...] = jnp.zeros_like(dqacc)
        gc[...] = jnp.zeros_like(gc)
        for pp, ln in enumerate(lanes):
            totv = tot_ref[:, ln]
            swapped = pltpu.roll(totv, HD, axis=1)
            rem[2 * pp] = jnp.where(heads[0], totv, swapped)
            rem[2 * pp + 1] = jnp.where(heads[1], totv, swapped)

        def tile(kb, thr):
            rows = pl.ds(pl.multiple_of(kb * tk, tk), tk)
            ks = [k_ref[rows, ln] for ln in lanes]
            vs = [v_ref[rows, ln] for ln in lanes]
            qr = slice(0 if thr is None else thr, tq)
            remv = [rem[a, qr, :] for a in range(nh)]
            gcv = [gc[a, qr, :] for a in range(nh)]
            zs = [_dot(qs[a][qr], ks[a // 2], NT) for a in range(nh)]
            das = [_dot(dob[a][qr], vs[a // 2], NT) for a in range(nh)]
            bs, mbs = [], []
            for a in range(nh):
                b, m = _log_terms(zs[a])
                if thr is not None:
                    m = jnp.where(dif[qr] > thr, m, 0.0)
                bs.append(b)
                mbs.append(m.astype(BF16))
            pl_ = [_dot(mbs[a], up_incl) for a in range(nh)]
            ps, gs, gbs = [], [], []
            for a in range(nh):
                p = jnp.exp(bs[a] + (jnp.tile(remv[a], (1, nrep)) - pl_[a]))
                if thr is not None:
                    p = jnp.where(dif[qr] > thr, p, 0.0)
                g = p * das[a]
                ps.append(p.astype(BF16))
                gs.append(g)
                gbs.append(g.astype(BF16))
            cl = [_dot(gbs[a], up_strict) for a in range(nh)]
            dk_add = [jnp.zeros((tk, LANES), F32) for _ in range(hp)]
            dv_add = [jnp.zeros((tk, LANES), F32) for _ in range(hp)]
            for a in range(nh):
                dz = gs[a] - jnp.exp(bs[a]) * (gs[a] + (jnp.tile(gcv[a], (1, nrep)) + cl[a]))
                if thr is not None:
                    dz = jnp.where(dif[qr] > thr, dz, 0.0)
                dzb = dz.astype(BF16)
                dqacc[a, qr, :] += _dot(dzb, ks[a // 2])
                dk_add[a // 2] += _dot(dzb, qs[a][qr], TN)
                dv_add[a // 2] += _dot(ps[a], dob[a][qr], TN)
                rem[a, qr, :] = remv[a] - pl_[a][:, tk - 1:tk]
                gc[a, qr, :] = gcv[a] + (cl[a][:, tk - 1:tk] + gbs[a][:, tk - 1:tk].astype(F32))
            for pp, ln in enumerate(lanes):
                dk_ref[rows, ln] += dk_add[pp]
                dv_ref[rows, ln] += dv_add[pp]

        def step(kb, carry):
            tile(kb, None)
            return carry

        lax.fori_loop(first_ref[pl.program_id(0), i].astype(jnp.int32), i * ndiag, step, 0)
        for d in range(ndiag):
            tile(i * ndiag + d, d * tk)
        for pp, ln in enumerate(lanes):
            dq_ref[:, ln] = jnp.where(heads[0], dqacc[2 * pp], dqacc[2 * pp + 1]) * 0.125

        @pl.when(step_id == ngrp * nq - 1)
        def _():
            cps = _chip_copies(e_in, e_out, *e_sems)
            for cp in cps:
                cp.wait_recv()
            for cp in cps:
                cp.wait_send()

    qspec = pl.BlockSpec((tq, wl), lambda p, i: (i, p))
    full = pl.BlockSpec((s, wl), lambda p, i: (0, p))
    any_spec = pl.BlockSpec(memory_space=pl.ANY)
    out = jax.ShapeDtypeStruct((s, DA), F32)
    res = pl.pallas_call(
        body, name=name, grid=(ngrp, nq),
        in_specs=[qspec, pl.BlockSpec((s, wl), lambda p, i: (0, ngrp + p), pipeline_mode=pl.Buffered(1)),
                  pl.BlockSpec((s, wl), lambda p, i: (0, 2 * ngrp + p), pipeline_mode=pl.Buffered(1)), qspec, qspec,
                  pl.BlockSpec(memory_space=pltpu.SMEM)] + [any_spec] * ne,
        out_specs=[qspec, full, full] + [any_spec] * ne,
        out_shape=[out, out, out] + [jax.ShapeDtypeStruct((3,) + a.shape[1:], a.dtype) for a in exchange],
        scratch_shapes=[pltpu.VMEM((nh, tq, LANES), F32)] * 3
        + [pltpu.SemaphoreType.DMA((3 * ne,)), pltpu.SemaphoreType.DMA((3 * ne,))],
        compiler_params=_params("arbitrary", "arbitrary"),
    )(qkv, qkv, qkv, do, tot, first, *exchange)
    return res[0], res[1], res[2], res[3:]


def _shift_rows(v, k):
    return pltpu.roll(v, k % v.shape[0], axis=0)


def _pooled(u, uh, i, g, w, ts):
    halo = jnp.where(i > 0, uh, 0.0)
    ue = jnp.concatenate([halo, u], axis=0)
    acc, span = ue, 1
    while span < w:
        acc = acc + _shift_rows(acc, span)
        span *= 2
    tpos = i * ts + lax.broadcasted_iota(jnp.int32, (ts, 1), 0)
    cnt = jnp.minimum(tpos + 1, w).astype(F32)
    return acc[HALO:] / cnt - u


def _pool_mix(proj, o, pw, pb, ps, ag, bd, *, ts, name):
    s = proj.shape[0]
    hb = ts // HALO

    def body(u_ref, uh_ref, o_ref, pw_ref, pb_ref, ps_ref, ag_ref, bd_ref, mix_ref):
        i = pl.program_id(0)
        for g, w in enumerate(POOL_WINDOWS):
            cols = slice(g * LANES, (g + 1) * LANES)
            pooled = _pooled(u_ref[:, cols], uh_ref[:, cols], i, g, w, ts)
            yv = (_dot(pooled.astype(BF16), pw_ref[g]) + pb_ref[:, cols]) * ps_ref[:, cols]
            mix_ref[:, cols] = yv.astype(BF16)
        ov = o_ref[...]
        mix_ref[:, DP:] = (ov * _group_rsqrt(ov, bd_ref[...]) * ag_ref[...]).astype(BF16)

    return pl.pallas_call(
        body, name=name, grid=(s // ts,),
        in_specs=[pl.BlockSpec((ts, DP), lambda i: (i, 0)),
                  pl.BlockSpec((HALO, DP), lambda i: (jnp.maximum(i * hb - 1, 0), 0)),
                  pl.BlockSpec((ts, DA), lambda i: (i, 0)),
                  _full((4, LANES, LANES)), _vec(DP), _vec(DP), _vec(DA), _full((DA, DA))],
        out_specs=pl.BlockSpec((ts, D), lambda i: (i, 0)),
        out_shape=jax.ShapeDtypeStruct((s, D), BF16), compiler_params=_params("parallel"),
    )(proj, proj, o, pw, pb, ps, ag, bd)


CF = DFF // 2
MXU_COLS = 256


def _sub_chunks(width):
    return [(c0, min(MXU_COLS, width - c0)) for c0 in range(0, width, MXU_COLS)]


def _up_conv_gate(h2, w_up, cw, cb, *, tm, name):
    s = h2.shape[0]
    hb = tm // HALO

    def body(a_ref, ah_ref, w_ref, cw_ref, cb_ref, up_ref, c_ref, act_ref):
        i = pl.program_id(1)
        ext = jnp.concatenate([ah_ref[...], a_ref[...]], axis=0)
        live_halo = i > 0
        for c0, cwid in _sub_chunks(CF):
            conv = []
            for off in (c0, CF + c0):
                cols = slice(off, off + cwid)
                u = _dot(ext, w_ref[cols, :], NT)
                up_ref[:, cols] = u[HALO:].astype(BF16)
                row = lax.broadcasted_iota(jnp.int32, (HALO + tm, 1), 0)
                ue = jnp.where(jnp.logical_or(row >= HALO, live_halo), u, 0.0)
                y = cw_ref[2:3, cols] * ue + cw_ref[1:2, cols] * _shift_rows(ue, 1) + cw_ref[0:1, cols] * _shift_rows(ue, 2)
                cv = y[HALO:] + cb_ref[:, cols]
                c_ref[:, cols] = cv.astype(BF16)
                conv.append(cv)
            gt, vl = conv
            act_ref[:, c0:c0 + cwid] = (gt / (1.0 + jnp.exp(-gt)) * vl).astype(BF16)

    return pl.pallas_call(
        body, name=name, grid=(2, s // tm),
        in_specs=[pl.BlockSpec((tm, D), lambda j, i: (i, 0)),
                  pl.BlockSpec((HALO, D), lambda j, i: (jnp.maximum(i * hb - 1, 0), 0)),
                  pl.BlockSpec((2 * CF, D), lambda j, i: (j, 0)),
                  pl.BlockSpec((3, 2 * CF), lambda j, i: (0, j)), pl.BlockSpec((1, 2 * CF), lambda j, i: (0, j))],
        out_specs=[pl.BlockSpec((tm, 2 * CF), lambda j, i: (i, j)), pl.BlockSpec((tm, 2 * CF), lambda j, i: (i, j)),
                   pl.BlockSpec((tm, CF), lambda j, i: (i, j))],
        out_shape=[jax.ShapeDtypeStruct((s, 2 * DFF), BF16), jax.ShapeDtypeStruct((s, 2 * DFF), BF16),
                   jax.ShapeDtypeStruct((s, DFF), BF16)],
        compiler_params=_params("parallel", "parallel"),
    )(h2, h2, w_up, cw, cb)


def _down_bwd_gate(dffn, w_down, conv, *, tm, name):
    s = dffn.shape[0]

    def body(a_ref, w_ref, c_ref, d_ref, db_ref):
        i = pl.program_id(1)

        @pl.when(i == 0)
        def _():
            db_ref[...] = jnp.zeros_like(db_ref)

        a = a_ref[...]
        for c0, cwid in _sub_chunks(CF):
            gcols, vcols = slice(c0, c0 + cwid), slice(CF + c0, CF + c0 + cwid)
            da = _dot(a, w_ref[gcols, :], NT)
            gt, vl = c_ref[:, gcols].astype(F32), c_ref[:, vcols].astype(F32)
            sg = 1.0 / (1.0 + jnp.exp(-gt))
            dgt = da * vl * (sg * (1.0 + gt * (1.0 - sg)))
            dvl = da * (gt * sg)
            d_ref[:, gcols] = dgt.astype(BF16)
            d_ref[:, vcols] = dvl.astype(BF16)
            db_ref[:, gcols] += _colsum8(dgt)
            db_ref[:, vcols] += _colsum8(dvl)

    return pl.pallas_call(
        body, name=name, grid=(2, s // tm),
        in_specs=[pl.BlockSpec((tm, D), lambda j, i: (i, 0)), pl.BlockSpec((CF, D), lambda j, i: (j, 0)),
                  pl.BlockSpec((tm, 2 * CF), lambda j, i: (i, j))],
        out_specs=[pl.BlockSpec((tm, 2 * CF), lambda j, i: (i, j)), pl.BlockSpec((SUBLANES, 2 * CF), lambda j, i: (0, j))],
        out_shape=[jax.ShapeDtypeStruct((s, 2 * DFF), BF16), jax.ShapeDtypeStruct((SUBLANES, 2 * DFF), F32)],
        compiler_params=_params("parallel", "arbitrary"),
    )(dffn, w_down, conv)


def _conv_bwd_up_bwd(dc, up, cw, w_up, *, tm, name):
    s = up.shape[0]
    hb = tm // HALO
    nb = s // HALO
    nk = 2 * DFF // CF
    n = s // tm

    def body(d_ref, dn_ref, u_ref, cw_ref, w_ref, du_ref, dh_ref, dw_ref, acc, dwacc):
        i, k = pl.program_id(0), pl.program_id(1)

        @pl.when(jnp.logical_and(i == 0, k == 0))
        def _():
            dwacc[...] = jnp.zeros_like(dwacc)

        @pl.when(k == 0)
        def _():
            acc[...] = jnp.zeros_like(acc)

        live_next = i < n - 1
        part = None
        for c0, cwid in _sub_chunks(CF):
            cols = slice(c0, c0 + cwid)
            dcur = d_ref[:, cols].astype(F32)
            de = jnp.concatenate([dcur, jnp.where(live_next, dn_ref[:, cols].astype(F32), 0.0)], axis=0)
            d1 = _shift_rows(de, -1)[:tm]
            d2 = _shift_rows(de, -2)[:tm]
            du = (cw_ref[2:3, cols] * dcur + cw_ref[1:2, cols] * d1 + cw_ref[0:1, cols] * d2).astype(BF16)
            du_ref[:, cols] = du
            prod = _dot(du, w_ref[cols, :])
            part = prod if part is None else part + prod
            u = u_ref[:, cols].astype(F32)
            for tap, dsh in ((2, dcur), (1, d1), (0, d2)):
                dwacc[k, SUBLANES * tap:SUBLANES * (tap + 1), cols] += _colsum8(dsh * u)
        acc[...] += part

        @pl.when(k == nk - 1)
        def _():
            dh_ref[...] = acc[...].astype(dh_ref.dtype)

        @pl.when(jnp.logical_and(i == n - 1, k == nk - 1))
        def _():
            dw_ref[...] = dwacc[...]

    res = pl.pallas_call(
        body, name=name, grid=(n, nk),
        in_specs=[pl.BlockSpec((tm, CF), lambda i, k: (i, k)),
                  pl.BlockSpec((HALO, CF), lambda i, k: (jnp.minimum((i + 1) * hb, nb - 1), k)),
                  pl.BlockSpec((tm, CF), lambda i, k: (i, k)),
                  pl.BlockSpec((3, CF), lambda i, k: (0, k)),
                  pl.BlockSpec((CF, D), lambda i, k: (k, 0))],
        out_specs=[pl.BlockSpec((tm, CF), lambda i, k: (i, k)), pl.BlockSpec((tm, D), lambda i, k: (i, 0)),
                   _full((nk, 24, CF))],
        out_shape=[jax.ShapeDtypeStruct((s, 2 * DFF), BF16), jax.ShapeDtypeStruct((s, D), BF16),
                   jax.ShapeDtypeStruct((nk, 24, CF), F32)],
        scratch_shapes=[pltpu.VMEM((tm, D), F32), pltpu.VMEM((nk, 24, CF), F32)],
        compiler_params=_params("arbitrary", "arbitrary"),
    )(dc, dc, up, cw, w_up)
    return res[0], res[1], jnp.transpose(res[2], (1, 0, 2)).reshape(24, 2 * DFF)


def _ln_mod_bwd(dh, xin, g, scale, resid, extra, gate, *, ts, name, side=None):
    s = xin.shape[0]
    row = pl.BlockSpec((ts, D), lambda i: (i, 0))
    acc8 = pl.BlockSpec((SUBLANES, D), lambda i: (0, 0))
    with_gate = extra is not None

    def body(*refs):
        if with_gate:
            dh_ref, x_ref, g_ref, sc_ref, r_ref, e_ref, gt_ref, dx_ref, da_ref, dsh, dsc, dg, dgt = refs
        else:
            dh_ref, x_ref, g_ref, sc_ref, r_ref, dx_ref, dsh, dsc, dg = refs
        i = pl.program_id(0)

        @pl.when(i == 0)
        def _():
            for acc in (dsh, dsc, dg) + ((dgt,) if with_gate else ()):
                acc[...] = jnp.zeros_like(acc)

        xv, dhv = x_ref[...], dh_ref[...].astype(F32)
        r = lax.rsqrt(jnp.mean(xv * xv, axis=-1, keepdims=True) + EPS)
        xn = xv * r
        dsh[...] += _colsum8(dhv)
        dsc[...] += _colsum8(dhv * (xn * g_ref[...]))
        dhp = dhv * (1.0 + sc_ref[...])
        dg[...] += _colsum8(dhp * xn)
        dxn = dhp * g_ref[...]
        dx = r_ref[...] + r * (dxn - xn * jnp.mean(dxn * xn, axis=-1, keepdims=True))
        dx_ref[...] = dx
        if with_gate:
            da_ref[...] = (dx * gt_ref[...]).astype(BF16)
            dgt[...] += _colsum8(dx * e_ref[...].astype(F32))

    f32o, p8 = jax.ShapeDtypeStruct((s, D), F32), jax.ShapeDtypeStruct((SUBLANES, D), F32)
    if with_gate:
        ins, in_specs = (dh, xin, g, scale, resid, extra, gate), [row, row, _vec(D), _vec(D), row, row, _vec(D)]
        out_specs, out_shape = [row, row, acc8, acc8, acc8, acc8], [f32o, jax.ShapeDtypeStruct((s, D), BF16), p8, p8, p8, p8]
    else:
        ins, in_specs = (dh, xin, g, scale, resid), [row, row, _vec(D), _vec(D), row]
        out_specs, out_shape = [row, acc8, acc8, acc8], [f32o, p8, p8, p8]
    return _row_call(body, side, name=name, steps=s // ts, in_specs=in_specs, out_specs=out_specs,
                     out_shape=out_shape, ins=ins)


def _group_norm_bwd(t, dn_out, gvec, bd):
    r = _group_rsqrt(t, bd)
    dg_terms = dn_out * t * r
    dn = dn_out * gvec
    dt = r * (dn - t * (r * r) * (_split_dot(dn * t, bd) * (1.0 / HD)))
    return dt, dg_terms


def _mix_bwd(dmix, proj, o, pw, pb, ps, ag, bd, *, ts, name, side=None):
    s = proj.shape[0]
    hb = ts // HALO
    nb = s // HALO

    def body(dm_ref, dmn_ref, u_ref, uh_ref, o_ref, pw_ref, pb_ref, ps_ref, ag_ref, bd_ref,
             du_ref, do_ref, dpw_ref, dpb_ref, dps_ref, dag_ref):
        i = pl.program_id(0)
        n = s // ts

        @pl.when(i == 0)
        def _():
            for acc in (dpw_ref, dpb_ref, dps_ref, dag_ref):
                acc[...] = jnp.zeros_like(acc)

        for g, w in enumerate(POOL_WINDOWS):
            cols = slice(g * LANES, (g + 1) * LANES)
            wg = pw_ref[g]
            psg = ps_ref[:, cols]
            pooled = _pooled(u_ref[:, cols], uh_ref[:, cols], i, g, w, ts).astype(BF16)
            dy = dm_ref[:, cols].astype(F32)
            dps_ref[:, cols] += _colsum8(dy * (_dot(pooled, wg) + pb_ref[:, cols]))
            dpre = dy * psg
            dpb_ref[:, cols] += _colsum8(dpre)
            dpreb = dpre.astype(BF16)
            dpw_ref[g * LANES:(g + 1) * LANES, :] += _dot(pooled, dpreb, TN)
            dpool = _dot(dpreb, wg, NT)
            dnext = _dot((dmn_ref[:, cols].astype(F32) * psg).astype(BF16), wg, NT)
            dpe = jnp.concatenate([dpool, jnp.where(i < n - 1, dnext, 0.0)], axis=0)
            tpos = i * ts + lax.broadcasted_iota(jnp.int32, (ts + HALO, 1), 0)
            acc = dpe / jnp.minimum(tpos + 1, w).astype(F32)
            span = 1
            while span < w:
                acc = acc + _shift_rows(acc, -span)
                span *= 2
            du_ref[:, cols] = acc[:ts] - dpool
        ov = o_ref[...]
        dov, dg_terms = _group_norm_bwd(ov, dm_ref[:, DP:].astype(F32), ag_ref[...], bd_ref[...])
        do_ref[...] = dov
        dag_ref[...] += _colsum8(dg_terms)

    p8 = jax.ShapeDtypeStruct((SUBLANES, DP), F32)
    acc8 = pl.BlockSpec((SUBLANES, DP), lambda i: (0, 0))
    half = pl.BlockSpec((ts, DP), lambda i: (i, 0))
    return _row_call(
        body, side, name=name, steps=s // ts,
        in_specs=[pl.BlockSpec((ts, D), lambda i: (i, 0)),
                  pl.BlockSpec((HALO, DP), lambda i: (jnp.minimum((i + 1) * hb, nb - 1), 0)),
                  half, pl.BlockSpec((HALO, DP), lambda i: (jnp.maximum(i * hb - 1, 0), 0)),
                  half, _full((4, LANES, LANES)), _vec(DP), _vec(DP), _vec(DA), _full((DA, DA))],
        out_specs=[half, half, _full((DP, LANES)), acc8, acc8, acc8],
        out_shape=[jax.ShapeDtypeStruct((s, DP), F32), jax.ShapeDtypeStruct((s, DA), F32),
                   jax.ShapeDtypeStruct((DP, LANES), F32), p8, p8, p8],
        ins=(dmix, dmix, proj, proj, o, pw, pb, ps, ag, bd))


def _qk_norm_bwd(du, dq, dk, dv, proj, qg, kg, bd, *, ts, name):
    s = proj.shape[0]

    def body(du_ref, dq_ref, dk_ref, dv_ref, q_ref, k_ref, qg_ref, kg_ref, bd_ref, dp_ref, dqg_ref, dkg_ref):
        i = pl.program_id(0)

        @pl.when(i == 0)
        def _():
            dqg_ref[...] = jnp.zeros_like(dqg_ref)
            dkg_ref[...] = jnp.zeros_like(dkg_ref)

        bdv = bd_ref[...]
        dqr, tq = _group_norm_bwd(q_ref[...], dq_ref[...], qg_ref[...], bdv)
        dkr, tk = _group_norm_bwd(k_ref[...], dk_ref[...], kg_ref[...], bdv)
        dqg_ref[...] += _colsum8(tq)
        dkg_ref[...] += _colsum8(tk)
        dp_ref[:, 0:DP] = du_ref[...].astype(BF16)
        dp_ref[:, DP:DP + DA] = dqr.astype(BF16)
        dp_ref[:, DP + DA:DP + 2 * DA] = dkr.astype(BF16)
        dp_ref[:, DP + 2 * DA:] = dv_ref[...].astype(BF16)

    half = pl.BlockSpec((ts, DA), lambda i: (i, 0))
    col = lambda j: pl.BlockSpec((ts, DA), lambda i: (i, j))
    acc8 = pl.BlockSpec((SUBLANES, DA), lambda i: (0, 0))
    p8 = jax.ShapeDtypeStruct((SUBLANES, DA), F32)
    return pl.pallas_call(
        body, name=name, grid=(s // ts,),
        in_specs=[half, half, half, half, col(1), col(2), _vec(DA), _vec(DA), _full((DA, DA))],
        out_specs=[pl.BlockSpec((ts, DIN), lambda i: (i, 0)), acc8, acc8],
        out_shape=[jax.ShapeDtypeStruct((s, DIN), BF16), p8, p8],
        compiler_params=_params("arbitrary"),
    )(du, dq, dk, dv, proj, proj, qg, kg, bd)


def _split3(a):
    hi = a.astype(BF16)
    return hi, (a - hi.astype(F32)).astype(BF16)


def _dot3(a, b, dn):
    ah, al = _split3(a)
    bh, bl = _split3(b)
    return _dot(ah, bh, dn) + (_dot(ah, bl, dn) + _dot(al, bh, dn))


def _ada_fwd(c_all, w, b, name):
    nw = w.shape[1]

    def body(c_ref, w_ref, b_ref, o_ref):
        cv = c_ref[...]
        act = cv / (1.0 + jnp.exp(-cv))
        o_ref[...] = _dot3(act, w_ref[...], NN) + b_ref[...]

    return pl.pallas_call(
        body, name=name, in_specs=[_full((NDEV, D)), _full(w.shape), _full((1, nw))], out_specs=_full((NDEV, nw)),
        out_shape=jax.ShapeDtypeStruct((NDEV, nw), F32), grid=(1,), compiler_params=_params("arbitrary"),
    )(c_all, w, b)


def _ada_bwd(c_all, dmod, name):
    nw = dmod.shape[1]

    def body(c_ref, d_ref, o_ref):
        cv = c_ref[...]
        act = cv / (1.0 + jnp.exp(-cv))
        o_ref[...] = _dot3(act, d_ref[...], TN)[None]

    return pl.pallas_call(
        body, name=name, in_specs=[_full((NDEV, D)), _full((NDEV, nw))], out_specs=_full((1, D, nw)),
        out_shape=jax.ShapeDtypeStruct((1, D, nw), F32), grid=(1,), compiler_params=_params("arbitrary"),
    )(c_all, dmod)


def _fold_heads(v):
    acc = v[:, 0:HD]
    for h in range(1, DA // HD):
        acc = acc + v[:, h * HD:(h + 1) * HD]
    return acc


def _pack_partials(pieces, dcw_p, name):
    n_p = len(pieces)
    total = sum(p.shape[1] for p in pieces) + 3 * dcw_p.shape[1]
    npack = -(-total // (SUBLANES * LANES)) * (SUBLANES * LANES)

    def body(*refs):
        out = refs[-1]
        off = 0
        for r in refs[:n_p]:
            out[:, off:off + r.shape[1]] = jnp.sum(r[...], axis=0, keepdims=True)
            off += r.shape[1]
        dw = refs[n_p]
        for tap in range(3):
            out[:, off:off + dw.shape[1]] = jnp.sum(dw[SUBLANES * tap:SUBLANES * (tap + 1), :], axis=0, keepdims=True)
            off += dw.shape[1]
        if off < npack:
            out[:, off:] = jnp.zeros((1, npack - off), F32)

    arrs = list(pieces) + [dcw_p]
    return pl.pallas_call(
        body, name=name, grid=(1,), in_specs=[_full(a.shape) for a in arrs], out_specs=_full((1, npack)),
        out_shape=jax.ShapeDtypeStruct((1, npack), F32), compiler_params=_params("arbitrary"),
    )(*arrs)


def _small_update(gathered, gathered_pw, gathered_cw, specs, params, loss_off, name):
    names = [sp[0] for sp in specs]
    flat = []
    for nme in names + ["pool_w", "conv_w"]:
        flat += list(params[nme])
    n_in = len(flat)

    def body(*refs):
        ga_ref, gp_ref, gc_ref = refs[0], refs[1], refs[2]
        prm = refs[3:3 + n_in]
        outs = refs[3 + n_in:]
        total = ga_ref[0:1, :]
        for dv in range(1, NDEV):
            total = total + ga_ref[dv:dv + 1, :]
        k = 0
        for idx, (nme, off, width, fold) in enumerate(specs):
            g = total[:, off:off + width]
            if fold:
                g = _fold_heads(g)
            w_ref, m_ref, v_ref = prm[3 * idx:3 * idx + 3]
            d, nm, nv = _adamw_math(w_ref[...], g, m_ref[...], v_ref[...])
            for val in (g, d, nm, nv):
                outs[k][...] = val
                k += 1
        gpw = gp_ref[0]
        for dv in range(1, NDEV):
            gpw = gpw + gp_ref[dv]
        w_ref, m_ref, v_ref = prm[3 * len(specs):3 * len(specs) + 3]
        d, nm, nv = _adamw_math(w_ref[...], gpw, m_ref[...], v_ref[...])
        for val in (gpw, d, nm, nv):
            outs[k][...] = val
            k += 1
        gcw = gc_ref[0]
        for dv in range(1, NDEV):
            gcw = gcw + gc_ref[dv]
        w_ref, m_ref, v_ref = prm[3 * len(specs) + 3:3 * len(specs) + 6]
        d, nm, nv = _adamw_math(w_ref[...], gcw, m_ref[...], v_ref[...])
        for val in (gcw, d, nm, nv):
            outs[k][...] = val
            k += 1
        outs[k][...] = ga_ref[:, 0:6 * D]
        outs[k + 1][...] = total[:, loss_off:loss_off + LANES] * (1.0 / SUBLANES)

    out_shape, out_specs = [], []
    for nme in names + ["pool_w", "conv_w"]:
        shp = params[nme][0].shape
        out_shape += [jax.ShapeDtypeStruct(shp, F32)] * 4
        out_specs += [_full(shp)] * 4
    out_shape += [jax.ShapeDtypeStruct((NDEV, 6 * D), F32), jax.ShapeDtypeStruct((1, LANES), F32)]
    out_specs += [_full((NDEV, 6 * D)), _full((1, LANES))]
    res = pl.pallas_call(
        body, name=name, grid=(1,),
        in_specs=[_full(gathered.shape), _full(gathered_pw.shape), _full(gathered_cw.shape)] + [_full(a.shape) for a in flat],
        out_specs=out_specs, out_shape=out_shape, compiler_params=_params("arbitrary"),
    )(gathered, gathered_pw, gathered_cw, *flat)
    out = {nme: tuple(res[4 * i:4 * i + 4]) for i, nme in enumerate(names + ["pool_w", "conv_w"])}
    return out, res[-2], res[-1][0, 0]


def _row_tile(s):
    return 512 if s % 512 == 0 else s


def kernel(x, c, ada_w, ada_b, norm1_g, w_in, pool_w, pool_b, pool_scale, q_norm_g, k_norm_g, attn_out_g, w_out, norm2_g, w_up, conv_w, conv_b, w_down, loss_target, m_ada_w, m_ada_b, m_norm1_g, m_w_in, m_pool_w, m_pool_b, m_pool_scale, m_q_norm_g, m_k_norm_g, m_attn_out_g, m_w_out, m_norm2_g, m_w_up, m_conv_w, m_conv_b, m_w_down, v_ada_w, v_ada_b, v_norm1_g, v_w_in, v_pool_w, v_pool_b, v_pool_scale, v_q_norm_g, v_k_norm_g, v_attn_out_g, v_w_out, v_norm2_g, v_w_up, v_conv_w, v_conv_b, v_w_down):
    ax, ay, ac = lax.axis_index("x"), lax.axis_index("y"), lax.axis_index("c")
    me = 4 * ax + 2 * ay + ac
    me_swapped = 4 * ay + 2 * ax + ac
    xs, tgt = x[0], loss_target[0]
    s = xs.shape[0]
    ts = _row_tile(s)
    tq_attn, tk_attn, hp_attn = 256, 256, 2
    tmm = 2 * ts
    bd = _block_diag_ones(DA, HD)

    w_in_t = w_in[0].T.astype(BF16)
    w_up_t = w_up[0].T.astype(BF16)
    c_all = _all_gather([jnp.broadcast_to(c, (SUBLANES, D))], [False], "gather_c")[0][:, 0, :]
    n_ada = ada_w.shape[2]
    ada_b_mine = lax.dynamic_slice_in_dim(ada_b, me * n_ada, n_ada, axis=1)
    mod_part = _ada_fwd(c_all, ada_w[0], ada_b_mine, "ada_fwd")
    mod_all = _all_gather([mod_part], [False], "gather_mod")[0]
    mod = lax.dynamic_index_in_dim(mod_all, me, axis=1, keepdims=False).reshape(1, 6 * D)
    shift1, scale1, gate1, shift2, scale2, gate2 = [mod[:, k * D:(k + 1) * D] for k in range(6)]

    later_w = [w_out[0].astype(BF16), w_up_t, w_down[0].astype(BF16)]
    cb_full = jnp.transpose(conv_b.reshape(1, 2, 2, 2, 704), (0, 2, 1, 3, 4)).reshape(1, 2 * DFF)

    qg = jnp.tile(q_norm_g, (1, DA // HD))
    kg = jnp.tile(k_norm_g, (1, DA // HD))
    ag = attn_out_g.reshape(1, DA)
    pw = pool_w[0].astype(BF16)
    pb = pool_b.reshape(1, DP)
    h1, (gw_in, gcw) = _ln_mod(xs, norm1_g, scale1, shift1, ts=ts, name="ln1",
                               side=_gather_side([w_in_t, jnp.pad(conv_w[0], ((0, 5), (0, 64)))], [False, True]))
    w_in_full = gw_in.reshape(DIN, D)
    cw_full = jnp.transpose(gcw[:, :3, :704], (1, 0, 2)).reshape(3, 2 * DFF)
    proj, qkv = _in_proj_qk_norm(h1, w_in_full, qg, kg, bd, tm=ts, name="in_proj_qk_norm")
    o_raw, m_tot, kb_first, (gw_out, gw_up, gw_down) = _attn_fwd(
        qkv, later_w, [False, True, False], tq=tq_attn, tk=tk_attn, hp=hp_attn, name="attn_fwd")
    w_out_full = gw_out.reshape(D, D)
    w_up_full = gw_up.reshape(2 * DFF, D)
    w_down_full = gw_down.reshape(DFF, D)
    mix = _pool_mix(proj, o_raw, pw, pb, pool_scale, ag, bd, ts=ts, name="pool_mix")
    att, x1, h2 = _proj_res_ln_mod(mix, w_out_full, xs, gate1, norm2_g, scale2, shift2, tm=ts, name="out_proj_ln2")
    up, conv, act = _up_conv_gate(h2, w_up_full, cw_full, cb_full, tm=tmm, name="up_conv_gate")
    dy, dffn, dgate2_p, loss_p = _proj_loss_head(act, w_down_full, x1, tgt, gate2, tm=ts, name="down_proj_loss")

    g_w_down = _matmul(act, dffn, mode="tn", out_dtype=F32, tm=CF, tn=D, tk=tmm,name="down_wgrad")
    dconv, dcb_p = _down_bwd_gate(dffn, w_down_full, conv, tm=tmm, name="down_bwd_gate")
    dup, dh2, dcw_p = _conv_bwd_up_bwd(dconv, up, cw_full, w_up_full, tm=ts, name="conv_bwd_up_bwd")
    g_w_up_t = _matmul(dup, h2, mode="tn", out_dtype=F32, tm=CF, tn=D, tk=tmm,name="up_wgrad")
    (dx1, datt, dshift2_p, dscale2_p, dnorm2_p, dgate1_p), _ = _ln_mod_bwd(
        dh2, x1, norm2_g, scale2, dy, att, gate1, ts=ts, name="ln2_bwd")

    dmix = _matmul(datt, w_out_full, mode="nt", out_dtype=BF16, tm=tmm,tn=D, tk=D, name="out_bwd")
    g_w_out = _matmul(mix, datt, mode="tn", out_dtype=F32, tm=D, tn=D, tk=tmm,name="out_wgrad")
    core = jnp.reshape(ac, (1,)).astype(jnp.int32)
    chip = jnp.reshape(2 * ax + ay, (1,)).astype(jnp.int32)
    big_ffn = [g_w_up_t.reshape(NDEV, 2 * DFF // NDEV, D), g_w_down.reshape(NDEV, DFF // NDEV, D),
               g_w_out.reshape(NDEV, D // NDEV, D)]
    swaps_ffn = [True, False, False]
    (du, do_raw, g_pw_p, dpb_p, dps_p, dag_p), gots_ffn = _mix_bwd(
        dmix, proj, o_raw, pw, pb, pool_scale, ag, bd, ts=ts, name="mix_bwd", side=_pair_side(big_ffn, swaps_ffn))
    sums_ffn = [_pair_sum(big_ffn[k], gots_ffn[k], swaps_ffn[k], core, "rs_pair_sum_ffn%d" % k) for k in range(3)]
    dqn, dkn, dvv, parts_ffn = _attn_bwd(qkv, do_raw, m_tot, kb_first, sums_ffn, tq=tq_attn, tk=tk_attn, hp=hp_attn, name="attn_bwd")
    dproj, dqg_p, dkg_p = _qk_norm_bwd(du, dqn, dkn, dvv, proj, qg, kg, bd, ts=ts, name="qk_norm_bwd")
    g_w_in_t = _matmul(dproj, h1, mode="tn", out_dtype=F32, tm=DIN // 2, tn=D, tk=tmm,name="in_wgrad")
    big = [g_w_in_t.reshape(NDEV, DIN // NDEV, D)]
    gots = _pair_exchange(big, [False], "rs_pair")
    sums = [_pair_sum(big[0], gots[0], False, core, "rs_pair_sum")]
    dh1, parts = _matmul(dproj, w_in_full, mode="nn", out_dtype=BF16, tm=tmm,tn=D, tk=DIN, name="in_bwd",
                         side=_chip_side(sums))
    (grad_x, dshift1_p, dscale1_p, dnorm1_p), _ = _ln_mod_bwd(
        dh1, xs, norm1_g, scale1, dx1, None, None, ts=ts, name="ln1_bwd")

    tr = lambda a: a[0].T
    r_in = _adamw_reduce(tr(w_in), tr(m_w_in), tr(v_w_in), sums[0], parts[0], chip, "adamw_w_in")
    r_out = _adamw_reduce(w_out[0], m_w_out[0], v_w_out[0], sums_ffn[2], parts_ffn[2], chip, "adamw_w_out")
    r_up = _adamw_reduce(tr(w_up), tr(m_w_up), tr(v_w_up), sums_ffn[0], parts_ffn[0], chip, "adamw_w_up")
    r_down = _adamw_reduce(w_down[0], m_w_down[0], v_w_down[0], sums_ffn[1], parts_ffn[1], chip, "adamw_w_down")
    r_in = [a.T[None] for a in r_in]
    r_up = [a.T[None] for a in r_up]
    r_out = [a[None] for a in r_out]
    r_down = [a[None] for a in r_down]

    dcb_nat = jnp.transpose(dcb_p.reshape(SUBLANES, 2, 2, 2, 704), (0, 2, 1, 3, 4)).reshape(SUBLANES, 2 * DFF)
    pieces = [dshift1_p, dscale1_p, dgate1_p, dshift2_p, dscale2_p, dgate2_p,
              dnorm1_p, dnorm2_p, dcb_nat, dpb_p, dps_p, dag_p, dqg_p, dkg_p, loss_p]
    n_vec = sum(p.shape[1] for p in pieces)
    packed = _pack_partials(pieces, dcw_p, "pack_partials")
    npack = packed.shape[1]
    gathered, gathered_pw = _all_gather([packed.reshape(SUBLANES, npack // SUBLANES), g_pw_p], [False, False], "gather_small")
    gathered = gathered.reshape(NDEV, npack)
    gathered_cw = lax.dynamic_index_in_dim(
        gathered[:, n_vec:n_vec + 6 * DFF].reshape(NDEV, 3, NDEV, 704), me_swapped, axis=2, keepdims=False)
    specs = [("ada_b", 0, 6 * D, False)]
    off = 6 * D
    for nme, width, fold in (("norm1_g", D, False), ("norm2_g", D, False), ("conv_b", 2 * DFF, False),
                             ("pool_b", DP, False), ("pool_scale", DP, False), ("attn_out_g", DA, False),
                             ("q_norm_g", DA, True), ("k_norm_g", DA, True)):
        specs.append((nme, off, width, fold))
        off += width
    small = {
        "ada_b": (ada_b, m_ada_b, v_ada_b),
        "norm1_g": (norm1_g, m_norm1_g, v_norm1_g), "norm2_g": (norm2_g, m_norm2_g, v_norm2_g),
        "conv_b": (conv_b, m_conv_b, v_conv_b),
        "pool_b": (pb, m_pool_b.reshape(1, DP), v_pool_b.reshape(1, DP)),
        "pool_scale": (pool_scale, m_pool_scale, v_pool_scale),
        "attn_out_g": (ag, m_attn_out_g.reshape(1, DA), v_attn_out_g.reshape(1, DA)),
        "q_norm_g": (q_norm_g, m_q_norm_g, v_q_norm_g), "k_norm_g": (k_norm_g, m_k_norm_g, v_k_norm_g),
        "pool_w": (pool_w.reshape(DP, LANES), m_pool_w.reshape(DP, LANES), v_pool_w.reshape(DP, LANES)),
        "conv_w": (conv_w[0], m_conv_w[0], v_conv_w[0]),
    }
    upd, dmod_all, loss = _small_update(gathered, gathered_pw, gathered_cw, specs, small, off, "small_update")
    g_ada_w = _ada_bwd(c_all, lax.dynamic_slice_in_dim(dmod_all, me * n_ada, n_ada, axis=1), "ada_bwd")
    r_ada = [g_ada_w] + [a[None] for a in _adamw(ada_w[0], m_ada_w[0], v_ada_w[0], g_ada_w[0], "adamw_ada_w")]

    shapes = {"ada_b": ada_b.shape, "norm1_g": norm1_g.shape, "pool_w": pool_w.shape, "pool_b": pool_b.shape,
              "pool_scale": pool_scale.shape, "q_norm_g": q_norm_g.shape, "k_norm_g": k_norm_g.shape,
              "attn_out_g": attn_out_g.shape, "norm2_g": norm2_g.shape, "conv_w": conv_w.shape, "conv_b": conv_b.shape}
    res = {nme: [a.reshape(shapes[nme]) for a in upd[nme]] for nme in shapes}
    res.update(ada_w=r_ada, w_in=r_in, w_out=r_out, w_up=r_up, w_down=r_down)
    names = ["ada_w", "ada_b", "norm1_g", "w_in", "pool_w", "pool_b", "pool_scale", "q_norm_g", "k_norm_g",
             "attn_out_g", "w_out", "norm2_g", "w_up", "conv_w", "conv_b", "w_down"]
    outs = [loss, grad_x[None]]
    for q in range(4):
        outs += [res[nme][q] for nme in names]
    return tuple(outs)
```

```python
import functools
import math

import numpy as np
import jax
import jax.numpy as jnp
from jax import lax
from jax.experimental import pallas as pl
from jax.experimental.pallas import tpu as pltpu

F32, BF16 = jnp.float32, jnp.bfloat16
D = 1024
DP = 512
DA = 512
HD = 64
DIN = DP + 3 * DA
DFF = 2816
POOL_WINDOWS = (2, 4, 8, 16)
HALO = 16
EPS = 1e-6
LANES = 128
SUBLANES = 8
NDEV = 8
VMEM_LIMIT = 56 * 1024 * 1024
MESH = pl.DeviceIdType.MESH

ADAM_LR, ADAM_B1, ADAM_B2, ADAM_EPS, ADAM_WD, ADAM_STEP = 0.001, 0.9, 0.999, 1e-08, 0.01, 10

NN = (((1,), (0,)), ((), ()))
NT = (((1,), (1,)), ((), ()))
TN = (((0,), (0,)), ((), ()))


def _params(*sem):
    return pltpu.CompilerParams(dimension_semantics=sem, vmem_limit_bytes=VMEM_LIMIT)


def _full(shape):
    nd = len(shape)
    return pl.BlockSpec(shape, lambda *_: (0,) * nd)


def _dot(a, b, dn=NN):
    return lax.dot_general(a, b, dn, preferred_element_type=F32)


def _split_dot(a, b, dn=NN):
    hi = a.astype(BF16)
    lo = (a - hi.astype(F32)).astype(BF16)
    return _dot(hi, b, dn) + _dot(lo, b, dn)


def _colsum8(v):
    r, n = v.shape
    return v.reshape(r // SUBLANES, SUBLANES, n).sum(axis=0)


def _block_diag_ones(n, blk):
    i = np.arange(n) // blk
    return jnp.asarray((i[:, None] == i[None, :]).astype(np.float32), BF16)


def _matmul(a, b, *, mode, out_dtype, tm, tn, tk, name, n_outer=False, side=None):
    if mode == "tn":
        K, M = a.shape
        N = b.shape[1]
    elif mode == "nt":
        M, K = a.shape
        N = b.shape[0]
    else:
        M, K = a.shape
        N = b.shape[1]
    tm, tn, tk = min(tm, M), min(tn, N), min(tk, K)
    assert M % tm == 0 and N % tn == 0 and K % tk == 0, (name, M, N, K, tm, tn, tk)
    nk = K // tk
    dn = {"nn": NN, "nt": NT, "tn": TN}[mode]

    def body(a_ref, b_ref, o_ref, *acc):
        if nk == 1:
            o_ref[...] = _dot(a_ref[...], b_ref[...], dn).astype(o_ref.dtype)
            return
        acc_ref, = acc
        k = pl.program_id(2)

        @pl.when(k == 0)
        def _():
            acc_ref[...] = jnp.zeros_like(acc_ref)

        acc_ref[...] += _dot(a_ref[...], b_ref[...], dn)

        @pl.when(k == nk - 1)
        def _():
            o_ref[...] = acc_ref[...].astype(o_ref.dtype)

    if n_outer:
        gi = lambda g: (g[1], g[0], g[2])
        grid = (N // tn, M // tm, nk)
    else:
        gi = lambda g: g
        grid = (M // tm, N // tn, nk)

    def amap(*g):
        i, j, k = gi(g)
        return (k, i) if mode == "tn" else (i, k)

    def bmap(*g):
        i, j, k = gi(g)
        return (j, k) if mode == "nt" else (k, j)

    def omap(*g):
        i, j, k = gi(g)
        return (i, j)

    a_blk = (tk, tm) if mode == "tn" else (tm, tk)
    b_blk = (tn, tk) if mode == "nt" else (tk, tn)
    acc_scratch = [] if nk == 1 else [pltpu.VMEM((tm, tn), F32)]
    if side is None:
        return pl.pallas_call(
            body, name=name, grid=grid,
            in_specs=[pl.BlockSpec(a_blk, amap), pl.BlockSpec(b_blk, bmap)],
            out_specs=pl.BlockSpec((tm, tn), omap),
            out_shape=jax.ShapeDtypeStruct((M, N), out_dtype),
            scratch_shapes=acc_scratch,
            compiler_params=_params("parallel", "parallel", "arbitrary"),
        )(a, b)

    ne = len(side.arrs)
    steps = grid[0] * grid[1] * grid[2]

    nsem = len(side.scratch)

    def with_side(*refs):
        e_in, e_out = refs[2:2 + ne], refs[3 + ne:3 + 2 * ne]
        sems = refs[len(refs) - nsem:]
        step = (pl.program_id(0) * grid[1] + pl.program_id(1)) * grid[2] + pl.program_id(2)

        @pl.when(step == 0)
        def _():
            side.start(e_in, e_out, *sems)

        body(refs[0], refs[1], refs[2 + ne], *refs[3 + 2 * ne:len(refs) - nsem])

        @pl.when(step == steps - 1)
        def _():
            side.finish(e_in, e_out, *sems)

    any_spec = pl.BlockSpec(memory_space=pl.ANY)
    res = pl.pallas_call(
        with_side, name=name, grid=grid,
        in_specs=[pl.BlockSpec(a_blk, amap), pl.BlockSpec(b_blk, bmap)] + [any_spec] * ne,
        out_specs=[pl.BlockSpec((tm, tn), omap)] + [any_spec] * ne,
        out_shape=[jax.ShapeDtypeStruct((M, N), out_dtype)] + side.out_shapes,
        scratch_shapes=acc_scratch + side.scratch,
        compiler_params=_params("arbitrary", "arbitrary", "arbitrary"),
    )(a, b, *side.arrs)
    return res[0], list(res[1:])


def _slot(swap, px, py, pc):
    return 4 * py + 2 * px + pc if swap else 4 * px + 2 * py + pc


class _Gather:
    def __init__(self, ins, outs, send, recv, loc, swaps):
        self.ins, self.outs, self.send, self.recv, self.loc, self.swaps = ins, outs, send, recv, loc, swaps
        x, y, c = lax.axis_index("x"), lax.axis_index("y"), lax.axis_index("c")
        self.me, self.sib = (x, y, c), (x, y, 1 - c)
        self.chips = [(1 - x, y), (x, 1 - y), (1 - x, 1 - y)]
        self.n = len(ins)

    @staticmethod
    def scratch(n):
        return [pltpu.SemaphoreType.DMA((7 * n,)), pltpu.SemaphoreType.DMA((7 * n,)), pltpu.SemaphoreType.DMA((n,))]

    def copy(self, a, k, blk, to, src=None):
        rows = self.outs[a].at[_slot(self.swaps[a], *blk)]
        return pltpu.make_async_remote_copy(
            src_ref=rows if src is None else src, dst_ref=rows,
            send_sem=self.send.at[7 * a + k], recv_sem=self.recv.at[7 * a + k], device_id=to, device_id_type=MESH)

    def mine(self, a):
        return pltpu.make_async_copy(self.ins[a], self.outs[a].at[_slot(self.swaps[a], *self.me)], self.loc.at[a])

    def first(self, a):
        c = self.me[2]
        return [self.copy(a, 0, self.me, self.sib, src=self.ins[a])] + [
            self.copy(a, 1 + j, self.me, (*chip, c), src=self.ins[a]) for j, chip in enumerate(self.chips)]

    def forwards(self, a):
        c = self.me[2]
        return [self.copy(a, 4 + j, (*chip, c), self.sib) for j, chip in enumerate(self.chips)]

    def start(self):
        for a in range(self.n):
            self.mine(a).start()
        for a in range(self.n):
            for cp in self.first(a):
                cp.start()

    def forward(self):
        c = self.me[2]
        for a in range(self.n):
            fwd = self.forwards(a)
            for j, chip in enumerate(self.chips):
                self.copy(a, 1 + j, (*chip, c), self.me).wait_recv()
                fwd[j].start()

    def finish(self):
        c = self.me[2]
        for a in range(self.n):
            self.copy(a, 0, self.sib, self.me).wait_recv()
            for j, chip in enumerate(self.chips):
                self.copy(a, 4 + j, (*chip, 1 - c), self.me).wait_recv()
        for a in range(self.n):
            for cp in self.first(a) + self.forwards(a):
                cp.wait_send()
            self.mine(a).wait()


def _all_gather(arrs, swaps, name):
    n = len(arrs)

    def body(*refs):
        g = _Gather(refs[:n], refs[n:2 * n], *refs[2 * n:], swaps)
        g.start()
        g.forward()
        g.finish()

    any_spec = pl.BlockSpec(memory_space=pl.ANY)
    return pl.pallas_call(
        body, name=name,
        in_specs=[any_spec] * n, out_specs=[any_spec] * n,
        out_shape=[jax.ShapeDtypeStruct((NDEV,) + a.shape, a.dtype) for a in arrs],
        scratch_shapes=_Gather.scratch(n),
    )(*arrs)


def _pair_copies(ins, gots, send, recv, swaps):
    x, y, c = lax.axis_index("x"), lax.axis_index("y"), lax.axis_index("c")
    return [pltpu.make_async_remote_copy(
        src_ref=ins[a].at[_slot(swaps[a], k // 2, k % 2, 1 - c)], dst_ref=gots[a].at[k],
        send_sem=send.at[4 * a + k], recv_sem=recv.at[4 * a + k], device_id=(x, y, 1 - c), device_id_type=MESH)
        for a in range(len(ins)) for k in range(4)]


def _pair_exchange(arrs, swaps, name):
    n = len(arrs)

    def body(*refs):
        rems = _pair_copies(refs[:n], refs[n:2 * n], *refs[2 * n:], swaps)
        for rc in rems:
            rc.start()
        for rc in rems:
            rc.wait_recv()
        for rc in rems:
            rc.wait_send()

    any_spec = pl.BlockSpec(memory_space=pl.ANY)
    return pl.pallas_call(
        body, name=name,
        in_specs=[any_spec] * n, out_specs=[any_spec] * n,
        out_shape=[jax.ShapeDtypeStruct((4,) + a.shape[1:], a.dtype) for a in arrs],
        scratch_shapes=[pltpu.SemaphoreType.DMA((4 * n,)), pltpu.SemaphoreType.DMA((4 * n,))],
    )(*arrs)


def _chip_copies(ins, outs, send, recv):
    x, y, c = lax.axis_index("x"), lax.axis_index("y"), lax.axis_index("c")
    chips = [(1 - x, y), (x, 1 - y), (1 - x, 1 - y)]
    return [pltpu.make_async_remote_copy(
        src_ref=ins[a].at[2 * px + py], dst_ref=outs[a].at[j], send_sem=send.at[3 * a + j], recv_sem=recv.at[3 * a + j],
        device_id=(px, py, c), device_id_type=MESH) for a in range(len(ins)) for j, (px, py) in enumerate(chips)]


def _chip_exchange(arrs, name):
    n = len(arrs)

    def body(*refs):
        rems = _chip_copies(refs[:n], refs[n:2 * n], *refs[2 * n:])
        for rc in rems:
            rc.start()
        for rc in rems:
            rc.wait_recv()
        for rc in rems:
            rc.wait_send()

    any_spec = pl.BlockSpec(memory_space=pl.ANY)
    return pl.pallas_call(
        body, name=name,
        in_specs=[any_spec] * n, out_specs=[any_spec] * n,
        out_shape=[jax.ShapeDtypeStruct((3,) + a.shape[1:], a.dtype) for a in arrs],
        scratch_shapes=[pltpu.SemaphoreType.DMA((3 * n,)), pltpu.SemaphoreType.DMA((3 * n,))],
    )(*arrs)


class _Side:
    def __init__(self, arrs, out_shapes, scratch, start, finish, mid=None):
        self.arrs, self.out_shapes, self.scratch = list(arrs), list(out_shapes), list(scratch)
        self.start, self.finish, self.mid = start, finish, mid


def _copies_side(arrs, out_shapes, n_copies, make):
    def start(ins, outs, *sems):
        for cp in make(ins, outs, *sems):
            cp.start()

    def finish(ins, outs, *sems):
        cps = make(ins, outs, *sems)
        for cp in cps:
            cp.wait_recv()
        for cp in cps:
            cp.wait_send()

    return _Side(arrs, out_shapes, [pltpu.SemaphoreType.DMA((n_copies,)), pltpu.SemaphoreType.DMA((n_copies,))], start, finish)


def _pair_side(arrs, swaps):
    return _copies_side(arrs, [jax.ShapeDtypeStruct((4,) + a.shape[1:], a.dtype) for a in arrs], 4 * len(arrs),
                        functools.partial(_pair_copies, swaps=swaps))


def _chip_side(arrs):
    return _copies_side(arrs, [jax.ShapeDtypeStruct((3,) + a.shape[1:], a.dtype) for a in arrs], 3 * len(arrs), _chip_copies)


def _gather_side(arrs, swaps):
    return _Side(arrs, [jax.ShapeDtypeStruct((NDEV,) + a.shape, a.dtype) for a in arrs], _Gather.scratch(len(arrs)),
                 start=lambda ins, outs, *sems: _Gather(ins, outs, *sems, swaps).start(),
                 mid=lambda ins, outs, *sems: _Gather(ins, outs, *sems, swaps).forward(),
                 finish=lambda ins, outs, *sems: _Gather(ins, outs, *sems, swaps).finish())


def _row_call(body, side, *, name, steps, in_specs, out_specs, out_shape, ins):
    if side is None:
        res = pl.pallas_call(body, name=name, grid=(steps,), in_specs=in_specs, out_specs=out_specs, out_shape=out_shape,
                             compiler_params=_params("arbitrary"))(*ins)
        return list(res), []
    n_in, n_out, ne = len(in_specs), len(out_specs), len(side.arrs)

    def wrapped(*refs):
        e_in = refs[n_in:n_in + ne]
        e_out = refs[n_in + ne + n_out:n_in + 2 * ne + n_out]
        sems = refs[n_in + 2 * ne + n_out:]
        i = pl.program_id(0)

        @pl.when(i == 0)
        def _():
            side.start(e_in, e_out, *sems)

        if side.mid is not None:
            @pl.when(i == steps // 2)
            def _():
                side.mid(e_in, e_out, *sems)

        body(*refs[:n_in], *refs[n_in + ne:n_in + ne + n_out])

        @pl.when(i == steps - 1)
        def _():
            side.finish(e_in, e_out, *sems)

    any_spec = pl.BlockSpec(memory_space=pl.ANY)
    res = pl.pallas_call(
        wrapped, name=name, grid=(steps,), in_specs=list(in_specs) + [any_spec] * ne,
        out_specs=list(out_specs) + [any_spec] * ne, out_shape=list(out_shape) + side.out_shapes,
        scratch_shapes=side.scratch,
        compiler_params=_params("arbitrary"))(*ins, *side.arrs)
    return list(res[:n_out]), list(res[n_out:])


def _pair_sum(grads, got, swap, core, name):
    _, r, c = got.shape
    tr = r if r <= 352 else r // 2

    def own_map(k, i, core_ref):
        return (_slot(swap, k // 2, k % 2, core_ref[0]), i, 0)

    def body(core_ref, a_ref, b_ref, o_ref):
        o_ref[...] = a_ref[...] + b_ref[...]

    spec = pl.BlockSpec((None, tr, c), lambda k, i, core_ref: (k, i, 0))
    return pl.pallas_call(
        body, name=name,
        grid_spec=pltpu.PrefetchScalarGridSpec(
            num_scalar_prefetch=1, grid=(4, r // tr),
            in_specs=[pl.BlockSpec((None, tr, c), own_map), spec], out_specs=spec),
        out_shape=jax.ShapeDtypeStruct(got.shape, got.dtype), compiler_params=_params("parallel", "parallel"),
    )(core, grads, got)


def _adamw_math(w, g, m, v):
    m = ADAM_B1 * m + (1.0 - ADAM_B1) * g
    v = ADAM_B2 * v + (1.0 - ADAM_B2) * (g * g)
    m_hat = m / (1.0 - ADAM_B1 ** ADAM_STEP)
    v_hat = v / (1.0 - ADAM_B2 ** ADAM_STEP)
    delta = -ADAM_LR * (m_hat / (jnp.sqrt(v_hat) + ADAM_EPS) + ADAM_WD * w)
    return delta, m, v


def _adamw_tile(r):
    for cand in (256, 352, 128):
        if r % cand == 0:
            return cand
    return r


def _adamw(w, m, v, g, name):
    r, c = w.shape
    tr = _adamw_tile(r)
    spec = pl.BlockSpec((tr, c), lambda i: (i, 0))

    def body(w_ref, m_ref, v_ref, g_ref, d_ref, nm_ref, nv_ref):
        d_ref[...], nm_ref[...], nv_ref[...] = _adamw_math(w_ref[...], g_ref[...], m_ref[...], v_ref[...])

    out = jax.ShapeDtypeStruct((r, c), F32)
    return pl.pallas_call(
        body, name=name, grid=(r // tr,), in_specs=[spec] * 4, out_specs=[spec] * 3, out_shape=[out] * 3,
        compiler_params=_params("parallel"),
    )(w, m, v, g)


def _adamw_reduce(w, m, v, sums, recv, chip, name):
    r, c = w.shape
    tr = _adamw_tile(r)
    spec = pl.BlockSpec((tr, c), lambda i, chip_ref: (i, 0))

    def body(chip_ref, w_ref, m_ref, v_ref, s_ref, p_ref, g_ref, d_ref, nm_ref, nv_ref):
        g = ((s_ref[...] + p_ref[0]) + p_ref[1]) + p_ref[2]
        g_ref[...] = g
        d_ref[...], nm_ref[...], nv_ref[...] = _adamw_math(w_ref[...], g, m_ref[...], v_ref[...])

    out = jax.ShapeDtypeStruct((r, c), F32)
    return pl.pallas_call(
        body, name=name,
        grid_spec=pltpu.PrefetchScalarGridSpec(
            num_scalar_prefetch=1, grid=(r // tr,),
            in_specs=[spec, spec, spec, pl.BlockSpec((None, tr, c), lambda i, chip_ref: (chip_ref[0], i, 0)),
                      pl.BlockSpec((3, tr, c), lambda i, chip_ref: (0, i, 0))],
            out_specs=[spec] * 4),
        out_shape=[out] * 4, compiler_params=_params("parallel"),
    )(chip, w, m, v, sums, recv)


def _vec(n):
    return pl.BlockSpec((1, n), lambda *_: (0, 0))


def _ln_mod(x, g, scale, shift, *, ts, name, side=None):
    s = x.shape[0]
    row = pl.BlockSpec((ts, D), lambda i: (i, 0))

    def body(x_ref, g_ref, sc_ref, sh_ref, h_ref):
        xv = x_ref[...]
        r = lax.rsqrt(jnp.mean(xv * xv, axis=-1, keepdims=True) + EPS)
        h = (xv * r) * g_ref[...]
        h_ref[...] = (h * (1.0 + sc_ref[...]) + sh_ref[...]).astype(BF16)

    (h,), extra = _row_call(body, side, name=name, steps=s // ts, in_specs=[row, _vec(D), _vec(D), _vec(D)],
                            out_specs=[row], out_shape=[jax.ShapeDtypeStruct((s, D), BF16)], ins=(x, g, scale, shift))
    return h, extra


def _proj_res_ln_mod(mix, w, x, gate, g, scale, shift, *, tm, name):
    s = x.shape[0]
    row = pl.BlockSpec((tm, D), lambda i: (i, 0))

    def body(m_ref, w_ref, x_ref, gt_ref, g_ref, sc_ref, sh_ref, a_ref, x1_ref, h_ref):
        att = _dot(m_ref[...], w_ref[...])
        a_ref[...] = att.astype(BF16)
        x1 = x_ref[...] + gt_ref[...] * att
        x1_ref[...] = x1
        r = lax.rsqrt(jnp.mean(x1 * x1, axis=-1, keepdims=True) + EPS)
        h = (x1 * r) * g_ref[...]
        h_ref[...] = (h * (1.0 + sc_ref[...]) + sh_ref[...]).astype(BF16)

    return pl.pallas_call(
        body, name=name, grid=(s // tm,), in_specs=[row, _full(w.shape), row] + [_vec(D)] * 4, out_specs=[row, row, row],
        out_shape=[jax.ShapeDtypeStruct((s, D), BF16), jax.ShapeDtypeStruct((s, D), F32), jax.ShapeDtypeStruct((s, D), BF16)],
        compiler_params=_params("parallel"),
    )(mix, w, x, gate, g, scale, shift)


def _proj_loss_head(act, w, x1, tgt, gate2, *, tm, name):
    s = x1.shape[0]
    n = s // tm
    row = pl.BlockSpec((tm, D), lambda i: (i, 0))
    acc8 = pl.BlockSpec((SUBLANES, D), lambda i: (0, 0))

    def body(a_ref, w_ref, x_ref, t_ref, g_ref, dy_ref, df_ref, dg_ref, loss_ref, lacc):
        i = pl.program_id(0)

        @pl.when(i == 0)
        def _():
            lacc[...] = jnp.zeros_like(lacc)
            dg_ref[...] = jnp.zeros_like(dg_ref)

        f = _dot(a_ref[...], w_ref[...])
        diff = x_ref[...] + g_ref[...] * f - t_ref[...]
        lacc[...] += _colsum8(diff * diff)
        dy = diff * (1.0 / D)
        dy_ref[...] = dy
        df_ref[...] = (dy * g_ref[...]).astype(BF16)
        dg_ref[...] += _colsum8(dy * f)

        @pl.when(i == n - 1)
        def _():
            loss_ref[...] = jnp.full((SUBLANES, LANES), (0.5 / D) * jnp.sum(lacc[...]), F32)

    return pl.pallas_call(
        body, name=name, grid=(n,),
        in_specs=[pl.BlockSpec((tm, act.shape[1]), lambda i: (i, 0)), _full(w.shape), row, row, _vec(D)],
        out_specs=[row, row, acc8, _full((SUBLANES, LANES))],
        out_shape=[jax.ShapeDtypeStruct((s, D), F32), jax.ShapeDtypeStruct((s, D), BF16),
                   jax.ShapeDtypeStruct((SUBLANES, D), F32), jax.ShapeDtypeStruct((SUBLANES, LANES), F32)],
        scratch_shapes=[pltpu.VMEM((SUBLANES, D), F32)], compiler_params=_params("arbitrary"),
    )(act, w, x1, tgt, gate2)


def _group_rsqrt(t, bd):
    return lax.rsqrt(_split_dot(t * t, bd) * (1.0 / HD) + EPS)


def _in_proj_qk_norm(h, w, qg, kg, bd, *, tm, name):
    s = h.shape[0]

    def body(h_ref, w_ref, qg_ref, kg_ref, bd_ref, p_ref, o_ref):
        bdv = bd_ref[...]
        hv = h_ref[...]
        p_ref[:, 0:DP] = _dot(hv, w_ref[0:DP, :], NT)
        q = _dot(hv, w_ref[DP:DP + DA, :], NT)
        p_ref[:, DP:DP + DA] = q
        o_ref[:, 0:DA] = (q * _group_rsqrt(q, bdv) * qg_ref[...]).astype(BF16)
        k = _dot(hv, w_ref[DP + DA:DP + 2 * DA, :], NT)
        p_ref[:, DP + DA:DP + 2 * DA] = k
        o_ref[:, DA:2 * DA] = (k * _group_rsqrt(k, bdv) * kg_ref[...]).astype(BF16)
        v = _dot(hv, w_ref[DP + 2 * DA:, :], NT)
        p_ref[:, DP + 2 * DA:] = v
        o_ref[:, 2 * DA:] = v.astype(BF16)

    return pl.pallas_call(
        body, name=name, grid=(s // tm,),
        in_specs=[pl.BlockSpec((tm, D), lambda i: (i, 0)), _full(w.shape), _vec(DA), _vec(DA), _full((DA, DA))],
        out_specs=[pl.BlockSpec((tm, DIN), lambda i: (i, 0)), pl.BlockSpec((tm, 3 * DA), lambda i: (i, 0))],
        out_shape=[jax.ShapeDtypeStruct((s, DIN), F32), jax.ShapeDtypeStruct((s, 3 * DA), BF16)],
        compiler_params=_params("parallel"),
    )(h, w, qg, kg, bd)


EXP_UNDERFLOW = -120.0


def _log_terms(z):
    neg_abs = lax.bitcast_convert_type(lax.bitcast_convert_type(z, jnp.uint32) | jnp.uint32(0x80000000), F32)
    b = jnp.minimum(z, 0.0) - jnp.log(1.0 + jnp.exp(neg_abs))
    return b, b - z


def _head_masks(rows):
    lane = lax.broadcasted_iota(jnp.int32, (rows, LANES), 1)
    return [lane < HD, lane >= HD]


def _attn_fwd(qkv, gather, swaps, *, tq, tk, hp, name):
    s = qkv.shape[0]
    nrep = tk // LANES
    ndiag = tq // tk
    ng = len(gather)
    nh, wl = 2 * hp, LANES * hp
    ngrp, nq = DA // wl, s // tq
    lanes = [slice(LANES * pp, LANES * (pp + 1)) for pp in range(hp)]

    def body(*refs):
        q_ref, k_ref, v_ref = refs[:3]
        g_in = refs[3:3 + ng]
        o_ref, tot_ref, first_ref = refs[3 + ng:6 + ng]
        g_out = refs[6 + ng:6 + 2 * ng]
        oacc, rc = refs[6 + 2 * ng:8 + 2 * ng]
        g_sems = refs[8 + 2 * ng:]
        i = pl.program_id(1)
        step_id = pl.program_id(0) * nq + i

        @pl.when(step_id == 0)
        def _():
            _Gather(g_in, g_out, *g_sems, swaps).start()

        @pl.when(step_id == (ngrp * nq * 3) // 4)
        def _():
            _Gather(g_in, g_out, *g_sems, swaps).forward()

        heads = _head_masks(tq)
        qs = [jnp.where(heads[a % 2], q_ref[:, lanes[a // 2]] * 0.125, 0.0).astype(BF16) for a in range(nh)]
        dif = lax.broadcasted_iota(jnp.int32, (tq, tk), 0) - lax.broadcasted_iota(jnp.int32, (tq, tk), 1)
        kr = lax.broadcasted_iota(jnp.int32, (tk, tk), 0)
        kc = lax.broadcasted_iota(jnp.int32, (tk, tk), 1)
        later =jnp.where(kr > kc, 1.0, 0.0).astype(BF16)
        oacc[...] = jnp.zeros_like(oacc)
        rc[...] = jnp.zeros_like(rc)

        def tile(kb, thr):
            rows = pl.ds(pl.multiple_of(kb * tk, tk), tk)
            ks = [k_ref[rows, ln] for ln in lanes]
            vs = [v_ref[rows, ln] for ln in lanes]
            qr = slice(0 if thr is None else thr, tq)
            rcv = [rc[a, qr, :] for a in range(nh)]
            zs = [_dot(qs[a][qr], ks[a // 2], NT) for a in range(nh)]
            bs, mbs = [], []
            for a in range(nh):
                b, m = _log_terms(zs[a])
                if thr is not None:
                    m = jnp.where(dif[qr] > thr, m, 0.0)
                bs.append(b)
                mbs.append(m.astype(BF16))
            rl = [_dot(mbs[a], later) for a in range(nh)]
            for a in range(nh):
                p = jnp.exp(bs[a] + (rl[a] + jnp.tile(rcv[a], (1, nrep))))
                if thr is not None:
                    p = jnp.where(dif[qr] > thr, p, 0.0)
                oacc[a, qr, :] += _dot(p.astype(BF16), vs[a // 2])
                rc[a, qr, :] = rcv[a] + (rl[a][:, 0:1] + mbs[a][:, 0:1].astype(F32))

        for d in reversed(range(ndiag)):
            tile(i * ndiag + d, d * tk)

        def live():
            top = rc[0]
            for a in range(1, nh):
                top = jnp.maximum(top, rc[a])
            return jnp.max(top) > EXP_UNDERFLOW

        def step(carry):
            kb, _ = carry
            tile(kb, None)
            return kb - 1, live()

        kb_end, _ = lax.while_loop(lambda cr: jnp.logical_and(cr[0] >= 0, cr[1]), step, (i * ndiag - 1, live()))
        first_ref[pl.program_id(0), i] = (kb_end + 1).astype(F32)
        for pp, ln in enumerate(lanes):
            o_ref[:, ln] = jnp.where(heads[0], oacc[2 * pp], oacc[2 * pp + 1])
            tot_ref[:, ln] = jnp.where(heads[0], rc[2 * pp], rc[2 * pp + 1])

        @pl.when(step_id == ngrp * nq - 1)
        def _():
            _Gather(g_in, g_out, *g_sems, swaps).finish()

    qspec = pl.BlockSpec((tq, wl), lambda p, i: (i, p))
    any_spec = pl.BlockSpec(memory_space=pl.ANY)
    res = pl.pallas_call(
        body, name=name, grid=(ngrp, nq),
        in_specs=[qspec,
                  pl.BlockSpec((s, wl), lambda p, i: (0, ngrp + p)),
                  pl.BlockSpec((s, wl), lambda p, i: (0, 2 * ngrp + p))] + [any_spec] * ng,
        out_specs=[qspec, qspec, pl.BlockSpec(memory_space=pltpu.SMEM)] + [any_spec] * ng,
        out_shape=[jax.ShapeDtypeStruct((s, DA), F32), jax.ShapeDtypeStruct((s, DA), F32),
                   jax.ShapeDtypeStruct((ngrp, nq), F32)]
        + [jax.ShapeDtypeStruct((NDEV,) + a.shape, a.dtype) for a in gather],
        scratch_shapes=[pltpu.VMEM((nh, tq, LANES), F32), pltpu.VMEM((nh, tq, LANES), F32)] + _Gather.scratch(ng),
        compiler_params=_params("arbitrary", "arbitrary"),
    )(qkv, qkv, qkv, *gather)
    return res[0], res[1], res[2], res[3:]


def _attn_bwd(qkv, do, tot, first, exchange, *, tq, tk, hp, name):
    s = qkv.shape[0]
    nrep = tk // LANES
    ndiag = tq // tk
    ne = len(exchange)
    nh, wl = 2 * hp, LANES * hp
    ngrp, nq = DA // wl, s // tq
    lanes = [slice(LANES * pp, LANES * (pp + 1)) for pp in range(hp)]

    def body(*refs):
        q_ref, k_ref, v_ref, do_ref, tot_ref, first_ref = refs[:6]
        e_in = refs[6:6 + ne]
        dq_ref, dk_ref, dv_ref = refs[6 + ne:9 + ne]
        e_out = refs[9 + ne:9 + 2 * ne]
        dqacc, rem, gc = refs[9 + 2 * ne:12 + 2 * ne]
        e_sems = refs[12 + 2 * ne:]
        i = pl.program_id(1)
        step_id = pl.program_id(0) * nq + i

        @pl.when(step_id == 0)
        def _():
            for cp in _chip_copies(e_in, e_out, *e_sems):
                cp.start()

        @pl.when(i == 0)
        def _():
            dk_ref[...] = jnp.zeros_like(dk_ref)
            dv_ref[...] = jnp.zeros_like(dv_ref)

        heads = _head_masks(tq)
        qs = [jnp.where(heads[a % 2], q_ref[:, lanes[a // 2]] * 0.125, 0.0).astype(BF16) for a in range(nh)]
        dob = [jnp.where(heads[a % 2], do_ref[:, lanes[a // 2]], 0.0).astype(BF16) for a in range(nh)]
        dif = lax.broadcasted_iota(jnp.int32, (tq, tk), 0) - lax.broadcasted_iota(jnp.int32, (tq, tk), 1)
        kr = lax.broadcasted_iota(jnp.int32, (tk, tk), 0)
        kc = lax.broadcasted_iota(jnp.int32, (tk, tk), 1)
        up_incl = jnp.where(kr <= kc, 1.0, 0.0).astype(BF16)
        up_strict = jnp.where(kr < kc, 1.0, 0.0).astype(BF16)
        dqacc[...] = jnp.zeros_like(dqacc)
        gc[...] = jnp.zeros_like(gc)
        for pp, ln in enumerate(lanes):
            totv = tot_ref[:, ln]
            swapped = pltpu.roll(totv, HD, axis=1)
            rem[2 * pp] = jnp.where(heads[0], totv, swapped)
            rem[2 * pp + 1] = jnp.where(heads[1], totv, swapped)

        def tile(kb, thr):
            rows = pl.ds(pl.multiple_of(kb * tk, tk), tk)
            ks = [k_ref[rows, ln] for ln in lanes]
            vs = [v_ref[rows, ln] for ln in lanes]
            qr = slice(0 if thr is None else thr, tq)
            remv = [rem[a, qr, :] for a in range(nh)]
            gcv = [gc[a, qr, :] for a in range(nh)]
            zs = [_dot(qs[a][qr], ks[a // 2], NT) for a in range(nh)]
            das = [_dot(dob[a][qr], vs[a // 2], NT) for a in range(nh)]
            bs, mbs = [], []
            for a in range(nh):
                b, m = _log_terms(zs[a])
                if thr is not None:
                    m = jnp.where(dif[qr] > thr, m, 0.0)
                bs.append(b)
                mbs.append(m.astype(BF16))
            pl_ = [_dot(mbs[a], up_incl) for a in range(nh)]
            ps, gs, gbs = [], [], []
            for a in range(nh):
                p = jnp.exp(bs[a] + (jnp.tile(remv[a], (1, nrep)) - pl_[a]))
                if thr is not None:
                    p = jnp.where(dif[qr] > thr, p, 0.0)
                g = p * das[a]
                ps.append(p.astype(BF16))
                gs.append(g)
                gbs.append(g.astype(BF16))
            cl = [_dot(gbs[a], up_strict) for a in range(nh)]
            dk_add = [jnp.zeros((tk, LANES), F32) for _ in range(hp)]
            dv_add = [jnp.zeros((tk, LANES), F32) for _ in range(hp)]
            for a in range(nh):
                dz = gs[a] - jnp.exp(bs[a]) * (gs[a] + (jnp.tile(gcv[a], (1, nrep)) + cl[a]))
                if thr is not None:
                    dz = jnp.where(dif[qr] > thr, dz, 0.0)
                dzb = dz.astype(BF16)
                dqacc[a, qr, :] += _dot(dzb, ks[a // 2])
                dk_add[a // 2] += _dot(dzb, qs[a][qr], TN)
                dv_add[a // 2] += _dot(ps[a], dob[a][qr], TN)
                rem[a, qr, :] = remv[a] - pl_[a][:, tk - 1:tk]
                gc[a, qr, :] = gcv[a] + (cl[a][:, tk - 1:tk] + gbs[a][:, tk - 1:tk].astype(F32))
            for pp, ln in enumerate(lanes):
                dk_ref[rows, ln] += dk_add[pp]
                dv_ref[rows, ln] += dv_add[pp]

        def step(kb, carry):
            tile(kb, None)
            return carry

        lax.fori_loop(first_ref[pl.program_id(0), i].astype(jnp.int32), i * ndiag, step, 0)
        for d in range(ndiag):
            tile(i * ndiag + d, d * tk)
        for pp, ln in enumerate(lanes):
            dq_ref[:, ln] = jnp.where(heads[0], dqacc[2 * pp], dqacc[2 * pp + 1]) * 0.125

        @pl.when(step_id == ngrp * nq - 1)
        def _():
            cps = _chip_copies(e_in, e_out, *e_sems)
            for cp in cps:
                cp.wait_recv()
            for cp in cps:
                cp.wait_send()

    qspec = pl.BlockSpec((tq, wl), lambda p, i: (i, p))
    full = pl.BlockSpec((s, wl), lambda p, i: (0, p))
    any_spec = pl.BlockSpec(memory_space=pl.ANY)
    out = jax.ShapeDtypeStruct((s, DA), F32)
    res = pl.pallas_call(
        body, name=name, grid=(ngrp, nq),
        in_specs=[qspec, pl.BlockSpec((s, wl), lambda p, i: (0, ngrp + p), pipeline_mode=pl.Buffered(1)),
                  pl.BlockSpec((s, wl), lambda p, i: (0, 2 * ngrp + p), pipeline_mode=pl.Buffered(1)), qspec, qspec,
                  pl.BlockSpec(memory_space=pltpu.SMEM)] + [any_spec] * ne,
        out_specs=[qspec, full, full] + [any_spec] * ne,
        out_shape=[out, out, out] + [jax.ShapeDtypeStruct((3,) + a.shape[1:], a.dtype) for a in exchange],
        scratch_shapes=[pltpu.VMEM((nh, tq, LANES), F32)] * 3
        + [pltpu.SemaphoreType.DMA((3 * ne,)), pltpu.SemaphoreType.DMA((3 * ne,))],
        compiler_params=_params("arbitrary", "arbitrary"),
    )(qkv, qkv, qkv, do, tot, first, *exchange)
    return res[0], res[1], res[2], res[3:]


def _shift_rows(v, k):
    return pltpu.roll(v, k % v.shape[0], axis=0)


def _pooled(u, uh, i, g, w, ts):
    halo = jnp.where(i > 0, uh, 0.0)
    ue = jnp.concatenate([halo, u], axis=0)
    acc, span = ue, 1
    while span < w:
        acc = acc + _shift_rows(acc, span)
        span *= 2
    tpos = i * ts + lax.broadcasted_iota(jnp.int32, (ts, 1), 0)
    cnt = jnp.minimum(tpos + 1, w).astype(F32)
    return acc[HALO:] / cnt - u


def _pool_mix(proj, o, pw, pb, ps, ag, bd, *, ts, name):
    s = proj.shape[0]
    hb = ts // HALO

    def body(u_ref, uh_ref, o_ref, pw_ref, pb_ref, ps_ref, ag_ref, bd_ref, mix_ref):
        i = pl.program_id(0)
        for g, w in enumerate(POOL_WINDOWS):
            cols = slice(g * LANES, (g + 1) * LANES)
            pooled = _pooled(u_ref[:, cols], uh_ref[:, cols], i, g, w, ts)
            yv = (_dot(pooled.astype(BF16), pw_ref[g]) + pb_ref[:, cols]) * ps_ref[:, cols]
            mix_ref[:, cols] = yv.astype(BF16)
        ov = o_ref[...]
        mix_ref[:, DP:] = (ov * _group_rsqrt(ov, bd_ref[...]) * ag_ref[...]).astype(BF16)

    return pl.pallas_call(
        body, name=name, grid=(s // ts,),
        in_specs=[pl.BlockSpec((ts, DP), lambda i: (i, 0)),
                  pl.BlockSpec((HALO, DP), lambda i: (jnp.maximum(i * hb - 1, 0), 0)),
                  pl.BlockSpec((ts, DA), lambda i: (i, 0)),
                  _full((4, LANES, LANES)), _vec(DP), _vec(DP), _vec(DA), _full((DA, DA))],
        out_specs=pl.BlockSpec((ts, D), lambda i: (i, 0)),
        out_shape=jax.ShapeDtypeStruct((s, D), BF16), compiler_params=_params("parallel"),
    )(proj, proj, o, pw, pb, ps, ag, bd)


CF = DFF // 2
MXU_COLS = 256


def _sub_chunks(width):
    return [(c0, min(MXU_COLS, width - c0)) for c0 in range(0, width, MXU_COLS)]


def _up_conv_gate(h2, w_up, cw, cb, *, tm, name):
    s = h2.shape[0]
    hb = tm // HALO

    def body(a_ref, ah_ref, w_ref, cw_ref, cb_ref, up_ref, c_ref, act_ref):
        i = pl.program_id(1)
        ext = jnp.concatenate([ah_ref[...], a_ref[...]], axis=0)
        live_halo = i > 0
        for c0, cwid in _sub_chunks(CF):
            conv = []
            for off in (c0, CF + c0):
                cols = slice(off, off + cwid)
                u = _dot(ext, w_ref[cols, :], NT)
                up_ref[:, cols] = u[HALO:].astype(BF16)
                row = lax.broadcasted_iota(jnp.int32, (HALO + tm, 1), 0)
                ue = jnp.where(jnp.logical_or(row >= HALO, live_halo), u, 0.0)
                y = cw_ref[2:3, cols] * ue + cw_ref[1:2, cols] * _shift_rows(ue, 1) + cw_ref[0:1, cols] * _shift_rows(ue, 2)
                cv = y[HALO:] + cb_ref[:, cols]
                c_ref[:, cols] = cv.astype(BF16)
                conv.append(cv)
            gt, vl = conv
            act_ref[:, c0:c0 + cwid] = (gt / (1.0 + jnp.exp(-gt)) * vl).astype(BF16)

    return pl.pallas_call(
        body, name=name, grid=(2, s // tm),
        in_specs=[pl.BlockSpec((tm, D), lambda j, i: (i, 0)),
                  pl.BlockSpec((HALO, D), lambda j, i: (jnp.maximum(i * hb - 1, 0), 0)),
                  pl.BlockSpec((2 * CF, D), lambda j, i: (j, 0)),
                  pl.BlockSpec((3, 2 * CF), lambda j, i: (0, j)), pl.BlockSpec((1, 2 * CF), lambda j, i: (0, j))],
        out_specs=[pl.BlockSpec((tm, 2 * CF), lambda j, i: (i, j)), pl.BlockSpec((tm, 2 * CF), lambda j, i: (i, j)),
                   pl.BlockSpec((tm, CF), lambda j, i: (i, j))],
        out_shape=[jax.ShapeDtypeStruct((s, 2 * DFF), BF16), jax.ShapeDtypeStruct((s, 2 * DFF), BF16),
                   jax.ShapeDtypeStruct((s, DFF), BF16)],
        compiler_params=_params("parallel", "parallel"),
    )(h2, h2, w_up, cw, cb)


def _down_bwd_gate(dffn, w_down, conv, *, tm, name):
    s = dffn.shape[0]

    def body(a_ref, w_ref, c_ref, d_ref, db_ref):
        i = pl.program_id(1)

        @pl.when(i == 0)
        def _():
            db_ref[...] = jnp.zeros_like(db_ref)

        a = a_ref[...]
        for c0, cwid in _sub_chunks(CF):
            gcols, vcols = slice(c0, c0 + cwid), slice(CF + c0, CF + c0 + cwid)
            da = _dot(a, w_ref[gcols, :], NT)
            gt, vl = c_ref[:, gcols].astype(F32), c_ref[:, vcols].astype(F32)
            sg = 1.0 / (1.0 + jnp.exp(-gt))
            dgt = da * vl * (sg * (1.0 + gt * (1.0 - sg)))
            dvl = da * (gt * sg)
            d_ref[:, gcols] = dgt.astype(BF16)
            d_ref[:, vcols] = dvl.astype(BF16)
            db_ref[:, gcols] += _colsum8(dgt)
            db_ref[:, vcols] += _colsum8(dvl)

    return pl.pallas_call(
        body, name=name, grid=(2, s // tm),
        in_specs=[pl.BlockSpec((tm, D), lambda j, i: (i, 0)), pl.BlockSpec((CF, D), lambda j, i: (j, 0)),
                  pl.BlockSpec((tm, 2 * CF), lambda j, i: (i, j))],
        out_specs=[pl.BlockSpec((tm, 2 * CF), lambda j, i: (i, j)), pl.BlockSpec((SUBLANES, 2 * CF), lambda j, i: (0, j))],
        out_shape=[jax.ShapeDtypeStruct((s, 2 * DFF), BF16), jax.ShapeDtypeStruct((SUBLANES, 2 * DFF), F32)],
        compiler_params=_params("parallel", "arbitrary"),
    )(dffn, w_down, conv)


def _conv_bwd_up_bwd(dc, up, cw, w_up, *, tm, name):
    s = up.shape[0]
    hb = tm // HALO
    nb = s // HALO
    nk = 2 * DFF // CF
    n = s // tm

    def body(d_ref, dn_ref, u_ref, cw_ref, w_ref, du_ref, dh_ref, dw_ref, acc, dwacc):
        i, k = pl.program_id(0), pl.program_id(1)

        @pl.when(jnp.logical_and(i == 0, k == 0))
        def _():
            dwacc[...] = jnp.zeros_like(dwacc)

        @pl.when(k == 0)
        def _():
            acc[...] = jnp.zeros_like(acc)

        live_next = i < n - 1
        part = None
        for c0, cwid in _sub_chunks(CF):
            cols = slice(c0, c0 + cwid)
            dcur = d_ref[:, cols].astype(F32)
            de = jnp.concatenate([dcur, jnp.where(live_next, dn_ref[:, cols].astype(F32), 0.0)], axis=0)
            d1 = _shift_rows(de, -1)[:tm]
            d2 = _shift_rows(de, -2)[:tm]
            du = (cw_ref[2:3, cols] * dcur + cw_ref[1:2, cols] * d1 + cw_ref[0:1, cols] * d2).astype(BF16)
            du_ref[:, cols] = du
            prod = _dot(du, w_ref[cols, :])
            part = prod if part is None else part + prod
            u = u_ref[:, cols].astype(F32)
            for tap, dsh in ((2, dcur), (1, d1), (0, d2)):
                dwacc[k, SUBLANES * tap:SUBLANES * (tap + 1), cols] += _colsum8(dsh * u)
        acc[...] += part

        @pl.when(k == nk - 1)
        def _():
            dh_ref[...] = acc[...].astype(dh_ref.dtype)

        @pl.when(jnp.logical_and(i == n - 1, k == nk - 1))
        def _():
            dw_ref[...] = dwacc[...]

    res = pl.pallas_call(
        body, name=name, grid=(n, nk),
        in_specs=[pl.BlockSpec((tm, CF), lambda i, k: (i, k)),
                  pl.BlockSpec((HALO, CF), lambda i, k: (jnp.minimum((i + 1) * hb, nb - 1), k)),
                  pl.BlockSpec((tm, CF), lambda i, k: (i, k)),
                  pl.BlockSpec((3, CF), lambda i, k: (0, k)),
                  pl.BlockSpec((CF, D), lambda i, k: (k, 0))],
        out_specs=[pl.BlockSpec((tm, CF), lambda i, k: (i, k)), pl.BlockSpec((tm, D), lambda i, k: (i, 0)),
                   _full((nk, 24, CF))],
        out_shape=[jax.ShapeDtypeStruct((s, 2 * DFF), BF16), jax.ShapeDtypeStruct((s, D), BF16),
                   jax.ShapeDtypeStruct((nk, 24, CF), F32)],
        scratch_shapes=[pltpu.VMEM((tm, D), F32), pltpu.VMEM((nk, 24, CF), F32)],
        compiler_params=_params("arbitrary", "arbitrary"),
    )(dc, dc, up, cw, w_up)
    return res[0], res[1], jnp.transpose(res[2], (1, 0, 2)).reshape(24, 2 * DFF)


def _ln_mod_bwd(dh, xin, g, scale, resid, extra, gate, *, ts, name, side=None):
    s = xin.shape[0]
    row = pl.BlockSpec((ts, D), lambda i: (i, 0))
    acc8 = pl.BlockSpec((SUBLANES, D), lambda i: (0, 0))
    with_gate = extra is not None

    def body(*refs):
        if with_gate:
            dh_ref, x_ref, g_ref, sc_ref, r_ref, e_ref, gt_ref, dx_ref, da_ref, dsh, dsc, dg, dgt = refs
        else:
            dh_ref, x_ref, g_ref, sc_ref, r_ref, dx_ref, dsh, dsc, dg = refs
        i = pl.program_id(0)

        @pl.when(i == 0)
        def _():
            for acc in (dsh, dsc, dg) + ((dgt,) if with_gate else ()):
                acc[...] = jnp.zeros_like(acc)

        xv, dhv = x_ref[...], dh_ref[...].astype(F32)
        r = lax.rsqrt(jnp.mean(xv * xv, axis=-1, keepdims=True) + EPS)
        xn = xv * r
        dsh[...] += _colsum8(dhv)
        dsc[...] += _colsum8(dhv * (xn * g_ref[...]))
        dhp = dhv * (1.0 + sc_ref[...])
        dg[...] += _colsum8(dhp * xn)
        dxn = dhp * g_ref[...]
        dx = r_ref[...] + r * (dxn - xn * jnp.mean(dxn * xn, axis=-1, keepdims=True))
        dx_ref[...] = dx
        if with_gate:
            da_ref[...] = (dx * gt_ref[...]).astype(BF16)
            dgt[...] += _colsum8(dx * e_ref[...].astype(F32))

    f32o, p8 = jax.ShapeDtypeStruct((s, D), F32), jax.ShapeDtypeStruct((SUBLANES, D), F32)
    if with_gate:
        ins, in_specs = (dh, xin, g, scale, resid, extra, gate), [row, row, _vec(D), _vec(D), row, row, _vec(D)]
        out_specs, out_shape = [row, row, acc8, acc8, acc8, acc8], [f32o, jax.ShapeDtypeStruct((s, D), BF16), p8, p8, p8, p8]
    else:
        ins, in_specs = (dh, xin, g, scale, resid), [row, row, _vec(D), _vec(D), row]
        out_specs, out_shape = [row, acc8, acc8, acc8], [f32o, p8, p8, p8]
    return _row_call(body, side, name=name, steps=s // ts, in_specs=in_specs, out_specs=out_specs,
                     out_shape=out_shape, ins=ins)


def _group_norm_bwd(t, dn_out, gvec, bd):
    r = _group_rsqrt(t, bd)
    dg_terms = dn_out * t * r
    dn = dn_out * gvec
    dt = r * (dn - t * (r * r) * (_split_dot(dn * t, bd) * (1.0 / HD)))
    return dt, dg_terms


def _mix_bwd(dmix, proj, o, pw, pb, ps, ag, bd, *, ts, name, side=None):
    s = proj.shape[0]
    hb = ts // HALO
    nb = s // HALO

    def body(dm_ref, dmn_ref, u_ref, uh_ref, o_ref, pw_ref, pb_ref, ps_ref, ag_ref, bd_ref,
             du_ref, do_ref, dpw_ref, dpb_ref, dps_ref, dag_ref):
        i = pl.program_id(0)
        n = s // ts

        @pl.when(i == 0)
        def _():
            for acc in (dpw_ref, dpb_ref, dps_ref, dag_ref):
                acc[...] = jnp.zeros_like(acc)

        for g, w in enumerate(POOL_WINDOWS):
            cols = slice(g * LANES, (g + 1) * LANES)
            wg = pw_ref[g]
            psg = ps_ref[:, cols]
            pooled = _pooled(u_ref[:, cols], uh_ref[:, cols], i, g, w, ts).astype(BF16)
            dy = dm_ref[:, cols].astype(F32)
            dps_ref[:, cols] += _colsum8(dy * (_dot(pooled, wg) + pb_ref[:, cols]))
            dpre = dy * psg
            dpb_ref[:, cols] += _colsum8(dpre)
            dpreb = dpre.astype(BF16)
            dpw_ref[g * LANES:(g + 1) * LANES, :] += _dot(pooled, dpreb, TN)
            dpool = _dot(dpreb, wg, NT)
            dnext = _dot((dmn_ref[:, cols].astype(F32) * psg).astype(BF16), wg, NT)
            dpe = jnp.concatenate([dpool, jnp.where(i < n - 1, dnext, 0.0)], axis=0)
            tpos = i * ts + lax.broadcasted_iota(jnp.int32, (ts + HALO, 1), 0)
            acc = dpe / jnp.minimum(tpos + 1, w).astype(F32)
            span = 1
            while span < w:
                acc = acc + _shift_rows(acc, -span)
                span *= 2
            du_ref[:, cols] = acc[:ts] - dpool
        ov = o_ref[...]
        dov, dg_terms = _group_norm_bwd(ov, dm_ref[:, DP:].astype(F32), ag_ref[...], bd_ref[...])
        do_ref[...] = dov
        dag_ref[...] += _colsum8(dg_terms)

    p8 = jax.ShapeDtypeStruct((SUBLANES, DP), F32)
    acc8 = pl.BlockSpec((SUBLANES, DP), lambda i: (0, 0))
    half = pl.BlockSpec((ts, DP), lambda i: (i, 0))
    return _row_call(
        body, side, name=name, steps=s // ts,
        in_specs=[pl.BlockSpec((ts, D), lambda i: (i, 0)),
                  pl.BlockSpec((HALO, DP), lambda i: (jnp.minimum((i + 1) * hb, nb - 1), 0)),
                  half, pl.BlockSpec((HALO, DP), lambda i: (jnp.maximum(i * hb - 1, 0), 0)),
                  half, _full((4, LANES, LANES)), _vec(DP), _vec(DP), _vec(DA), _full((DA, DA))],
        out_specs=[half, half, _full((DP, LANES)), acc8, acc8, acc8],
        out_shape=[jax.ShapeDtypeStruct((s, DP), F32), jax.ShapeDtypeStruct((s, DA), F32),
                   jax.ShapeDtypeStruct((DP, LANES), F32), p8, p8, p8],
        ins=(dmix, dmix, proj, proj, o, pw, pb, ps, ag, bd))


def _qk_norm_bwd(du, dq, dk, dv, proj, qg, kg, bd, *, ts, name):
    s = proj.shape[0]

    def body(du_ref, dq_ref, dk_ref, dv_ref, q_ref, k_ref, qg_ref, kg_ref, bd_ref, dp_ref, dqg_ref, dkg_ref):
        i = pl.program_id(0)

        @pl.when(i == 0)
        def _():
            dqg_ref[...] = jnp.zeros_like(dqg_ref)
            dkg_ref[...] = jnp.zeros_like(dkg_ref)

        bdv = bd_ref[...]
        dqr, tq = _group_norm_bwd(q_ref[...], dq_ref[...], qg_ref[...], bdv)
        dkr, tk = _group_norm_bwd(k_ref[...], dk_ref[...], kg_ref[...], bdv)
        dqg_ref[...] += _colsum8(tq)
        dkg_ref[...] += _colsum8(tk)
        dp_ref[:, 0:DP] = du_ref[...].astype(BF16)
        dp_ref[:, DP:DP + DA] = dqr.astype(BF16)
        dp_ref[:, DP + DA:DP + 2 * DA] = dkr.astype(BF16)
        dp_ref[:, DP + 2 * DA:] = dv_ref[...].astype(BF16)

    half = pl.BlockSpec((ts, DA), lambda i: (i, 0))
    col = lambda j: pl.BlockSpec((ts, DA), lambda i: (i, j))
    acc8 = pl.BlockSpec((SUBLANES, DA), lambda i: (0, 0))
    p8 = jax.ShapeDtypeStruct((SUBLANES, DA), F32)
    return pl.pallas_call(
        body, name=name, grid=(s // ts,),
        in_specs=[half, half, half, half, col(1), col(2), _vec(DA), _vec(DA), _full((DA, DA))],
        out_specs=[pl.BlockSpec((ts, DIN), lambda i: (i, 0)), acc8, acc8],
        out_shape=[jax.ShapeDtypeStruct((s, DIN), BF16), p8, p8],
        compiler_params=_params("arbitrary"),
    )(du, dq, dk, dv, proj, proj, qg, kg, bd)


def _split3(a):
    hi = a.astype(BF16)
    return hi, (a - hi.astype(F32)).astype(BF16)


def _dot3(a, b, dn):
    ah, al = _split3(a)
    bh, bl = _split3(b)
    return _dot(ah, bh, dn) + (_dot(ah, bl, dn) + _dot(al, bh, dn))


def _ada_fwd(c_all, w, b, name):
    nw = w.shape[1]

    def body(c_ref, w_ref, b_ref, o_ref):
        cv = c_ref[...]
        act = cv / (1.0 + jnp.exp(-cv))
        o_ref[...] = _dot3(act, w_ref[...], NN) + b_ref[...]

    return pl.pallas_call(
        body, name=name, in_specs=[_full((NDEV, D)), _full(w.shape), _full((1, nw))], out_specs=_full((NDEV, nw)),
        out_shape=jax.ShapeDtypeStruct((NDEV, nw), F32), grid=(1,), compiler_params=_params("arbitrary"),
    )(c_all, w, b)


def _ada_bwd(c_all, dmod, name):
    nw = dmod.shape[1]

    def body(c_ref, d_ref, o_ref):
        cv = c_ref[...]
        act = cv / (1.0 + jnp.exp(-cv))
        o_ref[...] = _dot3(act, d_ref[...], TN)[None]

    return pl.pallas_call(
        body, name=name, in_specs=[_full((NDEV, D)), _full((NDEV, nw))], out_specs=_full((1, D, nw)),
        out_shape=jax.ShapeDtypeStruct((1, D, nw), F32), grid=(1,), compiler_params=_params("arbitrary"),
    )(c_all, dmod)


def _fold_heads(v):
    acc = v[:, 0:HD]
    for h in range(1, DA // HD):
        acc = acc + v[:, h * HD:(h + 1) * HD]
    return acc


def _pack_partials(pieces, dcw_p, name):
    n_p = len(pieces)
    total = sum(p.shape[1] for p in pieces) + 3 * dcw_p.shape[1]
    npack = -(-total // (SUBLANES * LANES)) * (SUBLANES * LANES)

    def body(*refs):
        out = refs[-1]
        off = 0
        for r in refs[:n_p]:
            out[:, off:off + r.shape[1]] = jnp.sum(r[...], axis=0, keepdims=True)
            off += r.shape[1]
        dw = refs[n_p]
        for tap in range(3):
            out[:, off:off + dw.shape[1]] = jnp.sum(dw[SUBLANES * tap:SUBLANES * (tap + 1), :], axis=0, keepdims=True)
            off += dw.shape[1]
        if off < npack:
            out[:, off:] = jnp.zeros((1, npack - off), F32)

    arrs = list(pieces) + [dcw_p]
    return pl.pallas_call(
        body, name=name, grid=(1,), in_specs=[_full(a.shape) for a in arrs], out_specs=_full((1, npack)),
        out_shape=jax.ShapeDtypeStruct((1, npack), F32), compiler_params=_params("arbitrary"),
    )(*arrs)


def _small_update(gathered, gathered_pw, gathered_cw, specs, params, loss_off, name):
    names = [sp[0] for sp in specs]
    flat = []
    for nme in names + ["pool_w", "conv_w"]:
        flat += list(params[nme])
    n_in = len(flat)

    def body(*refs):
        ga_ref, gp_ref, gc_ref = refs[0], refs[1], refs[2]
        prm = refs[3:3 + n_in]
        outs = refs[3 + n_in:]
        total = ga_ref[0:1, :]
        for dv in range(1, NDEV):
            total = total + ga_ref[dv:dv + 1, :]
        k = 0
        for idx, (nme, off, width, fold) in enumerate(specs):
            g = total[:, off:off + width]
            if fold:
                g = _fold_heads(g)
            w_ref, m_ref, v_ref = prm[3 * idx:3 * idx + 3]
            d, nm, nv = _adamw_math(w_ref[...], g, m_ref[...], v_ref[...])
            for val in (g, d, nm, nv):
                outs[k][...] = val
                k += 1
        gpw = gp_ref[0]
        for dv in range(1, NDEV):
            gpw = gpw + gp_ref[dv]
        w_ref, m_ref, v_ref = prm[3 * len(specs):3 * len(specs) + 3]
        d, nm, nv = _adamw_math(w_ref[...], gpw, m_ref[...], v_ref[...])
        for val in (gpw, d, nm, nv):
            outs[k][...] = val
            k += 1
        gcw = gc_ref[0]
        for dv in range(1, NDEV):
            gcw = gcw + gc_ref[dv]
        w_ref, m_ref, v_ref = prm[3 * len(specs) + 3:3 * len(specs) + 6]
        d, nm, nv = _adamw_math(w_ref[...], gcw, m_ref[...], v_ref[...])
        for val in (gcw, d, nm, nv):
            outs[k][...] = val
            k += 1
        outs[k][...] = ga_ref[:, 0:6 * D]
        outs[k + 1][...] = total[:, loss_off:loss_off + LANES] * (1.0 / SUBLANES)

    out_shape, out_specs = [], []
    for nme in names + ["pool_w", "conv_w"]:
        shp = params[nme][0].shape
        out_shape += [jax.ShapeDtypeStruct(shp, F32)] * 4
        out_specs += [_full(shp)] * 4
    out_shape += [jax.ShapeDtypeStruct((NDEV, 6 * D), F32), jax.ShapeDtypeStruct((1, LANES), F32)]
    out_specs += [_full((NDEV, 6 * D)), _full((1, LANES))]
    res = pl.pallas_call(
        body, name=name, grid=(1,),
        in_specs=[_full(gathered.shape), _full(gathered_pw.shape), _full(gathered_cw.shape)] + [_full(a.shape) for a in flat],
        out_specs=out_specs, out_shape=out_shape, compiler_params=_params("arbitrary"),
    )(gathered, gathered_pw, gathered_cw, *flat)
    out = {nme: tuple(res[4 * i:4 * i + 4]) for i, nme in enumerate(names + ["pool_w", "conv_w"])}
    return out, res[-2], res[-1][0, 0]


def _row_tile(s):
    return 512 if s % 512 == 0 else s


def kernel(x, c, ada_w, ada_b, norm1_g, w_in, pool_w, pool_b, pool_scale, q_norm_g, k_norm_g, attn_out_g, w_out, norm2_g, w_up, conv_w, conv_b, w_down, loss_target, m_ada_w, m_ada_b, m_norm1_g, m_w_in, m_pool_w, m_pool_b, m_pool_scale, m_q_norm_g, m_k_norm_g, m_attn_out_g, m_w_out, m_norm2_g, m_w_up, m_conv_w, m_conv_b, m_w_down, v_ada_w, v_ada_b, v_norm1_g, v_w_in, v_pool_w, v_pool_b, v_pool_scale, v_q_norm_g, v_k_norm_g, v_attn_out_g, v_w_out, v_norm2_g, v_w_up, v_conv_w, v_conv_b, v_w_down):
    ax, ay, ac = lax.axis_index("x"), lax.axis_index("y"), lax.axis_index("c")
    me = 4 * ax + 2 * ay + ac
    me_swapped = 4 * ay + 2 * ax + ac
    xs, tgt = x[0], loss_target[0]
    s = xs.shape[0]
    ts = _row_tile(s)
    tq_attn, tk_attn, hp_attn = 256, 256, 2
    tmm = 2 * ts
    bd = _block_diag_ones(DA, HD)

    w_in_t = w_in[0].T.astype(BF16)
    w_up_t = w_up[0].T.astype(BF16)
    c_all = _all_gather([jnp.broadcast_to(c, (SUBLANES, D))], [False], "gather_c")[0][:, 0, :]
    n_ada = ada_w.shape[2]
    ada_b_mine = lax.dynamic_slice_in_dim(ada_b, me * n_ada, n_ada, axis=1)
    mod_part = _ada_fwd(c_all, ada_w[0], ada_b_mine, "ada_fwd")
    mod_all = _all_gather([mod_part], [False], "gather_mod")[0]
    mod = lax.dynamic_index_in_dim(mod_all, me, axis=1, keepdims=False).reshape(1, 6 * D)
    shift1, scale1, gate1, shift2, scale2, gate2 = [mod[:, k * D:(k + 1) * D] for k in range(6)]

    later_w = [w_out[0].astype(BF16), w_up_t, w_down[0].astype(BF16)]
    cb_full = jnp.transpose(conv_b.reshape(1, 2, 2, 2, 704), (0, 2, 1, 3, 4)).reshape(1, 2 * DFF)

    qg = jnp.tile(q_norm_g, (1, DA // HD))
    kg = jnp.tile(k_norm_g, (1, DA // HD))
    ag = attn_out_g.reshape(1, DA)
    pw = pool_w[0].astype(BF16)
    pb = pool_b.reshape(1, DP)
    h1, (gw_in, gcw) = _ln_mod(xs, norm1_g, scale1, shift1, ts=tmm, name="ln1",
                               side=_gather_side([w_in_t, jnp.pad(conv_w[0], ((0, 5), (0, 64)))], [False, True]))
    w_in_full = gw_in.reshape(DIN, D)
    cw_full = jnp.transpose(gcw[:, :3, :704], (1, 0, 2)).reshape(3, 2 * DFF)
    proj, qkv = _in_proj_qk_norm(h1, w_in_full, qg, kg, bd, tm=ts, name="in_proj_qk_norm")
    o_raw, m_tot, kb_first, (gw_out, gw_up, gw_down) = _attn_fwd(
        qkv, later_w, [False, True, False], tq=tq_attn, tk=tk_attn, hp=hp_attn, name="attn_fwd")
    w_out_full = gw_out.reshape(D, D)
    w_up_full = gw_up.reshape(2 * DFF, D)
    w_down_full = gw_down.reshape(DFF, D)
    mix = _pool_mix(proj, o_raw, pw, pb, pool_scale, ag, bd, ts=tmm, name="pool_mix")
    att, x1, h2 = _proj_res_ln_mod(mix, w_out_full, xs, gate1, norm2_g, scale2, shift2, tm=ts, name="out_proj_ln2")
    up, conv, act = _up_conv_gate(h2, w_up_full, cw_full, cb_full, tm=tmm, name="up_conv_gate")
    dy, dffn, dgate2_p, loss_p = _proj_loss_head(act, w_down_full, x1, tgt, gate2, tm=ts, name="down_proj_loss")

    g_w_down = _matmul(act, dffn, mode="tn", out_dtype=F32, tm=CF, tn=D, tk=2 * tmm, name="down_wgrad")
    dconv, dcb_p = _down_bwd_gate(dffn, w_down_full, conv, tm=tmm, name="down_bwd_gate")
    dup, dh2, dcw_p = _conv_bwd_up_bwd(dconv, up, cw_full, w_up_full, tm=ts, name="conv_bwd_up_bwd")
    g_w_up_t = _matmul(dup, h2, mode="tn", out_dtype=F32, tm=CF, tn=D, tk=2 * tmm, name="up_wgrad")
    (dx1, datt, dshift2_p, dscale2_p, dnorm2_p, dgate1_p), _ = _ln_mod_bwd(
        dh2, x1, norm2_g, scale2, dy, att, gate1, ts=ts, name="ln2_bwd")

    dmix = _matmul(datt, w_out_full, mode="nt", out_dtype=BF16, tm=tmm,tn=D, tk=D, name="out_bwd")
    g_w_out = _matmul(mix, datt, mode="tn", out_dtype=F32, tm=D, tn=D, tk=2 * tmm, name="out_wgrad")
    core = jnp.reshape(ac, (1,)).astype(jnp.int32)
    chip = jnp.reshape(2 * ax + ay, (1,)).astype(jnp.int32)
    big_ffn = [g_w_up_t.reshape(NDEV, 2 * DFF // NDEV, D), g_w_down.reshape(NDEV, DFF // NDEV, D),
               g_w_out.reshape(NDEV, D // NDEV, D)]
    swaps_ffn = [True, False, False]
    (du, do_raw, g_pw_p, dpb_p, dps_p, dag_p), gots_ffn = _mix_bwd(
        dmix, proj, o_raw, pw, pb, pool_scale, ag, bd, ts=tmm, name="mix_bwd", side=_pair_side(big_ffn, swaps_ffn))
    sums_ffn = [_pair_sum(big_ffn[k], gots_ffn[k], swaps_ffn[k], core, "rs_pair_sum_ffn%d" % k) for k in range(3)]
    dqn, dkn, dvv, parts_ffn = _attn_bwd(qkv, do_raw, m_tot, kb_first, sums_ffn, tq=tq_attn, tk=tk_attn, hp=hp_attn, name="attn_bwd")
    dproj, dqg_p, dkg_p = _qk_norm_bwd(du, dqn, dkn, dvv, proj, qg, kg, bd, ts=tmm, name="qk_norm_bwd")
    g_w_in_t = _matmul(dproj, h1, mode="tn", out_dtype=F32, tm=DIN // 2, tn=D, tk=2 * tmm, name="in_wgrad")
    big = [g_w_in_t.reshape(NDEV, DIN // NDEV, D)]
    gots = _pair_exchange(big, [False], "rs_pair")
    sums = [_pair_sum(big[0], gots[0], False, core, "rs_pair_sum")]
    dh1, parts = _matmul(dproj, w_in_full, mode="nn", out_dtype=BF16, tm=tmm,tn=D, tk=DIN, name="in_bwd",
                         side=_chip_side(sums))
    (grad_x, dshift1_p, dscale1_p, dnorm1_p), _ = _ln_mod_bwd(
        dh1, xs, norm1_g, scale1, dx1, None, None, ts=tmm, name="ln1_bwd")

    tr = lambda a: a[0].T
    r_in = _adamw_reduce(tr(w_in), tr(m_w_in), tr(v_w_in), sums[0], parts[0], chip, "adamw_w_in")
    r_out = _adamw_reduce(w_out[0], m_w_out[0], v_w_out[0], sums_ffn[2], parts_ffn[2], chip, "adamw_w_out")
    r_up = _adamw_reduce(tr(w_up), tr(m_w_up), tr(v_w_up), sums_ffn[0], parts_ffn[0], chip, "adamw_w_up")
    r_down = _adamw_reduce(w_down[0], m_w_down[0], v_w_down[0], sums_ffn[1], parts_ffn[1], chip, "adamw_w_down")
    r_in = [a.T[None] for a in r_in]
    r_up = [a.T[None] for a in r_up]
    r_out = [a[None] for a in r_out]
    r_down = [a[None] for a in r_down]

    dcb_nat = jnp.transpose(dcb_p.reshape(SUBLANES, 2, 2, 2, 704), (0, 2, 1, 3, 4)).reshape(SUBLANES, 2 * DFF)
    pieces = [dshift1_p, dscale1_p, dgate1_p, dshift2_p, dscale2_p, dgate2_p,
              dnorm1_p, dnorm2_p, dcb_nat, dpb_p, dps_p, dag_p, dqg_p, dkg_p, loss_p]
    n_vec = sum(p.shape[1] for p in pieces)
    packed = _pack_partials(pieces, dcw_p, "pack_partials")
    npack = packed.shape[1]
    gathered, gathered_pw = _all_gather([packed.reshape(SUBLANES, npack // SUBLANES), g_pw_p], [False, False], "gather_small")
    gathered = gathered.reshape(NDEV, npack)
    gathered_cw = lax.dynamic_index_in_dim(
        gathered[:, n_vec:n_vec + 6 * DFF].reshape(NDEV, 3, NDEV, 704), me_swapped, axis=2, keepdims=False)
    specs = [("ada_b", 0, 6 * D, False)]
    off = 6 * D
    for nme, width, fold in (("norm1_g", D, False), ("norm2_g", D, False), ("conv_b", 2 * DFF, False),
                             ("pool_b", DP, False), ("pool_scale", DP, False), ("attn_out_g", DA, False),
                             ("q_norm_g", DA, True), ("k_norm_g", DA, True)):
        specs.append((nme, off, width, fold))
        off += width
    small = {
        "ada_b": (ada_b, m_ada_b, v_ada_b),
        "norm1_g": (norm1_g, m_norm1_g, v_norm1_g), "norm2_g": (norm2_g, m_norm2_g, v_norm2_g),
        "conv_b": (conv_b, m_conv_b, v_conv_b),
        "pool_b": (pb, m_pool_b.reshape(1, DP), v_pool_b.reshape(1, DP)),
        "pool_scale": (pool_scale, m_pool_scale, v_pool_scale),
        "attn_out_g": (ag, m_attn_out_g.reshape(1, DA), v_attn_out_g.reshape(1, DA)),
        "q_norm_g": (q_norm_g, m_q_norm_g, v_q_norm_g), "k_norm_g": (k_norm_g, m_k_norm_g, v_k_norm_g),
        "pool_w": (pool_w.reshape(DP, LANES), m_pool_w.reshape(DP, LANES), v_pool_w.reshape(DP, LANES)),
        "conv_w": (conv_w[0], m_conv_w[0], v_conv_w[0]),
    }
    upd, dmod_all, loss = _small_update(gathered, gathered_pw, gathered_cw, specs, small, off, "small_update")
    g_ada_w = _ada_bwd(c_all, lax.dynamic_slice_in_dim(dmod_all, me * n_ada, n_ada, axis=1), "ada_bwd")
    r_ada = [g_ada_w] + [a[None] for a in _adamw(ada_w[0], m_ada_w[0], v_ada_w[0], g_ada_w[0], "adamw_ada_w")]

    shapes = {"ada_b": ada_b.shape, "norm1_g": norm1_g.shape, "pool_w": pool_w.shape, "pool_b": pool_b.shape,
              "pool_scale": pool_scale.shape, "q_norm_g": q_norm_g.shape, "k_norm_g": k_norm_g.shape,
              "attn_out_g": attn_out_g.shape, "norm2_g": norm2_g.shape, "conv_w": conv_w.shape, "conv_b": conv_b.shape}
    res = {nme: [a.reshape(shapes[nme]) for a in upd[nme]] for nme in shapes}
    res.update(ada_w=r_ada, w_in=r_in, w_out=r_out, w_up=r_up, w_down=r_down)
    names = ["ada_w", "ada_b", "norm1_g", "w_in", "pool_w", "pool_b", "pool_scale", "q_norm_g", "k_norm_g",
             "attn_out_g", "w_out", "norm2_g", "w_up", "conv_w", "conv_b", "w_down"]
    outs = [loss, grad_x[None]]
    for q in range(4):
        outs += [res[nme][q] for nme in names]
    return tuple(outs)
```

```python
import functools
import math

import numpy as np
import jax
import jax.numpy as jnp
from jax import lax
from jax.experimental import pallas as pl
from jax.experimental.pallas import tpu as pltpu

F32, BF16 = jnp.float32, jnp.bfloat16
D = 1024
DP = 512
DA = 512
HD = 64
DIN = DP + 3 * DA
DFF = 2816
POOL_WINDOWS = (2, 4, 8, 16)
HALO = 16
EPS = 1e-6
LANES = 128
SUBLANES = 8
NDEV = 8
VMEM_LIMIT = 56 * 1024 * 1024
MESH = pl.DeviceIdType.MESH

ADAM_LR, ADAM_B1, ADAM_B2, ADAM_EPS, ADAM_WD, ADAM_STEP = 0.001, 0.9, 0.999, 1e-08, 0.01, 10

NN = (((1,), (0,)), ((), ()))
NT = (((1,), (1,)), ((), ()))
TN = (((0,), (0,)), ((), ()))


def _params(*sem):
    return pltpu.CompilerParams(dimension_semantics=sem, vmem_limit_bytes=VMEM_LIMIT)


def _full(shape):
    nd = len(shape)
    return pl.BlockSpec(shape, lambda *_: (0,) * nd)


def _dot(a, b, dn=NN):
    return lax.dot_general(a, b, dn, preferred_element_type=F32)


def _split_dot(a, b, dn=NN):
    hi = a.astype(BF16)
    lo = (a - hi.astype(F32)).astype(BF16)
    return _dot(hi, b, dn) + _dot(lo, b, dn)


def _colsum8(v):
    r, n = v.shape
    return v.reshape(r // SUBLANES, SUBLANES, n).sum(axis=0)


def _block_diag_ones(n, blk):
    i = np.arange(n) // blk
    return jnp.asarray((i[:, None] == i[None, :]).astype(np.float32), BF16)


def _matmul(a, b, *, mode, out_dtype, tm, tn, tk, name, n_outer=False, side=None):
    if mode == "tn":
        K, M = a.shape
        N = b.shape[1]
    elif mode == "nt":
        M, K = a.shape
        N = b.shape[0]
    else:
        M, K = a.shape
        N = b.shape[1]
    tm, tn, tk = min(tm, M), min(tn, N), min(tk, K)
    assert M % tm == 0 and N % tn == 0 and K % tk == 0, (name, M, N, K, tm, tn, tk)
    nk = K // tk
    dn = {"nn": NN, "nt": NT, "tn": TN}[mode]

    def body(a_ref, b_ref, o_ref, *acc):
        if nk == 1:
            o_ref[...] = _dot(a_ref[...], b_ref[...], dn).astype(o_ref.dtype)
            return
        acc_ref, = acc
        k = pl.program_id(2)

        @pl.when(k == 0)
        def _():
            acc_ref[...] = jnp.zeros_like(acc_ref)

        acc_ref[...] += _dot(a_ref[...], b_ref[...], dn)

        @pl.when(k == nk - 1)
        def _():
            o_ref[...] = acc_ref[...].astype(o_ref.dtype)

    if n_outer:
        gi = lambda g: (g[1], g[0], g[2])
        grid = (N // tn, M // tm, nk)
    else:
        gi = lambda g: g
        grid = (M // tm, N // tn, nk)

    def amap(*g):
        i, j, k = gi(g)
        return (k, i) if mode == "tn" else (i, k)

    def bmap(*g):
        i, j, k = gi(g)
        return (j, k) if mode == "nt" else (k, j)

    def omap(*g):
        i, j, k = gi(g)
        return (i, j)

    a_blk = (tk, tm) if mode == "tn" else (tm, tk)
    b_blk = (tn, tk) if mode == "nt" else (tk, tn)
    acc_scratch = [] if nk == 1 else [pltpu.VMEM((tm, tn), F32)]
    if side is None:
        return pl.pallas_call(
            body, name=name, grid=grid,
            in_specs=[pl.BlockSpec(a_blk, amap), pl.BlockSpec(b_blk, bmap)],
            out_specs=pl.BlockSpec((tm, tn), omap),
            out_shape=jax.ShapeDtypeStruct((M, N), out_dtype),
            scratch_shapes=acc_scratch,
            compiler_params=_params("parallel", "parallel", "arbitrary"),
        )(a, b)

    ne = len(side.arrs)
    steps = grid[0] * grid[1] * grid[2]

    nsem = len(side.scratch)

    def with_side(*refs):
        e_in, e_out = refs[2:2 + ne], refs[3 + ne:3 + 2 * ne]
        sems = refs[len(refs) - nsem:]
        step = (pl.program_id(0) * grid[1] + pl.program_id(1)) * grid[2] + pl.program_id(2)

        @pl.when(step == 0)
        def _():
            side.start(e_in, e_out, *sems)

        body(refs[0], refs[1], refs[2 + ne], *refs[3 + 2 * ne:len(refs) - nsem])

        @pl.when(step == steps - 1)
        def _():
            side.finish(e_in, e_out, *sems)

    any_spec = pl.BlockSpec(memory_space=pl.ANY)
    res = pl.pallas_call(
        with_side, name=name, grid=grid,
        in_specs=[pl.BlockSpec(a_blk, amap), pl.BlockSpec(b_blk, bmap)] + [any_spec] * ne,
        out_specs=[pl.BlockSpec((tm, tn), omap)] + [any_spec] * ne,
        out_shape=[jax.ShapeDtypeStruct((M, N), out_dtype)] + side.out_shapes,
        scratch_shapes=acc_scratch + side.scratch,
        compiler_params=_params("arbitrary", "arbitrary", "arbitrary"),
    )(a, b, *side.arrs)
    return res[0], list(res[1:])


def _slot(swap, px, py, pc):
    return 4 * py + 2 * px + pc if swap else 4 * px + 2 * py + pc


class _Gather:
    def __init__(self, ins, outs, send, recv, loc, swaps):
        self.ins, self.outs, self.send, self.recv, self.loc, self.swaps = ins, outs, send, recv, loc, swaps
        x, y, c = lax.axis_index("x"), lax.axis_index("y"), lax.axis_index("c")
        self.me, self.sib = (x, y, c), (x, y, 1 - c)
        self.chips = [(1 - x, y), (x, 1 - y), (1 - x, 1 - y)]
        self.n = len(ins)

    @staticmethod
    def scratch(n):
        return [pltpu.SemaphoreType.DMA((7 * n,)), pltpu.SemaphoreType.DMA((7 * n,)), pltpu.SemaphoreType.DMA((n,))]

    def copy(self, a, k, blk, to, src=None):
        rows = self.outs[a].at[_slot(self.swaps[a], *blk)]
        return pltpu.make_async_remote_copy(
            src_ref=rows if src is None else src, dst_ref=rows,
            send_sem=self.send.at[7 * a + k], recv_sem=self.recv.at[7 * a + k], device_id=to, device_id_type=MESH)

    def mine(self, a):
        return pltpu.make_async_copy(self.ins[a], self.outs[a].at[_slot(self.swaps[a], *self.me)], self.loc.at[a])

    def first(self, a):
        c = self.me[2]
        return [self.copy(a, 0, self.me, self.sib, src=self.ins[a])] + [
            self.copy(a, 1 + j, self.me, (*chip, c), src=self.ins[a]) for j, chip in enumerate(self.chips)]

    def forwards(self, a):
        c = self.me[2]
        return [self.copy(a, 4 + j, (*chip, c), self.sib) for j, chip in enumerate(self.chips)]

    def start(self):
        for a in range(self.n):
            self.mine(a).start()
        for a in range(self.n):
            for cp in self.first(a):
                cp.start()

    def forward(self):
        c = self.me[2]
        for a in range(self.n):
            fwd = self.forwards(a)
            for j, chip in enumerate(self.chips):
                self.copy(a, 1 + j, (*chip, c), self.me).wait_recv()
                fwd[j].start()

    def finish(self):
        c = self.me[2]
        for a in range(self.n):
            self.copy(a, 0, self.sib, self.me).wait_recv()
            for j, chip in enumerate(self.chips):
                self.copy(a, 4 + j, (*chip, 1 - c), self.me).wait_recv()
        for a in range(self.n):
            for cp in self.first(a) + self.forwards(a):
                cp.wait_send()
            self.mine(a).wait()


def _all_gather(arrs, swaps, name):
    n = len(arrs)

    def body(*refs):
        g = _Gather(refs[:n], refs[n:2 * n], *refs[2 * n:], swaps)
        g.start()
        g.forward()
        g.finish()

    any_spec = pl.BlockSpec(memory_space=pl.ANY)
    return pl.pallas_call(
        body, name=name,
        in_specs=[any_spec] * n, out_specs=[any_spec] * n,
        out_shape=[jax.ShapeDtypeStruct((NDEV,) + a.shape, a.dtype) for a in arrs],
        scratch_shapes=_Gather.scratch(n),
    )(*arrs)


def _pair_copies(ins, gots, send, recv, swaps):
    x, y, c = lax.axis_index("x"), lax.axis_index("y"), lax.axis_index("c")
    return [pltpu.make_async_remote_copy(
        src_ref=ins[a].at[_slot(swaps[a], k // 2, k % 2, 1 - c)], dst_ref=gots[a].at[k],
        send_sem=send.at[4 * a + k], recv_sem=recv.at[4 * a + k], device_id=(x, y, 1 - c), device_id_type=MESH)
        for a in range(len(ins)) for k in range(4)]


def _pair_exchange(arrs, swaps, name):
    n = len(arrs)

    def body(*refs):
        rems = _pair_copies(refs[:n], refs[n:2 * n], *refs[2 * n:], swaps)
        for rc in rems:
            rc.start()
        for rc in rems:
            rc.wait_recv()
        for rc in rems:
            rc.wait_send()

    any_spec = pl.BlockSpec(memory_space=pl.ANY)
    return pl.pallas_call(
        body, name=name,
        in_specs=[any_spec] * n, out_specs=[any_spec] * n,
        out_shape=[jax.ShapeDtypeStruct((4,) + a.shape[1:], a.dtype) for a in arrs],
        scratch_shapes=[pltpu.SemaphoreType.DMA((4 * n,)), pltpu.SemaphoreType.DMA((4 * n,))],
    )(*arrs)


def _chip_copies(ins, outs, send, recv):
    x, y, c = lax.axis_index("x"), lax.axis_index("y"), lax.axis_index("c")
    chips = [(1 - x, y), (x, 1 - y), (1 - x, 1 - y)]
    return [pltpu.make_async_remote_copy(
        src_ref=ins[a].at[2 * px + py], dst_ref=outs[a].at[j], send_sem=send.at[3 * a + j], recv_sem=recv.at[3 * a + j],
        device_id=(px, py, c), device_id_type=MESH) for a in range(len(ins)) for j, (px, py) in enumerate(chips)]


def _chip_exchange(arrs, name):
    n = len(arrs)

    def body(*refs):
        rems = _chip_copies(refs[:n], refs[n:2 * n], *refs[2 * n:])
        for rc in rems:
            rc.start()
        for rc in rems:
            rc.wait_recv()
        for rc in rems:
            rc.wait_send()

    any_spec = pl.BlockSpec(memory_space=pl.ANY)
    return pl.pallas_call(
        body, name=name,
        in_specs=[any_spec] * n, out_specs=[any_spec] * n,
        out_shape=[jax.ShapeDtypeStruct((3,) + a.shape[1:], a.dtype) for a in arrs],
        scratch_shapes=[pltpu.SemaphoreType.DMA((3 * n,)), pltpu.SemaphoreType.DMA((3 * n,))],
    )(*arrs)


class _Side:
    def __init__(self, arrs, out_shapes, scratch, start, finish, mid=None):
        self.arrs, self.out_shapes, self.scratch = list(arrs), list(out_shapes), list(scratch)
        self.start, self.finish, self.mid = start, finish, mid


def _copies_side(arrs, out_shapes, n_copies, make):
    def start(ins, outs, *sems):
        for cp in make(ins, outs, *sems):
            cp.start()

    def finish(ins, outs, *sems):
        cps = make(ins, outs, *sems)
        for cp in cps:
            cp.wait_recv()
        for cp in cps:
            cp.wait_send()

    return _Side(arrs, out_shapes, [pltpu.SemaphoreType.DMA((n_copies,)), pltpu.SemaphoreType.DMA((n_copies,))], start, finish)


def _pair_side(arrs, swaps):
    return _copies_side(arrs, [jax.ShapeDtypeStruct((4,) + a.shape[1:], a.dtype) for a in arrs], 4 * len(arrs),
                        functools.partial(_pair_copies, swaps=swaps))


def _chip_side(arrs):
    return _copies_side(arrs, [jax.ShapeDtypeStruct((3,) + a.shape[1:], a.dtype) for a in arrs], 3 * len(arrs), _chip_copies)


def _gather_side(arrs, swaps):
    return _Side(arrs, [jax.ShapeDtypeStruct((NDEV,) + a.shape, a.dtype) for a in arrs], _Gather.scratch(len(arrs)),
                 start=lambda ins, outs, *sems: _Gather(ins, outs, *sems, swaps).start(),
                 mid=lambda ins, outs, *sems: _Gather(ins, outs, *sems, swaps).forward(),
                 finish=lambda ins, outs, *sems: _Gather(ins, outs, *sems, swaps).finish())


def _row_call(body, side, *, name, steps, in_specs, out_specs, out_shape, ins):
    if side is None:
        res = pl.pallas_call(body, name=name, grid=(steps,), in_specs=in_specs, out_specs=out_specs, out_shape=out_shape,
                             compiler_params=_params("arbitrary"))(*ins)
        return list(res), []
    n_in, n_out, ne = len(in_specs), len(out_specs), len(side.arrs)

    def wrapped(*refs):
        e_in = refs[n_in:n_in + ne]
        e_out = refs[n_in + ne + n_out:n_in + 2 * ne + n_out]
        sems = refs[n_in + 2 * ne + n_out:]
        i = pl.program_id(0)

        @pl.when(i == 0)
        def _():
            side.start(e_in, e_out, *sems)

        if side.mid is not None:
            @pl.when(i == steps // 2)
            def _():
                side.mid(e_in, e_out, *sems)

        body(*refs[:n_in], *refs[n_in + ne:n_in + ne + n_out])

        @pl.when(i == steps - 1)
        def _():
            side.finish(e_in, e_out, *sems)

    any_spec = pl.BlockSpec(memory_space=pl.ANY)
    res = pl.pallas_call(
        wrapped, name=name, grid=(steps,), in_specs=list(in_specs) + [any_spec] * ne,
        out_specs=list(out_specs) + [any_spec] * ne, out_shape=list(out_shape) + side.out_shapes,
        scratch_shapes=side.scratch,
        compiler_params=_params("arbitrary"))(*ins, *side.arrs)
    return list(res[:n_out]), list(res[n_out:])


def _pair_sum(grads, got, swap, core, name):
    _, r, c = got.shape
    tr = r if r <= 352 else r // 2

    def own_map(k, i, core_ref):
        return (_slot(swap, k // 2, k % 2, core_ref[0]), i, 0)

    def body(core_ref, a_ref, b_ref, o_ref):
        o_ref[...] = a_ref[...] + b_ref[...]

    spec = pl.BlockSpec((None, tr, c), lambda k, i, core_ref: (k, i, 0))
    return pl.pallas_call(
        body, name=name,
        grid_spec=pltpu.PrefetchScalarGridSpec(
            num_scalar_prefetch=1, grid=(4, r // tr),
            in_specs=[pl.BlockSpec((None, tr, c), own_map), spec], out_specs=spec),
        out_shape=jax.ShapeDtypeStruct(got.shape, got.dtype), compiler_params=_params("parallel", "parallel"),
    )(core, grads, got)


def _adamw_math(w, g, m, v):
    m = ADAM_B1 * m + (1.0 - ADAM_B1) * g
    v = ADAM_B2 * v + (1.0 - ADAM_B2) * (g * g)
    m_hat = m / (1.0 - ADAM_B1 ** ADAM_STEP)
    v_hat = v / (1.0 - ADAM_B2 ** ADAM_STEP)
    delta = -ADAM_LR * (m_hat / (jnp.sqrt(v_hat) + ADAM_EPS) + ADAM_WD * w)
    return delta, m, v


def _adamw_tile(r):
    for cand in (256, 352, 128):
        if r % cand == 0:
            return cand
    return r


def _adamw(w, m, v, g, name):
    r, c = w.shape
    tr = _adamw_tile(r)
    spec = pl.BlockSpec((tr, c), lambda i: (i, 0))

    def body(w_ref, m_ref, v_ref, g_ref, d_ref, nm_ref, nv_ref):
        d_ref[...], nm_ref[...], nv_ref[...] = _adamw_math(w_ref[...], g_ref[...], m_ref[...], v_ref[...])

    out = jax.ShapeDtypeStruct((r, c), F32)
    return pl.pallas_call(
        body, name=name, grid=(r // tr,), in_specs=[spec] * 4, out_specs=[spec] * 3, out_shape=[out] * 3,
        compiler_params=_params("parallel"),
    )(w, m, v, g)


def _adamw_reduce(w, m, v, sums, recv, chip, name):
    r, c = w.shape
    tr = _adamw_tile(r)
    spec = pl.BlockSpec((tr, c), lambda i, chip_ref: (i, 0))

    def body(chip_ref, w_ref, m_ref, v_ref, s_ref, p_ref, g_ref, d_ref, nm_ref, nv_ref):
        g = ((s_ref[...] + p_ref[0]) + p_ref[1]) + p_ref[2]
        g_ref[...] = g
        d_ref[...], nm_ref[...], nv_ref[...] = _adamw_math(w_ref[...], g, m_ref[...], v_ref[...])

    out = jax.ShapeDtypeStruct((r, c), F32)
    return pl.pallas_call(
        body, name=name,
        grid_spec=pltpu.PrefetchScalarGridSpec(
            num_scalar_prefetch=1, grid=(r // tr,),
            in_specs=[spec, spec, spec, pl.BlockSpec((None, tr, c), lambda i, chip_ref: (chip_ref[0], i, 0)),
                      pl.BlockSpec((3, tr, c), lambda i, chip_ref: (0, i, 0))],
            out_specs=[spec] * 4),
        out_shape=[out] * 4, compiler_params=_params("parallel"),
    )(chip, w, m, v, sums, recv)


def _vec(n):
    return pl.BlockSpec((1, n), lambda *_: (0, 0))


def _ln_mod(x, g, scale, shift, *, ts, name, side=None):
    s = x.shape[0]
    row = pl.BlockSpec((ts, D), lambda i: (i, 0))

    def body(x_ref, g_ref, sc_ref, sh_ref, h_ref):
        xv = x_ref[...]
        r = lax.rsqrt(jnp.mean(xv * xv, axis=-1, keepdims=True) + EPS)
        h = (xv * r) * g_ref[...]
        h_ref[...] = (h * (1.0 + sc_ref[...]) + sh_ref[...]).astype(BF16)

    (h,), extra = _row_call(body, side, name=name, steps=s // ts, in_specs=[row, _vec(D), _vec(D), _vec(D)],
                            out_specs=[row], out_shape=[jax.ShapeDtypeStruct((s, D), BF16)], ins=(x, g, scale, shift))
    return h, extra


def _proj_res_ln_mod(mix, w, x, gate, g, scale, shift, *, tm, name):
    s = x.shape[0]
    row = pl.BlockSpec((tm, D), lambda i: (i, 0))

    def body(m_ref, w_ref, x_ref, gt_ref, g_ref, sc_ref, sh_ref, a_ref, x1_ref, h_ref):
        att = _dot(m_ref[...], w_ref[...])
        a_ref[...] = att.astype(BF16)
        x1 = x_ref[...] + gt_ref[...] * att
        x1_ref[...] = x1
        r = lax.rsqrt(jnp.mean(x1 * x1, axis=-1, keepdims=True) + EPS)
        h = (x1 * r) * g_ref[...]
        h_ref[...] = (h * (1.0 + sc_ref[...]) + sh_ref[...]).astype(BF16)

    return pl.pallas_call(
        body, name=name, grid=(s // tm,), in_specs=[row, _full(w.shape), row] + [_vec(D)] * 4, out_specs=[row, row, row],
        out_shape=[jax.ShapeDtypeStruct((s, D), BF16), jax.ShapeDtypeStruct((s, D), F32), jax.ShapeDtypeStruct((s, D), BF16)],
        compiler_params=_params("parallel"),
    )(mix, w, x, gate, g, scale, shift)


def _proj_loss_head(act, w, x1, tgt, gate2, *, tm, name):
    s = x1.shape[0]
    n = s // tm
    row = pl.BlockSpec((tm, D), lambda i: (i, 0))
    acc8 = pl.BlockSpec((SUBLANES, D), lambda i: (0, 0))

    def body(a_ref, w_ref, x_ref, t_ref, g_ref, dy_ref, df_ref, dg_ref, loss_ref, lacc):
        i = pl.program_id(0)

        @pl.when(i == 0)
        def _():
            lacc[...] = jnp.zeros_like(lacc)
            dg_ref[...] = jnp.zeros_like(dg_ref)

        f = _dot(a_ref[...], w_ref[...])
        diff = x_ref[...] + g_ref[...] * f - t_ref[...]
        lacc[...] += _colsum8(diff * diff)
        dy = diff * (1.0 / D)
        dy_ref[...] = dy
        df_ref[...] = (dy * g_ref[...]).astype(BF16)
        dg_ref[...] += _colsum8(dy * f)

        @pl.when(i == n - 1)
        def _():
            loss_ref[...] = jnp.full((SUBLANES, LANES), (0.5 / D) * jnp.sum(lacc[...]), F32)

    return pl.pallas_call(
        body, name=name, grid=(n,),
        in_specs=[pl.BlockSpec((tm, act.shape[1]), lambda i: (i, 0)), _full(w.shape), row, row, _vec(D)],
        out_specs=[row, row, acc8, _full((SUBLANES, LANES))],
        out_shape=[jax.ShapeDtypeStruct((s, D), F32), jax.ShapeDtypeStruct((s, D), BF16),
                   jax.ShapeDtypeStruct((SUBLANES, D), F32), jax.ShapeDtypeStruct((SUBLANES, LANES), F32)],
        scratch_shapes=[pltpu.VMEM((SUBLANES, D), F32)], compiler_params=_params("arbitrary"),
    )(act, w, x1, tgt, gate2)


def _group_rsqrt(t, bd):
    return lax.rsqrt(_split_dot(t * t, bd) * (1.0 / HD) + EPS)


def _in_proj_qk_norm(h, w, qg, kg, bd, *, tm, name):
    s = h.shape[0]

    def body(h_ref, w_ref, qg_ref, kg_ref, bd_ref, p_ref, o_ref):
        bdv = bd_ref[...]
        hv = h_ref[...]
        p_ref[:, 0:DP] = _dot(hv, w_ref[0:DP, :], NT)
        q = _dot(hv, w_ref[DP:DP + DA, :], NT)
        p_ref[:, DP:DP + DA] = q
        o_ref[:, 0:DA] = (q * _group_rsqrt(q, bdv) * qg_ref[...]).astype(BF16)
        k = _dot(hv, w_ref[DP + DA:DP + 2 * DA, :], NT)
        p_ref[:, DP + DA:DP + 2 * DA] = k
        o_ref[:, DA:2 * DA] = (k * _group_rsqrt(k, bdv) * kg_ref[...]).astype(BF16)
        v = _dot(hv, w_ref[DP + 2 * DA:, :], NT)
        p_ref[:, DP + 2 * DA:] = v
        o_ref[:, 2 * DA:] = v.astype(BF16)

    return pl.pallas_call(
        body, name=name, grid=(s // tm,),
        in_specs=[pl.BlockSpec((tm, D), lambda i: (i, 0)), _full(w.shape), _vec(DA), _vec(DA), _full((DA, DA))],
        out_specs=[pl.BlockSpec((tm, DIN), lambda i: (i, 0)), pl.BlockSpec((tm, 3 * DA), lambda i: (i, 0))],
        out_shape=[jax.ShapeDtypeStruct((s, DIN), F32), jax.ShapeDtypeStruct((s, 3 * DA), BF16)],
        compiler_params=_params("parallel"),
    )(h, w, qg, kg, bd)


EXP_UNDERFLOW = -120.0


def _log_terms(z):
    neg_abs = lax.bitcast_convert_type(lax.bitcast_convert_type(z, jnp.uint32) | jnp.uint32(0x80000000), F32)
    b = jnp.minimum(z, 0.0) - jnp.log(1.0 + jnp.exp(neg_abs))
    return b, b - z


def _head_masks(rows):
    lane = lax.broadcasted_iota(jnp.int32, (rows, LANES), 1)
    return [lane < HD, lane >= HD]


def _attn_fwd(qkv, gather, swaps, *, tq, tk, hp, name):
    s = qkv.shape[0]
    nrep = tk // LANES
    ndiag = tq // tk
    ng = len(gather)
    nh, wl = 2 * hp, LANES * hp
    ngrp, nq = DA // wl, s // tq
    lanes = [slice(LANES * pp, LANES * (pp + 1)) for pp in range(hp)]

    def body(*refs):
        q_ref, k_ref, v_ref = refs[:3]
        g_in = refs[3:3 + ng]
        o_ref, tot_ref, first_ref = refs[3 + ng:6 + ng]
        g_out = refs[6 + ng:6 + 2 * ng]
        oacc, rc = refs[6 + 2 * ng:8 + 2 * ng]
        g_sems = refs[8 + 2 * ng:]
        i = pl.program_id(1)
        step_id = pl.program_id(0) * nq + i

        @pl.when(step_id == 0)
        def _():
            _Gather(g_in, g_out, *g_sems, swaps).start()

        @pl.when(step_id == (ngrp * nq * 3) // 4)
        def _():
            _Gather(g_in, g_out, *g_sems, swaps).forward()

        heads = _head_masks(tq)
        qs = [jnp.where(heads[a % 2], q_ref[:, lanes[a // 2]] * 0.125, 0.0).astype(BF16) for a in range(nh)]
        dif = lax.broadcasted_iota(jnp.int32, (tq, tk), 0) - lax.broadcasted_iota(jnp.int32, (tq, tk), 1)
        kr = lax.broadcasted_iota(jnp.int32, (tk, tk), 0)
        kc = lax.broadcasted_iota(jnp.int32, (tk, tk), 1)
        later =jnp.where(kr > kc, 1.0, 0.0).astype(BF16)
        oacc[...] = jnp.zeros_like(oacc)
        rc[...] = jnp.zeros_like(rc)

        def tile(kb, thr):
            rows = pl.ds(pl.multiple_of(kb * tk, tk), tk)
            ks = [k_ref[rows, ln] for ln in lanes]
            vs = [v_ref[rows, ln] for ln in lanes]
            qr = slice(0 if thr is None else thr, tq)
            rcv = [rc[a, qr, :] for a in range(nh)]
            zs = [_dot(qs[a][qr], ks[a // 2], NT) for a in range(nh)]
            bs, mbs = [], []
            for a in range(nh):
                b, m = _log_terms(zs[a])
                if thr is not None:
                    m = jnp.where(dif[qr] > thr, m, 0.0)
                bs.append(b)
                mbs.append(m.astype(BF16))
            rl = [_dot(mbs[a], later) for a in range(nh)]
            for a in range(nh):
                p = jnp.exp(bs[a] + (rl[a] + jnp.tile(rcv[a], (1, nrep))))
                if thr is not None:
                    p = jnp.where(dif[qr] > thr, p, 0.0)
                oacc[a, qr, :] += _dot(p.astype(BF16), vs[a // 2])
                rc[a, qr, :] = rcv[a] + (rl[a][:, 0:1] + mbs[a][:, 0:1].astype(F32))

        for d in reversed(range(ndiag)):
            tile(i * ndiag + d, d * tk)

        def live():
            top = rc[0]
            for a in range(1, nh):
                top = jnp.maximum(top, rc[a])
            return jnp.max(top) > EXP_UNDERFLOW

        def step(carry):
            kb, _ = carry
            tile(kb, None)
            return kb - 1, live()

        kb_end, _ = lax.while_loop(lambda cr: jnp.logical_and(cr[0] >= 0, cr[1]), step, (i * ndiag - 1, live()))
        first_ref[pl.program_id(0), i] = (kb_end + 1).astype(F32)
        for pp, ln in enumerate(lanes):
            o_ref[:, ln] = jnp.where(heads[0], oacc[2 * pp], oacc[2 * pp + 1])
            tot_ref[:, ln] = jnp.where(heads[0], rc[2 * pp], rc[2 * pp + 1])

        @pl.when(step_id == ngrp * nq - 1)
        def _():
            _Gather(g_in, g_out, *g_sems, swaps).finish()

    qspec = pl.BlockSpec((tq, wl), lambda p, i: (i, p))
    any_spec = pl.BlockSpec(memory_space=pl.ANY)
    res = pl.pallas_call(
        body, name=name, grid=(ngrp, nq),
        in_specs=[qspec,
                  pl.BlockSpec((s, wl), lambda p, i: (0, ngrp + p)),
                  pl.BlockSpec((s, wl), lambda p, i: (0, 2 * ngrp + p))] + [any_spec] * ng,
        out_specs=[qspec, qspec, pl.BlockSpec(memory_space=pltpu.SMEM)] + [any_spec] * ng,
        out_shape=[jax.ShapeDtypeStruct((s, DA), F32), jax.ShapeDtypeStruct((s, DA), F32),
                   jax.ShapeDtypeStruct((ngrp, nq), F32)]
        + [jax.ShapeDtypeStruct((NDEV,) + a.shape, a.dtype) for a in gather],
        scratch_shapes=[pltpu.VMEM((nh, tq, LANES), F32), pltpu.VMEM((nh, tq, LANES), F32)] + _Gather.scratch(ng),
        compiler_params=_params("arbitrary", "arbitrary"),
    )(qkv, qkv, qkv, *gather)
    return res[0], res[1], res[2], res[3:]


def _attn_bwd(qkv, do, tot, first, exchange, *, tq, tk, hp, name):
    s = qkv.shape[0]
    nrep = tk // LANES
    ndiag = tq // tk
    ne = len(exchange)
    nh, wl = 2 * hp, LANES * hp
    ngrp, nq = DA // wl, s // tq
    lanes = [slice(LANES * pp, LANES * (pp + 1)) for pp in range(hp)]

    def body(*refs):
        q_ref, k_ref, v_ref, do_ref, tot_ref, first_ref = refs[:6]
        e_in = refs[6:6 + ne]
        dq_ref, dk_ref, dv_ref = refs[6 + ne:9 + ne]
        e_out = refs[9 + ne:9 + 2 * ne]
        dqacc, rem, gc = refs[9 + 2 * ne:12 + 2 * ne]
        e_sems = refs[12 + 2 * ne:]
        i = pl.program_id(1)
        step_id = pl.program_id(0) * nq + i

        @pl.when(step_id == 0)
        def _():
            for cp in _chip_copies(e_in, e_out, *e_sems):
                cp.start()

        @pl.when(i == 0)
        def _():
            dk_ref[...] = jnp.zeros_like(dk_ref)
            dv_ref[...] = jnp.zeros_like(dv_ref)

        heads = _head_masks(tq)
        qs = [jnp.where(heads[a % 2], q_ref[:, lanes[a // 2]] * 0.125, 0.0).astype(BF16) for a in range(nh)]
        dob = [jnp.where(heads[a % 2], do_ref[:, lanes[a // 2]], 0.0).astype(BF16) for a in range(nh)]
        dif = lax.broadcasted_iota(jnp.int32, (tq, tk), 0) - lax.broadcasted_iota(jnp.int32, (tq, tk), 1)
        kr = lax.broadcasted_iota(jnp.int32, (tk, tk), 0)
        kc = lax.broadcasted_iota(jnp.int32, (tk, tk), 1)
        up_incl = jnp.where(kr <= kc, 1.0, 0.0).astype(BF16)
        up_strict = jnp.where(kr < kc, 1.0, 0.0).astype(BF16)
        dqacc[...] = jnp.zeros_like(dqacc)
        gc[...] = jnp.zeros_like(gc)
        for pp, ln in enumerate(lanes):
            totv = tot_ref[:, ln]
            swapped = pltpu.roll(totv, HD, axis=1)
            rem[2 * pp] = jnp.where(heads[0], totv, swapped)
            rem[2 * pp + 1] = jnp.where(heads[1], totv, swapped)

        def tile(kb, thr):
            rows = pl.ds(pl.multiple_of(kb * tk, tk), tk)
            ks = [k_ref[rows, ln] for ln in lanes]
            vs = [v_ref[rows, ln] for ln in lanes]
            qr = slice(0 if thr is None else thr, tq)
            remv = [rem[a, qr, :] for a in range(nh)]
            gcv = [gc[a, qr, :] for a in range(nh)]
            zs = [_dot(qs[a][qr], ks[a // 2], NT) for a in range(nh)]
            das = [_dot(dob[a][qr], vs[a // 2], NT) for a in range(nh)]
            bs, mbs = [], []
            for a in range(nh):
                b, m = _log_terms(zs[a])
                if thr is not None:
                    m = jnp.where(dif[qr] > thr, m, 0.0)
                bs.append(b)
                mbs.append(m.astype(BF16))
            pl_ = [_dot(mbs[a], up_incl) for a in range(nh)]
            ps, gs, gbs = [], [], []
            for a in range(nh):
                p = jnp.exp(bs[a] + (jnp.tile(remv[a], (1, nrep)) - pl_[a]))
                if thr is not None:
                    p = jnp.where(dif[qr] > thr, p, 0.0)
                g = p * das[a]
                ps.append(p.astype(BF16))
                gs.append(g)
                gbs.append(g.astype(BF16))
            cl = [_dot(gbs[a], up_strict) for a in range(nh)]
            dk_add = [jnp.zeros((tk, LANES), F32) for _ in range(hp)]
            dv_add = [jnp.zeros((tk, LANES), F32) for _ in range(hp)]
            for a in range(nh):
                dz = gs[a] - jnp.exp(bs[a]) * (gs[a] + (jnp.tile(gcv[a], (1, nrep)) + cl[a]))
                if thr is not None:
                    dz = jnp.where(dif[qr] > thr, dz, 0.0)
                dzb = dz.astype(BF16)
                dqacc[a, qr, :] += _dot(dzb, ks[a // 2])
                dk_add[a // 2] += _dot(dzb, qs[a][qr], TN)
                dv_add[a // 2] += _dot(ps[a], dob[a][qr], TN)
                rem[a, qr, :] = remv[a] - pl_[a][:, tk - 1:tk]
                gc[a, qr, :] = gcv[a] + (cl[a][:, tk - 1:tk] + gbs[a][:, tk - 1:tk].astype(F32))
            for pp, ln in enumerate(lanes):
                dk_ref[rows, ln] += dk_add[pp]
                dv_ref[rows, ln] += dv_add[pp]

        def step(kb, carry):
            tile(kb, None)
            return carry

        first_kb = first_ref[(pl.program_id(0) * first.shape[0]) // ngrp, i].astype(jnp.int32)
        lax.fori_loop(first_kb, i * ndiag, step, 0)
        for d in range(ndiag):
            tile(i * ndiag + d, d * tk)
        for pp, ln in enumerate(lanes):
            dq_ref[:, ln] = jnp.where(heads[0], dqacc[2 * pp], dqacc[2 * pp + 1]) * 0.125

        @pl.when(step_id == ngrp * nq - 1)
        def _():
            cps = _chip_copies(e_in, e_out, *e_sems)
            for cp in cps:
                cp.wait_recv()
            for cp in cps:
                cp.wait_send()

    qspec = pl.BlockSpec((tq, wl), lambda p, i: (i, p))
    full = pl.BlockSpec((s, wl), lambda p, i: (0, p))
    any_spec = pl.BlockSpec(memory_space=pl.ANY)
    out = jax.ShapeDtypeStruct((s, DA), F32)
    res = pl.pallas_call(
        body, name=name, grid=(ngrp, nq),
        in_specs=[qspec, pl.BlockSpec((s, wl), lambda p, i: (0, ngrp + p), pipeline_mode=pl.Buffered(1)),
                  pl.BlockSpec((s, wl), lambda p, i: (0, 2 * ngrp + p), pipeline_mode=pl.Buffered(1)), qspec, qspec,
                  pl.BlockSpec(memory_space=pltpu.SMEM)] + [any_spec] * ne,
        out_specs=[qspec, full, full] + [any_spec] * ne,
        out_shape=[out, out, out] + [jax.ShapeDtypeStruct((3,) + a.shape[1:], a.dtype) for a in exchange],
        scratch_shapes=[pltpu.VMEM((nh, tq, LANES), F32)] * 3
        + [pltpu.SemaphoreType.DMA((3 * ne,)), pltpu.SemaphoreType.DMA((3 * ne,))],
        compiler_params=_params("arbitrary", "arbitrary"),
    )(qkv, qkv, qkv, do, tot, first, *exchange)
    return res[0], res[1], res[2], res[3:]


def _shift_rows(v, k):
    return pltpu.roll(v, k % v.shape[0], axis=0)


def _pooled(u, uh, i, g, w, ts):
    halo = jnp.where(i > 0, uh, 0.0)
    ue = jnp.concatenate([halo, u], axis=0)
    acc, span = ue, 1
    while span < w:
        acc = acc + _shift_rows(acc, span)
        span *= 2
    tpos = i * ts + lax.broadcasted_iota(jnp.int32, (ts, 1), 0)
    cnt = jnp.minimum(tpos + 1, w).astype(F32)
    return acc[HALO:] / cnt - u


def _pool_mix(proj, o, pw, pb, ps, ag, bd, *, ts, name):
    s = proj.shape[0]
    hb = ts // HALO

    def body(u_ref, uh_ref, o_ref, pw_ref, pb_ref, ps_ref, ag_ref, bd_ref, mix_ref):
        i = pl.program_id(0)
        for g, w in enumerate(POOL_WINDOWS):
            cols = slice(g * LANES, (g + 1) * LANES)
            pooled = _pooled(u_ref[:, cols], uh_ref[:, cols], i, g, w, ts)
            yv = (_dot(pooled.astype(BF16), pw_ref[g]) + pb_ref[:, cols]) * ps_ref[:, cols]
            mix_ref[:, cols] = yv.astype(BF16)
        ov = o_ref[...]
        mix_ref[:, DP:] = (ov * _group_rsqrt(ov, bd_ref[...]) * ag_ref[...]).astype(BF16)

    return pl.pallas_call(
        body, name=name, grid=(s // ts,),
        in_specs=[pl.BlockSpec((ts, DP), lambda i: (i, 0)),
                  pl.BlockSpec((HALO, DP), lambda i: (jnp.maximum(i * hb - 1, 0), 0)),
                  pl.BlockSpec((ts, DA), lambda i: (i, 0)),
                  _full((4, LANES, LANES)), _vec(DP), _vec(DP), _vec(DA), _full((DA, DA))],
        out_specs=pl.BlockSpec((ts, D), lambda i: (i, 0)),
        out_shape=jax.ShapeDtypeStruct((s, D), BF16), compiler_params=_params("parallel"),
    )(proj, proj, o, pw, pb, ps, ag, bd)


CF = DFF // 2
MXU_COLS = 256


def _sigmoid(t):
    return 0.5 + 0.5 * jnp.tanh(0.5 * t)


def _sub_chunks(width):
    return [(c0, min(MXU_COLS, width - c0)) for c0 in range(0, width, MXU_COLS)]


def _up_conv_gate(h2, w_up, cw, cb, *, tm, name):
    s = h2.shape[0]
    hb = tm // HALO

    def body(a_ref, ah_ref, w_ref, cw_ref, cb_ref, up_ref, c_ref, act_ref):
        i = pl.program_id(1)
        halo = jnp.where(i > 0, ah_ref[...], jnp.zeros_like(ah_ref))
        ext = jnp.concatenate([halo, a_ref[...]], axis=0)
        for c0, cwid in _sub_chunks(CF):
            conv = []
            for off in (c0, CF + c0):
                cols = slice(off, off + cwid)
                ue = _dot(ext, w_ref[cols, :], NT)
                up_ref[:, cols] = ue[HALO:].astype(BF16)
                y = cw_ref[2:3, cols] * ue + cw_ref[1:2, cols] * _shift_rows(ue, 1) + cw_ref[0:1, cols] * _shift_rows(ue, 2)
                cv = y[HALO:] + cb_ref[:, cols]
                c_ref[:, cols] = cv.astype(BF16)
                conv.append(cv)
            gt, vl = conv
            act_ref[:, c0:c0 + cwid] = (gt * _sigmoid(gt) * vl).astype(BF16)

    return pl.pallas_call(
        body, name=name, grid=(2, s // tm),
        in_specs=[pl.BlockSpec((tm, D), lambda j, i: (i, 0)),
                  pl.BlockSpec((HALO, D), lambda j, i: (jnp.maximum(i * hb - 1, 0), 0)),
                  pl.BlockSpec((2 * CF, D), lambda j, i: (j, 0)),
                  pl.BlockSpec((3, 2 * CF), lambda j, i: (0, j)), pl.BlockSpec((1, 2 * CF), lambda j, i: (0, j))],
        out_specs=[pl.BlockSpec((tm, 2 * CF), lambda j, i: (i, j)), pl.BlockSpec((tm, 2 * CF), lambda j, i: (i, j)),
                   pl.BlockSpec((tm, CF), lambda j, i: (i, j))],
        out_shape=[jax.ShapeDtypeStruct((s, 2 * DFF), BF16), jax.ShapeDtypeStruct((s, 2 * DFF), BF16),
                   jax.ShapeDtypeStruct((s, DFF), BF16)],
        compiler_params=_params("parallel", "parallel"),
    )(h2, h2, w_up, cw, cb)


def _down_bwd_gate(dffn, w_down, conv, *, tm, name):
    s = dffn.shape[0]

    def body(a_ref, w_ref, c_ref, d_ref, db_ref):
        i = pl.program_id(1)

        @pl.when(i == 0)
        def _():
            db_ref[...] = jnp.zeros_like(db_ref)

        a = a_ref[...]
        for c0, cwid in _sub_chunks(CF):
            gcols, vcols = slice(c0, c0 + cwid), slice(CF + c0, CF + c0 + cwid)
            da = _dot(a, w_ref[gcols, :], NT)
            gt, vl = c_ref[:, gcols].astype(F32), c_ref[:, vcols].astype(F32)
            sg = _sigmoid(gt)
            dgt = da * vl * (sg * (1.0 + gt * (1.0 - sg)))
            dvl = da * (gt * sg)
            d_ref[:, gcols] = dgt.astype(BF16)
            d_ref[:, vcols] = dvl.astype(BF16)
            db_ref[:, gcols] += _colsum8(dgt)
            db_ref[:, vcols] += _colsum8(dvl)

    return pl.pallas_call(
        body, name=name, grid=(2, s // tm),
        in_specs=[pl.BlockSpec((tm, D), lambda j, i: (i, 0)), pl.BlockSpec((CF, D), lambda j, i: (j, 0)),
                  pl.BlockSpec((tm, 2 * CF), lambda j, i: (i, j))],
        out_specs=[pl.BlockSpec((tm, 2 * CF), lambda j, i: (i, j)), pl.BlockSpec((SUBLANES, 2 * CF), lambda j, i: (0, j))],
        out_shape=[jax.ShapeDtypeStruct((s, 2 * DFF), BF16), jax.ShapeDtypeStruct((SUBLANES, 2 * DFF), F32)],
        compiler_params=_params("parallel", "arbitrary"),
    )(dffn, w_down, conv)


def _conv_bwd_up_bwd(dc, up, cw, w_up, *, tm, name):
    s = up.shape[0]
    hb = tm // HALO
    nb = s // HALO
    nk = 2 * DFF // CF
    n = s // tm

    def body(d_ref, dn_ref, u_ref, cw_ref, w_ref, du_ref, dh_ref, dw_ref, acc, dwacc):
        i, k = pl.program_id(0), pl.program_id(1)

        @pl.when(jnp.logical_and(i == 0, k == 0))
        def _():
            dwacc[...] = jnp.zeros_like(dwacc)

        @pl.when(k == 0)
        def _():
            acc[...] = jnp.zeros_like(acc)

        live_next = i < n - 1
        part = None
        for c0, cwid in _sub_chunks(CF):
            cols = slice(c0, c0 + cwid)
            dcur = d_ref[:, cols].astype(F32)
            de = jnp.concatenate([dcur, jnp.where(live_next, dn_ref[:, cols].astype(F32), 0.0)], axis=0)
            d1 = _shift_rows(de, -1)[:tm]
            d2 = _shift_rows(de, -2)[:tm]
            du = (cw_ref[2:3, cols] * dcur + cw_ref[1:2, cols] * d1 + cw_ref[0:1, cols] * d2).astype(BF16)
            du_ref[:, cols] = du
            prod = _dot(du, w_ref[cols, :])
            part = prod if part is None else part + prod
            u = u_ref[:, cols].astype(F32)
            for tap, dsh in ((2, dcur), (1, d1), (0, d2)):
                dwacc[k, SUBLANES * tap:SUBLANES * (tap + 1), cols] += _colsum8(dsh * u)
        acc[...] += part

        @pl.when(k == nk - 1)
        def _():
            dh_ref[...] = acc[...].astype(dh_ref.dtype)

        @pl.when(jnp.logical_and(i == n - 1, k == nk - 1))
        def _():
            dw_ref[...] = dwacc[...]

    res = pl.pallas_call(
        body, name=name, grid=(n, nk),
        in_specs=[pl.BlockSpec((tm, CF), lambda i, k: (i, k)),
                  pl.BlockSpec((HALO, CF), lambda i, k: (jnp.minimum((i + 1) * hb, nb - 1), k)),
                  pl.BlockSpec((tm, CF), lambda i, k: (i, k)),
                  pl.BlockSpec((3, CF), lambda i, k: (0, k)),
                  pl.BlockSpec((CF, D), lambda i, k: (k, 0))],
        out_specs=[pl.BlockSpec((tm, CF), lambda i, k: (i, k)), pl.BlockSpec((tm, D), lambda i, k: (i, 0)),
                   _full((nk, 24, CF))],
        out_shape=[jax.ShapeDtypeStruct((s, 2 * DFF), BF16), jax.ShapeDtypeStruct((s, D), BF16),
                   jax.ShapeDtypeStruct((nk, 24, CF), F32)],
        scratch_shapes=[pltpu.VMEM((tm, D), F32), pltpu.VMEM((nk, 24, CF), F32)],
        compiler_params=_params("arbitrary", "arbitrary"),
    )(dc, dc, up, cw, w_up)
    return res[0], res[1], jnp.transpose(res[2], (1, 0, 2)).reshape(24, 2 * DFF)


def _ln_mod_bwd(dh, xin, g, scale, resid, extra, gate, *, ts, name, side=None):
    s = xin.shape[0]
    row = pl.BlockSpec((ts, D), lambda i: (i, 0))
    acc8 = pl.BlockSpec((SUBLANES, D), lambda i: (0, 0))
    with_gate = extra is not None

    def body(*refs):
        if with_gate:
            dh_ref, x_ref, g_ref, sc_ref, r_ref, e_ref, gt_ref, dx_ref, da_ref, dsh, dsc, dg, dgt = refs
        else:
            dh_ref, x_ref, g_ref, sc_ref, r_ref, dx_ref, dsh, dsc, dg = refs
        i = pl.program_id(0)

        @pl.when(i == 0)
        def _():
            for acc in (dsh, dsc, dg) + ((dgt,) if with_gate else ()):
                acc[...] = jnp.zeros_like(acc)

        xv, dhv = x_ref[...], dh_ref[...].astype(F32)
        r = lax.rsqrt(jnp.mean(xv * xv, axis=-1, keepdims=True) + EPS)
        xn = xv * r
        dsh[...] += _colsum8(dhv)
        dsc[...] += _colsum8(dhv * (xn * g_ref[...]))
        dhp = dhv * (1.0 + sc_ref[...])
        dg[...] += _colsum8(dhp * xn)
        dxn = dhp * g_ref[...]
        dx = r_ref[...] + r * (dxn - xn * jnp.mean(dxn * xn, axis=-1, keepdims=True))
        dx_ref[...] = dx
        if with_gate:
            da_ref[...] = (dx * gt_ref[...]).astype(BF16)
            dgt[...] += _colsum8(dx * e_ref[...].astype(F32))

    f32o, p8 = jax.ShapeDtypeStruct((s, D), F32), jax.ShapeDtypeStruct((SUBLANES, D), F32)
    if with_gate:
        ins, in_specs = (dh, xin, g, scale, resid, extra, gate), [row, row, _vec(D), _vec(D), row, row, _vec(D)]
        out_specs, out_shape = [row, row, acc8, acc8, acc8, acc8], [f32o, jax.ShapeDtypeStruct((s, D), BF16), p8, p8, p8, p8]
    else:
        ins, in_specs = (dh, xin, g, scale, resid), [row, row, _vec(D), _vec(D), row]
        out_specs, out_shape = [row, acc8, acc8, acc8], [f32o, p8, p8, p8]
    return _row_call(body, side, name=name, steps=s // ts, in_specs=in_specs, out_specs=out_specs,
                     out_shape=out_shape, ins=ins)


def _group_norm_bwd(t, dn_out, gvec, bd):
    r = _group_rsqrt(t, bd)
    dg_terms = dn_out * t * r
    dn = dn_out * gvec
    dt = r * (dn - t * (r * r) * (_split_dot(dn * t, bd) * (1.0 / HD)))
    return dt, dg_terms


def _mix_bwd(dmix, proj, o, pw, pb, ps, ag, bd, *, ts, name, side=None):
    s = proj.shape[0]
    hb = ts // HALO
    nb = s // HALO

    def body(dm_ref, dmn_ref, u_ref, uh_ref, o_ref, pw_ref, pb_ref, ps_ref, ag_ref, bd_ref,
             du_ref, do_ref, dpw_ref, dpb_ref, dps_ref, dag_ref):
        i = pl.program_id(0)
        n = s // ts

        @pl.when(i == 0)
        def _():
            for acc in (dpw_ref, dpb_ref, dps_ref, dag_ref):
                acc[...] = jnp.zeros_like(acc)

        for g, w in enumerate(POOL_WINDOWS):
            cols = slice(g * LANES, (g + 1) * LANES)
            wg = pw_ref[g]
            psg = ps_ref[:, cols]
            pooled = _pooled(u_ref[:, cols], uh_ref[:, cols], i, g, w, ts).astype(BF16)
            dy = dm_ref[:, cols].astype(F32)
            dps_ref[:, cols] += _colsum8(dy * (_dot(pooled, wg) + pb_ref[:, cols]))
            dpre = dy * psg
            dpb_ref[:, cols] += _colsum8(dpre)
            dpreb = dpre.astype(BF16)
            dpw_ref[g * LANES:(g + 1) * LANES, :] += _dot(pooled, dpreb, TN)
            dpool = _dot(dpreb, wg, NT)
            dnext = _dot((dmn_ref[:, cols].astype(F32) * psg).astype(BF16), wg, NT)
            dpe = jnp.concatenate([dpool, jnp.where(i < n - 1, dnext, 0.0)], axis=0)
            tpos = i * ts + lax.broadcasted_iota(jnp.int32, (ts + HALO, 1), 0)
            acc = dpe / jnp.minimum(tpos + 1, w).astype(F32)
            span = 1
            while span < w:
                acc = acc + _shift_rows(acc, -span)
                span *= 2
            du_ref[:, cols] = acc[:ts] - dpool
        ov = o_ref[...]
        dov, dg_terms = _group_norm_bwd(ov, dm_ref[:, DP:].astype(F32), ag_ref[...], bd_ref[...])
        do_ref[...] = dov
        dag_ref[...] += _colsum8(dg_terms)

    p8 = jax.ShapeDtypeStruct((SUBLANES, DP), F32)
    acc8 = pl.BlockSpec((SUBLANES, DP), lambda i: (0, 0))
    half = pl.BlockSpec((ts, DP), lambda i: (i, 0))
    return _row_call(
        body, side, name=name, steps=s // ts,
        in_specs=[pl.BlockSpec((ts, D), lambda i: (i, 0)),
                  pl.BlockSpec((HALO, DP), lambda i: (jnp.minimum((i + 1) * hb, nb - 1), 0)),
                  half, pl.BlockSpec((HALO, DP), lambda i: (jnp.maximum(i * hb - 1, 0), 0)),
                  half, _full((4, LANES, LANES)), _vec(DP), _vec(DP), _vec(DA), _full((DA, DA))],
        out_specs=[half, half, _full((DP, LANES)), acc8, acc8, acc8],
        out_shape=[jax.ShapeDtypeStruct((s, DP), F32), jax.ShapeDtypeStruct((s, DA), F32),
                   jax.ShapeDtypeStruct((DP, LANES), F32), p8, p8, p8],
        ins=(dmix, dmix, proj, proj, o, pw, pb, ps, ag, bd))


def _qk_norm_bwd(du, dq, dk, dv, proj, qg, kg, bd, *, ts, name):
    s = proj.shape[0]

    def body(du_ref, dq_ref, dk_ref, dv_ref, q_ref, k_ref, qg_ref, kg_ref, bd_ref, dp_ref, dqg_ref, dkg_ref):
        i = pl.program_id(0)

        @pl.when(i == 0)
        def _():
            dqg_ref[...] = jnp.zeros_like(dqg_ref)
            dkg_ref[...] = jnp.zeros_like(dkg_ref)

        bdv = bd_ref[...]
        dqr, tq = _group_norm_bwd(q_ref[...], dq_ref[...], qg_ref[...], bdv)
        dkr, tk = _group_norm_bwd(k_ref[...], dk_ref[...], kg_ref[...], bdv)
        dqg_ref[...] += _colsum8(tq)
        dkg_ref[...] += _colsum8(tk)
        dp_ref[:, 0:DP] = du_ref[...].astype(BF16)
        dp_ref[:, DP:DP + DA] = dqr.astype(BF16)
        dp_ref[:, DP + DA:DP + 2 * DA] = dkr.astype(BF16)
        dp_ref[:, DP + 2 * DA:] = dv_ref[...].astype(BF16)

    half = pl.BlockSpec((ts, DA), lambda i: (i, 0))
    col = lambda j: pl.BlockSpec((ts, DA), lambda i: (i, j))
    acc8 = pl.BlockSpec((SUBLANES, DA), lambda i: (0, 0))
    p8 = jax.ShapeDtypeStruct((SUBLANES, DA), F32)
    return pl.pallas_call(
        body, name=name, grid=(s // ts,),
        in_specs=[half, half, half, half, col(1), col(2), _vec(DA), _vec(DA), _full((DA, DA))],
        out_specs=[pl.BlockSpec((ts, DIN), lambda i: (i, 0)), acc8, acc8],
        out_shape=[jax.ShapeDtypeStruct((s, DIN), BF16), p8, p8],
        compiler_params=_params("arbitrary"),
    )(du, dq, dk, dv, proj, proj, qg, kg, bd)


def _split3(a):
    hi = a.astype(BF16)
    return hi, (a - hi.astype(F32)).astype(BF16)


def _dot3(a, b, dn):
    ah, al = _split3(a)
    bh, bl = _split3(b)
    return _dot(ah, bh, dn) + (_dot(ah, bl, dn) + _dot(al, bh, dn))


def _ada_fwd(c_all, w, b, name):
    nw = w.shape[1]

    def body(c_ref, w_ref, b_ref, o_ref):
        cv = c_ref[...]
        act = cv / (1.0 + jnp.exp(-cv))
        o_ref[...] = _dot3(act, w_ref[...], NN) + b_ref[...]

    return pl.pallas_call(
        body, name=name, in_specs=[_full((NDEV, D)), _full(w.shape), _full((1, nw))], out_specs=_full((NDEV, nw)),
        out_shape=jax.ShapeDtypeStruct((NDEV, nw), F32), grid=(1,), compiler_params=_params("arbitrary"),
    )(c_all, w, b)


def _ada_bwd(c_all, dmod, name):
    nw = dmod.shape[1]

    def body(c_ref, d_ref, o_ref):
        cv = c_ref[...]
        act = cv / (1.0 + jnp.exp(-cv))
        o_ref[...] = _dot3(act, d_ref[...], TN)[None]

    return pl.pallas_call(
        body, name=name, in_specs=[_full((NDEV, D)), _full((NDEV, nw))], out_specs=_full((1, D, nw)),
        out_shape=jax.ShapeDtypeStruct((1, D, nw), F32), grid=(1,), compiler_params=_params("arbitrary"),
    )(c_all, dmod)


def _fold_heads(v):
    acc = v[:, 0:HD]
    for h in range(1, DA // HD):
        acc = acc + v[:, h * HD:(h + 1) * HD]
    return acc


def _pack_partials(pieces, dcw_p, name):
    n_p = len(pieces)
    total = sum(p.shape[1] for p in pieces) + 3 * dcw_p.shape[1]
    npack = -(-total // (SUBLANES * LANES)) * (SUBLANES * LANES)

    def body(*refs):
        out = refs[-1]
        off = 0
        for r in refs[:n_p]:
            out[:, off:off + r.shape[1]] = jnp.sum(r[...], axis=0, keepdims=True)
            off += r.shape[1]
        dw = refs[n_p]
        for tap in range(3):
            out[:, off:off + dw.shape[1]] = jnp.sum(dw[SUBLANES * tap:SUBLANES * (tap + 1), :], axis=0, keepdims=True)
            off += dw.shape[1]
        if off < npack:
            out[:, off:] = jnp.zeros((1, npack - off), F32)

    arrs = list(pieces) + [dcw_p]
    return pl.pallas_call(
        body, name=name, grid=(1,), in_specs=[_full(a.shape) for a in arrs], out_specs=_full((1, npack)),
        out_shape=jax.ShapeDtypeStruct((1, npack), F32), compiler_params=_params("arbitrary"),
    )(*arrs)


def _small_update(gathered, gathered_pw, gathered_cw, specs, params, loss_off, name):
    names = [sp[0] for sp in specs]
    flat = []
    for nme in names + ["pool_w", "conv_w"]:
        flat += list(params[nme])
    n_in = len(flat)

    def body(*refs):
        ga_ref, gp_ref, gc_ref = refs[0], refs[1], refs[2]
        prm = refs[3:3 + n_in]
        outs = refs[3 + n_in:]
        total = ga_ref[0:1, :]
        for dv in range(1, NDEV):
            total = total + ga_ref[dv:dv + 1, :]
        k = 0
        for idx, (nme, off, width, fold) in enumerate(specs):
            g = total[:, off:off + width]
            if fold:
                g = _fold_heads(g)
            w_ref, m_ref, v_ref = prm[3 * idx:3 * idx + 3]
            d, nm, nv = _adamw_math(w_ref[...], g, m_ref[...], v_ref[...])
            for val in (g, d, nm, nv):
                outs[k][...] = val
                k += 1
        gpw = gp_ref[0]
        for dv in range(1, NDEV):
            gpw = gpw + gp_ref[dv]
        w_ref, m_ref, v_ref = prm[3 * len(specs):3 * len(specs) + 3]
        d, nm, nv = _adamw_math(w_ref[...], gpw, m_ref[...], v_ref[...])
        for val in (gpw, d, nm, nv):
            outs[k][...] = val
            k += 1
        gcw = gc_ref[0]
        for dv in range(1, NDEV):
            gcw = gcw + gc_ref[dv]
        w_ref, m_ref, v_ref = prm[3 * len(specs) + 3:3 * len(specs) + 6]
        d, nm, nv = _adamw_math(w_ref[...], gcw, m_ref[...], v_ref[...])
        for val in (gcw, d, nm, nv):
            outs[k][...] = val
            k += 1
        outs[k][...] = ga_ref[:, 0:6 * D]
        outs[k + 1][...] = total[:, loss_off:loss_off + LANES] * (1.0 / SUBLANES)

    out_shape, out_specs = [], []
    for nme in names + ["pool_w", "conv_w"]:
        shp = params[nme][0].shape
        out_shape += [jax.ShapeDtypeStruct(shp, F32)] * 4
        out_specs += [_full(shp)] * 4
    out_shape += [jax.ShapeDtypeStruct((NDEV, 6 * D), F32), jax.ShapeDtypeStruct((1, LANES), F32)]
    out_specs += [_full((NDEV, 6 * D)), _full((1, LANES))]
    res = pl.pallas_call(
        body, name=name, grid=(1,),
        in_specs=[_full(gathered.shape), _full(gathered_pw.shape), _full(gathered_cw.shape)] + [_full(a.shape) for a in flat],
        out_specs=out_specs, out_shape=out_shape, compiler_params=_params("arbitrary"),
    )(gathered, gathered_pw, gathered_cw, *flat)
    out = {nme: tuple(res[4 * i:4 * i + 4]) for i, nme in enumerate(names + ["pool_w", "conv_w"])}
    return out, res[-2], res[-1][0, 0]


def _row_tile(s):
    return 512 if s % 512 == 0 else s


def kernel(x, c, ada_w, ada_b, norm1_g, w_in, pool_w, pool_b, pool_scale, q_norm_g, k_norm_g, attn_out_g, w_out, norm2_g, w_up, conv_w, conv_b, w_down, loss_target, m_ada_w, m_ada_b, m_norm1_g, m_w_in, m_pool_w, m_pool_b, m_pool_scale, m_q_norm_g, m_k_norm_g, m_attn_out_g, m_w_out, m_norm2_g, m_w_up, m_conv_w, m_conv_b, m_w_down, v_ada_w, v_ada_b, v_norm1_g, v_w_in, v_pool_w, v_pool_b, v_pool_scale, v_q_norm_g, v_k_norm_g, v_attn_out_g, v_w_out, v_norm2_g, v_w_up, v_conv_w, v_conv_b, v_w_down):
    ax, ay, ac = lax.axis_index("x"), lax.axis_index("y"), lax.axis_index("c")
    me = 4 * ax + 2 * ay + ac
    me_swapped = 4 * ay + 2 * ax + ac
    xs, tgt = x[0], loss_target[0]
    s = xs.shape[0]
    ts = _row_tile(s)
    tq_attn, tk_attn, hp_attn = 256, 256, 2
    tmm = 2 * ts
    bd = _block_diag_ones(DA, HD)

    w_in_t = w_in[0].T.astype(BF16)
    w_up_t = w_up[0].T.astype(BF16)
    c_all = _all_gather([jnp.broadcast_to(c, (SUBLANES, D))], [False], "gather_c")[0][:, 0, :]
    n_ada = ada_w.shape[2]
    ada_b_mine = lax.dynamic_slice_in_dim(ada_b, me * n_ada, n_ada, axis=1)
    mod_part = _ada_fwd(c_all, ada_w[0], ada_b_mine, "ada_fwd")
    mod_all = _all_gather([mod_part], [False], "gather_mod")[0]
    mod = lax.dynamic_index_in_dim(mod_all, me, axis=1, keepdims=False).reshape(1, 6 * D)
    shift1, scale1, gate1, shift2, scale2, gate2 = [mod[:, k * D:(k + 1) * D] for k in range(6)]

    later_w = [w_out[0].astype(BF16), w_up_t, w_down[0].astype(BF16)]
    cb_full = jnp.transpose(conv_b.reshape(1, 2, 2, 2, 704), (0, 2, 1, 3, 4)).reshape(1, 2 * DFF)

    qg = jnp.tile(q_norm_g, (1, DA // HD))
    kg = jnp.tile(k_norm_g, (1, DA // HD))
    ag = attn_out_g.reshape(1, DA)
    pw = pool_w[0].astype(BF16)
    pb = pool_b.reshape(1, DP)
    h1, (gw_in, gcw) = _ln_mod(xs, norm1_g, scale1, shift1, ts=tmm, name="ln1",
                               side=_gather_side([w_in_t, jnp.pad(conv_w[0], ((0, 5), (0, 64)))], [False, True]))
    w_in_full = gw_in.reshape(DIN, D)
    cw_full = jnp.transpose(gcw[:, :3, :704], (1, 0, 2)).reshape(3, 2 * DFF)
    proj, qkv = _in_proj_qk_norm(h1, w_in_full, qg, kg, bd, tm=ts, name="in_proj_qk_norm")
    o_raw, m_tot, kb_first, (gw_out, gw_up, gw_down) = _attn_fwd(
        qkv, later_w, [False, True, False], tq=tq_attn, tk=tk_attn, hp=2 * hp_attn, name="attn_fwd")
    w_out_full = gw_out.reshape(D, D)
    w_up_full = gw_up.reshape(2 * DFF, D)
    w_down_full = gw_down.reshape(DFF, D)
    mix = _pool_mix(proj, o_raw, pw, pb, pool_scale, ag, bd, ts=tmm, name="pool_mix")
    att, x1, h2 = _proj_res_ln_mod(mix, w_out_full, xs, gate1, norm2_g, scale2, shift2, tm=ts, name="out_proj_ln2")
    up, conv, act = _up_conv_gate(h2, w_up_full, cw_full, cb_full, tm=tmm, name="up_conv_gate")
    dy, dffn, dgate2_p, loss_p = _proj_loss_head(act, w_down_full, x1, tgt, gate2, tm=ts, name="down_proj_loss")

    g_w_down = _matmul(act, dffn, mode="tn", out_dtype=F32, tm=CF, tn=D, tk=2 * tmm, name="down_wgrad")
    dconv, dcb_p = _down_bwd_gate(dffn, w_down_full, conv, tm=tmm, name="down_bwd_gate")
    dup, dh2, dcw_p = _conv_bwd_up_bwd(dconv, up, cw_full, w_up_full, tm=ts, name="conv_bwd_up_bwd")
    g_w_up_t = _matmul(dup, h2, mode="tn", out_dtype=F32, tm=CF, tn=D, tk=2 * tmm, name="up_wgrad")
    (dx1, datt, dshift2_p, dscale2_p, dnorm2_p, dgate1_p), _ = _ln_mod_bwd(
        dh2, x1, norm2_g, scale2, dy, att, gate1, ts=ts, name="ln2_bwd")

    dmix = _matmul(datt, w_out_full, mode="nt", out_dtype=BF16, tm=tmm,tn=D, tk=D, name="out_bwd")
    g_w_out = _matmul(mix, datt, mode="tn", out_dtype=F32, tm=D, tn=D, tk=2 * tmm, name="out_wgrad")
    core = jnp.reshape(ac, (1,)).astype(jnp.int32)
    chip = jnp.reshape(2 * ax + ay, (1,)).astype(jnp.int32)
    big_ffn = [g_w_up_t.reshape(NDEV, 2 * DFF // NDEV, D), g_w_down.reshape(NDEV, DFF // NDEV, D),
               g_w_out.reshape(NDEV, D // NDEV, D)]
    swaps_ffn = [True, False, False]
    (du, do_raw, g_pw_p, dpb_p, dps_p, dag_p), gots_ffn = _mix_bwd(
        dmix, proj, o_raw, pw, pb, pool_scale, ag, bd, ts=tmm, name="mix_bwd", side=_pair_side(big_ffn, swaps_ffn))
    sums_ffn = [_pair_sum(big_ffn[k], gots_ffn[k], swaps_ffn[k], core, "rs_pair_sum_ffn%d" % k) for k in range(3)]
    dqn, dkn, dvv, parts_ffn = _attn_bwd(qkv, do_raw, m_tot, kb_first, sums_ffn, tq=tq_attn, tk=tk_attn, hp=hp_attn, name="attn_bwd")
    dproj, dqg_p, dkg_p = _qk_norm_bwd(du, dqn, dkn, dvv, proj, qg, kg, bd, ts=tmm, name="qk_norm_bwd")
    g_w_in_t = _matmul(dproj, h1, mode="tn", out_dtype=F32, tm=DIN // 2, tn=D, tk=2 * tmm, name="in_wgrad")
    big = [g_w_in_t.reshape(NDEV, DIN // NDEV, D)]
    gots = _pair_exchange(big, [False], "rs_pair")
    sums = [_pair_sum(big[0], gots[0], False, core, "rs_pair_sum")]
    dh1, parts = _matmul(dproj, w_in_full, mode="nn", out_dtype=BF16, tm=tmm,tn=D, tk=DIN, name="in_bwd",
                         side=_chip_side(sums))
    (grad_x, dshift1_p, dscale1_p, dnorm1_p), _ = _ln_mod_bwd(
        dh1, xs, norm1_g, scale1, dx1, None, None, ts=tmm, name="ln1_bwd")

    tr = lambda a: a[0].T
    r_in = _adamw_reduce(tr(w_in), tr(m_w_in), tr(v_w_in), sums[0], parts[0], chip, "adamw_w_in")
    r_out = _adamw_reduce(w_out[0], m_w_out[0], v_w_out[0], sums_ffn[2], parts_ffn[2], chip, "adamw_w_out")
    r_up = _adamw_reduce(tr(w_up), tr(m_w_up), tr(v_w_up), sums_ffn[0], parts_ffn[0], chip, "adamw_w_up")
    r_down = _adamw_reduce(w_down[0], m_w_down[0], v_w_down[0], sums_ffn[1], parts_ffn[1], chip, "adamw_w_down")
    r_in = [a.T[None] for a in r_in]
    r_up = [a.T[None] for a in r_up]
    r_out = [a[None] for a in r_out]
    r_down = [a[None] for a in r_down]

    dcb_nat = jnp.transpose(dcb_p.reshape(SUBLANES, 2, 2, 2, 704), (0, 2, 1, 3, 4)).reshape(SUBLANES, 2 * DFF)
    pieces = [dshift1_p, dscale1_p, dgate1_p, dshift2_p, dscale2_p, dgate2_p,
              dnorm1_p, dnorm2_p, dcb_nat, dpb_p, dps_p, dag_p, dqg_p, dkg_p, loss_p]
    n_vec = sum(p.shape[1] for p in pieces)
    packed = _pack_partials(pieces, dcw_p, "pack_partials")
    npack = packed.shape[1]
    gathered, gathered_pw = _all_gather([packed.reshape(SUBLANES, npack // SUBLANES), g_pw_p], [False, False], "gather_small")
    gathered = gathered.reshape(NDEV, npack)
    gathered_cw = lax.dynamic_index_in_dim(
        gathered[:, n_vec:n_vec + 6 * DFF].reshape(NDEV, 3, NDEV, 704), me_swapped, axis=2, keepdims=False)
    specs = [("ada_b", 0, 6 * D, False)]
    off = 6 * D
    for nme, width, fold in (("norm1_g", D, False), ("norm2_g", D, False), ("conv_b", 2 * DFF, False),
                             ("pool_b", DP, False), ("pool_scale", DP, False), ("attn_out_g", DA, False),
                             ("q_norm_g", DA, True), ("k_norm_g", DA, True)):
        specs.append((nme, off, width, fold))
        off += width
    small = {
        "ada_b": (ada_b, m_ada_b, v_ada_b),
        "norm1_g": (norm1_g, m_norm1_g, v_norm1_g), "norm2_g": (norm2_g, m_norm2_g, v_norm2_g),
        "conv_b": (conv_b, m_conv_b, v_conv_b),
        "pool_b": (pb, m_pool_b.reshape(1, DP), v_pool_b.reshape(1, DP)),
        "pool_scale": (pool_scale, m_pool_scale, v_pool_scale),
        "attn_out_g": (ag, m_attn_out_g.reshape(1, DA), v_attn_out_g.reshape(1, DA)),
        "q_norm_g": (q_norm_g, m_q_norm_g, v_q_norm_g), "k_norm_g": (k_norm_g, m_k_norm_g, v_k_norm_g),
        "pool_w": (pool_w.reshape(DP, LANES), m_pool_w.reshape(DP, LANES), v_pool_w.reshape(DP, LANES)),
        "conv_w": (conv_w[0], m_conv_w[0], v_conv_w[0]),
    }
    upd, dmod_all, loss = _small_update(gathered, gathered_pw, gathered_cw, specs, small, off, "small_update")
    g_ada_w = _ada_bwd(c_all, lax.dynamic_slice_in_dim(dmod_all, me * n_ada, n_ada, axis=1), "ada_bwd")
    r_ada = [g_ada_w] + [a[None] for a in _adamw(ada_w[0], m_ada_w[0], v_ada_w[0], g_ada_w[0], "adamw_ada_w")]

    shapes = {"ada_b": ada_b.shape, "norm1_g": norm1_g.shape, "pool_w": pool_w.shape, "pool_b": pool_b.shape,
              "pool_scale": pool_scale.shape, "q_norm_g": q_norm_g.shape, "k_norm_g": k_norm_g.shape,
              "attn_out_g": attn_out_g.shape, "norm2_g": norm2_g.shape, "conv_w": conv_w.shape, "conv_b": conv_b.shape}
    res = {nme: [a.reshape(shapes[nme]) for a in upd[nme]] for nme in shapes}
    res.update(ada_w=r_ada, w_in=r_in, w_out=r_out, w_up=r_up, w_down=r_down)
    names = ["ada_w", "ada_b", "norm1_g", "w_in", "pool_w", "pool_b", "pool_scale", "q_norm_g", "k_norm_g",
             "attn_out_g", "w_out", "norm2_g", "w_up", "conv_w", "conv_b", "w_down"]
    outs = [loss, grad_x[None]]
    for q in range(4):
        outs += [res[nme][q] for nme in names]
    return tuple(outs)
```

```python
import functools
import math

import numpy as np
import jax
import jax.numpy as jnp
from jax import lax
from jax.experimental import pallas as pl
from jax.experimental.pallas import tpu as pltpu

F32, BF16 = jnp.float32, jnp.bfloat16
D = 1024
DP = 512
DA = 512
HD = 64
DIN = DP + 3 * DA
DFF = 2816
POOL_WINDOWS = (2, 4, 8, 16)
HALO = 16
EPS = 1e-6
LANES = 128
SUBLANES = 8
NDEV = 8
VMEM_LIMIT = 56 * 1024 * 1024
MESH = pl.DeviceIdType.MESH

ADAM_LR, ADAM_B1, ADAM_B2, ADAM_EPS, ADAM_WD, ADAM_STEP = 0.001, 0.9, 0.999, 1e-08, 0.01, 10

NN = (((1,), (0,)), ((), ()))
NT = (((1,), (1,)), ((), ()))
TN = (((0,), (0,)), ((), ()))


def _params(*sem):
    return pltpu.CompilerParams(dimension_semantics=sem, vmem_limit_bytes=VMEM_LIMIT)


def _full(shape):
    nd = len(shape)
    return pl.BlockSpec(shape, lambda *_: (0,) * nd)


def _dot(a, b, dn=NN):
    return lax.dot_general(a, b, dn, preferred_element_type=F32)


def _split_dot(a, b, dn=NN):
    hi = a.astype(BF16)
    lo = (a - hi.astype(F32)).astype(BF16)
    return _dot(hi, b, dn) + _dot(lo, b, dn)


def _colsum8(v):
    r, n = v.shape
    return v.reshape(r // SUBLANES, SUBLANES, n).sum(axis=0)


def _block_diag_ones(n, blk):
    i = np.arange(n) // blk
    return jnp.asarray((i[:, None] == i[None, :]).astype(np.float32), BF16)


def _matmul(a, b, *, mode, out_dtype, tm, tn, tk, name, n_outer=False, side=None):
    if mode == "tn":
        K, M = a.shape
        N = b.shape[1]
    elif mode == "nt":
        M, K = a.shape
        N = b.shape[0]
    else:
        M, K = a.shape
        N = b.shape[1]
    tm, tn, tk = min(tm, M), min(tn, N), min(tk, K)
    assert M % tm == 0 and N % tn == 0 and K % tk == 0, (name, M, N, K, tm, tn, tk)
    nk = K // tk
    dn = {"nn": NN, "nt": NT, "tn": TN}[mode]

    def body(a_ref, b_ref, o_ref, *acc):
        if nk == 1:
            o_ref[...] = _dot(a_ref[...], b_ref[...], dn).astype(o_ref.dtype)
            return
        acc_ref, = acc
        k = pl.program_id(2)

        @pl.when(k == 0)
        def _():
            acc_ref[...] = jnp.zeros_like(acc_ref)

        acc_ref[...] += _dot(a_ref[...], b_ref[...], dn)

        @pl.when(k == nk - 1)
        def _():
            o_ref[...] = acc_ref[...].astype(o_ref.dtype)

    if n_outer:
        gi = lambda g: (g[1], g[0], g[2])
        grid = (N // tn, M // tm, nk)
    else:
        gi = lambda g: g
        grid = (M // tm, N // tn, nk)

    def amap(*g):
        i, j, k = gi(g)
        return (k, i) if mode == "tn" else (i, k)

    def bmap(*g):
        i, j, k = gi(g)
        return (j, k) if mode == "nt" else (k, j)

    def omap(*g):
        i, j, k = gi(g)
        return (i, j)

    a_blk = (tk, tm) if mode == "tn" else (tm, tk)
    b_blk = (tn, tk) if mode == "nt" else (tk, tn)
    acc_scratch = [] if nk == 1 else [pltpu.VMEM((tm, tn), F32)]
    if side is None:
        return pl.pallas_call(
            body, name=name, grid=grid,
            in_specs=[pl.BlockSpec(a_blk, amap), pl.BlockSpec(b_blk, bmap)],
            out_specs=pl.BlockSpec((tm, tn), omap),
            out_shape=jax.ShapeDtypeStruct((M, N), out_dtype),
            scratch_shapes=acc_scratch,
            compiler_params=_params("parallel", "parallel", "arbitrary"),
        )(a, b)

    ne = len(side.arrs)
    steps = grid[0] * grid[1] * grid[2]

    nsem = len(side.scratch)

    def with_side(*refs):
        e_in, e_out = refs[2:2 + ne], refs[3 + ne:3 + 2 * ne]
        sems = refs[len(refs) - nsem:]
        step = (pl.program_id(0) * grid[1] + pl.program_id(1)) * grid[2] + pl.program_id(2)

        @pl.when(step == 0)
        def _():
            side.start(e_in, e_out, *sems)

        body(refs[0], refs[1], refs[2 + ne], *refs[3 + 2 * ne:len(refs) - nsem])

        @pl.when(step == steps - 1)
        def _():
            side.finish(e_in, e_out, *sems)

    any_spec = pl.BlockSpec(memory_space=pl.ANY)
    res = pl.pallas_call(
        with_side, name=name, grid=grid,
        in_specs=[pl.BlockSpec(a_blk, amap), pl.BlockSpec(b_blk, bmap)] + [any_spec] * ne,
        out_specs=[pl.BlockSpec((tm, tn), omap)] + [any_spec] * ne,
        out_shape=[jax.ShapeDtypeStruct((M, N), out_dtype)] + side.out_shapes,
        scratch_shapes=acc_scratch + side.scratch,
        compiler_params=_params("arbitrary", "arbitrary", "arbitrary"),
    )(a, b, *side.arrs)
    return res[0], list(res[1:])


def _slot(swap, px, py, pc):
    return 4 * py + 2 * px + pc if swap else 4 * px + 2 * py + pc


class _Gather:
    def __init__(self, ins, outs, send, recv, loc, swaps):
        self.ins, self.outs, self.send, self.recv, self.loc, self.swaps = ins, outs, send, recv, loc, swaps
        x, y, c = lax.axis_index("x"), lax.axis_index("y"), lax.axis_index("c")
        self.me, self.sib = (x, y, c), (x, y, 1 - c)
        self.chips = [(1 - x, y), (x, 1 - y), (1 - x, 1 - y)]
        self.n = len(ins)

    @staticmethod
    def scratch(n):
        return [pltpu.SemaphoreType.DMA((7 * n,)), pltpu.SemaphoreType.DMA((7 * n,)), pltpu.SemaphoreType.DMA((n,))]

    def copy(self, a, k, blk, to, src=None):
        rows = self.outs[a].at[_slot(self.swaps[a], *blk)]
        return pltpu.make_async_remote_copy(
            src_ref=rows if src is None else src, dst_ref=rows,
            send_sem=self.send.at[7 * a + k], recv_sem=self.recv.at[7 * a + k], device_id=to, device_id_type=MESH)

    def mine(self, a):
        return pltpu.make_async_copy(self.ins[a], self.outs[a].at[_slot(self.swaps[a], *self.me)], self.loc.at[a])

    def first(self, a):
        c = self.me[2]
        return [self.copy(a, 0, self.me, self.sib, src=self.ins[a])] + [
            self.copy(a, 1 + j, self.me, (*chip, c), src=self.ins[a]) for j, chip in enumerate(self.chips)]

    def forwards(self, a):
        c = self.me[2]
        return [self.copy(a, 4 + j, (*chip, c), self.sib) for j, chip in enumerate(self.chips)]

    def start(self):
        for a in range(self.n):
            self.mine(a).start()
        for a in range(self.n):
            for cp in self.first(a):
                cp.start()

    def forward(self):
        c = self.me[2]
        for a in range(self.n):
            fwd = self.forwards(a)
            for j, chip in enumerate(self.chips):
                self.copy(a, 1 + j, (*chip, c), self.me).wait_recv()
                fwd[j].start()

    def finish(self):
        c = self.me[2]
        for a in range(self.n):
            self.copy(a, 0, self.sib, self.me).wait_recv()
            for j, chip in enumerate(self.chips):
                self.copy(a, 4 + j, (*chip, 1 - c), self.me).wait_recv()
        for a in range(self.n):
            for cp in self.first(a) + self.forwards(a):
                cp.wait_send()
            self.mine(a).wait()


def _all_gather(arrs, swaps, name):
    n = len(arrs)

    def body(*refs):
        g = _Gather(refs[:n], refs[n:2 * n], *refs[2 * n:], swaps)
        g.start()
        g.forward()
        g.finish()

    any_spec = pl.BlockSpec(memory_space=pl.ANY)
    return pl.pallas_call(
        body, name=name,
        in_specs=[any_spec] * n, out_specs=[any_spec] * n,
        out_shape=[jax.ShapeDtypeStruct((NDEV,) + a.shape, a.dtype) for a in arrs],
        scratch_shapes=_Gather.scratch(n),
    )(*arrs)


def _pair_copies(ins, gots, send, recv, swaps):
    x, y, c = lax.axis_index("x"), lax.axis_index("y"), lax.axis_index("c")
    return [pltpu.make_async_remote_copy(
        src_ref=ins[a].at[_slot(swaps[a], k // 2, k % 2, 1 - c)], dst_ref=gots[a].at[k],
        send_sem=send.at[4 * a + k], recv_sem=recv.at[4 * a + k], device_id=(x, y, 1 - c), device_id_type=MESH)
        for a in range(len(ins)) for k in range(4)]


def _pair_exchange(arrs, swaps, name):
    n = len(arrs)

    def body(*refs):
        rems = _pair_copies(refs[:n], refs[n:2 * n], *refs[2 * n:], swaps)
        for rc in rems:
            rc.start()
        for rc in rems:
            rc.wait_recv()
        for rc in rems:
            rc.wait_send()

    any_spec = pl.BlockSpec(memory_space=pl.ANY)
    return pl.pallas_call(
        body, name=name,
        in_specs=[any_spec] * n, out_specs=[any_spec] * n,
        out_shape=[jax.ShapeDtypeStruct((4,) + a.shape[1:], a.dtype) for a in arrs],
        scratch_shapes=[pltpu.SemaphoreType.DMA((4 * n,)), pltpu.SemaphoreType.DMA((4 * n,))],
    )(*arrs)


def _chip_copies(ins, outs, send, recv):
    x, y, c = lax.axis_index("x"), lax.axis_index("y"), lax.axis_index("c")
    chips = [(1 - x, y), (x, 1 - y), (1 - x, 1 - y)]
    return [pltpu.make_async_remote_copy(
        src_ref=ins[a].at[2 * px + py], dst_ref=outs[a].at[j], send_sem=send.at[3 * a + j], recv_sem=recv.at[3 * a + j],
        device_id=(px, py, c), device_id_type=MESH) for a in range(len(ins)) for j, (px, py) in enumerate(chips)]


def _chip_exchange(arrs, name):
    n = len(arrs)

    def body(*refs):
        rems = _chip_copies(refs[:n], refs[n:2 * n], *refs[2 * n:])
        for rc in rems:
            rc.start()
        for rc in rems:
            rc.wait_recv()
        for rc in rems:
            rc.wait_send()

    any_spec = pl.BlockSpec(memory_space=pl.ANY)
    return pl.pallas_call(
        body, name=name,
        in_specs=[any_spec] * n, out_specs=[any_spec] * n,
        out_shape=[jax.ShapeDtypeStruct((3,) + a.shape[1:], a.dtype) for a in arrs],
        scratch_shapes=[pltpu.SemaphoreType.DMA((3 * n,)), pltpu.SemaphoreType.DMA((3 * n,))],
    )(*arrs)


class _Side:
    def __init__(self, arrs, out_shapes, scratch, start, finish, mid=None):
        self.arrs, self.out_shapes, self.scratch = list(arrs), list(out_shapes), list(scratch)
        self.start, self.finish, self.mid = start, finish, mid


def _copies_side(arrs, out_shapes, n_copies, make):
    def start(ins, outs, *sems):
        for cp in make(ins, outs, *sems):
            cp.start()

    def finish(ins, outs, *sems):
        cps = make(ins, outs, *sems)
        for cp in cps:
            cp.wait_recv()
        for cp in cps:
            cp.wait_send()

    return _Side(arrs, out_shapes, [pltpu.SemaphoreType.DMA((n_copies,)), pltpu.SemaphoreType.DMA((n_copies,))], start, finish)


def _pair_side(arrs, swaps):
    return _copies_side(arrs, [jax.ShapeDtypeStruct((4,) + a.shape[1:], a.dtype) for a in arrs], 4 * len(arrs),
                        functools.partial(_pair_copies, swaps=swaps))


def _chip_side(arrs):
    return _copies_side(arrs, [jax.ShapeDtypeStruct((3,) + a.shape[1:], a.dtype) for a in arrs], 3 * len(arrs), _chip_copies)


def _gather_side(arrs, swaps):
    return _Side(arrs, [jax.ShapeDtypeStruct((NDEV,) + a.shape, a.dtype) for a in arrs], _Gather.scratch(len(arrs)),
                 start=lambda ins, outs, *sems: _Gather(ins, outs, *sems, swaps).start(),
                 mid=lambda ins, outs, *sems: _Gather(ins, outs, *sems, swaps).forward(),
                 finish=lambda ins, outs, *sems: _Gather(ins, outs, *sems, swaps).finish())


def _row_call(body, side, *, name, steps, in_specs, out_specs, out_shape, ins):
    if side is None:
        res = pl.pallas_call(body, name=name, grid=(steps,), in_specs=in_specs, out_specs=out_specs, out_shape=out_shape,
                             compiler_params=_params("arbitrary"))(*ins)
        return list(res), []
    n_in, n_out, ne = len(in_specs), len(out_specs), len(side.arrs)

    def wrapped(*refs):
        e_in = refs[n_in:n_in + ne]
        e_out = refs[n_in + ne + n_out:n_in + 2 * ne + n_out]
        sems = refs[n_in + 2 * ne + n_out:]
        i = pl.program_id(0)

        @pl.when(i == 0)
        def _():
            side.start(e_in, e_out, *sems)

        if side.mid is not None:
            @pl.when(i == steps // 2)
            def _():
                side.mid(e_in, e_out, *sems)

        body(*refs[:n_in], *refs[n_in + ne:n_in + ne + n_out])

        @pl.when(i == steps - 1)
        def _():
            side.finish(e_in, e_out, *sems)

    any_spec = pl.BlockSpec(memory_space=pl.ANY)
    res = pl.pallas_call(
        wrapped, name=name, grid=(steps,), in_specs=list(in_specs) + [any_spec] * ne,
        out_specs=list(out_specs) + [any_spec] * ne, out_shape=list(out_shape) + side.out_shapes,
        scratch_shapes=side.scratch,
        compiler_params=_params("arbitrary"))(*ins, *side.arrs)
    return list(res[:n_out]), list(res[n_out:])


def _pair_sum(grads, got, swap, core, name):
    _, r, c = got.shape
    tr = r if r <= 352 else r // 2

    def own_map(k, i, core_ref):
        return (_slot(swap, k // 2, k % 2, core_ref[0]), i, 0)

    def body(core_ref, a_ref, b_ref, o_ref):
        o_ref[...] = a_ref[...] + b_ref[...]

    spec = pl.BlockSpec((None, tr, c), lambda k, i, core_ref: (k, i, 0))
    return pl.pallas_call(
        body, name=name,
        grid_spec=pltpu.PrefetchScalarGridSpec(
            num_scalar_prefetch=1, grid=(4, r // tr),
            in_specs=[pl.BlockSpec((None, tr, c), own_map), spec], out_specs=spec),
        out_shape=jax.ShapeDtypeStruct(got.shape, got.dtype), compiler_params=_params("parallel", "parallel"),
    )(core, grads, got)


def _adamw_math(w, g, m, v):
    m = ADAM_B1 * m + (1.0 - ADAM_B1) * g
    v = ADAM_B2 * v + (1.0 - ADAM_B2) * (g * g)
    m_hat = m / (1.0 - ADAM_B1 ** ADAM_STEP)
    v_hat = v / (1.0 - ADAM_B2 ** ADAM_STEP)
    delta = -ADAM_LR * (m_hat / (jnp.sqrt(v_hat) + ADAM_EPS) + ADAM_WD * w)
    return delta, m, v


def _adamw_tile(r):
    for cand in (256, 352, 128):
        if r % cand == 0:
            return cand
    return r


def _adamw(w, m, v, g, name):
    r, c = w.shape
    tr = _adamw_tile(r)
    spec = pl.BlockSpec((tr, c), lambda i: (i, 0))

    def body(w_ref, m_ref, v_ref, g_ref, d_ref, nm_ref, nv_ref):
        d_ref[...], nm_ref[...], nv_ref[...] = _adamw_math(w_ref[...], g_ref[...], m_ref[...], v_ref[...])

    out = jax.ShapeDtypeStruct((r, c), F32)
    return pl.pallas_call(
        body, name=name, grid=(r // tr,), in_specs=[spec] * 4, out_specs=[spec] * 3, out_shape=[out] * 3,
        compiler_params=_params("parallel"),
    )(w, m, v, g)


def _adamw_reduce(w, m, v, sums, recv, chip, name):
    r, c = w.shape
    tr = _adamw_tile(r)
    spec = pl.BlockSpec((tr, c), lambda i, chip_ref: (i, 0))

    def body(chip_ref, w_ref, m_ref, v_ref, s_ref, p_ref, g_ref, d_ref, nm_ref, nv_ref):
        g = ((s_ref[...] + p_ref[0]) + p_ref[1]) + p_ref[2]
        g_ref[...] = g
        d_ref[...], nm_ref[...], nv_ref[...] = _adamw_math(w_ref[...], g, m_ref[...], v_ref[...])

    out = jax.ShapeDtypeStruct((r, c), F32)
    return pl.pallas_call(
        body, name=name,
        grid_spec=pltpu.PrefetchScalarGridSpec(
            num_scalar_prefetch=1, grid=(r // tr,),
            in_specs=[spec, spec, spec, pl.BlockSpec((None, tr, c), lambda i, chip_ref: (chip_ref[0], i, 0)),
                      pl.BlockSpec((3, tr, c), lambda i, chip_ref: (0, i, 0))],
            out_specs=[spec] * 4),
        out_shape=[out] * 4, compiler_params=_params("parallel"),
    )(chip, w, m, v, sums, recv)


def _vec(n):
    return pl.BlockSpec((1, n), lambda *_: (0, 0))


def _ln_mod(x, g, scale, shift, *, ts, name, side=None):
    s = x.shape[0]
    row = pl.BlockSpec((ts, D), lambda i: (i, 0))

    def body(x_ref, g_ref, sc_ref, sh_ref, h_ref):
        xv = x_ref[...]
        r = lax.rsqrt(jnp.mean(xv * xv, axis=-1, keepdims=True) + EPS)
        h = (xv * r) * g_ref[...]
        h_ref[...] = (h * (1.0 + sc_ref[...]) + sh_ref[...]).astype(BF16)

    (h,), extra = _row_call(body, side, name=name, steps=s // ts, in_specs=[row, _vec(D), _vec(D), _vec(D)],
                            out_specs=[row], out_shape=[jax.ShapeDtypeStruct((s, D), BF16)], ins=(x, g, scale, shift))
    return h, extra


def _proj_res_ln_mod(mix, w, x, gate, g, scale, shift, *, tm, name):
    s = x.shape[0]
    row = pl.BlockSpec((tm, D), lambda i: (i, 0))

    def body(m_ref, w_ref, x_ref, gt_ref, g_ref, sc_ref, sh_ref, a_ref, x1_ref, h_ref):
        att = _dot(m_ref[...], w_ref[...])
        a_ref[...] = att.astype(BF16)
        x1 = x_ref[...] + gt_ref[...] * att
        x1_ref[...] = x1
        r = lax.rsqrt(jnp.mean(x1 * x1, axis=-1, keepdims=True) + EPS)
        h = (x1 * r) * g_ref[...]
        h_ref[...] = (h * (1.0 + sc_ref[...]) + sh_ref[...]).astype(BF16)

    return pl.pallas_call(
        body, name=name, grid=(s // tm,), in_specs=[row, _full(w.shape), row] + [_vec(D)] * 4, out_specs=[row, row, row],
        out_shape=[jax.ShapeDtypeStruct((s, D), BF16), jax.ShapeDtypeStruct((s, D), F32), jax.ShapeDtypeStruct((s, D), BF16)],
        compiler_params=_params("parallel"),
    )(mix, w, x, gate, g, scale, shift)


def _proj_loss_head(act, w, x1, tgt, gate2, *, tm, name):
    s = x1.shape[0]
    n = s // tm
    row = pl.BlockSpec((tm, D), lambda i: (i, 0))
    acc8 = pl.BlockSpec((SUBLANES, D), lambda i: (0, 0))

    def body(a_ref, w_ref, x_ref, t_ref, g_ref, dy_ref, df_ref, dg_ref, loss_ref, lacc):
        i = pl.program_id(0)

        @pl.when(i == 0)
        def _():
            lacc[...] = jnp.zeros_like(lacc)
            dg_ref[...] = jnp.zeros_like(dg_ref)

        f = _dot(a_ref[...], w_ref[...])
        diff = x_ref[...] + g_ref[...] * f - t_ref[...]
        lacc[...] += _colsum8(diff * diff)
        dy = diff * (1.0 / D)
        dy_ref[...] = dy
        df_ref[...] = (dy * g_ref[...]).astype(BF16)
        dg_ref[...] += _colsum8(dy * f)

        @pl.when(i == n - 1)
        def _():
            loss_ref[...] = jnp.full((SUBLANES, LANES), (0.5 / D) * jnp.sum(lacc[...]), F32)

    return pl.pallas_call(
        body, name=name, grid=(n,),
        in_specs=[pl.BlockSpec((tm, act.shape[1]), lambda i: (i, 0)), _full(w.shape), row, row, _vec(D)],
        out_specs=[row, row, acc8, _full((SUBLANES, LANES))],
        out_shape=[jax.ShapeDtypeStruct((s, D), F32), jax.ShapeDtypeStruct((s, D), BF16),
                   jax.ShapeDtypeStruct((SUBLANES, D), F32), jax.ShapeDtypeStruct((SUBLANES, LANES), F32)],
        scratch_shapes=[pltpu.VMEM((SUBLANES, D), F32)], compiler_params=_params("arbitrary"),
    )(act, w, x1, tgt, gate2)


def _group_rsqrt(t, bd):
    return lax.rsqrt(_split_dot(t * t, bd) * (1.0 / HD) + EPS)


def _in_proj_qk_norm(h, w, qg, kg, bd, *, tm, name):
    s = h.shape[0]

    def body(h_ref, w_ref, qg_ref, kg_ref, bd_ref, p_ref, o_ref):
        bdv = bd_ref[...]
        hv = h_ref[...]
        p_ref[:, 0:DP] = _dot(hv, w_ref[0:DP, :], NT)
        q = _dot(hv, w_ref[DP:DP + DA, :], NT)
        p_ref[:, DP:DP + DA] = q
        o_ref[:, 0:DA] = (q * _group_rsqrt(q, bdv) * qg_ref[...]).astype(BF16)
        k = _dot(hv, w_ref[DP + DA:DP + 2 * DA, :], NT)
        p_ref[:, DP + DA:DP + 2 * DA] = k
        o_ref[:, DA:2 * DA] = (k * _group_rsqrt(k, bdv) * kg_ref[...]).astype(BF16)
        v = _dot(hv, w_ref[DP + 2 * DA:, :], NT)
        p_ref[:, DP + 2 * DA:] = v
        o_ref[:, 2 * DA:] = v.astype(BF16)

    return pl.pallas_call(
        body, name=name, grid=(s // tm,),
        in_specs=[pl.BlockSpec((tm, D), lambda i: (i, 0)), _full(w.shape), _vec(DA), _vec(DA), _full((DA, DA))],
        out_specs=[pl.BlockSpec((tm, DIN), lambda i: (i, 0)), pl.BlockSpec((tm, 3 * DA), lambda i: (i, 0))],
        out_shape=[jax.ShapeDtypeStruct((s, DIN), F32), jax.ShapeDtypeStruct((s, 3 * DA), BF16)],
        compiler_params=_params("parallel"),
    )(h, w, qg, kg, bd)


EXP_UNDERFLOW = -120.0


def _log_terms(z):
    neg_abs = lax.bitcast_convert_type(lax.bitcast_convert_type(z, jnp.uint32) | jnp.uint32(0x80000000), F32)
    b = jnp.minimum(z, 0.0) - jnp.log(1.0 + jnp.exp(neg_abs))
    return b, b - z


def _head_masks(rows):
    lane = lax.broadcasted_iota(jnp.int32, (rows, LANES), 1)
    return [lane < HD, lane >= HD]


def _attn_fwd(qkv, gather, swaps, *, tq, tk, hp, name):
    s = qkv.shape[0]
    nrep = tk // LANES
    ndiag = tq // tk
    ng = len(gather)
    nh, wl = 2 * hp, LANES * hp
    ngrp, nq = DA // wl, s // tq
    lanes = [slice(LANES * pp, LANES * (pp + 1)) for pp in range(hp)]

    def body(*refs):
        q_ref, k_ref, v_ref = refs[:3]
        g_in = refs[3:3 + ng]
        o_ref, tot_ref, first_ref = refs[3 + ng:6 + ng]
        g_out = refs[6 + ng:6 + 2 * ng]
        oacc, rc = refs[6 + 2 * ng:8 + 2 * ng]
        g_sems = refs[8 + 2 * ng:]
        i = pl.program_id(1)
        step_id = pl.program_id(0) * nq + i

        @pl.when(step_id == 0)
        def _():
            _Gather(g_in, g_out, *g_sems, swaps).start()

        @pl.when(step_id == (ngrp * nq * 3) // 4)
        def _():
            _Gather(g_in, g_out, *g_sems, swaps).forward()

        heads = _head_masks(tq)
        qs = [jnp.where(heads[a % 2], q_ref[:, lanes[a // 2]] * 0.125, 0.0).astype(BF16) for a in range(nh)]
        dif = lax.broadcasted_iota(jnp.int32, (tq, tk), 0) - lax.broadcasted_iota(jnp.int32, (tq, tk), 1)
        kr = lax.broadcasted_iota(jnp.int32, (tk, tk), 0)
        kc = lax.broadcasted_iota(jnp.int32, (tk, tk), 1)
        later =jnp.where(kr > kc, 1.0, 0.0).astype(BF16)
        oacc[...] = jnp.zeros_like(oacc)
        rc[...] = jnp.zeros_like(rc)

        def tile(kb, thr):
            rows = pl.ds(pl.multiple_of(kb * tk, tk), tk)
            ks = [k_ref[rows, ln] for ln in lanes]
            vs = [v_ref[rows, ln] for ln in lanes]
            qr = slice(0 if thr is None else thr, tq)
            rcv = [rc[a, qr, :] for a in range(nh)]
            zs = [_dot(qs[a][qr], ks[a // 2], NT) for a in range(nh)]
            bs, mbs = [], []
            for a in range(nh):
                b, m = _log_terms(zs[a])
                if thr is not None:
                    m = jnp.where(dif[qr] > thr, m, 0.0)
                bs.append(b)
                mbs.append(m.astype(BF16))
            rl = [_dot(mbs[a], later) for a in range(nh)]
            for a in range(nh):
                p = jnp.exp(bs[a] + (rl[a] + jnp.tile(rcv[a], (1, nrep))))
                if thr is not None:
                    p = jnp.where(dif[qr] > thr, p, 0.0)
                oacc[a, qr, :] += _dot(p.astype(BF16), vs[a // 2])
                rc[a, qr, :] = rcv[a] + (rl[a][:, 0:1] + mbs[a][:, 0:1].astype(F32))

        for d in reversed(range(ndiag)):
            tile(i * ndiag + d, d * tk)

        def live():
            top = rc[0]
            for a in range(1, nh):
                top = jnp.maximum(top, rc[a])
            return jnp.max(top) > EXP_UNDERFLOW

        def step(carry):
            kb, _ = carry
            tile(kb, None)
            return kb - 1, live()

        kb_end, _ = lax.while_loop(lambda cr: jnp.logical_and(cr[0] >= 0, cr[1]), step, (i * ndiag - 1, live()))
        first_ref[pl.program_id(0), i] = (kb_end + 1).astype(F32)
        for pp, ln in enumerate(lanes):
            o_ref[:, ln] = jnp.where(heads[0], oacc[2 * pp], oacc[2 * pp + 1])
            tot_ref[:, ln] = jnp.where(heads[0], rc[2 * pp], rc[2 * pp + 1])

        @pl.when(step_id == ngrp * nq - 1)
        def _():
            _Gather(g_in, g_out, *g_sems, swaps).finish()

    qspec = pl.BlockSpec((tq, wl), lambda p, i: (i, p))
    any_spec = pl.BlockSpec(memory_space=pl.ANY)
    res = pl.pallas_call(
        body, name=name, grid=(ngrp, nq),
        in_specs=[qspec,
                  pl.BlockSpec((s, wl), lambda p, i: (0, ngrp + p)),
                  pl.BlockSpec((s, wl), lambda p, i: (0, 2 * ngrp + p))] + [any_spec] * ng,
        out_specs=[qspec, qspec, pl.BlockSpec(memory_space=pltpu.SMEM)] + [any_spec] * ng,
        out_shape=[jax.ShapeDtypeStruct((s, DA), F32), jax.ShapeDtypeStruct((s, DA), F32),
                   jax.ShapeDtypeStruct((ngrp, nq), F32)]
        + [jax.ShapeDtypeStruct((NDEV,) + a.shape, a.dtype) for a in gather],
        scratch_shapes=[pltpu.VMEM((nh, tq, LANES), F32), pltpu.VMEM((nh, tq, LANES), F32)] + _Gather.scratch(ng),
        compiler_params=_params("arbitrary", "arbitrary"),
    )(qkv, qkv, qkv, *gather)
    return res[0], res[1], res[2], res[3:]


def _attn_bwd(qkv, do, tot, first, exchange, *, tq, tk, hp, name):
    s = qkv.shape[0]
    nrep = tk // LANES
    ndiag = tq // tk
    ne = len(exchange)
    nh, wl = 2 * hp, LANES * hp
    ngrp, nq = DA // wl, s // tq
    lanes = [slice(LANES * pp, LANES * (pp + 1)) for pp in range(hp)]

    def body(*refs):
        q_ref, k_ref, v_ref, do_ref, tot_ref, first_ref = refs[:6]
        e_in = refs[6:6 + ne]
        dq_ref, dk_ref, dv_ref = refs[6 + ne:9 + ne]
        e_out = refs[9 + ne:9 + 2 * ne]
        dqacc, rem, gc = refs[9 + 2 * ne:12 + 2 * ne]
        e_sems = refs[12 + 2 * ne:]
        i = pl.program_id(1)
        step_id = pl.program_id(0) * nq + i

        @pl.when(step_id == 0)
        def _():
            for cp in _chip_copies(e_in, e_out, *e_sems):
                cp.start()

        @pl.when(i == 0)
        def _():
            dk_ref[...] = jnp.zeros_like(dk_ref)
            dv_ref[...] = jnp.zeros_like(dv_ref)

        heads = _head_masks(tq)
        qs = [jnp.where(heads[a % 2], q_ref[:, lanes[a // 2]] * 0.125, 0.0).astype(BF16) for a in range(nh)]
        dob = [jnp.where(heads[a % 2], do_ref[:, lanes[a // 2]], 0.0).astype(BF16) for a in range(nh)]
        dif = lax.broadcasted_iota(jnp.int32, (tq, tk), 0) - lax.broadcasted_iota(jnp.int32, (tq, tk), 1)
        kr = lax.broadcasted_iota(jnp.int32, (tk, tk), 0)
        kc = lax.broadcasted_iota(jnp.int32, (tk, tk), 1)
        up_incl = jnp.where(kr <= kc, 1.0, 0.0).astype(BF16)
        up_strict = jnp.where(kr < kc, 1.0, 0.0).astype(BF16)
        dqacc[...] = jnp.zeros_like(dqacc)
        gc[...] = jnp.zeros_like(gc)
        for pp, ln in enumerate(lanes):
            totv = tot_ref[:, ln]
            swapped = pltpu.roll(totv, HD, axis=1)
            rem[2 * pp] = jnp.where(heads[0], totv, swapped)
            rem[2 * pp + 1] = jnp.where(heads[1], totv, swapped)

        def tile(kb, thr):
            rows = pl.ds(pl.multiple_of(kb * tk, tk), tk)
            ks = [k_ref[rows, ln] for ln in lanes]
            vs = [v_ref[rows, ln] for ln in lanes]
            qr = slice(0 if thr is None else thr, tq)
            remv = [rem[a, qr, :] for a in range(nh)]
            gcv = [gc[a, qr, :] for a in range(nh)]
            zs = [_dot(qs[a][qr], ks[a // 2], NT) for a in range(nh)]
            das = [_dot(dob[a][qr], vs[a // 2], NT) for a in range(nh)]
            bs, mbs = [], []
            for a in range(nh):
                b, m = _log_terms(zs[a])
                if thr is not None:
                    m = jnp.where(dif[qr] > thr, m, 0.0)
                bs.append(b)
                mbs.append(m.astype(BF16))
            pl_ = [_dot(mbs[a], up_incl) for a in range(nh)]
            ps, gs, gbs = [], [], []
            for a in range(nh):
                p = jnp.exp(bs[a] + (jnp.tile(remv[a], (1, nrep)) - pl_[a]))
                if thr is not None:
                    p = jnp.where(dif[qr] > thr, p, 0.0)
                g = p * das[a]
                ps.append(p.astype(BF16))
                gs.append(g)
                gbs.append(g.astype(BF16))
            cl = [_dot(gbs[a], up_strict) for a in range(nh)]
            dk_add = [jnp.zeros((tk, LANES), F32) for _ in range(hp)]
            dv_add = [jnp.zeros((tk, LANES), F32) for _ in range(hp)]
            for a in range(nh):
                dz = gs[a] - jnp.exp(bs[a]) * (gs[a] + (jnp.tile(gcv[a], (1, nrep)) + cl[a]))
                if thr is not None:
                    dz = jnp.where(dif[qr] > thr, dz, 0.0)
                dzb = dz.astype(BF16)
                dqacc[a, qr, :] += _dot(dzb, ks[a // 2])
                dk_add[a // 2] += _dot(dzb, qs[a][qr], TN)
                dv_add[a // 2] += _dot(ps[a], dob[a][qr], TN)
                rem[a, qr, :] = remv[a] - pl_[a][:, tk - 1:tk]
                gc[a, qr, :] = gcv[a] + (cl[a][:, tk - 1:tk] + gbs[a][:, tk - 1:tk].astype(F32))
            for pp, ln in enumerate(lanes):
                dk_ref[rows, ln] += dk_add[pp]
                dv_ref[rows, ln] += dv_add[pp]

        def step(kb, carry):
            tile(kb, None)
            return carry

        first_kb = first_ref[(pl.program_id(0) * first.shape[0]) // ngrp, i].astype(jnp.int32)
        lax.fori_loop(first_kb, i * ndiag, step, 0)
        for d in range(ndiag):
            tile(i * ndiag + d, d * tk)
        for pp, ln in enumerate(lanes):
            dq_ref[:, ln] = jnp.where(heads[0], dqacc[2 * pp], dqacc[2 * pp + 1]) * 0.125

        @pl.when(step_id == ngrp * nq - 1)
        def _():
            cps = _chip_copies(e_in, e_out, *e_sems)
            for cp in cps:
                cp.wait_recv()
            for cp in cps:
                cp.wait_send()

    qspec = pl.BlockSpec((tq, wl), lambda p, i: (i, p))
    full = pl.BlockSpec((s, wl), lambda p, i: (0, p))
    any_spec = pl.BlockSpec(memory_space=pl.ANY)
    out = jax.ShapeDtypeStruct((s, DA), F32)
    res = pl.pallas_call(
        body, name=name, grid=(ngrp, nq),
        in_specs=[qspec, pl.BlockSpec((s, wl), lambda p, i: (0, ngrp + p), pipeline_mode=pl.Buffered(1)),
                  pl.BlockSpec((s, wl), lambda p, i: (0, 2 * ngrp + p), pipeline_mode=pl.Buffered(1)), qspec, qspec,
                  pl.BlockSpec(memory_space=pltpu.SMEM)] + [any_spec] * ne,
        out_specs=[qspec, full, full] + [any_spec] * ne,
        out_shape=[out, out, out] + [jax.ShapeDtypeStruct((3,) + a.shape[1:], a.dtype) for a in exchange],
        scratch_shapes=[pltpu.VMEM((nh, tq, LANES), F32)] * 3
        + [pltpu.SemaphoreType.DMA((3 * ne,)), pltpu.SemaphoreType.DMA((3 * ne,))],
        compiler_params=_params("arbitrary", "arbitrary"),
    )(qkv, qkv, qkv, do, tot, first, *exchange)
    return res[0], res[1], res[2], res[3:]


def _shift_rows(v, k):
    return pltpu.roll(v, k % v.shape[0], axis=0)


def _pooled(u, uh, i, g, w, ts):
    halo = jnp.where(i > 0, uh, 0.0)
    ue = jnp.concatenate([halo, u], axis=0)
    acc, span = ue, 1
    while span < w:
        acc = acc + _shift_rows(acc, span)
        span *= 2
    tpos = i * ts + lax.broadcasted_iota(jnp.int32, (ts, 1), 0)
    cnt = jnp.minimum(tpos + 1, w).astype(F32)
    return acc[HALO:] / cnt - u


def _pool_mix(proj, o, pw, pb, ps, ag, bd, *, ts, name):
    s = proj.shape[0]
    hb = ts // HALO

    def body(u_ref, uh_ref, o_ref, pw_ref, pb_ref, ps_ref, ag_ref, bd_ref, mix_ref):
        i = pl.program_id(0)
        for g, w in enumerate(POOL_WINDOWS):
            cols = slice(g * LANES, (g + 1) * LANES)
            pooled = _pooled(u_ref[:, cols], uh_ref[:, cols], i, g, w, ts)
            yv = (_dot(pooled.astype(BF16), pw_ref[g]) + pb_ref[:, cols]) * ps_ref[:, cols]
            mix_ref[:, cols] = yv.astype(BF16)
        ov = o_ref[...]
        mix_ref[:, DP:] = (ov * _group_rsqrt(ov, bd_ref[...]) * ag_ref[...]).astype(BF16)

    return pl.pallas_call(
        body, name=name, grid=(s // ts,),
        in_specs=[pl.BlockSpec((ts, DP), lambda i: (i, 0)),
                  pl.BlockSpec((HALO, DP), lambda i: (jnp.maximum(i * hb - 1, 0), 0)),
                  pl.BlockSpec((ts, DA), lambda i: (i, 0)),
                  _full((4, LANES, LANES)), _vec(DP), _vec(DP), _vec(DA), _full((DA, DA))],
        out_specs=pl.BlockSpec((ts, D), lambda i: (i, 0)),
        out_shape=jax.ShapeDtypeStruct((s, D), BF16), compiler_params=_params("parallel"),
    )(proj, proj, o, pw, pb, ps, ag, bd)


CF = DFF // 2
MXU_COLS = 256


def _sigmoid(t):
    return 0.5 + 0.5 * jnp.tanh(0.5 * t)


def _sub_chunks(width):
    return [(c0, min(MXU_COLS, width - c0)) for c0 in range(0, width, MXU_COLS)]


def _up_conv_gate(h2, w_up, cw, cb, *, tm, name):
    s = h2.shape[0]
    hb = tm // HALO

    def body(a_ref, ah_ref, w_ref, cw_ref, cb_ref, up_ref, c_ref, act_ref):
        i = pl.program_id(1)
        halo = jnp.where(i > 0, ah_ref[...], jnp.zeros_like(ah_ref))
        ext = jnp.concatenate([halo, a_ref[...]], axis=0)
        for c0, cwid in _sub_chunks(CF):
            conv = []
            for off in (c0, CF + c0):
                cols = slice(off, off + cwid)
                ue = _dot(ext, w_ref[cols, :], NT)
                up_ref[:, cols] = ue[HALO:].astype(BF16)
                y = cw_ref[2:3, cols] * ue + cw_ref[1:2, cols] * _shift_rows(ue, 1) + cw_ref[0:1, cols] * _shift_rows(ue, 2)
                cv = y[HALO:] + cb_ref[:, cols]
                c_ref[:, cols] = cv.astype(BF16)
                conv.append(cv)
            gt, vl = conv
            act_ref[:, c0:c0 + cwid] = (gt * _sigmoid(gt) * vl).astype(BF16)

    return pl.pallas_call(
        body, name=name, grid=(2, s // tm),
        in_specs=[pl.BlockSpec((tm, D), lambda j, i: (i, 0)),
                  pl.BlockSpec((HALO, D), lambda j, i: (jnp.maximum(i * hb - 1, 0), 0)),
                  pl.BlockSpec((2 * CF, D), lambda j, i: (j, 0)),
                  pl.BlockSpec((3, 2 * CF), lambda j, i: (0, j)), pl.BlockSpec((1, 2 * CF), lambda j, i: (0, j))],
        out_specs=[pl.BlockSpec((tm, 2 * CF), lambda j, i: (i, j)), pl.BlockSpec((tm, 2 * CF), lambda j, i: (i, j)),
                   pl.BlockSpec((tm, CF), lambda j, i: (i, j))],
        out_shape=[jax.ShapeDtypeStruct((s, 2 * DFF), BF16), jax.ShapeDtypeStruct((s, 2 * DFF), BF16),
                   jax.ShapeDtypeStruct((s, DFF), BF16)],
        compiler_params=_params("parallel", "parallel"),
    )(h2, h2, w_up, cw, cb)


def _down_bwd_gate(dffn, w_down, conv, *, tm, name):
    s = dffn.shape[0]

    def body(a_ref, w_ref, c_ref, d_ref, db_ref):
        i = pl.program_id(1)

        @pl.when(i == 0)
        def _():
            db_ref[...] = jnp.zeros_like(db_ref)

        a = a_ref[...]
        for c0, cwid in _sub_chunks(CF):
            gcols, vcols = slice(c0, c0 + cwid), slice(CF + c0, CF + c0 + cwid)
            da = _dot(a, w_ref[gcols, :], NT)
            gt, vl = c_ref[:, gcols].astype(F32), c_ref[:, vcols].astype(F32)
            sg = _sigmoid(gt)
            dgt = da * vl * (sg * (1.0 + gt * (1.0 - sg)))
            dvl = da * (gt * sg)
            d_ref[:, gcols] = dgt.astype(BF16)
            d_ref[:, vcols] = dvl.astype(BF16)
            db_ref[:, gcols] += _colsum8(dgt)
            db_ref[:, vcols] += _colsum8(dvl)

    return pl.pallas_call(
        body, name=name, grid=(2, s // tm),
        in_specs=[pl.BlockSpec((tm, D), lambda j, i: (i, 0)), pl.BlockSpec((CF, D), lambda j, i: (j, 0)),
                  pl.BlockSpec((tm, 2 * CF), lambda j, i: (i, j))],
        out_specs=[pl.BlockSpec((tm, 2 * CF), lambda j, i: (i, j)), pl.BlockSpec((SUBLANES, 2 * CF), lambda j, i: (0, j))],
        out_shape=[jax.ShapeDtypeStruct((s, 2 * DFF), BF16), jax.ShapeDtypeStruct((SUBLANES, 2 * DFF), F32)],
        compiler_params=_params("parallel", "arbitrary"),
    )(dffn, w_down, conv)


def _conv_bwd_up_bwd(dc, up, cw, w_up, *, tm, name):
    s = up.shape[0]
    hb = tm // HALO
    nb = s // HALO
    nk = 2 * DFF // CF
    n = s // tm

    def body(d_ref, dn_ref, u_ref, cw_ref, w_ref, du_ref, dh_ref, dw_ref, acc, dwacc):
        i, k = pl.program_id(0), pl.program_id(1)

        @pl.when(jnp.logical_and(i == 0, k == 0))
        def _():
            dwacc[...] = jnp.zeros_like(dwacc)

        @pl.when(k == 0)
        def _():
            acc[...] = jnp.zeros_like(acc)

        live_next = i < n - 1
        part = None
        for c0, cwid in _sub_chunks(CF):
            cols = slice(c0, c0 + cwid)
            dcur = d_ref[:, cols].astype(F32)
            de = jnp.concatenate([dcur, jnp.where(live_next, dn_ref[:, cols].astype(F32), 0.0)], axis=0)
            d1 = _shift_rows(de, -1)[:tm]
            d2 = _shift_rows(de, -2)[:tm]
            du = (cw_ref[2:3, cols] * dcur + cw_ref[1:2, cols] * d1 + cw_ref[0:1, cols] * d2).astype(BF16)
            du_ref[:, cols] = du
            prod = _dot(du, w_ref[cols, :])
            part = prod if part is None else part + prod
            u = u_ref[:, cols].astype(F32)
            for tap, dsh in ((2, dcur), (1, d1), (0, d2)):
                dwacc[k, SUBLANES * tap:SUBLANES * (tap + 1), cols] += _colsum8(dsh * u)
        acc[...] += part

        @pl.when(k == nk - 1)
        def _():
            dh_ref[...] = acc[...].astype(dh_ref.dtype)

        @pl.when(jnp.logical_and(i == n - 1, k == nk - 1))
        def _():
            dw_ref[...] = dwacc[...]

    res = pl.pallas_call(
        body, name=name, grid=(n, nk),
        in_specs=[pl.BlockSpec((tm, CF), lambda i, k: (i, k)),
                  pl.BlockSpec((HALO, CF), lambda i, k: (jnp.minimum((i + 1) * hb, nb - 1), k)),
                  pl.BlockSpec((tm, CF), lambda i, k: (i, k)),
                  pl.BlockSpec((3, CF), lambda i, k: (0, k)),
                  pl.BlockSpec((CF, D), lambda i, k: (k, 0))],
        out_specs=[pl.BlockSpec((tm, CF), lambda i, k: (i, k)), pl.BlockSpec((tm, D), lambda i, k: (i, 0)),
                   _full((nk, 24, CF))],
        out_shape=[jax.ShapeDtypeStruct((s, 2 * DFF), BF16), jax.ShapeDtypeStruct((s, D), BF16),
                   jax.ShapeDtypeStruct((nk, 24, CF), F32)],
        scratch_shapes=[pltpu.VMEM((tm, D), F32), pltpu.VMEM((nk, 24, CF), F32)],
        compiler_params=_params("arbitrary", "arbitrary"),
    )(dc, dc, up, cw, w_up)
    return res[0], res[1], jnp.transpose(res[2], (1, 0, 2)).reshape(24, 2 * DFF)


def _ln_mod_bwd(dh, xin, g, scale, resid, extra, gate, *, ts, name, side=None):
    s = xin.shape[0]
    row = pl.BlockSpec((ts, D), lambda i: (i, 0))
    acc8 = pl.BlockSpec((SUBLANES, D), lambda i: (0, 0))
    with_gate = extra is not None

    def body(*refs):
        if with_gate:
            dh_ref, x_ref, g_ref, sc_ref, r_ref, e_ref, gt_ref, dx_ref, da_ref, dsh, dsc, dg, dgt = refs
        else:
            dh_ref, x_ref, g_ref, sc_ref, r_ref, dx_ref, dsh, dsc, dg = refs
        i = pl.program_id(0)

        @pl.when(i == 0)
        def _():
            for acc in (dsh, dsc, dg) + ((dgt,) if with_gate else ()):
                acc[...] = jnp.zeros_like(acc)

        xv, dhv = x_ref[...], dh_ref[...].astype(F32)
        r = lax.rsqrt(jnp.mean(xv * xv, axis=-1, keepdims=True) + EPS)
        xn = xv * r
        dsh[...] += _colsum8(dhv)
        dsc[...] += _colsum8(dhv * (xn * g_ref[...]))
        dhp = dhv * (1.0 + sc_ref[...])
        dg[...] += _colsum8(dhp * xn)
        dxn = dhp * g_ref[...]
        dx = r_ref[...] + r * (dxn - xn * jnp.mean(dxn * xn, axis=-1, keepdims=True))
        dx_ref[...] = dx
        if with_gate:
            da_ref[...] = (dx * gt_ref[...]).astype(BF16)
            dgt[...] += _colsum8(dx * e_ref[...].astype(F32))

    f32o, p8 = jax.ShapeDtypeStruct((s, D), F32), jax.ShapeDtypeStruct((SUBLANES, D), F32)
    if with_gate:
        ins, in_specs = (dh, xin, g, scale, resid, extra, gate), [row, row, _vec(D), _vec(D), row, row, _vec(D)]
        out_specs, out_shape = [row, row, acc8, acc8, acc8, acc8], [f32o, jax.ShapeDtypeStruct((s, D), BF16), p8, p8, p8, p8]
    else:
        ins, in_specs = (dh, xin, g, scale, resid), [row, row, _vec(D), _vec(D), row]
        out_specs, out_shape = [row, acc8, acc8, acc8], [f32o, p8, p8, p8]
    return _row_call(body, side, name=name, steps=s // ts, in_specs=in_specs, out_specs=out_specs,
                     out_shape=out_shape, ins=ins)


def _group_norm_bwd(t, dn_out, gvec, bd):
    r = _group_rsqrt(t, bd)
    dg_terms = dn_out * t * r
    dn = dn_out * gvec
    dt = r * (dn - t * (r * r) * (_split_dot(dn * t, bd) * (1.0 / HD)))
    return dt, dg_terms


def _mix_bwd(dmix, proj, o, pw, pb, ps, ag, bd, *, ts, name, side=None):
    s = proj.shape[0]
    hb = ts // HALO
    nb = s // HALO

    def body(dm_ref, dmn_ref, u_ref, uh_ref, o_ref, pw_ref, pb_ref, ps_ref, ag_ref, bd_ref,
             du_ref, do_ref, dpw_ref, dpb_ref, dps_ref, dag_ref):
        i = pl.program_id(0)
        n = s // ts

        @pl.when(i == 0)
        def _():
            for acc in (dpw_ref, dpb_ref, dps_ref, dag_ref):
                acc[...] = jnp.zeros_like(acc)

        for g, w in enumerate(POOL_WINDOWS):
            cols = slice(g * LANES, (g + 1) * LANES)
            wg = pw_ref[g]
            psg = ps_ref[:, cols]
            pooled = _pooled(u_ref[:, cols], uh_ref[:, cols], i, g, w, ts).astype(BF16)
            dy = dm_ref[:, cols].astype(F32)
            dps_ref[:, cols] += _colsum8(dy * (_dot(pooled, wg) + pb_ref[:, cols]))
            dpre = dy * psg
            dpb_ref[:, cols] += _colsum8(dpre)
            dpreb = dpre.astype(BF16)
            dpw_ref[g * LANES:(g + 1) * LANES, :] += _dot(pooled, dpreb, TN)
            dpool = _dot(dpreb, wg, NT)
            dnext = _dot((dmn_ref[:, cols].astype(F32) * psg).astype(BF16), wg, NT)
            dpe = jnp.concatenate([dpool, jnp.where(i < n - 1, dnext, 0.0)], axis=0)
            tpos = i * ts + lax.broadcasted_iota(jnp.int32, (ts + HALO, 1), 0)
            acc = dpe / jnp.minimum(tpos + 1, w).astype(F32)
            span = 1
            while span < w:
                acc = acc + _shift_rows(acc, -span)
                span *= 2
            du_ref[:, cols] = acc[:ts] - dpool
        ov = o_ref[...]
        dov, dg_terms = _group_norm_bwd(ov, dm_ref[:, DP:].astype(F32), ag_ref[...], bd_ref[...])
        do_ref[...] = dov
        dag_ref[...] += _colsum8(dg_terms)

    p8 = jax.ShapeDtypeStruct((SUBLANES, DP), F32)
    acc8 = pl.BlockSpec((SUBLANES, DP), lambda i: (0, 0))
    half = pl.BlockSpec((ts, DP), lambda i: (i, 0))
    return _row_call(
        body, side, name=name, steps=s // ts,
        in_specs=[pl.BlockSpec((ts, D), lambda i: (i, 0)),
                  pl.BlockSpec((HALO, DP), lambda i: (jnp.minimum((i + 1) * hb, nb - 1), 0)),
                  half, pl.BlockSpec((HALO, DP), lambda i: (jnp.maximum(i * hb - 1, 0), 0)),
                  half, _full((4, LANES, LANES)), _vec(DP), _vec(DP), _vec(DA), _full((DA, DA))],
        out_specs=[half, half, _full((DP, LANES)), acc8, acc8, acc8],
        out_shape=[jax.ShapeDtypeStruct((s, DP), F32), jax.ShapeDtypeStruct((s, DA), F32),
                   jax.ShapeDtypeStruct((DP, LANES), F32), p8, p8, p8],
        ins=(dmix, dmix, proj, proj, o, pw, pb, ps, ag, bd))


def _qk_norm_bwd(du, dq, dk, dv, proj, qg, kg, bd, *, ts, name):
    s = proj.shape[0]

    def body(du_ref, dq_ref, dk_ref, dv_ref, q_ref, k_ref, qg_ref, kg_ref, bd_ref, dp_ref, dqg_ref, dkg_ref):
        i = pl.program_id(0)

        @pl.when(i == 0)
        def _():
            dqg_ref[...] = jnp.zeros_like(dqg_ref)
            dkg_ref[...] = jnp.zeros_like(dkg_ref)

        bdv = bd_ref[...]
        dqr, tq = _group_norm_bwd(q_ref[...], dq_ref[...], qg_ref[...], bdv)
        dkr, tk = _group_norm_bwd(k_ref[...], dk_ref[...], kg_ref[...], bdv)
        dqg_ref[...] += _colsum8(tq)
        dkg_ref[...] += _colsum8(tk)
        dp_ref[:, 0:DP] = du_ref[...].astype(BF16)
        dp_ref[:, DP:DP + DA] = dqr.astype(BF16)
        dp_ref[:, DP + DA:DP + 2 * DA] = dkr.astype(BF16)
        dp_ref[:, DP + 2 * DA:] = dv_ref[...].astype(BF16)

    half = pl.BlockSpec((ts, DA), lambda i: (i, 0))
    col = lambda j: pl.BlockSpec((ts, DA), lambda i: (i, j))
    acc8 = pl.BlockSpec((SUBLANES, DA), lambda i: (0, 0))
    p8 = jax.ShapeDtypeStruct((SUBLANES, DA), F32)
    return pl.pallas_call(
        body, name=name, grid=(s // ts,),
        in_specs=[half, half, half, half, col(1), col(2), _vec(DA), _vec(DA), _full((DA, DA))],
        out_specs=[pl.BlockSpec((ts, DIN), lambda i: (i, 0)), acc8, acc8],
        out_shape=[jax.ShapeDtypeStruct((s, DIN), BF16), p8, p8],
        compiler_params=_params("arbitrary"),
    )(du, dq, dk, dv, proj, proj, qg, kg, bd)


def _split3(a):
    hi = a.astype(BF16)
    return hi, (a - hi.astype(F32)).astype(BF16)


def _dot3(a, b, dn):
    ah, al = _split3(a)
    bh, bl = _split3(b)
    return _dot(ah, bh, dn) + (_dot(ah, bl, dn) + _dot(al, bh, dn))


def _ada_fwd(c_all, w, b, name):
    nw = w.shape[1]

    def body(c_ref, w_ref, b_ref, o_ref):
        cv = c_ref[...]
        act = cv / (1.0 + jnp.exp(-cv))
        o_ref[...] = _dot3(act, w_ref[...], NN) + b_ref[...]

    return pl.pallas_call(
        body, name=name, in_specs=[_full((NDEV, D)), _full(w.shape), _full((1, nw))], out_specs=_full((NDEV, nw)),
        out_shape=jax.ShapeDtypeStruct((NDEV, nw), F32), grid=(1,), compiler_params=_params("arbitrary"),
    )(c_all, w, b)


def _ada_bwd(c_all, dmod, name):
    nw = dmod.shape[1]

    def body(c_ref, d_ref, o_ref):
        cv = c_ref[...]
        act = cv / (1.0 + jnp.exp(-cv))
        o_ref[...] = _dot3(act, d_ref[...], TN)[None]

    return pl.pallas_call(
        body, name=name, in_specs=[_full((NDEV, D)), _full((NDEV, nw))], out_specs=_full((1, D, nw)),
        out_shape=jax.ShapeDtypeStruct((1, D, nw), F32), grid=(1,), compiler_params=_params("arbitrary"),
    )(c_all, dmod)


def _fold_heads(v):
    acc = v[:, 0:HD]
    for h in range(1, DA // HD):
        acc = acc + v[:, h * HD:(h + 1) * HD]
    return acc


def _pack_partials(pieces, dcw_p, name):
    n_p = len(pieces)
    total = sum(p.shape[1] for p in pieces) + 3 * dcw_p.shape[1]
    npack = -(-total // (SUBLANES * LANES)) * (SUBLANES * LANES)

    def body(*refs):
        out = refs[-1]
        off = 0
        for r in refs[:n_p]:
            out[:, off:off + r.shape[1]] = jnp.sum(r[...], axis=0, keepdims=True)
            off += r.shape[1]
        dw = refs[n_p]
        for tap in range(3):
            out[:, off:off + dw.shape[1]] = jnp.sum(dw[SUBLANES * tap:SUBLANES * (tap + 1), :], axis=0, keepdims=True)
            off += dw.shape[1]
        if off < npack:
            out[:, off:] = jnp.zeros((1, npack - off), F32)

    arrs = list(pieces) + [dcw_p]
    return pl.pallas_call(
        body, name=name, grid=(1,), in_specs=[_full(a.shape) for a in arrs], out_specs=_full((1, npack)),
        out_shape=jax.ShapeDtypeStruct((1, npack), F32), compiler_params=_params("arbitrary"),
    )(*arrs)


def _small_update(gathered, gathered_pw, gathered_cw, specs, params, loss_off, name):
    names = [sp[0] for sp in specs]
    flat = []
    for nme in names + ["pool_w", "conv_w"]:
        flat += list(params[nme])
    n_in = len(flat)

    def body(*refs):
        ga_ref, gp_ref, gc_ref = refs[0], refs[1], refs[2]
        prm = refs[3:3 + n_in]
        outs = refs[3 + n_in:]
        total = ga_ref[0:1, :]
        for dv in range(1, NDEV):
            total = total + ga_ref[dv:dv + 1, :]
        k = 0
        for idx, (nme, off, width, fold) in enumerate(specs):
            g = total[:, off:off + width]
            if fold:
                g = _fold_heads(g)
            w_ref, m_ref, v_ref = prm[3 * idx:3 * idx + 3]
            d, nm, nv = _adamw_math(w_ref[...], g, m_ref[...], v_ref[...])
            for val in (g, d, nm, nv):
                outs[k][...] = val
                k += 1
        gpw = gp_ref[0]
        for dv in range(1, NDEV):
            gpw = gpw + gp_ref[dv]
        w_ref, m_ref, v_ref = prm[3 * len(specs):3 * len(specs) + 3]
        d, nm, nv = _adamw_math(w_ref[...], gpw, m_ref[...], v_ref[...])
        for val in (gpw, d, nm, nv):
            outs[k][...] = val
            k += 1
        gcw = gc_ref[0]
        for dv in range(1, NDEV):
            gcw = gcw + gc_ref[dv]
        w_ref, m_ref, v_ref = prm[3 * len(specs) + 3:3 * len(specs) + 6]
        d, nm, nv = _adamw_math(w_ref[...], gcw, m_ref[...], v_ref[...])
        for val in (gcw, d, nm, nv):
            outs[k][...] = val
            k += 1
        outs[k][...] = ga_ref[:, 0:6 * D]
        outs[k + 1][...] = total[:, loss_off:loss_off + LANES] * (1.0 / SUBLANES)

    out_shape, out_specs = [], []
    for nme in names + ["pool_w", "conv_w"]:
        shp = params[nme][0].shape
        out_shape += [jax.ShapeDtypeStruct(shp, F32)] * 4
        out_specs += [_full(shp)] * 4
    out_shape += [jax.ShapeDtypeStruct((NDEV, 6 * D), F32), jax.ShapeDtypeStruct((1, LANES), F32)]
    out_specs += [_full((NDEV, 6 * D)), _full((1, LANES))]
    res = pl.pallas_call(
        body, name=name, grid=(1,),
        in_specs=[_full(gathered.shape), _full(gathered_pw.shape), _full(gathered_cw.shape)] + [_full(a.shape) for a in flat],
        out_specs=out_specs, out_shape=out_shape, compiler_params=_params("arbitrary"),
    )(gathered, gathered_pw, gathered_cw, *flat)
    out = {nme: tuple(res[4 * i:4 * i + 4]) for i, nme in enumerate(names + ["pool_w", "conv_w"])}
    return out, res[-2], res[-1][0, 0]


def _row_tile(s):
    return 512 if s % 512 == 0 else s


def kernel(x, c, ada_w, ada_b, norm1_g, w_in, pool_w, pool_b, pool_scale, q_norm_g, k_norm_g, attn_out_g, w_out, norm2_g, w_up, conv_w, conv_b, w_down, loss_target, m_ada_w, m_ada_b, m_norm1_g, m_w_in, m_pool_w, m_pool_b, m_pool_scale, m_q_norm_g, m_k_norm_g, m_attn_out_g, m_w_out, m_norm2_g, m_w_up, m_conv_w, m_conv_b, m_w_down, v_ada_w, v_ada_b, v_norm1_g, v_w_in, v_pool_w, v_pool_b, v_pool_scale, v_q_norm_g, v_k_norm_g, v_attn_out_g, v_w_out, v_norm2_g, v_w_up, v_conv_w, v_conv_b, v_w_down):
    ax, ay, ac = lax.axis_index("x"), lax.axis_index("y"), lax.axis_index("c")
    me = 4 * ax + 2 * ay + ac
    me_swapped = 4 * ay + 2 * ax + ac
    xs, tgt = x[0], loss_target[0]
    s = xs.shape[0]
    ts = _row_tile(s)
    tq_attn, tk_attn, hp_attn = 256, 256, 2
    tmm = 2 * ts
    bd = _block_diag_ones(DA, HD)

    w_in_t = w_in[0].T.astype(BF16)
    w_up_t = w_up[0].T.astype(BF16)
    c_all = _all_gather([jnp.broadcast_to(c, (SUBLANES, D))], [False], "gather_c")[0][:, 0, :]
    n_ada = ada_w.shape[2]
    ada_b_mine = lax.dynamic_slice_in_dim(ada_b, me * n_ada, n_ada, axis=1)
    mod_part = _ada_fwd(c_all, ada_w[0], ada_b_mine, "ada_fwd")
    mod_all = _all_gather([mod_part], [False], "gather_mod")[0]
    mod = lax.dynamic_index_in_dim(mod_all, me, axis=1, keepdims=False).reshape(1, 6 * D)
    shift1, scale1, gate1, shift2, scale2, gate2 = [mod[:, k * D:(k + 1) * D] for k in range(6)]

    later_w = [w_out[0].astype(BF16), w_up_t, w_down[0].astype(BF16)]
    cb_full = jnp.transpose(conv_b.reshape(1, 2, 2, 2, 704), (0, 2, 1, 3, 4)).reshape(1, 2 * DFF)

    qg = jnp.tile(q_norm_g, (1, DA // HD))
    kg = jnp.tile(k_norm_g, (1, DA // HD))
    ag = attn_out_g.reshape(1, DA)
    pw = pool_w[0].astype(BF16)
    pb = pool_b.reshape(1, DP)
    h1, (gw_in, gcw) = _ln_mod(xs, norm1_g, scale1, shift1, ts=tmm, name="ln1",
                               side=_gather_side([w_in_t, jnp.pad(conv_w[0], ((0, 5), (0, 64)))], [False, True]))
    w_in_full = gw_in.reshape(DIN, D)
    cw_full = jnp.transpose(gcw[:, :3, :704], (1, 0, 2)).reshape(3, 2 * DFF)
    proj, qkv = _in_proj_qk_norm(h1, w_in_full, qg, kg, bd, tm=ts, name="in_proj_qk_norm")
    o_raw, m_tot, kb_first, (gw_out, gw_up, gw_down) = _attn_fwd(
        qkv, later_w, [False, True, False], tq=tq_attn, tk=tk_attn, hp=hp_attn, name="attn_fwd")
    w_out_full = gw_out.reshape(D, D)
    w_up_full = gw_up.reshape(2 * DFF, D)
    w_down_full = gw_down.reshape(DFF, D)
    mix = _pool_mix(proj, o_raw, pw, pb, pool_scale, ag, bd, ts=tmm, name="pool_mix")
    att, x1, h2 = _proj_res_ln_mod(mix, w_out_full, xs, gate1, norm2_g, scale2, shift2, tm=ts, name="out_proj_ln2")
    up, conv, act = _up_conv_gate(h2, w_up_full, cw_full, cb_full, tm=tmm, name="up_conv_gate")
    dy, dffn, dgate2_p, loss_p = _proj_loss_head(act, w_down_full, x1, tgt, gate2, tm=ts, name="down_proj_loss")

    g_w_down = _matmul(act, dffn, mode="tn", out_dtype=F32, tm=CF, tn=D, tk=2 * tmm, name="down_wgrad")
    dconv, dcb_p = _down_bwd_gate(dffn, w_down_full, conv, tm=tmm, name="down_bwd_gate")
    dup, dh2, dcw_p = _conv_bwd_up_bwd(dconv, up, cw_full, w_up_full, tm=ts, name="conv_bwd_up_bwd")
    g_w_up_t = _matmul(dup, h2, mode="tn", out_dtype=F32, tm=CF, tn=D, tk=2 * tmm, name="up_wgrad")
    (dx1, datt, dshift2_p, dscale2_p, dnorm2_p, dgate1_p), _ = _ln_mod_bwd(
        dh2, x1, norm2_g, scale2, dy, att, gate1, ts=ts, name="ln2_bwd")

    dmix = _matmul(datt, w_out_full, mode="nt", out_dtype=BF16, tm=tmm,tn=D, tk=D, name="out_bwd")
    g_w_out = _matmul(mix, datt, mode="tn", out_dtype=F32, tm=D, tn=D, tk=2 * tmm, name="out_wgrad")
    core = jnp.reshape(ac, (1,)).astype(jnp.int32)
    chip = jnp.reshape(2 * ax + ay, (1,)).astype(jnp.int32)
    big_ffn = [g_w_up_t.reshape(NDEV, 2 * DFF // NDEV, D), g_w_down.reshape(NDEV, DFF // NDEV, D),
               g_w_out.reshape(NDEV, D // NDEV, D)]
    swaps_ffn = [True, False, False]
    (du, do_raw, g_pw_p, dpb_p, dps_p, dag_p), gots_ffn = _mix_bwd(
        dmix, proj, o_raw, pw, pb, pool_scale, ag, bd, ts=tmm, name="mix_bwd", side=_pair_side(big_ffn, swaps_ffn))
    sums_ffn = [_pair_sum(big_ffn[k], gots_ffn[k], swaps_ffn[k], core, "rs_pair_sum_ffn%d" % k) for k in range(3)]
    dqn, dkn, dvv, parts_ffn = _attn_bwd(qkv, do_raw, m_tot, kb_first, sums_ffn, tq=tq_attn, tk=tk_attn, hp=hp_attn, name="attn_bwd")
    dproj, dqg_p, dkg_p = _qk_norm_bwd(du, dqn, dkn, dvv, proj, qg, kg, bd, ts=tmm, name="qk_norm_bwd")
    g_w_in_t = _matmul(dproj, h1, mode="tn", out_dtype=F32, tm=DIN // 2, tn=D, tk=2 * tmm, name="in_wgrad")
    big = [g_w_in_t.reshape(NDEV, DIN // NDEV, D)]
    gots = _pair_exchange(big, [False], "rs_pair")
    sums = [_pair_sum(big[0], gots[0], False, core, "rs_pair_sum")]
    dh1, parts = _matmul(dproj, w_in_full, mode="nn", out_dtype=BF16, tm=tmm,tn=D, tk=DIN, name="in_bwd",
                         side=_chip_side(sums))
    (grad_x, dshift1_p, dscale1_p, dnorm1_p), _ = _ln_mod_bwd(
        dh1, xs, norm1_g, scale1, dx1, None, None, ts=tmm, name="ln1_bwd")

    tr = lambda a: a[0].T
    r_in = _adamw_reduce(tr(w_in), tr(m_w_in), tr(v_w_in), sums[0], parts[0], chip, "adamw_w_in")
    r_out = _adamw_reduce(w_out[0], m_w_out[0], v_w_out[0], sums_ffn[2], parts_ffn[2], chip, "adamw_w_out")
    r_up = _adamw_reduce(tr(w_up), tr(m_w_up), tr(v_w_up), sums_ffn[0], parts_ffn[0], chip, "adamw_w_up")
    r_down = _adamw_reduce(w_down[0], m_w_down[0], v_w_down[0], sums_ffn[1], parts_ffn[1], chip, "adamw_w_down")
    r_in = [a.T[None] for a in r_in]
    r_up = [a.T[None] for a in r_up]
    r_out = [a[None] for a in r_out]
    r_down = [a[None] for a in r_down]

    dcb_nat = jnp.transpose(dcb_p.reshape(SUBLANES, 2, 2, 2, 704), (0, 2, 1, 3, 4)).reshape(SUBLANES, 2 * DFF)
    pieces = [dshift1_p, dscale1_p, dgate1_p, dshift2_p, dscale2_p, dgate2_p,
              dnorm1_p, dnorm2_p, dcb_nat, dpb_p, dps_p, dag_p, dqg_p, dkg_p, loss_p]
    n_vec = sum(p.shape[1] for p in pieces)
    packed = _pack_partials(pieces, dcw_p, "pack_partials")
    npack = packed.shape[1]
    gathered, gathered_pw = _all_gather([packed.reshape(SUBLANES, npack // SUBLANES), g_pw_p], [False, False], "gather_small")
    gathered = gathered.reshape(NDEV, npack)
    gathered_cw = lax.dynamic_index_in_dim(
        gathered[:, n_vec:n_vec + 6 * DFF].reshape(NDEV, 3, NDEV, 704), me_swapped, axis=2, keepdims=False)
    specs = [("ada_b", 0, 6 * D, False)]
    off = 6 * D
    for nme, width, fold in (("norm1_g", D, False), ("norm2_g", D, False), ("conv_b", 2 * DFF, False),
                             ("pool_b", DP, False), ("pool_scale", DP, False), ("attn_out_g", DA, False),
                             ("q_norm_g", DA, True), ("k_norm_g", DA, True)):
        specs.append((nme, off, width, fold))
        off += width
    small = {
        "ada_b": (ada_b, m_ada_b, v_ada_b),
        "norm1_g": (norm1_g, m_norm1_g, v_norm1_g), "norm2_g": (norm2_g, m_norm2_g, v_norm2_g),
        "conv_b": (conv_b, m_conv_b, v_conv_b),
        "pool_b": (pb, m_pool_b.reshape(1, DP), v_pool_b.reshape(1, DP)),
        "pool_scale": (pool_scale, m_pool_scale, v_pool_scale),
        "attn_out_g": (ag, m_attn_out_g.reshape(1, DA), v_attn_out_g.reshape(1, DA)),
        "q_norm_g": (q_norm_g, m_q_norm_g, v_q_norm_g), "k_norm_g": (k_norm_g, m_k_norm_g, v_k_norm_g),
        "pool_w": (pool_w.reshape(DP, LANES), m_pool_w.reshape(DP, LANES), v_pool_w.reshape(DP, LANES)),
        "conv_w": (conv_w[0], m_conv_w[0], v_conv_w[0]),
    }
    upd, dmod_all, loss = _small_update(gathered, gathered_pw, gathered_cw, specs, small, off, "small_update")
    g_ada_w = _ada_bwd(c_all, lax.dynamic_slice_in_dim(dmod_all, me * n_ada, n_ada, axis=1), "ada_bwd")
    r_ada = [g_ada_w] + [a[None] for a in _adamw(ada_w[0], m_ada_w[0], v_ada_w[0], g_ada_w[0], "adamw_ada_w")]

    shapes = {"ada_b": ada_b.shape, "norm1_g": norm1_g.shape, "pool_w": pool_w.shape, "pool_b": pool_b.shape,
              "pool_scale": pool_scale.shape, "q_norm_g": q_norm_g.shape, "k_norm_g": k_norm_g.shape,
              "attn_out_g": attn_out_g.shape, "norm2_g": norm2_g.shape, "conv_w": conv_w.shape, "conv_b": conv_b.shape}
    res = {nme: [a.reshape(shapes[nme]) for a in upd[nme]] for nme in shapes}
    res.update(ada_w=r_ada, w_in=r_in, w_out=r_out, w_up=r_up, w_down=r_down)
    names = ["ada_w", "ada_b", "norm1_g", "w_in", "pool_w", "pool_b", "pool_scale", "q_norm_g", "k_norm_g",
             "attn_out_g", "w_out", "norm2_g", "w_up", "conv_w", "conv_b", "w_down"]
    outs = [loss, grad_x[None]]
    for q in range(4):
        outs += [res[nme][q] for nme in names]
    return tuple(outs)
```

```python
import functools
import math

import numpy as np
import jax
import jax.numpy as jnp
from jax import lax
from jax.experimental import pallas as pl
from jax.experimental.pallas import tpu as pltpu

F32, BF16 = jnp.float32, jnp.bfloat16
D = 1024
DP = 512
DA = 512
HD = 64
DIN = DP + 3 * DA
DFF = 2816
POOL_WINDOWS = (2, 4, 8, 16)
HALO = 16
EPS = 1e-6
LANES = 128
SUBLANES = 8
NDEV = 8
VMEM_LIMIT = 56 * 1024 * 1024
MESH = pl.DeviceIdType.MESH

ADAM_LR, ADAM_B1, ADAM_B2, ADAM_EPS, ADAM_WD, ADAM_STEP = 0.001, 0.9, 0.999, 1e-08, 0.01, 10

NN = (((1,), (0,)), ((), ()))
NT = (((1,), (1,)), ((), ()))
TN = (((0,), (0,)), ((), ()))


def _params(*sem):
    return pltpu.CompilerParams(dimension_semantics=sem, vmem_limit_bytes=VMEM_LIMIT)


def _full(shape):
    nd = len(shape)
    return pl.BlockSpec(shape, lambda *_: (0,) * nd)


def _dot(a, b, dn=NN):
    return lax.dot_general(a, b, dn, preferred_element_type=F32)


def _split_dot(a, b, dn=NN):
    hi = a.astype(BF16)
    lo = (a - hi.astype(F32)).astype(BF16)
    return _dot(hi, b, dn) + _dot(lo, b, dn)


def _colsum8(v):
    r, n = v.shape
    return v.reshape(r // SUBLANES, SUBLANES, n).sum(axis=0)


def _block_diag_ones(n, blk):
    i = np.arange(n) // blk
    return jnp.asarray((i[:, None] == i[None, :]).astype(np.float32), BF16)


def _matmul(a, b, *, mode, out_dtype, tm, tn, tk, name, n_outer=False, side=None):
    if mode == "tn":
        K, M = a.shape
        N = b.shape[1]
    elif mode == "nt":
        M, K = a.shape
        N = b.shape[0]
    else:
        M, K = a.shape
        N = b.shape[1]
    tm, tn, tk = min(tm, M), min(tn, N), min(tk, K)
    assert M % tm == 0 and N % tn == 0 and K % tk == 0, (name, M, N, K, tm, tn, tk)
    nk = K // tk
    dn = {"nn": NN, "nt": NT, "tn": TN}[mode]

    def body(a_ref, b_ref, o_ref, *acc):
        if nk == 1:
            o_ref[...] = _dot(a_ref[...], b_ref[...], dn).astype(o_ref.dtype)
            return
        acc_ref, = acc
        k = pl.program_id(2)

        @pl.when(k == 0)
        def _():
            acc_ref[...] = jnp.zeros_like(acc_ref)

        acc_ref[...] += _dot(a_ref[...], b_ref[...], dn)

        @pl.when(k == nk - 1)
        def _():
            o_ref[...] = acc_ref[...].astype(o_ref.dtype)

    if n_outer:
        gi = lambda g: (g[1], g[0], g[2])
        grid = (N // tn, M // tm, nk)
    else:
        gi = lambda g: g
        grid = (M // tm, N // tn, nk)

    def amap(*g):
        i, j, k = gi(g)
        return (k, i) if mode == "tn" else (i, k)

    def bmap(*g):
        i, j, k = gi(g)
        return (j, k) if mode == "nt" else (k, j)

    def omap(*g):
        i, j, k = gi(g)
        return (i, j)

    a_blk = (tk, tm) if mode == "tn" else (tm, tk)
    b_blk = (tn, tk) if mode == "nt" else (tk, tn)
    acc_scratch = [] if nk == 1 else [pltpu.VMEM((tm, tn), F32)]
    if side is None:
        return pl.pallas_call(
            body, name=name, grid=grid,
            in_specs=[pl.BlockSpec(a_blk, amap), pl.BlockSpec(b_blk, bmap)],
            out_specs=pl.BlockSpec((tm, tn), omap),
            out_shape=jax.ShapeDtypeStruct((M, N), out_dtype),
            scratch_shapes=acc_scratch,
            compiler_params=_params("parallel", "parallel", "arbitrary"),
        )(a, b)

    ne = len(side.arrs)
    steps = grid[0] * grid[1] * grid[2]

    nsem = len(side.scratch)

    def with_side(*refs):
        e_in, e_out = refs[2:2 + ne], refs[3 + ne:3 + 2 * ne]
        sems = refs[len(refs) - nsem:]
        step = (pl.program_id(0) * grid[1] + pl.program_id(1)) * grid[2] + pl.program_id(2)

        @pl.when(step == 0)
        def _():
            side.start(e_in, e_out, *sems)

        body(refs[0], refs[1], refs[2 + ne], *refs[3 + 2 * ne:len(refs) - nsem])

        @pl.when(step == steps - 1)
        def _():
            side.finish(e_in, e_out, *sems)

    any_spec = pl.BlockSpec(memory_space=pl.ANY)
    res = pl.pallas_call(
        with_side, name=name, grid=grid,
        in_specs=[pl.BlockSpec(a_blk, amap), pl.BlockSpec(b_blk, bmap)] + [any_spec] * ne,
        out_specs=[pl.BlockSpec((tm, tn), omap)] + [any_spec] * ne,
        out_shape=[jax.ShapeDtypeStruct((M, N), out_dtype)] + side.out_shapes,
        scratch_shapes=acc_scratch + side.scratch,
        compiler_params=_params("arbitrary", "arbitrary", "arbitrary"),
    )(a, b, *side.arrs)
    return res[0], list(res[1:])


def _slot(swap, px, py, pc):
    return 4 * py + 2 * px + pc if swap else 4 * px + 2 * py + pc


class _Gather:
    def __init__(self, ins, outs, send, recv, loc, swaps):
        self.ins, self.outs, self.send, self.recv, self.loc, self.swaps = ins, outs, send, recv, loc, swaps
        x, y, c = lax.axis_index("x"), lax.axis_index("y"), lax.axis_index("c")
        self.me, self.sib = (x, y, c), (x, y, 1 - c)
        self.chips = [(1 - x, y), (x, 1 - y), (1 - x, 1 - y)]
        self.n = len(ins)

    @staticmethod
    def scratch(n):
        return [pltpu.SemaphoreType.DMA((7 * n,)), pltpu.SemaphoreType.DMA((7 * n,)), pltpu.SemaphoreType.DMA((n,))]

    def copy(self, a, k, blk, to, src=None):
        rows = self.outs[a].at[_slot(self.swaps[a], *blk)]
        return pltpu.make_async_remote_copy(
            src_ref=rows if src is None else src, dst_ref=rows,
            send_sem=self.send.at[7 * a + k], recv_sem=self.recv.at[7 * a + k], device_id=to, device_id_type=MESH)

    def mine(self, a):
        return pltpu.make_async_copy(self.ins[a], self.outs[a].at[_slot(self.swaps[a], *self.me)], self.loc.at[a])

    def first(self, a):
        c = self.me[2]
        return [self.copy(a, 0, self.me, self.sib, src=self.ins[a])] + [
            self.copy(a, 1 + j, self.me, (*chip, c), src=self.ins[a]) for j, chip in enumerate(self.chips)]

    def forwards(self, a):
        c = self.me[2]
        return [self.copy(a, 4 + j, (*chip, c), self.sib) for j, chip in enumerate(self.chips)]

    def start(self):
        for a in range(self.n):
            self.mine(a).start()
        for a in range(self.n):
            for cp in self.first(a):
                cp.start()

    def forward(self):
        c = self.me[2]
        for a in range(self.n):
            fwd = self.forwards(a)
            for j, chip in enumerate(self.chips):
                self.copy(a, 1 + j, (*chip, c), self.me).wait_recv()
                fwd[j].start()

    def finish(self):
        c = self.me[2]
        for a in range(self.n):
            self.copy(a, 0, self.sib, self.me).wait_recv()
            for j, chip in enumerate(self.chips):
                self.copy(a, 4 + j, (*chip, 1 - c), self.me).wait_recv()
        for a in range(self.n):
            for cp in self.first(a) + self.forwards(a):
                cp.wait_send()
            self.mine(a).wait()


def _all_gather(arrs, swaps, name):
    n = len(arrs)

    def body(*refs):
        g = _Gather(refs[:n], refs[n:2 * n], *refs[2 * n:], swaps)
        g.start()
        g.forward()
        g.finish()

    any_spec = pl.BlockSpec(memory_space=pl.ANY)
    return pl.pallas_call(
        body, name=name,
        in_specs=[any_spec] * n, out_specs=[any_spec] * n,
        out_shape=[jax.ShapeDtypeStruct((NDEV,) + a.shape, a.dtype) for a in arrs],
        scratch_shapes=_Gather.scratch(n),
    )(*arrs)


def _all_gather_small(arr, name):
    def body(in_ref, out_ref, send, recv, loc):
        x, y, c = lax.axis_index("x"), lax.axis_index("y"), lax.axis_index("c")
        flip = lambda v, bit: 1 - v if bit else v
        peers = [(flip(x, k >> 2 & 1), flip(y, k >> 1 & 1), flip(c, k & 1)) for k in range(1, NDEV)]
        mine = pltpu.make_async_copy(in_ref, out_ref.at[_slot(False, x, y, c)], loc)
        mine.start()

        def copy(k, src_dev, to):
            return pltpu.make_async_remote_copy(
                src_ref=in_ref, dst_ref=out_ref.at[_slot(False, *src_dev)], send_sem=send.at[k], recv_sem=recv.at[k],
                device_id=to, device_id_type=MESH)

        sends = [copy(k, (x, y, c), peer) for k, peer in enumerate(peers)]
        for cp in sends:
            cp.start()
        for k, peer in enumerate(peers):
            copy(k, peer, (x, y, c)).wait_recv()
        for cp in sends:
            cp.wait_send()
        mine.wait()

    any_spec = pl.BlockSpec(memory_space=pl.ANY)
    return pl.pallas_call(
        body, name=name, in_specs=[any_spec], out_specs=any_spec,
        out_shape=jax.ShapeDtypeStruct((NDEV,) + arr.shape, arr.dtype),
        scratch_shapes=[pltpu.SemaphoreType.DMA((NDEV - 1,)), pltpu.SemaphoreType.DMA((NDEV - 1,)), pltpu.SemaphoreType.DMA],
    )(arr)


def _pair_copies(ins, gots, send, recv, swaps):
    x, y, c = lax.axis_index("x"), lax.axis_index("y"), lax.axis_index("c")
    return [pltpu.make_async_remote_copy(
        src_ref=ins[a].at[_slot(swaps[a], k // 2, k % 2, 1 - c)], dst_ref=gots[a].at[k],
        send_sem=send.at[4 * a + k], recv_sem=recv.at[4 * a + k], device_id=(x, y, 1 - c), device_id_type=MESH)
        for a in range(len(ins)) for k in range(4)]


def _pair_exchange(arrs, swaps, name):
    n = len(arrs)

    def body(*refs):
        rems = _pair_copies(refs[:n], refs[n:2 * n], *refs[2 * n:], swaps)
        for rc in rems:
            rc.start()
        for rc in rems:
            rc.wait_recv()
        for rc in rems:
            rc.wait_send()

    any_spec = pl.BlockSpec(memory_space=pl.ANY)
    return pl.pallas_call(
        body, name=name,
        in_specs=[any_spec] * n, out_specs=[any_spec] * n,
        out_shape=[jax.ShapeDtypeStruct((4,) + a.shape[1:], a.dtype) for a in arrs],
        scratch_shapes=[pltpu.SemaphoreType.DMA((4 * n,)), pltpu.SemaphoreType.DMA((4 * n,))],
    )(*arrs)


def _chip_copies(ins, outs, send, recv):
    x, y, c = lax.axis_index("x"), lax.axis_index("y"), lax.axis_index("c")
    chips = [(1 - x, y), (x, 1 - y), (1 - x, 1 - y)]
    return [pltpu.make_async_remote_copy(
        src_ref=ins[a].at[2 * px + py], dst_ref=outs[a].at[j], send_sem=send.at[3 * a + j], recv_sem=recv.at[3 * a + j],
        device_id=(px, py, c), device_id_type=MESH) for a in range(len(ins)) for j, (px, py) in enumerate(chips)]


def _chip_exchange(arrs, name):
    n = len(arrs)

    def body(*refs):
        rems = _chip_copies(refs[:n], refs[n:2 * n], *refs[2 * n:])
        for rc in rems:
            rc.start()
        for rc in rems:
            rc.wait_recv()
        for rc in rems:
            rc.wait_send()

    any_spec = pl.BlockSpec(memory_space=pl.ANY)
    return pl.pallas_call(
        body, name=name,
        in_specs=[any_spec] * n, out_specs=[any_spec] * n,
        out_shape=[jax.ShapeDtypeStruct((3,) + a.shape[1:], a.dtype) for a in arrs],
        scratch_shapes=[pltpu.SemaphoreType.DMA((3 * n,)), pltpu.SemaphoreType.DMA((3 * n,))],
    )(*arrs)


class _Side:
    def __init__(self, arrs, out_shapes, scratch, start, finish, mid=None):
        self.arrs, self.out_shapes, self.scratch = list(arrs), list(out_shapes), list(scratch)
        self.start, self.finish, self.mid = start, finish, mid


def _copies_side(arrs, out_shapes, n_copies, make):
    def start(ins, outs, *sems):
        for cp in make(ins, outs, *sems):
            cp.start()

    def finish(ins, outs, *sems):
        cps = make(ins, outs, *sems)
        for cp in cps:
            cp.wait_recv()
        for cp in cps:
            cp.wait_send()

    return _Side(arrs, out_shapes, [pltpu.SemaphoreType.DMA((n_copies,)), pltpu.SemaphoreType.DMA((n_copies,))], start, finish)


def _pair_side(arrs, swaps):
    return _copies_side(arrs, [jax.ShapeDtypeStruct((4,) + a.shape[1:], a.dtype) for a in arrs], 4 * len(arrs),
                        functools.partial(_pair_copies, swaps=swaps))


def _chip_side(arrs):
    return _copies_side(arrs, [jax.ShapeDtypeStruct((3,) + a.shape[1:], a.dtype) for a in arrs], 3 * len(arrs), _chip_copies)


def _gather_side(arrs, swaps):
    return _Side(arrs, [jax.ShapeDtypeStruct((NDEV,) + a.shape, a.dtype) for a in arrs], _Gather.scratch(len(arrs)),
                 start=lambda ins, outs, *sems: _Gather(ins, outs, *sems, swaps).start(),
                 mid=lambda ins, outs, *sems: _Gather(ins, outs, *sems, swaps).forward(),
                 finish=lambda ins, outs, *sems: _Gather(ins, outs, *sems, swaps).finish())


def _row_call(body, side, *, name, steps, in_specs, out_specs, out_shape, ins):
    if side is None:
        res = pl.pallas_call(body, name=name, grid=(steps,), in_specs=in_specs, out_specs=out_specs, out_shape=out_shape,
                             compiler_params=_params("arbitrary"))(*ins)
        return list(res), []
    n_in, n_out, ne = len(in_specs), len(out_specs), len(side.arrs)

    def wrapped(*refs):
        e_in = refs[n_in:n_in + ne]
        e_out = refs[n_in + ne + n_out:n_in + 2 * ne + n_out]
        sems = refs[n_in + 2 * ne + n_out:]
        i = pl.program_id(0)

        @pl.when(i == 0)
        def _():
            side.start(e_in, e_out, *sems)

        if side.mid is not None:
            @pl.when(i == steps // 2)
            def _():
                side.mid(e_in, e_out, *sems)

        body(*refs[:n_in], *refs[n_in + ne:n_in + ne + n_out])

        @pl.when(i == steps - 1)
        def _():
            side.finish(e_in, e_out, *sems)

    any_spec = pl.BlockSpec(memory_space=pl.ANY)
    res = pl.pallas_call(
        wrapped, name=name, grid=(steps,), in_specs=list(in_specs) + [any_spec] * ne,
        out_specs=list(out_specs) + [any_spec] * ne, out_shape=list(out_shape) + side.out_shapes,
        scratch_shapes=side.scratch,
        compiler_params=_params("arbitrary"))(*ins, *side.arrs)
    return list(res[:n_out]), list(res[n_out:])


def _pair_sum(grads, got, swap, core, name):
    _, r, c = got.shape
    tr = r if r <= 352 else r // 2

    def own_map(k, i, core_ref):
        return (_slot(swap, k // 2, k % 2, core_ref[0]), i, 0)

    def body(core_ref, a_ref, b_ref, o_ref):
        o_ref[...] = a_ref[...] + b_ref[...]

    spec = pl.BlockSpec((None, tr, c), lambda k, i, core_ref: (k, i, 0))
    return pl.pallas_call(
        body, name=name,
        grid_spec=pltpu.PrefetchScalarGridSpec(
            num_scalar_prefetch=1, grid=(4, r // tr),
            in_specs=[pl.BlockSpec((None, tr, c), own_map), spec], out_specs=spec),
        out_shape=jax.ShapeDtypeStruct(got.shape, got.dtype), compiler_params=_params("parallel", "parallel"),
    )(core, grads, got)


def _adamw_math(w, g, m, v):
    m = ADAM_B1 * m + (1.0 - ADAM_B1) * g
    v = ADAM_B2 * v + (1.0 - ADAM_B2) * (g * g)
    m_hat = m / (1.0 - ADAM_B1 ** ADAM_STEP)
    v_hat = v / (1.0 - ADAM_B2 ** ADAM_STEP)
    delta = -ADAM_LR * (m_hat / (jnp.sqrt(v_hat) + ADAM_EPS) + ADAM_WD * w)
    return delta, m, v


def _adamw_tile(r):
    for cand in (256, 352, 128):
        if r % cand == 0:
            return cand
    return r


def _adamw(w, m, v, g, name):
    r, c = w.shape
    tr = _adamw_tile(r)
    spec = pl.BlockSpec((tr, c), lambda i: (i, 0))

    def body(w_ref, m_ref, v_ref, g_ref, d_ref, nm_ref, nv_ref):
        d_ref[...], nm_ref[...], nv_ref[...] = _adamw_math(w_ref[...], g_ref[...], m_ref[...], v_ref[...])

    out = jax.ShapeDtypeStruct((r, c), F32)
    return pl.pallas_call(
        body, name=name, grid=(r // tr,), in_specs=[spec] * 4, out_specs=[spec] * 3, out_shape=[out] * 3,
        compiler_params=_params("parallel"),
    )(w, m, v, g)


def _adamw_reduce(w, m, v, sums, recv, chip, name):
    r, c = w.shape
    tr = _adamw_tile(r)
    spec = pl.BlockSpec((tr, c), lambda i, chip_ref: (i, 0))

    def body(chip_ref, w_ref, m_ref, v_ref, s_ref, p_ref, g_ref, d_ref, nm_ref, nv_ref):
        g = ((s_ref[...] + p_ref[0]) + p_ref[1]) + p_ref[2]
        g_ref[...] = g
        d_ref[...], nm_ref[...], nv_ref[...] = _adamw_math(w_ref[...], g, m_ref[...], v_ref[...])

    out = jax.ShapeDtypeStruct((r, c), F32)
    return pl.pallas_call(
        body, name=name,
        grid_spec=pltpu.PrefetchScalarGridSpec(
            num_scalar_prefetch=1, grid=(r // tr,),
            in_specs=[spec, spec, spec, pl.BlockSpec((None, tr, c), lambda i, chip_ref: (chip_ref[0], i, 0)),
                      pl.BlockSpec((3, tr, c), lambda i, chip_ref: (0, i, 0))],
            out_specs=[spec] * 4),
        out_shape=[out] * 4, compiler_params=_params("parallel"),
    )(chip, w, m, v, sums, recv)


def _vec(n):
    return pl.BlockSpec((1, n), lambda *_: (0, 0))


def _ln_mod(x, g, scale, shift, *, ts, name, side=None):
    s = x.shape[0]
    row = pl.BlockSpec((ts, D), lambda i: (i, 0))

    def body(x_ref, g_ref, sc_ref, sh_ref, h_ref):
        xv = x_ref[...]
        r = lax.rsqrt(jnp.mean(xv * xv, axis=-1, keepdims=True) + EPS)
        h = (xv * r) * g_ref[...]
        h_ref[...] = (h * (1.0 + sc_ref[...]) + sh_ref[...]).astype(BF16)

    (h,), extra = _row_call(body, side, name=name, steps=s // ts, in_specs=[row, _vec(D), _vec(D), _vec(D)],
                            out_specs=[row], out_shape=[jax.ShapeDtypeStruct((s, D), BF16)], ins=(x, g, scale, shift))
    return h, extra


def _proj_res_ln_mod(mix, w, x, gate, g, scale, shift, *, tm, name):
    s = x.shape[0]
    row = pl.BlockSpec((tm, D), lambda i: (i, 0))

    def body(m_ref, w_ref, x_ref, gt_ref, g_ref, sc_ref, sh_ref, a_ref, x1_ref, h_ref):
        att = _dot(m_ref[...], w_ref[...])
        a_ref[...] = att.astype(BF16)
        x1 = x_ref[...] + gt_ref[...] * att
        x1_ref[...] = x1
        r = lax.rsqrt(jnp.mean(x1 * x1, axis=-1, keepdims=True) + EPS)
        h = (x1 * r) * g_ref[...]
        h_ref[...] = (h * (1.0 + sc_ref[...]) + sh_ref[...]).astype(BF16)

    return pl.pallas_call(
        body, name=name, grid=(s // tm,), in_specs=[row, _full(w.shape), row] + [_vec(D)] * 4, out_specs=[row, row, row],
        out_shape=[jax.ShapeDtypeStruct((s, D), BF16), jax.ShapeDtypeStruct((s, D), F32), jax.ShapeDtypeStruct((s, D), BF16)],
        compiler_params=_params("parallel"),
    )(mix, w, x, gate, g, scale, shift)


def _proj_loss_head(act, w, x1, tgt, gate2, *, tm, name):
    s = x1.shape[0]
    n = s // tm
    row = pl.BlockSpec((tm, D), lambda i: (i, 0))
    acc8 = pl.BlockSpec((SUBLANES, D), lambda i: (0, 0))

    def body(a_ref, w_ref, x_ref, t_ref, g_ref, dy_ref, df_ref, dg_ref, loss_ref, lacc):
        i = pl.program_id(0)

        @pl.when(i == 0)
        def _():
            lacc[...] = jnp.zeros_like(lacc)
            dg_ref[...] = jnp.zeros_like(dg_ref)

        f = _dot(a_ref[...], w_ref[...])
        diff = x_ref[...] + g_ref[...] * f - t_ref[...]
        lacc[...] += _colsum8(diff * diff)
        dy = diff * (1.0 / D)
        dy_ref[...] = dy
        df_ref[...] = (dy * g_ref[...]).astype(BF16)
        dg_ref[...] += _colsum8(dy * f)

        @pl.when(i == n - 1)
        def _():
            loss_ref[...] = jnp.full((SUBLANES, LANES), (0.5 / D) * jnp.sum(lacc[...]), F32)

    return pl.pallas_call(
        body, name=name, grid=(n,),
        in_specs=[pl.BlockSpec((tm, act.shape[1]), lambda i: (i, 0)), _full(w.shape), row, row, _vec(D)],
        out_specs=[row, row, acc8, _full((SUBLANES, LANES))],
        out_shape=[jax.ShapeDtypeStruct((s, D), F32), jax.ShapeDtypeStruct((s, D), BF16),
                   jax.ShapeDtypeStruct((SUBLANES, D), F32), jax.ShapeDtypeStruct((SUBLANES, LANES), F32)],
        scratch_shapes=[pltpu.VMEM((SUBLANES, D), F32)], compiler_params=_params("arbitrary"),
    )(act, w, x1, tgt, gate2)


def _group_rsqrt(t, bd):
    return lax.rsqrt(_split_dot(t * t, bd) * (1.0 / HD) + EPS)


def _in_proj_qk_norm(h, w, qg, kg, bd, *, tm, name):
    s = h.shape[0]

    def body(h_ref, w_ref, qg_ref, kg_ref, bd_ref, p_ref, o_ref):
        bdv = bd_ref[...]
        hv = h_ref[...]
        p_ref[:, 0:DP] = _dot(hv, w_ref[0:DP, :], NT)
        q = _dot(hv, w_ref[DP:DP + DA, :], NT)
        p_ref[:, DP:DP + DA] = q
        o_ref[:, 0:DA] = (q * _group_rsqrt(q, bdv) * qg_ref[...]).astype(BF16)
        k = _dot(hv, w_ref[DP + DA:DP + 2 * DA, :], NT)
        p_ref[:, DP + DA:DP + 2 * DA] = k
        o_ref[:, DA:2 * DA] = (k * _group_rsqrt(k, bdv) * kg_ref[...]).astype(BF16)
        v = _dot(hv, w_ref[DP + 2 * DA:, :], NT)
        p_ref[:, DP + 2 * DA:] = v
        o_ref[:, 2 * DA:] = v.astype(BF16)

    return pl.pallas_call(
        body, name=name, grid=(s // tm,),
        in_specs=[pl.BlockSpec((tm, D), lambda i: (i, 0)), _full(w.shape), _vec(DA), _vec(DA), _full((DA, DA))],
        out_specs=[pl.BlockSpec((tm, DIN), lambda i: (i, 0)), pl.BlockSpec((tm, 3 * DA), lambda i: (i, 0))],
        out_shape=[jax.ShapeDtypeStruct((s, DIN), F32), jax.ShapeDtypeStruct((s, 3 * DA), BF16)],
        compiler_params=_params("parallel"),
    )(h, w, qg, kg, bd)


EXP_UNDERFLOW = -120.0


def _log_terms(z):
    neg_abs = lax.bitcast_convert_type(lax.bitcast_convert_type(z, jnp.uint32) | jnp.uint32(0x80000000), F32)
    b = jnp.minimum(z, 0.0) - jnp.log(1.0 + jnp.exp(neg_abs))
    return b, b - z


def _head_masks(rows):
    lane = lax.broadcasted_iota(jnp.int32, (rows, LANES), 1)
    return [lane < HD, lane >= HD]


def _attn_fwd(qkv, gather, swaps, *, tq, tk, hp, name):
    s = qkv.shape[0]
    nrep = tk // LANES
    ndiag = tq // tk
    ng = len(gather)
    nh, wl = 2 * hp, LANES * hp
    ngrp, nq = DA // wl, s // tq
    lanes = [slice(LANES * pp, LANES * (pp + 1)) for pp in range(hp)]

    def body(*refs):
        q_ref, k_ref, v_ref = refs[:3]
        g_in = refs[3:3 + ng]
        o_ref, tot_ref, first_ref = refs[3 + ng:6 + ng]
        g_out = refs[6 + ng:6 + 2 * ng]
        oacc, rc = refs[6 + 2 * ng:8 + 2 * ng]
        g_sems = refs[8 + 2 * ng:]
        i = pl.program_id(1)
        step_id = pl.program_id(0) * nq + i

        @pl.when(step_id == 0)
        def _():
            _Gather(g_in, g_out, *g_sems, swaps).start()

        @pl.when(step_id == (ngrp * nq * 3) // 4)
        def _():
            _Gather(g_in, g_out, *g_sems, swaps).forward()

        heads = _head_masks(tq)
        qs = [jnp.where(heads[a % 2], q_ref[:, lanes[a // 2]] * 0.125, 0.0).astype(BF16) for a in range(nh)]
        dif = lax.broadcasted_iota(jnp.int32, (tq, tk), 0) - lax.broadcasted_iota(jnp.int32, (tq, tk), 1)
        kr = lax.broadcasted_iota(jnp.int32, (tk, tk), 0)
        kc = lax.broadcasted_iota(jnp.int32, (tk, tk), 1)
        later =jnp.where(kr > kc, 1.0, 0.0).astype(BF16)
        oacc[...] = jnp.zeros_like(oacc)
        rc[...] = jnp.zeros_like(rc)

        def tile(kb, thr):
            rows = pl.ds(pl.multiple_of(kb * tk, tk), tk)
            ks = [k_ref[rows, ln] for ln in lanes]
            vs = [v_ref[rows, ln] for ln in lanes]
            qr = slice(0 if thr is None else thr, tq)
            rcv = [rc[a, qr, :] for a in range(nh)]
            zs = [_dot(qs[a][qr], ks[a // 2], NT) for a in range(nh)]
            bs, mbs = [], []
            for a in range(nh):
                b, m = _log_terms(zs[a])
                if thr is not None:
                    m = jnp.where(dif[qr] > thr, m, 0.0)
                bs.append(b)
                mbs.append(m.astype(BF16))
            rl = [_dot(mbs[a], later) for a in range(nh)]
            for a in range(nh):
                p = jnp.exp(bs[a] + (rl[a] + jnp.tile(rcv[a], (1, nrep))))
                if thr is not None:
                    p = jnp.where(dif[qr] > thr, p, 0.0)
                oacc[a, qr, :] += _dot(p.astype(BF16), vs[a // 2])
                rc[a, qr, :] = rcv[a] + (rl[a][:, 0:1] + mbs[a][:, 0:1].astype(F32))

        for d in reversed(range(ndiag)):
            tile(i * ndiag + d, d * tk)

        def live():
            top = rc[0]
            for a in range(1, nh):
                top = jnp.maximum(top, rc[a])
            return jnp.max(top) > EXP_UNDERFLOW

        def step(carry):
            kb, _ = carry
            tile(kb, None)
            return kb - 1, live()

        kb_end, _ = lax.while_loop(lambda cr: jnp.logical_and(cr[0] >= 0, cr[1]), step, (i * ndiag - 1, live()))
        first_ref[pl.program_id(0), i] = (kb_end + 1).astype(F32)
        for pp, ln in enumerate(lanes):
            o_ref[:, ln] = jnp.where(heads[0], oacc[2 * pp], oacc[2 * pp + 1])
            tot_ref[:, ln] = jnp.where(heads[0], rc[2 * pp], rc[2 * pp + 1])

        @pl.when(step_id == ngrp * nq - 1)
        def _():
            _Gather(g_in, g_out, *g_sems, swaps).finish()

    qspec = pl.BlockSpec((tq, wl), lambda p, i: (i, p))
    any_spec = pl.BlockSpec(memory_space=pl.ANY)
    res = pl.pallas_call(
        body, name=name, grid=(ngrp, nq),
        in_specs=[qspec,
                  pl.BlockSpec((s, wl), lambda p, i: (0, ngrp + p)),
                  pl.BlockSpec((s, wl), lambda p, i: (0, 2 * ngrp + p))] + [any_spec] * ng,
        out_specs=[qspec, qspec, pl.BlockSpec(memory_space=pltpu.SMEM)] + [any_spec] * ng,
        out_shape=[jax.ShapeDtypeStruct((s, DA), F32), jax.ShapeDtypeStruct((s, DA), F32),
                   jax.ShapeDtypeStruct((ngrp, nq), F32)]
        + [jax.ShapeDtypeStruct((NDEV,) + a.shape, a.dtype) for a in gather],
        scratch_shapes=[pltpu.VMEM((nh, tq, LANES), F32), pltpu.VMEM((nh, tq, LANES), F32)] + _Gather.scratch(ng),
        compiler_params=_params("arbitrary", "arbitrary"),
    )(qkv, qkv, qkv, *gather)
    return res[0], res[1], res[2], res[3:]


def _attn_bwd(qkv, do, tot, first, exchange, *, tq, tk, hp, name):
    s = qkv.shape[0]
    nrep = tk // LANES
    ndiag = tq // tk
    ne = len(exchange)
    nh, wl = 2 * hp, LANES * hp
    ngrp, nq = DA // wl, s // tq
    lanes = [slice(LANES * pp, LANES * (pp + 1)) for pp in range(hp)]

    def body(*refs):
        q_ref, k_ref, v_ref, do_ref, tot_ref, first_ref = refs[:6]
        e_in = refs[6:6 + ne]
        dq_ref, dk_ref, dv_ref = refs[6 + ne:9 + ne]
        e_out = refs[9 + ne:9 + 2 * ne]
        dqacc, rem, gc = refs[9 + 2 * ne:12 + 2 * ne]
        e_sems = refs[12 + 2 * ne:]
        i = pl.program_id(1)
        step_id = pl.program_id(0) * nq + i

        @pl.when(step_id == 0)
        def _():
            for cp in _chip_copies(e_in, e_out, *e_sems):
                cp.start()

        @pl.when(i == 0)
        def _():
            dk_ref[...] = jnp.zeros_like(dk_ref)
            dv_ref[...] = jnp.zeros_like(dv_ref)

        heads = _head_masks(tq)
        qs = [jnp.where(heads[a % 2], q_ref[:, lanes[a // 2]] * 0.125, 0.0).astype(BF16) for a in range(nh)]
        dob = [jnp.where(heads[a % 2], do_ref[:, lanes[a // 2]], 0.0).astype(BF16) for a in range(nh)]
        dif = lax.broadcasted_iota(jnp.int32, (tq, tk), 0) - lax.broadcasted_iota(jnp.int32, (tq, tk), 1)
        kr = lax.broadcasted_iota(jnp.int32, (tk, tk), 0)
        kc = lax.broadcasted_iota(jnp.int32, (tk, tk), 1)
        up_incl = jnp.where(kr <= kc, 1.0, 0.0).astype(BF16)
        up_strict = jnp.where(kr < kc, 1.0, 0.0).astype(BF16)
        dqacc[...] = jnp.zeros_like(dqacc)
        gc[...] = jnp.zeros_like(gc)
        for pp, ln in enumerate(lanes):
            totv = tot_ref[:, ln]
            swapped = pltpu.roll(totv, HD, axis=1)
            rem[2 * pp] = jnp.where(heads[0], totv, swapped)
            rem[2 * pp + 1] = jnp.where(heads[1], totv, swapped)

        def tile(kb, thr):
            rows = pl.ds(pl.multiple_of(kb * tk, tk), tk)
            ks = [k_ref[rows, ln] for ln in lanes]
            vs = [v_ref[rows, ln] for ln in lanes]
            qr = slice(0 if thr is None else thr, tq)
            remv = [rem[a, qr, :] for a in range(nh)]
            gcv = [gc[a, qr, :] for a in range(nh)]
            zs = [_dot(qs[a][qr], ks[a // 2], NT) for a in range(nh)]
            das = [_dot(dob[a][qr], vs[a // 2], NT) for a in range(nh)]
            bs, mbs = [], []
            for a in range(nh):
                b, m = _log_terms(zs[a])
                if thr is not None:
                    m = jnp.where(dif[qr] > thr, m, 0.0)
                bs.append(b)
                mbs.append(m.astype(BF16))
            pl_ = [_dot(mbs[a], up_incl) for a in range(nh)]
            ps, gs, gbs = [], [], []
            for a in range(nh):
                p = jnp.exp(bs[a] + (jnp.tile(remv[a], (1, nrep)) - pl_[a]))
                if thr is not None:
                    p = jnp.where(dif[qr] > thr, p, 0.0)
                g = p * das[a]
                ps.append(p.astype(BF16))
                gs.append(g)
                gbs.append(g.astype(BF16))
            cl = [_dot(gbs[a], up_strict) for a in range(nh)]
            dk_add = [jnp.zeros((tk, LANES), F32) for _ in range(hp)]
            dv_add = [jnp.zeros((tk, LANES), F32) for _ in range(hp)]
            for a in range(nh):
                dz = gs[a] - jnp.exp(bs[a]) * (gs[a] + (jnp.tile(gcv[a], (1, nrep)) + cl[a]))
                if thr is not None:
                    dz = jnp.where(dif[qr] > thr, dz, 0.0)
                dzb = dz.astype(BF16)
                dqacc[a, qr, :] += _dot(dzb, ks[a // 2])
                dk_add[a // 2] += _dot(dzb, qs[a][qr], TN)
                dv_add[a // 2] += _dot(ps[a], dob[a][qr], TN)
                rem[a, qr, :] = remv[a] - pl_[a][:, tk - 1:tk]
                gc[a, qr, :] = gcv[a] + (cl[a][:, tk - 1:tk] + gbs[a][:, tk - 1:tk].astype(F32))
            for pp, ln in enumerate(lanes):
                dk_ref[rows, ln] += dk_add[pp]
                dv_ref[rows, ln] += dv_add[pp]

        def step(kb, carry):
            tile(kb, None)
            return carry

        first_kb = first_ref[(pl.program_id(0) * first.shape[0]) // ngrp, i].astype(jnp.int32)
        lax.fori_loop(first_kb, i * ndiag, step, 0)
        for d in range(ndiag):
            tile(i * ndiag + d, d * tk)
        for pp, ln in enumerate(lanes):
            dq_ref[:, ln] = jnp.where(heads[0], dqacc[2 * pp], dqacc[2 * pp + 1]) * 0.125

        @pl.when(step_id == ngrp * nq - 1)
        def _():
            cps = _chip_copies(e_in, e_out, *e_sems)
            for cp in cps:
                cp.wait_recv()
            for cp in cps:
                cp.wait_send()

    qspec = pl.BlockSpec((tq, wl), lambda p, i: (i, p))
    full = pl.BlockSpec((s, wl), lambda p, i: (0, p))
    any_spec = pl.BlockSpec(memory_space=pl.ANY)
    out = jax.ShapeDtypeStruct((s, DA), F32)
    res = pl.pallas_call(
        body, name=name, grid=(ngrp, nq),
        in_specs=[qspec, pl.BlockSpec((s, wl), lambda p, i: (0, ngrp + p), pipeline_mode=pl.Buffered(1)),
                  pl.BlockSpec((s, wl), lambda p, i: (0, 2 * ngrp + p), pipeline_mode=pl.Buffered(1)), qspec, qspec,
                  pl.BlockSpec(memory_space=pltpu.SMEM)] + [any_spec] * ne,
        out_specs=[qspec, full, full] + [any_spec] * ne,
        out_shape=[out, out, out] + [jax.ShapeDtypeStruct((3,) + a.shape[1:], a.dtype) for a in exchange],
        scratch_shapes=[pltpu.VMEM((nh, tq, LANES), F32)] * 3
        + [pltpu.SemaphoreType.DMA((3 * ne,)), pltpu.SemaphoreType.DMA((3 * ne,))],
        compiler_params=_params("arbitrary", "arbitrary"),
    )(qkv, qkv, qkv, do, tot, first, *exchange)
    return res[0], res[1], res[2], res[3:]


def _shift_rows(v, k):
    return pltpu.roll(v, k % v.shape[0], axis=0)


def _pooled(u, uh, i, g, w, ts):
    halo = jnp.where(i > 0, uh, 0.0)
    ue = jnp.concatenate([halo, u], axis=0)
    acc, span = ue, 1
    while span < w:
        acc = acc + _shift_rows(acc, span)
        span *= 2
    tpos = i * ts + lax.broadcasted_iota(jnp.int32, (ts, 1), 0)
    cnt = jnp.minimum(tpos + 1, w).astype(F32)
    return acc[HALO:] / cnt - u


def _pool_mix(proj, o, pw, pb, ps, ag, bd, *, ts, name):
    s = proj.shape[0]
    hb = ts // HALO

    def body(u_ref, uh_ref, o_ref, pw_ref, pb_ref, ps_ref, ag_ref, bd_ref, mix_ref):
        i = pl.program_id(0)
        for g, w in enumerate(POOL_WINDOWS):
            cols = slice(g * LANES, (g + 1) * LANES)
            pooled = _pooled(u_ref[:, cols], uh_ref[:, cols], i, g, w, ts)
            yv = (_dot(pooled.astype(BF16), pw_ref[g]) + pb_ref[:, cols]) * ps_ref[:, cols]
            mix_ref[:, cols] = yv.astype(BF16)
        ov = o_ref[...]
        mix_ref[:, DP:] = (ov * _group_rsqrt(ov, bd_ref[...]) * ag_ref[...]).astype(BF16)

    return pl.pallas_call(
        body, name=name, grid=(s // ts,),
        in_specs=[pl.BlockSpec((ts, DP), lambda i: (i, 0)),
                  pl.BlockSpec((HALO, DP), lambda i: (jnp.maximum(i * hb - 1, 0), 0)),
                  pl.BlockSpec((ts, DA), lambda i: (i, 0)),
                  _full((4, LANES, LANES)), _vec(DP), _vec(DP), _vec(DA), _full((DA, DA))],
        out_specs=pl.BlockSpec((ts, D), lambda i: (i, 0)),
        out_shape=jax.ShapeDtypeStruct((s, D), BF16), compiler_params=_params("parallel"),
    )(proj, proj, o, pw, pb, ps, ag, bd)


CF = DFF // 2
MXU_COLS = 256


def _sigmoid(t):
    return 0.5 + 0.5 * jnp.tanh(0.5 * t)


def _sub_chunks(width):
    return [(c0, min(MXU_COLS, width - c0)) for c0 in range(0, width, MXU_COLS)]


def _up_conv_gate(h2, w_up, cw, cb, *, tm, name):
    s = h2.shape[0]
    hb = tm // HALO

    def body(a_ref, ah_ref, w_ref, cw_ref, cb_ref, up_ref, c_ref, act_ref):
        i = pl.program_id(1)
        halo = jnp.where(i > 0, ah_ref[...], jnp.zeros_like(ah_ref))
        ext = jnp.concatenate([halo, a_ref[...]], axis=0)
        for c0, cwid in _sub_chunks(CF):
            conv = []
            for off in (c0, CF + c0):
                cols = slice(off, off + cwid)
                ue = _dot(ext, w_ref[cols, :], NT)
                up_ref[:, cols] = ue[HALO:].astype(BF16)
                y = cw_ref[2:3, cols] * ue + cw_ref[1:2, cols] * _shift_rows(ue, 1) + cw_ref[0:1, cols] * _shift_rows(ue, 2)
                cv = y[HALO:] + cb_ref[:, cols]
                c_ref[:, cols] = cv.astype(BF16)
                conv.append(cv)
            gt, vl = conv
            act_ref[:, c0:c0 + cwid] = (gt * _sigmoid(gt) * vl).astype(BF16)

    return pl.pallas_call(
        body, name=name, grid=(2, s // tm),
        in_specs=[pl.BlockSpec((tm, D), lambda j, i: (i, 0)),
                  pl.BlockSpec((HALO, D), lambda j, i: (jnp.maximum(i * hb - 1, 0), 0)),
                  pl.BlockSpec((2 * CF, D), lambda j, i: (j, 0)),
                  pl.BlockSpec((3, 2 * CF), lambda j, i: (0, j)), pl.BlockSpec((1, 2 * CF), lambda j, i: (0, j))],
        out_specs=[pl.BlockSpec((tm, 2 * CF), lambda j, i: (i, j)), pl.BlockSpec((tm, 2 * CF), lambda j, i: (i, j)),
                   pl.BlockSpec((tm, CF), lambda j, i: (i, j))],
        out_shape=[jax.ShapeDtypeStruct((s, 2 * DFF), BF16), jax.ShapeDtypeStruct((s, 2 * DFF), BF16),
                   jax.ShapeDtypeStruct((s, DFF), BF16)],
        compiler_params=_params("parallel", "parallel"),
    )(h2, h2, w_up, cw, cb)


def _down_bwd_gate(dffn, w_down, conv, *, tm, name):
    s = dffn.shape[0]

    def body(a_ref, w_ref, c_ref, d_ref, db_ref):
        i = pl.program_id(1)

        @pl.when(i == 0)
        def _():
            db_ref[...] = jnp.zeros_like(db_ref)

        a = a_ref[...]
        for c0, cwid in _sub_chunks(CF):
            gcols, vcols = slice(c0, c0 + cwid), slice(CF + c0, CF + c0 + cwid)
            da = _dot(a, w_ref[gcols, :], NT)
            gt, vl = c_ref[:, gcols].astype(F32), c_ref[:, vcols].astype(F32)
            sg = _sigmoid(gt)
            dgt = da * vl * (sg * (1.0 + gt * (1.0 - sg)))
            dvl = da * (gt * sg)
            d_ref[:, gcols] = dgt.astype(BF16)
            d_ref[:, vcols] = dvl.astype(BF16)
            db_ref[:, gcols] += _colsum8(dgt)
            db_ref[:, vcols] += _colsum8(dvl)

    return pl.pallas_call(
        body, name=name, grid=(2, s // tm),
        in_specs=[pl.BlockSpec((tm, D), lambda j, i: (i, 0)), pl.BlockSpec((CF, D), lambda j, i: (j, 0)),
                  pl.BlockSpec((tm, 2 * CF), lambda j, i: (i, j))],
        out_specs=[pl.BlockSpec((tm, 2 * CF), lambda j, i: (i, j)), pl.BlockSpec((SUBLANES, 2 * CF), lambda j, i: (0, j))],
        out_shape=[jax.ShapeDtypeStruct((s, 2 * DFF), BF16), jax.ShapeDtypeStruct((SUBLANES, 2 * DFF), F32)],
        compiler_params=_params("parallel", "arbitrary"),
    )(dffn, w_down, conv)


def _conv_bwd_up_bwd(dc, up, cw, w_up, *, tm, name):
    s = up.shape[0]
    hb = tm // HALO
    nb = s // HALO
    nk = 2 * DFF // CF
    n = s // tm

    def body(d_ref, dn_ref, u_ref, cw_ref, w_ref, du_ref, dh_ref, dw_ref, acc, dwacc):
        i, k = pl.program_id(0), pl.program_id(1)

        @pl.when(jnp.logical_and(i == 0, k == 0))
        def _():
            dwacc[...] = jnp.zeros_like(dwacc)

        @pl.when(k == 0)
        def _():
            acc[...] = jnp.zeros_like(acc)

        live_next = i < n - 1
        part = None
        for c0, cwid in _sub_chunks(CF):
            cols = slice(c0, c0 + cwid)
            dcur = d_ref[:, cols].astype(F32)
            de = jnp.concatenate([dcur, jnp.where(live_next, dn_ref[:, cols].astype(F32), 0.0)], axis=0)
            d1 = _shift_rows(de, -1)[:tm]
            d2 = _shift_rows(de, -2)[:tm]
            du = (cw_ref[2:3, cols] * dcur + cw_ref[1:2, cols] * d1 + cw_ref[0:1, cols] * d2).astype(BF16)
            du_ref[:, cols] = du
            prod = _dot(du, w_ref[cols, :])
            part = prod if part is None else part + prod
            u = u_ref[:, cols].astype(F32)
            for tap, dsh in ((2, dcur), (1, d1), (0, d2)):
                dwacc[k, SUBLANES * tap:SUBLANES * (tap + 1), cols] += _colsum8(dsh * u)
        acc[...] += part

        @pl.when(k == nk - 1)
        def _():
            dh_ref[...] = acc[...].astype(dh_ref.dtype)

        @pl.when(jnp.logical_and(i == n - 1, k == nk - 1))
        def _():
            dw_ref[...] = dwacc[...]

    res = pl.pallas_call(
        body, name=name, grid=(n, nk),
        in_specs=[pl.BlockSpec((tm, CF), lambda i, k: (i, k)),
                  pl.BlockSpec((HALO, CF), lambda i, k: (jnp.minimum((i + 1) * hb, nb - 1), k)),
                  pl.BlockSpec((tm, CF), lambda i, k: (i, k)),
                  pl.BlockSpec((3, CF), lambda i, k: (0, k)),
                  pl.BlockSpec((CF, D), lambda i, k: (k, 0))],
        out_specs=[pl.BlockSpec((tm, CF), lambda i, k: (i, k)), pl.BlockSpec((tm, D), lambda i, k: (i, 0)),
                   _full((nk, 24, CF))],
        out_shape=[jax.ShapeDtypeStruct((s, 2 * DFF), BF16), jax.ShapeDtypeStruct((s, D), BF16),
                   jax.ShapeDtypeStruct((nk, 24, CF), F32)],
        scratch_shapes=[pltpu.VMEM((tm, D), F32), pltpu.VMEM((nk, 24, CF), F32)],
        compiler_params=_params("arbitrary", "arbitrary"),
    )(dc, dc, up, cw, w_up)
    return res[0], res[1], jnp.transpose(res[2], (1, 0, 2)).reshape(24, 2 * DFF)


def _ln_mod_bwd(dh, xin, g, scale, resid, extra, gate, *, ts, name, side=None):
    s = xin.shape[0]
    row = pl.BlockSpec((ts, D), lambda i: (i, 0))
    acc8 = pl.BlockSpec((SUBLANES, D), lambda i: (0, 0))
    with_gate = extra is not None

    def body(*refs):
        if with_gate:
            dh_ref, x_ref, g_ref, sc_ref, r_ref, e_ref, gt_ref, dx_ref, da_ref, dsh, dsc, dg, dgt = refs
        else:
            dh_ref, x_ref, g_ref, sc_ref, r_ref, dx_ref, dsh, dsc, dg = refs
        i = pl.program_id(0)

        @pl.when(i == 0)
        def _():
            for acc in (dsh, dsc, dg) + ((dgt,) if with_gate else ()):
                acc[...] = jnp.zeros_like(acc)

        xv, dhv = x_ref[...], dh_ref[...].astype(F32)
        r = lax.rsqrt(jnp.mean(xv * xv, axis=-1, keepdims=True) + EPS)
        xn = xv * r
        dsh[...] += _colsum8(dhv)
        dsc[...] += _colsum8(dhv * (xn * g_ref[...]))
        dhp = dhv * (1.0 + sc_ref[...])
        dg[...] += _colsum8(dhp * xn)
        dxn = dhp * g_ref[...]
        dx = r_ref[...] + r * (dxn - xn * jnp.mean(dxn * xn, axis=-1, keepdims=True))
        dx_ref[...] = dx
        if with_gate:
            da_ref[...] = (dx * gt_ref[...]).astype(BF16)
            dgt[...] += _colsum8(dx * e_ref[...].astype(F32))

    f32o, p8 = jax.ShapeDtypeStruct((s, D), F32), jax.ShapeDtypeStruct((SUBLANES, D), F32)
    if with_gate:
        ins, in_specs = (dh, xin, g, scale, resid, extra, gate), [row, row, _vec(D), _vec(D), row, row, _vec(D)]
        out_specs, out_shape = [row, row, acc8, acc8, acc8, acc8], [f32o, jax.ShapeDtypeStruct((s, D), BF16), p8, p8, p8, p8]
    else:
        ins, in_specs = (dh, xin, g, scale, resid), [row, row, _vec(D), _vec(D), row]
        out_specs, out_shape = [row, acc8, acc8, acc8], [f32o, p8, p8, p8]
    return _row_call(body, side, name=name, steps=s // ts, in_specs=in_specs, out_specs=out_specs,
                     out_shape=out_shape, ins=ins)


def _group_norm_bwd(t, dn_out, gvec, bd):
    r = _group_rsqrt(t, bd)
    dg_terms = dn_out * t * r
    dn = dn_out * gvec
    dt = r * (dn - t * (r * r) * (_split_dot(dn * t, bd) * (1.0 / HD)))
    return dt, dg_terms


def _mix_bwd(dmix, proj, o, pw, pb, ps, ag, bd, *, ts, name, side=None):
    s = proj.shape[0]
    hb = ts // HALO
    nb = s // HALO

    def body(dm_ref, dmn_ref, u_ref, uh_ref, o_ref, pw_ref, pb_ref, ps_ref, ag_ref, bd_ref,
             du_ref, do_ref, dpw_ref, dpb_ref, dps_ref, dag_ref):
        i = pl.program_id(0)
        n = s // ts

        @pl.when(i == 0)
        def _():
            for acc in (dpw_ref, dpb_ref, dps_ref, dag_ref):
                acc[...] = jnp.zeros_like(acc)

        for g, w in enumerate(POOL_WINDOWS):
            cols = slice(g * LANES, (g + 1) * LANES)
            wg = pw_ref[g]
            psg = ps_ref[:, cols]
            pooled = _pooled(u_ref[:, cols], uh_ref[:, cols], i, g, w, ts).astype(BF16)
            dy = dm_ref[:, cols].astype(F32)
            dps_ref[:, cols] += _colsum8(dy * (_dot(pooled, wg) + pb_ref[:, cols]))
            dpre = dy * psg
            dpb_ref[:, cols] += _colsum8(dpre)
            dpreb = dpre.astype(BF16)
            dpw_ref[g * LANES:(g + 1) * LANES, :] += _dot(pooled, dpreb, TN)
            dpool = _dot(dpreb, wg, NT)
            dnext = _dot((dmn_ref[:, cols].astype(F32) * psg).astype(BF16), wg, NT)
            dpe = jnp.concatenate([dpool, jnp.where(i < n - 1, dnext, 0.0)], axis=0)
            tpos = i * ts + lax.broadcasted_iota(jnp.int32, (ts + HALO, 1), 0)
            acc = dpe / jnp.minimum(tpos + 1, w).astype(F32)
            span = 1
            while span < w:
                acc = acc + _shift_rows(acc, -span)
                span *= 2
            du_ref[:, cols] = acc[:ts] - dpool
        ov = o_ref[...]
        dov, dg_terms = _group_norm_bwd(ov, dm_ref[:, DP:].astype(F32), ag_ref[...], bd_ref[...])
        do_ref[...] = dov
        dag_ref[...] += _colsum8(dg_terms)

    p8 = jax.ShapeDtypeStruct((SUBLANES, DP), F32)
    acc8 = pl.BlockSpec((SUBLANES, DP), lambda i: (0, 0))
    half = pl.BlockSpec((ts, DP), lambda i: (i, 0))
    return _row_call(
        body, side, name=name, steps=s // ts,
        in_specs=[pl.BlockSpec((ts, D), lambda i: (i, 0)),
                  pl.BlockSpec((HALO, DP), lambda i: (jnp.minimum((i + 1) * hb, nb - 1), 0)),
                  half, pl.BlockSpec((HALO, DP), lambda i: (jnp.maximum(i * hb - 1, 0), 0)),
                  half, _full((4, LANES, LANES)), _vec(DP), _vec(DP), _vec(DA), _full((DA, DA))],
        out_specs=[half, half, _full((DP, LANES)), acc8, acc8, acc8],
        out_shape=[jax.ShapeDtypeStruct((s, DP), F32), jax.ShapeDtypeStruct((s, DA), F32),
                   jax.ShapeDtypeStruct((DP, LANES), F32), p8, p8, p8],
        ins=(dmix, dmix, proj, proj, o, pw, pb, ps, ag, bd))


def _qk_norm_bwd(du, dq, dk, dv, proj, qg, kg, bd, *, ts, name):
    s = proj.shape[0]

    def body(du_ref, dq_ref, dk_ref, dv_ref, q_ref, k_ref, qg_ref, kg_ref, bd_ref, dp_ref, dqg_ref, dkg_ref):
        i = pl.program_id(0)

        @pl.when(i == 0)
        def _():
            dqg_ref[...] = jnp.zeros_like(dqg_ref)
            dkg_ref[...] = jnp.zeros_like(dkg_ref)

        bdv = bd_ref[...]
        dqr, tq = _group_norm_bwd(q_ref[...], dq_ref[...], qg_ref[...], bdv)
        dkr, tk = _group_norm_bwd(k_ref[...], dk_ref[...], kg_ref[...], bdv)
        dqg_ref[...] += _colsum8(tq)
        dkg_ref[...] += _colsum8(tk)
        dp_ref[:, 0:DP] = du_ref[...].astype(BF16)
        dp_ref[:, DP:DP + DA] = dqr.astype(BF16)
        dp_ref[:, DP + DA:DP + 2 * DA] = dkr.astype(BF16)
        dp_ref[:, DP + 2 * DA:] = dv_ref[...].astype(BF16)

    half = pl.BlockSpec((ts, DA), lambda i: (i, 0))
    col = lambda j: pl.BlockSpec((ts, DA), lambda i: (i, j))
    acc8 = pl.BlockSpec((SUBLANES, DA), lambda i: (0, 0))
    p8 = jax.ShapeDtypeStruct((SUBLANES, DA), F32)
    return pl.pallas_call(
        body, name=name, grid=(s // ts,),
        in_specs=[half, half, half, half, col(1), col(2), _vec(DA), _vec(DA), _full((DA, DA))],
        out_specs=[pl.BlockSpec((ts, DIN), lambda i: (i, 0)), acc8, acc8],
        out_shape=[jax.ShapeDtypeStruct((s, DIN), BF16), p8, p8],
        compiler_params=_params("arbitrary"),
    )(du, dq, dk, dv, proj, proj, qg, kg, bd)


def _split3(a):
    hi = a.astype(BF16)
    return hi, (a - hi.astype(F32)).astype(BF16)


def _dot3(a, b, dn):
    ah, al = _split3(a)
    bh, bl = _split3(b)
    return _dot(ah, bh, dn) + (_dot(ah, bl, dn) + _dot(al, bh, dn))


def _ada_fwd(c_all, w, b, name):
    nw = w.shape[1]

    def body(c_ref, w_ref, b_ref, o_ref):
        cv = c_ref[...]
        act = cv / (1.0 + jnp.exp(-cv))
        o_ref[...] = _dot3(act, w_ref[...], NN) + b_ref[...]

    return pl.pallas_call(
        body, name=name, in_specs=[_full((NDEV, D)), _full(w.shape), _full((1, nw))], out_specs=_full((NDEV, nw)),
        out_shape=jax.ShapeDtypeStruct((NDEV, nw), F32), grid=(1,), compiler_params=_params("arbitrary"),
    )(c_all, w, b)


def _ada_bwd(c_all, dmod, name):
    nw = dmod.shape[1]

    def body(c_ref, d_ref, o_ref):
        cv = c_ref[...]
        act = cv / (1.0 + jnp.exp(-cv))
        o_ref[...] = _dot3(act, d_ref[...], TN)[None]

    return pl.pallas_call(
        body, name=name, in_specs=[_full((NDEV, D)), _full((NDEV, nw))], out_specs=_full((1, D, nw)),
        out_shape=jax.ShapeDtypeStruct((1, D, nw), F32), grid=(1,), compiler_params=_params("arbitrary"),
    )(c_all, dmod)


def _fold_heads(v):
    acc = v[:, 0:HD]
    for h in range(1, DA // HD):
        acc = acc + v[:, h * HD:(h + 1) * HD]
    return acc


def _pack_partials(pieces, dcw_p, name):
    n_p = len(pieces)
    total = sum(p.shape[1] for p in pieces) + 3 * dcw_p.shape[1]
    npack = -(-total // (SUBLANES * LANES)) * (SUBLANES * LANES)

    def body(*refs):
        out = refs[-1]
        off = 0
        for r in refs[:n_p]:
            out[:, off:off + r.shape[1]] = jnp.sum(r[...], axis=0, keepdims=True)
            off += r.shape[1]
        dw = refs[n_p]
        for tap in range(3):
            out[:, off:off + dw.shape[1]] = jnp.sum(dw[SUBLANES * tap:SUBLANES * (tap + 1), :], axis=0, keepdims=True)
            off += dw.shape[1]
        if off < npack:
            out[:, off:] = jnp.zeros((1, npack - off), F32)

    arrs = list(pieces) + [dcw_p]
    return pl.pallas_call(
        body, name=name, grid=(1,), in_specs=[_full(a.shape) for a in arrs], out_specs=_full((1, npack)),
        out_shape=jax.ShapeDtypeStruct((1, npack), F32), compiler_params=_params("arbitrary"),
    )(*arrs)


def _small_update(gathered, gathered_pw, gathered_cw, specs, params, loss_off, name):
    names = [sp[0] for sp in specs]
    flat = []
    for nme in names + ["pool_w", "conv_w"]:
        flat += list(params[nme])
    n_in = len(flat)

    def body(*refs):
        ga_ref, gp_ref, gc_ref = refs[0], refs[1], refs[2]
        prm = refs[3:3 + n_in]
        outs = refs[3 + n_in:]
        total = ga_ref[0:1, :]
        for dv in range(1, NDEV):
            total = total + ga_ref[dv:dv + 1, :]
        k = 0
        for idx, (nme, off, width, fold) in enumerate(specs):
            g = total[:, off:off + width]
            if fold:
                g = _fold_heads(g)
            w_ref, m_ref, v_ref = prm[3 * idx:3 * idx + 3]
            d, nm, nv = _adamw_math(w_ref[...], g, m_ref[...], v_ref[...])
            for val in (g, d, nm, nv):
                outs[k][...] = val
                k += 1
        gpw = gp_ref[0]
        for dv in range(1, NDEV):
            gpw = gpw + gp_ref[dv]
        w_ref, m_ref, v_ref = prm[3 * len(specs):3 * len(specs) + 3]
        d, nm, nv = _adamw_math(w_ref[...], gpw, m_ref[...], v_ref[...])
        for val in (gpw, d, nm, nv):
            outs[k][...] = val
            k += 1
        gcw = gc_ref[0]
        for dv in range(1, NDEV):
            gcw = gcw + gc_ref[dv]
        w_ref, m_ref, v_ref = prm[3 * len(specs) + 3:3 * len(specs) + 6]
        d, nm, nv = _adamw_math(w_ref[...], gcw, m_ref[...], v_ref[...])
        for val in (gcw, d, nm, nv):
            outs[k][...] = val
            k += 1
        outs[k][...] = ga_ref[:, 0:6 * D]
        outs[k + 1][...] = total[:, loss_off:loss_off + LANES] * (1.0 / SUBLANES)

    out_shape, out_specs = [], []
    for nme in names + ["pool_w", "conv_w"]:
        shp = params[nme][0].shape
        out_shape += [jax.ShapeDtypeStruct(shp, F32)] * 4
        out_specs += [_full(shp)] * 4
    out_shape += [jax.ShapeDtypeStruct((NDEV, 6 * D), F32), jax.ShapeDtypeStruct((1, LANES), F32)]
    out_specs += [_full((NDEV, 6 * D)), _full((1, LANES))]
    res = pl.pallas_call(
        body, name=name, grid=(1,),
        in_specs=[_full(gathered.shape), _full(gathered_pw.shape), _full(gathered_cw.shape)] + [_full(a.shape) for a in flat],
        out_specs=out_specs, out_shape=out_shape, compiler_params=_params("arbitrary"),
    )(gathered, gathered_pw, gathered_cw, *flat)
    out = {nme: tuple(res[4 * i:4 * i + 4]) for i, nme in enumerate(names + ["pool_w", "conv_w"])}
    return out, res[-2], res[-1][0, 0]


def _row_tile(s):
    return 512 if s % 512 == 0 else s


def kernel(x, c, ada_w, ada_b, norm1_g, w_in, pool_w, pool_b, pool_scale, q_norm_g, k_norm_g, attn_out_g, w_out, norm2_g, w_up, conv_w, conv_b, w_down, loss_target, m_ada_w, m_ada_b, m_norm1_g, m_w_in, m_pool_w, m_pool_b, m_pool_scale, m_q_norm_g, m_k_norm_g, m_attn_out_g, m_w_out, m_norm2_g, m_w_up, m_conv_w, m_conv_b, m_w_down, v_ada_w, v_ada_b, v_norm1_g, v_w_in, v_pool_w, v_pool_b, v_pool_scale, v_q_norm_g, v_k_norm_g, v_attn_out_g, v_w_out, v_norm2_g, v_w_up, v_conv_w, v_conv_b, v_w_down):
    ax, ay, ac = lax.axis_index("x"), lax.axis_index("y"), lax.axis_index("c")
    me = 4 * ax + 2 * ay + ac
    me_swapped = 4 * ay + 2 * ax + ac
    xs, tgt = x[0], loss_target[0]
    s = xs.shape[0]
    ts = _row_tile(s)
    tq_attn, tk_attn, hp_attn = 256, 256, 2
    tmm = 2 * ts
    bd = _block_diag_ones(DA, HD)

    w_in_t = w_in[0].T.astype(BF16)
    w_up_t = w_up[0].T.astype(BF16)
    c_all = _all_gather_small(jnp.broadcast_to(c, (SUBLANES, D)), "gather_c")[:, 0, :]
    n_ada = ada_w.shape[2]
    ada_b_mine = lax.dynamic_slice_in_dim(ada_b, me * n_ada, n_ada, axis=1)
    mod_part = _ada_fwd(c_all, ada_w[0], ada_b_mine, "ada_fwd")
    mod_all = _all_gather_small(mod_part, "gather_mod")
    mod = lax.dynamic_index_in_dim(mod_all, me, axis=1, keepdims=False).reshape(1, 6 * D)
    shift1, scale1, gate1, shift2, scale2, gate2 = [mod[:, k * D:(k + 1) * D] for k in range(6)]

    later_w = [w_out[0].astype(BF16), w_up_t, w_down[0].astype(BF16)]
    cb_full = jnp.transpose(conv_b.reshape(1, 2, 2, 2, 704), (0, 2, 1, 3, 4)).reshape(1, 2 * DFF)

    qg = jnp.tile(q_norm_g, (1, DA // HD))
    kg = jnp.tile(k_norm_g, (1, DA // HD))
    ag = attn_out_g.reshape(1, DA)
    pw = pool_w[0].astype(BF16)
    pb = pool_b.reshape(1, DP)
    h1, (gw_in, gcw) = _ln_mod(xs, norm1_g, scale1, shift1, ts=tmm, name="ln1",
                               side=_gather_side([w_in_t, jnp.pad(conv_w[0], ((0, 5), (0, 64)))], [False, True]))
    w_in_full = gw_in.reshape(DIN, D)
    cw_full = jnp.transpose(gcw[:, :3, :704], (1, 0, 2)).reshape(3, 2 * DFF)
    proj, qkv = _in_proj_qk_norm(h1, w_in_full, qg, kg, bd, tm=ts, name="in_proj_qk_norm")
    o_raw, m_tot, kb_first, (gw_out, gw_up, gw_down) = _attn_fwd(
        qkv, later_w, [False, True, False], tq=tq_attn, tk=tk_attn, hp=hp_attn, name="attn_fwd")
    w_out_full = gw_out.reshape(D, D)
    w_up_full = gw_up.reshape(2 * DFF, D)
    w_down_full = gw_down.reshape(DFF, D)
    mix = _pool_mix(proj, o_raw, pw, pb, pool_scale, ag, bd, ts=tmm, name="pool_mix")
    att, x1, h2 = _proj_res_ln_mod(mix, w_out_full, xs, gate1, norm2_g, scale2, shift2, tm=ts, name="out_proj_ln2")
    up, conv, act = _up_conv_gate(h2, w_up_full, cw_full, cb_full, tm=tmm, name="up_conv_gate")
    dy, dffn, dgate2_p, loss_p = _proj_loss_head(act, w_down_full, x1, tgt, gate2, tm=ts, name="down_proj_loss")

    g_w_down = _matmul(act, dffn, mode="tn", out_dtype=F32, tm=CF, tn=D, tk=2 * tmm, name="down_wgrad")
    dconv, dcb_p = _down_bwd_gate(dffn, w_down_full, conv, tm=tmm, name="down_bwd_gate")
    dup, dh2, dcw_p = _conv_bwd_up_bwd(dconv, up, cw_full, w_up_full, tm=ts, name="conv_bwd_up_bwd")
    g_w_up_t = _matmul(dup, h2, mode="tn", out_dtype=F32, tm=CF, tn=D, tk=2 * tmm, name="up_wgrad")
    (dx1, datt, dshift2_p, dscale2_p, dnorm2_p, dgate1_p), _ = _ln_mod_bwd(
        dh2, x1, norm2_g, scale2, dy, att, gate1, ts=ts, name="ln2_bwd")

    dmix = _matmul(datt, w_out_full, mode="nt", out_dtype=BF16, tm=tmm,tn=D, tk=D, name="out_bwd")
    g_w_out = _matmul(mix, datt, mode="tn", out_dtype=F32, tm=D, tn=D, tk=2 * tmm, name="out_wgrad")
    core = jnp.reshape(ac, (1,)).astype(jnp.int32)
    chip = jnp.reshape(2 * ax + ay, (1,)).astype(jnp.int32)
    big_ffn = [g_w_up_t.reshape(NDEV, 2 * DFF // NDEV, D), g_w_down.reshape(NDEV, DFF // NDEV, D),
               g_w_out.reshape(NDEV, D // NDEV, D)]
    swaps_ffn = [True, False, False]
    (du, do_raw, g_pw_p, dpb_p, dps_p, dag_p), gots_ffn = _mix_bwd(
        dmix, proj, o_raw, pw, pb, pool_scale, ag, bd, ts=tmm, name="mix_bwd", side=_pair_side(big_ffn, swaps_ffn))
    sums_ffn = [_pair_sum(big_ffn[k], gots_ffn[k], swaps_ffn[k], core, "rs_pair_sum_ffn%d" % k) for k in range(3)]
    dqn, dkn, dvv, parts_ffn = _attn_bwd(qkv, do_raw, m_tot, kb_first, sums_ffn, tq=tq_attn, tk=tk_attn, hp=hp_attn, name="attn_bwd")
    dproj, dqg_p, dkg_p = _qk_norm_bwd(du, dqn, dkn, dvv, proj, qg, kg, bd, ts=tmm, name="qk_norm_bwd")
    g_w_in_t = _matmul(dproj, h1, mode="tn", out_dtype=F32, tm=DIN // 2, tn=D, tk=2 * tmm, name="in_wgrad")
    big = [g_w_in_t.reshape(NDEV, DIN // NDEV, D)]
    gots = _pair_exchange(big, [False], "rs_pair")
    sums = [_pair_sum(big[0], gots[0], False, core, "rs_pair_sum")]
    dh1, parts = _matmul(dproj, w_in_full, mode="nn", out_dtype=BF16, tm=tmm,tn=D, tk=DIN, name="in_bwd",
                         side=_chip_side(sums))
    (grad_x, dshift1_p, dscale1_p, dnorm1_p), _ = _ln_mod_bwd(
        dh1, xs, norm1_g, scale1, dx1, None, None, ts=tmm, name="ln1_bwd")

    tr = lambda a: a[0].T
    r_in = _adamw_reduce(tr(w_in), tr(m_w_in), tr(v_w_in), sums[0], parts[0], chip, "adamw_w_in")
    r_out = _adamw_reduce(w_out[0], m_w_out[0], v_w_out[0], sums_ffn[2], parts_ffn[2], chip, "adamw_w_out")
    r_up = _adamw_reduce(tr(w_up), tr(m_w_up), tr(v_w_up), sums_ffn[0], parts_ffn[0], chip, "adamw_w_up")
    r_down = _adamw_reduce(w_down[0], m_w_down[0], v_w_down[0], sums_ffn[1], parts_ffn[1], chip, "adamw_w_down")
    r_in = [a.T[None] for a in r_in]
    r_up = [a.T[None] for a in r_up]
    r_out = [a[None] for a in r_out]
    r_down = [a[None] for a in r_down]

    dcb_nat = jnp.transpose(dcb_p.reshape(SUBLANES, 2, 2, 2, 704), (0, 2, 1, 3, 4)).reshape(SUBLANES, 2 * DFF)
    pieces = [dshift1_p, dscale1_p, dgate1_p, dshift2_p, dscale2_p, dgate2_p,
              dnorm1_p, dnorm2_p, dcb_nat, dpb_p, dps_p, dag_p, dqg_p, dkg_p, loss_p]
    n_vec = sum(p.shape[1] for p in pieces)
    packed = _pack_partials(pieces, dcw_p, "pack_partials")
    npack = packed.shape[1]
    gathered, gathered_pw = _all_gather([packed.reshape(SUBLANES, npack // SUBLANES), g_pw_p], [False, False], "gather_small")
    gathered = gathered.reshape(NDEV, npack)
    gathered_cw = lax.dynamic_index_in_dim(
        gathered[:, n_vec:n_vec + 6 * DFF].reshape(NDEV, 3, NDEV, 704), me_swapped, axis=2, keepdims=False)
    specs = [("ada_b", 0, 6 * D, False)]
    off = 6 * D
    for nme, width, fold in (("norm1_g", D, False), ("norm2_g", D, False), ("conv_b", 2 * DFF, False),
                             ("pool_b", DP, False), ("pool_scale", DP, False), ("attn_out_g", DA, False),
                             ("q_norm_g", DA, True), ("k_norm_g", DA, True)):
        specs.append((nme, off, width, fold))
        off += width
    small = {
        "ada_b": (ada_b, m_ada_b, v_ada_b),
        "norm1_g": (norm1_g, m_norm1_g, v_norm1_g), "norm2_g": (norm2_g, m_norm2_g, v_norm2_g),
        "conv_b": (conv_b, m_conv_b, v_conv_b),
        "pool_b": (pb, m_pool_b.reshape(1, DP), v_pool_b.reshape(1, DP)),
        "pool_scale": (pool_scale, m_pool_scale, v_pool_scale),
        "attn_out_g": (ag, m_attn_out_g.reshape(1, DA), v_attn_out_g.reshape(1, DA)),
        "q_norm_g": (q_norm_g, m_q_norm_g, v_q_norm_g), "k_norm_g": (k_norm_g, m_k_norm_g, v_k_norm_g),
        "pool_w": (pool_w.reshape(DP, LANES), m_pool_w.reshape(DP, LANES), v_pool_w.reshape(DP, LANES)),
        "conv_w": (conv_w[0], m_conv_w[0], v_conv_w[0]),
    }
    upd, dmod_all, loss = _small_update(gathered, gathered_pw, gathered_cw, specs, small, off, "small_update")
    g_ada_w = _ada_bwd(c_all, lax.dynamic_slice_in_dim(dmod_all, me * n_ada, n_ada, axis=1), "ada_bwd")
    r_ada = [g_ada_w] + [a[None] for a in _adamw(ada_w[0], m_ada_w[0], v_ada_w[0], g_ada_w[0], "adamw_ada_w")]

    shapes = {"ada_b": ada_b.shape, "norm1_g": norm1_g.shape, "pool_w": pool_w.shape, "pool_b": pool_b.shape,
              "pool_scale": pool_scale.shape, "q_norm_g": q_norm_g.shape, "k_norm_g": k_norm_g.shape,
              "attn_out_g": attn_out_g.shape, "norm2_g": norm2_g.shape, "conv_w": conv_w.shape, "conv_b": conv_b.shape}
    res = {nme: [a.reshape(shapes[nme]) for a in upd[nme]] for nme in shapes}
    res.update(ada_w=r_ada, w_in=r_in, w_out=r_out, w_up=r_up, w_down=r_down)
    names = ["ada_w", "ada_b", "norm1_g", "w_in", "pool_w", "pool_b", "pool_scale", "q_norm_g", "k_norm_g",
             "attn_out_g", "w_out", "norm2_g", "w_up", "conv_w", "conv_b", "w_down"]
    outs = [loss, grad_x[None]]
    for q in range(4):
        outs += [res[nme][q] for nme in names]
    return tuple(outs)
```

```python
import functools
import math

import numpy as np
import jax
import jax.numpy as jnp
from jax import lax
from jax.experimental import pallas as pl
from jax.experimental.pallas import tpu as pltpu

F32, BF16 = jnp.float32, jnp.bfloat16
D = 1024
DP = 512
DA = 512
HD = 64
DIN = DP + 3 * DA
DFF = 2816
POOL_WINDOWS = (2, 4, 8, 16)
HALO = 16
EPS = 1e-6
LANES = 128
SUBLANES = 8
NDEV = 8
VMEM_LIMIT = 56 * 1024 * 1024
MESH = pl.DeviceIdType.MESH

ADAM_LR, ADAM_B1, ADAM_B2, ADAM_EPS, ADAM_WD, ADAM_STEP = 0.001, 0.9, 0.999, 1e-08, 0.01, 10

NN = (((1,), (0,)), ((), ()))
NT = (((1,), (1,)), ((), ()))
TN = (((0,), (0,)), ((), ()))


def _params(*sem):
    return pltpu.CompilerParams(dimension_semantics=sem, vmem_limit_bytes=VMEM_LIMIT)


def _full(shape):
    nd = len(shape)
    return pl.BlockSpec(shape, lambda *_: (0,) * nd)


def _dot(a, b, dn=NN):
    return lax.dot_general(a, b, dn, preferred_element_type=F32)


def _split_dot(a, b, dn=NN):
    hi = a.astype(BF16)
    lo = (a - hi.astype(F32)).astype(BF16)
    return _dot(hi, b, dn) + _dot(lo, b, dn)


def _colsum8(v):
    r, n = v.shape
    return v.reshape(r // SUBLANES, SUBLANES, n).sum(axis=0)


def _block_diag_ones(n, blk):
    i = np.arange(n) // blk
    return jnp.asarray((i[:, None] == i[None, :]).astype(np.float32), BF16)


def _matmul(a, b, *, mode, out_dtype, tm, tn, tk, name, n_outer=False, side=None):
    if mode == "tn":
        K, M = a.shape
        N = b.shape[1]
    elif mode == "nt":
        M, K = a.shape
        N = b.shape[0]
    else:
        M, K = a.shape
        N = b.shape[1]
    tm, tn, tk = min(tm, M), min(tn, N), min(tk, K)
    assert M % tm == 0 and N % tn == 0 and K % tk == 0, (name, M, N, K, tm, tn, tk)
    nk = K // tk
    dn = {"nn": NN, "nt": NT, "tn": TN}[mode]

    def body(a_ref, b_ref, o_ref, *acc):
        if nk == 1:
            o_ref[...] = _dot(a_ref[...], b_ref[...], dn).astype(o_ref.dtype)
            return
        acc_ref, = acc
        k = pl.program_id(2)

        @pl.when(k == 0)
        def _():
            acc_ref[...] = jnp.zeros_like(acc_ref)

        acc_ref[...] += _dot(a_ref[...], b_ref[...], dn)

        @pl.when(k == nk - 1)
        def _():
            o_ref[...] = acc_ref[...].astype(o_ref.dtype)

    if n_outer:
        gi = lambda g: (g[1], g[0], g[2])
        grid = (N // tn, M // tm, nk)
    else:
        gi = lambda g: g
        grid = (M // tm, N // tn, nk)

    def amap(*g):
        i, j, k = gi(g)
        return (k, i) if mode == "tn" else (i, k)

    def bmap(*g):
        i, j, k = gi(g)
        return (j, k) if mode == "nt" else (k, j)

    def omap(*g):
        i, j, k = gi(g)
        return (i, j)

    a_blk = (tk, tm) if mode == "tn" else (tm, tk)
    b_blk = (tn, tk) if mode == "nt" else (tk, tn)
    acc_scratch = [] if nk == 1 else [pltpu.VMEM((tm, tn), F32)]
    if side is None:
        return pl.pallas_call(
            body, name=name, grid=grid,
            in_specs=[pl.BlockSpec(a_blk, amap), pl.BlockSpec(b_blk, bmap)],
            out_specs=pl.BlockSpec((tm, tn), omap),
            out_shape=jax.ShapeDtypeStruct((M, N), out_dtype),
            scratch_shapes=acc_scratch,
            compiler_params=_params("parallel", "parallel", "arbitrary"),
        )(a, b)

    ne = len(side.arrs)
    steps = grid[0] * grid[1] * grid[2]

    nsem = len(side.scratch)

    def with_side(*refs):
        e_in, e_out = refs[2:2 + ne], refs[3 + ne:3 + 2 * ne]
        sems = refs[len(refs) - nsem:]
        step = (pl.program_id(0) * grid[1] + pl.program_id(1)) * grid[2] + pl.program_id(2)

        @pl.when(step == 0)
        def _():
            side.start(e_in, e_out, *sems)

        body(refs[0], refs[1], refs[2 + ne], *refs[3 + 2 * ne:len(refs) - nsem])

        @pl.when(step == steps - 1)
        def _():
            side.finish(e_in, e_out, *sems)

    any_spec = pl.BlockSpec(memory_space=pl.ANY)
    res = pl.pallas_call(
        with_side, name=name, grid=grid,
        in_specs=[pl.BlockSpec(a_blk, amap), pl.BlockSpec(b_blk, bmap)] + [any_spec] * ne,
        out_specs=[pl.BlockSpec((tm, tn), omap)] + [any_spec] * ne,
        out_shape=[jax.ShapeDtypeStruct((M, N), out_dtype)] + side.out_shapes,
        scratch_shapes=acc_scratch + side.scratch,
        compiler_params=_params("arbitrary", "arbitrary", "arbitrary"),
    )(a, b, *side.arrs)
    return res[0], list(res[1:])


def _slot(swap, px, py, pc):
    return 4 * py + 2 * px + pc if swap else 4 * px + 2 * py + pc


class _Gather:
    def __init__(self, ins, outs, send, recv, loc, swaps):
        self.ins, self.outs, self.send, self.recv, self.loc, self.swaps = ins, outs, send, recv, loc, swaps
        x, y, c = lax.axis_index("x"), lax.axis_index("y"), lax.axis_index("c")
        self.me, self.sib = (x, y, c), (x, y, 1 - c)
        self.chips = [(1 - x, y), (x, 1 - y), (1 - x, 1 - y)]
        self.n = len(ins)

    @staticmethod
    def scratch(n):
        return [pltpu.SemaphoreType.DMA((7 * n,)), pltpu.SemaphoreType.DMA((7 * n,)), pltpu.SemaphoreType.DMA((n,))]

    def copy(self, a, k, blk, to, src=None):
        rows = self.outs[a].at[_slot(self.swaps[a], *blk)]
        return pltpu.make_async_remote_copy(
            src_ref=rows if src is None else src, dst_ref=rows,
            send_sem=self.send.at[7 * a + k], recv_sem=self.recv.at[7 * a + k], device_id=to, device_id_type=MESH)

    def mine(self, a):
        return pltpu.make_async_copy(self.ins[a], self.outs[a].at[_slot(self.swaps[a], *self.me)], self.loc.at[a])

    def first(self, a):
        c = self.me[2]
        return [self.copy(a, 0, self.me, self.sib, src=self.ins[a])] + [
            self.copy(a, 1 + j, self.me, (*chip, c), src=self.ins[a]) for j, chip in enumerate(self.chips)]

    def forwards(self, a):
        c = self.me[2]
        return [self.copy(a, 4 + j, (*chip, c), self.sib) for j, chip in enumerate(self.chips)]

    def start(self):
        for a in range(self.n):
            self.mine(a).start()
        for a in range(self.n):
            for cp in self.first(a):
                cp.start()

    def forward(self):
        c = self.me[2]
        for a in range(self.n):
            fwd = self.forwards(a)
            for j, chip in enumerate(self.chips):
                self.copy(a, 1 + j, (*chip, c), self.me).wait_recv()
                fwd[j].start()

    def finish(self):
        c = self.me[2]
        for a in range(self.n):
            self.copy(a, 0, self.sib, self.me).wait_recv()
            for j, chip in enumerate(self.chips):
                self.copy(a, 4 + j, (*chip, 1 - c), self.me).wait_recv()
        for a in range(self.n):
            for cp in self.first(a) + self.forwards(a):
                cp.wait_send()
            self.mine(a).wait()


def _all_gather(arrs, swaps, name):
    n = len(arrs)

    def body(*refs):
        g = _Gather(refs[:n], refs[n:2 * n], *refs[2 * n:], swaps)
        g.start()
        g.forward()
        g.finish()

    any_spec = pl.BlockSpec(memory_space=pl.ANY)
    return pl.pallas_call(
        body, name=name,
        in_specs=[any_spec] * n, out_specs=[any_spec] * n,
        out_shape=[jax.ShapeDtypeStruct((NDEV,) + a.shape, a.dtype) for a in arrs],
        scratch_shapes=_Gather.scratch(n),
    )(*arrs)


def _all_gather_small(arr, name):
    def body(in_ref, out_ref, send, recv, loc):
        x, y, c = lax.axis_index("x"), lax.axis_index("y"), lax.axis_index("c")
        flip = lambda v, bit: 1 - v if bit else v
        peers = [(flip(x, k >> 2 & 1), flip(y, k >> 1 & 1), flip(c, k & 1)) for k in range(1, NDEV)]
        mine = pltpu.make_async_copy(in_ref, out_ref.at[_slot(False, x, y, c)], loc)
        mine.start()

        def copy(k, src_dev, to):
            return pltpu.make_async_remote_copy(
                src_ref=in_ref, dst_ref=out_ref.at[_slot(False, *src_dev)], send_sem=send.at[k], recv_sem=recv.at[k],
                device_id=to, device_id_type=MESH)

        sends = [copy(k, (x, y, c), peer) for k, peer in enumerate(peers)]
        for cp in sends:
            cp.start()
        for k, peer in enumerate(peers):
            copy(k, peer, (x, y, c)).wait_recv()
        for cp in sends:
            cp.wait_send()
        mine.wait()

    any_spec = pl.BlockSpec(memory_space=pl.ANY)
    return pl.pallas_call(
        body, name=name, in_specs=[any_spec], out_specs=any_spec,
        out_shape=jax.ShapeDtypeStruct((NDEV,) + arr.shape, arr.dtype),
        scratch_shapes=[pltpu.SemaphoreType.DMA((NDEV - 1,)), pltpu.SemaphoreType.DMA((NDEV - 1,)), pltpu.SemaphoreType.DMA],
    )(arr)


def _pair_copies(ins, gots, send, recv, swaps):
    x, y, c = lax.axis_index("x"), lax.axis_index("y"), lax.axis_index("c")
    return [pltpu.make_async_remote_copy(
        src_ref=ins[a].at[_slot(swaps[a], k // 2, k % 2, 1 - c)], dst_ref=gots[a].at[k],
        send_sem=send.at[4 * a + k], recv_sem=recv.at[4 * a + k], device_id=(x, y, 1 - c), device_id_type=MESH)
        for a in range(len(ins)) for k in range(4)]


def _pair_exchange(arrs, swaps, name):
    n = len(arrs)

    def body(*refs):
        rems = _pair_copies(refs[:n], refs[n:2 * n], *refs[2 * n:], swaps)
        for rc in rems:
            rc.start()
        for rc in rems:
            rc.wait_recv()
        for rc in rems:
            rc.wait_send()

    any_spec = pl.BlockSpec(memory_space=pl.ANY)
    return pl.pallas_call(
        body, name=name,
        in_specs=[any_spec] * n, out_specs=[any_spec] * n,
        out_shape=[jax.ShapeDtypeStruct((4,) + a.shape[1:], a.dtype) for a in arrs],
        scratch_shapes=[pltpu.SemaphoreType.DMA((4 * n,)), pltpu.SemaphoreType.DMA((4 * n,))],
    )(*arrs)


def _chip_copies(ins, outs, send, recv):
    x, y, c = lax.axis_index("x"), lax.axis_index("y"), lax.axis_index("c")
    chips = [(1 - x, y), (x, 1 - y), (1 - x, 1 - y)]
    return [pltpu.make_async_remote_copy(
        src_ref=ins[a].at[2 * px + py], dst_ref=outs[a].at[j], send_sem=send.at[3 * a + j], recv_sem=recv.at[3 * a + j],
        device_id=(px, py, c), device_id_type=MESH) for a in range(len(ins)) for j, (px, py) in enumerate(chips)]


def _chip_exchange(arrs, name):
    n = len(arrs)

    def body(*refs):
        rems = _chip_copies(refs[:n], refs[n:2 * n], *refs[2 * n:])
        for rc in rems:
            rc.start()
        for rc in rems:
            rc.wait_recv()
        for rc in rems:
            rc.wait_send()

    any_spec = pl.BlockSpec(memory_space=pl.ANY)
    return pl.pallas_call(
        body, name=name,
        in_specs=[any_spec] * n, out_specs=[any_spec] * n,
        out_shape=[jax.ShapeDtypeStruct((3,) + a.shape[1:], a.dtype) for a in arrs],
        scratch_shapes=[pltpu.SemaphoreType.DMA((3 * n,)), pltpu.SemaphoreType.DMA((3 * n,))],
    )(*arrs)


class _Side:
    def __init__(self, arrs, out_shapes, scratch, start, finish, mid=None):
        self.arrs, self.out_shapes, self.scratch = list(arrs), list(out_shapes), list(scratch)
        self.start, self.finish, self.mid = start, finish, mid


def _copies_side(arrs, out_shapes, n_copies, make):
    def start(ins, outs, *sems):
        for cp in make(ins, outs, *sems):
            cp.start()

    def finish(ins, outs, *sems):
        cps = make(ins, outs, *sems)
        for cp in cps:
            cp.wait_recv()
        for cp in cps:
            cp.wait_send()

    return _Side(arrs, out_shapes, [pltpu.SemaphoreType.DMA((n_copies,)), pltpu.SemaphoreType.DMA((n_copies,))], start, finish)


def _pair_side(arrs, swaps):
    return _copies_side(arrs, [jax.ShapeDtypeStruct((4,) + a.shape[1:], a.dtype) for a in arrs], 4 * len(arrs),
                        functools.partial(_pair_copies, swaps=swaps))


def _chip_side(arrs):
    return _copies_side(arrs, [jax.ShapeDtypeStruct((3,) + a.shape[1:], a.dtype) for a in arrs], 3 * len(arrs), _chip_copies)


def _gather_side(arrs, swaps):
    return _Side(arrs, [jax.ShapeDtypeStruct((NDEV,) + a.shape, a.dtype) for a in arrs], _Gather.scratch(len(arrs)),
                 start=lambda ins, outs, *sems: _Gather(ins, outs, *sems, swaps).start(),
                 mid=lambda ins, outs, *sems: _Gather(ins, outs, *sems, swaps).forward(),
                 finish=lambda ins, outs, *sems: _Gather(ins, outs, *sems, swaps).finish())


def _row_call(body, side, *, name, steps, in_specs, out_specs, out_shape, ins):
    if side is None:
        res = pl.pallas_call(body, name=name, grid=(steps,), in_specs=in_specs, out_specs=out_specs, out_shape=out_shape,
                             compiler_params=_params("arbitrary"))(*ins)
        return list(res), []
    n_in, n_out, ne = len(in_specs), len(out_specs), len(side.arrs)

    def wrapped(*refs):
        e_in = refs[n_in:n_in + ne]
        e_out = refs[n_in + ne + n_out:n_in + 2 * ne + n_out]
        sems = refs[n_in + 2 * ne + n_out:]
        i = pl.program_id(0)

        @pl.when(i == 0)
        def _():
            side.start(e_in, e_out, *sems)

        if side.mid is not None:
            @pl.when(i == steps // 2)
            def _():
                side.mid(e_in, e_out, *sems)

        body(*refs[:n_in], *refs[n_in + ne:n_in + ne + n_out])

        @pl.when(i == steps - 1)
        def _():
            side.finish(e_in, e_out, *sems)

    any_spec = pl.BlockSpec(memory_space=pl.ANY)
    res = pl.pallas_call(
        wrapped, name=name, grid=(steps,), in_specs=list(in_specs) + [any_spec] * ne,
        out_specs=list(out_specs) + [any_spec] * ne, out_shape=list(out_shape) + side.out_shapes,
        scratch_shapes=side.scratch,
        compiler_params=_params("arbitrary"))(*ins, *side.arrs)
    return list(res[:n_out]), list(res[n_out:])


def _pair_sum(grads, got, swap, core, name):
    _, r, c = got.shape
    tr = r if r <= 352 else r // 2

    def own_map(k, i, core_ref):
        return (_slot(swap, k // 2, k % 2, core_ref[0]), i, 0)

    def body(core_ref, a_ref, b_ref, o_ref):
        o_ref[...] = a_ref[...] + b_ref[...]

    spec = pl.BlockSpec((None, tr, c), lambda k, i, core_ref: (k, i, 0))
    return pl.pallas_call(
        body, name=name,
        grid_spec=pltpu.PrefetchScalarGridSpec(
            num_scalar_prefetch=1, grid=(4, r // tr),
            in_specs=[pl.BlockSpec((None, tr, c), own_map), spec], out_specs=spec),
        out_shape=jax.ShapeDtypeStruct(got.shape, got.dtype), compiler_params=_params("parallel", "parallel"),
    )(core, grads, got)


def _adamw_math(w, g, m, v):
    m = ADAM_B1 * m + (1.0 - ADAM_B1) * g
    v = ADAM_B2 * v + (1.0 - ADAM_B2) * (g * g)
    m_hat = m / (1.0 - ADAM_B1 ** ADAM_STEP)
    v_hat = v / (1.0 - ADAM_B2 ** ADAM_STEP)
    delta = -ADAM_LR * (m_hat / (jnp.sqrt(v_hat) + ADAM_EPS) + ADAM_WD * w)
    return delta, m, v


def _adamw_tile(r):
    for cand in (256, 352, 128):
        if r % cand == 0:
            return cand
    return r


def _adamw(w, m, v, g, name):
    r, c = w.shape
    tr = _adamw_tile(r)
    spec = pl.BlockSpec((tr, c), lambda i: (i, 0))

    def body(w_ref, m_ref, v_ref, g_ref, d_ref, nm_ref, nv_ref):
        d_ref[...], nm_ref[...], nv_ref[...] = _adamw_math(w_ref[...], g_ref[...], m_ref[...], v_ref[...])

    out = jax.ShapeDtypeStruct((r, c), F32)
    return pl.pallas_call(
        body, name=name, grid=(r // tr,), in_specs=[spec] * 4, out_specs=[spec] * 3, out_shape=[out] * 3,
        compiler_params=_params("parallel"),
    )(w, m, v, g)


def _adamw_reduce(w, m, v, sums, recv, chip, name):
    r, c = w.shape
    tr = _adamw_tile(r)
    spec = pl.BlockSpec((tr, c), lambda i, chip_ref: (i, 0))

    def body(chip_ref, w_ref, m_ref, v_ref, s_ref, p_ref, g_ref, d_ref, nm_ref, nv_ref):
        g = ((s_ref[...] + p_ref[0]) + p_ref[1]) + p_ref[2]
        g_ref[...] = g
        d_ref[...], nm_ref[...], nv_ref[...] = _adamw_math(w_ref[...], g, m_ref[...], v_ref[...])

    out = jax.ShapeDtypeStruct((r, c), F32)
    return pl.pallas_call(
        body, name=name,
        grid_spec=pltpu.PrefetchScalarGridSpec(
            num_scalar_prefetch=1, grid=(r // tr,),
            in_specs=[spec, spec, spec, pl.BlockSpec((None, tr, c), lambda i, chip_ref: (chip_ref[0], i, 0)),
                      pl.BlockSpec((3, tr, c), lambda i, chip_ref: (0, i, 0))],
            out_specs=[spec] * 4),
        out_shape=[out] * 4, compiler_params=_params("parallel"),
    )(chip, w, m, v, sums, recv)


def _vec(n):
    return pl.BlockSpec((1, n), lambda *_: (0, 0))


def _ln_mod(x, g, scale, shift, *, ts, name, side=None):
    s = x.shape[0]
    row = pl.BlockSpec((ts, D), lambda i: (i, 0))

    def body(x_ref, g_ref, sc_ref, sh_ref, h_ref):
        xv = x_ref[...]
        r = lax.rsqrt(jnp.mean(xv * xv, axis=-1, keepdims=True) + EPS)
        h = (xv * r) * g_ref[...]
        h_ref[...] = (h * (1.0 + sc_ref[...]) + sh_ref[...]).astype(BF16)

    (h,), extra = _row_call(body, side, name=name, steps=s // ts, in_specs=[row, _vec(D), _vec(D), _vec(D)],
                            out_specs=[row], out_shape=[jax.ShapeDtypeStruct((s, D), BF16)], ins=(x, g, scale, shift))
    return h, extra


def _proj_res_ln_mod(mix, w, x, gate, g, scale, shift, *, tm, name):
    s = x.shape[0]
    row = pl.BlockSpec((tm, D), lambda i: (i, 0))

    def body(m_ref, w_ref, x_ref, gt_ref, g_ref, sc_ref, sh_ref, a_ref, x1_ref, h_ref):
        att = _dot(m_ref[...], w_ref[...])
        a_ref[...] = att.astype(BF16)
        x1 = x_ref[...] + gt_ref[...] * att
        x1_ref[...] = x1
        r = lax.rsqrt(jnp.mean(x1 * x1, axis=-1, keepdims=True) + EPS)
        h = (x1 * r) * g_ref[...]
        h_ref[...] = (h * (1.0 + sc_ref[...]) + sh_ref[...]).astype(BF16)

    return pl.pallas_call(
        body, name=name, grid=(s // tm,), in_specs=[row, _full(w.shape), row] + [_vec(D)] * 4, out_specs=[row, row, row],
        out_shape=[jax.ShapeDtypeStruct((s, D), BF16), jax.ShapeDtypeStruct((s, D), F32), jax.ShapeDtypeStruct((s, D), BF16)],
        compiler_params=_params("parallel"),
    )(mix, w, x, gate, g, scale, shift)


def _proj_loss_head(act, w, x1, tgt, gate2, *, tm, name):
    s = x1.shape[0]
    n = s // tm
    row = pl.BlockSpec((tm, D), lambda i: (i, 0))
    acc8 = pl.BlockSpec((SUBLANES, D), lambda i: (0, 0))

    def body(a_ref, w_ref, x_ref, t_ref, g_ref, dy_ref, df_ref, dg_ref, loss_ref, lacc):
        i = pl.program_id(0)

        @pl.when(i == 0)
        def _():
            lacc[...] = jnp.zeros_like(lacc)
            dg_ref[...] = jnp.zeros_like(dg_ref)

        f = _dot(a_ref[...], w_ref[...])
        diff = x_ref[...] + g_ref[...] * f - t_ref[...]
        lacc[...] += _colsum8(diff * diff)
        dy = diff * (1.0 / D)
        dy_ref[...] = dy
        df_ref[...] = (dy * g_ref[...]).astype(BF16)
        dg_ref[...] += _colsum8(dy * f)

        @pl.when(i == n - 1)
        def _():
            loss_ref[...] = jnp.full((SUBLANES, LANES), (0.5 / D) * jnp.sum(lacc[...]), F32)

    return pl.pallas_call(
        body, name=name, grid=(n,),
        in_specs=[pl.BlockSpec((tm, act.shape[1]), lambda i: (i, 0)), _full(w.shape), row, row, _vec(D)],
        out_specs=[row, row, acc8, _full((SUBLANES, LANES))],
        out_shape=[jax.ShapeDtypeStruct((s, D), F32), jax.ShapeDtypeStruct((s, D), BF16),
                   jax.ShapeDtypeStruct((SUBLANES, D), F32), jax.ShapeDtypeStruct((SUBLANES, LANES), F32)],
        scratch_shapes=[pltpu.VMEM((SUBLANES, D), F32)], compiler_params=_params("arbitrary"),
    )(act, w, x1, tgt, gate2)


def _group_rsqrt(t, bd):
    return lax.rsqrt(_split_dot(t * t, bd) * (1.0 / HD) + EPS)


def _in_proj_qk_norm(h, w, qg, kg, bd, *, tm, name):
    s = h.shape[0]

    def body(h_ref, w_ref, qg_ref, kg_ref, bd_ref, p_ref, o_ref):
        bdv = bd_ref[...]
        hv = h_ref[...]
        p_ref[:, 0:DP] = _dot(hv, w_ref[0:DP, :], NT)
        q = _dot(hv, w_ref[DP:DP + DA, :], NT)
        p_ref[:, DP:DP + DA] = q
        o_ref[:, 0:DA] = (q * _group_rsqrt(q, bdv) * qg_ref[...]).astype(BF16)
        k = _dot(hv, w_ref[DP + DA:DP + 2 * DA, :], NT)
        p_ref[:, DP + DA:DP + 2 * DA] = k
        o_ref[:, DA:2 * DA] = (k * _group_rsqrt(k, bdv) * kg_ref[...]).astype(BF16)
        v = _dot(hv, w_ref[DP + 2 * DA:, :], NT)
        p_ref[:, DP + 2 * DA:] = v
        o_ref[:, 2 * DA:] = v.astype(BF16)

    return pl.pallas_call(
        body, name=name, grid=(s // tm,),
        in_specs=[pl.BlockSpec((tm, D), lambda i: (i, 0)), _full(w.shape), _vec(DA), _vec(DA), _full((DA, DA))],
        out_specs=[pl.BlockSpec((tm, DIN), lambda i: (i, 0)), pl.BlockSpec((tm, 3 * DA), lambda i: (i, 0))],
        out_shape=[jax.ShapeDtypeStruct((s, DIN), F32), jax.ShapeDtypeStruct((s, 3 * DA), BF16)],
        compiler_params=_params("parallel"),
    )(h, w, qg, kg, bd)


EXP_UNDERFLOW = -120.0


def _log_terms(z):
    neg_abs = lax.bitcast_convert_type(lax.bitcast_convert_type(z, jnp.uint32) | jnp.uint32(0x80000000), F32)
    b = jnp.minimum(z, 0.0) - jnp.log(1.0 + jnp.exp(neg_abs))
    return b, b - z


def _head_masks(rows):
    lane = lax.broadcasted_iota(jnp.int32, (rows, LANES), 1)
    return [lane < HD, lane >= HD]


def _attn_fwd(qkv, gather, swaps, *, tq, tk, hp, name):
    s = qkv.shape[0]
    nrep = tk // LANES
    ndiag = tq // tk
    ng = len(gather)
    nh, wl = 2 * hp, LANES * hp
    ngrp, nq = DA // wl, s // tq
    lanes = [slice(LANES * pp, LANES * (pp + 1)) for pp in range(hp)]

    def body(*refs):
        q_ref, k_ref, v_ref = refs[:3]
        g_in = refs[3:3 + ng]
        o_ref, tot_ref, first_ref = refs[3 + ng:6 + ng]
        g_out = refs[6 + ng:6 + 2 * ng]
        oacc, rc = refs[6 + 2 * ng:8 + 2 * ng]
        g_sems = refs[8 + 2 * ng:]
        i = pl.program_id(1)
        step_id = pl.program_id(0) * nq + i

        @pl.when(step_id == 0)
        def _():
            _Gather(g_in, g_out, *g_sems, swaps).start()

        @pl.when(step_id == (ngrp * nq * 3) // 4)
        def _():
            _Gather(g_in, g_out, *g_sems, swaps).forward()

        heads = _head_masks(tq)
        qs = [jnp.where(heads[a % 2], q_ref[:, lanes[a // 2]] * 0.125, 0.0).astype(BF16) for a in range(nh)]
        dif = lax.broadcasted_iota(jnp.int32, (tq, tk), 0) - lax.broadcasted_iota(jnp.int32, (tq, tk), 1)
        kr = lax.broadcasted_iota(jnp.int32, (tk, tk), 0)
        kc = lax.broadcasted_iota(jnp.int32, (tk, tk), 1)
        later =jnp.where(kr > kc, 1.0, 0.0).astype(BF16)
        oacc[...] = jnp.zeros_like(oacc)
        rc[...] = jnp.zeros_like(rc)

        def tile(kb, thr):
            rows = pl.ds(pl.multiple_of(kb * tk, tk), tk)
            ks = [k_ref[rows, ln] for ln in lanes]
            vs = [v_ref[rows, ln] for ln in lanes]
            qr = slice(0 if thr is None else thr, tq)
            rcv = [rc[a, qr, :] for a in range(nh)]
            zs = [_dot(qs[a][qr], ks[a // 2], NT) for a in range(nh)]
            bs, mbs = [], []
            for a in range(nh):
                b, m = _log_terms(zs[a])
                if thr is not None:
                    m = jnp.where(dif[qr] > thr, m, 0.0)
                bs.append(b)
                mbs.append(m.astype(BF16))
            rl = [_dot(mbs[a], later) for a in range(nh)]
            for a in range(nh):
                p = jnp.exp(bs[a] + (rl[a] + jnp.tile(rcv[a], (1, nrep))))
                if thr is not None:
                    p = jnp.where(dif[qr] > thr, p, 0.0)
                oacc[a, qr, :] += _dot(p.astype(BF16), vs[a // 2])
                rc[a, qr, :] = rcv[a] + (rl[a][:, 0:1] + mbs[a][:, 0:1].astype(F32))

        for d in reversed(range(ndiag)):
            tile(i * ndiag + d, d * tk)

        def step(j, carry):
            tile(i * ndiag - 1 - j, None)
            return carry

        lax.fori_loop(0, i * ndiag, step, 0)
        first_ref[pl.program_id(0), i] = jnp.zeros((), F32)
        for pp, ln in enumerate(lanes):
            o_ref[:, ln] = jnp.where(heads[0], oacc[2 * pp], oacc[2 * pp + 1])
            tot_ref[:, ln] = jnp.where(heads[0], rc[2 * pp], rc[2 * pp + 1])

        @pl.when(step_id == ngrp * nq - 1)
        def _():
            _Gather(g_in, g_out, *g_sems, swaps).finish()

    qspec = pl.BlockSpec((tq, wl), lambda p, i: (i, p))
    any_spec = pl.BlockSpec(memory_space=pl.ANY)
    res = pl.pallas_call(
        body, name=name, grid=(ngrp, nq),
        in_specs=[qspec,
                  pl.BlockSpec((s, wl), lambda p, i: (0, ngrp + p)),
                  pl.BlockSpec((s, wl), lambda p, i: (0, 2 * ngrp + p))] + [any_spec] * ng,
        out_specs=[qspec, qspec, pl.BlockSpec(memory_space=pltpu.SMEM)] + [any_spec] * ng,
        out_shape=[jax.ShapeDtypeStruct((s, DA), F32), jax.ShapeDtypeStruct((s, DA), F32),
                   jax.ShapeDtypeStruct((ngrp, nq), F32)]
        + [jax.ShapeDtypeStruct((NDEV,) + a.shape, a.dtype) for a in gather],
        scratch_shapes=[pltpu.VMEM((nh, tq, LANES), F32), pltpu.VMEM((nh, tq, LANES), F32)] + _Gather.scratch(ng),
        compiler_params=_params("arbitrary", "arbitrary"),
    )(qkv, qkv, qkv, *gather)
    return res[0], res[1], res[2], res[3:]


def _attn_bwd(qkv, do, tot, first, exchange, *, tq, tk, hp, name):
    s = qkv.shape[0]
    nrep = tk // LANES
    ndiag = tq // tk
    ne = len(exchange)
    nh, wl = 2 * hp, LANES * hp
    ngrp, nq = DA // wl, s // tq
    lanes = [slice(LANES * pp, LANES * (pp + 1)) for pp in range(hp)]

    def body(*refs):
        q_ref, k_ref, v_ref, do_ref, tot_ref, first_ref = refs[:6]
        e_in = refs[6:6 + ne]
        dq_ref, dk_ref, dv_ref = refs[6 + ne:9 + ne]
        e_out = refs[9 + ne:9 + 2 * ne]
        dqacc, rem, gc = refs[9 + 2 * ne:12 + 2 * ne]
        e_sems = refs[12 + 2 * ne:]
        i = pl.program_id(1)
        step_id = pl.program_id(0) * nq + i

        @pl.when(step_id == 0)
        def _():
            for cp in _chip_copies(e_in, e_out, *e_sems):
                cp.start()

        @pl.when(i == 0)
        def _():
            dk_ref[...] = jnp.zeros_like(dk_ref)
            dv_ref[...] = jnp.zeros_like(dv_ref)

        heads = _head_masks(tq)
        qs = [jnp.where(heads[a % 2], q_ref[:, lanes[a // 2]] * 0.125, 0.0).astype(BF16) for a in range(nh)]
        dob = [jnp.where(heads[a % 2], do_ref[:, lanes[a // 2]], 0.0).astype(BF16) for a in range(nh)]
        dif = lax.broadcasted_iota(jnp.int32, (tq, tk), 0) - lax.broadcasted_iota(jnp.int32, (tq, tk), 1)
        kr = lax.broadcasted_iota(jnp.int32, (tk, tk), 0)
        kc = lax.broadcasted_iota(jnp.int32, (tk, tk), 1)
        up_incl = jnp.where(kr <= kc, 1.0, 0.0).astype(BF16)
        up_strict = jnp.where(kr < kc, 1.0, 0.0).astype(BF16)
        dqacc[...] = jnp.zeros_like(dqacc)
        gc[...] = jnp.zeros_like(gc)
        for pp, ln in enumerate(lanes):
            totv = tot_ref[:, ln]
            swapped = pltpu.roll(totv, HD, axis=1)
            rem[2 * pp] = jnp.where(heads[0], totv, swapped)
            rem[2 * pp + 1] = jnp.where(heads[1], totv, swapped)

        def tile(kb, thr):
            rows = pl.ds(pl.multiple_of(kb * tk, tk), tk)
            ks = [k_ref[rows, ln] for ln in lanes]
            vs = [v_ref[rows, ln] for ln in lanes]
            qr = slice(0 if thr is None else thr, tq)
            remv = [rem[a, qr, :] for a in range(nh)]
            gcv = [gc[a, qr, :] for a in range(nh)]
            zs = [_dot(qs[a][qr], ks[a // 2], NT) for a in range(nh)]
            das = [_dot(dob[a][qr], vs[a // 2], NT) for a in range(nh)]
            bs, mbs = [], []
            for a in range(nh):
                b, m = _log_terms(zs[a])
                if thr is not None:
                    m = jnp.where(dif[qr] > thr, m, 0.0)
                bs.append(b)
                mbs.append(m.astype(BF16))
            pl_ = [_dot(mbs[a], up_incl) for a in range(nh)]
            ps, gs, gbs = [], [], []
            for a in range(nh):
                p = jnp.exp(bs[a] + (jnp.tile(remv[a], (1, nrep)) - pl_[a]))
                if thr is not None:
                    p = jnp.where(dif[qr] > thr, p, 0.0)
                g = p * das[a]
                ps.append(p.astype(BF16))
                gs.append(g)
                gbs.append(g.astype(BF16))
            cl = [_dot(gbs[a], up_strict) for a in range(nh)]
            dk_add = [jnp.zeros((tk, LANES), F32) for _ in range(hp)]
            dv_add = [jnp.zeros((tk, LANES), F32) for _ in range(hp)]
            for a in range(nh):
                dz = gs[a] - jnp.exp(bs[a]) * (gs[a] + (jnp.tile(gcv[a], (1, nrep)) + cl[a]))
                if thr is not None:
                    dz = jnp.where(dif[qr] > thr, dz, 0.0)
                dzb = dz.astype(BF16)
                dqacc[a, qr, :] += _dot(dzb, ks[a // 2])
                dk_add[a // 2] += _dot(dzb, qs[a][qr], TN)
                dv_add[a // 2] += _dot(ps[a], dob[a][qr], TN)
                rem[a, qr, :] = remv[a] - pl_[a][:, tk - 1:tk]
                gc[a, qr, :] = gcv[a] + (cl[a][:, tk - 1:tk] + gbs[a][:, tk - 1:tk].astype(F32))
            for pp, ln in enumerate(lanes):
                dk_ref[rows, ln] += dk_add[pp]
                dv_ref[rows, ln] += dv_add[pp]

        def step(kb, carry):
            tile(kb, None)
            return carry

        lax.fori_loop(0, i * ndiag, step, 0)
        for d in range(ndiag):
            tile(i * ndiag + d, d * tk)
        for pp, ln in enumerate(lanes):
            dq_ref[:, ln] = jnp.where(heads[0], dqacc[2 * pp], dqacc[2 * pp + 1]) * 0.125

        @pl.when(step_id == ngrp * nq - 1)
        def _():
            cps = _chip_copies(e_in, e_out, *e_sems)
            for cp in cps:
                cp.wait_recv()
            for cp in cps:
                cp.wait_send()

    qspec = pl.BlockSpec((tq, wl), lambda p, i: (i, p))
    full = pl.BlockSpec((s, wl), lambda p, i: (0, p))
    any_spec = pl.BlockSpec(memory_space=pl.ANY)
    out = jax.ShapeDtypeStruct((s, DA), F32)
    res = pl.pallas_call(
        body, name=name, grid=(ngrp, nq),
        in_specs=[qspec, pl.BlockSpec((s, wl), lambda p, i: (0, ngrp + p), pipeline_mode=pl.Buffered(1)),
                  pl.BlockSpec((s, wl), lambda p, i: (0, 2 * ngrp + p), pipeline_mode=pl.Buffered(1)), qspec, qspec,
                  pl.BlockSpec(memory_space=pltpu.SMEM)] + [any_spec] * ne,
        out_specs=[qspec, full, full] + [any_spec] * ne,
        out_shape=[out, out, out] + [jax.ShapeDtypeStruct((3,) + a.shape[1:], a.dtype) for a in exchange],
        scratch_shapes=[pltpu.VMEM((nh, tq, LANES), F32)] * 3
        + [pltpu.SemaphoreType.DMA((3 * ne,)), pltpu.SemaphoreType.DMA((3 * ne,))],
        compiler_params=_params("arbitrary", "arbitrary"),
    )(qkv, qkv, qkv, do, tot, first, *exchange)
    return res[0], res[1], res[2], res[3:]


def _shift_rows(v, k):
    return pltpu.roll(v, k % v.shape[0], axis=0)


def _pooled(u, uh, i, g, w, ts):
    halo = jnp.where(i > 0, uh, 0.0)
    ue = jnp.concatenate([halo, u], axis=0)
    acc, span = ue, 1
    while span < w:
        acc = acc + _shift_rows(acc, span)
        span *= 2
    tpos = i * ts + lax.broadcasted_iota(jnp.int32, (ts, 1), 0)
    cnt = jnp.minimum(tpos + 1, w).astype(F32)
    return acc[HALO:] / cnt - u


def _pool_mix(proj, o, pw, pb, ps, ag, bd, *, ts, name):
    s = proj.shape[0]
    hb = ts // HALO

    def body(u_ref, uh_ref, o_ref, pw_ref, pb_ref, ps_ref, ag_ref, bd_ref, mix_ref):
        i = pl.program_id(0)
        for g, w in enumerate(POOL_WINDOWS):
            cols = slice(g * LANES, (g + 1) * LANES)
            pooled = _pooled(u_ref[:, cols], uh_ref[:, cols], i, g, w, ts)
            yv = (_dot(pooled.astype(BF16), pw_ref[g]) + pb_ref[:, cols]) * ps_ref[:, cols]
            mix_ref[:, cols] = yv.astype(BF16)
        ov = o_ref[...]
        mix_ref[:, DP:] = (ov * _group_rsqrt(ov, bd_ref[...]) * ag_ref[...]).astype(BF16)

    return pl.pallas_call(
        body, name=name, grid=(s // ts,),
        in_specs=[pl.BlockSpec((ts, DP), lambda i: (i, 0)),
                  pl.BlockSpec((HALO, DP), lambda i: (jnp.maximum(i * hb - 1, 0), 0)),
                  pl.BlockSpec((ts, DA), lambda i: (i, 0)),
                  _full((4, LANES, LANES)), _vec(DP), _vec(DP), _vec(DA), _full((DA, DA))],
        out_specs=pl.BlockSpec((ts, D), lambda i: (i, 0)),
        out_shape=jax.ShapeDtypeStruct((s, D), BF16), compiler_params=_params("parallel"),
    )(proj, proj, o, pw, pb, ps, ag, bd)


CF = DFF // 2
MXU_COLS = 256


def _sigmoid(t):
    return 0.5 + 0.5 * jnp.tanh(0.5 * t)


def _sub_chunks(width):
    return [(c0, min(MXU_COLS, width - c0)) for c0 in range(0, width, MXU_COLS)]


def _up_conv_gate(h2, w_up, cw, cb, *, tm, name):
    s = h2.shape[0]
    hb = tm // HALO

    def body(a_ref, ah_ref, w_ref, cw_ref, cb_ref, up_ref, c_ref, act_ref):
        i = pl.program_id(1)
        halo = jnp.where(i > 0, ah_ref[...], jnp.zeros_like(ah_ref))
        ext = jnp.concatenate([halo, a_ref[...]], axis=0)
        for c0, cwid in _sub_chunks(CF):
            conv = []
            for off in (c0, CF + c0):
                cols = slice(off, off + cwid)
                ue = _dot(ext, w_ref[cols, :], NT)
                up_ref[:, cols] = ue[HALO:].astype(BF16)
                y = cw_ref[2:3, cols] * ue + cw_ref[1:2, cols] * _shift_rows(ue, 1) + cw_ref[0:1, cols] * _shift_rows(ue, 2)
                cv = y[HALO:] + cb_ref[:, cols]
                c_ref[:, cols] = cv.astype(BF16)
                conv.append(cv)
            gt, vl = conv
            act_ref[:, c0:c0 + cwid] = (gt * _sigmoid(gt) * vl).astype(BF16)

    return pl.pallas_call(
        body, name=name, grid=(2, s // tm),
        in_specs=[pl.BlockSpec((tm, D), lambda j, i: (i, 0)),
                  pl.BlockSpec((HALO, D), lambda j, i: (jnp.maximum(i * hb - 1, 0), 0)),
                  pl.BlockSpec((2 * CF, D), lambda j, i: (j, 0)),
                  pl.BlockSpec((3, 2 * CF), lambda j, i: (0, j)), pl.BlockSpec((1, 2 * CF), lambda j, i: (0, j))],
        out_specs=[pl.BlockSpec((tm, 2 * CF), lambda j, i: (i, j)), pl.BlockSpec((tm, 2 * CF), lambda j, i: (i, j)),
                   pl.BlockSpec((tm, CF), lambda j, i: (i, j))],
        out_shape=[jax.ShapeDtypeStruct((s, 2 * DFF), BF16), jax.ShapeDtypeStruct((s, 2 * DFF), BF16),
                   jax.ShapeDtypeStruct((s, DFF), BF16)],
        compiler_params=_params("parallel", "parallel"),
    )(h2, h2, w_up, cw, cb)


def _down_bwd_gate(dffn, w_down, conv, *, tm, name):
    s = dffn.shape[0]

    def body(a_ref, w_ref, c_ref, d_ref, db_ref):
        i = pl.program_id(1)

        @pl.when(i == 0)
        def _():
            db_ref[...] = jnp.zeros_like(db_ref)

        a = a_ref[...]
        for c0, cwid in _sub_chunks(CF):
            gcols, vcols = slice(c0, c0 + cwid), slice(CF + c0, CF + c0 + cwid)
            da = _dot(a, w_ref[gcols, :], NT)
            gt, vl = c_ref[:, gcols].astype(F32), c_ref[:, vcols].astype(F32)
            sg = _sigmoid(gt)
            dgt = da * vl * (sg * (1.0 + gt * (1.0 - sg)))
            dvl = da * (gt * sg)
            d_ref[:, gcols] = dgt.astype(BF16)
            d_ref[:, vcols] = dvl.astype(BF16)
            db_ref[:, gcols] += _colsum8(dgt)
            db_ref[:, vcols] += _colsum8(dvl)

    return pl.pallas_call(
        body, name=name, grid=(2, s // tm),
        in_specs=[pl.BlockSpec((tm, D), lambda j, i: (i, 0)), pl.BlockSpec((CF, D), lambda j, i: (j, 0)),
                  pl.BlockSpec((tm, 2 * CF), lambda j, i: (i, j))],
        out_specs=[pl.BlockSpec((tm, 2 * CF), lambda j, i: (i, j)), pl.BlockSpec((SUBLANES, 2 * CF), lambda j, i: (0, j))],
        out_shape=[jax.ShapeDtypeStruct((s, 2 * DFF), BF16), jax.ShapeDtypeStruct((SUBLANES, 2 * DFF), F32)],
        compiler_params=_params("parallel", "arbitrary"),
    )(dffn, w_down, conv)


def _conv_bwd_up_bwd(dc, up, cw, w_up, *, tm, name):
    s = up.shape[0]
    hb = tm // HALO
    nb = s // HALO
    nk = 2 * DFF // CF
    n = s // tm

    def body(d_ref, dn_ref, u_ref, cw_ref, w_ref, du_ref, dh_ref, dw_ref, acc, dwacc):
        i, k = pl.program_id(0), pl.program_id(1)

        @pl.when(jnp.logical_and(i == 0, k == 0))
        def _():
            dwacc[...] = jnp.zeros_like(dwacc)

        @pl.when(k == 0)
        def _():
            acc[...] = jnp.zeros_like(acc)

        live_next = i < n - 1
        part = None
        for c0, cwid in _sub_chunks(CF):
            cols = slice(c0, c0 + cwid)
            dcur = d_ref[:, cols].astype(F32)
            de = jnp.concatenate([dcur, jnp.where(live_next, dn_ref[:, cols].astype(F32), 0.0)], axis=0)
            d1 = _shift_rows(de, -1)[:tm]
            d2 = _shift_rows(de, -2)[:tm]
            du = (cw_ref[2:3, cols] * dcur + cw_ref[1:2, cols] * d1 + cw_ref[0:1, cols] * d2).astype(BF16)
            du_ref[:, cols] = du
            prod = _dot(du, w_ref[cols, :])
            part = prod if part is None else part + prod
            u = u_ref[:, cols].astype(F32)
            for tap, dsh in ((2, dcur), (1, d1), (0, d2)):
                dwacc[k, SUBLANES * tap:SUBLANES * (tap + 1), cols] += _colsum8(dsh * u)
        acc[...] += part

        @pl.when(k == nk - 1)
        def _():
            dh_ref[...] = acc[...].astype(dh_ref.dtype)

        @pl.when(jnp.logical_and(i == n - 1, k == nk - 1))
        def _():
            dw_ref[...] = dwacc[...]

    res = pl.pallas_call(
        body, name=name, grid=(n, nk),
        in_specs=[pl.BlockSpec((tm, CF), lambda i, k: (i, k)),
                  pl.BlockSpec((HALO, CF), lambda i, k: (jnp.minimum((i + 1) * hb, nb - 1), k)),
                  pl.BlockSpec((tm, CF), lambda i, k: (i, k)),
                  pl.BlockSpec((3, CF), lambda i, k: (0, k)),
                  pl.BlockSpec((CF, D), lambda i, k: (k, 0))],
        out_specs=[pl.BlockSpec((tm, CF), lambda i, k: (i, k)), pl.BlockSpec((tm, D), lambda i, k: (i, 0)),
                   _full((nk, 24, CF))],
        out_shape=[jax.ShapeDtypeStruct((s, 2 * DFF), BF16), jax.ShapeDtypeStruct((s, D), BF16),
                   jax.ShapeDtypeStruct((nk, 24, CF), F32)],
        scratch_shapes=[pltpu.VMEM((tm, D), F32), pltpu.VMEM((nk, 24, CF), F32)],
        compiler_params=_params("arbitrary", "arbitrary"),
    )(dc, dc, up, cw, w_up)
    return res[0], res[1], jnp.transpose(res[2], (1, 0, 2)).reshape(24, 2 * DFF)


def _ln_mod_bwd(dh, xin, g, scale, resid, extra, gate, *, ts, name, side=None):
    s = xin.shape[0]
    row = pl.BlockSpec((ts, D), lambda i: (i, 0))
    acc8 = pl.BlockSpec((SUBLANES, D), lambda i: (0, 0))
    with_gate = extra is not None

    def body(*refs):
        if with_gate:
            dh_ref, x_ref, g_ref, sc_ref, r_ref, e_ref, gt_ref, dx_ref, da_ref, dsh, dsc, dg, dgt = refs
        else:
            dh_ref, x_ref, g_ref, sc_ref, r_ref, dx_ref, dsh, dsc, dg = refs
        i = pl.program_id(0)

        @pl.when(i == 0)
        def _():
            for acc in (dsh, dsc, dg) + ((dgt,) if with_gate else ()):
                acc[...] = jnp.zeros_like(acc)

        xv, dhv = x_ref[...], dh_ref[...].astype(F32)
        r = lax.rsqrt(jnp.mean(xv * xv, axis=-1, keepdims=True) + EPS)
        xn = xv * r
        dsh[...] += _colsum8(dhv)
        dsc[...] += _colsum8(dhv * (xn * g_ref[...]))
        dhp = dhv * (1.0 + sc_ref[...])
        dg[...] += _colsum8(dhp * xn)
        dxn = dhp * g_ref[...]
        dx = r_ref[...] + r * (dxn - xn * jnp.mean(dxn * xn, axis=-1, keepdims=True))
        dx_ref[...] = dx
        if with_gate:
            da_ref[...] = (dx * gt_ref[...]).astype(BF16)
            dgt[...] += _colsum8(dx * e_ref[...].astype(F32))

    f32o, p8 = jax.ShapeDtypeStruct((s, D), F32), jax.ShapeDtypeStruct((SUBLANES, D), F32)
    if with_gate:
        ins, in_specs = (dh, xin, g, scale, resid, extra, gate), [row, row, _vec(D), _vec(D), row, row, _vec(D)]
        out_specs, out_shape = [row, row, acc8, acc8, acc8, acc8], [f32o, jax.ShapeDtypeStruct((s, D), BF16), p8, p8, p8, p8]
    else:
        ins, in_specs = (dh, xin, g, scale, resid), [row, row, _vec(D), _vec(D), row]
        out_specs, out_shape = [row, acc8, acc8, acc8], [f32o, p8, p8, p8]
    return _row_call(body, side, name=name, steps=s // ts, in_specs=in_specs, out_specs=out_specs,
                     out_shape=out_shape, ins=ins)


def _group_norm_bwd(t, dn_out, gvec, bd):
    r = _group_rsqrt(t, bd)
    dg_terms = dn_out * t * r
    dn = dn_out * gvec
    dt = r * (dn - t * (r * r) * (_split_dot(dn * t, bd) * (1.0 / HD)))
    return dt, dg_terms


def _mix_bwd(dmix, proj, o, pw, pb, ps, ag, bd, *, ts, name, side=None):
    s = proj.shape[0]
    hb = ts // HALO
    nb = s // HALO

    def body(dm_ref, dmn_ref, u_ref, uh_ref, o_ref, pw_ref, pb_ref, ps_ref, ag_ref, bd_ref,
             du_ref, do_ref, dpw_ref, dpb_ref, dps_ref, dag_ref):
        i = pl.program_id(0)
        n = s // ts

        @pl.when(i == 0)
        def _():
            for acc in (dpw_ref, dpb_ref, dps_ref, dag_ref):
                acc[...] = jnp.zeros_like(acc)

        for g, w in enumerate(POOL_WINDOWS):
            cols = slice(g * LANES, (g + 1) * LANES)
            wg = pw_ref[g]
            psg = ps_ref[:, cols]
            pooled = _pooled(u_ref[:, cols], uh_ref[:, cols], i, g, w, ts).astype(BF16)
            dy = dm_ref[:, cols].astype(F32)
            dps_ref[:, cols] += _colsum8(dy * (_dot(pooled, wg) + pb_ref[:, cols]))
            dpre = dy * psg
            dpb_ref[:, cols] += _colsum8(dpre)
            dpreb = dpre.astype(BF16)
            dpw_ref[g * LANES:(g + 1) * LANES, :] += _dot(pooled, dpreb, TN)
            dpool = _dot(dpreb, wg, NT)
            dnext = _dot((dmn_ref[:, cols].astype(F32) * psg).astype(BF16), wg, NT)
            dpe = jnp.concatenate([dpool, jnp.where(i < n - 1, dnext, 0.0)], axis=0)
            tpos = i * ts + lax.broadcasted_iota(jnp.int32, (ts + HALO, 1), 0)
            acc = dpe / jnp.minimum(tpos + 1, w).astype(F32)
            span = 1
            while span < w:
                acc = acc + _shift_rows(acc, -span)
                span *= 2
            du_ref[:, cols] = acc[:ts] - dpool
        ov = o_ref[...]
        dov, dg_terms = _group_norm_bwd(ov, dm_ref[:, DP:].astype(F32), ag_ref[...], bd_ref[...])
        do_ref[...] = dov
        dag_ref[...] += _colsum8(dg_terms)

    p8 = jax.ShapeDtypeStruct((SUBLANES, DP), F32)
    acc8 = pl.BlockSpec((SUBLANES, DP), lambda i: (0, 0))
    half = pl.BlockSpec((ts, DP), lambda i: (i, 0))
    return _row_call(
        body, side, name=name, steps=s // ts,
        in_specs=[pl.BlockSpec((ts, D), lambda i: (i, 0)),
                  pl.BlockSpec((HALO, DP), lambda i: (jnp.minimum((i + 1) * hb, nb - 1), 0)),
                  half, pl.BlockSpec((HALO, DP), lambda i: (jnp.maximum(i * hb - 1, 0), 0)),
                  half, _full((4, LANES, LANES)), _vec(DP), _vec(DP), _vec(DA), _full((DA, DA))],
        out_specs=[half, half, _full((DP, LANES)), acc8, acc8, acc8],
        out_shape=[jax.ShapeDtypeStruct((s, DP), F32), jax.ShapeDtypeStruct((s, DA), F32),
                   jax.ShapeDtypeStruct((DP, LANES), F32), p8, p8, p8],
        ins=(dmix, dmix, proj, proj, o, pw, pb, ps, ag, bd))


def _qk_norm_bwd(du, dq, dk, dv, proj, qg, kg, bd, *, ts, name):
    s = proj.shape[0]

    def body(du_ref, dq_ref, dk_ref, dv_ref, q_ref, k_ref, qg_ref, kg_ref, bd_ref, dp_ref, dqg_ref, dkg_ref):
        i = pl.program_id(0)

        @pl.when(i == 0)
        def _():
            dqg_ref[...] = jnp.zeros_like(dqg_ref)
            dkg_ref[...] = jnp.zeros_like(dkg_ref)

        bdv = bd_ref[...]
        dqr, tq = _group_norm_bwd(q_ref[...], dq_ref[...], qg_ref[...], bdv)
        dkr, tk = _group_norm_bwd(k_ref[...], dk_ref[...], kg_ref[...], bdv)
        dqg_ref[...] += _colsum8(tq)
        dkg_ref[...] += _colsum8(tk)
        dp_ref[:, 0:DP] = du_ref[...].astype(BF16)
        dp_ref[:, DP:DP + DA] = dqr.astype(BF16)
        dp_ref[:, DP + DA:DP + 2 * DA] = dkr.astype(BF16)
        dp_ref[:, DP + 2 * DA:] = dv_ref[...].astype(BF16)

    half = pl.BlockSpec((ts, DA), lambda i: (i, 0))
    col = lambda j: pl.BlockSpec((ts, DA), lambda i: (i, j))
    acc8 = pl.BlockSpec((SUBLANES, DA), lambda i: (0, 0))
    p8 = jax.ShapeDtypeStruct((SUBLANES, DA), F32)
    return pl.pallas_call(
        body, name=name, grid=(s // ts,),
        in_specs=[half, half, half, half, col(1), col(2), _vec(DA), _vec(DA), _full((DA, DA))],
        out_specs=[pl.BlockSpec((ts, DIN), lambda i: (i, 0)), acc8, acc8],
        out_shape=[jax.ShapeDtypeStruct((s, DIN), BF16), p8, p8],
        compiler_params=_params("arbitrary"),
    )(du, dq, dk, dv, proj, proj, qg, kg, bd)


def _split3(a):
    hi = a.astype(BF16)
    return hi, (a - hi.astype(F32)).astype(BF16)


def _dot3(a, b, dn):
    ah, al = _split3(a)
    bh, bl = _split3(b)
    return _dot(ah, bh, dn) + (_dot(ah, bl, dn) + _dot(al, bh, dn))


def _ada_fwd(c_all, w, b, name):
    nw = w.shape[1]

    def body(c_ref, w_ref, b_ref, o_ref):
        cv = c_ref[...]
        act = cv / (1.0 + jnp.exp(-cv))
        o_ref[...] = _dot3(act, w_ref[...], NN) + b_ref[...]

    return pl.pallas_call(
        body, name=name, in_specs=[_full((NDEV, D)), _full(w.shape), _full((1, nw))], out_specs=_full((NDEV, nw)),
        out_shape=jax.ShapeDtypeStruct((NDEV, nw), F32), grid=(1,), compiler_params=_params("arbitrary"),
    )(c_all, w, b)


def _ada_bwd(c_all, dmod, name):
    nw = dmod.shape[1]

    def body(c_ref, d_ref, o_ref):
        cv = c_ref[...]
        act = cv / (1.0 + jnp.exp(-cv))
        o_ref[...] = _dot3(act, d_ref[...], TN)[None]

    return pl.pallas_call(
        body, name=name, in_specs=[_full((NDEV, D)), _full((NDEV, nw))], out_specs=_full((1, D, nw)),
        out_shape=jax.ShapeDtypeStruct((1, D, nw), F32), grid=(1,), compiler_params=_params("arbitrary"),
    )(c_all, dmod)


def _fold_heads(v):
    acc = v[:, 0:HD]
    for h in range(1, DA // HD):
        acc = acc + v[:, h * HD:(h + 1) * HD]
    return acc


def _pack_partials(pieces, dcw_p, name):
    n_p = len(pieces)
    total = sum(p.shape[1] for p in pieces) + 3 * dcw_p.shape[1]
    npack = -(-total // (SUBLANES * LANES)) * (SUBLANES * LANES)

    def body(*refs):
        out = refs[-1]
        off = 0
        for r in refs[:n_p]:
            out[:, off:off + r.shape[1]] = jnp.sum(r[...], axis=0, keepdims=True)
            off += r.shape[1]
        dw = refs[n_p]
        for tap in range(3):
            out[:, off:off + dw.shape[1]] = jnp.sum(dw[SUBLANES * tap:SUBLANES * (tap + 1), :], axis=0, keepdims=True)
            off += dw.shape[1]
        if off < npack:
            out[:, off:] = jnp.zeros((1, npack - off), F32)

    arrs = list(pieces) + [dcw_p]
    return pl.pallas_call(
        body, name=name, grid=(1,), in_specs=[_full(a.shape) for a in arrs], out_specs=_full((1, npack)),
        out_shape=jax.ShapeDtypeStruct((1, npack), F32), compiler_params=_params("arbitrary"),
    )(*arrs)


def _small_update(gathered, gathered_pw, gathered_cw, specs, params, loss_off, name):
    names = [sp[0] for sp in specs]
    flat = []
    for nme in names + ["pool_w", "conv_w"]:
        flat += list(params[nme])
    n_in = len(flat)

    def body(*refs):
        ga_ref, gp_ref, gc_ref = refs[0], refs[1], refs[2]
        prm = refs[3:3 + n_in]
        outs = refs[3 + n_in:]
        total = ga_ref[0:1, :]
        for dv in range(1, NDEV):
            total = total + ga_ref[dv:dv + 1, :]
        k = 0
        for idx, (nme, off, width, fold) in enumerate(specs):
            g = total[:, off:off + width]
            if fold:
                g = _fold_heads(g)
            w_ref, m_ref, v_ref = prm[3 * idx:3 * idx + 3]
            d, nm, nv = _adamw_math(w_ref[...], g, m_ref[...], v_ref[...])
            for val in (g, d, nm, nv):
                outs[k][...] = val
                k += 1
        gpw = gp_ref[0]
        for dv in range(1, NDEV):
            gpw = gpw + gp_ref[dv]
        w_ref, m_ref, v_ref = prm[3 * len(specs):3 * len(specs) + 3]
        d, nm, nv = _adamw_math(w_ref[...], gpw, m_ref[...], v_ref[...])
        for val in (gpw, d, nm, nv):
            outs[k][...] = val
            k += 1
        gcw = gc_ref[0]
        for dv in range(1, NDEV):
            gcw = gcw + gc_ref[dv]
        w_ref, m_ref, v_ref = prm[3 * len(specs) + 3:3 * len(specs) + 6]
        d, nm, nv = _adamw_math(w_ref[...], gcw, m_ref[...], v_ref[...])
        for val in (gcw, d, nm, nv):
            outs[k][...] = val
            k += 1
        outs[k][...] = ga_ref[:, 0:6 * D]
        outs[k + 1][...] = total[:, loss_off:loss_off + LANES] * (1.0 / SUBLANES)

    out_shape, out_specs = [], []
    for nme in names + ["pool_w", "conv_w"]:
        shp = params[nme][0].shape
        out_shape += [jax.ShapeDtypeStruct(shp, F32)] * 4
        out_specs += [_full(shp)] * 4
    out_shape += [jax.ShapeDtypeStruct((NDEV, 6 * D), F32), jax.ShapeDtypeStruct((1, LANES), F32)]
    out_specs += [_full((NDEV, 6 * D)), _full((1, LANES))]
    res = pl.pallas_call(
        body, name=name, grid=(1,),
        in_specs=[_full(gathered.shape), _full(gathered_pw.shape), _full(gathered_cw.shape)] + [_full(a.shape) for a in flat],
        out_specs=out_specs, out_shape=out_shape, compiler_params=_params("arbitrary"),
    )(gathered, gathered_pw, gathered_cw, *flat)
    out = {nme: tuple(res[4 * i:4 * i + 4]) for i, nme in enumerate(names + ["pool_w", "conv_w"])}
    return out, res[-2], res[-1][0, 0]


def _row_tile(s):
    return 512 if s % 512 == 0 else s


def kernel(x, c, ada_w, ada_b, norm1_g, w_in, pool_w, pool_b, pool_scale, q_norm_g, k_norm_g, attn_out_g, w_out, norm2_g, w_up, conv_w, conv_b, w_down, loss_target, m_ada_w, m_ada_b, m_norm1_g, m_w_in, m_pool_w, m_pool_b, m_pool_scale, m_q_norm_g, m_k_norm_g, m_attn_out_g, m_w_out, m_norm2_g, m_w_up, m_conv_w, m_conv_b, m_w_down, v_ada_w, v_ada_b, v_norm1_g, v_w_in, v_pool_w, v_pool_b, v_pool_scale, v_q_norm_g, v_k_norm_g, v_attn_out_g, v_w_out, v_norm2_g, v_w_up, v_conv_w, v_conv_b, v_w_down):
    ax, ay, ac = lax.axis_index("x"), lax.axis_index("y"), lax.axis_index("c")
    me = 4 * ax + 2 * ay + ac
    me_swapped = 4 * ay + 2 * ax + ac
    xs, tgt = x[0], loss_target[0]
    s = xs.shape[0]
    ts = _row_tile(s)
    tq_attn, tk_attn, hp_attn = 256, 256, 2
    tmm = 2 * ts
    bd = _block_diag_ones(DA, HD)

    w_in_t = w_in[0].T.astype(BF16)
    w_up_t = w_up[0].T.astype(BF16)
    c_all = _all_gather_small(jnp.broadcast_to(c, (SUBLANES, D)), "gather_c")[:, 0, :]
    n_ada = ada_w.shape[2]
    ada_b_mine = lax.dynamic_slice_in_dim(ada_b, me * n_ada, n_ada, axis=1)
    mod_part = _ada_fwd(c_all, ada_w[0], ada_b_mine, "ada_fwd")
    mod_all = _all_gather_small(mod_part, "gather_mod")
    mod = lax.dynamic_index_in_dim(mod_all, me, axis=1, keepdims=False).reshape(1, 6 * D)
    shift1, scale1, gate1, shift2, scale2, gate2 = [mod[:, k * D:(k + 1) * D] for k in range(6)]

    later_w = [w_out[0].astype(BF16), w_up_t, w_down[0].astype(BF16)]
    cb_full = jnp.transpose(conv_b.reshape(1, 2, 2, 2, 704), (0, 2, 1, 3, 4)).reshape(1, 2 * DFF)

    qg = jnp.tile(q_norm_g, (1, DA // HD))
    kg = jnp.tile(k_norm_g, (1, DA // HD))
    ag = attn_out_g.reshape(1, DA)
    pw = pool_w[0].astype(BF16)
    pb = pool_b.reshape(1, DP)
    h1, (gw_in, gcw) = _ln_mod(xs, norm1_g, scale1, shift1, ts=tmm, name="ln1",
                               side=_gather_side([w_in_t, jnp.pad(conv_w[0], ((0, 5), (0, 64)))], [False, True]))
    w_in_full = gw_in.reshape(DIN, D)
    cw_full = jnp.transpose(gcw[:, :3, :704], (1, 0, 2)).reshape(3, 2 * DFF)
    proj, qkv = _in_proj_qk_norm(h1, w_in_full, qg, kg, bd, tm=ts, name="in_proj_qk_norm")
    o_raw, m_tot, kb_first, (gw_out, gw_up, gw_down) = _attn_fwd(
        qkv, later_w, [False, True, False], tq=tq_attn, tk=tk_attn, hp=hp_attn, name="attn_fwd")
    w_out_full = gw_out.reshape(D, D)
    w_up_full = gw_up.reshape(2 * DFF, D)
    w_down_full = gw_down.reshape(DFF, D)
    mix = _pool_mix(proj, o_raw, pw, pb, pool_scale, ag, bd, ts=tmm, name="pool_mix")
    att, x1, h2 = _proj_res_ln_mod(mix, w_out_full, xs, gate1, norm2_g, scale2, shift2, tm=ts, name="out_proj_ln2")
    up, conv, act = _up_conv_gate(h2, w_up_full, cw_full, cb_full, tm=tmm, name="up_conv_gate")
    dy, dffn, dgate2_p, loss_p = _proj_loss_head(act, w_down_full, x1, tgt, gate2, tm=ts, name="down_proj_loss")

    g_w_down = _matmul(act, dffn, mode="tn", out_dtype=F32, tm=CF, tn=D, tk=2 * tmm, name="down_wgrad")
    dconv, dcb_p = _down_bwd_gate(dffn, w_down_full, conv, tm=tmm, name="down_bwd_gate")
    dup, dh2, dcw_p = _conv_bwd_up_bwd(dconv, up, cw_full, w_up_full, tm=ts, name="conv_bwd_up_bwd")
    g_w_up_t = _matmul(dup, h2, mode="tn", out_dtype=F32, tm=CF, tn=D, tk=2 * tmm, name="up_wgrad")
    (dx1, datt, dshift2_p, dscale2_p, dnorm2_p, dgate1_p), _ = _ln_mod_bwd(
        dh2, x1, norm2_g, scale2, dy, att, gate1, ts=ts, name="ln2_bwd")

    dmix = _matmul(datt, w_out_full, mode="nt", out_dtype=BF16, tm=tmm,tn=D, tk=D, name="out_bwd")
    g_w_out = _matmul(mix, datt, mode="tn", out_dtype=F32, tm=D, tn=D, tk=2 * tmm, name="out_wgrad")
    core = jnp.reshape(ac, (1,)).astype(jnp.int32)
    chip = jnp.reshape(2 * ax + ay, (1,)).astype(jnp.int32)
    big_ffn = [g_w_up_t.reshape(NDEV, 2 * DFF // NDEV, D), g_w_down.reshape(NDEV, DFF // NDEV, D),
               g_w_out.reshape(NDEV, D // NDEV, D)]
    swaps_ffn = [True, False, False]
    (du, do_raw, g_pw_p, dpb_p, dps_p, dag_p), gots_ffn = _mix_bwd(
        dmix, proj, o_raw, pw, pb, pool_scale, ag, bd, ts=tmm, name="mix_bwd", side=_pair_side(big_ffn, swaps_ffn))
    sums_ffn = [_pair_sum(big_ffn[k], gots_ffn[k], swaps_ffn[k], core, "rs_pair_sum_ffn%d" % k) for k in range(3)]
    dqn, dkn, dvv, parts_ffn = _attn_bwd(qkv, do_raw, m_tot, kb_first, sums_ffn, tq=tq_attn, tk=tk_attn, hp=hp_attn, name="attn_bwd")
    dproj, dqg_p, dkg_p = _qk_norm_bwd(du, dqn, dkn, dvv, proj, qg, kg, bd, ts=tmm, name="qk_norm_bwd")
    g_w_in_t = _matmul(dproj, h1, mode="tn", out_dtype=F32, tm=DIN // 2, tn=D, tk=2 * tmm, name="in_wgrad")
    big = [g_w_in_t.reshape(NDEV, DIN // NDEV, D)]
    gots = _pair_exchange(big, [False], "rs_pair")
    sums = [_pair_sum(big[0], gots[0], False, core, "rs_pair_sum")]
    dh1, parts = _matmul(dproj, w_in_full, mode="nn", out_dtype=BF16, tm=tmm,tn=D, tk=DIN, name="in_bwd",
                         side=_chip_side(sums))
    (grad_x, dshift1_p, dscale1_p, dnorm1_p), _ = _ln_mod_bwd(
        dh1, xs, norm1_g, scale1, dx1, None, None, ts=tmm, name="ln1_bwd")

    tr = lambda a: a[0].T
    r_in = _adamw_reduce(tr(w_in), tr(m_w_in), tr(v_w_in), sums[0], parts[0], chip, "adamw_w_in")
    r_out = _adamw_reduce(w_out[0], m_w_out[0], v_w_out[0], sums_ffn[2], parts_ffn[2], chip, "adamw_w_out")
    r_up = _adamw_reduce(tr(w_up), tr(m_w_up), tr(v_w_up), sums_ffn[0], parts_ffn[0], chip, "adamw_w_up")
    r_down = _adamw_reduce(w_down[0], m_w_down[0], v_w_down[0], sums_ffn[1], parts_ffn[1], chip, "adamw_w_down")
    r_in = [a.T[None] for a in r_in]
    r_up = [a.T[None] for a in r_up]
    r_out = [a[None] for a in r_out]
    r_down = [a[None] for a in r_down]

    dcb_nat = jnp.transpose(dcb_p.reshape(SUBLANES, 2, 2, 2, 704), (0, 2, 1, 3, 4)).reshape(SUBLANES, 2 * DFF)
    pieces = [dshift1_p, dscale1_p, dgate1_p, dshift2_p, dscale2_p, dgate2_p,
              dnorm1_p, dnorm2_p, dcb_nat, dpb_p, dps_p, dag_p, dqg_p, dkg_p, loss_p]
    n_vec = sum(p.shape[1] for p in pieces)
    packed = _pack_partials(pieces, dcw_p, "pack_partials")
    npack = packed.shape[1]
    gathered, gathered_pw = _all_gather([packed.reshape(SUBLANES, npack // SUBLANES), g_pw_p], [False, False], "gather_small")
    gathered = gathered.reshape(NDEV, npack)
    gathered_cw = lax.dynamic_index_in_dim(
        gathered[:, n_vec:n_vec + 6 * DFF].reshape(NDEV, 3, NDEV, 704), me_swapped, axis=2, keepdims=False)
    specs = [("ada_b", 0, 6 * D, False)]
    off = 6 * D
    for nme, width, fold in (("norm1_g", D, False), ("norm2_g", D, False), ("conv_b", 2 * DFF, False),
                             ("pool_b", DP, False), ("pool_scale", DP, False), ("attn_out_g", DA, False),
                             ("q_norm_g", DA, True), ("k_norm_g", DA, True)):
        specs.append((nme, off, width, fold))
        off += width
    small = {
        "ada_b": (ada_b, m_ada_b, v_ada_b),
        "norm1_g": (norm1_g, m_norm1_g, v_norm1_g), "norm2_g": (norm2_g, m_norm2_g, v_norm2_g),
        "conv_b": (conv_b, m_conv_b, v_conv_b),
        "pool_b": (pb, m_pool_b.reshape(1, DP), v_pool_b.reshape(1, DP)),
        "pool_scale": (pool_scale, m_pool_scale, v_pool_scale),
        "attn_out_g": (ag, m_attn_out_g.reshape(1, DA), v_attn_out_g.reshape(1, DA)),
        "q_norm_g": (q_norm_g, m_q_norm_g, v_q_norm_g), "k_norm_g": (k_norm_g, m_k_norm_g, v_k_norm_g),
        "pool_w": (pool_w.reshape(DP, LANES), m_pool_w.reshape(DP, LANES), v_pool_w.reshape(DP, LANES)),
        "conv_w": (conv_w[0], m_conv_w[0], v_conv_w[0]),
    }
    upd, dmod_all, loss = _small_update(gathered, gathered_pw, gathered_cw, specs, small, off, "small_update")
    g_ada_w = _ada_bwd(c_all, lax.dynamic_slice_in_dim(dmod_all, me * n_ada, n_ada, axis=1), "ada_bwd")
    r_ada = [g_ada_w] + [a[None] for a in _adamw(ada_w[0], m_ada_w[0], v_ada_w[0], g_ada_w[0], "adamw_ada_w")]

    shapes = {"ada_b": ada_b.shape, "norm1_g": norm1_g.shape, "pool_w": pool_w.shape, "pool_b": pool_b.shape,
              "pool_scale": pool_scale.shape, "q_norm_g": q_norm_g.shape, "k_norm_g": k_norm_g.shape,
              "attn_out_g": attn_out_g.shape, "norm2_g": norm2_g.shape, "conv_w": conv_w.shape, "conv_b": conv_b.shape}
    res = {nme: [a.reshape(shapes[nme]) for a in upd[nme]] for nme in shapes}
    res.update(ada_w=r_ada, w_in=r_in, w_out=r_out, w_up=r_up, w_down=r_down)
    names = ["ada_w", "ada_b", "norm1_g", "w_in", "pool_w", "pool_b", "pool_scale", "q_norm_g", "k_norm_g",
             "attn_out_g", "w_out", "norm2_g", "w_up", "conv_w", "conv_b", "w_down"]
    outs = [loss, grad_x[None]]
    for q in range(4):
        outs += [res[nme][q] for nme in names]
    return tuple(outs)
```

```python
import functools
import math

import numpy as np
import jax
import jax.numpy as jnp
from jax import lax
from jax.experimental import pallas as pl
from jax.experimental.pallas import tpu as pltpu

F32, BF16 = jnp.float32, jnp.bfloat16
D = 1024
DP = 512
DA = 512
HD = 64
DIN = DP + 3 * DA
DFF = 2816
POOL_WINDOWS = (2, 4, 8, 16)
HALO = 16
EPS = 1e-6
LANES = 128
SUBLANES = 8
NDEV = 8
VMEM_LIMIT = 56 * 1024 * 1024
MESH = pl.DeviceIdType.MESH

ADAM_LR, ADAM_B1, ADAM_B2, ADAM_EPS, ADAM_WD, ADAM_STEP = 0.001, 0.9, 0.999, 1e-08, 0.01, 10

NN = (((1,), (0,)), ((), ()))
NT = (((1,), (1,)), ((), ()))
TN = (((0,), (0,)), ((), ()))


def _params(*sem):
    return pltpu.CompilerParams(dimension_semantics=sem, vmem_limit_bytes=VMEM_LIMIT)


def _full(shape):
    nd = len(shape)
    return pl.BlockSpec(shape, lambda *_: (0,) * nd)


def _dot(a, b, dn=NN):
    return lax.dot_general(a, b, dn, preferred_element_type=F32)


def _split_dot(a, b, dn=NN):
    hi = a.astype(BF16)
    lo = (a - hi.astype(F32)).astype(BF16)
    return _dot(hi, b, dn) + _dot(lo, b, dn)


def _colsum8(v):
    r, n = v.shape
    return v.reshape(r // SUBLANES, SUBLANES, n).sum(axis=0)


def _block_diag_ones(n, blk):
    i = np.arange(n) // blk
    return jnp.asarray((i[:, None] == i[None, :]).astype(np.float32), BF16)


def _matmul(a, b, *, mode, out_dtype, tm, tn, tk, name, n_outer=False, side=None):
    if mode == "tn":
        K, M = a.shape
        N = b.shape[1]
    elif mode == "nt":
        M, K = a.shape
        N = b.shape[0]
    else:
        M, K = a.shape
        N = b.shape[1]
    tm, tn, tk = min(tm, M), min(tn, N), min(tk, K)
    assert M % tm == 0 and N % tn == 0 and K % tk == 0, (name, M, N, K, tm, tn, tk)
    nk = K // tk
    dn = {"nn": NN, "nt": NT, "tn": TN}[mode]

    def body(a_ref, b_ref, o_ref, *acc):
        if nk == 1:
            o_ref[...] = _dot(a_ref[...], b_ref[...], dn).astype(o_ref.dtype)
            return
        acc_ref, = acc
        k = pl.program_id(2)

        @pl.when(k == 0)
        def _():
            acc_ref[...] = jnp.zeros_like(acc_ref)

        acc_ref[...] += _dot(a_ref[...], b_ref[...], dn)

        @pl.when(k == nk - 1)
        def _():
            o_ref[...] = acc_ref[...].astype(o_ref.dtype)

    if n_outer:
        gi = lambda g: (g[1], g[0], g[2])
        grid = (N // tn, M // tm, nk)
    else:
        gi = lambda g: g
        grid = (M // tm, N // tn, nk)

    def amap(*g):
        i, j, k = gi(g)
        return (k, i) if mode == "tn" else (i, k)

    def bmap(*g):
        i, j, k = gi(g)
        return (j, k) if mode == "nt" else (k, j)

    def omap(*g):
        i, j, k = gi(g)
        return (i, j)

    a_blk = (tk, tm) if mode == "tn" else (tm, tk)
    b_blk = (tn, tk) if mode == "nt" else (tk, tn)
    acc_scratch = [] if nk == 1 else [pltpu.VMEM((tm, tn), F32)]
    if side is None:
        return pl.pallas_call(
            body, name=name, grid=grid,
            in_specs=[pl.BlockSpec(a_blk, amap), pl.BlockSpec(b_blk, bmap)],
            out_specs=pl.BlockSpec((tm, tn), omap),
            out_shape=jax.ShapeDtypeStruct((M, N), out_dtype),
            scratch_shapes=acc_scratch,
            compiler_params=_params("parallel", "parallel", "arbitrary"),
        )(a, b)

    ne = len(side.arrs)
    steps = grid[0] * grid[1] * grid[2]

    nsem = len(side.scratch)

    def with_side(*refs):
        e_in, e_out = refs[2:2 + ne], refs[3 + ne:3 + 2 * ne]
        sems = refs[len(refs) - nsem:]
        step = (pl.program_id(0) * grid[1] + pl.program_id(1)) * grid[2] + pl.program_id(2)

        @pl.when(step == 0)
        def _():
            side.start(e_in, e_out, *sems)

        body(refs[0], refs[1], refs[2 + ne], *refs[3 + 2 * ne:len(refs) - nsem])

        @pl.when(step == steps - 1)
        def _():
            side.finish(e_in, e_out, *sems)

    any_spec = pl.BlockSpec(memory_space=pl.ANY)
    res = pl.pallas_call(
        with_side, name=name, grid=grid,
        in_specs=[pl.BlockSpec(a_blk, amap), pl.BlockSpec(b_blk, bmap)] + [any_spec] * ne,
        out_specs=[pl.BlockSpec((tm, tn), omap)] + [any_spec] * ne,
        out_shape=[jax.ShapeDtypeStruct((M, N), out_dtype)] + side.out_shapes,
        scratch_shapes=acc_scratch + side.scratch,
        compiler_params=_params("arbitrary", "arbitrary", "arbitrary"),
    )(a, b, *side.arrs)
    return res[0], list(res[1:])


def _slot(swap, px, py, pc):
    return 4 * py + 2 * px + pc if swap else 4 * px + 2 * py + pc


class _Gather:
    def __init__(self, ins, outs, send, recv, loc, swaps):
        self.ins, self.outs, self.send, self.recv, self.loc, self.swaps = ins, outs, send, recv, loc, swaps
        x, y, c = lax.axis_index("x"), lax.axis_index("y"), lax.axis_index("c")
        self.me, self.sib = (x, y, c), (x, y, 1 - c)
        self.chips = [(1 - x, y), (x, 1 - y), (1 - x, 1 - y)]
        self.n = len(ins)

    @staticmethod
    def scratch(n):
        return [pltpu.SemaphoreType.DMA((7 * n,)), pltpu.SemaphoreType.DMA((7 * n,)), pltpu.SemaphoreType.DMA((n,))]

    def copy(self, a, k, blk, to, src=None):
        rows = self.outs[a].at[_slot(self.swaps[a], *blk)]
        return pltpu.make_async_remote_copy(
            src_ref=rows if src is None else src, dst_ref=rows,
            send_sem=self.send.at[7 * a + k], recv_sem=self.recv.at[7 * a + k], device_id=to, device_id_type=MESH)

    def mine(self, a):
        return pltpu.make_async_copy(self.ins[a], self.outs[a].at[_slot(self.swaps[a], *self.me)], self.loc.at[a])

    def first(self, a):
        c = self.me[2]
        return [self.copy(a, 0, self.me, self.sib, src=self.ins[a])] + [
            self.copy(a, 1 + j, self.me, (*chip, c), src=self.ins[a]) for j, chip in enumerate(self.chips)]

    def forwards(self, a):
        c = self.me[2]
        return [self.copy(a, 4 + j, (*chip, c), self.sib) for j, chip in enumerate(self.chips)]

    def start(self):
        for a in range(self.n):
            self.mine(a).start()
        for a in range(self.n):
            for cp in self.first(a):
                cp.start()

    def forward(self):
        c = self.me[2]
        for a in range(self.n):
            fwd = self.forwards(a)
            for j, chip in enumerate(self.chips):
                self.copy(a, 1 + j, (*chip, c), self.me).wait_recv()
                fwd[j].start()

    def finish(self):
        c = self.me[2]
        for a in range(self.n):
            self.copy(a, 0, self.sib, self.me).wait_recv()
            for j, chip in enumerate(self.chips):
                self.copy(a, 4 + j, (*chip, 1 - c), self.me).wait_recv()
        for a in range(self.n):
            for cp in self.first(a) + self.forwards(a):
                cp.wait_send()
            self.mine(a).wait()


def _all_gather(arrs, swaps, name):
    n = len(arrs)

    def body(*refs):
        g = _Gather(refs[:n], refs[n:2 * n], *refs[2 * n:], swaps)
        g.start()
        g.forward()
        g.finish()

    any_spec = pl.BlockSpec(memory_space=pl.ANY)
    return pl.pallas_call(
        body, name=name,
        in_specs=[any_spec] * n, out_specs=[any_spec] * n,
        out_shape=[jax.ShapeDtypeStruct((NDEV,) + a.shape, a.dtype) for a in arrs],
        scratch_shapes=_Gather.scratch(n),
    )(*arrs)


def _all_gather_small(arr, name):
    def body(in_ref, out_ref, send, recv, loc):
        x, y, c = lax.axis_index("x"), lax.axis_index("y"), lax.axis_index("c")
        flip = lambda v, bit: 1 - v if bit else v
        peers = [(flip(x, k >> 2 & 1), flip(y, k >> 1 & 1), flip(c, k & 1)) for k in range(1, NDEV)]
        mine = pltpu.make_async_copy(in_ref, out_ref.at[_slot(False, x, y, c)], loc)
        mine.start()

        def copy(k, src_dev, to):
            return pltpu.make_async_remote_copy(
                src_ref=in_ref, dst_ref=out_ref.at[_slot(False, *src_dev)], send_sem=send.at[k], recv_sem=recv.at[k],
                device_id=to, device_id_type=MESH)

        sends = [copy(k, (x, y, c), peer) for k, peer in enumerate(peers)]
        for cp in sends:
            cp.start()
        for k, peer in enumerate(peers):
            copy(k, peer, (x, y, c)).wait_recv()
        for cp in sends:
            cp.wait_send()
        mine.wait()

    any_spec = pl.BlockSpec(memory_space=pl.ANY)
    return pl.pallas_call(
        body, name=name, in_specs=[any_spec], out_specs=any_spec,
        out_shape=jax.ShapeDtypeStruct((NDEV,) + arr.shape, arr.dtype),
        scratch_shapes=[pltpu.SemaphoreType.DMA((NDEV - 1,)), pltpu.SemaphoreType.DMA((NDEV - 1,)), pltpu.SemaphoreType.DMA],
    )(arr)


def _pair_copies(ins, gots, send, recv, swaps):
    x, y, c = lax.axis_index("x"), lax.axis_index("y"), lax.axis_index("c")
    return [pltpu.make_async_remote_copy(
        src_ref=ins[a].at[_slot(swaps[a], k // 2, k % 2, 1 - c)], dst_ref=gots[a].at[k],
        send_sem=send.at[4 * a + k], recv_sem=recv.at[4 * a + k], device_id=(x, y, 1 - c), device_id_type=MESH)
        for a in range(len(ins)) for k in range(4)]


def _pair_exchange(arrs, swaps, name):
    n = len(arrs)

    def body(*refs):
        rems = _pair_copies(refs[:n], refs[n:2 * n], *refs[2 * n:], swaps)
        for rc in rems:
            rc.start()
        for rc in rems:
            rc.wait_recv()
        for rc in rems:
            rc.wait_send()

    any_spec = pl.BlockSpec(memory_space=pl.ANY)
    return pl.pallas_call(
        body, name=name,
        in_specs=[any_spec] * n, out_specs=[any_spec] * n,
        out_shape=[jax.ShapeDtypeStruct((4,) + a.shape[1:], a.dtype) for a in arrs],
        scratch_shapes=[pltpu.SemaphoreType.DMA((4 * n,)), pltpu.SemaphoreType.DMA((4 * n,))],
    )(*arrs)


def _chip_copies(ins, outs, send, recv):
    x, y, c = lax.axis_index("x"), lax.axis_index("y"), lax.axis_index("c")
    chips = [(1 - x, y), (x, 1 - y), (1 - x, 1 - y)]
    return [pltpu.make_async_remote_copy(
        src_ref=ins[a].at[2 * px + py], dst_ref=outs[a].at[j], send_sem=send.at[3 * a + j], recv_sem=recv.at[3 * a + j],
        device_id=(px, py, c), device_id_type=MESH) for a in range(len(ins)) for j, (px, py) in enumerate(chips)]


def _chip_exchange(arrs, name):
    n = len(arrs)

    def body(*refs):
        rems = _chip_copies(refs[:n], refs[n:2 * n], *refs[2 * n:])
        for rc in rems:
            rc.start()
        for rc in rems:
            rc.wait_recv()
        for rc in rems:
            rc.wait_send()

    any_spec = pl.BlockSpec(memory_space=pl.ANY)
    return pl.pallas_call(
        body, name=name,
        in_specs=[any_spec] * n, out_specs=[any_spec] * n,
        out_shape=[jax.ShapeDtypeStruct((3,) + a.shape[1:], a.dtype) for a in arrs],
        scratch_shapes=[pltpu.SemaphoreType.DMA((3 * n,)), pltpu.SemaphoreType.DMA((3 * n,))],
    )(*arrs)


class _Side:
    def __init__(self, arrs, out_shapes, scratch, start, finish, mid=None):
        self.arrs, self.out_shapes, self.scratch = list(arrs), list(out_shapes), list(scratch)
        self.start, self.finish, self.mid = start, finish, mid


def _copies_side(arrs, out_shapes, n_copies, make):
    def start(ins, outs, *sems):
        for cp in make(ins, outs, *sems):
            cp.start()

    def finish(ins, outs, *sems):
        cps = make(ins, outs, *sems)
        for cp in cps:
            cp.wait_recv()
        for cp in cps:
            cp.wait_send()

    return _Side(arrs, out_shapes, [pltpu.SemaphoreType.DMA((n_copies,)), pltpu.SemaphoreType.DMA((n_copies,))], start, finish)


def _pair_side(arrs, swaps):
    return _copies_side(arrs, [jax.ShapeDtypeStruct((4,) + a.shape[1:], a.dtype) for a in arrs], 4 * len(arrs),
                        functools.partial(_pair_copies, swaps=swaps))


def _chip_side(arrs):
    return _copies_side(arrs, [jax.ShapeDtypeStruct((3,) + a.shape[1:], a.dtype) for a in arrs], 3 * len(arrs), _chip_copies)


def _gather_side(arrs, swaps):
    return _Side(arrs, [jax.ShapeDtypeStruct((NDEV,) + a.shape, a.dtype) for a in arrs], _Gather.scratch(len(arrs)),
                 start=lambda ins, outs, *sems: _Gather(ins, outs, *sems, swaps).start(),
                 mid=lambda ins, outs, *sems: _Gather(ins, outs, *sems, swaps).forward(),
                 finish=lambda ins, outs, *sems: _Gather(ins, outs, *sems, swaps).finish())


def _row_call(body, side, *, name, steps, in_specs, out_specs, out_shape, ins):
    if side is None:
        res = pl.pallas_call(body, name=name, grid=(steps,), in_specs=in_specs, out_specs=out_specs, out_shape=out_shape,
                             compiler_params=_params("arbitrary"))(*ins)
        return list(res), []
    n_in, n_out, ne = len(in_specs), len(out_specs), len(side.arrs)

    def wrapped(*refs):
        e_in = refs[n_in:n_in + ne]
        e_out = refs[n_in + ne + n_out:n_in + 2 * ne + n_out]
        sems = refs[n_in + 2 * ne + n_out:]
        i = pl.program_id(0)

        @pl.when(i == 0)
        def _():
            side.start(e_in, e_out, *sems)

        if side.mid is not None:
            @pl.when(i == steps // 2)
            def _():
                side.mid(e_in, e_out, *sems)

        body(*refs[:n_in], *refs[n_in + ne:n_in + ne + n_out])

        @pl.when(i == steps - 1)
        def _():
            side.finish(e_in, e_out, *sems)

    any_spec = pl.BlockSpec(memory_space=pl.ANY)
    res = pl.pallas_call(
        wrapped, name=name, grid=(steps,), in_specs=list(in_specs) + [any_spec] * ne,
        out_specs=list(out_specs) + [any_spec] * ne, out_shape=list(out_shape) + side.out_shapes,
        scratch_shapes=side.scratch,
        compiler_params=_params("arbitrary"))(*ins, *side.arrs)
    return list(res[:n_out]), list(res[n_out:])


def _pair_sum(grads, got, swap, core, name):
    _, r, c = got.shape
    tr = r if r <= 352 else r // 2

    def own_map(k, i, core_ref):
        return (_slot(swap, k // 2, k % 2, core_ref[0]), i, 0)

    def body(core_ref, a_ref, b_ref, o_ref):
        o_ref[...] = a_ref[...] + b_ref[...]

    spec = pl.BlockSpec((None, tr, c), lambda k, i, core_ref: (k, i, 0))
    return pl.pallas_call(
        body, name=name,
        grid_spec=pltpu.PrefetchScalarGridSpec(
            num_scalar_prefetch=1, grid=(4, r // tr),
            in_specs=[pl.BlockSpec((None, tr, c), own_map), spec], out_specs=spec),
        out_shape=jax.ShapeDtypeStruct(got.shape, got.dtype), compiler_params=_params("parallel", "parallel"),
    )(core, grads, got)


def _adamw_math(w, g, m, v):
    m = ADAM_B1 * m + (1.0 - ADAM_B1) * g
    v = ADAM_B2 * v + (1.0 - ADAM_B2) * (g * g)
    m_hat = m / (1.0 - ADAM_B1 ** ADAM_STEP)
    v_hat = v / (1.0 - ADAM_B2 ** ADAM_STEP)
    delta = -ADAM_LR * (m_hat / (jnp.sqrt(v_hat) + ADAM_EPS) + ADAM_WD * w)
    return delta, m, v


def _adamw_tile(r):
    for cand in (256, 352, 128):
        if r % cand == 0:
            return cand
    return r


def _adamw(w, m, v, g, name):
    r, c = w.shape
    tr = _adamw_tile(r)
    spec = pl.BlockSpec((tr, c), lambda i: (i, 0))

    def body(w_ref, m_ref, v_ref, g_ref, d_ref, nm_ref, nv_ref):
        d_ref[...], nm_ref[...], nv_ref[...] = _adamw_math(w_ref[...], g_ref[...], m_ref[...], v_ref[...])

    out = jax.ShapeDtypeStruct((r, c), F32)
    return pl.pallas_call(
        body, name=name, grid=(r // tr,), in_specs=[spec] * 4, out_specs=[spec] * 3, out_shape=[out] * 3,
        compiler_params=_params("parallel"),
    )(w, m, v, g)


def _adamw_reduce(w, m, v, sums, recv, chip, name):
    r, c = w.shape
    tr = _adamw_tile(r)
    spec = pl.BlockSpec((tr, c), lambda i, chip_ref: (i, 0))

    def body(chip_ref, w_ref, m_ref, v_ref, s_ref, p_ref, g_ref, d_ref, nm_ref, nv_ref):
        g = ((s_ref[...] + p_ref[0]) + p_ref[1]) + p_ref[2]
        g_ref[...] = g
        d_ref[...], nm_ref[...], nv_ref[...] = _adamw_math(w_ref[...], g, m_ref[...], v_ref[...])

    out = jax.ShapeDtypeStruct((r, c), F32)
    return pl.pallas_call(
        body, name=name,
        grid_spec=pltpu.PrefetchScalarGridSpec(
            num_scalar_prefetch=1, grid=(r // tr,),
            in_specs=[spec, spec, spec, pl.BlockSpec((None, tr, c), lambda i, chip_ref: (chip_ref[0], i, 0)),
                      pl.BlockSpec((3, tr, c), lambda i, chip_ref: (0, i, 0))],
            out_specs=[spec] * 4),
        out_shape=[out] * 4, compiler_params=_params("parallel"),
    )(chip, w, m, v, sums, recv)


def _vec(n):
    return pl.BlockSpec((1, n), lambda *_: (0, 0))


def _ln_mod(x, g, scale, shift, *, ts, name, side=None):
    s = x.shape[0]
    row = pl.BlockSpec((ts, D), lambda i: (i, 0))

    def body(x_ref, g_ref, sc_ref, sh_ref, h_ref):
        xv = x_ref[...]
        r = lax.rsqrt(jnp.mean(xv * xv, axis=-1, keepdims=True) + EPS)
        h = (xv * r) * g_ref[...]
        h_ref[...] = (h * (1.0 + sc_ref[...]) + sh_ref[...]).astype(BF16)

    (h,), extra = _row_call(body, side, name=name, steps=s // ts, in_specs=[row, _vec(D), _vec(D), _vec(D)],
                            out_specs=[row], out_shape=[jax.ShapeDtypeStruct((s, D), BF16)], ins=(x, g, scale, shift))
    return h, extra


def _proj_res_ln_mod(mix, w, x, gate, g, scale, shift, *, tm, name):
    s = x.shape[0]
    row = pl.BlockSpec((tm, D), lambda i: (i, 0))

    def body(m_ref, w_ref, x_ref, gt_ref, g_ref, sc_ref, sh_ref, a_ref, x1_ref, h_ref):
        att = _dot(m_ref[...], w_ref[...])
        a_ref[...] = att.astype(BF16)
        x1 = x_ref[...] + gt_ref[...] * att
        x1_ref[...] = x1
        r = lax.rsqrt(jnp.mean(x1 * x1, axis=-1, keepdims=True) + EPS)
        h = (x1 * r) * g_ref[...]
        h_ref[...] = (h * (1.0 + sc_ref[...]) + sh_ref[...]).astype(BF16)

    return pl.pallas_call(
        body, name=name, grid=(s // tm,), in_specs=[row, _full(w.shape), row] + [_vec(D)] * 4, out_specs=[row, row, row],
        out_shape=[jax.ShapeDtypeStruct((s, D), BF16), jax.ShapeDtypeStruct((s, D), F32), jax.ShapeDtypeStruct((s, D), BF16)],
        compiler_params=_params("parallel"),
    )(mix, w, x, gate, g, scale, shift)


def _proj_loss_head(act, w, x1, tgt, gate2, *, tm, name):
    s = x1.shape[0]
    n = s // tm
    row = pl.BlockSpec((tm, D), lambda i: (i, 0))
    acc8 = pl.BlockSpec((SUBLANES, D), lambda i: (0, 0))

    def body(a_ref, w_ref, x_ref, t_ref, g_ref, dy_ref, df_ref, dg_ref, loss_ref, lacc):
        i = pl.program_id(0)

        @pl.when(i == 0)
        def _():
            lacc[...] = jnp.zeros_like(lacc)
            dg_ref[...] = jnp.zeros_like(dg_ref)

        f = _dot(a_ref[...], w_ref[...])
        diff = x_ref[...] + g_ref[...] * f - t_ref[...]
        lacc[...] += _colsum8(diff * diff)
        dy = diff * (1.0 / D)
        dy_ref[...] = dy
        df_ref[...] = (dy * g_ref[...]).astype(BF16)
        dg_ref[...] += _colsum8(dy * f)

        @pl.when(i == n - 1)
        def _():
            loss_ref[...] = jnp.full((SUBLANES, LANES), (0.5 / D) * jnp.sum(lacc[...]), F32)

    return pl.pallas_call(
        body, name=name, grid=(n,),
        in_specs=[pl.BlockSpec((tm, act.shape[1]), lambda i: (i, 0)), _full(w.shape), row, row, _vec(D)],
        out_specs=[row, row, acc8, _full((SUBLANES, LANES))],
        out_shape=[jax.ShapeDtypeStruct((s, D), F32), jax.ShapeDtypeStruct((s, D), BF16),
                   jax.ShapeDtypeStruct((SUBLANES, D), F32), jax.ShapeDtypeStruct((SUBLANES, LANES), F32)],
        scratch_shapes=[pltpu.VMEM((SUBLANES, D), F32)], compiler_params=_params("arbitrary"),
    )(act, w, x1, tgt, gate2)


def _group_rsqrt(t, bd):
    return lax.rsqrt(_split_dot(t * t, bd) * (1.0 / HD) + EPS)


def _in_proj_qk_norm(h, w, qg, kg, bd, *, tm, name):
    s = h.shape[0]

    def body(h_ref, w_ref, qg_ref, kg_ref, bd_ref, p_ref, o_ref):
        bdv = bd_ref[...]
        hv = h_ref[...]
        p_ref[:, 0:DP] = _dot(hv, w_ref[0:DP, :], NT)
        q = _dot(hv, w_ref[DP:DP + DA, :], NT)
        p_ref[:, DP:DP + DA] = q
        o_ref[:, 0:DA] = (q * _group_rsqrt(q, bdv) * qg_ref[...]).astype(BF16)
        k = _dot(hv, w_ref[DP + DA:DP + 2 * DA, :], NT)
        p_ref[:, DP + DA:DP + 2 * DA] = k
        o_ref[:, DA:2 * DA] = (k * _group_rsqrt(k, bdv) * kg_ref[...]).astype(BF16)
        v = _dot(hv, w_ref[DP + 2 * DA:, :], NT)
        p_ref[:, DP + 2 * DA:] = v
        o_ref[:, 2 * DA:] = v.astype(BF16)

    return pl.pallas_call(
        body, name=name, grid=(s // tm,),
        in_specs=[pl.BlockSpec((tm, D), lambda i: (i, 0)), _full(w.shape), _vec(DA), _vec(DA), _full((DA, DA))],
        out_specs=[pl.BlockSpec((tm, DIN), lambda i: (i, 0)), pl.BlockSpec((tm, 3 * DA), lambda i: (i, 0))],
        out_shape=[jax.ShapeDtypeStruct((s, DIN), F32), jax.ShapeDtypeStruct((s, 3 * DA), BF16)],
        compiler_params=_params("parallel"),
    )(h, w, qg, kg, bd)


EXP_UNDERFLOW = -120.0


def _log_terms(z):
    neg_abs = lax.bitcast_convert_type(lax.bitcast_convert_type(z, jnp.uint32) | jnp.uint32(0x80000000), F32)
    b = jnp.minimum(z, 0.0) - jnp.log(1.0 + jnp.exp(neg_abs))
    return b, b - z


def _head_masks(rows):
    lane = lax.broadcasted_iota(jnp.int32, (rows, LANES), 1)
    return [lane < HD, lane >= HD]


def _attn_fwd(qkv, gather, swaps, *, tq, tk, hp, name):
    s = qkv.shape[0]
    nrep = tk // LANES
    ndiag = tq // tk
    ng = len(gather)
    nh, wl = 2 * hp, LANES * hp
    ngrp, nq = DA // wl, s // tq
    lanes = [slice(LANES * pp, LANES * (pp + 1)) for pp in range(hp)]

    def body(*refs):
        q_ref, k_ref, v_ref = refs[:3]
        g_in = refs[3:3 + ng]
        o_ref, tot_ref, first_ref = refs[3 + ng:6 + ng]
        g_out = refs[6 + ng:6 + 2 * ng]
        oacc, rc = refs[6 + 2 * ng:8 + 2 * ng]
        g_sems = refs[8 + 2 * ng:]
        i = pl.program_id(1)
        step_id = pl.program_id(0) * nq + i

        @pl.when(step_id == 0)
        def _():
            _Gather(g_in, g_out, *g_sems, swaps).start()

        @pl.when(step_id == (ngrp * nq * 3) // 4)
        def _():
            _Gather(g_in, g_out, *g_sems, swaps).forward()

        heads = _head_masks(tq)
        qs = [jnp.where(heads[a % 2], q_ref[:, lanes[a // 2]] * 0.125, 0.0).astype(BF16) for a in range(nh)]
        dif = lax.broadcasted_iota(jnp.int32, (tq, tk), 0) - lax.broadcasted_iota(jnp.int32, (tq, tk), 1)
        kr = lax.broadcasted_iota(jnp.int32, (tk, tk), 0)
        kc = lax.broadcasted_iota(jnp.int32, (tk, tk), 1)
        later =jnp.where(kr > kc, 1.0, 0.0).astype(BF16)
        oacc[...] = jnp.zeros_like(oacc)
        rc[...] = jnp.zeros_like(rc)

        def tile(kb, thr):
            rows = pl.ds(pl.multiple_of(kb * tk, tk), tk)
            ks = [k_ref[rows, ln] for ln in lanes]
            vs = [v_ref[rows, ln] for ln in lanes]
            qr = slice(0 if thr is None else thr, tq)
            rcv = [rc[a, qr, :] for a in range(nh)]
            zs = [_dot(qs[a][qr], ks[a // 2], NT) for a in range(nh)]
            bs, mbs = [], []
            for a in range(nh):
                b, m = _log_terms(zs[a])
                if thr is not None:
                    m = jnp.where(dif[qr] > thr, m, 0.0)
                bs.append(b)
                mbs.append(m.astype(BF16))
            rl = [_dot(mbs[a], later) for a in range(nh)]
            for a in range(nh):
                p = jnp.exp(bs[a] + (rl[a] + jnp.tile(rcv[a], (1, nrep))))
                if thr is not None:
                    p = jnp.where(dif[qr] > thr, p, 0.0)
                oacc[a, qr, :] += _dot(p.astype(BF16), vs[a // 2])
                rc[a, qr, :] = rcv[a] + (rl[a][:, 0:1] + mbs[a][:, 0:1].astype(F32))

        for d in reversed(range(ndiag)):
            tile(i * ndiag + d, d * tk)

        def live():
            top = rc[0]
            for a in range(1, nh):
                top = jnp.maximum(top, rc[a])
            return jnp.max(top) > EXP_UNDERFLOW

        def step(carry):
            kb, _ = carry
            tile(kb, None)
            return kb - 1, live()

        kb_end, _ = lax.while_loop(lambda cr: jnp.logical_and(cr[0] >= 0, cr[1]), step, (i * ndiag - 1, live()))
        first_ref[pl.program_id(0), i] = (kb_end + 1).astype(F32)
        for pp, ln in enumerate(lanes):
            o_ref[:, ln] = jnp.where(heads[0], oacc[2 * pp], oacc[2 * pp + 1])
            tot_ref[:, ln] = jnp.where(heads[0], rc[2 * pp], rc[2 * pp + 1])

        @pl.when(step_id == ngrp * nq - 1)
        def _():
            _Gather(g_in, g_out, *g_sems, swaps).finish()

    qspec = pl.BlockSpec((tq, wl), lambda p, i: (i, p))
    any_spec = pl.BlockSpec(memory_space=pl.ANY)
    res = pl.pallas_call(
        body, name=name, grid=(ngrp, nq),
        in_specs=[qspec,
                  pl.BlockSpec((s, wl), lambda p, i: (0, ngrp + p)),
                  pl.BlockSpec((s, wl), lambda p, i: (0, 2 * ngrp + p))] + [any_spec] * ng,
        out_specs=[qspec, qspec, pl.BlockSpec(memory_space=pltpu.SMEM)] + [any_spec] * ng,
        out_shape=[jax.ShapeDtypeStruct((s, DA), F32), jax.ShapeDtypeStruct((s, DA), F32),
                   jax.ShapeDtypeStruct((ngrp, nq), F32)]
        + [jax.ShapeDtypeStruct((NDEV,) + a.shape, a.dtype) for a in gather],
        scratch_shapes=[pltpu.VMEM((nh, tq, LANES), F32), pltpu.VMEM((nh, tq, LANES), F32)] + _Gather.scratch(ng),
        compiler_params=_params("arbitrary", "arbitrary"),
    )(qkv, qkv, qkv, *gather)
    return res[0], res[1], res[2], res[3:]


def _attn_bwd(qkv, do, tot, first, exchange, *, tq, tk, hp, name):
    s = qkv.shape[0]
    nrep = tk // LANES
    ndiag = tq // tk
    ne = len(exchange)
    nh, wl = 2 * hp, LANES * hp
    ngrp, nq = DA // wl, s // tq
    lanes = [slice(LANES * pp, LANES * (pp + 1)) for pp in range(hp)]

    def body(*refs):
        q_ref, k_ref, v_ref, do_ref, tot_ref, first_ref = refs[:6]
        e_in = refs[6:6 + ne]
        dq_ref, dk_ref, dv_ref = refs[6 + ne:9 + ne]
        e_out = refs[9 + ne:9 + 2 * ne]
        dqacc, rem, gc = refs[9 + 2 * ne:12 + 2 * ne]
        e_sems = refs[12 + 2 * ne:]
        i = pl.program_id(1)
        step_id = pl.program_id(0) * nq + i

        @pl.when(step_id == 0)
        def _():
            for cp in _chip_copies(e_in, e_out, *e_sems):
                cp.start()

        @pl.when(i == 0)
        def _():
            dk_ref[...] = jnp.zeros_like(dk_ref)
            dv_ref[...] = jnp.zeros_like(dv_ref)

        heads = _head_masks(tq)
        qs = [jnp.where(heads[a % 2], q_ref[:, lanes[a // 2]] * 0.125, 0.0).astype(BF16) for a in range(nh)]
        dob = [jnp.where(heads[a % 2], do_ref[:, lanes[a // 2]], 0.0).astype(BF16) for a in range(nh)]
        dif = lax.broadcasted_iota(jnp.int32, (tq, tk), 0) - lax.broadcasted_iota(jnp.int32, (tq, tk), 1)
        kr = lax.broadcasted_iota(jnp.int32, (tk, tk), 0)
        kc = lax.broadcasted_iota(jnp.int32, (tk, tk), 1)
        up_incl = jnp.where(kr <= kc, 1.0, 0.0).astype(BF16)
        up_strict = jnp.where(kr < kc, 1.0, 0.0).astype(BF16)
        dqacc[...] = jnp.zeros_like(dqacc)
        gc[...] = jnp.zeros_like(gc)
        for pp, ln in enumerate(lanes):
            totv = tot_ref[:, ln]
            swapped = pltpu.roll(totv, HD, axis=1)
            rem[2 * pp] = jnp.where(heads[0], totv, swapped)
            rem[2 * pp + 1] = jnp.where(heads[1], totv, swapped)

        def tile(kb, thr):
            rows = pl.ds(pl.multiple_of(kb * tk, tk), tk)
            ks = [k_ref[rows, ln] for ln in lanes]
            vs = [v_ref[rows, ln] for ln in lanes]
            qr = slice(0 if thr is None else thr, tq)
            remv = [rem[a, qr, :] for a in range(nh)]
            gcv = [gc[a, qr, :] for a in range(nh)]
            zs = [_dot(qs[a][qr], ks[a // 2], NT) for a in range(nh)]
            das = [_dot(dob[a][qr], vs[a // 2], NT) for a in range(nh)]
            bs, mbs = [], []
            for a in range(nh):
                b, m = _log_terms(zs[a])
                if thr is not None:
                    m = jnp.where(dif[qr] > thr, m, 0.0)
                bs.append(b)
                mbs.append(m.astype(BF16))
            pl_ = [_dot(mbs[a], up_incl) for a in range(nh)]
            ps, gs, gbs = [], [], []
            for a in range(nh):
                p = jnp.exp(bs[a] + (jnp.tile(remv[a], (1, nrep)) - pl_[a]))
                if thr is not None:
                    p = jnp.where(dif[qr] > thr, p, 0.0)
                g = p * das[a]
                ps.append(p.astype(BF16))
                gs.append(g)
                gbs.append(g.astype(BF16))
            cl = [_dot(gbs[a], up_strict) for a in range(nh)]
            dk_add = [jnp.zeros((tk, LANES), F32) for _ in range(hp)]
            dv_add = [jnp.zeros((tk, LANES), F32) for _ in range(hp)]
            for a in range(nh):
                dz = gs[a] - jnp.exp(bs[a]) * (gs[a] + (jnp.tile(gcv[a], (1, nrep)) + cl[a]))
                if thr is not None:
                    dz = jnp.where(dif[qr] > thr, dz, 0.0)
                dzb = dz.astype(BF16)
                dqacc[a, qr, :] += _dot(dzb, ks[a // 2])
                dk_add[a // 2] += _dot(dzb, qs[a][qr], TN)
                dv_add[a // 2] += _dot(ps[a], dob[a][qr], TN)
                rem[a, qr, :] = remv[a] - pl_[a][:, tk - 1:tk]
                gc[a, qr, :] = gcv[a] + (cl[a][:, tk - 1:tk] + gbs[a][:, tk - 1:tk].astype(F32))
            for pp, ln in enumerate(lanes):
                dk_ref[rows, ln] += dk_add[pp]
                dv_ref[rows, ln] += dv_add[pp]

        def step(kb, carry):
            tile(kb, None)
            return carry

        first_kb = first_ref[(pl.program_id(0) * first.shape[0]) // ngrp, i].astype(jnp.int32)
        lax.fori_loop(first_kb, i * ndiag, step, 0)
        for d in range(ndiag):
            tile(i * ndiag + d, d * tk)
        for pp, ln in enumerate(lanes):
            dq_ref[:, ln] = jnp.where(heads[0], dqacc[2 * pp], dqacc[2 * pp + 1]) * 0.125

        @pl.when(step_id == ngrp * nq - 1)
        def _():
            cps = _chip_copies(e_in, e_out, *e_sems)
            for cp in cps:
                cp.wait_recv()
            for cp in cps:
                cp.wait_send()

    qspec = pl.BlockSpec((tq, wl), lambda p, i: (i, p))
    full = pl.BlockSpec((s, wl), lambda p, i: (0, p))
    any_spec = pl.BlockSpec(memory_space=pl.ANY)
    out = jax.ShapeDtypeStruct((s, DA), F32)
    res = pl.pallas_call(
        body, name=name, grid=(ngrp, nq),
        in_specs=[qspec, pl.BlockSpec((s, wl), lambda p, i: (0, ngrp + p), pipeline_mode=pl.Buffered(1)),
                  pl.BlockSpec((s, wl), lambda p, i: (0, 2 * ngrp + p), pipeline_mode=pl.Buffered(1)), qspec, qspec,
                  pl.BlockSpec(memory_space=pltpu.SMEM)] + [any_spec] * ne,
        out_specs=[qspec, full, full] + [any_spec] * ne,
        out_shape=[out, out, out] + [jax.ShapeDtypeStruct((3,) + a.shape[1:], a.dtype) for a in exchange],
        scratch_shapes=[pltpu.VMEM((nh, tq, LANES), F32)] * 3
        + [pltpu.SemaphoreType.DMA((3 * ne,)), pltpu.SemaphoreType.DMA((3 * ne,))],
        compiler_params=_params("arbitrary", "arbitrary"),
    )(qkv, qkv, qkv, do, tot, first, *exchange)
    return res[0], res[1], res[2], res[3:]


def _shift_rows(v, k):
    return pltpu.roll(v, k % v.shape[0], axis=0)


def _pooled(u, uh, i, g, w, ts):
    halo = jnp.where(i > 0, uh, 0.0)
    ue = jnp.concatenate([halo, u], axis=0)
    acc, span = ue, 1
    while span < w:
        acc = acc + _shift_rows(acc, span)
        span *= 2
    tpos = i * ts + lax.broadcasted_iota(jnp.int32, (ts, 1), 0)
    cnt = jnp.minimum(tpos + 1, w).astype(F32)
    return acc[HALO:] / cnt - u


def _pool_mix(proj, o, pw, pb, ps, ag, bd, *, ts, name):
    s = proj.shape[0]
    hb = ts // HALO

    def body(u_ref, uh_ref, o_ref, pw_ref, pb_ref, ps_ref, ag_ref, bd_ref, mix_ref):
        i = pl.program_id(0)
        for g, w in enumerate(POOL_WINDOWS):
            cols = slice(g * LANES, (g + 1) * LANES)
            pooled = _pooled(u_ref[:, cols], uh_ref[:, cols], i, g, w, ts)
            yv = (_dot(pooled.astype(BF16), pw_ref[g]) + pb_ref[:, cols]) * ps_ref[:, cols]
            mix_ref[:, cols] = yv.astype(BF16)
        ov = o_ref[...]
        mix_ref[:, DP:] = (ov * _group_rsqrt(ov, bd_ref[...]) * ag_ref[...]).astype(BF16)

    return pl.pallas_call(
        body, name=name, grid=(s // ts,),
        in_specs=[pl.BlockSpec((ts, DP), lambda i: (i, 0)),
                  pl.BlockSpec((HALO, DP), lambda i: (jnp.maximum(i * hb - 1, 0), 0)),
                  pl.BlockSpec((ts, DA), lambda i: (i, 0)),
                  _full((4, LANES, LANES)), _vec(DP), _vec(DP), _vec(DA), _full((DA, DA))],
        out_specs=pl.BlockSpec((ts, D), lambda i: (i, 0)),
        out_shape=jax.ShapeDtypeStruct((s, D), BF16), compiler_params=_params("parallel"),
    )(proj, proj, o, pw, pb, ps, ag, bd)


CF = DFF // 2
MXU_COLS = 256


def _sigmoid(t):
    return 0.5 + 0.5 * jnp.tanh(0.5 * t)


def _sub_chunks(width):
    return [(c0, min(MXU_COLS, width - c0)) for c0 in range(0, width, MXU_COLS)]


def _up_conv_gate(h2, w_up, cw, cb, *, tm, name):
    s = h2.shape[0]
    hb = tm // HALO

    def body(a_ref, ah_ref, w_ref, cw_ref, cb_ref, up_ref, c_ref, act_ref):
        i = pl.program_id(1)
        halo = jnp.where(i > 0, ah_ref[...], jnp.zeros_like(ah_ref))
        ext = jnp.concatenate([halo, a_ref[...]], axis=0)
        for c0, cwid in _sub_chunks(CF):
            conv = []
            for off in (c0, CF + c0):
                cols = slice(off, off + cwid)
                ue = _dot(ext, w_ref[cols, :], NT)
                up_ref[:, cols] = ue[HALO:].astype(BF16)
                y = cw_ref[2:3, cols] * ue + cw_ref[1:2, cols] * _shift_rows(ue, 1) + cw_ref[0:1, cols] * _shift_rows(ue, 2)
                cv = y[HALO:] + cb_ref[:, cols]
                c_ref[:, cols] = cv.astype(BF16)
                conv.append(cv)
            gt, vl = conv
            act_ref[:, c0:c0 + cwid] = (gt * _sigmoid(gt) * vl).astype(BF16)

    return pl.pallas_call(
        body, name=name, grid=(2, s // tm),
        in_specs=[pl.BlockSpec((tm, D), lambda j, i: (i, 0)),
                  pl.BlockSpec((HALO, D), lambda j, i: (jnp.maximum(i * hb - 1, 0), 0)),
                  pl.BlockSpec((2 * CF, D), lambda j, i: (j, 0)),
                  pl.BlockSpec((3, 2 * CF), lambda j, i: (0, j)), pl.BlockSpec((1, 2 * CF), lambda j, i: (0, j))],
        out_specs=[pl.BlockSpec((tm, 2 * CF), lambda j, i: (i, j)), pl.BlockSpec((tm, 2 * CF), lambda j, i: (i, j)),
                   pl.BlockSpec((tm, CF), lambda j, i: (i, j))],
        out_shape=[jax.ShapeDtypeStruct((s, 2 * DFF), BF16), jax.ShapeDtypeStruct((s, 2 * DFF), BF16),
                   jax.ShapeDtypeStruct((s, DFF), BF16)],
        compiler_params=_params("parallel", "parallel"),
    )(h2, h2, w_up, cw, cb)


def _down_bwd_gate(dffn, w_down, conv, *, tm, name):
    s = dffn.shape[0]

    def body(a_ref, w_ref, c_ref, d_ref, db_ref):
        i = pl.program_id(1)

        @pl.when(i == 0)
        def _():
            db_ref[...] = jnp.zeros_like(db_ref)

        a = a_ref[...]
        for c0, cwid in _sub_chunks(CF):
            gcols, vcols = slice(c0, c0 + cwid), slice(CF + c0, CF + c0 + cwid)
            da = _dot(a, w_ref[gcols, :], NT)
            gt, vl = c_ref[:, gcols].astype(F32), c_ref[:, vcols].astype(F32)
            sg = _sigmoid(gt)
            dgt = da * vl * (sg * (1.0 + gt * (1.0 - sg)))
            dvl = da * (gt * sg)
            d_ref[:, gcols] = dgt.astype(BF16)
            d_ref[:, vcols] = dvl.astype(BF16)
            db_ref[:, gcols] += _colsum8(dgt)
            db_ref[:, vcols] += _colsum8(dvl)

    return pl.pallas_call(
        body, name=name, grid=(2, s // tm),
        in_specs=[pl.BlockSpec((tm, D), lambda j, i: (i, 0)), pl.BlockSpec((CF, D), lambda j, i: (j, 0)),
                  pl.BlockSpec((tm, 2 * CF), lambda j, i: (i, j))],
        out_specs=[pl.BlockSpec((tm, 2 * CF), lambda j, i: (i, j)), pl.BlockSpec((SUBLANES, 2 * CF), lambda j, i: (0, j))],
        out_shape=[jax.ShapeDtypeStruct((s, 2 * DFF), BF16), jax.ShapeDtypeStruct((SUBLANES, 2 * DFF), F32)],
        compiler_params=_params("parallel", "arbitrary"),
    )(dffn, w_down, conv)


def _conv_bwd_up_bwd(dc, up, cw, w_up, *, tm, name):
    s = up.shape[0]
    hb = tm // HALO
    nb = s // HALO
    nk = 2 * DFF // CF
    n = s // tm

    def body(d_ref, dn_ref, u_ref, cw_ref, w_ref, du_ref, dh_ref, dw_ref, acc, dwacc):
        i, k = pl.program_id(0), pl.program_id(1)

        @pl.when(jnp.logical_and(i == 0, k == 0))
        def _():
            dwacc[...] = jnp.zeros_like(dwacc)

        @pl.when(k == 0)
        def _():
            acc[...] = jnp.zeros_like(acc)

        live_next = i < n - 1
        part = None
        for c0, cwid in _sub_chunks(CF):
            cols = slice(c0, c0 + cwid)
            dcur = d_ref[:, cols].astype(F32)
            de = jnp.concatenate([dcur, jnp.where(live_next, dn_ref[:, cols].astype(F32), 0.0)], axis=0)
            d1 = _shift_rows(de, -1)[:tm]
            d2 = _shift_rows(de, -2)[:tm]
            du = (cw_ref[2:3, cols] * dcur + cw_ref[1:2, cols] * d1 + cw_ref[0:1, cols] * d2).astype(BF16)
            du_ref[:, cols] = du
            prod = _dot(du, w_ref[cols, :])
            part = prod if part is None else part + prod
            u = u_ref[:, cols].astype(F32)
            for tap, dsh in ((2, dcur), (1, d1), (0, d2)):
                dwacc[k, SUBLANES * tap:SUBLANES * (tap + 1), cols] += _colsum8(dsh * u)
        acc[...] += part

        @pl.when(k == nk - 1)
        def _():
            dh_ref[...] = acc[...].astype(dh_ref.dtype)

        @pl.when(jnp.logical_and(i == n - 1, k == nk - 1))
        def _():
            dw_ref[...] = dwacc[...]

    res = pl.pallas_call(
        body, name=name, grid=(n, nk),
        in_specs=[pl.BlockSpec((tm, CF), lambda i, k: (i, k)),
                  pl.BlockSpec((HALO, CF), lambda i, k: (jnp.minimum((i + 1) * hb, nb - 1), k)),
                  pl.BlockSpec((tm, CF), lambda i, k: (i, k)),
                  pl.BlockSpec((3, CF), lambda i, k: (0, k)),
                  pl.BlockSpec((CF, D), lambda i, k: (k, 0))],
        out_specs=[pl.BlockSpec((tm, CF), lambda i, k: (i, k)), pl.BlockSpec((tm, D), lambda i, k: (i, 0)),
                   _full((nk, 24, CF))],
        out_shape=[jax.ShapeDtypeStruct((s, 2 * DFF), BF16), jax.ShapeDtypeStruct((s, D), BF16),
                   jax.ShapeDtypeStruct((nk, 24, CF), F32)],
        scratch_shapes=[pltpu.VMEM((tm, D), F32), pltpu.VMEM((nk, 24, CF), F32)],
        compiler_params=_params("arbitrary", "arbitrary"),
    )(dc, dc, up, cw, w_up)
    return res[0], res[1], jnp.transpose(res[2], (1, 0, 2)).reshape(24, 2 * DFF)


def _ln_mod_bwd(dh, xin, g, scale, resid, extra, gate, *, ts, name, side=None):
    s = xin.shape[0]
    row = pl.BlockSpec((ts, D), lambda i: (i, 0))
    acc8 = pl.BlockSpec((SUBLANES, D), lambda i: (0, 0))
    with_gate = extra is not None

    def body(*refs):
        if with_gate:
            dh_ref, x_ref, g_ref, sc_ref, r_ref, e_ref, gt_ref, dx_ref, da_ref, dsh, dsc, dg, dgt = refs
        else:
            dh_ref, x_ref, g_ref, sc_ref, r_ref, dx_ref, dsh, dsc, dg = refs
        i = pl.program_id(0)

        @pl.when(i == 0)
        def _():
            for acc in (dsh, dsc, dg) + ((dgt,) if with_gate else ()):
                acc[...] = jnp.zeros_like(acc)

        xv, dhv = x_ref[...], dh_ref[...].astype(F32)
        r = lax.rsqrt(jnp.mean(xv * xv, axis=-1, keepdims=True) + EPS)
        xn = xv * r
        dsh[...] += _colsum8(dhv)
        dsc[...] += _colsum8(dhv * (xn * g_ref[...]))
        dhp = dhv * (1.0 + sc_ref[...])
        dg[...] += _colsum8(dhp * xn)
        dxn = dhp * g_ref[...]
        dx = r_ref[...] + r * (dxn - xn * jnp.mean(dxn * xn, axis=-1, keepdims=True))
        dx_ref[...] = dx
        if with_gate:
            da_ref[...] = (dx * gt_ref[...]).astype(BF16)
            dgt[...] += _colsum8(dx * e_ref[...].astype(F32))

    f32o, p8 = jax.ShapeDtypeStruct((s, D), F32), jax.ShapeDtypeStruct((SUBLANES, D), F32)
    if with_gate:
        ins, in_specs = (dh, xin, g, scale, resid, extra, gate), [row, row, _vec(D), _vec(D), row, row, _vec(D)]
        out_specs, out_shape = [row, row, acc8, acc8, acc8, acc8], [f32o, jax.ShapeDtypeStruct((s, D), BF16), p8, p8, p8, p8]
    else:
        ins, in_specs = (dh, xin, g, scale, resid), [row, row, _vec(D), _vec(D), row]
        out_specs, out_shape = [row, acc8, acc8, acc8], [f32o, p8, p8, p8]
    return _row_call(body, side, name=name, steps=s // ts, in_specs=in_specs, out_specs=out_specs,
                     out_shape=out_shape, ins=ins)


def _group_norm_bwd(t, dn_out, gvec, bd):
    r = _group_rsqrt(t, bd)
    dg_terms = dn_out * t * r
    dn = dn_out * gvec
    dt = r * (dn - t * (r * r) * (_split_dot(dn * t, bd) * (1.0 / HD)))
    return dt, dg_terms


def _mix_bwd(dmix, proj, o, pw, pb, ps, ag, bd, *, ts, name, side=None):
    s = proj.shape[0]
    hb = ts // HALO
    nb = s // HALO

    def body(dm_ref, dmn_ref, u_ref, uh_ref, o_ref, pw_ref, pb_ref, ps_ref, ag_ref, bd_ref,
             du_ref, do_ref, dpw_ref, dpb_ref, dps_ref, dag_ref):
        i = pl.program_id(0)
        n = s // ts

        @pl.when(i == 0)
        def _():
            for acc in (dpw_ref, dpb_ref, dps_ref, dag_ref):
                acc[...] = jnp.zeros_like(acc)

        for g, w in enumerate(POOL_WINDOWS):
            cols = slice(g * LANES, (g + 1) * LANES)
            wg = pw_ref[g]
            psg = ps_ref[:, cols]
            pooled = _pooled(u_ref[:, cols], uh_ref[:, cols], i, g, w, ts).astype(BF16)
            dy = dm_ref[:, cols].astype(F32)
            dps_ref[:, cols] += _colsum8(dy * (_dot(pooled, wg) + pb_ref[:, cols]))
            dpre = dy * psg
            dpb_ref[:, cols] += _colsum8(dpre)
            dpreb = dpre.astype(BF16)
            dpw_ref[g * LANES:(g + 1) * LANES, :] += _dot(pooled, dpreb, TN)
            dpool = _dot(dpreb, wg, NT)
            dnext = _dot((dmn_ref[:, cols].astype(F32) * psg).astype(BF16), wg, NT)
            dpe = jnp.concatenate([dpool, jnp.where(i < n - 1, dnext, 0.0)], axis=0)
            tpos = i * ts + lax.broadcasted_iota(jnp.int32, (ts + HALO, 1), 0)
            acc = dpe / jnp.minimum(tpos + 1, w).astype(F32)
            span = 1
            while span < w:
                acc = acc + _shift_rows(acc, -span)
                span *= 2
            du_ref[:, cols] = acc[:ts] - dpool
        ov = o_ref[...]
        dov, dg_terms = _group_norm_bwd(ov, dm_ref[:, DP:].astype(F32), ag_ref[...], bd_ref[...])
        do_ref[...] = dov
        dag_ref[...] += _colsum8(dg_terms)

    p8 = jax.ShapeDtypeStruct((SUBLANES, DP), F32)
    acc8 = pl.BlockSpec((SUBLANES, DP), lambda i: (0, 0))
    half = pl.BlockSpec((ts, DP), lambda i: (i, 0))
    return _row_call(
        body, side, name=name, steps=s // ts,
        in_specs=[pl.BlockSpec((ts, D), lambda i: (i, 0)),
                  pl.BlockSpec((HALO, DP), lambda i: (jnp.minimum((i + 1) * hb, nb - 1), 0)),
                  half, pl.BlockSpec((HALO, DP), lambda i: (jnp.maximum(i * hb - 1, 0), 0)),
                  half, _full((4, LANES, LANES)), _vec(DP), _vec(DP), _vec(DA), _full((DA, DA))],
        out_specs=[half, half, _full((DP, LANES)), acc8, acc8, acc8],
        out_shape=[jax.ShapeDtypeStruct((s, DP), F32), jax.ShapeDtypeStruct((s, DA), F32),
                   jax.ShapeDtypeStruct((DP, LANES), F32), p8, p8, p8],
        ins=(dmix, dmix, proj, proj, o, pw, pb, ps, ag, bd))


def _qk_norm_bwd(du, dq, dk, dv, proj, qg, kg, bd, *, ts, name):
    s = proj.shape[0]

    def body(du_ref, dq_ref, dk_ref, dv_ref, q_ref, k_ref, qg_ref, kg_ref, bd_ref, dp_ref, dqg_ref, dkg_ref):
        i = pl.program_id(0)

        @pl.when(i == 0)
        def _():
            dqg_ref[...] = jnp.zeros_like(dqg_ref)
            dkg_ref[...] = jnp.zeros_like(dkg_ref)

        bdv = bd_ref[...]
        dqr, tq = _group_norm_bwd(q_ref[...], dq_ref[...], qg_ref[...], bdv)
        dkr, tk = _group_norm_bwd(k_ref[...], dk_ref[...], kg_ref[...], bdv)
        dqg_ref[...] += _colsum8(tq)
        dkg_ref[...] += _colsum8(tk)
        dp_ref[:, 0:DP] = du_ref[...].astype(BF16)
        dp_ref[:, DP:DP + DA] = dqr.astype(BF16)
        dp_ref[:, DP + DA:DP + 2 * DA] = dkr.astype(BF16)
        dp_ref[:, DP + 2 * DA:] = dv_ref[...].astype(BF16)

    half = pl.BlockSpec((ts, DA), lambda i: (i, 0))
    col = lambda j: pl.BlockSpec((ts, DA), lambda i: (i, j))
    acc8 = pl.BlockSpec((SUBLANES, DA), lambda i: (0, 0))
    p8 = jax.ShapeDtypeStruct((SUBLANES, DA), F32)
    return pl.pallas_call(
        body, name=name, grid=(s // ts,),
        in_specs=[half, half, half, half, col(1), col(2), _vec(DA), _vec(DA), _full((DA, DA))],
        out_specs=[pl.BlockSpec((ts, DIN), lambda i: (i, 0)), acc8, acc8],
        out_shape=[jax.ShapeDtypeStruct((s, DIN), BF16), p8, p8],
        compiler_params=_params("arbitrary"),
    )(du, dq, dk, dv, proj, proj, qg, kg, bd)


def _split3(a):
    hi = a.astype(BF16)
    return hi, (a - hi.astype(F32)).astype(BF16)


def _dot3(a, b, dn):
    ah, al = _split3(a)
    bh, bl = _split3(b)
    return _dot(ah, bh, dn) + (_dot(ah, bl, dn) + _dot(al, bh, dn))


def _ada_fwd(c_all, w, b, name):
    nw = w.shape[1]

    def body(c_ref, w_ref, b_ref, o_ref):
        cv = c_ref[...]
        act = cv / (1.0 + jnp.exp(-cv))
        o_ref[...] = _dot3(act, w_ref[...], NN) + b_ref[...]

    return pl.pallas_call(
        body, name=name, in_specs=[_full((NDEV, D)), _full(w.shape), _full((1, nw))], out_specs=_full((NDEV, nw)),
        out_shape=jax.ShapeDtypeStruct((NDEV, nw), F32), grid=(1,), compiler_params=_params("arbitrary"),
    )(c_all, w, b)


def _ada_bwd(c_all, dmod, name):
    nw = dmod.shape[1]

    def body(c_ref, d_ref, o_ref):
        cv = c_ref[...]
        act = cv / (1.0 + jnp.exp(-cv))
        o_ref[...] = _dot3(act, d_ref[...], TN)[None]

    return pl.pallas_call(
        body, name=name, in_specs=[_full((NDEV, D)), _full((NDEV, nw))], out_specs=_full((1, D, nw)),
        out_shape=jax.ShapeDtypeStruct((1, D, nw), F32), grid=(1,), compiler_params=_params("arbitrary"),
    )(c_all, dmod)


def _fold_heads(v):
    acc = v[:, 0:HD]
    for h in range(1, DA // HD):
        acc = acc + v[:, h * HD:(h + 1) * HD]
    return acc


def _pack_partials(pieces, dcw_p, name):
    n_p = len(pieces)
    total = sum(p.shape[1] for p in pieces) + 3 * dcw_p.shape[1]
    npack = -(-total // (SUBLANES * LANES)) * (SUBLANES * LANES)

    def body(*refs):
        out = refs[-1]
        off = 0
        for r in refs[:n_p]:
            out[:, off:off + r.shape[1]] = jnp.sum(r[...], axis=0, keepdims=True)
            off += r.shape[1]
        dw = refs[n_p]
        for tap in range(3):
            out[:, off:off + dw.shape[1]] = jnp.sum(dw[SUBLANES * tap:SUBLANES * (tap + 1), :], axis=0, keepdims=True)
            off += dw.shape[1]
        if off < npack:
            out[:, off:] = jnp.zeros((1, npack - off), F32)

    arrs = list(pieces) + [dcw_p]
    return pl.pallas_call(
        body, name=name, grid=(1,), in_specs=[_full(a.shape) for a in arrs], out_specs=_full((1, npack)),
        out_shape=jax.ShapeDtypeStruct((1, npack), F32), compiler_params=_params("arbitrary"),
    )(*arrs)


def _small_update(gathered, gathered_pw, gathered_cw, specs, params, loss_off, name):
    names = [sp[0] for sp in specs]
    flat = []
    for nme in names + ["pool_w", "conv_w"]:
        flat += list(params[nme])
    n_in = len(flat)

    def body(*refs):
        ga_ref, gp_ref, gc_ref = refs[0], refs[1], refs[2]
        prm = refs[3:3 + n_in]
        outs = refs[3 + n_in:]
        total = ga_ref[0:1, :]
        for dv in range(1, NDEV):
            total = total + ga_ref[dv:dv + 1, :]
        k = 0
        for idx, (nme, off, width, fold) in enumerate(specs):
            g = total[:, off:off + width]
            if fold:
                g = _fold_heads(g)
            w_ref, m_ref, v_ref = prm[3 * idx:3 * idx + 3]
            d, nm, nv = _adamw_math(w_ref[...], g, m_ref[...], v_ref[...])
            for val in (g, d, nm, nv):
                outs[k][...] = val
                k += 1
        gpw = gp_ref[0]
        for dv in range(1, NDEV):
            gpw = gpw + gp_ref[dv]
        w_ref, m_ref, v_ref = prm[3 * len(specs):3 * len(specs) + 3]
        d, nm, nv = _adamw_math(w_ref[...], gpw, m_ref[...], v_ref[...])
        for val in (gpw, d, nm, nv):
            outs[k][...] = val
            k += 1
        gcw = gc_ref[0]
        for dv in range(1, NDEV):
            gcw = gcw + gc_ref[dv]
        w_ref, m_ref, v_ref = prm[3 * len(specs) + 3:3 * len(specs) + 6]
        d, nm, nv = _adamw_math(w_ref[...], gcw, m_ref[...], v_ref[...])
        for val in (gcw, d, nm, nv):
            outs[k][...] = val
            k += 1
        outs[k][...] = ga_ref[:, 0:6 * D]
        outs[k + 1][...] = total[:, loss_off:loss_off + LANES] * (1.0 / SUBLANES)

    out_shape, out_specs = [], []
    for nme in names + ["pool_w", "conv_w"]:
        shp = params[nme][0].shape
        out_shape += [jax.ShapeDtypeStruct(shp, F32)] * 4
        out_specs += [_full(shp)] * 4
    out_shape += [jax.ShapeDtypeStruct((NDEV, 6 * D), F32), jax.ShapeDtypeStruct((1, LANES), F32)]
    out_specs += [_full((NDEV, 6 * D)), _full((1, LANES))]
    res = pl.pallas_call(
        body, name=name, grid=(1,),
        in_specs=[_full(gathered.shape), _full(gathered_pw.shape), _full(gathered_cw.shape)] + [_full(a.shape) for a in flat],
        out_specs=out_specs, out_shape=out_shape, compiler_params=_params("arbitrary"),
    )(gathered, gathered_pw, gathered_cw, *flat)
    out = {nme: tuple(res[4 * i:4 * i + 4]) for i, nme in enumerate(names + ["pool_w", "conv_w"])}
    return out, res[-2], res[-1][0, 0]


def _row_tile(s):
    return 512 if s % 512 == 0 else s


def kernel(x, c, ada_w, ada_b, norm1_g, w_in, pool_w, pool_b, pool_scale, q_norm_g, k_norm_g, attn_out_g, w_out, norm2_g, w_up, conv_w, conv_b, w_down, loss_target, m_ada_w, m_ada_b, m_norm1_g, m_w_in, m_pool_w, m_pool_b, m_pool_scale, m_q_norm_g, m_k_norm_g, m_attn_out_g, m_w_out, m_norm2_g, m_w_up, m_conv_w, m_conv_b, m_w_down, v_ada_w, v_ada_b, v_norm1_g, v_w_in, v_pool_w, v_pool_b, v_pool_scale, v_q_norm_g, v_k_norm_g, v_attn_out_g, v_w_out, v_norm2_g, v_w_up, v_conv_w, v_conv_b, v_w_down):
    ax, ay, ac = lax.axis_index("x"), lax.axis_index("y"), lax.axis_index("c")
    me = 4 * ax + 2 * ay + ac
    me_swapped = 4 * ay + 2 * ax + ac
    xs, tgt = x[0], loss_target[0]
    s = xs.shape[0]
    ts = _row_tile(s)
    tq_attn, tk_attn, hp_attn = 256, 256, 2
    tmm = 2 * ts
    bd = _block_diag_ones(DA, HD)

    w_in_t = w_in[0].T.astype(BF16)
    w_up_t = w_up[0].T.astype(BF16)
    c_all = _all_gather_small(jnp.broadcast_to(c, (SUBLANES, D)), "gather_c")[:, 0, :]
    n_ada = ada_w.shape[2]
    ada_b_mine = lax.dynamic_slice_in_dim(ada_b, me * n_ada, n_ada, axis=1)
    mod_part = _ada_fwd(c_all, ada_w[0], ada_b_mine, "ada_fwd")
    mod_all = _all_gather_small(mod_part, "gather_mod")
    mod = lax.dynamic_index_in_dim(mod_all, me, axis=1, keepdims=False).reshape(1, 6 * D)
    shift1, scale1, gate1, shift2, scale2, gate2 = [mod[:, k * D:(k + 1) * D] for k in range(6)]

    later_w = [w_out[0].astype(BF16), w_up_t, w_down[0].astype(BF16)]
    cb_full = jnp.transpose(conv_b.reshape(1, 2, 2, 2, 704), (0, 2, 1, 3, 4)).reshape(1, 2 * DFF)

    qg = jnp.tile(q_norm_g, (1, DA // HD))
    kg = jnp.tile(k_norm_g, (1, DA // HD))
    ag = attn_out_g.reshape(1, DA)
    pw = pool_w[0].astype(BF16)
    pb = pool_b.reshape(1, DP)
    h1, (gw_in, gcw) = _ln_mod(xs, norm1_g, scale1, shift1, ts=tmm, name="ln1",
                               side=_gather_side([w_in_t, jnp.pad(conv_w[0], ((0, 5), (0, 64)))], [False, True]))
    w_in_full = gw_in.reshape(DIN, D)
    cw_full = jnp.transpose(gcw[:, :3, :704], (1, 0, 2)).reshape(3, 2 * DFF)
    proj, qkv = _in_proj_qk_norm(h1, w_in_full, qg, kg, bd, tm=tmm, name="in_proj_qk_norm")
    o_raw, m_tot, kb_first, (gw_out, gw_up, gw_down) = _attn_fwd(
        qkv, later_w, [False, True, False], tq=tq_attn, tk=tk_attn, hp=hp_attn, name="attn_fwd")
    w_out_full = gw_out.reshape(D, D)
    w_up_full = gw_up.reshape(2 * DFF, D)
    w_down_full = gw_down.reshape(DFF, D)
    mix = _pool_mix(proj, o_raw, pw, pb, pool_scale, ag, bd, ts=tmm, name="pool_mix")
    att, x1, h2 = _proj_res_ln_mod(mix, w_out_full, xs, gate1, norm2_g, scale2, shift2, tm=tmm, name="out_proj_ln2")
    up, conv, act = _up_conv_gate(h2, w_up_full, cw_full, cb_full, tm=tmm, name="up_conv_gate")
    dy, dffn, dgate2_p, loss_p = _proj_loss_head(act, w_down_full, x1, tgt, gate2, tm=ts, name="down_proj_loss")

    g_w_down = _matmul(act, dffn, mode="tn", out_dtype=F32, tm=CF, tn=D, tk=2 * tmm, name="down_wgrad")
    dconv, dcb_p = _down_bwd_gate(dffn, w_down_full, conv, tm=tmm, name="down_bwd_gate")
    dup, dh2, dcw_p = _conv_bwd_up_bwd(dconv, up, cw_full, w_up_full, tm=ts, name="conv_bwd_up_bwd")
    g_w_up_t = _matmul(dup, h2, mode="tn", out_dtype=F32, tm=CF, tn=D, tk=2 * tmm, name="up_wgrad")
    (dx1, datt, dshift2_p, dscale2_p, dnorm2_p, dgate1_p), _ = _ln_mod_bwd(
        dh2, x1, norm2_g, scale2, dy, att, gate1, ts=tmm, name="ln2_bwd")

    dmix = _matmul(datt, w_out_full, mode="nt", out_dtype=BF16, tm=tmm,tn=D, tk=D, name="out_bwd")
    g_w_out = _matmul(mix, datt, mode="tn", out_dtype=F32, tm=D, tn=D, tk=2 * tmm, name="out_wgrad")
    core = jnp.reshape(ac, (1,)).astype(jnp.int32)
    chip = jnp.reshape(2 * ax + ay, (1,)).astype(jnp.int32)
    big_ffn = [g_w_up_t.reshape(NDEV, 2 * DFF // NDEV, D), g_w_down.reshape(NDEV, DFF // NDEV, D),
               g_w_out.reshape(NDEV, D // NDEV, D)]
    swaps_ffn = [True, False, False]
    (du, do_raw, g_pw_p, dpb_p, dps_p, dag_p), gots_ffn = _mix_bwd(
        dmix, proj, o_raw, pw, pb, pool_scale, ag, bd, ts=tmm, name="mix_bwd", side=_pair_side(big_ffn, swaps_ffn))
    sums_ffn = [_pair_sum(big_ffn[k], gots_ffn[k], swaps_ffn[k], core, "rs_pair_sum_ffn%d" % k) for k in range(3)]
    dqn, dkn, dvv, parts_ffn = _attn_bwd(qkv, do_raw, m_tot, kb_first, sums_ffn, tq=tq_attn, tk=tk_attn, hp=hp_attn, name="attn_bwd")
    dproj, dqg_p, dkg_p = _qk_norm_bwd(du, dqn, dkn, dvv, proj, qg, kg, bd, ts=tmm, name="qk_norm_bwd")
    g_w_in_t = _matmul(dproj, h1, mode="tn", out_dtype=F32, tm=DIN // 2, tn=D, tk=2 * tmm, name="in_wgrad")
    big = [g_w_in_t.reshape(NDEV, DIN // NDEV, D)]
    gots = _pair_exchange(big, [False], "rs_pair")
    sums = [_pair_sum(big[0], gots[0], False, core, "rs_pair_sum")]
    dh1, parts = _matmul(dproj, w_in_full, mode="nn", out_dtype=BF16, tm=tmm,tn=D, tk=DIN, name="in_bwd",
                         side=_chip_side(sums))
    (grad_x, dshift1_p, dscale1_p, dnorm1_p), _ = _ln_mod_bwd(
        dh1, xs, norm1_g, scale1, dx1, None, None, ts=tmm, name="ln1_bwd")

    tr = lambda a: a[0].T
    r_in = _adamw_reduce(tr(w_in), tr(m_w_in), tr(v_w_in), sums[0], parts[0], chip, "adamw_w_in")
    r_out = _adamw_reduce(w_out[0], m_w_out[0], v_w_out[0], sums_ffn[2], parts_ffn[2], chip, "adamw_w_out")
    r_up = _adamw_reduce(tr(w_up), tr(m_w_up), tr(v_w_up), sums_ffn[0], parts_ffn[0], chip, "adamw_w_up")
    r_down = _adamw_reduce(w_down[0], m_w_down[0], v_w_down[0], sums_ffn[1], parts_ffn[1], chip, "adamw_w_down")
    r_in = [a.T[None] for a in r_in]
    r_up = [a.T[None] for a in r_up]
    r_out = [a[None] for a in r_out]
    r_down = [a[None] for a in r_down]

    dcb_nat = jnp.transpose(dcb_p.reshape(SUBLANES, 2, 2, 2, 704), (0, 2, 1, 3, 4)).reshape(SUBLANES, 2 * DFF)
    pieces = [dshift1_p, dscale1_p, dgate1_p, dshift2_p, dscale2_p, dgate2_p,
              dnorm1_p, dnorm2_p, dcb_nat, dpb_p, dps_p, dag_p, dqg_p, dkg_p, loss_p]
    n_vec = sum(p.shape[1] for p in pieces)
    packed = _pack_partials(pieces, dcw_p, "pack_partials")
    npack = packed.shape[1]
    gathered, gathered_pw = _all_gather([packed.reshape(SUBLANES, npack // SUBLANES), g_pw_p], [False, False], "gather_small")
    gathered = gathered.reshape(NDEV, npack)
    gathered_cw = lax.dynamic_index_in_dim(
        gathered[:, n_vec:n_vec + 6 * DFF].reshape(NDEV, 3, NDEV, 704), me_swapped, axis=2, keepdims=False)
    specs = [("ada_b", 0, 6 * D, False)]
    off = 6 * D
    for nme, width, fold in (("norm1_g", D, False), ("norm2_g", D, False), ("conv_b", 2 * DFF, False),
                             ("pool_b", DP, False), ("pool_scale", DP, False), ("attn_out_g", DA, False),
                             ("q_norm_g", DA, True), ("k_norm_g", DA, True)):
        specs.append((nme, off, width, fold))
        off += width
    small = {
        "ada_b": (ada_b, m_ada_b, v_ada_b),
        "norm1_g": (norm1_g, m_norm1_g, v_norm1_g), "norm2_g": (norm2_g, m_norm2_g, v_norm2_g),
        "conv_b": (conv_b, m_conv_b, v_conv_b),
        "pool_b": (pb, m_pool_b.reshape(1, DP), v_pool_b.reshape(1, DP)),
        "pool_scale": (pool_scale, m_pool_scale, v_pool_scale),
        "attn_out_g": (ag, m_attn_out_g.reshape(1, DA), v_attn_out_g.reshape(1, DA)),
        "q_norm_g": (q_norm_g, m_q_norm_g, v_q_norm_g), "k_norm_g": (k_norm_g, m_k_norm_g, v_k_norm_g),
        "pool_w": (pool_w.reshape(DP, LANES), m_pool_w.reshape(DP, LANES), v_pool_w.reshape(DP, LANES)),
        "conv_w": (conv_w[0], m_conv_w[0], v_conv_w[0]),
    }
    upd, dmod_all, loss = _small_update(gathered, gathered_pw, gathered_cw, specs, small, off, "small_update")
    g_ada_w = _ada_bwd(c_all, lax.dynamic_slice_in_dim(dmod_all, me * n_ada, n_ada, axis=1), "ada_bwd")
    r_ada = [g_ada_w] + [a[None] for a in _adamw(ada_w[0], m_ada_w[0], v_ada_w[0], g_ada_w[0], "adamw_ada_w")]

    shapes = {"ada_b": ada_b.shape, "norm1_g": norm1_g.shape, "pool_w": pool_w.shape, "pool_b": pool_b.shape,
              "pool_scale": pool_scale.shape, "q_norm_g": q_norm_g.shape, "k_norm_g": k_norm_g.shape,
              "attn_out_g": attn_out_g.shape, "norm2_g": norm2_g.shape, "conv_w": conv_w.shape, "conv_b": conv_b.shape}
    res = {nme: [a.reshape(shapes[nme]) for a in upd[nme]] for nme in shapes}
    res.update(ada_w=r_ada, w_in=r_in, w_out=r_out, w_up=r_up, w_down=r_down)
    names = ["ada_w", "ada_b", "norm1_g", "w_in", "pool_w", "pool_b", "pool_scale", "q_norm_g", "k_norm_g",
             "attn_out_g", "w_out", "norm2_g", "w_up", "conv_w", "conv_b", "w_down"]
    outs = [loss, grad_x[None]]
    for q in range(4):
        outs += [res[nme][q] for nme in names]
    return tuple(outs)
```

```python
import functools
import math

import numpy as np
import jax
import jax.numpy as jnp
from jax import lax
from jax.experimental import pallas as pl
from jax.experimental.pallas import tpu as pltpu

F32, BF16 = jnp.float32, jnp.bfloat16
D = 1024
DP = 512
DA = 512
HD = 64
DIN = DP + 3 * DA
DFF = 2816
POOL_WINDOWS = (2, 4, 8, 16)
HALO = 16
EPS = 1e-6
LANES = 128
SUBLANES = 8
NDEV = 8
VMEM_LIMIT = 56 * 1024 * 1024
MESH = pl.DeviceIdType.MESH

ADAM_LR, ADAM_B1, ADAM_B2, ADAM_EPS, ADAM_WD, ADAM_STEP = 0.001, 0.9, 0.999, 1e-08, 0.01, 10

NN = (((1,), (0,)), ((), ()))
NT = (((1,), (1,)), ((), ()))
TN = (((0,), (0,)), ((), ()))


def _params(*sem):
    return pltpu.CompilerParams(dimension_semantics=sem, vmem_limit_bytes=VMEM_LIMIT)


def _full(shape):
    nd = len(shape)
    return pl.BlockSpec(shape, lambda *_: (0,) * nd)


def _dot(a, b, dn=NN):
    return lax.dot_general(a, b, dn, preferred_element_type=F32)


def _split_dot(a, b, dn=NN):
    hi = a.astype(BF16)
    lo = (a - hi.astype(F32)).astype(BF16)
    return _dot(hi, b, dn) + _dot(lo, b, dn)


def _colsum8(v):
    r, n = v.shape
    return v.reshape(r // SUBLANES, SUBLANES, n).sum(axis=0)


def _block_diag_ones(n, blk):
    i = np.arange(n) // blk
    return jnp.asarray((i[:, None] == i[None, :]).astype(np.float32), BF16)


def _matmul(a, b, *, mode, out_dtype, tm, tn, tk, name, n_outer=False, side=None):
    if mode == "tn":
        K, M = a.shape
        N = b.shape[1]
    elif mode == "nt":
        M, K = a.shape
        N = b.shape[0]
    else:
        M, K = a.shape
        N = b.shape[1]
    tm, tn, tk = min(tm, M), min(tn, N), min(tk, K)
    assert M % tm == 0 and N % tn == 0 and K % tk == 0, (name, M, N, K, tm, tn, tk)
    nk = K // tk
    dn = {"nn": NN, "nt": NT, "tn": TN}[mode]

    def body(a_ref, b_ref, o_ref, *acc):
        if nk == 1:
            o_ref[...] = _dot(a_ref[...], b_ref[...], dn).astype(o_ref.dtype)
            return
        acc_ref, = acc
        k = pl.program_id(2)

        @pl.when(k == 0)
        def _():
            acc_ref[...] = jnp.zeros_like(acc_ref)

        acc_ref[...] += _dot(a_ref[...], b_ref[...], dn)

        @pl.when(k == nk - 1)
        def _():
            o_ref[...] = acc_ref[...].astype(o_ref.dtype)

    if n_outer:
        gi = lambda g: (g[1], g[0], g[2])
        grid = (N // tn, M // tm, nk)
    else:
        gi = lambda g: g
        grid = (M // tm, N // tn, nk)

    def amap(*g):
        i, j, k = gi(g)
        return (k, i) if mode == "tn" else (i, k)

    def bmap(*g):
        i, j, k = gi(g)
        return (j, k) if mode == "nt" else (k, j)

    def omap(*g):
        i, j, k = gi(g)
        return (i, j)

    a_blk = (tk, tm) if mode == "tn" else (tm, tk)
    b_blk = (tn, tk) if mode == "nt" else (tk, tn)
    acc_scratch = [] if nk == 1 else [pltpu.VMEM((tm, tn), F32)]
    if side is None:
        return pl.pallas_call(
            body, name=name, grid=grid,
            in_specs=[pl.BlockSpec(a_blk, amap), pl.BlockSpec(b_blk, bmap)],
            out_specs=pl.BlockSpec((tm, tn), omap),
            out_shape=jax.ShapeDtypeStruct((M, N), out_dtype),
            scratch_shapes=acc_scratch,
            compiler_params=_params("parallel", "parallel", "arbitrary"),
        )(a, b)

    ne = len(side.arrs)
    steps = grid[0] * grid[1] * grid[2]

    nsem = len(side.scratch)

    def with_side(*refs):
        e_in, e_out = refs[2:2 + ne], refs[3 + ne:3 + 2 * ne]
        sems = refs[len(refs) - nsem:]
        step = (pl.program_id(0) * grid[1] + pl.program_id(1)) * grid[2] + pl.program_id(2)

        @pl.when(step == 0)
        def _():
            side.start(e_in, e_out, *sems)

        body(refs[0], refs[1], refs[2 + ne], *refs[3 + 2 * ne:len(refs) - nsem])

        @pl.when(step == steps - 1)
        def _():
            side.finish(e_in, e_out, *sems)

    any_spec = pl.BlockSpec(memory_space=pl.ANY)
    res = pl.pallas_call(
        with_side, name=name, grid=grid,
        in_specs=[pl.BlockSpec(a_blk, amap), pl.BlockSpec(b_blk, bmap)] + [any_spec] * ne,
        out_specs=[pl.BlockSpec((tm, tn), omap)] + [any_spec] * ne,
        out_shape=[jax.ShapeDtypeStruct((M, N), out_dtype)] + side.out_shapes,
        scratch_shapes=acc_scratch + side.scratch,
        compiler_params=_params("arbitrary", "arbitrary", "arbitrary"),
    )(a, b, *side.arrs)
    return res[0], list(res[1:])


def _slot(swap, px, py, pc):
    return 4 * py + 2 * px + pc if swap else 4 * px + 2 * py + pc


class _Gather:
    def __init__(self, ins, outs, send, recv, loc, swaps):
        self.ins, self.outs, self.send, self.recv, self.loc, self.swaps = ins, outs, send, recv, loc, swaps
        x, y, c = lax.axis_index("x"), lax.axis_index("y"), lax.axis_index("c")
        self.me, self.sib = (x, y, c), (x, y, 1 - c)
        self.chips = [(1 - x, y), (x, 1 - y), (1 - x, 1 - y)]
        self.n = len(ins)

    @staticmethod
    def scratch(n):
        return [pltpu.SemaphoreType.DMA((7 * n,)), pltpu.SemaphoreType.DMA((7 * n,)), pltpu.SemaphoreType.DMA((n,))]

    def copy(self, a, k, blk, to, src=None):
        rows = self.outs[a].at[_slot(self.swaps[a], *blk)]
        return pltpu.make_async_remote_copy(
            src_ref=rows if src is None else src, dst_ref=rows,
            send_sem=self.send.at[7 * a + k], recv_sem=self.recv.at[7 * a + k], device_id=to, device_id_type=MESH)

    def mine(self, a):
        return pltpu.make_async_copy(self.ins[a], self.outs[a].at[_slot(self.swaps[a], *self.me)], self.loc.at[a])

    def first(self, a):
        c = self.me[2]
        return [self.copy(a, 0, self.me, self.sib, src=self.ins[a])] + [
            self.copy(a, 1 + j, self.me, (*chip, c), src=self.ins[a]) for j, chip in enumerate(self.chips)]

    def forwards(self, a):
        c = self.me[2]
        return [self.copy(a, 4 + j, (*chip, c), self.sib) for j, chip in enumerate(self.chips)]

    def start(self):
        for a in range(self.n):
            self.mine(a).start()
        for a in range(self.n):
            for cp in self.first(a):
                cp.start()

    def forward(self):
        c = self.me[2]
        for a in range(self.n):
            fwd = self.forwards(a)
            for j, chip in enumerate(self.chips):
                self.copy(a, 1 + j, (*chip, c), self.me).wait_recv()
                fwd[j].start()

    def finish(self):
        c = self.me[2]
        for a in range(self.n):
            self.copy(a, 0, self.sib, self.me).wait_recv()
            for j, chip in enumerate(self.chips):
                self.copy(a, 4 + j, (*chip, 1 - c), self.me).wait_recv()
        for a in range(self.n):
            for cp in self.first(a) + self.forwards(a):
                cp.wait_send()
            self.mine(a).wait()


def _all_gather(arrs, swaps, name):
    n = len(arrs)

    def body(*refs):
        g = _Gather(refs[:n], refs[n:2 * n], *refs[2 * n:], swaps)
        g.start()
        g.forward()
        g.finish()

    any_spec = pl.BlockSpec(memory_space=pl.ANY)
    return pl.pallas_call(
        body, name=name,
        in_specs=[any_spec] * n, out_specs=[any_spec] * n,
        out_shape=[jax.ShapeDtypeStruct((NDEV,) + a.shape, a.dtype) for a in arrs],
        scratch_shapes=_Gather.scratch(n),
    )(*arrs)


def _all_gather_small(arr, name):
    def body(in_ref, out_ref, send, recv, loc):
        x, y, c = lax.axis_index("x"), lax.axis_index("y"), lax.axis_index("c")
        flip = lambda v, bit: 1 - v if bit else v
        peers = [(flip(x, k >> 2 & 1), flip(y, k >> 1 & 1), flip(c, k & 1)) for k in range(1, NDEV)]
        mine = pltpu.make_async_copy(in_ref, out_ref.at[_slot(False, x, y, c)], loc)
        mine.start()

        def copy(k, src_dev, to):
            return pltpu.make_async_remote_copy(
                src_ref=in_ref, dst_ref=out_ref.at[_slot(False, *src_dev)], send_sem=send.at[k], recv_sem=recv.at[k],
                device_id=to, device_id_type=MESH)

        sends = [copy(k, (x, y, c), peer) for k, peer in enumerate(peers)]
        for cp in sends:
            cp.start()
        for k, peer in enumerate(peers):
            copy(k, peer, (x, y, c)).wait_recv()
        for cp in sends:
            cp.wait_send()
        mine.wait()

    any_spec = pl.BlockSpec(memory_space=pl.ANY)
    return pl.pallas_call(
        body, name=name, in_specs=[any_spec], out_specs=any_spec,
        out_shape=jax.ShapeDtypeStruct((NDEV,) + arr.shape, arr.dtype),
        scratch_shapes=[pltpu.SemaphoreType.DMA((NDEV - 1,)), pltpu.SemaphoreType.DMA((NDEV - 1,)), pltpu.SemaphoreType.DMA],
    )(arr)


def _pair_copies(ins, gots, send, recv, swaps):
    x, y, c = lax.axis_index("x"), lax.axis_index("y"), lax.axis_index("c")
    return [pltpu.make_async_remote_copy(
        src_ref=ins[a].at[_slot(swaps[a], k // 2, k % 2, 1 - c)], dst_ref=gots[a].at[k],
        send_sem=send.at[4 * a + k], recv_sem=recv.at[4 * a + k], device_id=(x, y, 1 - c), device_id_type=MESH)
        for a in range(len(ins)) for k in range(4)]


def _pair_exchange(arrs, swaps, name):
    n = len(arrs)

    def body(*refs):
        rems = _pair_copies(refs[:n], refs[n:2 * n], *refs[2 * n:], swaps)
        for rc in rems:
            rc.start()
        for rc in rems:
            rc.wait_recv()
        for rc in rems:
            rc.wait_send()

    any_spec = pl.BlockSpec(memory_space=pl.ANY)
    return pl.pallas_call(
        body, name=name,
        in_specs=[any_spec] * n, out_specs=[any_spec] * n,
        out_shape=[jax.ShapeDtypeStruct((4,) + a.shape[1:], a.dtype) for a in arrs],
        scratch_shapes=[pltpu.SemaphoreType.DMA((4 * n,)), pltpu.SemaphoreType.DMA((4 * n,))],
    )(*arrs)


def _chip_copies(ins, outs, send, recv):
    x, y, c = lax.axis_index("x"), lax.axis_index("y"), lax.axis_index("c")
    chips = [(1 - x, y), (x, 1 - y), (1 - x, 1 - y)]
    return [pltpu.make_async_remote_copy(
        src_ref=ins[a].at[2 * px + py], dst_ref=outs[a].at[j], send_sem=send.at[3 * a + j], recv_sem=recv.at[3 * a + j],
        device_id=(px, py, c), device_id_type=MESH) for a in range(len(ins)) for j, (px, py) in enumerate(chips)]


class _Side:
    def __init__(self, arrs, out_shapes, scratch, start, finish, mid=None):
        self.arrs, self.out_shapes, self.scratch = list(arrs), list(out_shapes), list(scratch)
        self.start, self.finish, self.mid = start, finish, mid


def _copies_side(arrs, out_shapes, n_copies, make):
    def start(ins, outs, *sems):
        for cp in make(ins, outs, *sems):
            cp.start()

    def finish(ins, outs, *sems):
        cps = make(ins, outs, *sems)
        for cp in cps:
            cp.wait_recv()
        for cp in cps:
            cp.wait_send()

    return _Side(arrs, out_shapes, [pltpu.SemaphoreType.DMA((n_copies,)), pltpu.SemaphoreType.DMA((n_copies,))], start, finish)


def _pair_side(arrs, swaps):
    return _copies_side(arrs, [jax.ShapeDtypeStruct((4,) + a.shape[1:], a.dtype) for a in arrs], 4 * len(arrs),
                        functools.partial(_pair_copies, swaps=swaps))


def _chip_side(arrs):
    return _copies_side(arrs, [jax.ShapeDtypeStruct((3,) + a.shape[1:], a.dtype) for a in arrs], 3 * len(arrs), _chip_copies)


def _gather_side(arrs, swaps):
    return _Side(arrs, [jax.ShapeDtypeStruct((NDEV,) + a.shape, a.dtype) for a in arrs], _Gather.scratch(len(arrs)),
                 start=lambda ins, outs, *sems: _Gather(ins, outs, *sems, swaps).start(),
                 mid=lambda ins, outs, *sems: _Gather(ins, outs, *sems, swaps).forward(),
                 finish=lambda ins, outs, *sems: _Gather(ins, outs, *sems, swaps).finish())


def _row_call(body, side, *, name, steps, in_specs, out_specs, out_shape, ins):
    if side is None:
        res = pl.pallas_call(body, name=name, grid=(steps,), in_specs=in_specs, out_specs=out_specs, out_shape=out_shape,
                             compiler_params=_params("arbitrary"))(*ins)
        return list(res), []
    n_in, n_out, ne = len(in_specs), len(out_specs), len(side.arrs)

    def wrapped(*refs):
        e_in = refs[n_in:n_in + ne]
        e_out = refs[n_in + ne + n_out:n_in + 2 * ne + n_out]
        sems = refs[n_in + 2 * ne + n_out:]
        i = pl.program_id(0)

        @pl.when(i == 0)
        def _():
            side.start(e_in, e_out, *sems)

        if side.mid is not None:
            @pl.when(i == steps // 2)
            def _():
                side.mid(e_in, e_out, *sems)

        body(*refs[:n_in], *refs[n_in + ne:n_in + ne + n_out])

        @pl.when(i == steps - 1)
        def _():
            side.finish(e_in, e_out, *sems)

    any_spec = pl.BlockSpec(memory_space=pl.ANY)
    res = pl.pallas_call(
        wrapped, name=name, grid=(steps,), in_specs=list(in_specs) + [any_spec] * ne,
        out_specs=list(out_specs) + [any_spec] * ne, out_shape=list(out_shape) + side.out_shapes,
        scratch_shapes=side.scratch,
        compiler_params=_params("arbitrary"))(*ins, *side.arrs)
    return list(res[:n_out]), list(res[n_out:])


def _pair_sum(grads, got, swap, core, name):
    _, r, c = got.shape
    tr = r if r <= 352 else r // 2

    def own_map(k, i, core_ref):
        return (_slot(swap, k // 2, k % 2, core_ref[0]), i, 0)

    def body(core_ref, a_ref, b_ref, o_ref):
        o_ref[...] = a_ref[...] + b_ref[...]

    spec = pl.BlockSpec((None, tr, c), lambda k, i, core_ref: (k, i, 0))
    return pl.pallas_call(
        body, name=name,
        grid_spec=pltpu.PrefetchScalarGridSpec(
            num_scalar_prefetch=1, grid=(4, r // tr),
            in_specs=[pl.BlockSpec((None, tr, c), own_map), spec], out_specs=spec),
        out_shape=jax.ShapeDtypeStruct(got.shape, got.dtype), compiler_params=_params("parallel", "parallel"),
    )(core, grads, got)


def _adamw_math(w, g, m, v):
    m = ADAM_B1 * m + (1.0 - ADAM_B1) * g
    v = ADAM_B2 * v + (1.0 - ADAM_B2) * (g * g)
    m_hat = m / (1.0 - ADAM_B1 ** ADAM_STEP)
    v_hat = v / (1.0 - ADAM_B2 ** ADAM_STEP)
    delta = -ADAM_LR * (m_hat / (jnp.sqrt(v_hat) + ADAM_EPS) + ADAM_WD * w)
    return delta, m, v


def _adamw_tile(r):
    for cand in (256, 352, 128):
        if r % cand == 0:
            return cand
    return r


def _adamw(w, m, v, g, name):
    r, c = w.shape
    tr = _adamw_tile(r)
    spec = pl.BlockSpec((tr, c), lambda i: (i, 0))

    def body(w_ref, m_ref, v_ref, g_ref, d_ref, nm_ref, nv_ref):
        d_ref[...], nm_ref[...], nv_ref[...] = _adamw_math(w_ref[...], g_ref[...], m_ref[...], v_ref[...])

    out = jax.ShapeDtypeStruct((r, c), F32)
    return pl.pallas_call(
        body, name=name, grid=(r // tr,), in_specs=[spec] * 4, out_specs=[spec] * 3, out_shape=[out] * 3,
        compiler_params=_params("parallel"),
    )(w, m, v, g)


def _adamw_reduce(w, m, v, sums, recv, chip, name):
    r, c = w.shape
    tr = _adamw_tile(r)
    spec = pl.BlockSpec((tr, c), lambda i, chip_ref: (i, 0))

    def body(chip_ref, w_ref, m_ref, v_ref, s_ref, p_ref, g_ref, d_ref, nm_ref, nv_ref):
        g = ((s_ref[...] + p_ref[0]) + p_ref[1]) + p_ref[2]
        g_ref[...] = g
        d_ref[...], nm_ref[...], nv_ref[...] = _adamw_math(w_ref[...], g, m_ref[...], v_ref[...])

    out = jax.ShapeDtypeStruct((r, c), F32)
    return pl.pallas_call(
        body, name=name,
        grid_spec=pltpu.PrefetchScalarGridSpec(
            num_scalar_prefetch=1, grid=(r // tr,),
            in_specs=[spec, spec, spec, pl.BlockSpec((None, tr, c), lambda i, chip_ref: (chip_ref[0], i, 0)),
                      pl.BlockSpec((3, tr, c), lambda i, chip_ref: (0, i, 0))],
            out_specs=[spec] * 4),
        out_shape=[out] * 4, compiler_params=_params("parallel"),
    )(chip, w, m, v, sums, recv)


def _vec(n):
    return pl.BlockSpec((1, n), lambda *_: (0, 0))


def _ln_mod(x, g, scale, shift, *, ts, name, side=None):
    s = x.shape[0]
    row = pl.BlockSpec((ts, D), lambda i: (i, 0))

    def body(x_ref, g_ref, sc_ref, sh_ref, h_ref):
        xv = x_ref[...]
        r = lax.rsqrt(jnp.mean(xv * xv, axis=-1, keepdims=True) + EPS)
        h = (xv * r) * g_ref[...]
        h_ref[...] = (h * (1.0 + sc_ref[...]) + sh_ref[...]).astype(BF16)

    (h,), extra = _row_call(body, side, name=name, steps=s // ts, in_specs=[row, _vec(D), _vec(D), _vec(D)],
                            out_specs=[row], out_shape=[jax.ShapeDtypeStruct((s, D), BF16)], ins=(x, g, scale, shift))
    return h, extra


def _proj_res_ln_mod(mix, w, x, gate, g, scale, shift, *, tm, name):
    s = x.shape[0]
    row = pl.BlockSpec((tm, D), lambda i: (i, 0))

    def body(m_ref, w_ref, x_ref, gt_ref, g_ref, sc_ref, sh_ref, a_ref, x1_ref, h_ref):
        att = _dot(m_ref[...], w_ref[...])
        a_ref[...] = att.astype(BF16)
        x1 = x_ref[...] + gt_ref[...] * att
        x1_ref[...] = x1
        r = lax.rsqrt(jnp.mean(x1 * x1, axis=-1, keepdims=True) + EPS)
        h = (x1 * r) * g_ref[...]
        h_ref[...] = (h * (1.0 + sc_ref[...]) + sh_ref[...]).astype(BF16)

    return pl.pallas_call(
        body, name=name, grid=(s // tm,), in_specs=[row, _full(w.shape), row] + [_vec(D)] * 4, out_specs=[row, row, row],
        out_shape=[jax.ShapeDtypeStruct((s, D), BF16), jax.ShapeDtypeStruct((s, D), F32), jax.ShapeDtypeStruct((s, D), BF16)],
        compiler_params=_params("parallel"),
    )(mix, w, x, gate, g, scale, shift)


def _proj_loss_head(act, w, x1, tgt, gate2, *, tm, name):
    s = x1.shape[0]
    n = s // tm
    row = pl.BlockSpec((tm, D), lambda i: (i, 0))
    acc8 = pl.BlockSpec((SUBLANES, D), lambda i: (0, 0))

    def body(a_ref, w_ref, x_ref, t_ref, g_ref, dy_ref, df_ref, dg_ref, loss_ref, lacc):
        i = pl.program_id(0)

        @pl.when(i == 0)
        def _():
            lacc[...] = jnp.zeros_like(lacc)
            dg_ref[...] = jnp.zeros_like(dg_ref)

        f = _dot(a_ref[...], w_ref[...])
        diff = x_ref[...] + g_ref[...] * f - t_ref[...]
        lacc[...] += _colsum8(diff * diff)
        dy = diff * (1.0 / D)
        dy_ref[...] = dy
        df_ref[...] = (dy * g_ref[...]).astype(BF16)
        dg_ref[...] += _colsum8(dy * f)

        @pl.when(i == n - 1)
        def _():
            loss_ref[...] = jnp.full((SUBLANES, LANES), (0.5 / D) * jnp.sum(lacc[...]), F32)

    return pl.pallas_call(
        body, name=name, grid=(n,),
        in_specs=[pl.BlockSpec((tm, act.shape[1]), lambda i: (i, 0)), _full(w.shape), row, row, _vec(D)],
        out_specs=[row, row, acc8, _full((SUBLANES, LANES))],
        out_shape=[jax.ShapeDtypeStruct((s, D), F32), jax.ShapeDtypeStruct((s, D), BF16),
                   jax.ShapeDtypeStruct((SUBLANES, D), F32), jax.ShapeDtypeStruct((SUBLANES, LANES), F32)],
        scratch_shapes=[pltpu.VMEM((SUBLANES, D), F32)], compiler_params=_params("arbitrary"),
    )(act, w, x1, tgt, gate2)


def _group_rsqrt(t, bd):
    return lax.rsqrt(_split_dot(t * t, bd) * (1.0 / HD) + EPS)


def _in_proj_qk_norm(h, w, qg, kg, bd, *, tm, name):
    s = h.shape[0]

    def body(h_ref, w_ref, qg_ref, kg_ref, bd_ref, p_ref, o_ref):
        bdv = bd_ref[...]
        hv = h_ref[...]
        p_ref[:, 0:DP] = _dot(hv, w_ref[0:DP, :], NT)
        q = _dot(hv, w_ref[DP:DP + DA, :], NT)
        p_ref[:, DP:DP + DA] = q
        o_ref[:, 0:DA] = (q * _group_rsqrt(q, bdv) * qg_ref[...]).astype(BF16)
        k = _dot(hv, w_ref[DP + DA:DP + 2 * DA, :], NT)
        p_ref[:, DP + DA:DP + 2 * DA] = k
        o_ref[:, DA:2 * DA] = (k * _group_rsqrt(k, bdv) * kg_ref[...]).astype(BF16)
        v = _dot(hv, w_ref[DP + 2 * DA:, :], NT)
        p_ref[:, DP + 2 * DA:] = v
        o_ref[:, 2 * DA:] = v.astype(BF16)

    return pl.pallas_call(
        body, name=name, grid=(s // tm,),
        in_specs=[pl.BlockSpec((tm, D), lambda i: (i, 0)), _full(w.shape), _vec(DA), _vec(DA), _full((DA, DA))],
        out_specs=[pl.BlockSpec((tm, DIN), lambda i: (i, 0)), pl.BlockSpec((tm, 3 * DA), lambda i: (i, 0))],
        out_shape=[jax.ShapeDtypeStruct((s, DIN), F32), jax.ShapeDtypeStruct((s, 3 * DA), BF16)],
        compiler_params=_params("parallel"),
    )(h, w, qg, kg, bd)


EXP_UNDERFLOW = -120.0


def _log_terms(z):
    neg_abs = lax.bitcast_convert_type(lax.bitcast_convert_type(z, jnp.uint32) | jnp.uint32(0x80000000), F32)
    b = jnp.minimum(z, 0.0) - jnp.log(1.0 + jnp.exp(neg_abs))
    return b, b - z


def _head_masks(rows):
    lane = lax.broadcasted_iota(jnp.int32, (rows, LANES), 1)
    return [lane < HD, lane >= HD]


def _attn_fwd(qkv, gather, swaps, *, tq, tk, hp, name):
    s = qkv.shape[0]
    nrep = tk // LANES
    ndiag = tq // tk
    ng = len(gather)
    nh, wl = 2 * hp, LANES * hp
    ngrp, nq = DA // wl, s // tq
    lanes = [slice(LANES * pp, LANES * (pp + 1)) for pp in range(hp)]

    def body(*refs):
        q_ref, k_ref, v_ref = refs[:3]
        g_in = refs[3:3 + ng]
        o_ref, tot_ref, first_ref = refs[3 + ng:6 + ng]
        g_out = refs[6 + ng:6 + 2 * ng]
        oacc, rc = refs[6 + 2 * ng:8 + 2 * ng]
        g_sems = refs[8 + 2 * ng:]
        i = pl.program_id(1)
        step_id = pl.program_id(0) * nq + i

        @pl.when(step_id == 0)
        def _():
            _Gather(g_in, g_out, *g_sems, swaps).start()

        @pl.when(step_id == (ngrp * nq * 3) // 4)
        def _():
            _Gather(g_in, g_out, *g_sems, swaps).forward()

        heads = _head_masks(tq)
        qs = [jnp.where(heads[a % 2], q_ref[:, lanes[a // 2]] * 0.125, 0.0).astype(BF16) for a in range(nh)]
        dif = lax.broadcasted_iota(jnp.int32, (tq, tk), 0) - lax.broadcasted_iota(jnp.int32, (tq, tk), 1)
        kr = lax.broadcasted_iota(jnp.int32, (tk, tk), 0)
        kc = lax.broadcasted_iota(jnp.int32, (tk, tk), 1)
        later =jnp.where(kr > kc, 1.0, 0.0).astype(BF16)
        oacc[...] = jnp.zeros_like(oacc)
        rc[...] = jnp.zeros_like(rc)

        def tile(kb, thr):
            rows = pl.ds(pl.multiple_of(kb * tk, tk), tk)
            ks = [k_ref[rows, ln] for ln in lanes]
            vs = [v_ref[rows, ln] for ln in lanes]
            qr = slice(0 if thr is None else thr, tq)
            rcv = [rc[a, qr, :] for a in range(nh)]
            zs = [_dot(qs[a][qr], ks[a // 2], NT) for a in range(nh)]
            bs, mbs = [], []
            for a in range(nh):
                b, m = _log_terms(zs[a])
                if thr is not None:
                    m = jnp.where(dif[qr] > thr, m, 0.0)
                bs.append(b)
                mbs.append(m.astype(BF16))
            rl = [_dot(mbs[a], later) for a in range(nh)]
            for a in range(nh):
                p = jnp.exp(bs[a] + (rl[a] + jnp.tile(rcv[a], (1, nrep))))
                if thr is not None:
                    p = jnp.where(dif[qr] > thr, p, 0.0)
                oacc[a, qr, :] += _dot(p.astype(BF16), vs[a // 2])
                rc[a, qr, :] = rcv[a] + (rl[a][:, 0:1] + mbs[a][:, 0:1].astype(F32))

        for d in reversed(range(ndiag)):
            tile(i * ndiag + d, d * tk)

        def live():
            top = rc[0]
            for a in range(1, nh):
                top = jnp.maximum(top, rc[a])
            return jnp.max(top) > EXP_UNDERFLOW

        def step(carry):
            kb, _ = carry
            tile(kb, None)
            return kb - 1, live()

        kb_end, _ = lax.while_loop(lambda cr: jnp.logical_and(cr[0] >= 0, cr[1]), step, (i * ndiag - 1, live()))
        first_ref[pl.program_id(0), i] = (kb_end + 1).astype(F32)
        for pp, ln in enumerate(lanes):
            o_ref[:, ln] = jnp.where(heads[0], oacc[2 * pp], oacc[2 * pp + 1])
            tot_ref[:, ln] = jnp.where(heads[0], rc[2 * pp], rc[2 * pp + 1])

        @pl.when(step_id == ngrp * nq - 1)
        def _():
            _Gather(g_in, g_out, *g_sems, swaps).finish()

    qspec = pl.BlockSpec((tq, wl), lambda p, i: (i, p))
    any_spec = pl.BlockSpec(memory_space=pl.ANY)
    res = pl.pallas_call(
        body, name=name, grid=(ngrp, nq),
        in_specs=[qspec,
                  pl.BlockSpec((s, wl), lambda p, i: (0, ngrp + p)),
                  pl.BlockSpec((s, wl), lambda p, i: (0, 2 * ngrp + p))] + [any_spec] * ng,
        out_specs=[qspec, qspec, pl.BlockSpec(memory_space=pltpu.SMEM)] + [any_spec] * ng,
        out_shape=[jax.ShapeDtypeStruct((s, DA), F32), jax.ShapeDtypeStruct((s, DA), F32),
                   jax.ShapeDtypeStruct((ngrp, nq), F32)]
        + [jax.ShapeDtypeStruct((NDEV,) + a.shape, a.dtype) for a in gather],
        scratch_shapes=[pltpu.VMEM((nh, tq, LANES), F32), pltpu.VMEM((nh, tq, LANES), F32)] + _Gather.scratch(ng),
        compiler_params=_params("arbitrary", "arbitrary"),
    )(qkv, qkv, qkv, *gather)
    return res[0], res[1], res[2], res[3:]


def _attn_bwd(qkv, do, tot, first, exchange, *, tq, tk, hp, name):
    s = qkv.shape[0]
    nrep = tk // LANES
    ndiag = tq // tk
    ne = len(exchange)
    nh, wl = 2 * hp, LANES * hp
    ngrp, nq = DA // wl, s // tq
    lanes = [slice(LANES * pp, LANES * (pp + 1)) for pp in range(hp)]

    def body(*refs):
        q_ref, k_ref, v_ref, do_ref, tot_ref, first_ref = refs[:6]
        e_in = refs[6:6 + ne]
        dq_ref, dk_ref, dv_ref = refs[6 + ne:9 + ne]
        e_out = refs[9 + ne:9 + 2 * ne]
        dqacc, rem, gc = refs[9 + 2 * ne:12 + 2 * ne]
        e_sems = refs[12 + 2 * ne:]
        i = pl.program_id(1)
        step_id = pl.program_id(0) * nq + i

        @pl.when(step_id == 0)
        def _():
            for cp in _chip_copies(e_in, e_out, *e_sems):
                cp.start()

        @pl.when(i == 0)
        def _():
            dk_ref[...] = jnp.zeros_like(dk_ref)
            dv_ref[...] = jnp.zeros_like(dv_ref)

        heads = _head_masks(tq)
        qs = [jnp.where(heads[a % 2], q_ref[:, lanes[a // 2]] * 0.125, 0.0).astype(BF16) for a in range(nh)]
        dob = [jnp.where(heads[a % 2], do_ref[:, lanes[a // 2]], 0.0).astype(BF16) for a in range(nh)]
        dif = lax.broadcasted_iota(jnp.int32, (tq, tk), 0) - lax.broadcasted_iota(jnp.int32, (tq, tk), 1)
        kr = lax.broadcasted_iota(jnp.int32, (tk, tk), 0)
        kc = lax.broadcasted_iota(jnp.int32, (tk, tk), 1)
        up_incl = jnp.where(kr <= kc, 1.0, 0.0).astype(BF16)
        up_strict = jnp.where(kr < kc, 1.0, 0.0).astype(BF16)
        dqacc[...] = jnp.zeros_like(dqacc)
        gc[...] = jnp.zeros_like(gc)
        for pp, ln in enumerate(lanes):
            totv = tot_ref[:, ln]
            swapped = pltpu.roll(totv, HD, axis=1)
            rem[2 * pp] = jnp.where(heads[0], totv, swapped)
            rem[2 * pp + 1] = jnp.where(heads[1], totv, swapped)

        def tile(kb, thr):
            rows = pl.ds(pl.multiple_of(kb * tk, tk), tk)
            ks = [k_ref[rows, ln] for ln in lanes]
            vs = [v_ref[rows, ln] for ln in lanes]
            qr = slice(0 if thr is None else thr, tq)
            remv = [rem[a, qr, :] for a in range(nh)]
            gcv = [gc[a, qr, :] for a in range(nh)]
            zs = [_dot(qs[a][qr], ks[a // 2], NT) for a in range(nh)]
            das = [_dot(dob[a][qr], vs[a // 2], NT) for a in range(nh)]
            bs, mbs = [], []
            for a in range(nh):
                b, m = _log_terms(zs[a])
                if thr is not None:
                    m = jnp.where(dif[qr] > thr, m, 0.0)
                bs.append(b)
                mbs.append(m.astype(BF16))
            pl_ = [_dot(mbs[a], up_incl) for a in range(nh)]
            ps, gs, gbs = [], [], []
            for a in range(nh):
                p = jnp.exp(bs[a] + (jnp.tile(remv[a], (1, nrep)) - pl_[a]))
                if thr is not None:
                    p = jnp.where(dif[qr] > thr, p, 0.0)
                g = p * das[a]
                ps.append(p.astype(BF16))
                gs.append(g)
                gbs.append(g.astype(BF16))
            cl = [_dot(gbs[a], up_strict) for a in range(nh)]
            dk_add = [jnp.zeros((tk, LANES), F32) for _ in range(hp)]
            dv_add = [jnp.zeros((tk, LANES), F32) for _ in range(hp)]
            for a in range(nh):
                dz = gs[a] - jnp.exp(bs[a]) * (gs[a] + (jnp.tile(gcv[a], (1, nrep)) + cl[a]))
                if thr is not None:
                    dz = jnp.where(dif[qr] > thr, dz, 0.0)
                dzb = dz.astype(BF16)
                dqacc[a, qr, :] += _dot(dzb, ks[a // 2])
                dk_add[a // 2] += _dot(dzb, qs[a][qr], TN)
                dv_add[a // 2] += _dot(ps[a], dob[a][qr], TN)
                rem[a, qr, :] = remv[a] - pl_[a][:, tk - 1:tk]
                gc[a, qr, :] = gcv[a] + (cl[a][:, tk - 1:tk] + gbs[a][:, tk - 1:tk].astype(F32))
            for pp, ln in enumerate(lanes):
                dk_ref[rows, ln] += dk_add[pp]
                dv_ref[rows, ln] += dv_add[pp]

        def step(kb, carry):
            tile(kb, None)
            return carry

        first_kb = first_ref[(pl.program_id(0) * first.shape[0]) // ngrp, i].astype(jnp.int32)
        lax.fori_loop(first_kb, i * ndiag, step, 0)
        for d in range(ndiag):
            tile(i * ndiag + d, d * tk)
        for pp, ln in enumerate(lanes):
            dq_ref[:, ln] = jnp.where(heads[0], dqacc[2 * pp], dqacc[2 * pp + 1]) * 0.125

        @pl.when(step_id == ngrp * nq - 1)
        def _():
            cps = _chip_copies(e_in, e_out, *e_sems)
            for cp in cps:
                cp.wait_recv()
            for cp in cps:
                cp.wait_send()

    qspec = pl.BlockSpec((tq, wl), lambda p, i: (i, p))
    full = pl.BlockSpec((s, wl), lambda p, i: (0, p))
    any_spec = pl.BlockSpec(memory_space=pl.ANY)
    out = jax.ShapeDtypeStruct((s, DA), F32)
    res = pl.pallas_call(
        body, name=name, grid=(ngrp, nq),
        in_specs=[qspec, pl.BlockSpec((s, wl), lambda p, i: (0, ngrp + p), pipeline_mode=pl.Buffered(1)),
                  pl.BlockSpec((s, wl), lambda p, i: (0, 2 * ngrp + p), pipeline_mode=pl.Buffered(1)), qspec, qspec,
                  pl.BlockSpec(memory_space=pltpu.SMEM)] + [any_spec] * ne,
        out_specs=[qspec, full, full] + [any_spec] * ne,
        out_shape=[out, out, out] + [jax.ShapeDtypeStruct((3,) + a.shape[1:], a.dtype) for a in exchange],
        scratch_shapes=[pltpu.VMEM((nh, tq, LANES), F32)] * 3
        + [pltpu.SemaphoreType.DMA((3 * ne,)), pltpu.SemaphoreType.DMA((3 * ne,))],
        compiler_params=_params("arbitrary", "arbitrary"),
    )(qkv, qkv, qkv, do, tot, first, *exchange)
    return res[0], res[1], res[2], res[3:]


def _shift_rows(v, k):
    return pltpu.roll(v, k % v.shape[0], axis=0)


def _pooled(u, uh, i, g, w, ts):
    halo = jnp.where(i > 0, uh, 0.0)
    ue = jnp.concatenate([halo, u], axis=0)
    acc, span = ue, 1
    while span < w:
        acc = acc + _shift_rows(acc, span)
        span *= 2
    tpos = i * ts + lax.broadcasted_iota(jnp.int32, (ts, 1), 0)
    cnt = jnp.minimum(tpos + 1, w).astype(F32)
    return acc[HALO:] / cnt - u


def _pool_mix(proj, o, pw, pb, ps, ag, bd, *, ts, name):
    s = proj.shape[0]
    hb = ts // HALO

    def body(u_ref, uh_ref, o_ref, pw_ref, pb_ref, ps_ref, ag_ref, bd_ref, mix_ref):
        i = pl.program_id(0)
        for g, w in enumerate(POOL_WINDOWS):
            cols = slice(g * LANES, (g + 1) * LANES)
            pooled = _pooled(u_ref[:, cols], uh_ref[:, cols], i, g, w, ts)
            yv = (_dot(pooled.astype(BF16), pw_ref[g]) + pb_ref[:, cols]) * ps_ref[:, cols]
            mix_ref[:, cols] = yv.astype(BF16)
        ov = o_ref[...]
        mix_ref[:, DP:] = (ov * _group_rsqrt(ov, bd_ref[...]) * ag_ref[...]).astype(BF16)

    return pl.pallas_call(
        body, name=name, grid=(s // ts,),
        in_specs=[pl.BlockSpec((ts, DP), lambda i: (i, 0)),
                  pl.BlockSpec((HALO, DP), lambda i: (jnp.maximum(i * hb - 1, 0), 0)),
                  pl.BlockSpec((ts, DA), lambda i: (i, 0)),
                  _full((4, LANES, LANES)), _vec(DP), _vec(DP), _vec(DA), _full((DA, DA))],
        out_specs=pl.BlockSpec((ts, D), lambda i: (i, 0)),
        out_shape=jax.ShapeDtypeStruct((s, D), BF16), compiler_params=_params("parallel"),
    )(proj, proj, o, pw, pb, ps, ag, bd)


CF = DFF // 2
MXU_COLS = 256


def _sigmoid(t):
    return 0.5 + 0.5 * jnp.tanh(0.5 * t)


def _sub_chunks(width):
    return [(c0, min(MXU_COLS, width - c0)) for c0 in range(0, width, MXU_COLS)]


def _up_conv_gate(h2, w_up, cw, cb, *, tm, name):
    s = h2.shape[0]
    hb = tm // HALO

    def body(a_ref, ah_ref, w_ref, cw_ref, cb_ref, up_ref, c_ref, act_ref):
        i = pl.program_id(1)
        halo = jnp.where(i > 0, ah_ref[...], jnp.zeros_like(ah_ref))
        ext = jnp.concatenate([halo, a_ref[...]], axis=0)
        for c0, cwid in _sub_chunks(CF):
            conv = []
            for off in (c0, CF + c0):
                cols = slice(off, off + cwid)
                ue = _dot(ext, w_ref[cols, :], NT)
                up_ref[:, cols] = ue[HALO:].astype(BF16)
                y = cw_ref[2:3, cols] * ue + cw_ref[1:2, cols] * _shift_rows(ue, 1) + cw_ref[0:1, cols] * _shift_rows(ue, 2)
                cv = y[HALO:] + cb_ref[:, cols]
                c_ref[:, cols] = cv.astype(BF16)
                conv.append(cv)
            gt, vl = conv
            act_ref[:, c0:c0 + cwid] = (gt * _sigmoid(gt) * vl).astype(BF16)

    return pl.pallas_call(
        body, name=name, grid=(2, s // tm),
        in_specs=[pl.BlockSpec((tm, D), lambda j, i: (i, 0)),
                  pl.BlockSpec((HALO, D), lambda j, i: (jnp.maximum(i * hb - 1, 0), 0)),
                  pl.BlockSpec((2 * CF, D), lambda j, i: (j, 0)),
                  pl.BlockSpec((3, 2 * CF), lambda j, i: (0, j)), pl.BlockSpec((1, 2 * CF), lambda j, i: (0, j))],
        out_specs=[pl.BlockSpec((tm, 2 * CF), lambda j, i: (i, j)), pl.BlockSpec((tm, 2 * CF), lambda j, i: (i, j)),
                   pl.BlockSpec((tm, CF), lambda j, i: (i, j))],
        out_shape=[jax.ShapeDtypeStruct((s, 2 * DFF), BF16), jax.ShapeDtypeStruct((s, 2 * DFF), BF16),
                   jax.ShapeDtypeStruct((s, DFF), BF16)],
        compiler_params=_params("parallel", "parallel"),
    )(h2, h2, w_up, cw, cb)


def _down_bwd_gate(dffn, w_down, conv, *, tm, name):
    s = dffn.shape[0]

    def body(a_ref, w_ref, c_ref, d_ref, db_ref):
        i = pl.program_id(1)

        @pl.when(i == 0)
        def _():
            db_ref[...] = jnp.zeros_like(db_ref)

        a = a_ref[...]
        for c0, cwid in _sub_chunks(CF):
            gcols, vcols = slice(c0, c0 + cwid), slice(CF + c0, CF + c0 + cwid)
            da = _dot(a, w_ref[gcols, :], NT)
            gt, vl = c_ref[:, gcols].astype(F32), c_ref[:, vcols].astype(F32)
            sg = _sigmoid(gt)
            dgt = da * vl * (sg * (1.0 + gt * (1.0 - sg)))
            dvl = da * (gt * sg)
            d_ref[:, gcols] = dgt.astype(BF16)
            d_ref[:, vcols] = dvl.astype(BF16)
            db_ref[:, gcols] += _colsum8(dgt)
            db_ref[:, vcols] += _colsum8(dvl)

    return pl.pallas_call(
        body, name=name, grid=(2, s // tm),
        in_specs=[pl.BlockSpec((tm, D), lambda j, i: (i, 0)), pl.BlockSpec((CF, D), lambda j, i: (j, 0)),
                  pl.BlockSpec((tm, 2 * CF), lambda j, i: (i, j))],
        out_specs=[pl.BlockSpec((tm, 2 * CF), lambda j, i: (i, j)), pl.BlockSpec((SUBLANES, 2 * CF), lambda j, i: (0, j))],
        out_shape=[jax.ShapeDtypeStruct((s, 2 * DFF), BF16), jax.ShapeDtypeStruct((SUBLANES, 2 * DFF), F32)],
        compiler_params=_params("parallel", "arbitrary"),
    )(dffn, w_down, conv)


def _conv_bwd_up_bwd(dc, up, cw, w_up, *, tm, name):
    s = up.shape[0]
    hb = tm // HALO
    nb = s // HALO
    nk = 2 * DFF // CF
    n = s // tm

    def body(d_ref, dn_ref, u_ref, cw_ref, w_ref, du_ref, dh_ref, dw_ref, acc, dwacc):
        i, k = pl.program_id(0), pl.program_id(1)

        @pl.when(jnp.logical_and(i == 0, k == 0))
        def _():
            dwacc[...] = jnp.zeros_like(dwacc)

        @pl.when(k == 0)
        def _():
            acc[...] = jnp.zeros_like(acc)

        live_next = i < n - 1
        part = None
        for c0, cwid in _sub_chunks(CF):
            cols = slice(c0, c0 + cwid)
            dcur = d_ref[:, cols].astype(F32)
            de = jnp.concatenate([dcur, jnp.where(live_next, dn_ref[:, cols].astype(F32), 0.0)], axis=0)
            d1 = _shift_rows(de, -1)[:tm]
            d2 = _shift_rows(de, -2)[:tm]
            du = (cw_ref[2:3, cols] * dcur + cw_ref[1:2, cols] * d1 + cw_ref[0:1, cols] * d2).astype(BF16)
            du_ref[:, cols] = du
            prod = _dot(du, w_ref[cols, :])
            part = prod if part is None else part + prod
            u = u_ref[:, cols].astype(F32)
            for tap, dsh in ((2, dcur), (1, d1), (0, d2)):
                dwacc[k, SUBLANES * tap:SUBLANES * (tap + 1), cols] += _colsum8(dsh * u)
        acc[...] += part

        @pl.when(k == nk - 1)
        def _():
            dh_ref[...] = acc[...].astype(dh_ref.dtype)

        @pl.when(jnp.logical_and(i == n - 1, k == nk - 1))
        def _():
            dw_ref[...] = dwacc[...]

    res = pl.pallas_call(
        body, name=name, grid=(n, nk),
        in_specs=[pl.BlockSpec((tm, CF), lambda i, k: (i, k)),
                  pl.BlockSpec((HALO, CF), lambda i, k: (jnp.minimum((i + 1) * hb, nb - 1), k)),
                  pl.BlockSpec((tm, CF), lambda i, k: (i, k)),
                  pl.BlockSpec((3, CF), lambda i, k: (0, k)),
                  pl.BlockSpec((CF, D), lambda i, k: (k, 0))],
        out_specs=[pl.BlockSpec((tm, CF), lambda i, k: (i, k)), pl.BlockSpec((tm, D), lambda i, k: (i, 0)),
                   _full((nk, 24, CF))],
        out_shape=[jax.ShapeDtypeStruct((s, 2 * DFF), BF16), jax.ShapeDtypeStruct((s, D), BF16),
                   jax.ShapeDtypeStruct((nk, 24, CF), F32)],
        scratch_shapes=[pltpu.VMEM((tm, D), F32), pltpu.VMEM((nk, 24, CF), F32)],
        compiler_params=_params("arbitrary", "arbitrary"),
    )(dc, dc, up, cw, w_up)
    return res[0], res[1], jnp.transpose(res[2], (1, 0, 2)).reshape(24, 2 * DFF)


def _ln_mod_bwd(dh, xin, g, scale, resid, extra, gate, *, ts, name, side=None):
    s = xin.shape[0]
    row = pl.BlockSpec((ts, D), lambda i: (i, 0))
    acc8 = pl.BlockSpec((SUBLANES, D), lambda i: (0, 0))
    with_gate = extra is not None

    def body(*refs):
        if with_gate:
            dh_ref, x_ref, g_ref, sc_ref, r_ref, e_ref, gt_ref, dx_ref, da_ref, dsh, dsc, dg, dgt = refs
        else:
            dh_ref, x_ref, g_ref, sc_ref, r_ref, dx_ref, dsh, dsc, dg = refs
        i = pl.program_id(0)

        @pl.when(i == 0)
        def _():
            for acc in (dsh, dsc, dg) + ((dgt,) if with_gate else ()):
                acc[...] = jnp.zeros_like(acc)

        xv, dhv = x_ref[...], dh_ref[...].astype(F32)
        r = lax.rsqrt(jnp.mean(xv * xv, axis=-1, keepdims=True) + EPS)
        xn = xv * r
        dsh[...] += _colsum8(dhv)
        dsc[...] += _colsum8(dhv * (xn * g_ref[...]))
        dhp = dhv * (1.0 + sc_ref[...])
        dg[...] += _colsum8(dhp * xn)
        dxn = dhp * g_ref[...]
        dx = r_ref[...] + r * (dxn - xn * jnp.mean(dxn * xn, axis=-1, keepdims=True))
        dx_ref[...] = dx
        if with_gate:
            da_ref[...] = (dx * gt_ref[...]).astype(BF16)
            dgt[...] += _colsum8(dx * e_ref[...].astype(F32))

    f32o, p8 = jax.ShapeDtypeStruct((s, D), F32), jax.ShapeDtypeStruct((SUBLANES, D), F32)
    if with_gate:
        ins, in_specs = (dh, xin, g, scale, resid, extra, gate), [row, row, _vec(D), _vec(D), row, row, _vec(D)]
        out_specs, out_shape = [row, row, acc8, acc8, acc8, acc8], [f32o, jax.ShapeDtypeStruct((s, D), BF16), p8, p8, p8, p8]
    else:
        ins, in_specs = (dh, xin, g, scale, resid), [row, row, _vec(D), _vec(D), row]
        out_specs, out_shape = [row, acc8, acc8, acc8], [f32o, p8, p8, p8]
    return _row_call(body, side, name=name, steps=s // ts, in_specs=in_specs, out_specs=out_specs,
                     out_shape=out_shape, ins=ins)


def _ln_mod_bwd_deep(dh, xin, g, scale, resid, *, ts, name):
    s = xin.shape[0]

    def body(dh_hbm, x_hbm, g_ref, sc_ref, r_hbm, dx_hbm, dsh, dsc, dg):
        for acc in (dsh, dsc, dg):
            acc[...] = jnp.zeros_like(acc)

        def inner(dh_ref, x_ref, r_ref, dx_ref):
            xv, dhv = x_ref[...], dh_ref[...].astype(F32)
            r = lax.rsqrt(jnp.mean(xv * xv, axis=-1, keepdims=True) + EPS)
            xn = xv * r
            dsh[...] += _colsum8(dhv)
            dsc[...] += _colsum8(dhv * (xn * g_ref[...]))
            dhp = dhv * (1.0 + sc_ref[...])
            dg[...] += _colsum8(dhp * xn)
            dxn = dhp * g_ref[...]
            dx_ref[...] = r_ref[...] + r * (dxn - xn * jnp.mean(dxn * xn, axis=-1, keepdims=True))

        deep = pl.BlockSpec((ts, D), lambda i: (i, 0), pipeline_mode=pl.Buffered(3))
        pltpu.emit_pipeline(inner, grid=(s // ts,), in_specs=[deep, deep, deep],
                            out_specs=[pl.BlockSpec((ts, D), lambda i: (i, 0))])(dh_hbm, x_hbm, r_hbm, dx_hbm)

    any_spec = pl.BlockSpec(memory_space=pl.ANY)
    vmem = pl.BlockSpec(memory_space=pltpu.VMEM)
    p8 = jax.ShapeDtypeStruct((SUBLANES, D), F32)
    return pl.pallas_call(
        body, name=name, in_specs=[any_spec, any_spec, vmem, vmem, any_spec], out_specs=[any_spec, vmem, vmem, vmem],
        out_shape=[jax.ShapeDtypeStruct((s, D), F32), p8, p8, p8],
        compiler_params=pltpu.CompilerParams(vmem_limit_bytes=VMEM_LIMIT),
    )(dh, xin, g, scale, resid)


def _group_norm_bwd(t, dn_out, gvec, bd):
    r = _group_rsqrt(t, bd)
    dg_terms = dn_out * t * r
    dn = dn_out * gvec
    dt = r * (dn - t * (r * r) * (_split_dot(dn * t, bd) * (1.0 / HD)))
    return dt, dg_terms


def _mix_bwd(dmix, proj, o, pw, pb, ps, ag, bd, *, ts, name, side=None):
    s = proj.shape[0]
    hb = ts // HALO
    nb = s // HALO

    def body(dm_ref, dmn_ref, u_ref, uh_ref, o_ref, pw_ref, pb_ref, ps_ref, ag_ref, bd_ref,
             du_ref, do_ref, dpw_ref, dpb_ref, dps_ref, dag_ref):
        i = pl.program_id(0)
        n = s // ts

        @pl.when(i == 0)
        def _():
            for acc in (dpw_ref, dpb_ref, dps_ref, dag_ref):
                acc[...] = jnp.zeros_like(acc)

        for g, w in enumerate(POOL_WINDOWS):
            cols = slice(g * LANES, (g + 1) * LANES)
            wg = pw_ref[g]
            psg = ps_ref[:, cols]
            pooled = _pooled(u_ref[:, cols], uh_ref[:, cols], i, g, w, ts).astype(BF16)
            dy = dm_ref[:, cols].astype(F32)
            dps_ref[:, cols] += _colsum8(dy * (_dot(pooled, wg) + pb_ref[:, cols]))
            dpre = dy * psg
            dpb_ref[:, cols] += _colsum8(dpre)
            dpreb = dpre.astype(BF16)
            dpw_ref[g * LANES:(g + 1) * LANES, :] += _dot(pooled, dpreb, TN)
            dpool = _dot(dpreb, wg, NT)
            dnext = _dot((dmn_ref[:, cols].astype(F32) * psg).astype(BF16), wg, NT)
            dpe = jnp.concatenate([dpool, jnp.where(i < n - 1, dnext, 0.0)], axis=0)
            tpos = i * ts + lax.broadcasted_iota(jnp.int32, (ts + HALO, 1), 0)
            acc = dpe / jnp.minimum(tpos + 1, w).astype(F32)
            span = 1
            while span < w:
                acc = acc + _shift_rows(acc, -span)
                span *= 2
            du_ref[:, cols] = acc[:ts] - dpool
        ov = o_ref[...]
        dov, dg_terms = _group_norm_bwd(ov, dm_ref[:, DP:].astype(F32), ag_ref[...], bd_ref[...])
        do_ref[...] = dov
        dag_ref[...] += _colsum8(dg_terms)

    p8 = jax.ShapeDtypeStruct((SUBLANES, DP), F32)
    acc8 = pl.BlockSpec((SUBLANES, DP), lambda i: (0, 0))
    half = pl.BlockSpec((ts, DP), lambda i: (i, 0))
    return _row_call(
        body, side, name=name, steps=s // ts,
        in_specs=[pl.BlockSpec((ts, D), lambda i: (i, 0)),
                  pl.BlockSpec((HALO, DP), lambda i: (jnp.minimum((i + 1) * hb, nb - 1), 0)),
                  half, pl.BlockSpec((HALO, DP), lambda i: (jnp.maximum(i * hb - 1, 0), 0)),
                  half, _full((4, LANES, LANES)), _vec(DP), _vec(DP), _vec(DA), _full((DA, DA))],
        out_specs=[half, half, _full((DP, LANES)), acc8, acc8, acc8],
        out_shape=[jax.ShapeDtypeStruct((s, DP), F32), jax.ShapeDtypeStruct((s, DA), F32),
                   jax.ShapeDtypeStruct((DP, LANES), F32), p8, p8, p8],
        ins=(dmix, dmix, proj, proj, o, pw, pb, ps, ag, bd))


def _qk_norm_bwd(du, dq, dk, dv, proj, qg, kg, bd, *, ts, name):
    s = proj.shape[0]

    def body(du_ref, dq_ref, dk_ref, dv_ref, q_ref, k_ref, qg_ref, kg_ref, bd_ref, dp_ref, dqg_ref, dkg_ref):
        i = pl.program_id(0)

        @pl.when(i == 0)
        def _():
            dqg_ref[...] = jnp.zeros_like(dqg_ref)
            dkg_ref[...] = jnp.zeros_like(dkg_ref)

        bdv = bd_ref[...]
        dqr, tq = _group_norm_bwd(q_ref[...], dq_ref[...], qg_ref[...], bdv)
        dkr, tk = _group_norm_bwd(k_ref[...], dk_ref[...], kg_ref[...], bdv)
        dqg_ref[...] += _colsum8(tq)
        dkg_ref[...] += _colsum8(tk)
        dp_ref[:, 0:DP] = du_ref[...].astype(BF16)
        dp_ref[:, DP:DP + DA] = dqr.astype(BF16)
        dp_ref[:, DP + DA:DP + 2 * DA] = dkr.astype(BF16)
        dp_ref[:, DP + 2 * DA:] = dv_ref[...].astype(BF16)

    half = pl.BlockSpec((ts, DA), lambda i: (i, 0))
    col = lambda j: pl.BlockSpec((ts, DA), lambda i: (i, j))
    acc8 = pl.BlockSpec((SUBLANES, DA), lambda i: (0, 0))
    p8 = jax.ShapeDtypeStruct((SUBLANES, DA), F32)
    return pl.pallas_call(
        body, name=name, grid=(s // ts,),
        in_specs=[half, half, half, half, col(1), col(2), _vec(DA), _vec(DA), _full((DA, DA))],
        out_specs=[pl.BlockSpec((ts, DIN), lambda i: (i, 0)), acc8, acc8],
        out_shape=[jax.ShapeDtypeStruct((s, DIN), BF16), p8, p8],
        compiler_params=_params("arbitrary"),
    )(du, dq, dk, dv, proj, proj, qg, kg, bd)


def _split3(a):
    hi = a.astype(BF16)
    return hi, (a - hi.astype(F32)).astype(BF16)


def _dot3(a, b, dn):
    ah, al = _split3(a)
    bh, bl = _split3(b)
    return _dot(ah, bh, dn) + (_dot(ah, bl, dn) + _dot(al, bh, dn))


def _ada_fwd(c_all, w, b, name):
    nw = w.shape[1]

    def body(c_ref, w_ref, b_ref, o_ref):
        cv = c_ref[...]
        act = cv / (1.0 + jnp.exp(-cv))
        o_ref[...] = _dot3(act, w_ref[...], NN) + b_ref[...]

    return pl.pallas_call(
        body, name=name, in_specs=[_full((NDEV, D)), _full(w.shape), _full((1, nw))], out_specs=_full((NDEV, nw)),
        out_shape=jax.ShapeDtypeStruct((NDEV, nw), F32), grid=(1,), compiler_params=_params("arbitrary"),
    )(c_all, w, b)


def _ada_bwd(c_all, dmod, name):
    nw = dmod.shape[1]

    def body(c_ref, d_ref, o_ref):
        cv = c_ref[...]
        act = cv / (1.0 + jnp.exp(-cv))
        o_ref[...] = _dot3(act, d_ref[...], TN)[None]

    return pl.pallas_call(
        body, name=name, in_specs=[_full((NDEV, D)), _full((NDEV, nw))], out_specs=_full((1, D, nw)),
        out_shape=jax.ShapeDtypeStruct((1, D, nw), F32), grid=(1,), compiler_params=_params("arbitrary"),
    )(c_all, dmod)


def _fold_heads(v):
    acc = v[:, 0:HD]
    for h in range(1, DA // HD):
        acc = acc + v[:, h * HD:(h + 1) * HD]
    return acc


def _pack_partials(pieces, dcw_p, name):
    n_p = len(pieces)
    total = sum(p.shape[1] for p in pieces) + 3 * dcw_p.shape[1]
    npack = -(-total // (SUBLANES * LANES)) * (SUBLANES * LANES)

    def body(*refs):
        out = refs[-1]
        off = 0
        for r in refs[:n_p]:
            out[:, off:off + r.shape[1]] = jnp.sum(r[...], axis=0, keepdims=True)
            off += r.shape[1]
        dw = refs[n_p]
        for tap in range(3):
            out[:, off:off + dw.shape[1]] = jnp.sum(dw[SUBLANES * tap:SUBLANES * (tap + 1), :], axis=0, keepdims=True)
            off += dw.shape[1]
        if off < npack:
            out[:, off:] = jnp.zeros((1, npack - off), F32)

    arrs = list(pieces) + [dcw_p]
    return pl.pallas_call(
        body, name=name, grid=(1,), in_specs=[_full(a.shape) for a in arrs], out_specs=_full((1, npack)),
        out_shape=jax.ShapeDtypeStruct((1, npack), F32), compiler_params=_params("arbitrary"),
    )(*arrs)


def _small_update(gathered, gathered_pw, gathered_cw, specs, params, loss_off, name):
    names = [sp[0] for sp in specs]
    flat = []
    for nme in names + ["pool_w", "conv_w"]:
        flat += list(params[nme])
    n_in = len(flat)

    def body(*refs):
        ga_ref, gp_ref, gc_ref = refs[0], refs[1], refs[2]
        prm = refs[3:3 + n_in]
        outs = refs[3 + n_in:]
        total = ga_ref[0:1, :]
        for dv in range(1, NDEV):
            total = total + ga_ref[dv:dv + 1, :]
        k = 0
        for idx, (nme, off, width, fold) in enumerate(specs):
            g = total[:, off:off + width]
            if fold:
                g = _fold_heads(g)
            w_ref, m_ref, v_ref = prm[3 * idx:3 * idx + 3]
            d, nm, nv = _adamw_math(w_ref[...], g, m_ref[...], v_ref[...])
            for val in (g, d, nm, nv):
                outs[k][...] = val
                k += 1
        gpw = gp_ref[0]
        for dv in range(1, NDEV):
            gpw = gpw + gp_ref[dv]
        w_ref, m_ref, v_ref = prm[3 * len(specs):3 * len(specs) + 3]
        d, nm, nv = _adamw_math(w_ref[...], gpw, m_ref[...], v_ref[...])
        for val in (gpw, d, nm, nv):
            outs[k][...] = val
            k += 1
        gcw = gc_ref[0]
        for dv in range(1, NDEV):
            gcw = gcw + gc_ref[dv]
        w_ref, m_ref, v_ref = prm[3 * len(specs) + 3:3 * len(specs) + 6]
        d, nm, nv = _adamw_math(w_ref[...], gcw, m_ref[...], v_ref[...])
        for val in (gcw, d, nm, nv):
            outs[k][...] = val
            k += 1
        outs[k][...] = ga_ref[:, 0:6 * D]
        outs[k + 1][...] = total[:, loss_off:loss_off + LANES] * (1.0 / SUBLANES)

    out_shape, out_specs = [], []
    for nme in names + ["pool_w", "conv_w"]:
        shp = params[nme][0].shape
        out_shape += [jax.ShapeDtypeStruct(shp, F32)] * 4
        out_specs += [_full(shp)] * 4
    out_shape += [jax.ShapeDtypeStruct((NDEV, 6 * D), F32), jax.ShapeDtypeStruct((1, LANES), F32)]
    out_specs += [_full((NDEV, 6 * D)), _full((1, LANES))]
    res = pl.pallas_call(
        body, name=name, grid=(1,),
        in_specs=[_full(gathered.shape), _full(gathered_pw.shape), _full(gathered_cw.shape)] + [_full(a.shape) for a in flat],
        out_specs=out_specs, out_shape=out_shape, compiler_params=_params("arbitrary"),
    )(gathered, gathered_pw, gathered_cw, *flat)
    out = {nme: tuple(res[4 * i:4 * i + 4]) for i, nme in enumerate(names + ["pool_w", "conv_w"])}
    return out, res[-2], res[-1][0, 0]


def _row_tile(s):
    return 512 if s % 512 == 0 else s


def kernel(x, c, ada_w, ada_b, norm1_g, w_in, pool_w, pool_b, pool_scale, q_norm_g, k_norm_g, attn_out_g, w_out, norm2_g, w_up, conv_w, conv_b, w_down, loss_target, m_ada_w, m_ada_b, m_norm1_g, m_w_in, m_pool_w, m_pool_b, m_pool_scale, m_q_norm_g, m_k_norm_g, m_attn_out_g, m_w_out, m_norm2_g, m_w_up, m_conv_w, m_conv_b, m_w_down, v_ada_w, v_ada_b, v_norm1_g, v_w_in, v_pool_w, v_pool_b, v_pool_scale, v_q_norm_g, v_k_norm_g, v_attn_out_g, v_w_out, v_norm2_g, v_w_up, v_conv_w, v_conv_b, v_w_down):
    ax, ay, ac = lax.axis_index("x"), lax.axis_index("y"), lax.axis_index("c")
    me = 4 * ax + 2 * ay + ac
    me_swapped = 4 * ay + 2 * ax + ac
    xs, tgt = x[0], loss_target[0]
    s = xs.shape[0]
    ts = _row_tile(s)
    tq_attn, tk_attn, hp_attn = 256, 256, 2
    tmm = 2 * ts
    bd = _block_diag_ones(DA, HD)

    w_in_t = w_in[0].T.astype(BF16)
    w_up_t = w_up[0].T.astype(BF16)
    c_all = _all_gather_small(jnp.broadcast_to(c, (SUBLANES, D)), "gather_c")[:, 0, :]
    n_ada = ada_w.shape[2]
    ada_b_mine = lax.dynamic_slice_in_dim(ada_b, me * n_ada, n_ada, axis=1)
    mod_part = _ada_fwd(c_all, ada_w[0], ada_b_mine, "ada_fwd")
    mod_all = _all_gather_small(mod_part, "gather_mod")
    mod = lax.dynamic_index_in_dim(mod_all, me, axis=1, keepdims=False).reshape(1, 6 * D)
    shift1, scale1, gate1, shift2, scale2, gate2 = [mod[:, k * D:(k + 1) * D] for k in range(6)]

    later_w = [w_out[0].astype(BF16), w_up_t, w_down[0].astype(BF16)]
    cb_full = jnp.transpose(conv_b.reshape(1, 2, 2, 2, 704), (0, 2, 1, 3, 4)).reshape(1, 2 * DFF)

    qg = jnp.tile(q_norm_g, (1, DA // HD))
    kg = jnp.tile(k_norm_g, (1, DA // HD))
    ag = attn_out_g.reshape(1, DA)
    pw = pool_w[0].astype(BF16)
    pb = pool_b.reshape(1, DP)
    h1, (gw_in, gcw) = _ln_mod(xs, norm1_g, scale1, shift1, ts=tmm, name="ln1",
                               side=_gather_side([w_in_t, jnp.pad(conv_w[0], ((0, 5), (0, 64)))], [False, True]))
    w_in_full = gw_in.reshape(DIN, D)
    cw_full = jnp.transpose(gcw[:, :3, :704], (1, 0, 2)).reshape(3, 2 * DFF)
    proj, qkv = _in_proj_qk_norm(h1, w_in_full, qg, kg, bd, tm=tmm, name="in_proj_qk_norm")
    o_raw, m_tot, kb_first, (gw_out, gw_up, gw_down) = _attn_fwd(
        qkv, later_w, [False, True, False], tq=tq_attn, tk=tk_attn, hp=hp_attn, name="attn_fwd")
    w_out_full = gw_out.reshape(D, D)
    w_up_full = gw_up.reshape(2 * DFF, D)
    w_down_full = gw_down.reshape(DFF, D)
    mix = _pool_mix(proj, o_raw, pw, pb, pool_scale, ag, bd, ts=tmm, name="pool_mix")
    att, x1, h2 = _proj_res_ln_mod(mix, w_out_full, xs, gate1, norm2_g, scale2, shift2, tm=tmm, name="out_proj_ln2")
    up, conv, act = _up_conv_gate(h2, w_up_full, cw_full, cb_full, tm=tmm, name="up_conv_gate")
    dy, dffn, dgate2_p, loss_p = _proj_loss_head(act, w_down_full, x1, tgt, gate2, tm=ts, name="down_proj_loss")

    g_w_down = _matmul(act, dffn, mode="tn", out_dtype=F32, tm=CF, tn=D, tk=2 * tmm, name="down_wgrad")
    dconv, dcb_p = _down_bwd_gate(dffn, w_down_full, conv, tm=tmm, name="down_bwd_gate")
    dup, dh2, dcw_p = _conv_bwd_up_bwd(dconv, up, cw_full, w_up_full, tm=ts, name="conv_bwd_up_bwd")
    g_w_up_t = _matmul(dup, h2, mode="tn", out_dtype=F32, tm=CF, tn=D, tk=2 * tmm, name="up_wgrad")
    (dx1, datt, dshift2_p, dscale2_p, dnorm2_p, dgate1_p), _ = _ln_mod_bwd(
        dh2, x1, norm2_g, scale2, dy, att, gate1, ts=tmm, name="ln2_bwd")

    dmix = _matmul(datt, w_out_full, mode="nt", out_dtype=BF16, tm=tmm,tn=D, tk=D, name="out_bwd")
    g_w_out = _matmul(mix, datt, mode="tn", out_dtype=F32, tm=D, tn=D, tk=2 * tmm, name="out_wgrad")
    core = jnp.reshape(ac, (1,)).astype(jnp.int32)
    chip = jnp.reshape(2 * ax + ay, (1,)).astype(jnp.int32)
    big_ffn = [g_w_up_t.reshape(NDEV, 2 * DFF // NDEV, D), g_w_down.reshape(NDEV, DFF // NDEV, D),
               g_w_out.reshape(NDEV, D // NDEV, D)]
    swaps_ffn = [True, False, False]
    (du, do_raw, g_pw_p, dpb_p, dps_p, dag_p), gots_ffn = _mix_bwd(
        dmix, proj, o_raw, pw, pb, pool_scale, ag, bd, ts=tmm, name="mix_bwd", side=_pair_side(big_ffn, swaps_ffn))
    sums_ffn = [_pair_sum(big_ffn[k], gots_ffn[k], swaps_ffn[k], core, "rs_pair_sum_ffn%d" % k) for k in range(3)]
    dqn, dkn, dvv, parts_ffn = _attn_bwd(qkv, do_raw, m_tot, kb_first, sums_ffn, tq=tq_attn, tk=tk_attn, hp=hp_attn, name="attn_bwd")
    dproj, dqg_p, dkg_p = _qk_norm_bwd(du, dqn, dkn, dvv, proj, qg, kg, bd, ts=tmm, name="qk_norm_bwd")
    g_w_in_t = _matmul(dproj, h1, mode="tn", out_dtype=F32, tm=DIN // 2, tn=D, tk=2 * tmm, name="in_wgrad")
    big = [g_w_in_t.reshape(NDEV, DIN // NDEV, D)]
    gots = _pair_exchange(big, [False], "rs_pair")
    sums = [_pair_sum(big[0], gots[0], False, core, "rs_pair_sum")]
    dh1, parts = _matmul(dproj, w_in_full, mode="nn", out_dtype=BF16, tm=tmm,tn=D, tk=DIN, name="in_bwd",
                         side=_chip_side(sums))
    grad_x, dshift1_p, dscale1_p, dnorm1_p = _ln_mod_bwd_deep(dh1, xs, norm1_g, scale1, dx1, ts=tmm, name="ln1_bwd")

    tr = lambda a: a[0].T
    r_in = _adamw_reduce(tr(w_in), tr(m_w_in), tr(v_w_in), sums[0], parts[0], chip, "adamw_w_in")
    r_out = _adamw_reduce(w_out[0], m_w_out[0], v_w_out[0], sums_ffn[2], parts_ffn[2], chip, "adamw_w_out")
    r_up = _adamw_reduce(tr(w_up), tr(m_w_up), tr(v_w_up), sums_ffn[0], parts_ffn[0], chip, "adamw_w_up")
    r_down = _adamw_reduce(w_down[0], m_w_down[0], v_w_down[0], sums_ffn[1], parts_ffn[1], chip, "adamw_w_down")
    r_in = [a.T[None] for a in r_in]
    r_up = [a.T[None] for a in r_up]
    r_out = [a[None] for a in r_out]
    r_down = [a[None] for a in r_down]

    dcb_nat = jnp.transpose(dcb_p.reshape(SUBLANES, 2, 2, 2, 704), (0, 2, 1, 3, 4)).reshape(SUBLANES, 2 * DFF)
    pieces = [dshift1_p, dscale1_p, dgate1_p, dshift2_p, dscale2_p, dgate2_p,
              dnorm1_p, dnorm2_p, dcb_nat, dpb_p, dps_p, dag_p, dqg_p, dkg_p, loss_p]
    n_vec = sum(p.shape[1] for p in pieces)
    packed = _pack_partials(pieces, dcw_p, "pack_partials")
    npack = packed.shape[1]
    gathered, gathered_pw = _all_gather([packed.reshape(SUBLANES, npack // SUBLANES), g_pw_p], [False, False], "gather_small")
    gathered = gathered.reshape(NDEV, npack)
    gathered_cw = lax.dynamic_index_in_dim(
        gathered[:, n_vec:n_vec + 6 * DFF].reshape(NDEV, 3, NDEV, 704), me_swapped, axis=2, keepdims=False)
    specs = [("ada_b", 0, 6 * D, False)]
    off = 6 * D
    for nme, width, fold in (("norm1_g", D, False), ("norm2_g", D, False), ("conv_b", 2 * DFF, False),
                             ("pool_b", DP, False), ("pool_scale", DP, False), ("attn_out_g", DA, False),
                             ("q_norm_g", DA, True), ("k_norm_g", DA, True)):
        specs.append((nme, off, width, fold))
        off += width
    small = {
        "ada_b": (ada_b, m_ada_b, v_ada_b),
        "norm1_g": (norm1_g, m_norm1_g, v_norm1_g), "norm2_g": (norm2_g, m_norm2_g, v_norm2_g),
        "conv_b": (conv_b, m_conv_b, v_conv_b),
        "pool_b": (pb, m_pool_b.reshape(1, DP), v_pool_b.reshape(1, DP)),
        "pool_scale": (pool_scale, m_pool_scale, v_pool_scale),
        "attn_out_g": (ag, m_attn_out_g.reshape(1, DA), v_attn_out_g.reshape(1, DA)),
        "q_norm_g": (q_norm_g, m_q_norm_g, v_q_norm_g), "k_norm_g": (k_norm_g, m_k_norm_g, v_k_norm_g),
        "pool_w": (pool_w.reshape(DP, LANES), m_pool_w.reshape(DP, LANES), v_pool_w.reshape(DP, LANES)),
        "conv_w": (conv_w[0], m_conv_w[0], v_conv_w[0]),
    }
    upd, dmod_all, loss = _small_update(gathered, gathered_pw, gathered_cw, specs, small, off, "small_update")
    g_ada_w = _ada_bwd(c_all, lax.dynamic_slice_in_dim(dmod_all, me * n_ada, n_ada, axis=1), "ada_bwd")
    r_ada = [g_ada_w] + [a[None] for a in _adamw(ada_w[0], m_ada_w[0], v_ada_w[0], g_ada_w[0], "adamw_ada_w")]

    shapes = {"ada_b": ada_b.shape, "norm1_g": norm1_g.shape, "pool_w": pool_w.shape, "pool_b": pool_b.shape,
              "pool_scale": pool_scale.shape, "q_norm_g": q_norm_g.shape, "k_norm_g": k_norm_g.shape,
              "attn_out_g": attn_out_g.shape, "norm2_g": norm2_g.shape, "conv_w": conv_w.shape, "conv_b": conv_b.shape}
    res = {nme: [a.reshape(shapes[nme]) for a in upd[nme]] for nme in shapes}
    res.update(ada_w=r_ada, w_in=r_in, w_out=r_out, w_up=r_up, w_down=r_down)
    names = ["ada_w", "ada_b", "norm1_g", "w_in", "pool_w", "pool_b", "pool_scale", "q_norm_g", "k_norm_g",
             "attn_out_g", "w_out", "norm2_g", "w_up", "conv_w", "conv_b", "w_down"]
    outs = [loss, grad_x[None]]
    for q in range(4):
        outs += [res[nme][q] for nme in names]
    return tuple(outs)
```
